```python
import math
import jax, jax.numpy as jnp
from jax import lax
import numpy as np

D_MODEL = 2048
BATCH = 8
SEQ = 8192
DEPTH = 1

N_MEM = 256
MEM_HEADS = 4
MEM_HD = 256
BRANCH_WIDTH = 1024
MEM_WIDTH = MEM_HEADS * MEM_HD
HG_HEADS = 8
HG_DK = 128
HG_DV = 128
HG_WIDTH = HG_HEADS * HG_DK
HG_CHUNK = 64
MLA_HEADS = 8
MLA_Q_RANK = 512
MLA_KV_RANK = 512
MLA_NOPE = 128
MLA_ROPE = 64
MLA_V = 128
MLA_QK = MLA_NOPE + MLA_ROPE
MLA_WIDTH = MLA_HEADS * MLA_V
ROPE_THETA = 10000.0
Q_BLOCK = 128
N_BRANCH = 3
D_FF = -(-8 * D_MODEL // (3 * 256)) * 256
ALPHA = (2.0 * DEPTH) ** 0.25
BETA = (8.0 * DEPTH) ** -0.25
LN_EPS = 1e-5
RMS_EPS = 1e-6
IN_SPLITS = (HG_WIDTH, HG_WIDTH, HG_WIDTH, HG_WIDTH, HG_WIDTH,
             MLA_Q_RANK, MLA_KV_RANK, MLA_ROPE, MEM_WIDTH, N_BRANCH * D_MODEL)
IN_WIDTH = sum(IN_SPLITS)

kernel_name = "hybrid_hgrn2_mla_memory_deepnorm_encoder"


def _split(a, sizes):
    idx, acc = [], 0
    for s in sizes[:-1]:
        acc += s
        idx.append(acc)
    return jnp.split(a, idx, axis=-1)


def _layernorm(x, g, b):
    xf = x.astype(jnp.float32)
    mu = jnp.mean(xf, -1, keepdims=True)
    xc = xf - mu
    var = jnp.mean(xc * xc, -1, keepdims=True)
    return (xc * lax.rsqrt(var + LN_EPS) * g.astype(jnp.float32) + b.astype(jnp.float32)).astype(x.dtype)


def _rmsnorm(x, g):
    xf = x.astype(jnp.float32)
    ms = jnp.mean(xf * xf, -1, keepdims=True)
    return (xf * lax.rsqrt(ms + RMS_EPS) * g.astype(jnp.float32)).astype(x.dtype)


def _rope(x, cos, sin):
    xf = x.astype(jnp.float32)
    x1, x2 = jnp.split(xf, 2, axis=-1)
    return jnp.concatenate([x1 * cos - x2 * sin, x2 * cos + x1 * sin], -1).astype(x.dtype)


def _chunk_gla(q, k, v, log_f):
    Z, B, H, T, dk = q.shape
    dv = v.shape[-1]
    n = T // HG_CHUNK

    def to_chunks(a):
        a = a.reshape(Z, B, H, n, HG_CHUNK, a.shape[-1])
        return jnp.moveaxis(a, 3, 0)

    tri = jnp.tril(jnp.ones((HG_CHUNK, HG_CHUNK), dtype=bool))[:, :, None]

    def step(S, inp):
        qc, kc, vc, gc = inp
        b = jnp.cumsum(gc, axis=-2)
        inter = jnp.einsum('zbhtk,zbhkv->zbhtv', qc * jnp.exp(b), S)
        diff = b[..., :, None, :] - b[..., None, :, :]
        decay = jnp.exp(jnp.where(tri, diff, -jnp.inf))
        att = jnp.einsum('zbhtk,zbhsk,zbhtsk->zbhts', qc, kc, decay)
        intra = jnp.einsum('zbhts,zbhsv->zbhtv', att, vc)
        b_last = b[..., -1:, :]
        S = jnp.swapaxes(jnp.exp(b_last), -1, -2) * S + jnp.einsum(
            'zbhsk,zbhsv->zbhkv', kc * jnp.exp(b_last - b), vc)
        return S, inter + intra

    S0 = jnp.zeros((Z, B, H, dk, dv), jnp.float32)
    _, o = lax.scan(step, S0, (to_chunks(q), to_chunks(k), to_chunks(v), to_chunks(log_f)))
    o = jnp.moveaxis(o, 0, 3)
    return o.reshape(Z, B, H, T, dv)


def _hgrn2_branch(q_raw, i_raw, f_fw_raw, f_bw_raw, g_raw, lb, norm_g):
    B, S, _ = q_raw.shape
    f32 = jnp.float32

    def heads(a):
        a = a.reshape(a.shape[:-1] + (HG_HEADS, -1))
        return jnp.swapaxes(a, -2, -3)

    q = heads(jax.nn.silu(q_raw.astype(f32)))
    v = heads(i_raw.astype(f32))
    f_raw = jnp.stack([f_fw_raw, f_bw_raw]).astype(f32)
    lbz = lb[:, None, None, :]
    f = lbz + (1.0 - lbz) * jax.nn.sigmoid(f_raw)
    k = heads(1.0 - f)
    log_f = heads(jnp.log(f))
    flip = lambda a: jnp.flip(a, axis=-2)
    qz = jnp.stack([q, flip(q)])
    vz = jnp.stack([v, flip(v)])
    kz = jnp.stack([k[0], flip(k[1])])
    gz = jnp.stack([log_f[0], flip(log_f[1])])
    o = _chunk_gla(qz, kz, vz, gz)
    o = o[0] + flip(o[1])
    o = _rmsnorm(jnp.swapaxes(o, 1, 2), norm_g)
    o = o.reshape(B, S, HG_WIDTH) * jax.nn.sigmoid(g_raw.astype(f32))
    return o.astype(q_raw.dtype)


def _mla_branch(cq_raw, ckv_raw, krope_raw, cos, sin, g_cq, g_ckv, w_uq, w_ukv):
    B, S, _ = cq_raw.shape
    q = (_rmsnorm(cq_raw, g_cq) @ w_uq).reshape(B, S, MLA_HEADS, MLA_QK)
    q = jnp.concatenate([q[..., :MLA_NOPE],
                         _rope(q[..., MLA_NOPE:], cos[:, :, None, :], sin[:, :, None, :])], -1)
    kv = (_rmsnorm(ckv_raw, g_ckv) @ w_ukv).reshape(B, S, MLA_HEADS, MLA_NOPE + MLA_V)
    k_rope = _rope(krope_raw, cos, sin)
    k = jnp.concatenate([kv[..., :MLA_NOPE],
                         jnp.broadcast_to(k_rope[:, :, None, :], (B, S, MLA_HEADS, MLA_ROPE))], -1)
    v = kv[..., MLA_NOPE:]
    scale = MLA_QK ** -0.5
    qb = jnp.moveaxis(q.reshape(B, S // Q_BLOCK, Q_BLOCK, MLA_HEADS, MLA_QK), 1, 0)

    def attend(qblk):
        s = jnp.einsum('bqhd,bkhd->bhqk', qblk, k).astype(jnp.float32) * scale
        p = jax.nn.softmax(s, axis=-1).astype(v.dtype)
        return jnp.einsum('bhqk,bkhd->bqhd', p, v)

    o = lax.map(attend, qb)
    return jnp.moveaxis(o, 0, 1).reshape(B, S, MLA_WIDTH)


def _memory_branch(q_raw, mem, w_kv):
    B, S, _ = q_raw.shape
    M = mem.shape[1]
    q = q_raw.reshape(B, S, MEM_HEADS, MEM_HD)
    kv = (mem @ w_kv).reshape(B, M, 2, MEM_HEADS, MEM_HD)
    k, v = kv[:, :, 0], kv[:, :, 1]
    s = jnp.einsum('bqhd,bmhd->bhqm', q, k).astype(jnp.float32) * (MEM_HD ** -0.5)
    p = jax.nn.softmax(s, axis=-1).astype(v.dtype)
    return jnp.einsum('bhqm,bmhd->bqhd', p, v).reshape(B, S, MEM_WIDTH)


def _fwd_setup_inputs(seed: int = 0) -> dict:
    key = jax.random.key(seed)
    ks = jax.random.split(key, 24)
    f32 = jnp.float32
    L = DEPTH

    def nrm(k, shape, scale):
        return jax.random.normal(k, shape, f32) * scale

    def gain(k, shape):
        return 1.0 + 0.02 * jax.random.normal(k, shape, f32)

    return {
        "x": nrm(ks[0], (BATCH, SEQ, D_MODEL), 1.0),
        "mem": nrm(ks[1], (BATCH, N_MEM, D_MODEL), 1.0),
        "positions": jnp.tile(jnp.arange(SEQ, dtype=jnp.int32)[None, :], (BATCH, 1)),
        "ln_emb_g": gain(ks[2], (D_MODEL,)),
        "ln_emb_b": nrm(ks[3], (D_MODEL,), 0.02),
        "hgrn_lb_logits": nrm(ks[4], (2, DEPTH + 1, HG_WIDTH), 0.5),
        "w_in": nrm(ks[5], (L, D_MODEL, IN_WIDTH), D_MODEL ** -0.5),
        "hgrn_norm_g": gain(ks[6], (L, HG_DV)),
        "mla_g_cq": gain(ks[7], (L, MLA_Q_RANK)),
        "mla_g_ckv": gain(ks[8], (L, MLA_KV_RANK)),
        "mla_w_uq": nrm(ks[9], (L, MLA_Q_RANK, MLA_HEADS * MLA_QK), MLA_Q_RANK ** -0.5),
        "mla_w_ukv": nrm(ks[10], (L, MLA_KV_RANK, MLA_HEADS * (MLA_NOPE + MLA_V)), MLA_KV_RANK ** -0.5),
        "mem_w_kv": nrm(ks[11], (L, D_MODEL, 2 * MEM_WIDTH), D_MODEL ** -0.5),
        "w_branch": nrm(ks[12], (L, N_BRANCH, BRANCH_WIDTH, D_MODEL), BETA * BRANCH_WIDTH ** -0.5),
        "w_o": nrm(ks[13], (L, D_MODEL, D_MODEL), BETA * D_MODEL ** -0.5),
        "ln1_g": gain(ks[14], (L, D_MODEL)),
        "ln1_b": nrm(ks[15], (L, D_MODEL), 0.02),
        "w_ffn_gate": nrm(ks[16], (L, D_MODEL, D_FF), D_MODEL ** -0.5),
        "w_ffn_up": nrm(ks[17], (L, D_MODEL, D_FF), D_MODEL ** -0.5),
        "w_ffn_down": nrm(ks[18], (L, D_FF, D_MODEL), BETA * D_FF ** -0.5),
        "ln2_g": gain(ks[19], (L, D_MODEL)),
        "ln2_b": nrm(ks[20], (L, D_MODEL), 0.02),
    }


def _fwd_reference(x, mem, positions, ln_emb_g, ln_emb_b, hgrn_lb_logits, w_in, hgrn_norm_g,
              mla_g_cq, mla_g_ckv, mla_w_uq, mla_w_ukv, mem_w_kv, w_branch, w_o,
              ln1_g, ln1_b, w_ffn_gate, w_ffn_up, w_ffn_down, ln2_g, ln2_b):
    B, S, D = x.shape
    f32 = jnp.float32
    half = MLA_ROPE // 2
    inv_freq = jnp.power(ROPE_THETA, -jnp.arange(half, dtype=f32) / half)
    ang = positions.astype(f32)[..., None] * inv_freq
    cos, sin = jnp.cos(ang), jnp.sin(ang)
    lb_all = jnp.cumsum(jax.nn.softmax(hgrn_lb_logits.astype(f32), axis=1), axis=1)

    h = _layernorm(x, ln_emb_g, ln_emb_b)
    for l in range(DEPTH):
        proj = h @ w_in[l]
        (q_hg, i_hg, f_fw, f_bw, g_hg, c_q, c_kv, k_rope, q_mem, gates) = _split(proj, IN_SPLITS)
        y_hg = _hgrn2_branch(q_hg, i_hg, f_fw, f_bw, g_hg, lb_all[:, l], hgrn_norm_g[l])
        y_mla = _mla_branch(c_q, c_kv, k_rope, cos, sin, mla_g_cq[l], mla_g_ckv[l],
                            mla_w_uq[l], mla_w_ukv[l])
        y_mem = _memory_branch(q_mem, mem, mem_w_kv[l])
        gsig = jax.nn.sigmoid(gates.astype(f32)).astype(h.dtype).reshape(B, S, N_BRANCH, D)
        merged = (gsig[:, :, 0] * (y_hg @ w_branch[l, 0])
                  + gsig[:, :, 1] * (y_mla @ w_branch[l, 1])
                  + gsig[:, :, 2] * (y_mem @ w_branch[l, 2]))
        mix = merged @ w_o[l]
        h = _layernorm(ALPHA * h + mix, ln1_g[l], ln1_b[l])
        ff = (jax.nn.silu(h @ w_ffn_gate[l]) * (h @ w_ffn_up[l])) @ w_ffn_down[l]
        h = _layernorm(ALPHA * h + ff, ln2_g[l], ln2_b[l])
    return h


import jax as _jax
import jax.numpy as _jnp

TWIN_FORMAT = 'train_step'
FWD_PARAMS = ['x', 'mem', 'positions', 'ln_emb_g', 'ln_emb_b', 'hgrn_lb_logits', 'w_in', 'hgrn_norm_g', 'mla_g_cq', 'mla_g_ckv', 'mla_w_uq', 'mla_w_ukv', 'mem_w_kv', 'w_branch', 'w_o', 'ln1_g', 'ln1_b', 'w_ffn_gate', 'w_ffn_up', 'w_ffn_down', 'ln2_g', 'ln2_b']
TWIN_WEIGHTS = ['ln_emb_g', 'ln_emb_b', 'hgrn_lb_logits', 'w_in', 'hgrn_norm_g', 'mla_g_cq', 'mla_g_ckv', 'mla_w_uq', 'mla_w_ukv', 'mem_w_kv', 'w_branch', 'w_o', 'ln1_g', 'ln1_b', 'w_ffn_gate', 'w_ffn_up', 'w_ffn_down', 'ln2_g', 'ln2_b']
TWIN_DIFF_INPUT = 'x'
TWIN_INPUTS = ['x', 'mem', 'positions', 'ln_emb_g', 'ln_emb_b', 'hgrn_lb_logits', 'w_in', 'hgrn_norm_g', 'mla_g_cq', 'mla_g_ckv', 'mla_w_uq', 'mla_w_ukv', 'mem_w_kv', 'w_branch', 'w_o', 'ln1_g', 'ln1_b', 'w_ffn_gate', 'w_ffn_up', 'w_ffn_down', 'ln2_g', 'ln2_b', 'loss_target', 'm_ln_emb_g', 'm_ln_emb_b', 'm_hgrn_lb_logits', 'm_w_in', 'm_hgrn_norm_g', 'm_mla_g_cq', 'm_mla_g_ckv', 'm_mla_w_uq', 'm_mla_w_ukv', 'm_mem_w_kv', 'm_w_branch', 'm_w_o', 'm_ln1_g', 'm_ln1_b', 'm_w_ffn_gate', 'm_w_ffn_up', 'm_w_ffn_down', 'm_ln2_g', 'm_ln2_b', 'v_ln_emb_g', 'v_ln_emb_b', 'v_hgrn_lb_logits', 'v_w_in', 'v_hgrn_norm_g', 'v_mla_g_cq', 'v_mla_g_ckv', 'v_mla_w_uq', 'v_mla_w_ukv', 'v_mem_w_kv', 'v_w_branch', 'v_w_o', 'v_ln1_g', 'v_ln1_b', 'v_w_ffn_gate', 'v_w_ffn_up', 'v_w_ffn_down', 'v_ln2_g', 'v_ln2_b']
TWIN_OUTPUTS = ['loss', 'grad_x', 'grad_ln_emb_g', 'grad_ln_emb_b', 'grad_hgrn_lb_logits', 'grad_w_in', 'grad_hgrn_norm_g', 'grad_mla_g_cq', 'grad_mla_g_ckv', 'grad_mla_w_uq', 'grad_mla_w_ukv', 'grad_mem_w_kv', 'grad_w_branch', 'grad_w_o', 'grad_ln1_g', 'grad_ln1_b', 'grad_w_ffn_gate', 'grad_w_ffn_up', 'grad_w_ffn_down', 'grad_ln2_g', 'grad_ln2_b', 'delta_ln_emb_g', 'delta_ln_emb_b', 'delta_hgrn_lb_logits', 'delta_w_in', 'delta_hgrn_norm_g', 'delta_mla_g_cq', 'delta_mla_g_ckv', 'delta_mla_w_uq', 'delta_mla_w_ukv', 'delta_mem_w_kv', 'delta_w_branch', 'delta_w_o', 'delta_ln1_g', 'delta_ln1_b', 'delta_w_ffn_gate', 'delta_w_ffn_up', 'delta_w_ffn_down', 'delta_ln2_g', 'delta_ln2_b', 'new_m_ln_emb_g', 'new_m_ln_emb_b', 'new_m_hgrn_lb_logits', 'new_m_w_in', 'new_m_hgrn_norm_g', 'new_m_mla_g_cq', 'new_m_mla_g_ckv', 'new_m_mla_w_uq', 'new_m_mla_w_ukv', 'new_m_mem_w_kv', 'new_m_w_branch', 'new_m_w_o', 'new_m_ln1_g', 'new_m_ln1_b', 'new_m_w_ffn_gate', 'new_m_w_ffn_up', 'new_m_w_ffn_down', 'new_m_ln2_g', 'new_m_ln2_b', 'new_v_ln_emb_g', 'new_v_ln_emb_b', 'new_v_hgrn_lb_logits', 'new_v_w_in', 'new_v_hgrn_norm_g', 'new_v_mla_g_cq', 'new_v_mla_g_ckv', 'new_v_mla_w_uq', 'new_v_mla_w_ukv', 'new_v_mem_w_kv', 'new_v_w_branch', 'new_v_w_o', 'new_v_ln1_g', 'new_v_ln1_b', 'new_v_w_ffn_gate', 'new_v_w_ffn_up', 'new_v_w_ffn_down', 'new_v_ln2_g', 'new_v_ln2_b']
TWIN_LEAF_KINDS = {'loss': 'loss', 'grad_x': 'grad_x', 'grad_ln_emb_g': 'grad_w', 'grad_ln_emb_b': 'grad_w', 'grad_hgrn_lb_logits': 'grad_w', 'grad_w_in': 'grad_w', 'grad_hgrn_norm_g': 'grad_w', 'grad_mla_g_cq': 'grad_w', 'grad_mla_g_ckv': 'grad_w', 'grad_mla_w_uq': 'grad_w', 'grad_mla_w_ukv': 'grad_w', 'grad_mem_w_kv': 'grad_w', 'grad_w_branch': 'grad_w', 'grad_w_o': 'grad_w', 'grad_ln1_g': 'grad_w', 'grad_ln1_b': 'grad_w', 'grad_w_ffn_gate': 'grad_w', 'grad_w_ffn_up': 'grad_w', 'grad_w_ffn_down': 'grad_w', 'grad_ln2_g': 'grad_w', 'grad_ln2_b': 'grad_w', 'delta_ln_emb_g': 'delta_w', 'delta_ln_emb_b': 'delta_w', 'delta_hgrn_lb_logits': 'delta_w', 'delta_w_in': 'delta_w', 'delta_hgrn_norm_g': 'delta_w', 'delta_mla_g_cq': 'delta_w', 'delta_mla_g_ckv': 'delta_w', 'delta_mla_w_uq': 'delta_w', 'delta_mla_w_ukv': 'delta_w', 'delta_mem_w_kv': 'delta_w', 'delta_w_branch': 'delta_w', 'delta_w_o': 'delta_w', 'delta_ln1_g': 'delta_w', 'delta_ln1_b': 'delta_w', 'delta_w_ffn_gate': 'delta_w', 'delta_w_ffn_up': 'delta_w', 'delta_w_ffn_down': 'delta_w', 'delta_ln2_g': 'delta_w', 'delta_ln2_b': 'delta_w', 'new_m_ln_emb_g': 'new_m', 'new_m_ln_emb_b': 'new_m', 'new_m_hgrn_lb_logits': 'new_m', 'new_m_w_in': 'new_m', 'new_m_hgrn_norm_g': 'new_m', 'new_m_mla_g_cq': 'new_m', 'new_m_mla_g_ckv': 'new_m', 'new_m_mla_w_uq': 'new_m', 'new_m_mla_w_ukv': 'new_m', 'new_m_mem_w_kv': 'new_m', 'new_m_w_branch': 'new_m', 'new_m_w_o': 'new_m', 'new_m_ln1_g': 'new_m', 'new_m_ln1_b': 'new_m', 'new_m_w_ffn_gate': 'new_m', 'new_m_w_ffn_up': 'new_m', 'new_m_w_ffn_down': 'new_m', 'new_m_ln2_g': 'new_m', 'new_m_ln2_b': 'new_m', 'new_v_ln_emb_g': 'new_v', 'new_v_ln_emb_b': 'new_v', 'new_v_hgrn_lb_logits': 'new_v', 'new_v_w_in': 'new_v', 'new_v_hgrn_norm_g': 'new_v', 'new_v_mla_g_cq': 'new_v', 'new_v_mla_g_ckv': 'new_v', 'new_v_mla_w_uq': 'new_v', 'new_v_mla_w_ukv': 'new_v', 'new_v_mem_w_kv': 'new_v', 'new_v_w_branch': 'new_v', 'new_v_w_o': 'new_v', 'new_v_ln1_g': 'new_v', 'new_v_ln1_b': 'new_v', 'new_v_w_ffn_gate': 'new_v', 'new_v_w_ffn_up': 'new_v', 'new_v_w_ffn_down': 'new_v', 'new_v_ln2_g': 'new_v', 'new_v_ln2_b': 'new_v'}


def _forward(args):
    return _fwd_reference(*[args[k] for k in FWD_PARAMS])


def _output_shape():
    def fwd():
        inp = _fwd_setup_inputs(0)
        return _fwd_reference(*[inp[k] for k in FWD_PARAMS])
    out = _jax.eval_shape(fwd)
    return out.shape, out.dtype

N_MICROBATCH = 1
ADAM_LR = 0.001
ADAM_B1 = 0.9
ADAM_B2 = 0.999
ADAM_EPS = 1e-08
ADAM_WD = 0.01
ADAM_STEP = 10
PER_EXAMPLE_BATCH_AXIS = {'x': 0, 'mem': 0, 'positions': 0, 'loss_target': 0}
SHARED_INPUTS = []
_WEIGHT_DTYPES = {'ln_emb_g': _jnp.float32, 'ln_emb_b': _jnp.float32, 'hgrn_lb_logits': _jnp.float32, 'w_in': _jnp.float32, 'hgrn_norm_g': _jnp.float32, 'mla_g_cq': _jnp.float32, 'mla_g_ckv': _jnp.float32, 'mla_w_uq': _jnp.float32, 'mla_w_ukv': _jnp.float32, 'mem_w_kv': _jnp.float32, 'w_branch': _jnp.float32, 'w_o': _jnp.float32, 'ln1_g': _jnp.float32, 'ln1_b': _jnp.float32, 'w_ffn_gate': _jnp.float32, 'w_ffn_up': _jnp.float32, 'w_ffn_down': _jnp.float32, 'ln2_g': _jnp.float32, 'ln2_b': _jnp.float32}
MOMENT_SCALE = {'ln_emb_g': 9.771424e-01, 'ln_emb_b': 4.891607e-01, 'hgrn_lb_logits': 7.740887e-04, 'w_in': 5.439247e-03, 'hgrn_norm_g': 7.345324e-02, 'mla_g_cq': 4.558731e-03, 'mla_g_ckv': 6.528106e-03, 'mla_w_uq': 2.593730e-03, 'mla_w_ukv': 3.050298e-03, 'mem_w_kv': 3.239453e-03, 'w_branch': 1.147717e-02, 'w_o': 1.984033e-02, 'ln1_g': 9.842501e-01, 'ln1_b': 4.694464e-01, 'w_ffn_gate': 2.228352e-02, 'w_ffn_up': 2.163367e-02, 'w_ffn_down': 6.030554e-02, 'ln2_g': 3.201153e+01, 'ln2_b': 9.287659e-01}


def _to_microbatches(a, axis):
    t = _jnp.moveaxis(a, axis, 0)
    t = t.reshape((N_MICROBATCH, t.shape[0] // N_MICROBATCH) + t.shape[1:])
    return _jnp.moveaxis(t, 1, axis + 1)


def setup_inputs(seed: int = 0) -> dict:
    inp = _fwd_setup_inputs(seed)
    key = _jax.random.fold_in(_jax.random.key(seed), 7919)
    shape, _ = _output_shape()
    out = dict(inp)
    out["loss_target"] = _jax.random.normal(_jax.random.fold_in(key, 0), shape, _jnp.float32)
    for i, name in enumerate(TWIN_WEIGHTS):
        w = inp[name].astype(_jnp.float32)
        if MOMENT_SCALE is None:
            s = _jnp.sqrt(_jnp.mean(_jnp.square(w)) + 1e-30)
        else:
            s = MOMENT_SCALE[name]
        km, kv = _jax.random.split(_jax.random.fold_in(key, i + 1))
        out[name] = w
        out["m_" + name] = s * _jax.random.normal(km, w.shape, _jnp.float32)
        out["v_" + name] = (s * s) * _jax.random.uniform(kv, w.shape, _jnp.float32, 0.5, 1.5)
    if N_MICROBATCH > 1:
        for name, axis in PER_EXAMPLE_BATCH_AXIS.items():
            out[name] = _to_microbatches(out[name], axis)
    return {'x': out['x'], 'mem': out['mem'], 'positions': out['positions'], 'ln_emb_g': out['ln_emb_g'], 'ln_emb_b': out['ln_emb_b'], 'hgrn_lb_logits': out['hgrn_lb_logits'], 'w_in': out['w_in'], 'hgrn_norm_g': out['hgrn_norm_g'], 'mla_g_cq': out['mla_g_cq'], 'mla_g_ckv': out['mla_g_ckv'], 'mla_w_uq': out['mla_w_uq'], 'mla_w_ukv': out['mla_w_ukv'], 'mem_w_kv': out['mem_w_kv'], 'w_branch': out['w_branch'], 'w_o': out['w_o'], 'ln1_g': out['ln1_g'], 'ln1_b': out['ln1_b'], 'w_ffn_gate': out['w_ffn_gate'], 'w_ffn_up': out['w_ffn_up'], 'w_ffn_down': out['w_ffn_down'], 'ln2_g': out['ln2_g'], 'ln2_b': out['ln2_b'], 'loss_target': out['loss_target'], 'm_ln_emb_g': out['m_ln_emb_g'], 'm_ln_emb_b': out['m_ln_emb_b'], 'm_hgrn_lb_logits': out['m_hgrn_lb_logits'], 'm_w_in': out['m_w_in'], 'm_hgrn_norm_g': out['m_hgrn_norm_g'], 'm_mla_g_cq': out['m_mla_g_cq'], 'm_mla_g_ckv': out['m_mla_g_ckv'], 'm_mla_w_uq': out['m_mla_w_uq'], 'm_mla_w_ukv': out['m_mla_w_ukv'], 'm_mem_w_kv': out['m_mem_w_kv'], 'm_w_branch': out['m_w_branch'], 'm_w_o': out['m_w_o'], 'm_ln1_g': out['m_ln1_g'], 'm_ln1_b': out['m_ln1_b'], 'm_w_ffn_gate': out['m_w_ffn_gate'], 'm_w_ffn_up': out['m_w_ffn_up'], 'm_w_ffn_down': out['m_w_ffn_down'], 'm_ln2_g': out['m_ln2_g'], 'm_ln2_b': out['m_ln2_b'], 'v_ln_emb_g': out['v_ln_emb_g'], 'v_ln_emb_b': out['v_ln_emb_b'], 'v_hgrn_lb_logits': out['v_hgrn_lb_logits'], 'v_w_in': out['v_w_in'], 'v_hgrn_norm_g': out['v_hgrn_norm_g'], 'v_mla_g_cq': out['v_mla_g_cq'], 'v_mla_g_ckv': out['v_mla_g_ckv'], 'v_mla_w_uq': out['v_mla_w_uq'], 'v_mla_w_ukv': out['v_mla_w_ukv'], 'v_mem_w_kv': out['v_mem_w_kv'], 'v_w_branch': out['v_w_branch'], 'v_w_o': out['v_w_o'], 'v_ln1_g': out['v_ln1_g'], 'v_ln1_b': out['v_ln1_b'], 'v_w_ffn_gate': out['v_w_ffn_gate'], 'v_w_ffn_up': out['v_w_ffn_up'], 'v_w_ffn_down': out['v_w_ffn_down'], 'v_ln2_g': out['v_ln2_g'], 'v_ln2_b': out['v_ln2_b']}


def _loss(weights, diff, rest, loss_target):
    with _jax.named_scope("forward"):
        args = {**rest, TWIN_DIFF_INPUT: diff, **{k: w.astype(_WEIGHT_DTYPES[k]) for k, w in weights.items()}}
        y = _forward(args)
    with _jax.named_scope("loss_head"):
        err = _jnp.square(y.astype(_jnp.float32) - loss_target)
        return 0.5 * _jnp.sum(_jnp.mean(err, axis=-1)) if err.ndim else 0.5 * err


def _adamw(w, g, m, v):
    m = ADAM_B1 * m + (1.0 - ADAM_B1) * g
    v = ADAM_B2 * v + (1.0 - ADAM_B2) * _jnp.square(g)
    m_hat = m / (1.0 - ADAM_B1 ** ADAM_STEP)
    v_hat = v / (1.0 - ADAM_B2 ** ADAM_STEP)
    delta = -ADAM_LR * (m_hat / (_jnp.sqrt(v_hat) + ADAM_EPS) + ADAM_WD * w)
    return delta, m, v


def reference(x, mem, positions, ln_emb_g, ln_emb_b, hgrn_lb_logits, w_in, hgrn_norm_g, mla_g_cq, mla_g_ckv, mla_w_uq, mla_w_ukv, mem_w_kv, w_branch, w_o, ln1_g, ln1_b, w_ffn_gate, w_ffn_up, w_ffn_down, ln2_g, ln2_b, loss_target, m_ln_emb_g, m_ln_emb_b, m_hgrn_lb_logits, m_w_in, m_hgrn_norm_g, m_mla_g_cq, m_mla_g_ckv, m_mla_w_uq, m_mla_w_ukv, m_mem_w_kv, m_w_branch, m_w_o, m_ln1_g, m_ln1_b, m_w_ffn_gate, m_w_ffn_up, m_w_ffn_down, m_ln2_g, m_ln2_b, v_ln_emb_g, v_ln_emb_b, v_hgrn_lb_logits, v_w_in, v_hgrn_norm_g, v_mla_g_cq, v_mla_g_ckv, v_mla_w_uq, v_mla_w_ukv, v_mem_w_kv, v_w_branch, v_w_o, v_ln1_g, v_ln1_b, v_w_ffn_gate, v_w_ffn_up, v_w_ffn_down, v_ln2_g, v_ln2_b):
    given = dict(x=x, mem=mem, positions=positions, ln_emb_g=ln_emb_g, ln_emb_b=ln_emb_b, hgrn_lb_logits=hgrn_lb_logits, w_in=w_in, hgrn_norm_g=hgrn_norm_g, mla_g_cq=mla_g_cq, mla_g_ckv=mla_g_ckv, mla_w_uq=mla_w_uq, mla_w_ukv=mla_w_ukv, mem_w_kv=mem_w_kv, w_branch=w_branch, w_o=w_o, ln1_g=ln1_g, ln1_b=ln1_b, w_ffn_gate=w_ffn_gate, w_ffn_up=w_ffn_up, w_ffn_down=w_ffn_down, ln2_g=ln2_g, ln2_b=ln2_b, loss_target=loss_target, m_ln_emb_g=m_ln_emb_g, m_ln_emb_b=m_ln_emb_b, m_hgrn_lb_logits=m_hgrn_lb_logits, m_w_in=m_w_in, m_hgrn_norm_g=m_hgrn_norm_g, m_mla_g_cq=m_mla_g_cq, m_mla_g_ckv=m_mla_g_ckv, m_mla_w_uq=m_mla_w_uq, m_mla_w_ukv=m_mla_w_ukv, m_mem_w_kv=m_mem_w_kv, m_w_branch=m_w_branch, m_w_o=m_w_o, m_ln1_g=m_ln1_g, m_ln1_b=m_ln1_b, m_w_ffn_gate=m_w_ffn_gate, m_w_ffn_up=m_w_ffn_up, m_w_ffn_down=m_w_ffn_down, m_ln2_g=m_ln2_g, m_ln2_b=m_ln2_b, v_ln_emb_g=v_ln_emb_g, v_ln_emb_b=v_ln_emb_b, v_hgrn_lb_logits=v_hgrn_lb_logits, v_w_in=v_w_in, v_hgrn_norm_g=v_hgrn_norm_g, v_mla_g_cq=v_mla_g_cq, v_mla_g_ckv=v_mla_g_ckv, v_mla_w_uq=v_mla_w_uq, v_mla_w_ukv=v_mla_w_ukv, v_mem_w_kv=v_mem_w_kv, v_w_branch=v_w_branch, v_w_o=v_w_o, v_ln1_g=v_ln1_g, v_ln1_b=v_ln1_b, v_w_ffn_gate=v_w_ffn_gate, v_w_ffn_up=v_w_ffn_up, v_w_ffn_down=v_w_ffn_down, v_ln2_g=v_ln2_g, v_ln2_b=v_ln2_b)
    weights = {n: given[n] for n in TWIN_WEIGHTS}
    shared = {n: given[n] for n in SHARED_INPUTS}
    per_example = {n: given[n] for n in ['x', 'mem', 'positions']}
    grad_fn = _jax.value_and_grad(_loss, argnums=(0, 1))

    def one_microbatch(ex, loss_target):
        ex = dict(ex)
        diff = ex.pop(TWIN_DIFF_INPUT)
        return grad_fn(weights, diff, {**shared, **ex}, loss_target)

    if N_MICROBATCH == 1:
        loss, (grad_w, grad_x) = one_microbatch(per_example, given["loss_target"])
    else:
        def body(carry, xs):
            loss_sum, grad_sum = carry
            l_k, (gw_k, gx_k) = one_microbatch(xs[0], xs[1])
            with _jax.named_scope("update"):
                return (loss_sum + l_k, _jax.tree.map(_jnp.add, grad_sum, gw_k)), gx_k

        init = (_jnp.zeros((), _jnp.float32), _jax.tree.map(_jnp.zeros_like, weights))
        (loss, grad_w), grad_x = _jax.lax.scan(body, init, (per_example, given["loss_target"]))
    with _jax.named_scope("update"):
        delta_w, new_m, new_v = {}, {}, {}
        for n in TWIN_WEIGHTS:
            delta_w[n], new_m[n], new_v[n] = _adamw(weights[n], grad_w[n], given["m_" + n], given["v_" + n])
    return (loss, grad_x, *[grad_w[n] for n in TWIN_WEIGHTS], *[delta_w[n] for n in TWIN_WEIGHTS],
            *[new_m[n] for n in TWIN_WEIGHTS], *[new_v[n] for n in TWIN_WEIGHTS])
```

```python
import functools

import jax
import jax.numpy as jnp
from jax import lax
from jax.experimental import pallas as pl
from jax.experimental.pallas import tpu as pltpu

F32 = jnp.float32
CDT = jnp.bfloat16
MESH = pl.DeviceIdType.MESH
N_DEV = 8
V7X_VMEM_LIMIT = 56 * 1024 * 1024
LANE = 128
SUB = 8

HG_HEADS, HG_DK, HG_CHUNK = 8, 128, 64
HG_W = HG_HEADS * HG_DK
MLA_HEADS, MLA_RANK, MLA_NOPE, MLA_ROPE, MLA_V = 8, 512, 128, 64, 128
MLA_QK = MLA_NOPE + MLA_ROPE
MLA_W = MLA_HEADS * MLA_V
MEM_HEADS, MEM_HD = 4, 256
MEM_W = MEM_HEADS * MEM_HD
BR_W = 1024
ROPE_THETA = 10000.0
ALPHA = 2.0 ** 0.25
LN_EPS = 1e-5
RMS_EPS = 1e-6
ADAM_LR, ADAM_B1, ADAM_B2, ADAM_EPS, ADAM_WD, ADAM_STEP = 0.001, 0.9, 0.999, 1e-08, 0.01, 10

OFF_Q, OFF_I, OFF_FF, OFF_FB, OFF_G = 0, 1024, 2048, 3072, 4096
OFF_CQ, OFF_CKV, OFF_QM, OFF_GATE = 5120, 5632, 6144, 7168
KR_PAD = 512


def _cparams(n_grid):
    return pltpu.CompilerParams(dimension_semantics=("arbitrary",) * n_grid, vmem_limit_bytes=V7X_VMEM_LIMIT)


def _tile(n, pref, *offsets):
    if n <= pref and all(o % n == 0 for o in offsets):
        return n
    t = (min(pref, n) // LANE) * LANE
    while t >= LANE:
        if n % t == 0 and all(o % t == 0 for o in offsets):
            return t
        t -= LANE
    raise ValueError(f"no tile for {n} {pref} {offsets}")


def _rtile(n, pref):
    if n <= pref:
        return n
    t = (pref // SUB) * SUB
    while t >= SUB:
        if n % t == 0:
            return t
        t -= SUB
    raise ValueError(f"no row tile for {n} {pref}")


def _dot(a, b, dims):
    return lax.dot_general(a.astype(CDT), b.astype(CDT), (dims, ((), ())), preferred_element_type=F32)


def _nn(a, b):
    return _dot(a, b, ((1,), (0,)))


def _nt(a, b):
    return _dot(a, b, ((1,), (1,)))


def _tn(a, b):
    return _dot(a, b, ((0,), (0,)))


_DOTS = {"nn": _nn, "nt": _nt, "tn": _tn}


def _sigmoid(x):
    return 1.0 / (1.0 + jnp.exp(-x))


def _rowsum8(v):
    r, w = v.shape
    return v.reshape(r // SUB, SUB, w).sum(axis=0)


def _fused_mm(name, mode, groups, M, N, K, tm, tn, tk, outs, epi, tiles=(), rows=(), n_racc=0):
    ni, nj, nk = M // tm, N // tn, K // tk
    assert M % tm == 0 and N % tn == 0 and K % tk == 0, (name, M, N, K, tm, tn, tk)
    assert n_racc == 0 or nj == 1
    dot = _DOTS[mode] if groups else None
    ins, in_specs = [], []
    for g in groups:
        for a, a_off, b, b_off in g:
            if mode == "tn":
                assert a_off % tm == 0
                in_specs.append(pl.BlockSpec((tk, tm), lambda i, j, k, o=a_off // tm: (k, i + o)))
            else:
                assert a_off % tk == 0
                in_specs.append(pl.BlockSpec((tm, tk), lambda i, j, k, o=a_off // tk: (i, k + o)))
            ins.append(a)
            if mode == "nt":
                assert b_off % tk == 0
                in_specs.append(pl.BlockSpec((tn, tk), lambda i, j, k, o=b_off // tk: (j, k + o)))
            else:
                assert b_off % tn == 0
                in_specs.append(pl.BlockSpec((tk, tn), lambda i, j, k, o=b_off // tn: (k, j + o)))
            ins.append(b)
    for arr, off in tiles:
        assert off % tn == 0
        ins.append(arr)
        in_specs.append(pl.BlockSpec((tm, tn), lambda i, j, k, o=off // tn: (i, j + o)))
    for arr, off in rows:
        assert off % tn == 0
        ins.append(arr)
        in_specs.append(pl.BlockSpec((1, tn), lambda i, j, k, o=off // tn: (0, j + o)))
    aliases = {}
    out_shape, out_specs = [], []
    for oi, (width, dtype, off, alias) in enumerate(outs):
        assert off % tn == 0
        if alias is not None:
            aliases[len(ins)] = oi
            ins.append(alias)
            in_specs.append(pl.BlockSpec(memory_space=pl.ANY))
        out_shape.append(jax.ShapeDtypeStruct((M, width), dtype))
        out_specs.append(pl.BlockSpec((tm, tn), lambda i, j, k, o=off // tn: (i, j + o)))
    for _ in range(n_racc):
        out_shape.append(jax.ShapeDtypeStruct((SUB, N), F32))
        out_specs.append(pl.BlockSpec((SUB, tn), lambda i, j, k: (0, 0)))
    n_alias = len(aliases)
    n_pairs = [len(g) for g in groups]
    use_scratch = nk > 1

    def body(*refs):
        it = iter(refs)
        pair_refs = [[(next(it), next(it)) for _ in range(n)] for n in n_pairs]
        tile_refs = [next(it) for _ in tiles]
        row_refs = [next(it) for _ in rows]
        for _ in range(n_alias):
            next(it)
        out_refs = [next(it) for _ in outs]
        racc_refs = [next(it) for _ in range(n_racc)]
        acc_refs = [next(it) for _ in groups] if use_scratch else []
        i, k = pl.program_id(0), pl.program_id(2)

        def products():
            res = []
            for prs in pair_refs:
                s = None
                for a_ref, b_ref in prs:
                    d = dot(a_ref[...], b_ref[...])
                    s = d if s is None else s + d
                res.append(s)
            return res

        def finish(accs):
            out_v, racc_v = epi(accs, [t[...] for t in tile_refs], [r[...] for r in row_refs])
            for o_ref, v in zip(out_refs, out_v):
                o_ref[...] = v.astype(o_ref.dtype)
            for r_ref, v in zip(racc_refs, racc_v):
                part = _rowsum8(v)

                @pl.when(i == 0)
                def _():
                    r_ref[...] = part

                @pl.when(i > 0)
                def _():
                    r_ref[...] += part

        if not use_scratch:
            finish(products())
        else:
            @pl.when(k == 0)
            def _():
                for acc in acc_refs:
                    acc[...] = jnp.zeros_like(acc)

            for acc, p in zip(acc_refs, products()):
                acc[...] += p

            @pl.when(k == nk - 1)
            def _():
                finish([acc[...] for acc in acc_refs])

    res = pl.pallas_call(
        body,
        name=name,
        grid=(ni, nj, nk),
        in_specs=in_specs,
        out_specs=out_specs,
        out_shape=out_shape,
        scratch_shapes=[pltpu.VMEM((tm, tn), F32) for _ in groups] if use_scratch else [],
        input_output_aliases=aliases,
        compiler_params=_cparams(3),
    )(*ins)
    return res


def _ln_stats(r):
    mu = jnp.mean(r, axis=-1, keepdims=True)
    xc = r - mu
    var = jnp.mean(xc * xc, axis=-1, keepdims=True)
    rstd = lax.rsqrt(var + LN_EPS)
    return xc * rstd, rstd


def _ln_bwd(dh, xhat, rstd, g):
    dxh = dh * g
    m1 = jnp.mean(dxh, axis=-1, keepdims=True)
    m2 = jnp.mean(dxh * xhat, axis=-1, keepdims=True)
    return rstd * (dxh - m1 - xhat * m2)


def _split3(x):
    hi = x.astype(CDT)
    r1 = x - hi.astype(F32)
    mid = r1.astype(CDT)
    lo = (r1 - mid.astype(F32)).astype(CDT)
    return hi, mid, lo


def _tri_matmul(tri, x):
    hi, mid, lo = _split3(x)
    return _nn(tri, hi) + _nn(tri, mid) + _nn(tri, lo)


def _dot3(dot, a, b):
    a_hi, b_hi = a.astype(CDT), b.astype(CDT)
    a_lo = (a - a_hi.astype(F32)).astype(CDT)
    b_lo = (b - b_hi.astype(F32)).astype(CDT)
    return dot(a_hi, b_hi) + dot(a_hi, b_lo) + dot(a_lo, b_hi)


def _gla_masks(reverse):
    C = HG_CHUNK
    r = lax.broadcasted_iota(jnp.int32, (C, C), 0)
    c = lax.broadcasted_iota(jnp.int32, (C, C), 1)
    keep = (c >= r) if reverse else (r >= c)
    return keep


def _gla_chunk_fwd(qraw, iraw, fraw, lb, keep, reverse):
    C = HG_CHUNK
    sq = _sigmoid(qraw)
    q = qraw * sq
    sg = _sigmoid(fraw)
    f = lb + (1.0 - lb) * sg
    k = 1.0 - f
    g = jnp.log(f)
    tri = jnp.where(keep, 1.0, 0.0).astype(CDT)
    b = _tri_matmul(tri, g)
    end = 0 if reverse else C - 1
    b_end = b[end:end + 1, :]
    b_mid = b[C // 2:C // 2 + 1, :]
    eq = jnp.exp(b - b_mid)
    ek = jnp.exp(b_mid - b)
    eb = jnp.exp(b)
    e2 = jnp.exp(b_end - b)
    e_end = jnp.exp(b_end)
    qt = q * eq
    kt = k * ek
    qs = (q * eb).astype(CDT)
    k2 = (k * e2).astype(CDT)
    a = jnp.where(keep, _dot3(_nt, qt, kt), 0.0).astype(CDT)
    return dict(sq=sq, q=q, sg=sg, f=f, k=k, eq=eq, ek=ek, eb=eb, e2=e2, e_end=e_end, qt=qt, kt=kt, qs=qs, k2=k2, a=a)


def _gla_fwd(proj, lbl4, f_off, reverse, name):
    S = proj.shape[0]
    C = HG_CHUNK
    R = _rtile(S, 512)
    cpb, nblk = R // C, S // R
    d = 1 if reverse else 0
    blk_map = (lambda b: nblk - 1 - b) if reverse else (lambda b: b)

    def body(q_ref, i_ref, f_ref, lb_ref, o_ref, st_ref, s_scr):
        @pl.when(pl.program_id(1) == 0)
        def _():
            s_scr[...] = jnp.zeros_like(s_scr)

        l = lb_ref[...]
        lb = _sigmoid(l[2 * d:2 * d + 1, :] - l[2 * d + 1:2 * d + 2, :])
        keep = _gla_masks(reverse)
        for cc in range(cpb):
            c = cpb - 1 - cc if reverse else cc
            sl = pl.ds(c * C, C)
            v = i_ref[sl, :]
            t = _gla_chunk_fwd(q_ref[sl, :], v, f_ref[sl, :], lb, keep, reverse)
            st = s_scr[...]
            st_ref[c, 0] = st
            o_ref[sl, :] = _nt(t["qs"], st) + _nn(t["a"], v)
            s_scr[...] = t["e_end"] * st + _tn(v, t["k2"])

    col = lambda off: (lambda h, b: (blk_map(b), off // HG_DK + h))
    return pl.pallas_call(
        body,
        name=name,
        grid=(HG_HEADS, nblk),
        in_specs=[
            pl.BlockSpec((R, HG_DK), col(OFF_Q)),
            pl.BlockSpec((R, HG_DK), col(OFF_I)),
            pl.BlockSpec((R, HG_DK), col(f_off)),
            pl.BlockSpec((4, HG_DK), lambda h, b: (0, h)),
        ],
        out_specs=[
            pl.BlockSpec((R, HG_DK), lambda h, b: (blk_map(b), h)),
            pl.BlockSpec((cpb, 1, HG_DK, HG_DK), lambda h, b: (blk_map(b), h, 0, 0)),
        ],
        out_shape=[
            jax.ShapeDtypeStruct((S, HG_W), F32),
            jax.ShapeDtypeStruct((S // C, HG_HEADS, HG_DK, HG_DK), F32),
        ],
        scratch_shapes=[pltpu.VMEM((HG_DK, HG_DK), F32)],
        compiler_params=_cparams(2),
    )(proj, proj, proj, lbl4)


def _gla_bwd(proj, lbl4, f_off, reverse, do, states, dproj, prev, name):
    S = proj.shape[0]
    PW = proj.shape[1]
    C = HG_CHUNK
    R = _rtile(S, 512)
    cpb, nblk = R // C, S // R
    d = 1 if reverse else 0
    blk_map = (lambda b: b) if reverse else (lambda b: nblk - 1 - b)
    final = prev is not None

    def body(*refs):
        if final:
            q_ref, i_ref, f_ref, lb_ref, do_ref, st_ref, pq_ref, pi_ref, _dp, dq_ref, di_ref, df_ref, dl_ref, ds_scr = refs
        else:
            q_ref, i_ref, f_ref, lb_ref, do_ref, st_ref, dq_ref, di_ref, df_ref, dl_ref, ds_scr = refs
        blk = pl.program_id(1)

        @pl.when(blk == 0)
        def _():
            ds_scr[...] = jnp.zeros_like(ds_scr)
            dl_ref[...] = jnp.zeros_like(dl_ref)

        l = lb_ref[...]
        lb = _sigmoid(l[2 * d:2 * d + 1, :] - l[2 * d + 1:2 * d + 2, :])
        keep = _gla_masks(reverse)
        keep_t = _gla_masks(not reverse)
        tri_t = jnp.where(keep_t, 1.0, 0.0).astype(CDT)
        end = 0 if reverse else C - 1
        is_end = lax.broadcasted_iota(jnp.int32, (C, HG_DK), 0) == end
        dl_acc = jnp.zeros((SUB, HG_DK), F32)
        for cc in range(cpb):
            c = cc if reverse else cpb - 1 - cc
            sl = pl.ds(c * C, C)
            qraw, v, fraw = q_ref[sl, :], i_ref[sl, :], f_ref[sl, :]
            t = _gla_chunk_fwd(qraw, v, fraw, lb, keep, reverse)
            dob = do_ref[sl, :].astype(CDT)
            vb = v.astype(CDT)
            st = st_ref[c, 0]
            ds = ds_scr[...]
            dsb = ds.astype(CDT)
            d_qs = _nn(dob, st)
            d_a = jnp.where(keep, _nt(dob, vb), 0.0)
            d_qt = _dot3(_nn, d_a, t["kt"])
            d_kt = _dot3(_tn, d_a, t["qt"])
            d_v = _tn(t["a"], dob) + _nt(t["k2"], dsb)
            d_k2 = _nn(vb, dsb)
            d_e = jnp.sum(st * ds, axis=0, keepdims=True)
            ds_scr[...] = t["e_end"] * ds + _tn(dob, t["qs"])
            dq = d_qt * t["eq"] + d_qs * t["eb"]
            dk = d_kt * t["ek"] + d_k2 * t["e2"]
            db = t["q"] * dq - t["k"] * dk
            db_end = jnp.sum(d_k2 * (t["k"] * t["e2"]), axis=0, keepdims=True) + d_e * t["e_end"]
            db = db + jnp.where(is_end, db_end, 0.0)
            dg = _tri_matmul(tri_t, db)
            df = dg / t["f"] - dk
            sg, sq = t["sg"], t["sq"]
            dfraw = df * (1.0 - lb) * sg * (1.0 - sg)
            dl_acc = dl_acc + _rowsum8(df * (1.0 - sg))
            dqraw = dq * (sq * (1.0 + qraw * (1.0 - sq)))
            if final:
                dqraw = dqraw + pq_ref[sl, :]
                d_v = d_v + pi_ref[sl, :]
            dq_ref[sl, :] = dqraw.astype(dq_ref.dtype)
            di_ref[sl, :] = d_v.astype(di_ref.dtype)
            df_ref[sl, :] = dfraw.astype(df_ref.dtype)
        dl_ref[...] += dl_acc * (lb * (1.0 - lb))

    col = lambda off: (lambda h, b: (blk_map(b), off // HG_DK + h))
    blk = lambda: pl.BlockSpec((R, HG_DK), lambda h, b: (blk_map(b), h))
    ins = [proj, proj, proj, lbl4, do, states]
    in_specs = [
        pl.BlockSpec((R, HG_DK), col(OFF_Q)),
        pl.BlockSpec((R, HG_DK), col(OFF_I)),
        pl.BlockSpec((R, HG_DK), col(f_off)),
        pl.BlockSpec((4, HG_DK), lambda h, b: (0, h)),
        blk(),
        pl.BlockSpec((cpb, 1, HG_DK, HG_DK), lambda h, b: (blk_map(b), h, 0, 0)),
    ]
    dl_shape = jax.ShapeDtypeStruct((SUB, HG_W), F32)
    dl_spec = pl.BlockSpec((SUB, HG_DK), lambda h, b: (0, h))
    dp_shape = jax.ShapeDtypeStruct((S, PW), CDT)
    if final:
        ins += [prev[0], prev[1], dproj]
        in_specs += [blk(), blk(), pl.BlockSpec(memory_space=pl.ANY)]
        out_shape = [jax.ShapeDtypeStruct((S, HG_W), CDT), jax.ShapeDtypeStruct((S, HG_W), CDT), dp_shape, dl_shape]
        out_specs = [blk(), blk(), pl.BlockSpec((R, HG_DK), col(f_off)), dl_spec]
        aliases = {8: 2}
    else:
        out_shape = [jax.ShapeDtypeStruct((S, HG_W), F32), jax.ShapeDtypeStruct((S, HG_W), F32), dp_shape, dl_shape]
        out_specs = [blk(), blk(), pl.BlockSpec((R, HG_DK), col(f_off)), dl_spec]
        aliases = {}
        if dproj is not None:
            ins += [dproj]
            in_specs += [pl.BlockSpec(memory_space=pl.ANY)]
            aliases = {6: 2}
    if (not final) and dproj is not None:
        def body_wrapped(*refs, _b=body):
            _b(*refs[:6], *refs[7:])
        kern = body_wrapped
    else:
        kern = body
    dq, di, dproj, dl = pl.pallas_call(
        kern,
        name=name,
        grid=(HG_HEADS, nblk),
        in_specs=in_specs,
        out_specs=out_specs,
        out_shape=out_shape,
        scratch_shapes=[pltpu.VMEM((HG_DK, HG_DK), F32)],
        input_output_aliases=aliases,
        compiler_params=_cparams(2),
    )(*ins)
    return dproj, dq, di, dl


def _hgrn_post_fwd(o_f, o_b, proj, norm_g):
    S = o_f.shape[0]

    def epi(accs, tiles, rows):
        of, ob, graw = tiles
        ng = rows[0][:, :HG_DK]
        o = of + ob
        ys = []
        for h in range(HG_HEADS):
            oh = o[:, h * HG_DK:(h + 1) * HG_DK]
            rs = lax.rsqrt(jnp.mean(oh * oh, axis=-1, keepdims=True) + RMS_EPS)
            ys.append(oh * rs * ng * _sigmoid(graw[:, h * HG_DK:(h + 1) * HG_DK]))
        return [jnp.concatenate(ys, axis=1)], []

    tm = _rtile(S, 512)
    (y,) = _fused_mm("hgrn_post_fwd", "nn", [], S, HG_W, 1, tm, HG_W, 1, [(HG_W, CDT, 0, None)], epi,
                     tiles=[(o_f, 0), (o_b, 0), (proj, OFF_G)], rows=[(jnp.tile(norm_g, (1, HG_HEADS)), 0)])
    return y


def _hgrn_post_bwd(dy, o_f, o_b, proj, norm_g, dproj):
    S = o_f.shape[0]

    def epi(accs, tiles, rows):
        dyv, of, ob, graw = tiles
        ng = rows[0][:, :HG_DK]
        o = of + ob
        dos, dgs, dns = [], [], []
        for h in range(HG_HEADS):
            sl = slice(h * HG_DK, (h + 1) * HG_DK)
            oh, gh, dyh = o[:, sl], graw[:, sl], dyv[:, sl].astype(F32)
            rs = lax.rsqrt(jnp.mean(oh * oh, axis=-1, keepdims=True) + RMS_EPS)
            xh = oh * rs
            sg = _sigmoid(gh)
            dn = dyh * sg
            dgs.append(dyh * (xh * ng) * sg * (1.0 - sg))
            dns.append(dn * xh)
            dxh = dn * ng
            dos.append(rs * (dxh - xh * jnp.mean(dxh * xh, axis=-1, keepdims=True)))
        return [jnp.concatenate(dos, axis=1), jnp.concatenate(dgs, axis=1)], [jnp.concatenate(dns, axis=1)]

    tm = _rtile(S, 512)
    do, dproj, dn = _fused_mm("hgrn_post_bwd", "nn", [], S, HG_W, 1, tm, HG_W, 1,
                              [(HG_W, F32, 0, None), (dproj.shape[1], CDT, OFF_G, dproj)], epi,
                              tiles=[(dy, 0), (o_f, 0), (o_b, 0), (proj, OFF_G)],
                              rows=[(jnp.tile(norm_g, (1, HG_HEADS)), 0)], n_racc=1)
    return do, dproj, dn


def _copy_into(name, src, dst, off):
    S, W = src.shape
    tm = _rtile(S, 512)
    (dst,) = _fused_mm(name, "nn", [], S, W, 1, tm, W, 1, [(dst.shape[1], dst.dtype, off, dst)],
                       lambda accs, tiles, rows: ([tiles[0]], []), tiles=[(src, 0)])
    return dst


def _rms_stats(x):
    rs = lax.rsqrt(jnp.mean(x * x, axis=-1, keepdims=True) + RMS_EPS)
    return x * rs, rs


def _mla_up(proj, cf, sf, g_cq, g_ckv, wuq_p, wukv):
    S = proj.shape[0]
    tm = _rtile(S, 512)
    H = MLA_HEADS

    def body(cq_ref, ckv_ref, kr_ref, krot_ref, cf_ref, sf_ref, gq_ref, gkv_ref, wq_ref, wkv_ref,
             q_ref, k_ref, v_ref, vt_ref, cqn_ref, ckvn_ref):
        cqn = (_rms_stats(cq_ref[...])[0] * gq_ref[...]).astype(CDT)
        ckvn = (_rms_stats(ckv_ref[...])[0] * gkv_ref[...]).astype(CDT)
        cqn_ref[...] = cqn
        ckvn_ref[...] = ckvn
        cfv, sfv = cf_ref[...], sf_ref[...]
        r = _nn(cqn, wq_ref[0])
        q_ref[0, :, 0:LANE] = r[:, 0:LANE].astype(CDT)
        q_ref[0, :, LANE:2 * LANE] = (r[:, LANE:2 * LANE] * cfv + r[:, 2 * LANE:3 * LANE] * sfv).astype(CDT)
        kv = _nn(ckvn, wkv_ref[0])
        k_ref[0, :, 0:LANE] = kv[:, 0:LANE].astype(CDT)
        k_ref[0, :, LANE:2 * LANE] = (kr_ref[...] * cfv + krot_ref[...] * sfv).astype(CDT)
        vv = kv[:, LANE:2 * LANE]
        v_ref[0] = vv.astype(CDT)
        vt_ref[0, 0] = vv.T.astype(CDT)

    PWb = proj.shape[1]
    kr_off = PWb - KR_PAD
    cspec = lambda off, w: pl.BlockSpec((tm, w), lambda i, h, o=off // w: (i, o))
    return pl.pallas_call(
        body,
        name="mla_up_fwd",
        grid=(S // tm, H),
        in_specs=[
            cspec(OFF_CQ, MLA_RANK), cspec(OFF_CKV, MLA_RANK), cspec(kr_off, LANE), cspec(kr_off + LANE, LANE),
            pl.BlockSpec((tm, LANE), lambda i, h: (i, 0)), pl.BlockSpec((tm, LANE), lambda i, h: (i, 0)),
            pl.BlockSpec((1, MLA_RANK), lambda i, h: (0, 0)), pl.BlockSpec((1, MLA_RANK), lambda i, h: (0, 0)),
            pl.BlockSpec((1, MLA_RANK, 3 * LANE), lambda i, h: (h, 0, 0)),
            pl.BlockSpec((1, MLA_RANK, 2 * LANE), lambda i, h: (h, 0, 0)),
        ],
        out_specs=[
            pl.BlockSpec((1, tm, 2 * LANE), lambda i, h: (h, i, 0)),
            pl.BlockSpec((1, tm, 2 * LANE), lambda i, h: (h, i, 0)),
            pl.BlockSpec((1, tm, LANE), lambda i, h: (h, i, 0)),
            pl.BlockSpec((1, 1, LANE, tm), lambda i, h: (h, i, 0, 0)),
            pl.BlockSpec((tm, MLA_RANK), lambda i, h: (i, 0)),
            pl.BlockSpec((tm, MLA_RANK), lambda i, h: (i, 0)),
        ],
        out_shape=[
            jax.ShapeDtypeStruct((H, S, 2 * LANE), CDT), jax.ShapeDtypeStruct((H, S, 2 * LANE), CDT),
            jax.ShapeDtypeStruct((H, S, LANE), CDT), jax.ShapeDtypeStruct((H, S // tm, LANE, tm), CDT),
            jax.ShapeDtypeStruct((S, MLA_RANK), CDT), jax.ShapeDtypeStruct((S, MLA_RANK), CDT),
        ],
        compiler_params=_cparams(2),
    )(proj, proj, proj, proj, cf, sf, g_cq, g_ckv, wuq_p, wukv)


def _mla_attn_fwd(q_cat, k_cat, vt):
    H, S, _ = q_cat.shape
    tq = _tile(S, 512)
    _, nkb, _, tk = vt.shape
    nq = S // tq
    scale = MLA_QK ** -0.5

    def body(q_ref, k_ref, vt_ref, y_ref, ot_ref, lse_ref, m_scr, l_scr, acc_scr):
        q = q_ref[0]
        m_scr[...] = jnp.full_like(m_scr, -jnp.inf)
        l_scr[...] = jnp.zeros_like(l_scr)
        acc_scr[...] = jnp.zeros_like(acc_scr)

        def step(j, carry):
            kj = k_ref[0, pl.ds(pl.multiple_of(j * tk, tk), tk), :]
            vtj = vt_ref[0, j]
            st = _nt(kj, q) * scale
            m_old = m_scr[...]
            m_new = jnp.maximum(m_old, jnp.max(st, axis=0, keepdims=True))
            alpha = jnp.exp(m_old - m_new)
            pt = jnp.exp(st - m_new)
            l_scr[...] = alpha * l_scr[...] + jnp.sum(pt, axis=0, keepdims=True)
            acc_scr[...] = alpha * acc_scr[...] + _nn(vtj, pt)
            m_scr[...] = m_new
            return carry

        lax.fori_loop(0, nkb, step, 0)
        l = l_scr[...]
        ot = acc_scr[...] / l
        ot_ref[0] = ot
        y_ref[...] = ot.T.astype(CDT)
        lse_ref[0, 0] = m_scr[...] + jnp.log(l)

    return pl.pallas_call(
        body,
        name="mla_attn_fwd",
        grid=(H, nq),
        in_specs=[
            pl.BlockSpec((1, tq, 2 * LANE), lambda h, i: (h, i, 0)),
            pl.BlockSpec((1, S, 2 * LANE), lambda h, i: (h, 0, 0)),
            pl.BlockSpec((1, nkb, LANE, tk), lambda h, i: (h, 0, 0, 0)),
        ],
        out_specs=[
            pl.BlockSpec((tq, LANE), lambda h, i: (i, h)),
            pl.BlockSpec((1, LANE, tq), lambda h, i: (h, 0, i)),
            pl.BlockSpec((1, 1, 1, tq), lambda h, i: (h, i, 0, 0)),
        ],
        out_shape=[
            jax.ShapeDtypeStruct((S, H * LANE), CDT),
            jax.ShapeDtypeStruct((H, LANE, S), F32),
            jax.ShapeDtypeStruct((H, nq, 1, tq), F32),
        ],
        scratch_shapes=[pltpu.VMEM((1, tq), F32), pltpu.VMEM((1, tq), F32), pltpu.VMEM((LANE, tq), F32)],
        compiler_params=_cparams(2),
    )(q_cat, k_cat, vt)


def _mla_delta(dy, ot):
    H, _, S = ot.shape
    tq = _tile(S, 512)
    nq = S // tq

    def body(dy_ref, ot_ref, d_ref):
        d_ref[0, 0] = jnp.sum(dy_ref[...].astype(F32).T * ot_ref[0], axis=0, keepdims=True)

    return pl.pallas_call(
        body,
        name="mla_delta",
        grid=(H, nq),
        in_specs=[pl.BlockSpec((tq, LANE), lambda h, i: (i, h)), pl.BlockSpec((1, LANE, tq), lambda h, i: (h, 0, i))],
        out_specs=pl.BlockSpec((1, 1, 1, tq), lambda h, i: (h, i, 0, 0)),
        out_shape=jax.ShapeDtypeStruct((H, nq, 1, tq), F32),
        compiler_params=_cparams(2),
    )(dy, ot)


def _mla_attn_bwd(q_cat, k_cat, v, dy, lse, delta):
    H, S, _ = q_cat.shape
    _, nq, _, tq = lse.shape
    tk = _tile(S, 512)
    nkb = S // tk
    scale = MLA_QK ** -0.5

    def body(k_ref, v_ref, q_ref, do_ref, lse_ref, dl_ref, dk_ref, dv_ref, dq_ref, dk_scr, dv_scr):
        ki = pl.program_id(1)

        @pl.when(ki == 0)
        def _():
            dq_ref[...] = jnp.zeros_like(dq_ref)

        kb, vb = k_ref[0], v_ref[0]
        dk_scr[...] = jnp.zeros_like(dk_scr)
        dv_scr[...] = jnp.zeros_like(dv_scr)

        def step(i, carry):
            rows = pl.ds(pl.multiple_of(i * tq, tq), tq)
            qc = q_ref[0, rows, :]
            doc = do_ref[rows, :]
            st = _nt(kb, qc) * scale
            pt = jnp.exp(st - lse_ref[0, i])
            dv_scr[...] += _nn(pt, doc)
            dpt = _nt(vb, doc)
            dst = (pt * (dpt - dl_ref[0, i]) * scale).astype(CDT)
            dk_scr[...] += _nn(dst, qc)
            dq_ref[0, rows, :] += _tn(dst, kb)
            return carry

        lax.fori_loop(0, nq, step, 0)
        dk_ref[0] = dk_scr[...]
        dv_ref[0] = dv_scr[...]

    return pl.pallas_call(
        body,
        name="mla_attn_bwd",
        grid=(H, nkb),
        in_specs=[
            pl.BlockSpec((1, tk, 2 * LANE), lambda h, j: (h, j, 0)),
            pl.BlockSpec((1, tk, LANE), lambda h, j: (h, j, 0)),
            pl.BlockSpec((1, S, 2 * LANE), lambda h, j: (h, 0, 0)),
            pl.BlockSpec((S, LANE), lambda h, j: (0, h)),
            pl.BlockSpec((1, nq, 1, tq), lambda h, j: (h, 0, 0, 0)),
            pl.BlockSpec((1, nq, 1, tq), lambda h, j: (h, 0, 0, 0)),
        ],
        out_specs=[
            pl.BlockSpec((1, tk, 2 * LANE), lambda h, j: (h, j, 0)),
            pl.BlockSpec((1, tk, LANE), lambda h, j: (h, j, 0)),
            pl.BlockSpec((1, S, 2 * LANE), lambda h, j: (h, 0, 0)),
        ],
        out_shape=[
            jax.ShapeDtypeStruct((H, S, 2 * LANE), F32),
            jax.ShapeDtypeStruct((H, S, LANE), F32),
            jax.ShapeDtypeStruct((H, S, 2 * LANE), F32),
        ],
        scratch_shapes=[pltpu.VMEM((tk, 2 * LANE), F32), pltpu.VMEM((tk, LANE), F32)],
        compiler_params=_cparams(2),
    )(k_cat, v, q_cat, dy, lse, delta)


def _mla_up_bwd(dq_cat, dk_cat, dv, proj, cf, sf, g_cq, g_ckv, wuq_p, wukv, dproj):
    H, S, _ = dq_cat.shape
    tm = _rtile(S, 256)
    PW = proj.shape[1]
    kr_off = PW - KR_PAD

    def body(dq_ref, dk_ref, dv_ref, cq_ref, ckv_ref, cf_ref, sf_ref, gq_ref, gkv_ref, wq_ref, wkv_ref,
             dqp_ref, dkvp_ref, dcq_ref, dckv_ref, dkr_ref, dgq_ref, dgkv_ref, aq_scr, akv_scr, akr_scr):
        i, h = pl.program_id(0), pl.program_id(1)

        @pl.when(h == 0)
        def _():
            aq_scr[...] = jnp.zeros_like(aq_scr)
            akv_scr[...] = jnp.zeros_like(akv_scr)
            akr_scr[...] = jnp.zeros_like(akr_scr)

        cfv, sfv = cf_ref[...], sf_ref[...]
        dq = dq_ref[0]
        dqr = dq[:, LANE:2 * LANE]
        dqp = jnp.concatenate([dq[:, 0:LANE], dqr * cfv, dqr * sfv], axis=1).astype(CDT)
        dqp_ref[0] = dqp
        aq_scr[...] += _nt(dqp, wq_ref[0])
        dk = dk_ref[0]
        dkvp = jnp.concatenate([dk[:, 0:LANE], dv_ref[0]], axis=1).astype(CDT)
        dkvp_ref[0] = dkvp
        akv_scr[...] += _nt(dkvp, wkv_ref[0])
        akr_scr[...] += dk[:, LANE:2 * LANE]

        @pl.when(h == H - 1)
        def _():
            def rms_bwd(c_ref, g_ref, acc_ref, d_ref, dg_ref):
                xh, rs = _rms_stats(c_ref[...])
                dn = acc_ref[...]
                dxh = dn * g_ref[...]
                d_ref[...] = (rs * (dxh - xh * jnp.mean(dxh * xh, axis=-1, keepdims=True))).astype(d_ref.dtype)
                part = _rowsum8(dn * xh)

                @pl.when(i == 0)
                def _():
                    dg_ref[...] = part

                @pl.when(i > 0)
                def _():
                    dg_ref[...] += part

            rms_bwd(cq_ref, gq_ref, aq_scr, dcq_ref, dgq_ref)
            rms_bwd(ckv_ref, gkv_ref, akv_scr, dckv_ref, dgkv_ref)
            dkr = akr_scr[...]
            dkr_ref[...] = jnp.concatenate([dkr * cfv, dkr * sfv, jnp.zeros((tm, KR_PAD - 2 * LANE), F32)], axis=1).astype(dkr_ref.dtype)

    cspec = lambda off, w: pl.BlockSpec((tm, w), lambda i, h, o=off // w: (i, o))
    hspec = lambda w: pl.BlockSpec((1, tm, w), lambda i, h: (h, i, 0))
    outs = pl.pallas_call(
        body,
        name="mla_up_bwd",
        grid=(S // tm, H),
        in_specs=[
            hspec(2 * LANE), hspec(2 * LANE), hspec(LANE),
            cspec(OFF_CQ, MLA_RANK), cspec(OFF_CKV, MLA_RANK),
            pl.BlockSpec((tm, LANE), lambda i, h: (i, 0)), pl.BlockSpec((tm, LANE), lambda i, h: (i, 0)),
            pl.BlockSpec((1, MLA_RANK), lambda i, h: (0, 0)), pl.BlockSpec((1, MLA_RANK), lambda i, h: (0, 0)),
            pl.BlockSpec((1, MLA_RANK, 3 * LANE), lambda i, h: (h, 0, 0)),
            pl.BlockSpec((1, MLA_RANK, 2 * LANE), lambda i, h: (h, 0, 0)),
        ],
        out_specs=[
            hspec(3 * LANE), hspec(2 * LANE),
            pl.BlockSpec((tm, MLA_RANK), lambda i, h: (i, 0)),
            pl.BlockSpec((tm, MLA_RANK), lambda i, h: (i, 0)),
            pl.BlockSpec((tm, KR_PAD), lambda i, h: (i, 0)),
            pl.BlockSpec((SUB, MLA_RANK), lambda i, h: (0, 0)),
            pl.BlockSpec((SUB, MLA_RANK), lambda i, h: (0, 0)),
        ],
        out_shape=[
            jax.ShapeDtypeStruct((H, S, 3 * LANE), CDT), jax.ShapeDtypeStruct((H, S, 2 * LANE), CDT),
            jax.ShapeDtypeStruct((S, MLA_RANK), CDT), jax.ShapeDtypeStruct((S, MLA_RANK), CDT),
            jax.ShapeDtypeStruct((S, KR_PAD), CDT),
            jax.ShapeDtypeStruct((SUB, MLA_RANK), F32), jax.ShapeDtypeStruct((SUB, MLA_RANK), F32),
        ],
        scratch_shapes=[pltpu.VMEM((tm, MLA_RANK), F32), pltpu.VMEM((tm, MLA_RANK), F32), pltpu.VMEM((tm, LANE), F32)],
        compiler_params=_cparams(2),
    )(dq_cat, dk_cat, dv, proj, proj, cf, sf, g_cq, g_ckv, wuq_p, wukv)
    dqp, dkvp, dcq, dckv, dkr, dgq, dgkv = outs
    dproj = _copy_into("dproj_cq", dcq, dproj, OFF_CQ)
    dproj = _copy_into("dproj_ckv", dckv, dproj, OFF_CKV)
    dproj = _copy_into("dproj_kr", dkr, dproj, kr_off)
    return dproj, dqp, dkvp, dgq, dgkv


def _heads_tn(name, a, b):
    S, Ka = a.shape
    H, _, W = b.shape
    tk = _rtile(S, 1024)
    nk = S // tk

    def body(a_ref, b_ref, o_ref, acc):
        k = pl.program_id(1)

        @pl.when(k == 0)
        def _():
            acc[...] = jnp.zeros_like(acc)

        acc[...] += _tn(a_ref[...], b_ref[0])

        @pl.when(k == nk - 1)
        def _():
            o_ref[0] = acc[...].astype(o_ref.dtype)

    return pl.pallas_call(
        body,
        name=name,
        grid=(H, nk),
        in_specs=[pl.BlockSpec((tk, Ka), lambda h, k: (k, 0)), pl.BlockSpec((1, tk, W), lambda h, k: (h, k, 0))],
        out_specs=pl.BlockSpec((1, Ka, W), lambda h, k: (h, 0, 0)),
        out_shape=jax.ShapeDtypeStruct((H, Ka, W), CDT),
        scratch_shapes=[pltpu.VMEM((Ka, W), F32)],
        compiler_params=_cparams(2),
    )(a, b)


def _mem_softmax(q, k):
    s = _nt(q, k) * (MEM_HD ** -0.5)
    p = jnp.exp(s - jnp.max(s, axis=1, keepdims=True))
    return p / jnp.sum(p, axis=1, keepdims=True)


def _mem_attn_fwd(proj, memkv):
    S = proj.shape[0]
    Mm = memkv.shape[0]
    tm = _rtile(S, 512)

    def body(q_ref, k_ref, v_ref, y_ref):
        pn = _mem_softmax(q_ref[...], k_ref[...])
        y_ref[...] = _nn(pn, v_ref[...]).astype(y_ref.dtype)

    return pl.pallas_call(
        body,
        name="mem_attn_fwd",
        grid=(S // tm, MEM_HEADS),
        in_specs=[
            pl.BlockSpec((tm, MEM_HD), lambda i, h: (i, OFF_QM // MEM_HD + h)),
            pl.BlockSpec((Mm, MEM_HD), lambda i, h: (0, h)),
            pl.BlockSpec((Mm, MEM_HD), lambda i, h: (0, MEM_HEADS + h)),
        ],
        out_specs=pl.BlockSpec((tm, MEM_HD), lambda i, h: (i, h)),
        out_shape=jax.ShapeDtypeStruct((S, MEM_W), CDT),
        compiler_params=_cparams(2),
    )(proj, memkv, memkv)


def _mem_attn_bwd(dy, proj, memkv, dproj):
    S = proj.shape[0]
    Mm = memkv.shape[0]
    tm = _rtile(S, 512)
    scale = MEM_HD ** -0.5

    def body(dy_ref, q_ref, k_ref, v_ref, _dp, dq_ref, dk_ref, dv_ref):
        i = pl.program_id(1)
        q, k, dyv = q_ref[...].astype(CDT), k_ref[...], dy_ref[...]
        pn = _mem_softmax(q, k)
        dvp = _tn(pn, dyv)
        dp = _nt(dyv, v_ref[...])
        ds = pn * (dp - jnp.sum(dp * pn, axis=1, keepdims=True)) * scale
        dq_ref[...] = _nn(ds, k).astype(dq_ref.dtype)
        dkp = _tn(ds, q)

        @pl.when(i == 0)
        def _():
            dk_ref[...] = dkp
            dv_ref[...] = dvp

        @pl.when(i > 0)
        def _():
            dk_ref[...] += dkp
            dv_ref[...] += dvp

    dproj, dk, dv = pl.pallas_call(
        body,
        name="mem_attn_bwd",
        grid=(MEM_HEADS, S // tm),
        in_specs=[
            pl.BlockSpec((tm, MEM_HD), lambda h, i: (i, h)),
            pl.BlockSpec((tm, MEM_HD), lambda h, i: (i, OFF_QM // MEM_HD + h)),
            pl.BlockSpec((Mm, MEM_HD), lambda h, i: (0, h)),
            pl.BlockSpec((Mm, MEM_HD), lambda h, i: (0, MEM_HEADS + h)),
            pl.BlockSpec(memory_space=pl.ANY),
        ],
        out_specs=[
            pl.BlockSpec((tm, MEM_HD), lambda h, i: (i, OFF_QM // MEM_HD + h)),
            pl.BlockSpec((Mm, MEM_HD), lambda h, i: (0, h)),
            pl.BlockSpec((Mm, MEM_HD), lambda h, i: (0, h)),
        ],
        out_shape=[
            jax.ShapeDtypeStruct(dproj.shape, dproj.dtype),
            jax.ShapeDtypeStruct((Mm, MEM_W), F32),
            jax.ShapeDtypeStruct((Mm, MEM_W), F32),
        ],
        input_output_aliases={4: 0},
        compiler_params=_cparams(2),
    )(dy, proj, memkv, memkv, dproj)
    return dproj, dk, dv


def _my_place():
    x, y, c = lax.axis_index("x"), lax.axis_index("y"), lax.axis_index("c")
    return x, y, c, 4 * x + 2 * y + c


def _peer(x, y, c, kk):
    px = 1 - x if kk & 4 else x
    py = 1 - y if kk & 2 else y
    pc = 1 - c if kk & 1 else c
    return (px, py, pc), 4 * px + 2 * py + pc


def _exchange(name, arrs, scatter):
    n = len(arrs)

    def body(*refs):
        ins, outs = refs[:n], refs[n:2 * n]
        send, recv, loc = refs[2 * n:]
        x, y, c, me = _my_place()
        copies = []
        for w in range(n):
            src = ins[w].at[me] if scatter else ins[w]
            lc = pltpu.make_async_copy(src, outs[w].at[me], loc.at[w])
            lc.start()
            copies.append(lc)
            for kk in range(1, N_DEV):
                peer, pid = _peer(x, y, c, kk)
                s = w * (N_DEV - 1) + kk - 1
                cp = pltpu.make_async_remote_copy(
                    src_ref=ins[w].at[pid] if scatter else ins[w], dst_ref=outs[w].at[me],
                    send_sem=send.at[s], recv_sem=recv.at[s], device_id=peer, device_id_type=MESH)
                cp.start()
                copies.append(cp)
        for cp in copies:
            cp.wait()

    hbm = pl.BlockSpec(memory_space=pl.ANY)
    return pl.pallas_call(
        body,
        name=name,
        in_specs=[hbm] * n,
        out_specs=[hbm] * n,
        out_shape=[jax.ShapeDtypeStruct(((N_DEV,) + a.shape[1:]) if scatter else ((N_DEV,) + a.shape), a.dtype) for a in arrs],
        scratch_shapes=[pltpu.SemaphoreType.DMA((n * (N_DEV - 1),)), pltpu.SemaphoreType.DMA((n * (N_DEV - 1),)),
                        pltpu.SemaphoreType.DMA((n,))],
        compiler_params=pltpu.CompilerParams(has_side_effects=True),
    )(*arrs)


def _small_allreduce(vec):
    NS = vec.shape[1]

    def body(v_ref, o_ref, gbuf, send, recv):
        x, y, c, me = _my_place()
        gbuf[me] = v_ref[...]
        copies = []
        for kk in range(1, N_DEV):
            peer, _ = _peer(x, y, c, kk)
            cp = pltpu.make_async_remote_copy(src_ref=v_ref, dst_ref=gbuf.at[me], send_sem=send.at[kk - 1],
                                              recv_sem=recv.at[kk - 1], device_id=peer, device_id_type=MESH)
            cp.start()
            copies.append(cp)
        for cp in copies:
            cp.wait()
        tot = gbuf[0]
        for d in range(1, N_DEV):
            tot = tot + gbuf[d]
        o_ref[...] = jnp.sum(tot, axis=0, keepdims=True)

    return pl.pallas_call(
        body,
        name="small_allreduce",
        in_specs=[pl.BlockSpec(memory_space=pltpu.VMEM)],
        out_specs=pl.BlockSpec(memory_space=pltpu.VMEM),
        out_shape=jax.ShapeDtypeStruct((1, NS), F32),
        scratch_shapes=[pltpu.VMEM((N_DEV, SUB, NS), F32), pltpu.SemaphoreType.DMA((N_DEV - 1,)),
                        pltpu.SemaphoreType.DMA((N_DEV - 1,))],
        compiler_params=pltpu.CompilerParams(has_side_effects=True, vmem_limit_bytes=V7X_VMEM_LIMIT),
    )(vec)


def _adamw_math(g, w, m, v):
    nm = ADAM_B1 * m + (1.0 - ADAM_B1) * g
    nv = ADAM_B2 * v + (1.0 - ADAM_B2) * (g * g)
    mh = nm / (1.0 - ADAM_B1 ** ADAM_STEP)
    vh = nv / (1.0 - ADAM_B2 ** ADAM_STEP)
    delta = -ADAM_LR * (mh / (jnp.sqrt(vh) + ADAM_EPS) + ADAM_WD * w)
    return delta, nm, nv


def _adam_big(name, recv, w, m, v):
    R, C = w.shape
    tr = _rtile(R, max(SUB, (65536 // C) // SUB * SUB))

    def body(r_ref, w_ref, m_ref, v_ref, g_ref, d_ref, nm_ref, nv_ref):
        g = r_ref[0].astype(F32)
        for d in range(1, N_DEV):
            g = g + r_ref[d].astype(F32)
        delta, nm, nv = _adamw_math(g, w_ref[...], m_ref[...], v_ref[...])
        g_ref[...] = g
        d_ref[...] = delta
        nm_ref[...] = nm
        nv_ref[...] = nv

    blk = pl.BlockSpec((tr, C), lambda i: (i, 0))
    return pl.pallas_call(
        body,
        name=name,
        grid=(R // tr,),
        in_specs=[pl.BlockSpec((N_DEV, tr, C), lambda i: (0, i, 0)), blk, blk, blk],
        out_specs=[blk, blk, blk, blk],
        out_shape=[jax.ShapeDtypeStruct((R, C), F32)] * 4,
        compiler_params=_cparams(1),
    )(recv, w, m, v)


def _adam_small(g, w, m, v):
    def body(g_ref, w_ref, m_ref, v_ref, d_ref, nm_ref, nv_ref):
        delta, nm, nv = _adamw_math(g_ref[...], w_ref[...], m_ref[...], v_ref[...])
        d_ref[...] = delta
        nm_ref[...] = nm
        nv_ref[...] = nv

    return pl.pallas_call(body, name="adam_small", out_shape=[jax.ShapeDtypeStruct(g.shape, F32)] * 3)(g, w, m, v)


def _rot(w):
    h = w.shape[-1] // 2
    return jnp.concatenate([-w[..., h:], w[..., :h]], axis=-1)


def _unrot(dw):
    h = dw.shape[-1] // 2
    return jnp.concatenate([dw[..., h:], -dw[..., :h]], axis=-1)


def _pad_cols(w, width):
    return jnp.pad(w, [(0, 0)] * (w.ndim - 1) + [(0, width - w.shape[-1])])


def kernel(x, mem, positions, ln_emb_g, ln_emb_b, hgrn_lb_logits, w_in, hgrn_norm_g, mla_g_cq, mla_g_ckv, mla_w_uq, mla_w_ukv, mem_w_kv, w_branch, w_o, ln1_g, ln1_b, w_ffn_gate, w_ffn_up, w_ffn_down, ln2_g, ln2_b, loss_target, m_ln_emb_g, m_ln_emb_b, m_hgrn_lb_logits, m_w_in, m_hgrn_norm_g, m_mla_g_cq, m_mla_g_ckv, m_mla_w_uq, m_mla_w_ukv, m_mem_w_kv, m_w_branch, m_w_o, m_ln1_g, m_ln1_b, m_w_ffn_gate, m_w_ffn_up, m_w_ffn_down, m_ln2_g, m_ln2_b, v_ln_emb_g, v_ln_emb_b, v_hgrn_lb_logits, v_w_in, v_hgrn_norm_g, v_mla_g_cq, v_mla_g_ckv, v_mla_w_uq, v_mla_w_ukv, v_mem_w_kv, v_w_branch, v_w_o, v_ln1_g, v_ln1_b, v_w_ffn_gate, v_w_ffn_up, v_w_ffn_down, v_ln2_g, v_ln2_b):
    x2, tgt = x[0], loss_target[0]
    S, D = x2.shape
    Mm = mem.shape[1]
    F = w_ffn_gate.shape[2] * N_DEV
    GW = 3 * D
    PW = OFF_GATE + GW + KR_PAD
    KR = OFF_GATE + GW
    NIN = w_in.shape[2] * N_DEV
    assert NIN == OFF_GATE + MLA_ROPE + GW
    _, _, _, me = _my_place()
    row = lambda a: a.reshape(1, -1)

    big_w = [w_in[0], mla_w_uq[0], mla_w_ukv[0], mem_w_kv[0], w_branch[0].reshape(3 * BR_W, -1), w_o[0],
             w_ffn_gate[0], w_ffn_up[0], w_ffn_down[0]]
    big_m = [m_w_in[0], m_mla_w_uq[0], m_mla_w_ukv[0], m_mem_w_kv[0], m_w_branch[0].reshape(3 * BR_W, -1), m_w_o[0],
             m_w_ffn_gate[0], m_w_ffn_up[0], m_w_ffn_down[0]]
    big_v = [v_w_in[0], v_mla_w_uq[0], v_mla_w_ukv[0], v_mem_w_kv[0], v_w_branch[0].reshape(3 * BR_W, -1), v_w_o[0],
             v_w_ffn_gate[0], v_w_ffn_up[0], v_w_ffn_down[0]]
    gathered = _exchange("weights_all_gather", [w.astype(CDT) for w in big_w] + [hgrn_lb_logits.reshape(4, -1)], False)
    g_in, g_uq, g_ukv, g_mkv, g_wb, g_wo, g_wg, g_wu, g_wd, g_lb = gathered
    cols = lambda g: jnp.transpose(g, (1, 0, 2)).reshape(g.shape[1], -1)
    win = cols(g_in)
    kr_w = win[:, OFF_QM:OFF_QM + MLA_ROPE]
    zeros64 = jnp.zeros_like(kr_w)
    win_p = jnp.concatenate([win[:, :OFF_QM], win[:, OFF_QM + MLA_ROPE:], kr_w, zeros64, _rot(kr_w), zeros64,
                             jnp.zeros((D, KR_PAD - 2 * LANE), CDT)], axis=1)
    wuq_p = jnp.concatenate([g_uq[..., :MLA_NOPE], _pad_cols(g_uq[..., MLA_NOPE:], LANE),
                             _pad_cols(_rot(g_uq[..., MLA_NOPE:]), LANE)], axis=-1)
    wukv = g_ukv
    wmkv = g_mkv.reshape(-1, g_mkv.shape[-1])
    wb = jnp.transpose(g_wb.reshape(N_DEV, 3, BR_W, -1), (1, 2, 0, 3)).reshape(3, BR_W, D)
    wo = g_wo.reshape(-1, D)
    wg, wu = cols(g_wg), cols(g_wu)
    wd = g_wd.reshape(-1, D)
    lbl4 = jnp.transpose(g_lb, (1, 0, 2)).reshape(4, -1)

    half = MLA_ROPE // 2
    inv_freq = jnp.power(ROPE_THETA, -jnp.arange(half, dtype=F32) / half)
    ang = positions[0].astype(F32)[:, None] * inv_freq
    cf = _pad_cols(jnp.tile(jnp.cos(ang), (1, 2)), LANE)
    sf = _pad_cols(jnp.tile(jnp.sin(ang), (1, 2)), LANE)

    tm512 = _rtile(S, 512)
    ident = lambda accs, tiles, rows: ([accs[0]], [])

    def epi_ln0(accs, tiles, rows):
        h = _ln_stats(tiles[0])[0] * rows[0] + rows[1]
        return [h, h], []

    h0, h0b = _fused_mm("ln_emb_fwd", "nn", [], S, D, 1, tm512, D, 1, [(D, F32, 0, None), (D, CDT, 0, None)], epi_ln0,
                        tiles=[(x2, 0)], rows=[(row(ln_emb_g), 0), (row(ln_emb_b), 0)])
    (proj,) = _fused_mm("proj", "nn", [[(h0b, 0, win_p, 0)]], S, PW, D, _rtile(S, 1024), _tile(PW, 512), D,
                        [(PW, F32, 0, None)], ident)
    o_f, st_f = _gla_fwd(proj, lbl4, OFF_FF, False, "gla_fwd_f")
    o_b, st_b = _gla_fwd(proj, lbl4, OFF_FB, True, "gla_fwd_b")
    y_hg = _hgrn_post_fwd(o_f, o_b, proj, hgrn_norm_g)
    q_cat, k_cat, v_mla, vt_mla, cqn, ckvn = _mla_up(proj, cf, sf, mla_g_cq, mla_g_ckv, wuq_p, wukv)
    y_mla, ot, lse = _mla_attn_fwd(q_cat, k_cat, vt_mla)
    memb = mem[0].astype(CDT)
    (memkv,) = _fused_mm("mem_kv", "nn", [[(memb, 0, wmkv, 0)]], Mm, 2 * MEM_W, D, Mm, _tile(2 * MEM_W, 512), D,
                         [(2 * MEM_W, CDT, 0, None)], ident)
    y_mem = _mem_attn_fwd(proj, memkv)
    ys = [y_hg, y_mla, y_mem]
    tnD = _tile(D, 512, OFF_GATE)

    def epi_branch(accs, tiles, rows):
        return [_sigmoid(tiles[0]) * accs[0] + _sigmoid(tiles[1]) * accs[1] + _sigmoid(tiles[2]) * accs[2]], []

    (merged,) = _fused_mm("branch_fwd", "nn", [[(ys[b], 0, wb[b], 0)] for b in range(3)], S, D, BR_W, tm512, tnD, BR_W,
                          [(D, CDT, 0, None)], epi_branch, tiles=[(proj, OFF_GATE + b * D) for b in range(3)])

    def epi_ln1(accs, tiles, rows):
        r1v = ALPHA * tiles[0] + accs[0]
        return [r1v, _ln_stats(r1v)[0] * rows[0] + rows[1]], []

    r1, h1b = _fused_mm("wo_ln1", "nn", [[(merged, 0, wo, 0)]], S, D, D, tm512, D, _tile(D, 512),
                        [(D, F32, 0, None), (D, CDT, 0, None)], epi_ln1, tiles=[(h0, 0)], rows=[(ln1_g, 0), (ln1_b, 0)])
    tnF = _tile(F, 512)

    def epi_up(accs, tiles, rows):
        gp, up = accs
        return [gp, up, gp * _sigmoid(gp) * up], []

    gpb, upb, act = _fused_mm("ffn_up", "nn", [[(h1b, 0, wg, 0)], [(h1b, 0, wu, 0)]], S, F, D, tm512, tnF, D,
                              [(F, CDT, 0, None)] * 3, epi_up)

    def epi_down(accs, tiles, rows):
        g1, b1, g2, b2 = rows
        h1 = _ln_stats(tiles[0])[0] * g1 + b1
        xh2, rstd2 = _ln_stats(ALPHA * h1 + accs[0])
        diff = xh2 * g2 + b2 - tiles[1]
        dh2 = diff * (1.0 / D)
        dr2v = _ln_bwd(dh2, xh2, rstd2, g2)
        return [dr2v, dr2v], [dh2 * xh2, dh2, diff * diff * (0.5 / D)]

    dr2, dr2b, dg2, db2, lossp = _fused_mm(
        "ffn_down_loss", "nn", [[(act, 0, wd, 0)]], S, D, F, tm512, D, tnF, [(D, F32, 0, None), (D, CDT, 0, None)],
        epi_down, tiles=[(r1, 0), (tgt, 0)], rows=[(ln1_g, 0), (ln1_b, 0), (ln2_g, 0), (ln2_b, 0)], n_racc=3)

    def epi_dact(accs, tiles, rows):
        da, gp, up = accs[0], tiles[0].astype(F32), tiles[1].astype(F32)
        s = _sigmoid(gp)
        return [da * up * (s * (1.0 + gp * (1.0 - s))), da * (gp * s)], []

    dgp, dup = _fused_mm("ffn_dact", "nt", [[(dr2b, 0, wd, 0)]], S, F, D, tm512, tnF, D, [(F, CDT, 0, None)] * 2,
                         epi_dact, tiles=[(gpb, 0), (upb, 0)])
    tkS = _rtile(S, 1024)
    (d_wd,) = _fused_mm("dw_down", "tn", [[(act, 0, dr2b, 0)]], F, D, S, tnF, D, tkS, [(D, CDT, 0, None)], ident)
    d_wg, d_wu = _fused_mm("dw_gate_up", "tn", [[(h1b, 0, dgp, 0)], [(h1b, 0, dup, 0)]], D, F, S, _tile(D, 1024), tnF, tkS,
                           [(F, CDT, 0, None)] * 2, lambda accs, tiles, rows: (accs, []))

    def epi_dh1(accs, tiles, rows):
        dh1 = accs[0] + ALPHA * tiles[0]
        xh1, rstd1 = _ln_stats(tiles[1])
        dr1v = _ln_bwd(dh1, xh1, rstd1, rows[0])
        return [dr1v, dr1v], [dh1 * xh1, dh1]

    dr1, dr1b, dg1, db1 = _fused_mm("dh1_ln1", "nt", [[(dgp, 0, wg, 0), (dup, 0, wu, 0)]], S, D, F, _rtile(S, 256), D, tnF,
                                    [(D, F32, 0, None), (D, CDT, 0, None)], epi_dh1, tiles=[(dr2, 0), (r1, 0)],
                                    rows=[(ln1_g, 0)], n_racc=2)
    (dmerged,) = _fused_mm("dmerged", "nt", [[(dr1b, 0, wo, 0)]], S, D, D, tm512, _tile(D, 512), D, [(D, CDT, 0, None)], ident)
    (d_wo,) = _fused_mm("dw_o", "tn", [[(merged, 0, dr1b, 0)]], D, D, S, _tile(D, 512), D, tkS, [(D, CDT, 0, None)], ident)

    def epi_dbranch(accs, tiles, rows):
        dm, s = tiles[0].astype(F32), _sigmoid(tiles[1])
        return [dm * s, dm * accs[0] * s * (1.0 - s)], []

    dproj = None
    d_wbs, dys = [], []
    for b in range(3):
        du, dproj = _fused_mm(f"branch_bwd{b}", "nn", [[(ys[b], 0, wb[b], 0)]], S, D, BR_W, tm512, tnD, BR_W,
                              [(D, CDT, 0, None), (PW, CDT, OFF_GATE + b * D, dproj)], epi_dbranch,
                              tiles=[(dmerged, 0), (proj, OFF_GATE + b * D)])
        (dwb,) = _fused_mm(f"dw_branch{b}", "tn", [[(ys[b], 0, du, 0)]], BR_W, D, S, _tile(BR_W, 512), D, tkS,
                           [(D, CDT, 0, None)], ident)
        (dyb,) = _fused_mm(f"dy_branch{b}", "nt", [[(du, 0, wb[b], 0)]], S, BR_W, D, tm512, _tile(BR_W, 512), D,
                           [(BR_W, F32 if b == 0 else CDT, 0, None)], ident)
        d_wbs.append(dwb)
        dys.append(dyb)
    dy_hg, dy_mla, dy_mem = dys

    dproj, dk_mem, dv_mem = _mem_attn_bwd(dy_mem, proj, memkv, dproj)
    dkv_mem = jnp.concatenate([dk_mem, dv_mem], axis=1).astype(CDT)
    (d_wmkv,) = _fused_mm("dw_memkv", "tn", [[(memb, 0, dkv_mem, 0)]], D, 2 * MEM_W, Mm, _tile(D, 512), 2 * MEM_W, Mm,
                          [(2 * MEM_W, CDT, 0, None)], ident)

    delta = _mla_delta(dy_mla, ot)
    dk_cat, dv_h, dq_cat = _mla_attn_bwd(q_cat, k_cat, v_mla, dy_mla, lse, delta)
    dproj, dqp, dkvp, dgq, dgkv = _mla_up_bwd(dq_cat, dk_cat, dv_h, proj, cf, sf, mla_g_cq, mla_g_ckv, wuq_p, wukv, dproj)
    d_wuq_p = _heads_tn("dw_uq", cqn, dqp).astype(F32)
    d_wukv = _heads_tn("dw_ukv", ckvn, dkvp)
    d_wuq = jnp.concatenate([d_wuq_p[..., :MLA_NOPE],
                             d_wuq_p[..., LANE:LANE + MLA_ROPE] + _unrot(d_wuq_p[..., 2 * LANE:2 * LANE + MLA_ROPE])],
                            axis=-1).astype(CDT)

    do_hg, dproj, dng = _hgrn_post_bwd(dy_hg, o_f, o_b, proj, hgrn_norm_g, dproj)
    dproj, dq1, di1, dl_f = _gla_bwd(proj, lbl4, OFF_FF, False, do_hg, st_f, dproj, None, "gla_bwd_f")
    dproj, dq2, di2, dl_b = _gla_bwd(proj, lbl4, OFF_FB, True, do_hg, st_b, dproj, (dq1, di1), "gla_bwd_b")
    dproj = _copy_into("dproj_q", dq2, dproj, OFF_Q)
    dproj = _copy_into("dproj_i", di2, dproj, OFF_I)

    def epi_dh0(accs, tiles, rows):
        dh0 = accs[0] + ALPHA * tiles[0]
        xh, rstd = _ln_stats(tiles[1])
        return [_ln_bwd(dh0, xh, rstd, rows[0])], [dh0 * xh, dh0]

    grad_x, dge, dbe = _fused_mm("dh0_ln_emb", "nt", [[(dproj, 0, win_p, 0)]], S, D, PW, tm512, D, _tile(PW, 512),
                                 [(D, F32, 0, None)], epi_dh0, tiles=[(dr1, 0), (x2, 0)], rows=[(row(ln_emb_g), 0)], n_racc=2)
    (d_win_p,) = _fused_mm("dw_in", "tn", [[(h0b, 0, dproj, 0)]], D, PW, S, _tile(D, 1024), _tile(PW, 512), tkS,
                           [(PW, CDT, 0, None)], ident)

    d_kr = (d_win_p[:, KR:KR + MLA_ROPE].astype(F32) + _unrot(d_win_p[:, KR + LANE:KR + LANE + MLA_ROPE].astype(F32))).astype(CDT)
    d_win = jnp.concatenate([d_win_p[:, :OFF_QM], d_kr, d_win_p[:, OFF_QM:KR]], axis=1)
    uncols = lambda dw: jnp.transpose(dw.reshape(dw.shape[0], N_DEV, -1), (1, 0, 2))
    d_wb = jnp.transpose(jnp.stack(d_wbs).reshape(3, BR_W, N_DEV, -1), (2, 0, 1, 3)).reshape(N_DEV, 3 * BR_W, -1)
    big_g = [uncols(d_win), d_wuq, d_wukv, d_wmkv.reshape(N_DEV, -1, 2 * MEM_W), d_wb, d_wo.reshape(N_DEV, -1, D),
             uncols(d_wg), uncols(d_wu), d_wd.reshape(N_DEV, -1, D)]
    recv = _exchange("grads_exchange", big_g, True)
    names = ["w_in", "w_uq", "w_ukv", "mem_w_kv", "w_branch", "w_o", "w_gate", "w_up", "w_down"]
    big_out = [_adam_big("adam_" + nme, r, w, m_, v_) for nme, r, w, m_, v_ in zip(names, recv, big_w, big_m, big_v)]

    parts = [dge, dbe, dng, dgq, dgkv, dg1, db1, dg2, db2, dl_f, dl_b, lossp]
    widths = [p.shape[1] for p in parts]
    red = _small_allreduce(jnp.concatenate(parts, axis=1))[0]
    offs = [sum(widths[:i]) for i in range(len(widths))]
    rs = [red[o:o + w_] for o, w_ in zip(offs, widths)]
    g_le_g, g_le_b, g_ng, g_gq, g_gkv, g_l1g, g_l1b, g_l2g, g_l2b, g_dlf, g_dlb, g_loss = rs
    loss = jnp.sum(g_loss)
    g_ng = g_ng.reshape(HG_HEADS, HG_DK).sum(axis=0)
    dl0 = jnp.stack([g_dlf, g_dlb])
    g_lb_full = jnp.stack([dl0, -dl0], axis=1)
    lbw = hgrn_lb_logits.shape[2]
    g_lb = lax.dynamic_slice_in_dim(g_lb_full, me * lbw, lbw, axis=2)

    small_g = [g_le_g, g_le_b, g_lb, g_ng.reshape(1, -1), g_gq.reshape(1, -1), g_gkv.reshape(1, -1), g_l1g.reshape(1, -1),
               g_l1b.reshape(1, -1), g_l2g.reshape(1, -1), g_l2b.reshape(1, -1)]
    small_w = [ln_emb_g, ln_emb_b, hgrn_lb_logits, hgrn_norm_g, mla_g_cq, mla_g_ckv, ln1_g, ln1_b, ln2_g, ln2_b]
    small_m = [m_ln_emb_g, m_ln_emb_b, m_hgrn_lb_logits, m_hgrn_norm_g, m_mla_g_cq, m_mla_g_ckv, m_ln1_g, m_ln1_b, m_ln2_g, m_ln2_b]
    small_v = [v_ln_emb_g, v_ln_emb_b, v_hgrn_lb_logits, v_hgrn_norm_g, v_mla_g_cq, v_mla_g_ckv, v_ln1_g, v_ln1_b, v_ln2_g, v_ln2_b]
    small_g = [g.reshape(w.shape) for g, w in zip(small_g, small_w)]
    pack = lambda lst: jnp.concatenate([a.reshape(-1) for a in lst]).reshape(1, -1)
    s_delta, s_nm, s_nv = _adam_small(pack(small_g), pack(small_w), pack(small_m), pack(small_v))
    sizes = [w.size for w in small_w]
    soffs = [sum(sizes[:i]) for i in range(len(sizes))]
    unpack = lambda p: [p[0, o:o + n].reshape(w.shape) for o, n, w in zip(soffs, sizes, small_w)]
    s_delta, s_nm, s_nv = unpack(s_delta), unpack(s_nm), unpack(s_nv)

    def ordered(small, big):
        sm = list(small)
        bg = [b.reshape(w.shape) for b, w in zip(big, [w_in, mla_w_uq, mla_w_ukv, mem_w_kv, w_branch, w_o, w_ffn_gate, w_ffn_up, w_ffn_down])]
        return [sm[0], sm[1], sm[2], bg[0], sm[3], sm[4], sm[5], bg[1], bg[2], bg[3], bg[4], bg[5], sm[6], sm[7], bg[6], bg[7], bg[8], sm[8], sm[9]]

    grads = ordered(small_g, [o[0] for o in big_out])
    deltas = ordered(s_delta, [o[1] for o in big_out])
    new_m = ordered(s_nm, [o[2] for o in big_out])
    new_v = ordered(s_nv, [o[3] for o in big_out])
    return (loss, grad_x[None], *grads, *deltas, *new_m, *new_v)
```

```python
import functools

import jax
import jax.numpy as jnp
from jax import lax
from jax.experimental import pallas as pl
from jax.experimental.pallas import tpu as pltpu

F32 = jnp.float32
CDT = jnp.bfloat16
MESH = pl.DeviceIdType.MESH
N_DEV = 8
V7X_VMEM_LIMIT = 56 * 1024 * 1024
LANE = 128
SUB = 8

HG_HEADS, HG_DK, HG_CHUNK = 8, 128, 64
HG_W = HG_HEADS * HG_DK
MLA_HEADS, MLA_RANK, MLA_NOPE, MLA_ROPE, MLA_V = 8, 512, 128, 64, 128
MLA_QK = MLA_NOPE + MLA_ROPE
MLA_SCALE = MLA_QK ** -0.5
MLA_QSCALE = MLA_SCALE * 1.4426950408889634
VT_ROWS = LANE + 16
MLA_W = MLA_HEADS * MLA_V
MEM_HEADS, MEM_HD = 4, 256
MEM_W = MEM_HEADS * MEM_HD
BR_W = 1024
ROPE_THETA = 10000.0
ALPHA = 2.0 ** 0.25
LN_EPS = 1e-5
RMS_EPS = 1e-6
ADAM_LR, ADAM_B1, ADAM_B2, ADAM_EPS, ADAM_WD, ADAM_STEP = 0.001, 0.9, 0.999, 1e-08, 0.01, 10

OFF_Q, OFF_I, OFF_FF, OFF_FB, OFF_G = 0, 1024, 2048, 3072, 4096
OFF_CQ, OFF_CKV, OFF_QM, OFF_GATE = 5120, 5632, 6144, 7168
KR_PAD = 512


def _cparams(n_grid, side_effects=False):
    return pltpu.CompilerParams(dimension_semantics=("arbitrary",) * n_grid, vmem_limit_bytes=V7X_VMEM_LIMIT,
                                has_side_effects=side_effects)


def _tile(n, pref, *offsets):
    if n <= pref and all(o % n == 0 for o in offsets):
        return n
    t = (min(pref, n) // LANE) * LANE
    while t >= LANE:
        if n % t == 0 and all(o % t == 0 for o in offsets):
            return t
        t -= LANE
    raise ValueError(f"no tile for {n} {pref} {offsets}")


def _rtile(n, pref):
    if n <= pref:
        return n
    t = (pref // SUB) * SUB
    while t >= SUB:
        if n % t == 0:
            return t
        t -= SUB
    raise ValueError(f"no row tile for {n} {pref}")


def _dot(a, b, dims):
    return lax.dot_general(a.astype(CDT), b.astype(CDT), (dims, ((), ())), preferred_element_type=F32)


def _nn(a, b):
    return _dot(a, b, ((1,), (0,)))


def _nt(a, b):
    return _dot(a, b, ((1,), (1,)))


def _tn(a, b):
    return _dot(a, b, ((0,), (0,)))


_DOTS = {"nn": _nn, "nt": _nt, "tn": _tn}


def _sigmoid(x):
    return 1.0 / (1.0 + jnp.exp(-x))


def _rowsum8(v):
    r, w = v.shape
    return v.reshape(r // SUB, SUB, w).sum(axis=0)


def _my_place():
    x, y, c = lax.axis_index("x"), lax.axis_index("y"), lax.axis_index("c")
    return x, y, c, 4 * x + 2 * y + c


def _peer(x, y, c, kk):
    px = 1 - x if kk & 4 else x
    py = 1 - y if kk & 2 else y
    pc = 1 - c if kk & 1 else c
    return (px, py, pc), 4 * px + 2 * py + pc


class _Xchg:
    def __init__(self, arrs, scatter):
        self.arrs, self.scatter, self.n = list(arrs), scatter, len(arrs)
        hbm = pl.BlockSpec(memory_space=pl.ANY)
        self.specs = [hbm] * self.n
        self.out_shape = [jax.ShapeDtypeStruct(((N_DEV,) + a.shape[1:]) if scatter else ((N_DEV,) + a.shape), a.dtype)
                          for a in self.arrs]
        ncp = self.n * (N_DEV - 1)
        self.scratch = [pltpu.SemaphoreType.DMA((ncp,)), pltpu.SemaphoreType.DMA((ncp,)), pltpu.SemaphoreType.DMA((self.n,))]

    def _copies(self, ins, outs, send, recv, loc):
        x, y, c, me = _my_place()
        copies = []
        for w in range(self.n):
            copies.append(pltpu.make_async_copy(ins[w].at[me] if self.scatter else ins[w], outs[w].at[me], loc.at[w]))
            for kk in range(1, N_DEV):
                peer, pid = _peer(x, y, c, kk)
                s = w * (N_DEV - 1) + kk - 1
                copies.append(pltpu.make_async_remote_copy(
                    src_ref=ins[w].at[pid] if self.scatter else ins[w], dst_ref=outs[w].at[me],
                    send_sem=send.at[s], recv_sem=recv.at[s], device_id=peer, device_id_type=MESH))
        return copies

    def start(self, ins, outs, sems):
        for cp in self._copies(ins, outs, *sems):
            cp.start()

    def wait(self, ins, outs, sems):
        for cp in self._copies(ins, outs, *sems):
            cp.wait()


def _exchange(name, arrs, scatter):
    xc = _Xchg(arrs, scatter)
    n = xc.n

    def body(*refs):
        ins, outs, sems = refs[:n], refs[n:2 * n], refs[2 * n:]
        xc.start(ins, outs, sems)
        xc.wait(ins, outs, sems)

    return pl.pallas_call(
        body,
        name=name,
        in_specs=xc.specs,
        out_specs=xc.specs,
        out_shape=xc.out_shape,
        scratch_shapes=xc.scratch,
        compiler_params=pltpu.CompilerParams(has_side_effects=True),
    )(*xc.arrs)


def _fused_mm(name, mode, groups, M, N, K, tm, tn, tk, outs, epi, tiles=(), rows=(), n_racc=0, xchg=None):
    ni, nj, nk = M // tm, N // tn, K // tk
    assert M % tm == 0 and N % tn == 0 and K % tk == 0, (name, M, N, K, tm, tn, tk)
    assert n_racc == 0 or nj == 1
    dot = _DOTS[mode] if groups else None
    ins, in_specs = [], []
    for g in groups:
        for a, a_off, b, b_off in g:
            if mode == "tn":
                assert a_off % tm == 0
                in_specs.append(pl.BlockSpec((tk, tm), lambda i, j, k, o=a_off // tm: (k, i + o)))
            else:
                assert a_off % tk == 0
                in_specs.append(pl.BlockSpec((tm, tk), lambda i, j, k, o=a_off // tk: (i, k + o)))
            ins.append(a)
            if mode == "nt":
                assert b_off % tk == 0
                in_specs.append(pl.BlockSpec((tn, tk), lambda i, j, k, o=b_off // tk: (j, k + o)))
            else:
                assert b_off % tn == 0
                in_specs.append(pl.BlockSpec((tk, tn), lambda i, j, k, o=b_off // tn: (k, j + o)))
            ins.append(b)
    for arr, off in tiles:
        assert off % tn == 0
        ins.append(arr)
        in_specs.append(pl.BlockSpec((tm, tn), lambda i, j, k, o=off // tn: (i, j + o)))
    for arr, off in rows:
        assert off % tn == 0
        ins.append(arr)
        in_specs.append(pl.BlockSpec((1, tn), lambda i, j, k, o=off // tn: (0, j + o)))
    aliases = {}
    out_shape, out_specs = [], []
    for oi, (width, dtype, off, alias) in enumerate(outs):
        assert off % tn == 0
        if alias is not None:
            aliases[len(ins)] = oi
            ins.append(alias)
            in_specs.append(pl.BlockSpec(memory_space=pl.ANY))
        out_shape.append(jax.ShapeDtypeStruct((M, width), dtype))
        out_specs.append(pl.BlockSpec((tm, tn), lambda i, j, k, o=off // tn: (i, j + o)))
    for _ in range(n_racc):
        out_shape.append(jax.ShapeDtypeStruct((SUB, N), F32))
        out_specs.append(pl.BlockSpec((SUB, tn), lambda i, j, k: (0, 0)))
    n_alias = len(aliases)
    n_pairs = [len(g) for g in groups]
    use_scratch = nk > 1
    scratch = [pltpu.VMEM((tm, tn), F32) for _ in groups] if use_scratch else []
    nx = 0
    if xchg is not None:
        nx = xchg.n
        ins += xchg.arrs
        in_specs += xchg.specs
        out_shape += xchg.out_shape
        out_specs += xchg.specs
        scratch += xchg.scratch

    def body(*refs):
        it = iter(refs)
        pair_refs = [[(next(it), next(it)) for _ in range(n)] for n in n_pairs]
        tile_refs = [next(it) for _ in tiles]
        row_refs = [next(it) for _ in rows]
        for _ in range(n_alias):
            next(it)
        x_in = [next(it) for _ in range(nx)]
        out_refs = [next(it) for _ in outs]
        racc_refs = [next(it) for _ in range(n_racc)]
        x_out = [next(it) for _ in range(nx)]
        acc_refs = [next(it) for _ in groups] if use_scratch else []
        x_sems = list(it)
        i, j, k = pl.program_id(0), pl.program_id(1), pl.program_id(2)
        if nx:
            @pl.when((i == 0) & (j == 0) & (k == 0))
            def _():
                xchg.start(x_in, x_out, x_sems)

        def products():
            res = []
            for prs in pair_refs:
                s = None
                for a_ref, b_ref in prs:
                    d = dot(a_ref[...], b_ref[...])
                    s = d if s is None else s + d
                res.append(s)
            return res

        def finish(accs):
            out_v, racc_v = epi(accs, [t[...] for t in tile_refs], [r[...] for r in row_refs])
            for o_ref, v in zip(out_refs, out_v):
                o_ref[...] = v.astype(o_ref.dtype)
            for r_ref, v in zip(racc_refs, racc_v):
                part = _rowsum8(v)

                @pl.when(i == 0)
                def _():
                    r_ref[...] = part

                @pl.when(i > 0)
                def _():
                    r_ref[...] += part

        if not use_scratch:
            finish(products())
        else:
            @pl.when(k == 0)
            def _():
                for acc in acc_refs:
                    acc[...] = jnp.zeros_like(acc)

            for acc, p in zip(acc_refs, products()):
                acc[...] += p

            @pl.when(k == nk - 1)
            def _():
                finish([acc[...] for acc in acc_refs])

        if nx:
            @pl.when((i == ni - 1) & (j == nj - 1) & (k == nk - 1))
            def _():
                xchg.wait(x_in, x_out, x_sems)

    res = pl.pallas_call(
        body,
        name=name,
        grid=(ni, nj, nk),
        in_specs=in_specs,
        out_specs=out_specs,
        out_shape=out_shape,
        scratch_shapes=scratch,
        input_output_aliases=aliases,
        compiler_params=_cparams(3, side_effects=nx > 0),
    )(*ins)
    return res


def _ln_stats(r):
    mu = jnp.mean(r, axis=-1, keepdims=True)
    xc = r - mu
    var = jnp.mean(xc * xc, axis=-1, keepdims=True)
    rstd = lax.rsqrt(var + LN_EPS)
    return xc * rstd, rstd


def _ln_bwd(dh, xhat, rstd, g):
    dxh = dh * g
    m1 = jnp.mean(dxh, axis=-1, keepdims=True)
    m2 = jnp.mean(dxh * xhat, axis=-1, keepdims=True)
    return rstd * (dxh - m1 - xhat * m2)


def _split3(x):
    hi = x.astype(CDT)
    r1 = x - hi.astype(F32)
    mid = r1.astype(CDT)
    lo = (r1 - mid.astype(F32)).astype(CDT)
    return hi, mid, lo


def _tri_matmul(tri, x):
    hi, mid, lo = _split3(x)
    return _nn(tri, hi) + _nn(tri, mid) + _nn(tri, lo)


def _dot3(dot, a, b):
    a_hi, b_hi = a.astype(CDT), b.astype(CDT)
    a_lo = (a - a_hi.astype(F32)).astype(CDT)
    b_lo = (b - b_hi.astype(F32)).astype(CDT)
    return dot(a_hi, b_hi) + dot(a_hi, b_lo) + dot(a_lo, b_hi)


def _gla_masks(reverse):
    C = HG_CHUNK
    r = lax.broadcasted_iota(jnp.int32, (C, C), 0)
    c = lax.broadcasted_iota(jnp.int32, (C, C), 1)
    keep = (c >= r) if reverse else (r >= c)
    return keep


def _gla_chunk_fwd(qraw, iraw, fraw, lb, keep, reverse):
    C = HG_CHUNK
    sq = _sigmoid(qraw)
    q = qraw * sq
    sg = _sigmoid(fraw)
    f = lb + (1.0 - lb) * sg
    k = 1.0 - f
    g = jnp.log(f)
    tri = jnp.where(keep, 1.0, 0.0).astype(CDT)
    b = _tri_matmul(tri, g)
    end = 0 if reverse else C - 1
    b_end = b[end:end + 1, :]
    b_mid = b[C // 2:C // 2 + 1, :]
    eq = jnp.exp(b - b_mid)
    ek = jnp.exp(b_mid - b)
    eb = jnp.exp(b)
    e2 = jnp.exp(b_end - b)
    e_end = jnp.exp(b_end)
    qt = q * eq
    kt = k * ek
    qs = (q * eb).astype(CDT)
    k2 = (k * e2).astype(CDT)
    a = jnp.where(keep, _dot3(_nt, qt, kt), 0.0).astype(CDT)
    return dict(sq=sq, q=q, sg=sg, f=f, k=k, eq=eq, ek=ek, eb=eb, e2=e2, e_end=e_end, qt=qt, kt=kt, qs=qs, k2=k2, a=a)


def _gla_fwd(proj, lbl4, f_off, reverse, name):
    S = proj.shape[0]
    C = HG_CHUNK
    R = _rtile(S, 512)
    cpb, nblk = R // C, S // R
    d = 1 if reverse else 0
    blk_map = (lambda b: nblk - 1 - b) if reverse else (lambda b: b)

    def body(q_ref, i_ref, f_ref, lb_ref, o_ref, st_ref, s_scr):
        @pl.when(pl.program_id(1) == 0)
        def _():
            s_scr[...] = jnp.zeros_like(s_scr)

        l = lb_ref[...]
        lb = _sigmoid(l[2 * d:2 * d + 1, :] - l[2 * d + 1:2 * d + 2, :])
        keep = _gla_masks(reverse)
        for cc in range(cpb):
            c = cpb - 1 - cc if reverse else cc
            sl = pl.ds(c * C, C)
            v = i_ref[sl, :]
            t = _gla_chunk_fwd(q_ref[sl, :], v, f_ref[sl, :], lb, keep, reverse)
            st = s_scr[...]
            st_ref[c, 0] = st
            o_ref[sl, :] = _nt(t["qs"], st) + _nn(t["a"], v)
            s_scr[...] = t["e_end"] * st + _tn(v, t["k2"])

    col = lambda off: (lambda h, b: (blk_map(b), off // HG_DK + h))
    return pl.pallas_call(
        body,
        name=name,
        grid=(HG_HEADS, nblk),
        in_specs=[
            pl.BlockSpec((R, HG_DK), col(OFF_Q)),
            pl.BlockSpec((R, HG_DK), col(OFF_I)),
            pl.BlockSpec((R, HG_DK), col(f_off)),
            pl.BlockSpec((4, HG_DK), lambda h, b: (0, h)),
        ],
        out_specs=[
            pl.BlockSpec((R, HG_DK), lambda h, b: (blk_map(b), h)),
            pl.BlockSpec((cpb, 1, HG_DK, HG_DK), lambda h, b: (blk_map(b), h, 0, 0)),
        ],
        out_shape=[
            jax.ShapeDtypeStruct((S, HG_W), F32),
            jax.ShapeDtypeStruct((S // C, HG_HEADS, HG_DK, HG_DK), F32),
        ],
        scratch_shapes=[pltpu.VMEM((HG_DK, HG_DK), F32)],
        compiler_params=_cparams(2),
    )(proj, proj, proj, lbl4)


def _gla_bwd(proj, lbl4, f_off, reverse, do, states, dproj, prev, name):
    S = proj.shape[0]
    PW = proj.shape[1]
    C = HG_CHUNK
    R = _rtile(S, 512)
    cpb, nblk = R // C, S // R
    d = 1 if reverse else 0
    blk_map = (lambda b: b) if reverse else (lambda b: nblk - 1 - b)
    final = prev is not None

    def body(*refs):
        if final:
            q_ref, i_ref, f_ref, lb_ref, do_ref, st_ref, pq_ref, pi_ref, _dp, dq_ref, di_ref, df_ref, dl_ref, ds_scr = refs
        else:
            q_ref, i_ref, f_ref, lb_ref, do_ref, st_ref, dq_ref, di_ref, df_ref, dl_ref, ds_scr = refs
        blk = pl.program_id(1)

        @pl.when(blk == 0)
        def _():
            ds_scr[...] = jnp.zeros_like(ds_scr)
            dl_ref[...] = jnp.zeros_like(dl_ref)

        l = lb_ref[...]
        lb = _sigmoid(l[2 * d:2 * d + 1, :] - l[2 * d + 1:2 * d + 2, :])
        keep = _gla_masks(reverse)
        keep_t = _gla_masks(not reverse)
        tri_t = jnp.where(keep_t, 1.0, 0.0).astype(CDT)
        end = 0 if reverse else C - 1
        is_end = lax.broadcasted_iota(jnp.int32, (C, HG_DK), 0) == end
        dl_acc = jnp.zeros((SUB, HG_DK), F32)
        for cc in range(cpb):
            c = cc if reverse else cpb - 1 - cc
            sl = pl.ds(c * C, C)
            qraw, v, fraw = q_ref[sl, :], i_ref[sl, :], f_ref[sl, :]
            t = _gla_chunk_fwd(qraw, v, fraw, lb, keep, reverse)
            dob = do_ref[sl, :].astype(CDT)
            vb = v.astype(CDT)
            st = st_ref[c, 0]
            ds = ds_scr[...]
            dsb = ds.astype(CDT)
            d_qs = _nn(dob, st)
            d_a = jnp.where(keep, _nt(dob, vb), 0.0)
            d_qt = _dot3(_nn, d_a, t["kt"])
            d_kt = _dot3(_tn, d_a, t["qt"])
            d_v = _tn(t["a"], dob) + _nt(t["k2"], dsb)
            d_k2 = _nn(vb, dsb)
            d_e = jnp.sum(st * ds, axis=0, keepdims=True)
            ds_scr[...] = t["e_end"] * ds + _tn(dob, t["qs"])
            dq = d_qt * t["eq"] + d_qs * t["eb"]
            dk = d_kt * t["ek"] + d_k2 * t["e2"]
            db = t["q"] * dq - t["k"] * dk
            db_end = jnp.sum(d_k2 * (t["k"] * t["e2"]), axis=0, keepdims=True) + d_e * t["e_end"]
            db = db + jnp.where(is_end, db_end, 0.0)
            dg = _tri_matmul(tri_t, db)
            df = dg / t["f"] - dk
            sg, sq = t["sg"], t["sq"]
            dfraw = df * (1.0 - lb) * sg * (1.0 - sg)
            dl_acc = dl_acc + _rowsum8(df * (1.0 - sg))
            dqraw = dq * (sq * (1.0 + qraw * (1.0 - sq)))
            if final:
                dqraw = dqraw + pq_ref[sl, :]
                d_v = d_v + pi_ref[sl, :]
            dq_ref[sl, :] = dqraw.astype(dq_ref.dtype)
            di_ref[sl, :] = d_v.astype(di_ref.dtype)
            df_ref[sl, :] = dfraw.astype(df_ref.dtype)
        dl_ref[...] += dl_acc * (lb * (1.0 - lb))

    col = lambda off: (lambda h, b: (blk_map(b), off // HG_DK + h))
    blk = lambda: pl.BlockSpec((R, HG_DK), lambda h, b: (blk_map(b), h))
    ins = [proj, proj, proj, lbl4, do, states]
    in_specs = [
        pl.BlockSpec((R, HG_DK), col(OFF_Q)),
        pl.BlockSpec((R, HG_DK), col(OFF_I)),
        pl.BlockSpec((R, HG_DK), col(f_off)),
        pl.BlockSpec((4, HG_DK), lambda h, b: (0, h)),
        blk(),
        pl.BlockSpec((cpb, 1, HG_DK, HG_DK), lambda h, b: (blk_map(b), h, 0, 0)),
    ]
    dl_shape = jax.ShapeDtypeStruct((SUB, HG_W), F32)
    dl_spec = pl.BlockSpec((SUB, HG_DK), lambda h, b: (0, h))
    dp_shape = jax.ShapeDtypeStruct((S, PW), CDT)
    if final:
        ins += [prev[0], prev[1], dproj]
        in_specs += [blk(), blk(), pl.BlockSpec(memory_space=pl.ANY)]
        out_shape = [jax.ShapeDtypeStruct((S, HG_W), CDT), jax.ShapeDtypeStruct((S, HG_W), CDT), dp_shape, dl_shape]
        out_specs = [blk(), blk(), pl.BlockSpec((R, HG_DK), col(f_off)), dl_spec]
        aliases = {8: 2}
    else:
        out_shape = [jax.ShapeDtypeStruct((S, HG_W), F32), jax.ShapeDtypeStruct((S, HG_W), F32), dp_shape, dl_shape]
        out_specs = [blk(), blk(), pl.BlockSpec((R, HG_DK), col(f_off)), dl_spec]
        aliases = {}
        if dproj is not None:
            ins += [dproj]
            in_specs += [pl.BlockSpec(memory_space=pl.ANY)]
            aliases = {6: 2}
    if (not final) and dproj is not None:
        def body_wrapped(*refs, _b=body):
            _b(*refs[:6], *refs[7:])
        kern = body_wrapped
    else:
        kern = body
    dq, di, dproj, dl = pl.pallas_call(
        kern,
        name=name,
        grid=(HG_HEADS, nblk),
        in_specs=in_specs,
        out_specs=out_specs,
        out_shape=out_shape,
        scratch_shapes=[pltpu.VMEM((HG_DK, HG_DK), F32)],
        input_output_aliases=aliases,
        compiler_params=_cparams(2),
    )(*ins)
    return dproj, dq, di, dl


def _hgrn_post_fwd(o_f, o_b, proj, norm_g):
    S = o_f.shape[0]

    def epi(accs, tiles, rows):
        of, ob, graw = tiles
        ng = rows[0][:, :HG_DK]
        o = of + ob
        ys = []
        for h in range(HG_HEADS):
            oh = o[:, h * HG_DK:(h + 1) * HG_DK]
            rs = lax.rsqrt(jnp.mean(oh * oh, axis=-1, keepdims=True) + RMS_EPS)
            ys.append(oh * rs * ng * _sigmoid(graw[:, h * HG_DK:(h + 1) * HG_DK]))
        return [jnp.concatenate(ys, axis=1)], []

    tm = _rtile(S, 512)
    (y,) = _fused_mm("hgrn_post_fwd", "nn", [], S, HG_W, 1, tm, HG_W, 1, [(HG_W, CDT, 0, None)], epi,
                     tiles=[(o_f, 0), (o_b, 0), (proj, OFF_G)], rows=[(jnp.tile(norm_g, (1, HG_HEADS)), 0)])
    return y


def _hgrn_post_bwd(dy, o_f, o_b, proj, norm_g, dproj):
    S = o_f.shape[0]

    def epi(accs, tiles, rows):
        dyv, of, ob, graw = tiles
        ng = rows[0][:, :HG_DK]
        o = of + ob
        dos, dgs, dns = [], [], []
        for h in range(HG_HEADS):
            sl = slice(h * HG_DK, (h + 1) * HG_DK)
            oh, gh, dyh = o[:, sl], graw[:, sl], dyv[:, sl].astype(F32)
            rs = lax.rsqrt(jnp.mean(oh * oh, axis=-1, keepdims=True) + RMS_EPS)
            xh = oh * rs
            sg = _sigmoid(gh)
            dn = dyh * sg
            dgs.append(dyh * (xh * ng) * sg * (1.0 - sg))
            dns.append(dn * xh)
            dxh = dn * ng
            dos.append(rs * (dxh - xh * jnp.mean(dxh * xh, axis=-1, keepdims=True)))
        return [jnp.concatenate(dos, axis=1), jnp.concatenate(dgs, axis=1)], [jnp.concatenate(dns, axis=1)]

    tm = _rtile(S, 512)
    do, dproj, dn = _fused_mm("hgrn_post_bwd", "nn", [], S, HG_W, 1, tm, HG_W, 1,
                              [(HG_W, F32, 0, None), (dproj.shape[1], CDT, OFF_G, dproj)], epi,
                              tiles=[(dy, 0), (o_f, 0), (o_b, 0), (proj, OFF_G)],
                              rows=[(jnp.tile(norm_g, (1, HG_HEADS)), 0)], n_racc=1)
    return do, dproj, dn


def _copy_into(name, src, dst, off):
    S, W = src.shape
    tm = _rtile(S, 512)
    (dst,) = _fused_mm(name, "nn", [], S, W, 1, tm, W, 1, [(dst.shape[1], dst.dtype, off, dst)],
                       lambda accs, tiles, rows: ([tiles[0]], []), tiles=[(src, 0)])
    return dst


def _rms_stats(x):
    rs = lax.rsqrt(jnp.mean(x * x, axis=-1, keepdims=True) + RMS_EPS)
    return x * rs, rs


def _mla_up(proj, cf, sf, g_cq, g_ckv, wuq_p, wukv):
    S = proj.shape[0]
    tm = _rtile(S, 512)
    H = MLA_HEADS

    def body(cq_ref, ckv_ref, kr_ref, krot_ref, cf_ref, sf_ref, gq_ref, gkv_ref, wq_ref, wkv_ref,
             q_ref, k_ref, v_ref, vt_ref, cqn_ref, ckvn_ref):
        cqn = (_rms_stats(cq_ref[...])[0] * gq_ref[...]).astype(CDT)
        ckvn = (_rms_stats(ckv_ref[...])[0] * gkv_ref[...]).astype(CDT)
        cqn_ref[...] = cqn
        ckvn_ref[...] = ckvn
        cfv, sfv = cf_ref[...], sf_ref[...]
        r = _nn(cqn, wq_ref[0]) * MLA_QSCALE
        q_ref[0, :, 0:LANE] = r[:, 0:LANE].astype(CDT)
        q_ref[0, :, LANE:2 * LANE] = (r[:, LANE:2 * LANE] * cfv + r[:, 2 * LANE:3 * LANE] * sfv).astype(CDT)
        kv = _nn(ckvn, wkv_ref[0])
        k_ref[0, :, 0:LANE] = kv[:, 0:LANE].astype(CDT)
        k_ref[0, :, LANE:2 * LANE] = (kr_ref[...] * cfv + krot_ref[...] * sfv).astype(CDT)
        vv = kv[:, LANE:2 * LANE]
        v_ref[0] = vv.astype(CDT)
        vt_ref[0, 0, 0:LANE, :] = vv.T.astype(CDT)
        vt_ref[0, 0, LANE:VT_ROWS, :] = jnp.ones((VT_ROWS - LANE, tm), CDT)

    PWb = proj.shape[1]
    kr_off = PWb - KR_PAD
    cspec = lambda off, w: pl.BlockSpec((tm, w), lambda i, h, o=off // w: (i, o))
    return pl.pallas_call(
        body,
        name="mla_up_fwd",
        grid=(S // tm, H),
        in_specs=[
            cspec(OFF_CQ, MLA_RANK), cspec(OFF_CKV, MLA_RANK), cspec(kr_off, LANE), cspec(kr_off + LANE, LANE),
            pl.BlockSpec((tm, LANE), lambda i, h: (i, 0)), pl.BlockSpec((tm, LANE), lambda i, h: (i, 0)),
            pl.BlockSpec((1, MLA_RANK), lambda i, h: (0, 0)), pl.BlockSpec((1, MLA_RANK), lambda i, h: (0, 0)),
            pl.BlockSpec((1, MLA_RANK, 3 * LANE), lambda i, h: (h, 0, 0)),
            pl.BlockSpec((1, MLA_RANK, 2 * LANE), lambda i, h: (h, 0, 0)),
        ],
        out_specs=[
            pl.BlockSpec((1, tm, 2 * LANE), lambda i, h: (h, i, 0)),
            pl.BlockSpec((1, tm, 2 * LANE), lambda i, h: (h, i, 0)),
            pl.BlockSpec((1, tm, LANE), lambda i, h: (h, i, 0)),
            pl.BlockSpec((1, 1, VT_ROWS, tm), lambda i, h: (h, i, 0, 0)),
            pl.BlockSpec((tm, MLA_RANK), lambda i, h: (i, 0)),
            pl.BlockSpec((tm, MLA_RANK), lambda i, h: (i, 0)),
        ],
        out_shape=[
            jax.ShapeDtypeStruct((H, S, 2 * LANE), CDT), jax.ShapeDtypeStruct((H, S, 2 * LANE), CDT),
            jax.ShapeDtypeStruct((H, S, LANE), CDT), jax.ShapeDtypeStruct((H, S // tm, VT_ROWS, tm), CDT),
            jax.ShapeDtypeStruct((S, MLA_RANK), CDT), jax.ShapeDtypeStruct((S, MLA_RANK), CDT),
        ],
        compiler_params=_cparams(2),
    )(proj, proj, proj, proj, cf, sf, g_cq, g_ckv, wuq_p, wukv)


def _mla_attn_fwd(q_cat, k_cat, vt, xchg=None):
    H, S, _ = q_cat.shape
    tq = _tile(S, 512)
    _, nkb, _, tk = vt.shape
    nq = S // tq
    nx = xchg.n if xchg is not None else 0

    def body(*refs):
        q_ref, k_ref, vt_ref = refs[:3]
        x_in = refs[3:3 + nx]
        y_ref, ot_ref, lse_ref = refs[3 + nx:6 + nx]
        x_out = refs[6 + nx:6 + 2 * nx]
        m_scr, acc_scr = refs[6 + 2 * nx:8 + 2 * nx]
        x_sems = refs[8 + 2 * nx:]
        h, i = pl.program_id(0), pl.program_id(1)
        if nx:
            @pl.when((h == 0) & (i == 0))
            def _():
                xchg.start(x_in, x_out, x_sems)

        q = q_ref[0]
        m_scr[...] = jnp.full_like(m_scr, -jnp.inf)
        acc_scr[...] = jnp.zeros_like(acc_scr)

        def step(j, carry):
            kj = k_ref[0, pl.ds(pl.multiple_of(j * tk, tk), tk), :]
            st = _nt(kj, q)
            m_old = m_scr[...]
            m_new = jnp.maximum(m_old, jnp.max(st, axis=0, keepdims=True))
            pt = jnp.exp2(st - m_new)
            acc_scr[...] = jnp.exp2(m_old - m_new) * acc_scr[...] + _nn(vt_ref[0, j], pt)
            m_scr[...] = m_new
            return carry

        lax.fori_loop(0, nkb, step, 0)
        l = acc_scr[LANE:LANE + 1, :]
        ot = acc_scr[0:LANE, :] / l
        ot_ref[0] = ot
        y_ref[...] = ot.T.astype(CDT)
        lse_ref[0, 0] = m_scr[...] + jnp.log2(l)

        if nx:
            @pl.when((h == H - 1) & (i == nq - 1))
            def _():
                xchg.wait(x_in, x_out, x_sems)

    return pl.pallas_call(
        body,
        name="mla_attn_fwd",
        grid=(H, nq),
        in_specs=[
            pl.BlockSpec((1, tq, 2 * LANE), lambda h, i: (h, i, 0)),
            pl.BlockSpec((1, S, 2 * LANE), lambda h, i: (h, 0, 0)),
            pl.BlockSpec((1, nkb, VT_ROWS, tk), lambda h, i: (h, 0, 0, 0)),
        ] + (xchg.specs if nx else []),
        out_specs=[
            pl.BlockSpec((tq, LANE), lambda h, i: (i, h)),
            pl.BlockSpec((1, LANE, tq), lambda h, i: (h, 0, i)),
            pl.BlockSpec((1, 1, 1, tq), lambda h, i: (h, i, 0, 0)),
        ] + (xchg.specs if nx else []),
        out_shape=[
            jax.ShapeDtypeStruct((S, H * LANE), CDT),
            jax.ShapeDtypeStruct((H, LANE, S), F32),
            jax.ShapeDtypeStruct((H, nq, 1, tq), F32),
        ] + (xchg.out_shape if nx else []),
        scratch_shapes=[pltpu.VMEM((1, tq), F32), pltpu.VMEM((VT_ROWS, tq), F32)] + (xchg.scratch if nx else []),
        compiler_params=_cparams(2, side_effects=nx > 0),
    )(q_cat, k_cat, vt, *(xchg.arrs if nx else []))


def _mla_delta(dy, ot):
    H, _, S = ot.shape
    tq = _tile(S, 512)
    nq = S // tq

    def body(dy_ref, ot_ref, d_ref):
        d_ref[0, 0] = jnp.sum(dy_ref[...].astype(F32).T * ot_ref[0], axis=0, keepdims=True)

    return pl.pallas_call(
        body,
        name="mla_delta",
        grid=(H, nq),
        in_specs=[pl.BlockSpec((tq, LANE), lambda h, i: (i, h)), pl.BlockSpec((1, LANE, tq), lambda h, i: (h, 0, i))],
        out_specs=pl.BlockSpec((1, 1, 1, tq), lambda h, i: (h, i, 0, 0)),
        out_shape=jax.ShapeDtypeStruct((H, nq, 1, tq), F32),
        compiler_params=_cparams(2),
    )(dy, ot)


def _mla_attn_bwd(q_cat, k_cat, v, dy, lse, delta):
    H, S, _ = q_cat.shape
    _, nq, _, tq = lse.shape
    tk = _tile(S, 512)
    nkb = S // tk

    def body(k_ref, v_ref, q_ref, do_ref, lse_ref, dl_ref, dk_ref, dv_ref, dq_ref, dk_scr, dv_scr):
        ki = pl.program_id(1)

        @pl.when(ki == 0)
        def _():
            dq_ref[...] = jnp.zeros_like(dq_ref)

        kb, vb = k_ref[0], v_ref[0]
        dk_scr[...] = jnp.zeros_like(dk_scr)
        dv_scr[...] = jnp.zeros_like(dv_scr)

        def step(i, carry):
            rows = pl.ds(pl.multiple_of(i * tq, tq), tq)
            qc = q_ref[0, rows, :]
            doc = do_ref[rows, :]
            pt = jnp.exp2(_nt(kb, qc) - lse_ref[0, i])
            dv_scr[...] += _nn(pt, doc)
            dst = (pt * (_nt(vb, doc) - dl_ref[0, i])).astype(CDT)
            dk_scr[...] += _nn(dst, qc)
            dq_ref[0, rows, :] += _tn(dst, kb)
            return carry

        lax.fori_loop(0, nq, step, 0)
        dk_ref[0] = dk_scr[...] * (MLA_SCALE / MLA_QSCALE)
        dv_ref[0] = dv_scr[...]

    return pl.pallas_call(
        body,
        name="mla_attn_bwd",
        grid=(H, nkb),
        in_specs=[
            pl.BlockSpec((1, tk, 2 * LANE), lambda h, j: (h, j, 0)),
            pl.BlockSpec((1, tk, LANE), lambda h, j: (h, j, 0)),
            pl.BlockSpec((1, S, 2 * LANE), lambda h, j: (h, 0, 0)),
            pl.BlockSpec((S, LANE), lambda h, j: (0, h)),
            pl.BlockSpec((1, nq, 1, tq), lambda h, j: (h, 0, 0, 0)),
            pl.BlockSpec((1, nq, 1, tq), lambda h, j: (h, 0, 0, 0)),
        ],
        out_specs=[
            pl.BlockSpec((1, tk, 2 * LANE), lambda h, j: (h, j, 0)),
            pl.BlockSpec((1, tk, LANE), lambda h, j: (h, j, 0)),
            pl.BlockSpec((1, S, 2 * LANE), lambda h, j: (h, 0, 0)),
        ],
        out_shape=[
            jax.ShapeDtypeStruct((H, S, 2 * LANE), F32),
            jax.ShapeDtypeStruct((H, S, LANE), F32),
            jax.ShapeDtypeStruct((H, S, 2 * LANE), F32),
        ],
        scratch_shapes=[pltpu.VMEM((tk, 2 * LANE), F32), pltpu.VMEM((tk, LANE), F32)],
        compiler_params=_cparams(2),
    )(k_cat, v, q_cat, dy, lse, delta)


def _mla_up_bwd(dq_cat, dk_cat, dv, proj, cf, sf, g_cq, g_ckv, wuq_p, wukv, dproj):
    H, S, _ = dq_cat.shape
    tm = _rtile(S, 256)
    PW = proj.shape[1]
    kr_off = PW - KR_PAD

    def body(dq_ref, dk_ref, dv_ref, cq_ref, ckv_ref, cf_ref, sf_ref, gq_ref, gkv_ref, wq_ref, wkv_ref,
             dqp_ref, dkvp_ref, dcq_ref, dckv_ref, dkr_ref, dgq_ref, dgkv_ref, aq_scr, akv_scr, akr_scr):
        i, h = pl.program_id(0), pl.program_id(1)

        @pl.when(h == 0)
        def _():
            aq_scr[...] = jnp.zeros_like(aq_scr)
            akv_scr[...] = jnp.zeros_like(akv_scr)
            akr_scr[...] = jnp.zeros_like(akr_scr)

        cfv, sfv = cf_ref[...], sf_ref[...]
        dq = dq_ref[0] * MLA_SCALE
        dqr = dq[:, LANE:2 * LANE]
        dqp = jnp.concatenate([dq[:, 0:LANE], dqr * cfv, dqr * sfv], axis=1).astype(CDT)
        dqp_ref[0] = dqp
        aq_scr[...] += _nt(dqp, wq_ref[0])
        dk = dk_ref[0]
        dkvp = jnp.concatenate([dk[:, 0:LANE], dv_ref[0]], axis=1).astype(CDT)
        dkvp_ref[0] = dkvp
        akv_scr[...] += _nt(dkvp, wkv_ref[0])
        akr_scr[...] += dk[:, LANE:2 * LANE]

        @pl.when(h == H - 1)
        def _():
            def rms_bwd(c_ref, g_ref, acc_ref, d_ref, dg_ref):
                xh, rs = _rms_stats(c_ref[...])
                dn = acc_ref[...]
                dxh = dn * g_ref[...]
                d_ref[...] = (rs * (dxh - xh * jnp.mean(dxh * xh, axis=-1, keepdims=True))).astype(d_ref.dtype)
                part = _rowsum8(dn * xh)

                @pl.when(i == 0)
                def _():
                    dg_ref[...] = part

                @pl.when(i > 0)
                def _():
                    dg_ref[...] += part

            rms_bwd(cq_ref, gq_ref, aq_scr, dcq_ref, dgq_ref)
            rms_bwd(ckv_ref, gkv_ref, akv_scr, dckv_ref, dgkv_ref)
            dkr = akr_scr[...]
            dkr_ref[...] = jnp.concatenate([dkr * cfv, dkr * sfv, jnp.zeros((tm, KR_PAD - 2 * LANE), F32)], axis=1).astype(dkr_ref.dtype)

    cspec = lambda off, w: pl.BlockSpec((tm, w), lambda i, h, o=off // w: (i, o))
    hspec = lambda w: pl.BlockSpec((1, tm, w), lambda i, h: (h, i, 0))
    outs = pl.pallas_call(
        body,
        name="mla_up_bwd",
        grid=(S // tm, H),
        in_specs=[
            hspec(2 * LANE), hspec(2 * LANE), hspec(LANE),
            cspec(OFF_CQ, MLA_RANK), cspec(OFF_CKV, MLA_RANK),
            pl.BlockSpec((tm, LANE), lambda i, h: (i, 0)), pl.BlockSpec((tm, LANE), lambda i, h: (i, 0)),
            pl.BlockSpec((1, MLA_RANK), lambda i, h: (0, 0)), pl.BlockSpec((1, MLA_RANK), lambda i, h: (0, 0)),
            pl.BlockSpec((1, MLA_RANK, 3 * LANE), lambda i, h: (h, 0, 0)),
            pl.BlockSpec((1, MLA_RANK, 2 * LANE), lambda i, h: (h, 0, 0)),
        ],
        out_specs=[
            hspec(3 * LANE), hspec(2 * LANE),
            pl.BlockSpec((tm, MLA_RANK), lambda i, h: (i, 0)),
            pl.BlockSpec((tm, MLA_RANK), lambda i, h: (i, 0)),
            pl.BlockSpec((tm, KR_PAD), lambda i, h: (i, 0)),
            pl.BlockSpec((SUB, MLA_RANK), lambda i, h: (0, 0)),
            pl.BlockSpec((SUB, MLA_RANK), lambda i, h: (0, 0)),
        ],
        out_shape=[
            jax.ShapeDtypeStruct((H, S, 3 * LANE), CDT), jax.ShapeDtypeStruct((H, S, 2 * LANE), CDT),
            jax.ShapeDtypeStruct((S, MLA_RANK), CDT), jax.ShapeDtypeStruct((S, MLA_RANK), CDT),
            jax.ShapeDtypeStruct((S, KR_PAD), CDT),
            jax.ShapeDtypeStruct((SUB, MLA_RANK), F32), jax.ShapeDtypeStruct((SUB, MLA_RANK), F32),
        ],
        scratch_shapes=[pltpu.VMEM((tm, MLA_RANK), F32), pltpu.VMEM((tm, MLA_RANK), F32), pltpu.VMEM((tm, LANE), F32)],
        compiler_params=_cparams(2),
    )(dq_cat, dk_cat, dv, proj, proj, cf, sf, g_cq, g_ckv, wuq_p, wukv)
    dqp, dkvp, dcq, dckv, dkr, dgq, dgkv = outs
    dproj = _copy_into("dproj_cq", dcq, dproj, OFF_CQ)
    dproj = _copy_into("dproj_ckv", dckv, dproj, OFF_CKV)
    dproj = _copy_into("dproj_kr", dkr, dproj, kr_off)
    return dproj, dqp, dkvp, dgq, dgkv


def _heads_tn(name, a, b):
    S, Ka = a.shape
    H, _, W = b.shape
    tk = _rtile(S, 1024)
    nk = S // tk

    def body(a_ref, b_ref, o_ref, acc):
        k = pl.program_id(1)

        @pl.when(k == 0)
        def _():
            acc[...] = jnp.zeros_like(acc)

        acc[...] += _tn(a_ref[...], b_ref[0])

        @pl.when(k == nk - 1)
        def _():
            o_ref[0] = acc[...].astype(o_ref.dtype)

    return pl.pallas_call(
        body,
        name=name,
        grid=(H, nk),
        in_specs=[pl.BlockSpec((tk, Ka), lambda h, k: (k, 0)), pl.BlockSpec((1, tk, W), lambda h, k: (h, k, 0))],
        out_specs=pl.BlockSpec((1, Ka, W), lambda h, k: (h, 0, 0)),
        out_shape=jax.ShapeDtypeStruct((H, Ka, W), CDT),
        scratch_shapes=[pltpu.VMEM((Ka, W), F32)],
        compiler_params=_cparams(2),
    )(a, b)


def _mem_softmax(q, k):
    s = _nt(q, k) * (MEM_HD ** -0.5)
    p = jnp.exp(s - jnp.max(s, axis=1, keepdims=True))
    return p / jnp.sum(p, axis=1, keepdims=True)


def _mem_attn_fwd(proj, memkv):
    S = proj.shape[0]
    Mm = memkv.shape[0]
    tm = _rtile(S, 512)

    def body(q_ref, k_ref, v_ref, y_ref):
        pn = _mem_softmax(q_ref[...], k_ref[...])
        y_ref[...] = _nn(pn, v_ref[...]).astype(y_ref.dtype)

    return pl.pallas_call(
        body,
        name="mem_attn_fwd",
        grid=(S // tm, MEM_HEADS),
        in_specs=[
            pl.BlockSpec((tm, MEM_HD), lambda i, h: (i, OFF_QM // MEM_HD + h)),
            pl.BlockSpec((Mm, MEM_HD), lambda i, h: (0, h)),
            pl.BlockSpec((Mm, MEM_HD), lambda i, h: (0, MEM_HEADS + h)),
        ],
        out_specs=pl.BlockSpec((tm, MEM_HD), lambda i, h: (i, h)),
        out_shape=jax.ShapeDtypeStruct((S, MEM_W), CDT),
        compiler_params=_cparams(2),
    )(proj, memkv, memkv)


def _mem_attn_bwd(dy, proj, memkv, dproj):
    S = proj.shape[0]
    Mm = memkv.shape[0]
    tm = _rtile(S, 512)
    scale = MEM_HD ** -0.5

    def body(dy_ref, q_ref, k_ref, v_ref, _dp, dq_ref, dk_ref, dv_ref):
        i = pl.program_id(1)
        q, k, dyv = q_ref[...].astype(CDT), k_ref[...], dy_ref[...]
        pn = _mem_softmax(q, k)
        dvp = _tn(pn, dyv)
        dp = _nt(dyv, v_ref[...])
        ds = pn * (dp - jnp.sum(dp * pn, axis=1, keepdims=True)) * scale
        dq_ref[...] = _nn(ds, k).astype(dq_ref.dtype)
        dkp = _tn(ds, q)

        @pl.when(i == 0)
        def _():
            dk_ref[...] = dkp
            dv_ref[...] = dvp

        @pl.when(i > 0)
        def _():
            dk_ref[...] += dkp
            dv_ref[...] += dvp

    dproj, dk, dv = pl.pallas_call(
        body,
        name="mem_attn_bwd",
        grid=(MEM_HEADS, S // tm),
        in_specs=[
            pl.BlockSpec((tm, MEM_HD), lambda h, i: (i, h)),
            pl.BlockSpec((tm, MEM_HD), lambda h, i: (i, OFF_QM // MEM_HD + h)),
            pl.BlockSpec((Mm, MEM_HD), lambda h, i: (0, h)),
            pl.BlockSpec((Mm, MEM_HD), lambda h, i: (0, MEM_HEADS + h)),
            pl.BlockSpec(memory_space=pl.ANY),
        ],
        out_specs=[
            pl.BlockSpec((tm, MEM_HD), lambda h, i: (i, OFF_QM // MEM_HD + h)),
            pl.BlockSpec((Mm, MEM_HD), lambda h, i: (0, h)),
            pl.BlockSpec((Mm, MEM_HD), lambda h, i: (0, h)),
        ],
        out_shape=[
            jax.ShapeDtypeStruct(dproj.shape, dproj.dtype),
            jax.ShapeDtypeStruct((Mm, MEM_W), F32),
            jax.ShapeDtypeStruct((Mm, MEM_W), F32),
        ],
        input_output_aliases={4: 0},
        compiler_params=_cparams(2),
    )(dy, proj, memkv, memkv, dproj)
    return dproj, dk, dv


def _small_allreduce(vec):
    NS = vec.shape[1]

    def body(v_ref, o_ref, gbuf, send, recv):
        x, y, c, me = _my_place()
        gbuf[me] = v_ref[...]
        copies = []
        for kk in range(1, N_DEV):
            peer, _ = _peer(x, y, c, kk)
            cp = pltpu.make_async_remote_copy(src_ref=v_ref, dst_ref=gbuf.at[me], send_sem=send.at[kk - 1],
                                              recv_sem=recv.at[kk - 1], device_id=peer, device_id_type=MESH)
            cp.start()
            copies.append(cp)
        for cp in copies:
            cp.wait()
        tot = gbuf[0]
        for d in range(1, N_DEV):
            tot = tot + gbuf[d]
        o_ref[...] = jnp.sum(tot, axis=0, keepdims=True)

    return pl.pallas_call(
        body,
        name="small_allreduce",
        in_specs=[pl.BlockSpec(memory_space=pltpu.VMEM)],
        out_specs=pl.BlockSpec(memory_space=pltpu.VMEM),
        out_shape=jax.ShapeDtypeStruct((1, NS), F32),
        scratch_shapes=[pltpu.VMEM((N_DEV, SUB, NS), F32), pltpu.SemaphoreType.DMA((N_DEV - 1,)),
                        pltpu.SemaphoreType.DMA((N_DEV - 1,))],
        compiler_params=pltpu.CompilerParams(has_side_effects=True, vmem_limit_bytes=V7X_VMEM_LIMIT),
    )(vec)


def _adamw_math(g, w, m, v):
    nm = ADAM_B1 * m + (1.0 - ADAM_B1) * g
    nv = ADAM_B2 * v + (1.0 - ADAM_B2) * (g * g)
    mh = nm / (1.0 - ADAM_B1 ** ADAM_STEP)
    vh = nv / (1.0 - ADAM_B2 ** ADAM_STEP)
    delta = -ADAM_LR * (mh / (jnp.sqrt(vh) + ADAM_EPS) + ADAM_WD * w)
    return delta, nm, nv


def _adam_big(name, recv, w, m, v):
    R, C = w.shape
    tr = _rtile(R, max(SUB, (65536 // C) // SUB * SUB))

    def body(r_ref, w_ref, m_ref, v_ref, g_ref, d_ref, nm_ref, nv_ref):
        g = r_ref[0].astype(F32)
        for d in range(1, N_DEV):
            g = g + r_ref[d].astype(F32)
        delta, nm, nv = _adamw_math(g, w_ref[...], m_ref[...], v_ref[...])
        g_ref[...] = g
        d_ref[...] = delta
        nm_ref[...] = nm
        nv_ref[...] = nv

    blk = pl.BlockSpec((tr, C), lambda i: (i, 0))
    return pl.pallas_call(
        body,
        name=name,
        grid=(R // tr,),
        in_specs=[pl.BlockSpec((N_DEV, tr, C), lambda i: (0, i, 0)), blk, blk, blk],
        out_specs=[blk, blk, blk, blk],
        out_shape=[jax.ShapeDtypeStruct((R, C), F32)] * 4,
        compiler_params=_cparams(1),
    )(recv, w, m, v)


def _adam_small(g, w, m, v):
    def body(g_ref, w_ref, m_ref, v_ref, d_ref, nm_ref, nv_ref):
        delta, nm, nv = _adamw_math(g_ref[...], w_ref[...], m_ref[...], v_ref[...])
        d_ref[...] = delta
        nm_ref[...] = nm
        nv_ref[...] = nv

    return pl.pallas_call(body, name="adam_small", out_shape=[jax.ShapeDtypeStruct(g.shape, F32)] * 3)(g, w, m, v)


def _rot(w):
    h = w.shape[-1] // 2
    return jnp.concatenate([-w[..., h:], w[..., :h]], axis=-1)


def _unrot(dw):
    h = dw.shape[-1] // 2
    return jnp.concatenate([dw[..., h:], -dw[..., :h]], axis=-1)


def _pad_cols(w, width):
    return jnp.pad(w, [(0, 0)] * (w.ndim - 1) + [(0, width - w.shape[-1])])


def kernel(x, mem, positions, ln_emb_g, ln_emb_b, hgrn_lb_logits, w_in, hgrn_norm_g, mla_g_cq, mla_g_ckv, mla_w_uq, mla_w_ukv, mem_w_kv, w_branch, w_o, ln1_g, ln1_b, w_ffn_gate, w_ffn_up, w_ffn_down, ln2_g, ln2_b, loss_target, m_ln_emb_g, m_ln_emb_b, m_hgrn_lb_logits, m_w_in, m_hgrn_norm_g, m_mla_g_cq, m_mla_g_ckv, m_mla_w_uq, m_mla_w_ukv, m_mem_w_kv, m_w_branch, m_w_o, m_ln1_g, m_ln1_b, m_w_ffn_gate, m_w_ffn_up, m_w_ffn_down, m_ln2_g, m_ln2_b, v_ln_emb_g, v_ln_emb_b, v_hgrn_lb_logits, v_w_in, v_hgrn_norm_g, v_mla_g_cq, v_mla_g_ckv, v_mla_w_uq, v_mla_w_ukv, v_mem_w_kv, v_w_branch, v_w_o, v_ln1_g, v_ln1_b, v_w_ffn_gate, v_w_ffn_up, v_w_ffn_down, v_ln2_g, v_ln2_b):
    x2, tgt = x[0], loss_target[0]
    S, D = x2.shape
    Mm = mem.shape[1]
    F = w_ffn_gate.shape[2] * N_DEV
    GW = 3 * D
    PW = OFF_GATE + GW + KR_PAD
    KR = OFF_GATE + GW
    NIN = w_in.shape[2] * N_DEV
    assert NIN == OFF_GATE + MLA_ROPE + GW
    _, _, _, me = _my_place()
    row = lambda a: a.reshape(1, -1)

    big_w = [w_in[0], mla_w_uq[0], mla_w_ukv[0], mem_w_kv[0], w_branch[0].reshape(3 * BR_W, -1), w_o[0],
             w_ffn_gate[0], w_ffn_up[0], w_ffn_down[0]]
    big_m = [m_w_in[0], m_mla_w_uq[0], m_mla_w_ukv[0], m_mem_w_kv[0], m_w_branch[0].reshape(3 * BR_W, -1), m_w_o[0],
             m_w_ffn_gate[0], m_w_ffn_up[0], m_w_ffn_down[0]]
    big_v = [v_w_in[0], v_mla_w_uq[0], v_mla_w_ukv[0], v_mem_w_kv[0], v_w_branch[0].reshape(3 * BR_W, -1), v_w_o[0],
             v_w_ffn_gate[0], v_w_ffn_up[0], v_w_ffn_down[0]]
    big_wb = [w.astype(CDT) for w in big_w]
    g_in, g_lb = _exchange("weights_all_gather", [big_wb[0], hgrn_lb_logits.reshape(4, -1)], False)
    cols = lambda g: jnp.transpose(g, (1, 0, 2)).reshape(g.shape[1], -1)
    win = cols(g_in)
    kr_w = win[:, OFF_QM:OFF_QM + MLA_ROPE]
    zeros64 = jnp.zeros_like(kr_w)
    win_p = jnp.concatenate([win[:, :OFF_QM], win[:, OFF_QM + MLA_ROPE:], kr_w, zeros64, _rot(kr_w), zeros64,
                             jnp.zeros((D, KR_PAD - 2 * LANE), CDT)], axis=1)
    lbl4 = jnp.transpose(g_lb, (1, 0, 2)).reshape(4, -1)

    half = MLA_ROPE // 2
    inv_freq = jnp.power(ROPE_THETA, -jnp.arange(half, dtype=F32) / half)
    ang = positions[0].astype(F32)[:, None] * inv_freq
    cf = _pad_cols(jnp.tile(jnp.cos(ang), (1, 2)), LANE)
    sf = _pad_cols(jnp.tile(jnp.sin(ang), (1, 2)), LANE)

    tm512 = _rtile(S, 512)
    ident = lambda accs, tiles, rows: ([accs[0]], [])

    def epi_ln0(accs, tiles, rows):
        h = _ln_stats(tiles[0])[0] * rows[0] + rows[1]
        return [h, h], []

    h0, h0b = _fused_mm("ln_emb_fwd", "nn", [], S, D, 1, tm512, D, 1, [(D, F32, 0, None), (D, CDT, 0, None)], epi_ln0,
                        tiles=[(x2, 0)], rows=[(row(ln_emb_g), 0), (row(ln_emb_b), 0)])
    proj, g_uq, g_ukv, g_mkv, g_wb, g_wo = _fused_mm(
        "proj", "nn", [[(h0b, 0, win_p, 0)]], S, PW, D, _rtile(S, 1024), _tile(PW, 512), D, [(PW, F32, 0, None)], ident,
        xchg=_Xchg(big_wb[1:6], False))
    wuq_p = jnp.concatenate([g_uq[..., :MLA_NOPE], _pad_cols(g_uq[..., MLA_NOPE:], LANE),
                             _pad_cols(_rot(g_uq[..., MLA_NOPE:]), LANE)], axis=-1)
    wukv = g_ukv
    wmkv = g_mkv.reshape(-1, g_mkv.shape[-1])
    wb = jnp.transpose(g_wb.reshape(N_DEV, 3, BR_W, -1), (1, 2, 0, 3)).reshape(3, BR_W, D)
    wo = g_wo.reshape(-1, D)
    o_f, st_f = _gla_fwd(proj, lbl4, OFF_FF, False, "gla_fwd_f")
    o_b, st_b = _gla_fwd(proj, lbl4, OFF_FB, True, "gla_fwd_b")
    y_hg = _hgrn_post_fwd(o_f, o_b, proj, hgrn_norm_g)
    q_cat, k_cat, v_mla, vt_mla, cqn, ckvn = _mla_up(proj, cf, sf, mla_g_cq, mla_g_ckv, wuq_p, wukv)
    y_mla, ot, lse, g_wg, g_wu, g_wd = _mla_attn_fwd(q_cat, k_cat, vt_mla, _Xchg(big_wb[6:9], False))
    wg, wu = cols(g_wg), cols(g_wu)
    wd = g_wd.reshape(-1, D)
    memb = mem[0].astype(CDT)
    (memkv,) = _fused_mm("mem_kv", "nn", [[(memb, 0, wmkv, 0)]], Mm, 2 * MEM_W, D, Mm, _tile(2 * MEM_W, 512), D,
                         [(2 * MEM_W, CDT, 0, None)], ident)
    y_mem = _mem_attn_fwd(proj, memkv)
    ys = [y_hg, y_mla, y_mem]
    tnD = _tile(D, 512, OFF_GATE)

    def epi_branch(accs, tiles, rows):
        return [_sigmoid(tiles[0]) * accs[0] + _sigmoid(tiles[1]) * accs[1] + _sigmoid(tiles[2]) * accs[2]], []

    (merged,) = _fused_mm("branch_fwd", "nn", [[(ys[b], 0, wb[b], 0)] for b in range(3)], S, D, BR_W, tm512, tnD, BR_W,
                          [(D, CDT, 0, None)], epi_branch, tiles=[(proj, OFF_GATE + b * D) for b in range(3)])

    def epi_ln1(accs, tiles, rows):
        r1v = ALPHA * tiles[0] + accs[0]
        return [r1v, _ln_stats(r1v)[0] * rows[0] + rows[1]], []

    r1, h1b = _fused_mm("wo_ln1", "nn", [[(merged, 0, wo, 0)]], S, D, D, tm512, D, _tile(D, 512),
                        [(D, F32, 0, None), (D, CDT, 0, None)], epi_ln1, tiles=[(h0, 0)], rows=[(ln1_g, 0), (ln1_b, 0)])
    tnF = _tile(F, 512)

    def epi_up(accs, tiles, rows):
        gp, up = accs
        return [gp, up, gp * _sigmoid(gp) * up], []

    gpb, upb, act = _fused_mm("ffn_up", "nn", [[(h1b, 0, wg, 0)], [(h1b, 0, wu, 0)]], S, F, D, tm512, tnF, D,
                              [(F, CDT, 0, None)] * 3, epi_up)

    def epi_down(accs, tiles, rows):
        g1, b1, g2, b2 = rows
        h1 = _ln_stats(tiles[0])[0] * g1 + b1
        xh2, rstd2 = _ln_stats(ALPHA * h1 + accs[0])
        diff = xh2 * g2 + b2 - tiles[1]
        dh2 = diff * (1.0 / D)
        dr2v = _ln_bwd(dh2, xh2, rstd2, g2)
        return [dr2v, dr2v], [dh2 * xh2, dh2, diff * diff * (0.5 / D)]

    dr2, dr2b, dg2, db2, lossp = _fused_mm(
        "ffn_down_loss", "nn", [[(act, 0, wd, 0)]], S, D, F, tm512, D, tnF, [(D, F32, 0, None), (D, CDT, 0, None)],
        epi_down, tiles=[(r1, 0), (tgt, 0)], rows=[(ln1_g, 0), (ln1_b, 0), (ln2_g, 0), (ln2_b, 0)], n_racc=3)

    def epi_dact(accs, tiles, rows):
        da, gp, up = accs[0], tiles[0].astype(F32), tiles[1].astype(F32)
        s = _sigmoid(gp)
        return [da * up * (s * (1.0 + gp * (1.0 - s))), da * (gp * s)], []

    dgp, dup = _fused_mm("ffn_dact", "nt", [[(dr2b, 0, wd, 0)]], S, F, D, tm512, tnF, D, [(F, CDT, 0, None)] * 2,
                         epi_dact, tiles=[(gpb, 0), (upb, 0)])
    tkS = _rtile(S, 1024)
    (d_wd,) = _fused_mm("dw_down", "tn", [[(act, 0, dr2b, 0)]], F, D, S, tnF, D, tkS, [(D, CDT, 0, None)], ident)
    d_wg, d_wu = _fused_mm("dw_gate_up", "tn", [[(h1b, 0, dgp, 0)], [(h1b, 0, dup, 0)]], D, F, S, _tile(D, 1024), tnF, tkS,
                           [(F, CDT, 0, None)] * 2, lambda accs, tiles, rows: (accs, []))

    def epi_dh1(accs, tiles, rows):
        dh1 = accs[0] + ALPHA * tiles[0]
        xh1, rstd1 = _ln_stats(tiles[1])
        dr1v = _ln_bwd(dh1, xh1, rstd1, rows[0])
        return [dr1v, dr1v], [dh1 * xh1, dh1]

    uncols = lambda dw: jnp.transpose(dw.reshape(dw.shape[0], N_DEV, -1), (1, 0, 2))
    dr1, dr1b, dg1, db1, r_wg, r_wu, r_wd = _fused_mm(
        "dh1_ln1", "nt", [[(dgp, 0, wg, 0), (dup, 0, wu, 0)]], S, D, F, _rtile(S, 256), D, tnF,
        [(D, F32, 0, None), (D, CDT, 0, None)], epi_dh1, tiles=[(dr2, 0), (r1, 0)], rows=[(ln1_g, 0)], n_racc=2,
        xchg=_Xchg([uncols(d_wg), uncols(d_wu), d_wd.reshape(N_DEV, -1, D)], True))
    (dmerged,) = _fused_mm("dmerged", "nt", [[(dr1b, 0, wo, 0)]], S, D, D, tm512, _tile(D, 512), D, [(D, CDT, 0, None)], ident)
    (d_wo,) = _fused_mm("dw_o", "tn", [[(merged, 0, dr1b, 0)]], D, D, S, _tile(D, 512), D, tkS, [(D, CDT, 0, None)], ident)

    def epi_dbranch(accs, tiles, rows):
        dm, s = tiles[0].astype(F32), _sigmoid(tiles[1])
        return [dm * s, dm * accs[0] * s * (1.0 - s)], []

    dproj = None
    d_wbs, dys = [], []
    for b in range(3):
        du, dproj = _fused_mm(f"branch_bwd{b}", "nn", [[(ys[b], 0, wb[b], 0)]], S, D, BR_W, tm512, tnD, BR_W,
                              [(D, CDT, 0, None), (PW, CDT, OFF_GATE + b * D, dproj)], epi_dbranch,
                              tiles=[(dmerged, 0), (proj, OFF_GATE + b * D)])
        (dwb,) = _fused_mm(f"dw_branch{b}", "tn", [[(ys[b], 0, du, 0)]], BR_W, D, S, _tile(BR_W, 512), D, tkS,
                           [(D, CDT, 0, None)], ident)
        (dyb,) = _fused_mm(f"dy_branch{b}", "nt", [[(du, 0, wb[b], 0)]], S, BR_W, D, tm512, _tile(BR_W, 512), D,
                           [(BR_W, F32 if b == 0 else CDT, 0, None)], ident)
        d_wbs.append(dwb)
        dys.append(dyb)
    dy_hg, dy_mla, dy_mem = dys

    dproj, dk_mem, dv_mem = _mem_attn_bwd(dy_mem, proj, memkv, dproj)
    dkv_mem = jnp.concatenate([dk_mem, dv_mem], axis=1).astype(CDT)
    (d_wmkv,) = _fused_mm("dw_memkv", "tn", [[(memb, 0, dkv_mem, 0)]], D, 2 * MEM_W, Mm, _tile(D, 512), 2 * MEM_W, Mm,
                          [(2 * MEM_W, CDT, 0, None)], ident)

    delta = _mla_delta(dy_mla, ot)
    dk_cat, dv_h, dq_cat = _mla_attn_bwd(q_cat, k_cat, v_mla, dy_mla, lse, delta)
    dproj, dqp, dkvp, dgq, dgkv = _mla_up_bwd(dq_cat, dk_cat, dv_h, proj, cf, sf, mla_g_cq, mla_g_ckv, wuq_p, wukv, dproj)
    d_wuq_p = _heads_tn("dw_uq", cqn, dqp).astype(F32)
    d_wukv = _heads_tn("dw_ukv", ckvn, dkvp)
    d_wuq = jnp.concatenate([d_wuq_p[..., :MLA_NOPE],
                             d_wuq_p[..., LANE:LANE + MLA_ROPE] + _unrot(d_wuq_p[..., 2 * LANE:2 * LANE + MLA_ROPE])],
                            axis=-1).astype(CDT)

    do_hg, dproj, dng = _hgrn_post_bwd(dy_hg, o_f, o_b, proj, hgrn_norm_g, dproj)
    dproj, dq1, di1, dl_f = _gla_bwd(proj, lbl4, OFF_FF, False, do_hg, st_f, dproj, None, "gla_bwd_f")
    dproj, dq2, di2, dl_b = _gla_bwd(proj, lbl4, OFF_FB, True, do_hg, st_b, dproj, (dq1, di1), "gla_bwd_b")
    dproj = _copy_into("dproj_q", dq2, dproj, OFF_Q)
    dproj = _copy_into("dproj_i", di2, dproj, OFF_I)

    def epi_dh0(accs, tiles, rows):
        dh0 = accs[0] + ALPHA * tiles[0]
        xh, rstd = _ln_stats(tiles[1])
        return [_ln_bwd(dh0, xh, rstd, rows[0])], [dh0 * xh, dh0]

    d_wb = jnp.transpose(jnp.stack(d_wbs).reshape(3, BR_W, N_DEV, -1), (2, 0, 1, 3)).reshape(N_DEV, 3 * BR_W, -1)
    d_win_p, r_uq, r_ukv, r_mkv, r_wb, r_wo = _fused_mm(
        "dw_in", "tn", [[(h0b, 0, dproj, 0)]], D, PW, S, _tile(D, 1024), _tile(PW, 512), tkS, [(PW, CDT, 0, None)], ident,
        xchg=_Xchg([d_wuq, d_wukv, d_wmkv.reshape(N_DEV, -1, 2 * MEM_W), d_wb, d_wo.reshape(N_DEV, -1, D)], True))
    d_kr = (d_win_p[:, KR:KR + MLA_ROPE].astype(F32) + _unrot(d_win_p[:, KR + LANE:KR + LANE + MLA_ROPE].astype(F32))).astype(CDT)
    d_win = jnp.concatenate([d_win_p[:, :OFF_QM], d_kr, d_win_p[:, OFF_QM:KR]], axis=1)
    grad_x, dge, dbe, r_in = _fused_mm(
        "dh0_ln_emb", "nt", [[(dproj, 0, win_p, 0)]], S, D, PW, tm512, D, _tile(PW, 512), [(D, F32, 0, None)], epi_dh0,
        tiles=[(dr1, 0), (x2, 0)], rows=[(row(ln_emb_g), 0)], n_racc=2, xchg=_Xchg([uncols(d_win)], True))

    recv = [r_in, r_uq, r_ukv, r_mkv, r_wb, r_wo, r_wg, r_wu, r_wd]
    names = ["w_in", "w_uq", "w_ukv", "mem_w_kv", "w_branch", "w_o", "w_gate", "w_up", "w_down"]
    big_out = [_adam_big("adam_" + nme, r, w, m_, v_) for nme, r, w, m_, v_ in zip(names, recv, big_w, big_m, big_v)]

    parts = [dge, dbe, dng, dgq, dgkv, dg1, db1, dg2, db2, dl_f, dl_b, lossp]
    widths = [p.shape[1] for p in parts]
    red = _small_allreduce(jnp.concatenate(parts, axis=1))[0]
    offs = [sum(widths[:i]) for i in range(len(widths))]
    rs = [red[o:o + w_] for o, w_ in zip(offs, widths)]
    g_le_g, g_le_b, g_ng, g_gq, g_gkv, g_l1g, g_l1b, g_l2g, g_l2b, g_dlf, g_dlb, g_loss = rs
    loss = jnp.sum(g_loss)
    g_ng = g_ng.reshape(HG_HEADS, HG_DK).sum(axis=0)
    dl0 = jnp.stack([g_dlf, g_dlb])
    g_lb_full = jnp.stack([dl0, -dl0], axis=1)
    lbw = hgrn_lb_logits.shape[2]
    g_lb = lax.dynamic_slice_in_dim(g_lb_full, me * lbw, lbw, axis=2)

    small_g = [g_le_g, g_le_b, g_lb, g_ng.reshape(1, -1), g_gq.reshape(1, -1), g_gkv.reshape(1, -1), g_l1g.reshape(1, -1),
               g_l1b.reshape(1, -1), g_l2g.reshape(1, -1), g_l2b.reshape(1, -1)]
    small_w = [ln_emb_g, ln_emb_b, hgrn_lb_logits, hgrn_norm_g, mla_g_cq, mla_g_ckv, ln1_g, ln1_b, ln2_g, ln2_b]
    small_m = [m_ln_emb_g, m_ln_emb_b, m_hgrn_lb_logits, m_hgrn_norm_g, m_mla_g_cq, m_mla_g_ckv, m_ln1_g, m_ln1_b, m_ln2_g, m_ln2_b]
    small_v = [v_ln_emb_g, v_ln_emb_b, v_hgrn_lb_logits, v_hgrn_norm_g, v_mla_g_cq, v_mla_g_ckv, v_ln1_g, v_ln1_b, v_ln2_g, v_ln2_b]
    small_g = [g.reshape(w.shape) for g, w in zip(small_g, small_w)]
    pack = lambda lst: jnp.concatenate([a.reshape(-1) for a in lst]).reshape(1, -1)
    s_delta, s_nm, s_nv = _adam_small(pack(small_g), pack(small_w), pack(small_m), pack(small_v))
    sizes = [w.size for w in small_w]
    soffs = [sum(sizes[:i]) for i in range(len(sizes))]
    unpack = lambda p: [p[0, o:o + n].reshape(w.shape) for o, n, w in zip(soffs, sizes, small_w)]
    s_delta, s_nm, s_nv = unpack(s_delta), unpack(s_nm), unpack(s_nv)

    def ordered(small, big):
        sm = list(small)
        bg = [b.reshape(w.shape) for b, w in zip(big, [w_in, mla_w_uq, mla_w_ukv, mem_w_kv, w_branch, w_o, w_ffn_gate, w_ffn_up, w_ffn_down])]
        return [sm[0], sm[1], sm[2], bg[0], sm[3], sm[4], sm[5], bg[1], bg[2], bg[3], bg[4], bg[5], sm[6], sm[7], bg[6], bg[7], bg[8], sm[8], sm[9]]

    grads = ordered(small_g, [o[0] for o in big_out])
    deltas = ordered(s_delta, [o[1] for o in big_out])
    new_m = ordered(s_nm, [o[2] for o in big_out])
    new_v = ordered(s_nv, [o[3] for o in big_out])
    return (loss, grad_x[None], *grads, *deltas, *new_m, *new_v)
```

```python
import functools

import jax
import jax.numpy as jnp
from jax import lax
from jax.experimental import pallas as pl
from jax.experimental.pallas import tpu as pltpu

F32 = jnp.float32
CDT = jnp.bfloat16
MESH = pl.DeviceIdType.MESH
N_DEV = 8
V7X_VMEM_LIMIT = 56 * 1024 * 1024
LANE = 128
SUB = 8

HG_HEADS, HG_DK, HG_CHUNK = 8, 128, 64
HG_HPS = 8
HG_W = HG_HEADS * HG_DK
MLA_HEADS, MLA_RANK, MLA_NOPE, MLA_ROPE, MLA_V = 8, 512, 128, 64, 128
MLA_QK = MLA_NOPE + MLA_ROPE
MLA_SCALE = MLA_QK ** -0.5
MLA_QSCALE = MLA_SCALE * 1.4426950408889634
VT_ROWS = LANE + 16
MLA_TQ = 1024
MLA_W = MLA_HEADS * MLA_V
MEM_HEADS, MEM_HD = 4, 256
MEM_W = MEM_HEADS * MEM_HD
BR_W = 1024
ROPE_THETA = 10000.0
ALPHA = 2.0 ** 0.25
LN_EPS = 1e-5
RMS_EPS = 1e-6
ADAM_LR, ADAM_B1, ADAM_B2, ADAM_EPS, ADAM_WD, ADAM_STEP = 0.001, 0.9, 0.999, 1e-08, 0.01, 10

OFF_Q, OFF_I, OFF_FF, OFF_FB, OFF_G = 0, 1024, 2048, 3072, 4096
OFF_CQ, OFF_CKV, OFF_QM, OFF_GATE = 5120, 5632, 6144, 7168
KR_PAD = 512


def _cparams(n_grid, side_effects=False):
    return pltpu.CompilerParams(dimension_semantics=("arbitrary",) * n_grid, vmem_limit_bytes=V7X_VMEM_LIMIT,
                                has_side_effects=side_effects)


def _tile(n, pref, *offsets):
    if n <= pref and all(o % n == 0 for o in offsets):
        return n
    t = (min(pref, n) // LANE) * LANE
    while t >= LANE:
        if n % t == 0 and all(o % t == 0 for o in offsets):
            return t
        t -= LANE
    raise ValueError(f"no tile for {n} {pref} {offsets}")


def _rtile(n, pref):
    if n <= pref:
        return n
    t = (pref // SUB) * SUB
    while t >= SUB:
        if n % t == 0:
            return t
        t -= SUB
    raise ValueError(f"no row tile for {n} {pref}")


def _dot(a, b, dims):
    return lax.dot_general(a.astype(CDT), b.astype(CDT), (dims, ((), ())), preferred_element_type=F32)


def _nn(a, b):
    return _dot(a, b, ((1,), (0,)))


def _nt(a, b):
    return _dot(a, b, ((1,), (1,)))


def _tn(a, b):
    return _dot(a, b, ((0,), (0,)))


_DOTS = {"nn": _nn, "nt": _nt, "tn": _tn}


def _sigmoid(x):
    return 1.0 / (1.0 + jnp.exp(-x))


def _rowsum8(v):
    r, w = v.shape
    return v.reshape(r // SUB, SUB, w).sum(axis=0)


def _my_place():
    x, y, c = lax.axis_index("x"), lax.axis_index("y"), lax.axis_index("c")
    return x, y, c, 4 * x + 2 * y + c


def _peer(x, y, c, kk):
    px = 1 - x if kk & 4 else x
    py = 1 - y if kk & 2 else y
    pc = 1 - c if kk & 1 else c
    return (px, py, pc), 4 * px + 2 * py + pc


class _Xchg:
    def __init__(self, arrs, scatter):
        self.arrs, self.scatter, self.n = list(arrs), scatter, len(arrs)
        hbm = pl.BlockSpec(memory_space=pl.ANY)
        self.specs = [hbm] * self.n
        self.out_shape = [jax.ShapeDtypeStruct(((N_DEV,) + a.shape[1:]) if scatter else ((N_DEV,) + a.shape), a.dtype)
                          for a in self.arrs]
        ncp = self.n * (N_DEV - 1)
        self.scratch = [pltpu.SemaphoreType.DMA((ncp,)), pltpu.SemaphoreType.DMA((ncp,)), pltpu.SemaphoreType.DMA((self.n,))]

    def _copies(self, ins, outs, send, recv, loc):
        x, y, c, me = _my_place()
        copies = []
        for w in range(self.n):
            copies.append(pltpu.make_async_copy(ins[w].at[me] if self.scatter else ins[w], outs[w].at[me], loc.at[w]))
            for kk in range(1, N_DEV):
                peer, pid = _peer(x, y, c, kk)
                s = w * (N_DEV - 1) + kk - 1
                copies.append(pltpu.make_async_remote_copy(
                    src_ref=ins[w].at[pid] if self.scatter else ins[w], dst_ref=outs[w].at[me],
                    send_sem=send.at[s], recv_sem=recv.at[s], device_id=peer, device_id_type=MESH))
        return copies

    def start(self, ins, outs, sems):
        for cp in self._copies(ins, outs, *sems):
            cp.start()

    def wait(self, ins, outs, sems):
        for cp in self._copies(ins, outs, *sems):
            cp.wait()


def _exchange(name, arrs, scatter):
    xc = _Xchg(arrs, scatter)
    n = xc.n

    def body(*refs):
        ins, outs, sems = refs[:n], refs[n:2 * n], refs[2 * n:]
        xc.start(ins, outs, sems)
        xc.wait(ins, outs, sems)

    return pl.pallas_call(
        body,
        name=name,
        in_specs=xc.specs,
        out_specs=xc.specs,
        out_shape=xc.out_shape,
        scratch_shapes=xc.scratch,
        compiler_params=pltpu.CompilerParams(has_side_effects=True),
    )(*xc.arrs)


def _fused_mm(name, mode, groups, M, N, K, tm, tn, tk, outs, epi, tiles=(), rows=(), n_racc=0, xchg=None):
    ni, nj, nk = M // tm, N // tn, K // tk
    assert M % tm == 0 and N % tn == 0 and K % tk == 0, (name, M, N, K, tm, tn, tk)
    assert n_racc == 0 or nj == 1
    dot = _DOTS[mode] if groups else None
    ins, in_specs = [], []
    for g in groups:
        for a, a_off, b, b_off in g:
            if mode == "tn":
                assert a_off % tm == 0
                in_specs.append(pl.BlockSpec((tk, tm), lambda i, j, k, o=a_off // tm: (k, i + o)))
            else:
                assert a_off % tk == 0
                in_specs.append(pl.BlockSpec((tm, tk), lambda i, j, k, o=a_off // tk: (i, k + o)))
            ins.append(a)
            if mode == "nt":
                assert b_off % tk == 0
                in_specs.append(pl.BlockSpec((tn, tk), lambda i, j, k, o=b_off // tk: (j, k + o)))
            else:
                assert b_off % tn == 0
                in_specs.append(pl.BlockSpec((tk, tn), lambda i, j, k, o=b_off // tn: (k, j + o)))
            ins.append(b)
    for arr, off in tiles:
        assert off % tn == 0
        ins.append(arr)
        in_specs.append(pl.BlockSpec((tm, tn), lambda i, j, k, o=off // tn: (i, j + o)))
    for arr, off in rows:
        assert off % tn == 0
        ins.append(arr)
        in_specs.append(pl.BlockSpec((1, tn), lambda i, j, k, o=off // tn: (0, j + o)))
    aliases = {}
    out_shape, out_specs = [], []
    for oi, (width, dtype, off, alias) in enumerate(outs):
        assert off % tn == 0
        if alias is not None:
            aliases[len(ins)] = oi
            ins.append(alias)
            in_specs.append(pl.BlockSpec(memory_space=pl.ANY))
        out_shape.append(jax.ShapeDtypeStruct((M, width), dtype))
        out_specs.append(pl.BlockSpec((tm, tn), lambda i, j, k, o=off // tn: (i, j + o)))
    for _ in range(n_racc):
        out_shape.append(jax.ShapeDtypeStruct((SUB, N), F32))
        out_specs.append(pl.BlockSpec((SUB, tn), lambda i, j, k: (0, 0)))
    n_alias = len(aliases)
    n_pairs = [len(g) for g in groups]
    use_scratch = nk > 1
    scratch = [pltpu.VMEM((tm, tn), F32) for _ in groups] if use_scratch else []
    nx = 0
    if xchg is not None:
        nx = xchg.n
        ins += xchg.arrs
        in_specs += xchg.specs
        out_shape += xchg.out_shape
        out_specs += xchg.specs
        scratch += xchg.scratch

    def body(*refs):
        it = iter(refs)
        pair_refs = [[(next(it), next(it)) for _ in range(n)] for n in n_pairs]
        tile_refs = [next(it) for _ in tiles]
        row_refs = [next(it) for _ in rows]
        for _ in range(n_alias):
            next(it)
        x_in = [next(it) for _ in range(nx)]
        out_refs = [next(it) for _ in outs]
        racc_refs = [next(it) for _ in range(n_racc)]
        x_out = [next(it) for _ in range(nx)]
        acc_refs = [next(it) for _ in groups] if use_scratch else []
        x_sems = list(it)
        i, j, k = pl.program_id(0), pl.program_id(1), pl.program_id(2)
        if nx:
            @pl.when((i == 0) & (j == 0) & (k == 0))
            def _():
                xchg.start(x_in, x_out, x_sems)

        def products():
            res = []
            for prs in pair_refs:
                s = None
                for a_ref, b_ref in prs:
                    d = dot(a_ref[...], b_ref[...])
                    s = d if s is None else s + d
                res.append(s)
            return res

        def finish(accs):
            out_v, racc_v = epi(accs, [t[...] for t in tile_refs], [r[...] for r in row_refs])
            for o_ref, v in zip(out_refs, out_v):
                o_ref[...] = v.astype(o_ref.dtype)
            for r_ref, v in zip(racc_refs, racc_v):
                part = _rowsum8(v)

                @pl.when(i == 0)
                def _():
                    r_ref[...] = part

                @pl.when(i > 0)
                def _():
                    r_ref[...] += part

        if not use_scratch:
            finish(products())
        else:
            @pl.when(k == 0)
            def _():
                for acc in acc_refs:
                    acc[...] = jnp.zeros_like(acc)

            for acc, p in zip(acc_refs, products()):
                acc[...] += p

            @pl.when(k == nk - 1)
            def _():
                finish([acc[...] for acc in acc_refs])

        if nx:
            @pl.when((i == ni - 1) & (j == nj - 1) & (k == nk - 1))
            def _():
                xchg.wait(x_in, x_out, x_sems)

    res = pl.pallas_call(
        body,
        name=name,
        grid=(ni, nj, nk),
        in_specs=in_specs,
        out_specs=out_specs,
        out_shape=out_shape,
        scratch_shapes=scratch,
        input_output_aliases=aliases,
        compiler_params=_cparams(3, side_effects=nx > 0),
    )(*ins)
    return res


def _ln_stats(r):
    mu = jnp.mean(r, axis=-1, keepdims=True)
    xc = r - mu
    var = jnp.mean(xc * xc, axis=-1, keepdims=True)
    rstd = lax.rsqrt(var + LN_EPS)
    return xc * rstd, rstd


def _ln_bwd(dh, xhat, rstd, g):
    dxh = dh * g
    m1 = jnp.mean(dxh, axis=-1, keepdims=True)
    m2 = jnp.mean(dxh * xhat, axis=-1, keepdims=True)
    return rstd * (dxh - m1 - xhat * m2)


def _split3(x):
    hi = x.astype(CDT)
    r1 = x - hi.astype(F32)
    mid = r1.astype(CDT)
    lo = (r1 - mid.astype(F32)).astype(CDT)
    return hi, mid, lo


def _tri_matmul(tri, x):
    hi, mid, lo = _split3(x)
    return _nn(tri, hi) + _nn(tri, mid) + _nn(tri, lo)


def _dot3(dot, a, b):
    a_hi, b_hi = a.astype(CDT), b.astype(CDT)
    a_lo = (a - a_hi.astype(F32)).astype(CDT)
    b_lo = (b - b_hi.astype(F32)).astype(CDT)
    return dot(a_hi, b_hi) + dot(a_hi, b_lo) + dot(a_lo, b_hi)


def _gla_masks(reverse):
    C = HG_CHUNK
    r = lax.broadcasted_iota(jnp.int32, (C, C), 0)
    c = lax.broadcasted_iota(jnp.int32, (C, C), 1)
    keep = (c >= r) if reverse else (r >= c)
    return keep


def _m(fn, *lists):
    return [fn(*args) for args in zip(*lists)]


def _gla_chunk_fwd(qraw, fraw, lb, keep, reverse):
    C = HG_CHUNK
    end = 0 if reverse else C - 1
    tri = jnp.where(keep, 1.0, 0.0).astype(CDT)
    sq = _m(_sigmoid, qraw)
    q = _m(lambda x, s: x * s, qraw, sq)
    sg = _m(_sigmoid, fraw)
    f = _m(lambda l_, s: l_ + (1.0 - l_) * s, lb, sg)
    k = _m(lambda x: 1.0 - x, f)
    g = _m(jnp.log, f)
    b = _m(lambda x: _tri_matmul(tri, x), g)
    b_end = _m(lambda x: x[end:end + 1, :], b)
    b_mid = _m(lambda x: x[C // 2:C // 2 + 1, :], b)
    eq = _m(lambda x, m_: jnp.exp(x - m_), b, b_mid)
    ek = _m(lambda x, m_: jnp.exp(m_ - x), b, b_mid)
    eb = _m(jnp.exp, b)
    e2 = _m(lambda x, e_: jnp.exp(e_ - x), b, b_end)
    e_end = _m(jnp.exp, b_end)
    qt = _m(lambda x, e_: x * e_, q, eq)
    kt = _m(lambda x, e_: x * e_, k, ek)
    qs = _m(lambda x, e_: (x * e_).astype(CDT), q, eb)
    k2 = _m(lambda x, e_: (x * e_).astype(CDT), k, e2)
    a = _m(lambda x, y: jnp.where(keep, _dot3(_nt, x, y), 0.0).astype(CDT), qt, kt)
    return dict(sq=sq, q=q, sg=sg, f=f, k=k, eq=eq, ek=ek, eb=eb, e2=e2, e_end=e_end, qt=qt, kt=kt, qs=qs, k2=k2, a=a)


def _gla_fwd(proj, lbl4, f_off, reverse, name):
    S = proj.shape[0]
    C = HG_CHUNK
    R = _rtile(S, 512)
    cpb, nblk = R // C, S // R
    d = 1 if reverse else 0
    blk_map = (lambda b: nblk - 1 - b) if reverse else (lambda b: b)

    W = HG_HPS * HG_DK

    def body(q_ref, i_ref, f_ref, lb_ref, o_ref, st_ref, s_scr):
        @pl.when(pl.program_id(1) == 0)
        def _():
            s_scr[...] = jnp.zeros_like(s_scr)

        l = lb_ref[...]
        lbs = _sigmoid(l[2 * d:2 * d + 1, :] - l[2 * d + 1:2 * d + 2, :])
        keep = _gla_masks(reverse)
        heads = list(range(HG_HPS))
        css = [pl.ds(hh * HG_DK, HG_DK) for hh in heads]
        lb = [lbs[:, hh * HG_DK:(hh + 1) * HG_DK] for hh in heads]
        for cc in range(cpb):
            c = cpb - 1 - cc if reverse else cc
            sl = pl.ds(c * C, C)
            v = [i_ref[sl, cs] for cs in css]
            t = _gla_chunk_fwd([q_ref[sl, cs] for cs in css], [f_ref[sl, cs] for cs in css], lb, keep, reverse)
            st = [s_scr[hh] for hh in heads]
            o = _m(lambda qs, s_, a, v_: _nt(qs, s_) + _nn(a, v_), t["qs"], st, t["a"], v)
            new = _m(lambda e_, s_, v_, k2: e_ * s_ + _tn(v_, k2), t["e_end"], st, v, t["k2"])
            for hh in heads:
                st_ref[c, hh] = st[hh]
                o_ref[sl, css[hh]] = o[hh]
                s_scr[hh] = new[hh]

    col = lambda off: (lambda h, b: (blk_map(b), off // W + h))
    return pl.pallas_call(
        body,
        name=name,
        grid=(HG_HEADS // HG_HPS, nblk),
        in_specs=[
            pl.BlockSpec((R, W), col(OFF_Q)),
            pl.BlockSpec((R, W), col(OFF_I)),
            pl.BlockSpec((R, W), col(f_off)),
            pl.BlockSpec((4, W), lambda h, b: (0, h)),
        ],
        out_specs=[
            pl.BlockSpec((R, W), lambda h, b: (blk_map(b), h)),
            pl.BlockSpec((cpb, HG_HPS, HG_DK, HG_DK), lambda h, b: (blk_map(b), h, 0, 0)),
        ],
        out_shape=[
            jax.ShapeDtypeStruct((S, HG_W), F32),
            jax.ShapeDtypeStruct((S // C, HG_HEADS, HG_DK, HG_DK), F32),
        ],
        scratch_shapes=[pltpu.VMEM((HG_HPS, HG_DK, HG_DK), F32)],
        compiler_params=_cparams(2),
    )(proj, proj, proj, lbl4)


def _gla_bwd(proj, lbl4, f_off, reverse, do, states, dproj, prev, name):
    S = proj.shape[0]
    PW = proj.shape[1]
    C = HG_CHUNK
    R = _rtile(S, 512)
    cpb, nblk = R // C, S // R
    d = 1 if reverse else 0
    blk_map = (lambda b: b) if reverse else (lambda b: nblk - 1 - b)
    final = prev is not None

    def body(*refs):
        if final:
            q_ref, i_ref, f_ref, lb_ref, do_ref, st_ref, pq_ref, pi_ref, _dp, dq_ref, di_ref, df_ref, dl_ref, ds_scr = refs
        else:
            q_ref, i_ref, f_ref, lb_ref, do_ref, st_ref, dq_ref, di_ref, df_ref, dl_ref, ds_scr = refs
        blk = pl.program_id(1)

        @pl.when(blk == 0)
        def _():
            ds_scr[...] = jnp.zeros_like(ds_scr)
            dl_ref[...] = jnp.zeros_like(dl_ref)

        l = lb_ref[...]
        lbs = _sigmoid(l[2 * d:2 * d + 1, :] - l[2 * d + 1:2 * d + 2, :])
        keep = _gla_masks(reverse)
        keep_t = _gla_masks(not reverse)
        tri_t = jnp.where(keep_t, 1.0, 0.0).astype(CDT)
        end = 0 if reverse else C - 1
        is_end = lax.broadcasted_iota(jnp.int32, (C, HG_DK), 0) == end
        dl_all = [jnp.zeros((SUB, HG_DK), F32) for _ in range(HG_HPS)]
        for cc, heads in [(cc, [hh]) for cc in range(cpb) for hh in range(HG_HPS)]:
            css = [pl.ds(hh * HG_DK, HG_DK) for hh in heads]
            lb = [lbs[:, hh * HG_DK:(hh + 1) * HG_DK] for hh in heads]
            dl_acc = [dl_all[hh] for hh in heads]
            c = cc if reverse else cpb - 1 - cc
            sl = pl.ds(c * C, C)
            qraw = [q_ref[sl, cs] for cs in css]
            v = [i_ref[sl, cs] for cs in css]
            t = _gla_chunk_fwd(qraw, [f_ref[sl, cs] for cs in css], lb, keep, reverse)
            dob = [do_ref[sl, cs].astype(CDT) for cs in css]
            vb = _m(lambda x: x.astype(CDT), v)
            st = [st_ref[c, hh] for hh in heads]
            ds = [ds_scr[hh] for hh in heads]
            dsb = _m(lambda x: x.astype(CDT), ds)
            d_qs = _m(_nn, dob, st)
            d_a = _m(lambda x, y: jnp.where(keep, _nt(x, y), 0.0), dob, vb)
            d_qt = _m(lambda x, y: _dot3(_nn, x, y), d_a, t["kt"])
            d_kt = _m(lambda x, y: _dot3(_tn, x, y), d_a, t["qt"])
            d_v = _m(lambda a, x, k2, s_: _tn(a, x) + _nt(k2, s_), t["a"], dob, t["k2"], dsb)
            d_k2 = _m(_nn, vb, dsb)
            d_e = _m(lambda s_, x: jnp.sum(s_ * x, axis=0, keepdims=True), st, ds)
            new_ds = _m(lambda e_, x, y, qs: e_ * x + _tn(y, qs), t["e_end"], ds, dob, t["qs"])
            dq = _m(lambda a, ea, b_, eb_: a * ea + b_ * eb_, d_qt, t["eq"], d_qs, t["eb"])
            dk = _m(lambda a, ea, b_, eb_: a * ea + b_ * eb_, d_kt, t["ek"], d_k2, t["e2"])
            db_end = _m(lambda x, k_, e2, de, ee: jnp.sum(x * (k_ * e2), axis=0, keepdims=True) + de * ee,
                        d_k2, t["k"], t["e2"], d_e, t["e_end"])
            db = _m(lambda q_, dq_, k_, dk_, be: q_ * dq_ - k_ * dk_ + jnp.where(is_end, be, 0.0),
                    t["q"], dq, t["k"], dk, db_end)
            dg = _m(lambda x: _tri_matmul(tri_t, x), db)
            df = _m(lambda g_, f_, dk_: g_ / f_ - dk_, dg, t["f"], dk)
            dfraw = _m(lambda x, l_, s_: x * (1.0 - l_) * s_ * (1.0 - s_), df, lb, t["sg"])
            dl_acc = _m(lambda acc, x, s_: acc + _rowsum8(x * (1.0 - s_)), dl_acc, df, t["sg"])
            dqraw = _m(lambda x, s_, r: x * (s_ * (1.0 + r * (1.0 - s_))), dq, t["sq"], qraw)
            if final:
                dqraw = [x + pq_ref[sl, cs] for x, cs in zip(dqraw, css)]
                d_v = [x + pi_ref[sl, cs] for x, cs in zip(d_v, css)]
            for n, hh in enumerate(heads):
                dl_all[hh] = dl_acc[n]
                ds_scr[hh] = new_ds[n]
                dq_ref[sl, css[n]] = dqraw[n].astype(dq_ref.dtype)
                di_ref[sl, css[n]] = d_v[n].astype(di_ref.dtype)
                df_ref[sl, css[n]] = dfraw[n].astype(df_ref.dtype)
        dl_ref[...] += jnp.concatenate(dl_all, axis=1) * (lbs * (1.0 - lbs))

    W = HG_HPS * HG_DK
    col = lambda off: (lambda h, b: (blk_map(b), off // W + h))
    blk = lambda: pl.BlockSpec((R, W), lambda h, b: (blk_map(b), h))
    ins = [proj, proj, proj, lbl4, do, states]
    in_specs = [
        pl.BlockSpec((R, W), col(OFF_Q)),
        pl.BlockSpec((R, W), col(OFF_I)),
        pl.BlockSpec((R, W), col(f_off)),
        pl.BlockSpec((4, W), lambda h, b: (0, h)),
        blk(),
        pl.BlockSpec((cpb, HG_HPS, HG_DK, HG_DK), lambda h, b: (blk_map(b), h, 0, 0)),
    ]
    dl_shape = jax.ShapeDtypeStruct((SUB, HG_W), F32)
    dl_spec = pl.BlockSpec((SUB, W), lambda h, b: (0, h))
    dp_shape = jax.ShapeDtypeStruct((S, PW), CDT)
    if final:
        ins += [prev[0], prev[1], dproj]
        in_specs += [blk(), blk(), pl.BlockSpec(memory_space=pl.ANY)]
        out_shape = [jax.ShapeDtypeStruct((S, HG_W), CDT), jax.ShapeDtypeStruct((S, HG_W), CDT), dp_shape, dl_shape]
        out_specs = [blk(), blk(), pl.BlockSpec((R, W), col(f_off)), dl_spec]
        aliases = {8: 2}
    else:
        out_shape = [jax.ShapeDtypeStruct((S, HG_W), F32), jax.ShapeDtypeStruct((S, HG_W), F32), dp_shape, dl_shape]
        out_specs = [blk(), blk(), pl.BlockSpec((R, W), col(f_off)), dl_spec]
        aliases = {}
        if dproj is not None:
            ins += [dproj]
            in_specs += [pl.BlockSpec(memory_space=pl.ANY)]
            aliases = {6: 2}
    if (not final) and dproj is not None:
        def body_wrapped(*refs, _b=body):
            _b(*refs[:6], *refs[7:])
        kern = body_wrapped
    else:
        kern = body
    dq, di, dproj, dl = pl.pallas_call(
        kern,
        name=name,
        grid=(HG_HEADS // HG_HPS, nblk),
        in_specs=in_specs,
        out_specs=out_specs,
        out_shape=out_shape,
        scratch_shapes=[pltpu.VMEM((HG_HPS, HG_DK, HG_DK), F32)],
        input_output_aliases=aliases,
        compiler_params=_cparams(2),
    )(*ins)
    return dproj, dq, di, dl


def _hgrn_post_fwd(o_f, o_b, proj, norm_g):
    S = o_f.shape[0]

    def epi(accs, tiles, rows):
        of, ob, graw = tiles
        ng = rows[0][:, :HG_DK]
        o = of + ob
        ys = []
        for h in range(HG_HEADS):
            oh = o[:, h * HG_DK:(h + 1) * HG_DK]
            rs = lax.rsqrt(jnp.mean(oh * oh, axis=-1, keepdims=True) + RMS_EPS)
            ys.append(oh * rs * ng * _sigmoid(graw[:, h * HG_DK:(h + 1) * HG_DK]))
        return [jnp.concatenate(ys, axis=1)], []

    tm = _rtile(S, 512)
    (y,) = _fused_mm("hgrn_post_fwd", "nn", [], S, HG_W, 1, tm, HG_W, 1, [(HG_W, CDT, 0, None)], epi,
                     tiles=[(o_f, 0), (o_b, 0), (proj, OFF_G)], rows=[(jnp.tile(norm_g, (1, HG_HEADS)), 0)])
    return y


def _hgrn_post_bwd(dy, o_f, o_b, proj, norm_g, dproj):
    S = o_f.shape[0]

    def epi(accs, tiles, rows):
        dyv, of, ob, graw = tiles
        ng = rows[0][:, :HG_DK]
        o = of + ob
        dos, dgs, dns = [], [], []
        for h in range(HG_HEADS):
            sl = slice(h * HG_DK, (h + 1) * HG_DK)
            oh, gh, dyh = o[:, sl], graw[:, sl], dyv[:, sl].astype(F32)
            rs = lax.rsqrt(jnp.mean(oh * oh, axis=-1, keepdims=True) + RMS_EPS)
            xh = oh * rs
            sg = _sigmoid(gh)
            dn = dyh * sg
            dgs.append(dyh * (xh * ng) * sg * (1.0 - sg))
            dns.append(dn * xh)
            dxh = dn * ng
            dos.append(rs * (dxh - xh * jnp.mean(dxh * xh, axis=-1, keepdims=True)))
        return [jnp.concatenate(dos, axis=1), jnp.concatenate(dgs, axis=1)], [jnp.concatenate(dns, axis=1)]

    tm = _rtile(S, 512)
    do, dproj, dn = _fused_mm("hgrn_post_bwd", "nn", [], S, HG_W, 1, tm, HG_W, 1,
                              [(HG_W, F32, 0, None), (dproj.shape[1], CDT, OFF_G, dproj)], epi,
                              tiles=[(dy, 0), (o_f, 0), (o_b, 0), (proj, OFF_G)],
                              rows=[(jnp.tile(norm_g, (1, HG_HEADS)), 0)], n_racc=1)
    return do, dproj, dn


def _copy_into(name, src, dst, off):
    S, W = src.shape
    tm = _rtile(S, 512)
    (dst,) = _fused_mm(name, "nn", [], S, W, 1, tm, W, 1, [(dst.shape[1], dst.dtype, off, dst)],
                       lambda accs, tiles, rows: ([tiles[0]], []), tiles=[(src, 0)])
    return dst


def _rms_stats(x):
    rs = lax.rsqrt(jnp.mean(x * x, axis=-1, keepdims=True) + RMS_EPS)
    return x * rs, rs


def _mla_up(proj, cf, sf, g_cq, g_ckv, wuq_p, wukv):
    S = proj.shape[0]
    tm = _rtile(S, 512)
    H = MLA_HEADS

    def body(cq_ref, ckv_ref, kr_ref, krot_ref, cf_ref, sf_ref, gq_ref, gkv_ref, wq_ref, wkv_ref,
             q_ref, k_ref, v_ref, vt_ref, cqn_ref, ckvn_ref):
        cqn = (_rms_stats(cq_ref[...])[0] * gq_ref[...]).astype(CDT)
        ckvn = (_rms_stats(ckv_ref[...])[0] * gkv_ref[...]).astype(CDT)
        cqn_ref[...] = cqn
        ckvn_ref[...] = ckvn
        cfv, sfv = cf_ref[...], sf_ref[...]
        r = _nn(cqn, wq_ref[0]) * MLA_QSCALE
        q_ref[0, :, 0:LANE] = r[:, 0:LANE].astype(CDT)
        q_ref[0, :, LANE:2 * LANE] = (r[:, LANE:2 * LANE] * cfv + r[:, 2 * LANE:3 * LANE] * sfv).astype(CDT)
        kv = _nn(ckvn, wkv_ref[0])
        k_ref[0, :, 0:LANE] = kv[:, 0:LANE].astype(CDT)
        k_ref[0, :, LANE:2 * LANE] = (kr_ref[...] * cfv + krot_ref[...] * sfv).astype(CDT)
        vv = kv[:, LANE:2 * LANE]
        v_ref[0] = vv.astype(CDT)
        vt_ref[0, 0, 0:LANE, :] = vv.T.astype(CDT)
        vt_ref[0, 0, LANE:VT_ROWS, :] = jnp.ones((VT_ROWS - LANE, tm), CDT)

    PWb = proj.shape[1]
    kr_off = PWb - KR_PAD
    cspec = lambda off, w: pl.BlockSpec((tm, w), lambda i, h, o=off // w: (i, o))
    return pl.pallas_call(
        body,
        name="mla_up_fwd",
        grid=(S // tm, H),
        in_specs=[
            cspec(OFF_CQ, MLA_RANK), cspec(OFF_CKV, MLA_RANK), cspec(kr_off, LANE), cspec(kr_off + LANE, LANE),
            pl.BlockSpec((tm, LANE), lambda i, h: (i, 0)), pl.BlockSpec((tm, LANE), lambda i, h: (i, 0)),
            pl.BlockSpec((1, MLA_RANK), lambda i, h: (0, 0)), pl.BlockSpec((1, MLA_RANK), lambda i, h: (0, 0)),
            pl.BlockSpec((1, MLA_RANK, 3 * LANE), lambda i, h: (h, 0, 0)),
            pl.BlockSpec((1, MLA_RANK, 2 * LANE), lambda i, h: (h, 0, 0)),
        ],
        out_specs=[
            pl.BlockSpec((1, tm, 2 * LANE), lambda i, h: (h, i, 0)),
            pl.BlockSpec((1, tm, 2 * LANE), lambda i, h: (h, i, 0)),
            pl.BlockSpec((1, tm, LANE), lambda i, h: (h, i, 0)),
            pl.BlockSpec((1, 1, VT_ROWS, tm), lambda i, h: (h, i, 0, 0)),
            pl.BlockSpec((tm, MLA_RANK), lambda i, h: (i, 0)),
            pl.BlockSpec((tm, MLA_RANK), lambda i, h: (i, 0)),
        ],
        out_shape=[
            jax.ShapeDtypeStruct((H, S, 2 * LANE), CDT), jax.ShapeDtypeStruct((H, S, 2 * LANE), CDT),
            jax.ShapeDtypeStruct((H, S, LANE), CDT), jax.ShapeDtypeStruct((H, S // tm, VT_ROWS, tm), CDT),
            jax.ShapeDtypeStruct((S, MLA_RANK), CDT), jax.ShapeDtypeStruct((S, MLA_RANK), CDT),
        ],
        compiler_params=_cparams(2),
    )(proj, proj, proj, proj, cf, sf, g_cq, g_ckv, wuq_p, wukv)


def _mla_attn_fwd(q_cat, k_cat, vt, xchg=None):
    H, S, _ = q_cat.shape
    tq = _tile(S, MLA_TQ)
    _, nkb, _, tk = vt.shape
    nq = S // tq
    nx = xchg.n if xchg is not None else 0

    def body(*refs):
        q_ref, k_ref, vt_ref = refs[:3]
        x_in = refs[3:3 + nx]
        y_ref, ot_ref, lse_ref = refs[3 + nx:6 + nx]
        x_out = refs[6 + nx:6 + 2 * nx]
        m_scr, acc_scr = refs[6 + 2 * nx:8 + 2 * nx]
        x_sems = refs[8 + 2 * nx:]
        h, i = pl.program_id(0), pl.program_id(1)
        if nx:
            @pl.when((h == 0) & (i == 0))
            def _():
                xchg.start(x_in, x_out, x_sems)

        q = q_ref[0]
        m_scr[...] = jnp.full_like(m_scr, -jnp.inf)
        acc_scr[...] = jnp.zeros_like(acc_scr)

        def step(j, carry):
            kj = k_ref[0, pl.ds(pl.multiple_of(j * tk, tk), tk), :]
            st = _nt(kj, q)
            m_old = m_scr[...]
            m_new = jnp.maximum(m_old, jnp.max(st, axis=0, keepdims=True))
            pt = jnp.exp2(st - m_new)
            acc_scr[...] = jnp.exp2(m_old - m_new) * acc_scr[...] + _nn(vt_ref[0, j], pt)
            m_scr[...] = m_new
            return carry

        lax.fori_loop(0, nkb, step, 0, unroll=2 if nkb % 2 == 0 else 1)
        l = acc_scr[LANE:LANE + 1, :]
        ot = acc_scr[0:LANE, :] / l
        ot_ref[0] = ot
        y_ref[...] = ot.T.astype(CDT)
        lse_ref[0, 0] = m_scr[...] + jnp.log2(l)

        if nx:
            @pl.when((h == H - 1) & (i == nq - 1))
            def _():
                xchg.wait(x_in, x_out, x_sems)

    return pl.pallas_call(
        body,
        name="mla_attn_fwd",
        grid=(H, nq),
        in_specs=[
            pl.BlockSpec((1, tq, 2 * LANE), lambda h, i: (h, i, 0)),
            pl.BlockSpec((1, S, 2 * LANE), lambda h, i: (h, 0, 0)),
            pl.BlockSpec((1, nkb, VT_ROWS, tk), lambda h, i: (h, 0, 0, 0)),
        ] + (xchg.specs if nx else []),
        out_specs=[
            pl.BlockSpec((tq, LANE), lambda h, i: (i, h)),
            pl.BlockSpec((1, LANE, tq), lambda h, i: (h, 0, i)),
            pl.BlockSpec((1, 1, 1, tq), lambda h, i: (h, i, 0, 0)),
        ] + (xchg.specs if nx else []),
        out_shape=[
            jax.ShapeDtypeStruct((S, H * LANE), CDT),
            jax.ShapeDtypeStruct((H, LANE, S), F32),
            jax.ShapeDtypeStruct((H, nq, 1, tq), F32),
        ] + (xchg.out_shape if nx else []),
        scratch_shapes=[pltpu.VMEM((1, tq), F32), pltpu.VMEM((VT_ROWS, tq), F32)] + (xchg.scratch if nx else []),
        compiler_params=_cparams(2, side_effects=nx > 0),
    )(q_cat, k_cat, vt, *(xchg.arrs if nx else []))


def _mla_delta(dy, ot):
    H, _, S = ot.shape
    tq = _tile(S, MLA_TQ)
    nq = S // tq

    def body(dy_ref, ot_ref, d_ref):
        d_ref[0, 0] = jnp.sum(dy_ref[...].astype(F32).T * ot_ref[0], axis=0, keepdims=True)

    return pl.pallas_call(
        body,
        name="mla_delta",
        grid=(H, nq),
        in_specs=[pl.BlockSpec((tq, LANE), lambda h, i: (i, h)), pl.BlockSpec((1, LANE, tq), lambda h, i: (h, 0, i))],
        out_specs=pl.BlockSpec((1, 1, 1, tq), lambda h, i: (h, i, 0, 0)),
        out_shape=jax.ShapeDtypeStruct((H, nq, 1, tq), F32),
        compiler_params=_cparams(2),
    )(dy, ot)


def _mla_attn_bwd(q_cat, k_cat, v, dy, lse, delta):
    H, S, _ = q_cat.shape
    _, nq, _, tq = lse.shape
    tk = _tile(S, 512)
    nkb = S // tk

    def body(k_ref, v_ref, q_ref, do_ref, lse_ref, dl_ref, dk_ref, dv_ref, dq_ref, dk_scr, dv_scr):
        ki = pl.program_id(1)

        @pl.when(ki == 0)
        def _():
            dq_ref[...] = jnp.zeros_like(dq_ref)

        kb, vb = k_ref[0], v_ref[0]
        dk_scr[...] = jnp.zeros_like(dk_scr)
        dv_scr[...] = jnp.zeros_like(dv_scr)

        def step(i, carry):
            rows = pl.ds(pl.multiple_of(i * tq, tq), tq)
            qc = q_ref[0, rows, :]
            doc = do_ref[rows, :]
            pt = jnp.exp2(_nt(kb, qc) - lse_ref[0, i])
            dv_scr[...] += _nn(pt, doc)
            dst = (pt * (_nt(vb, doc) - dl_ref[0, i])).astype(CDT)
            dk_scr[...] += _nn(dst, qc)
            dq_ref[0, rows, :] += _tn(dst, kb)
            return carry

        lax.fori_loop(0, nq, step, 0, unroll=2 if nq % 2 == 0 else 1)
        dk_ref[0] = dk_scr[...] * (MLA_SCALE / MLA_QSCALE)
        dv_ref[0] = dv_scr[...]

    return pl.pallas_call(
        body,
        name="mla_attn_bwd",
        grid=(H, nkb),
        in_specs=[
            pl.BlockSpec((1, tk, 2 * LANE), lambda h, j: (h, j, 0)),
            pl.BlockSpec((1, tk, LANE), lambda h, j: (h, j, 0)),
            pl.BlockSpec((1, S, 2 * LANE), lambda h, j: (h, 0, 0)),
            pl.BlockSpec((S, LANE), lambda h, j: (0, h)),
            pl.BlockSpec((1, nq, 1, tq), lambda h, j: (h, 0, 0, 0)),
            pl.BlockSpec((1, nq, 1, tq), lambda h, j: (h, 0, 0, 0)),
        ],
        out_specs=[
            pl.BlockSpec((1, tk, 2 * LANE), lambda h, j: (h, j, 0)),
            pl.BlockSpec((1, tk, LANE), lambda h, j: (h, j, 0)),
            pl.BlockSpec((1, S, 2 * LANE), lambda h, j: (h, 0, 0)),
        ],
        out_shape=[
            jax.ShapeDtypeStruct((H, S, 2 * LANE), F32),
            jax.ShapeDtypeStruct((H, S, LANE), F32),
            jax.ShapeDtypeStruct((H, S, 2 * LANE), F32),
        ],
        scratch_shapes=[pltpu.VMEM((tk, 2 * LANE), F32), pltpu.VMEM((tk, LANE), F32)],
        compiler_params=_cparams(2),
    )(k_cat, v, q_cat, dy, lse, delta)


def _mla_up_bwd(dq_cat, dk_cat, dv, proj, cf, sf, g_cq, g_ckv, wuq_p, wukv, dproj):
    H, S, _ = dq_cat.shape
    tm = _rtile(S, 256)
    PW = proj.shape[1]
    kr_off = PW - KR_PAD

    def body(dq_ref, dk_ref, dv_ref, cq_ref, ckv_ref, cf_ref, sf_ref, gq_ref, gkv_ref, wq_ref, wkv_ref,
             dqp_ref, dkvp_ref, dcq_ref, dckv_ref, dkr_ref, dgq_ref, dgkv_ref, aq_scr, akv_scr, akr_scr):
        i, h = pl.program_id(0), pl.program_id(1)

        @pl.when(h == 0)
        def _():
            aq_scr[...] = jnp.zeros_like(aq_scr)
            akv_scr[...] = jnp.zeros_like(akv_scr)
            akr_scr[...] = jnp.zeros_like(akr_scr)

        cfv, sfv = cf_ref[...], sf_ref[...]
        dq = dq_ref[0] * MLA_SCALE
        dqr = dq[:, LANE:2 * LANE]
        dqp = jnp.concatenate([dq[:, 0:LANE], dqr * cfv, dqr * sfv], axis=1).astype(CDT)
        dqp_ref[0] = dqp
        aq_scr[...] += _nt(dqp, wq_ref[0])
        dk = dk_ref[0]
        dkvp = jnp.concatenate([dk[:, 0:LANE], dv_ref[0]], axis=1).astype(CDT)
        dkvp_ref[0] = dkvp
        akv_scr[...] += _nt(dkvp, wkv_ref[0])
        akr_scr[...] += dk[:, LANE:2 * LANE]

        @pl.when(h == H - 1)
        def _():
            def rms_bwd(c_ref, g_ref, acc_ref, d_ref, dg_ref):
                xh, rs = _rms_stats(c_ref[...])
                dn = acc_ref[...]
                dxh = dn * g_ref[...]
                d_ref[...] = (rs * (dxh - xh * jnp.mean(dxh * xh, axis=-1, keepdims=True))).astype(d_ref.dtype)
                part = _rowsum8(dn * xh)

                @pl.when(i == 0)
                def _():
                    dg_ref[...] = part

                @pl.when(i > 0)
                def _():
                    dg_ref[...] += part

            rms_bwd(cq_ref, gq_ref, aq_scr, dcq_ref, dgq_ref)
            rms_bwd(ckv_ref, gkv_ref, akv_scr, dckv_ref, dgkv_ref)
            dkr = akr_scr[...]
            dkr_ref[...] = jnp.concatenate([dkr * cfv, dkr * sfv, jnp.zeros((tm, KR_PAD - 2 * LANE), F32)], axis=1).astype(dkr_ref.dtype)

    cspec = lambda off, w: pl.BlockSpec((tm, w), lambda i, h, o=off // w: (i, o))
    hspec = lambda w: pl.BlockSpec((1, tm, w), lambda i, h: (h, i, 0))
    outs = pl.pallas_call(
        body,
        name="mla_up_bwd",
        grid=(S // tm, H),
        in_specs=[
            hspec(2 * LANE), hspec(2 * LANE), hspec(LANE),
            cspec(OFF_CQ, MLA_RANK), cspec(OFF_CKV, MLA_RANK),
            pl.BlockSpec((tm, LANE), lambda i, h: (i, 0)), pl.BlockSpec((tm, LANE), lambda i, h: (i, 0)),
            pl.BlockSpec((1, MLA_RANK), lambda i, h: (0, 0)), pl.BlockSpec((1, MLA_RANK), lambda i, h: (0, 0)),
            pl.BlockSpec((1, MLA_RANK, 3 * LANE), lambda i, h: (h, 0, 0)),
            pl.BlockSpec((1, MLA_RANK, 2 * LANE), lambda i, h: (h, 0, 0)),
        ],
        out_specs=[
            hspec(3 * LANE), hspec(2 * LANE),
            pl.BlockSpec((tm, MLA_RANK), lambda i, h: (i, 0)),
            pl.BlockSpec((tm, MLA_RANK), lambda i, h: (i, 0)),
            pl.BlockSpec((tm, KR_PAD), lambda i, h: (i, 0)),
            pl.BlockSpec((SUB, MLA_RANK), lambda i, h: (0, 0)),
            pl.BlockSpec((SUB, MLA_RANK), lambda i, h: (0, 0)),
        ],
        out_shape=[
            jax.ShapeDtypeStruct((H, S, 3 * LANE), CDT), jax.ShapeDtypeStruct((H, S, 2 * LANE), CDT),
            jax.ShapeDtypeStruct((S, MLA_RANK), CDT), jax.ShapeDtypeStruct((S, MLA_RANK), CDT),
            jax.ShapeDtypeStruct((S, KR_PAD), CDT),
            jax.ShapeDtypeStruct((SUB, MLA_RANK), F32), jax.ShapeDtypeStruct((SUB, MLA_RANK), F32),
        ],
        scratch_shapes=[pltpu.VMEM((tm, MLA_RANK), F32), pltpu.VMEM((tm, MLA_RANK), F32), pltpu.VMEM((tm, LANE), F32)],
        compiler_params=_cparams(2),
    )(dq_cat, dk_cat, dv, proj, proj, cf, sf, g_cq, g_ckv, wuq_p, wukv)
    dqp, dkvp, dcq, dckv, dkr, dgq, dgkv = outs
    dproj = _copy_into("dproj_cq", dcq, dproj, OFF_CQ)
    dproj = _copy_into("dproj_ckv", dckv, dproj, OFF_CKV)
    dproj = _copy_into("dproj_kr", dkr, dproj, kr_off)
    return dproj, dqp, dkvp, dgq, dgkv


def _heads_tn(name, a, b):
    S, Ka = a.shape
    H, _, W = b.shape
    tk = _rtile(S, 1024)
    nk = S // tk

    def body(a_ref, b_ref, o_ref, acc):
        k = pl.program_id(1)

        @pl.when(k == 0)
        def _():
            acc[...] = jnp.zeros_like(acc)

        acc[...] += _tn(a_ref[...], b_ref[0])

        @pl.when(k == nk - 1)
        def _():
            o_ref[0] = acc[...].astype(o_ref.dtype)

    return pl.pallas_call(
        body,
        name=name,
        grid=(H, nk),
        in_specs=[pl.BlockSpec((tk, Ka), lambda h, k: (k, 0)), pl.BlockSpec((1, tk, W), lambda h, k: (h, k, 0))],
        out_specs=pl.BlockSpec((1, Ka, W), lambda h, k: (h, 0, 0)),
        out_shape=jax.ShapeDtypeStruct((H, Ka, W), CDT),
        scratch_shapes=[pltpu.VMEM((Ka, W), F32)],
        compiler_params=_cparams(2),
    )(a, b)


def _mem_softmax(q, k):
    s = _nt(q, k) * (MEM_HD ** -0.5)
    p = jnp.exp(s - jnp.max(s, axis=1, keepdims=True))
    return p / jnp.sum(p, axis=1, keepdims=True)


def _mem_attn_fwd(proj, memkv):
    S = proj.shape[0]
    Mm = memkv.shape[0]
    tm = _rtile(S, 512)

    def body(q_ref, k_ref, v_ref, y_ref):
        pn = _mem_softmax(q_ref[...], k_ref[...])
        y_ref[...] = _nn(pn, v_ref[...]).astype(y_ref.dtype)

    return pl.pallas_call(
        body,
        name="mem_attn_fwd",
        grid=(S // tm, MEM_HEADS),
        in_specs=[
            pl.BlockSpec((tm, MEM_HD), lambda i, h: (i, OFF_QM // MEM_HD + h)),
            pl.BlockSpec((Mm, MEM_HD), lambda i, h: (0, h)),
            pl.BlockSpec((Mm, MEM_HD), lambda i, h: (0, MEM_HEADS + h)),
        ],
        out_specs=pl.BlockSpec((tm, MEM_HD), lambda i, h: (i, h)),
        out_shape=jax.ShapeDtypeStruct((S, MEM_W), CDT),
        compiler_params=_cparams(2),
    )(proj, memkv, memkv)


def _mem_attn_bwd(dy, proj, memkv, dproj):
    S = proj.shape[0]
    Mm = memkv.shape[0]
    tm = _rtile(S, 512)
    scale = MEM_HD ** -0.5

    def body(dy_ref, q_ref, k_ref, v_ref, _dp, dq_ref, dk_ref, dv_ref):
        i = pl.program_id(1)
        q, k, dyv = q_ref[...].astype(CDT), k_ref[...], dy_ref[...]
        pn = _mem_softmax(q, k)
        dvp = _tn(pn, dyv)
        dp = _nt(dyv, v_ref[...])
        ds = pn * (dp - jnp.sum(dp * pn, axis=1, keepdims=True)) * scale
        dq_ref[...] = _nn(ds, k).astype(dq_ref.dtype)
        dkp = _tn(ds, q)

        @pl.when(i == 0)
        def _():
            dk_ref[...] = dkp
            dv_ref[...] = dvp

        @pl.when(i > 0)
        def _():
            dk_ref[...] += dkp
            dv_ref[...] += dvp

    dproj, dk, dv = pl.pallas_call(
        body,
        name="mem_attn_bwd",
        grid=(MEM_HEADS, S // tm),
        in_specs=[
            pl.BlockSpec((tm, MEM_HD), lambda h, i: (i, h)),
            pl.BlockSpec((tm, MEM_HD), lambda h, i: (i, OFF_QM // MEM_HD + h)),
            pl.BlockSpec((Mm, MEM_HD), lambda h, i: (0, h)),
            pl.BlockSpec((Mm, MEM_HD), lambda h, i: (0, MEM_HEADS + h)),
            pl.BlockSpec(memory_space=pl.ANY),
        ],
        out_specs=[
            pl.BlockSpec((tm, MEM_HD), lambda h, i: (i, OFF_QM // MEM_HD + h)),
            pl.BlockSpec((Mm, MEM_HD), lambda h, i: (0, h)),
            pl.BlockSpec((Mm, MEM_HD), lambda h, i: (0, h)),
        ],
        out_shape=[
            jax.ShapeDtypeStruct(dproj.shape, dproj.dtype),
            jax.ShapeDtypeStruct((Mm, MEM_W), F32),
            jax.ShapeDtypeStruct((Mm, MEM_W), F32),
        ],
        input_output_aliases={4: 0},
        compiler_params=_cparams(2),
    )(dy, proj, memkv, memkv, dproj)
    return dproj, dk, dv


def _small_allreduce(vec):
    NS = vec.shape[1]

    def body(v_ref, o_ref, gbuf, send, recv):
        x, y, c, me = _my_place()
        gbuf[me] = v_ref[...]
        copies = []
        for kk in range(1, N_DEV):
            peer, _ = _peer(x, y, c, kk)
            cp = pltpu.make_async_remote_copy(src_ref=v_ref, dst_ref=gbuf.at[me], send_sem=send.at[kk - 1],
                                              recv_sem=recv.at[kk - 1], device_id=peer, device_id_type=MESH)
            cp.start()
            copies.append(cp)
        for cp in copies:
            cp.wait()
        tot = gbuf[0]
        for d in range(1, N_DEV):
            tot = tot + gbuf[d]
        o_ref[...] = jnp.sum(tot, axis=0, keepdims=True)

    return pl.pallas_call(
        body,
        name="small_allreduce",
        in_specs=[pl.BlockSpec(memory_space=pltpu.VMEM)],
        out_specs=pl.BlockSpec(memory_space=pltpu.VMEM),
        out_shape=jax.ShapeDtypeStruct((1, NS), F32),
        scratch_shapes=[pltpu.VMEM((N_DEV, SUB, NS), F32), pltpu.SemaphoreType.DMA((N_DEV - 1,)),
                        pltpu.SemaphoreType.DMA((N_DEV - 1,))],
        compiler_params=pltpu.CompilerParams(has_side_effects=True, vmem_limit_bytes=V7X_VMEM_LIMIT),
    )(vec)


def _adamw_math(g, w, m, v):
    nm = ADAM_B1 * m + (1.0 - ADAM_B1) * g
    nv = ADAM_B2 * v + (1.0 - ADAM_B2) * (g * g)
    mh = nm / (1.0 - ADAM_B1 ** ADAM_STEP)
    vh = nv / (1.0 - ADAM_B2 ** ADAM_STEP)
    delta = -ADAM_LR * (mh / (jnp.sqrt(vh) + ADAM_EPS) + ADAM_WD * w)
    return delta, nm, nv


def _adam_big(name, recv, w, m, v):
    R, C = w.shape
    tr = _rtile(R, max(SUB, (65536 // C) // SUB * SUB))

    def body(r_ref, w_ref, m_ref, v_ref, g_ref, d_ref, nm_ref, nv_ref):
        g = r_ref[0].astype(F32)
        for d in range(1, N_DEV):
            g = g + r_ref[d].astype(F32)
        delta, nm, nv = _adamw_math(g, w_ref[...], m_ref[...], v_ref[...])
        g_ref[...] = g
        d_ref[...] = delta
        nm_ref[...] = nm
        nv_ref[...] = nv

    blk = pl.BlockSpec((tr, C), lambda i: (i, 0))
    return pl.pallas_call(
        body,
        name=name,
        grid=(R // tr,),
        in_specs=[pl.BlockSpec((N_DEV, tr, C), lambda i: (0, i, 0)), blk, blk, blk],
        out_specs=[blk, blk, blk, blk],
        out_shape=[jax.ShapeDtypeStruct((R, C), F32)] * 4,
        compiler_params=_cparams(1),
    )(recv, w, m, v)


def _adam_small(g, w, m, v):
    def body(g_ref, w_ref, m_ref, v_ref, d_ref, nm_ref, nv_ref):
        delta, nm, nv = _adamw_math(g_ref[...], w_ref[...], m_ref[...], v_ref[...])
        d_ref[...] = delta
        nm_ref[...] = nm
        nv_ref[...] = nv

    return pl.pallas_call(body, name="adam_small", out_shape=[jax.ShapeDtypeStruct(g.shape, F32)] * 3)(g, w, m, v)


def _rot(w):
    h = w.shape[-1] // 2
    return jnp.concatenate([-w[..., h:], w[..., :h]], axis=-1)


def _unrot(dw):
    h = dw.shape[-1] // 2
    return jnp.concatenate([dw[..., h:], -dw[..., :h]], axis=-1)


def _pad_cols(w, width):
    return jnp.pad(w, [(0, 0)] * (w.ndim - 1) + [(0, width - w.shape[-1])])


def kernel(x, mem, positions, ln_emb_g, ln_emb_b, hgrn_lb_logits, w_in, hgrn_norm_g, mla_g_cq, mla_g_ckv, mla_w_uq, mla_w_ukv, mem_w_kv, w_branch, w_o, ln1_g, ln1_b, w_ffn_gate, w_ffn_up, w_ffn_down, ln2_g, ln2_b, loss_target, m_ln_emb_g, m_ln_emb_b, m_hgrn_lb_logits, m_w_in, m_hgrn_norm_g, m_mla_g_cq, m_mla_g_ckv, m_mla_w_uq, m_mla_w_ukv, m_mem_w_kv, m_w_branch, m_w_o, m_ln1_g, m_ln1_b, m_w_ffn_gate, m_w_ffn_up, m_w_ffn_down, m_ln2_g, m_ln2_b, v_ln_emb_g, v_ln_emb_b, v_hgrn_lb_logits, v_w_in, v_hgrn_norm_g, v_mla_g_cq, v_mla_g_ckv, v_mla_w_uq, v_mla_w_ukv, v_mem_w_kv, v_w_branch, v_w_o, v_ln1_g, v_ln1_b, v_w_ffn_gate, v_w_ffn_up, v_w_ffn_down, v_ln2_g, v_ln2_b):
    x2, tgt = x[0], loss_target[0]
    S, D = x2.shape
    Mm = mem.shape[1]
    F = w_ffn_gate.shape[2] * N_DEV
    GW = 3 * D
    PW = OFF_GATE + GW + KR_PAD
    KR = OFF_GATE + GW
    NIN = w_in.shape[2] * N_DEV
    assert NIN == OFF_GATE + MLA_ROPE + GW
    _, _, _, me = _my_place()
    row = lambda a: a.reshape(1, -1)

    big_w = [w_in[0], mla_w_uq[0], mla_w_ukv[0], mem_w_kv[0], w_branch[0].reshape(3 * BR_W, -1), w_o[0],
             w_ffn_gate[0], w_ffn_up[0], w_ffn_down[0]]
    big_m = [m_w_in[0], m_mla_w_uq[0], m_mla_w_ukv[0], m_mem_w_kv[0], m_w_branch[0].reshape(3 * BR_W, -1), m_w_o[0],
             m_w_ffn_gate[0], m_w_ffn_up[0], m_w_ffn_down[0]]
    big_v = [v_w_in[0], v_mla_w_uq[0], v_mla_w_ukv[0], v_mem_w_kv[0], v_w_branch[0].reshape(3 * BR_W, -1), v_w_o[0],
             v_w_ffn_gate[0], v_w_ffn_up[0], v_w_ffn_down[0]]
    big_wb = [w.astype(CDT) for w in big_w]
    g_in, g_lb = _exchange("weights_all_gather", [big_wb[0], hgrn_lb_logits.reshape(4, -1)], False)
    cols = lambda g: jnp.transpose(g, (1, 0, 2)).reshape(g.shape[1], -1)
    win = cols(g_in)
    kr_w = win[:, OFF_QM:OFF_QM + MLA_ROPE]
    zeros64 = jnp.zeros_like(kr_w)
    win_p = jnp.concatenate([win[:, :OFF_QM], win[:, OFF_QM + MLA_ROPE:], kr_w, zeros64, _rot(kr_w), zeros64,
                             jnp.zeros((D, KR_PAD - 2 * LANE), CDT)], axis=1)
    lbl4 = jnp.transpose(g_lb, (1, 0, 2)).reshape(4, -1)

    half = MLA_ROPE // 2
    inv_freq = jnp.power(ROPE_THETA, -jnp.arange(half, dtype=F32) / half)
    ang = positions[0].astype(F32)[:, None] * inv_freq
    cf = _pad_cols(jnp.tile(jnp.cos(ang), (1, 2)), LANE)
    sf = _pad_cols(jnp.tile(jnp.sin(ang), (1, 2)), LANE)

    tm512 = _rtile(S, 512)
    ident = lambda accs, tiles, rows: ([accs[0]], [])

    def epi_ln0(accs, tiles, rows):
        h = _ln_stats(tiles[0])[0] * rows[0] + rows[1]
        return [h, h], []

    h0, h0b = _fused_mm("ln_emb_fwd", "nn", [], S, D, 1, tm512, D, 1, [(D, F32, 0, None), (D, CDT, 0, None)], epi_ln0,
                        tiles=[(x2, 0)], rows=[(row(ln_emb_g), 0), (row(ln_emb_b), 0)])
    proj, g_uq, g_ukv, g_mkv, g_wb, g_wo = _fused_mm(
        "proj", "nn", [[(h0b, 0, win_p, 0)]], S, PW, D, _rtile(S, 1024), _tile(PW, 512), D, [(PW, F32, 0, None)], ident,
        xchg=_Xchg(big_wb[1:6], False))
    wuq_p = jnp.concatenate([g_uq[..., :MLA_NOPE], _pad_cols(g_uq[..., MLA_NOPE:], LANE),
                             _pad_cols(_rot(g_uq[..., MLA_NOPE:]), LANE)], axis=-1)
    wukv = g_ukv
    wmkv = g_mkv.reshape(-1, g_mkv.shape[-1])
    wb = jnp.transpose(g_wb.reshape(N_DEV, 3, BR_W, -1), (1, 2, 0, 3)).reshape(3, BR_W, D)
    wo = g_wo.reshape(-1, D)
    o_f, st_f = _gla_fwd(proj, lbl4, OFF_FF, False, "gla_fwd_f")
    o_b, st_b = _gla_fwd(proj, lbl4, OFF_FB, True, "gla_fwd_b")
    y_hg = _hgrn_post_fwd(o_f, o_b, proj, hgrn_norm_g)
    q_cat, k_cat, v_mla, vt_mla, cqn, ckvn = _mla_up(proj, cf, sf, mla_g_cq, mla_g_ckv, wuq_p, wukv)
    y_mla, ot, lse, g_wg, g_wu, g_wd = _mla_attn_fwd(q_cat, k_cat, vt_mla, _Xchg(big_wb[6:9], False))
    wg, wu = cols(g_wg), cols(g_wu)
    wd = g_wd.reshape(-1, D)
    memb = mem[0].astype(CDT)
    (memkv,) = _fused_mm("mem_kv", "nn", [[(memb, 0, wmkv, 0)]], Mm, 2 * MEM_W, D, Mm, _tile(2 * MEM_W, 512), D,
                         [(2 * MEM_W, CDT, 0, None)], ident)
    y_mem = _mem_attn_fwd(proj, memkv)
    ys = [y_hg, y_mla, y_mem]
    tnD = _tile(D, 512, OFF_GATE)

    def epi_branch(accs, tiles, rows):
        return [_sigmoid(tiles[0]) * accs[0] + _sigmoid(tiles[1]) * accs[1] + _sigmoid(tiles[2]) * accs[2]], []

    (merged,) = _fused_mm("branch_fwd", "nn", [[(ys[b], 0, wb[b], 0)] for b in range(3)], S, D, BR_W, tm512, tnD, BR_W,
                          [(D, CDT, 0, None)], epi_branch, tiles=[(proj, OFF_GATE + b * D) for b in range(3)])

    def epi_ln1(accs, tiles, rows):
        r1v = ALPHA * tiles[0] + accs[0]
        return [r1v, _ln_stats(r1v)[0] * rows[0] + rows[1]], []

    r1, h1b = _fused_mm("wo_ln1", "nn", [[(merged, 0, wo, 0)]], S, D, D, tm512, D, _tile(D, 512),
                        [(D, F32, 0, None), (D, CDT, 0, None)], epi_ln1, tiles=[(h0, 0)], rows=[(ln1_g, 0), (ln1_b, 0)])
    tnF = _tile(F, 512)

    def epi_up(accs, tiles, rows):
        gp, up = accs
        return [gp, up, gp * _sigmoid(gp) * up], []

    gpb, upb, act = _fused_mm("ffn_up", "nn", [[(h1b, 0, wg, 0)], [(h1b, 0, wu, 0)]], S, F, D, tm512, tnF, D,
                              [(F, CDT, 0, None)] * 3, epi_up)

    def epi_down(accs, tiles, rows):
        g1, b1, g2, b2 = rows
        h1 = _ln_stats(tiles[0])[0] * g1 + b1
        xh2, rstd2 = _ln_stats(ALPHA * h1 + accs[0])
        diff = xh2 * g2 + b2 - tiles[1]
        dh2 = diff * (1.0 / D)
        dr2v = _ln_bwd(dh2, xh2, rstd2, g2)
        return [dr2v, dr2v], [dh2 * xh2, dh2, diff * diff * (0.5 / D)]

    dr2, dr2b, dg2, db2, lossp = _fused_mm(
        "ffn_down_loss", "nn", [[(act, 0, wd, 0)]], S, D, F, tm512, D, tnF, [(D, F32, 0, None), (D, CDT, 0, None)],
        epi_down, tiles=[(r1, 0), (tgt, 0)], rows=[(ln1_g, 0), (ln1_b, 0), (ln2_g, 0), (ln2_b, 0)], n_racc=3)

    def epi_dact(accs, tiles, rows):
        da, gp, up = accs[0], tiles[0].astype(F32), tiles[1].astype(F32)
        s = _sigmoid(gp)
        return [da * up * (s * (1.0 + gp * (1.0 - s))), da * (gp * s)], []

    dgp, dup = _fused_mm("ffn_dact", "nt", [[(dr2b, 0, wd, 0)]], S, F, D, tm512, tnF, D, [(F, CDT, 0, None)] * 2,
                         epi_dact, tiles=[(gpb, 0), (upb, 0)])
    tkS = _rtile(S, 1024)
    (d_wd,) = _fused_mm("dw_down", "tn", [[(act, 0, dr2b, 0)]], F, D, S, tnF, D, tkS, [(D, CDT, 0, None)], ident)
    d_wg, d_wu = _fused_mm("dw_gate_up", "tn", [[(h1b, 0, dgp, 0)], [(h1b, 0, dup, 0)]], D, F, S, _tile(D, 1024), tnF, tkS,
                           [(F, CDT, 0, None)] * 2, lambda accs, tiles, rows: (accs, []))

    def epi_dh1(accs, tiles, rows):
        dh1 = accs[0] + ALPHA * tiles[0]
        xh1, rstd1 = _ln_stats(tiles[1])
        dr1v = _ln_bwd(dh1, xh1, rstd1, rows[0])
        return [dr1v, dr1v], [dh1 * xh1, dh1]

    uncols = lambda dw: jnp.transpose(dw.reshape(dw.shape[0], N_DEV, -1), (1, 0, 2))
    dr1, dr1b, dg1, db1, r_wg, r_wu, r_wd = _fused_mm(
        "dh1_ln1", "nt", [[(dgp, 0, wg, 0), (dup, 0, wu, 0)]], S, D, F, _rtile(S, 256), D, tnF,
        [(D, F32, 0, None), (D, CDT, 0, None)], epi_dh1, tiles=[(dr2, 0), (r1, 0)], rows=[(ln1_g, 0)], n_racc=2,
        xchg=_Xchg([uncols(d_wg), uncols(d_wu), d_wd.reshape(N_DEV, -1, D)], True))
    (dmerged,) = _fused_mm("dmerged", "nt", [[(dr1b, 0, wo, 0)]], S, D, D, tm512, _tile(D, 512), D, [(D, CDT, 0, None)], ident)
    (d_wo,) = _fused_mm("dw_o", "tn", [[(merged, 0, dr1b, 0)]], D, D, S, _tile(D, 512), D, tkS, [(D, CDT, 0, None)], ident)

    def epi_dbranch(accs, tiles, rows):
        dm, s = tiles[0].astype(F32), _sigmoid(tiles[1])
        return [dm * s, dm * accs[0] * s * (1.0 - s)], []

    dproj = None
    d_wbs, dys = [], []
    for b in range(3):
        du, dproj = _fused_mm(f"branch_bwd{b}", "nn", [[(ys[b], 0, wb[b], 0)]], S, D, BR_W, tm512, tnD, BR_W,
                              [(D, CDT, 0, None), (PW, CDT, OFF_GATE + b * D, dproj)], epi_dbranch,
                              tiles=[(dmerged, 0), (proj, OFF_GATE + b * D)])
        (dwb,) = _fused_mm(f"dw_branch{b}", "tn", [[(ys[b], 0, du, 0)]], BR_W, D, S, _tile(BR_W, 512), D, tkS,
                           [(D, CDT, 0, None)], ident)
        (dyb,) = _fused_mm(f"dy_branch{b}", "nt", [[(du, 0, wb[b], 0)]], S, BR_W, D, tm512, _tile(BR_W, 512), D,
                           [(BR_W, F32 if b == 0 else CDT, 0, None)], ident)
        d_wbs.append(dwb)
        dys.append(dyb)
    dy_hg, dy_mla, dy_mem = dys

    dproj, dk_mem, dv_mem = _mem_attn_bwd(dy_mem, proj, memkv, dproj)
    dkv_mem = jnp.concatenate([dk_mem, dv_mem], axis=1).astype(CDT)
    (d_wmkv,) = _fused_mm("dw_memkv", "tn", [[(memb, 0, dkv_mem, 0)]], D, 2 * MEM_W, Mm, _tile(D, 512), 2 * MEM_W, Mm,
                          [(2 * MEM_W, CDT, 0, None)], ident)

    delta = _mla_delta(dy_mla, ot)
    dk_cat, dv_h, dq_cat = _mla_attn_bwd(q_cat, k_cat, v_mla, dy_mla, lse, delta)
    dproj, dqp, dkvp, dgq, dgkv = _mla_up_bwd(dq_cat, dk_cat, dv_h, proj, cf, sf, mla_g_cq, mla_g_ckv, wuq_p, wukv, dproj)
    d_wuq_p = _heads_tn("dw_uq", cqn, dqp).astype(F32)
    d_wukv = _heads_tn("dw_ukv", ckvn, dkvp)
    d_wuq = jnp.concatenate([d_wuq_p[..., :MLA_NOPE],
                             d_wuq_p[..., LANE:LANE + MLA_ROPE] + _unrot(d_wuq_p[..., 2 * LANE:2 * LANE + MLA_ROPE])],
                            axis=-1).astype(CDT)

    do_hg, dproj, dng = _hgrn_post_bwd(dy_hg, o_f, o_b, proj, hgrn_norm_g, dproj)
    dproj, dq1, di1, dl_f = _gla_bwd(proj, lbl4, OFF_FF, False, do_hg, st_f, dproj, None, "gla_bwd_f")
    dproj, dq2, di2, dl_b = _gla_bwd(proj, lbl4, OFF_FB, True, do_hg, st_b, dproj, (dq1, di1), "gla_bwd_b")
    dproj = _copy_into("dproj_q", dq2, dproj, OFF_Q)
    dproj = _copy_into("dproj_i", di2, dproj, OFF_I)

    def epi_dh0(accs, tiles, rows):
        dh0 = accs[0] + ALPHA * tiles[0]
        xh, rstd = _ln_stats(tiles[1])
        return [_ln_bwd(dh0, xh, rstd, rows[0])], [dh0 * xh, dh0]

    d_wb = jnp.transpose(jnp.stack(d_wbs).reshape(3, BR_W, N_DEV, -1), (2, 0, 1, 3)).reshape(N_DEV, 3 * BR_W, -1)
    d_win_p, r_uq, r_ukv, r_mkv, r_wb, r_wo = _fused_mm(
        "dw_in", "tn", [[(h0b, 0, dproj, 0)]], D, PW, S, _tile(D, 1024), _tile(PW, 512), tkS, [(PW, CDT, 0, None)], ident,
        xchg=_Xchg([d_wuq, d_wukv, d_wmkv.reshape(N_DEV, -1, 2 * MEM_W), d_wb, d_wo.reshape(N_DEV, -1, D)], True))
    d_kr = (d_win_p[:, KR:KR + MLA_ROPE].astype(F32) + _unrot(d_win_p[:, KR + LANE:KR + LANE + MLA_ROPE].astype(F32))).astype(CDT)
    d_win = jnp.concatenate([d_win_p[:, :OFF_QM], d_kr, d_win_p[:, OFF_QM:KR]], axis=1)
    grad_x, dge, dbe, r_in = _fused_mm(
        "dh0_ln_emb", "nt", [[(dproj, 0, win_p, 0)]], S, D, PW, tm512, D, _tile(PW, 512), [(D, F32, 0, None)], epi_dh0,
        tiles=[(dr1, 0), (x2, 0)], rows=[(row(ln_emb_g), 0)], n_racc=2, xchg=_Xchg([uncols(d_win)], True))

    recv = [r_in, r_uq, r_ukv, r_mkv, r_wb, r_wo, r_wg, r_wu, r_wd]
    names = ["w_in", "w_uq", "w_ukv", "mem_w_kv", "w_branch", "w_o", "w_gate", "w_up", "w_down"]
    big_out = [_adam_big("adam_" + nme, r, w, m_, v_) for nme, r, w, m_, v_ in zip(names, recv, big_w, big_m, big_v)]

    parts = [dge, dbe, dng, dgq, dgkv, dg1, db1, dg2, db2, dl_f, dl_b, lossp]
    widths = [p.shape[1] for p in parts]
    red = _small_allreduce(jnp.concatenate(parts, axis=1))[0]
    offs = [sum(widths[:i]) for i in range(len(widths))]
    rs = [red[o:o + w_] for o, w_ in zip(offs, widths)]
    g_le_g, g_le_b, g_ng, g_gq, g_gkv, g_l1g, g_l1b, g_l2g, g_l2b, g_dlf, g_dlb, g_loss = rs
    loss = jnp.sum(g_loss)
    g_ng = g_ng.reshape(HG_HEADS, HG_DK).sum(axis=0)
    dl0 = jnp.stack([g_dlf, g_dlb])
    g_lb_full = jnp.stack([dl0, -dl0], axis=1)
    lbw = hgrn_lb_logits.shape[2]
    g_lb = lax.dynamic_slice_in_dim(g_lb_full, me * lbw, lbw, axis=2)

    small_g = [g_le_g, g_le_b, g_lb, g_ng.reshape(1, -1), g_gq.reshape(1, -1), g_gkv.reshape(1, -1), g_l1g.reshape(1, -1),
               g_l1b.reshape(1, -1), g_l2g.reshape(1, -1), g_l2b.reshape(1, -1)]
    small_w = [ln_emb_g, ln_emb_b, hgrn_lb_logits, hgrn_norm_g, mla_g_cq, mla_g_ckv, ln1_g, ln1_b, ln2_g, ln2_b]
    small_m = [m_ln_emb_g, m_ln_emb_b, m_hgrn_lb_logits, m_hgrn_norm_g, m_mla_g_cq, m_mla_g_ckv, m_ln1_g, m_ln1_b, m_ln2_g, m_ln2_b]
    small_v = [v_ln_emb_g, v_ln_emb_b, v_hgrn_lb_logits, v_hgrn_norm_g, v_mla_g_cq, v_mla_g_ckv, v_ln1_g, v_ln1_b, v_ln2_g, v_ln2_b]
    small_g = [g.reshape(w.shape) for g, w in zip(small_g, small_w)]
    pack = lambda lst: jnp.concatenate([a.reshape(-1) for a in lst]).reshape(1, -1)
    s_delta, s_nm, s_nv = _adam_small(pack(small_g), pack(small_w), pack(small_m), pack(small_v))
    sizes = [w.size for w in small_w]
    soffs = [sum(sizes[:i]) for i in range(len(sizes))]
    unpack = lambda p: [p[0, o:o + n].reshape(w.shape) for o, n, w in zip(soffs, sizes, small_w)]
    s_delta, s_nm, s_nv = unpack(s_delta), unpack(s_nm), unpack(s_nv)

    def ordered(small, big):
        sm = list(small)
        bg = [b.reshape(w.shape) for b, w in zip(big, [w_in, mla_w_uq, mla_w_ukv, mem_w_kv, w_branch, w_o, w_ffn_gate, w_ffn_up, w_ffn_down])]
        return [sm[0], sm[1], sm[2], bg[0], sm[3], sm[4], sm[5], bg[1], bg[2], bg[3], bg[4], bg[5], sm[6], sm[7], bg[6], bg[7], bg[8], sm[8], sm[9]]

    grads = ordered(small_g, [o[0] for o in big_out])
    deltas = ordered(s_delta, [o[1] for o in big_out])
    new_m = ordered(s_nm, [o[2] for o in big_out])
    new_v = ordered(s_nv, [o[3] for o in big_out])
    return (loss, grad_x[None], *grads, *deltas, *new_m, *new_v)
```

```python
import functools

import jax
import jax.numpy as jnp
from jax import lax
from jax.experimental import pallas as pl
from jax.experimental.pallas import tpu as pltpu

F32 = jnp.float32
CDT = jnp.bfloat16
MESH = pl.DeviceIdType.MESH
N_DEV = 8
V7X_VMEM_LIMIT = 60 * 1024 * 1024
LANE = 128
SUB = 8

HG_HEADS, HG_DK, HG_CHUNK = 8, 128, 64
HG_HPS = 8
HG_W = HG_HEADS * HG_DK
MLA_HEADS, MLA_RANK, MLA_NOPE, MLA_ROPE, MLA_V = 8, 512, 128, 64, 128
MLA_QK = MLA_NOPE + MLA_ROPE
MLA_SCALE = MLA_QK ** -0.5
MLA_QSCALE = MLA_SCALE * 1.4426950408889634
VT_ROWS = LANE + 16
MLA_TQ = 1024
MLA_W = MLA_HEADS * MLA_V
MEM_HEADS, MEM_HD = 4, 256
MEM_W = MEM_HEADS * MEM_HD
BR_W = 1024
ROPE_THETA = 10000.0
ALPHA = 2.0 ** 0.25
LN_EPS = 1e-5
RMS_EPS = 1e-6
ADAM_LR, ADAM_B1, ADAM_B2, ADAM_EPS, ADAM_WD, ADAM_STEP = 0.001, 0.9, 0.999, 1e-08, 0.01, 10

OFF_Q, OFF_I, OFF_FF, OFF_FB, OFF_G = 0, 1024, 2048, 3072, 4096
OFF_CQ, OFF_CKV, OFF_QM, OFF_GATE = 5120, 5632, 6144, 7168
KR_PAD = 512


def _cparams(n_grid, side_effects=False):
    return pltpu.CompilerParams(dimension_semantics=("arbitrary",) * n_grid, vmem_limit_bytes=V7X_VMEM_LIMIT,
                                has_side_effects=side_effects)


def _tile(n, pref, *offsets):
    if n <= pref and all(o % n == 0 for o in offsets):
        return n
    t = (min(pref, n) // LANE) * LANE
    while t >= LANE:
        if n % t == 0 and all(o % t == 0 for o in offsets):
            return t
        t -= LANE
    raise ValueError(f"no tile for {n} {pref} {offsets}")


def _rtile(n, pref):
    if n <= pref:
        return n
    t = (pref // SUB) * SUB
    while t >= SUB:
        if n % t == 0:
            return t
        t -= SUB
    raise ValueError(f"no row tile for {n} {pref}")


def _dot(a, b, dims):
    return lax.dot_general(a.astype(CDT), b.astype(CDT), (dims, ((), ())), preferred_element_type=F32)


def _nn(a, b):
    return _dot(a, b, ((1,), (0,)))


def _nt(a, b):
    return _dot(a, b, ((1,), (1,)))


def _tn(a, b):
    return _dot(a, b, ((0,), (0,)))


_DOTS = {"nn": _nn, "nt": _nt, "tn": _tn}


def _sigmoid(x):
    return 1.0 / (1.0 + jnp.exp(-x))


def _rowsum8(v):
    r, w = v.shape
    return v.reshape(r // SUB, SUB, w).sum(axis=0)


def _my_place():
    x, y, c = lax.axis_index("x"), lax.axis_index("y"), lax.axis_index("c")
    return x, y, c, 4 * x + 2 * y + c


def _peer(x, y, c, kk):
    px = 1 - x if kk & 4 else x
    py = 1 - y if kk & 2 else y
    pc = 1 - c if kk & 1 else c
    return (px, py, pc), 4 * px + 2 * py + pc


class _Xchg:
    def __init__(self, arrs, scatter):
        self.arrs, self.scatter, self.n = list(arrs), scatter, len(arrs)
        hbm = pl.BlockSpec(memory_space=pl.ANY)
        self.specs = [hbm] * self.n
        self.out_shape = [jax.ShapeDtypeStruct(((N_DEV,) + a.shape[1:]) if scatter else ((N_DEV,) + a.shape), a.dtype)
                          for a in self.arrs]
        ncp = self.n * (N_DEV - 1)
        self.scratch = [pltpu.SemaphoreType.DMA((ncp,)), pltpu.SemaphoreType.DMA((ncp,)), pltpu.SemaphoreType.DMA((self.n,))]

    def _copies(self, ins, outs, send, recv, loc):
        x, y, c, me = _my_place()
        copies = []
        for w in range(self.n):
            copies.append(pltpu.make_async_copy(ins[w].at[me] if self.scatter else ins[w], outs[w].at[me], loc.at[w]))
            for kk in range(1, N_DEV):
                peer, pid = _peer(x, y, c, kk)
                s = w * (N_DEV - 1) + kk - 1
                copies.append(pltpu.make_async_remote_copy(
                    src_ref=ins[w].at[pid] if self.scatter else ins[w], dst_ref=outs[w].at[me],
                    send_sem=send.at[s], recv_sem=recv.at[s], device_id=peer, device_id_type=MESH))
        return copies

    def start(self, ins, outs, sems):
        for cp in self._copies(ins, outs, *sems):
            cp.start()

    def wait(self, ins, outs, sems):
        for cp in self._copies(ins, outs, *sems):
            cp.wait()


def _all_gather_two_level(name, arrs):
    n = len(arrs)
    NC = N_DEV - 1

    def body(*refs):
        ins, outs = refs[:n], refs[n:2 * n]
        send, recv, loc = refs[2 * n:]
        x, y, c, me = _my_place()
        sibling = (x, y, 1 - c)
        chips = [(1 - x, y), (x, 1 - y), (1 - x, 1 - y)]
        slot = lambda px, py, pc: 4 * px + 2 * py + pc

        def copy(w, k, block, to, src=None):
            dst = outs[w].at[slot(*block)]
            return pltpu.make_async_remote_copy(src_ref=dst if src is None else src, dst_ref=dst,
                                                send_sem=send.at[w * NC + k], recv_sem=recv.at[w * NC + k],
                                                device_id=to, device_id_type=MESH)

        mine = [pltpu.make_async_copy(ins[w], outs[w].at[me], loc.at[w]) for w in range(n)]
        for cp in mine:
            cp.start()
        first = []
        for w in range(n):
            first.append(copy(w, 0, (x, y, c), sibling, src=ins[w]))
            first += [copy(w, 1 + j, (x, y, c), (*chip, c), src=ins[w]) for j, chip in enumerate(chips)]
        for cp in first:
            cp.start()
        passed = []
        for j, chip in enumerate(chips):
            for w in range(n):
                copy(w, 1 + j, (*chip, c), (x, y, c)).wait_recv()
                fwd = copy(w, 4 + j, (*chip, c), sibling)
                fwd.start()
                passed.append(fwd)
        for w in range(n):
            copy(w, 0, sibling, (x, y, c)).wait_recv()
            for j, chip in enumerate(chips):
                copy(w, 4 + j, (*chip, 1 - c), (x, y, c)).wait_recv()
        for cp in first + passed:
            cp.wait_send()
        for cp in mine:
            cp.wait()

    hbm = pl.BlockSpec(memory_space=pl.ANY)
    return pl.pallas_call(
        body,
        name=name,
        in_specs=[hbm] * n,
        out_specs=[hbm] * n,
        out_shape=[jax.ShapeDtypeStruct((N_DEV,) + a.shape, a.dtype) for a in arrs],
        scratch_shapes=[pltpu.SemaphoreType.DMA((n * NC,)), pltpu.SemaphoreType.DMA((n * NC,)), pltpu.SemaphoreType.DMA((n,))],
        compiler_params=pltpu.CompilerParams(has_side_effects=True),
    )(*arrs)


def _fused_mm(name, mode, groups, M, N, K, tm, tn, tk, outs, epi, tiles=(), rows=(), n_racc=0, xchg=None, msplit=1):
    ni, nj, nk = M // tm, N // tn, K // tk
    assert M % tm == 0 and N % tn == 0 and K % tk == 0, (name, M, N, K, tm, tn, tk)
    assert n_racc == 0 or nj == 1
    assert msplit == 1 or (nk == 1 and n_racc == 0 and tm % (16 * msplit) == 0)
    dot = _DOTS[mode] if groups else None
    ins, in_specs = [], []
    for g in groups:
        for a, a_off, b, b_off in g:
            if mode == "tn":
                assert a_off % tm == 0
                in_specs.append(pl.BlockSpec((tk, tm), lambda i, j, k, o=a_off // tm: (k, i + o)))
            else:
                assert a_off % tk == 0
                in_specs.append(pl.BlockSpec((tm, tk), lambda i, j, k, o=a_off // tk: (i, k + o)))
            ins.append(a)
            if mode == "nt":
                assert b_off % tk == 0
                in_specs.append(pl.BlockSpec((tn, tk), lambda i, j, k, o=b_off // tk: (j, k + o)))
            else:
                assert b_off % tn == 0
                in_specs.append(pl.BlockSpec((tk, tn), lambda i, j, k, o=b_off // tn: (k, j + o)))
            ins.append(b)
    for arr, off in tiles:
        assert off % tn == 0
        ins.append(arr)
        in_specs.append(pl.BlockSpec((tm, tn), lambda i, j, k, o=off // tn: (i, j + o)))
    for arr, off in rows:
        assert off % tn == 0
        ins.append(arr)
        in_specs.append(pl.BlockSpec((1, tn), lambda i, j, k, o=off // tn: (0, j + o)))
    aliases = {}
    out_shape, out_specs = [], []
    for oi, (width, dtype, off, alias) in enumerate(outs):
        assert off % tn == 0
        if alias is not None:
            aliases[len(ins)] = oi
            ins.append(alias)
            in_specs.append(pl.BlockSpec(memory_space=pl.ANY))
        out_shape.append(jax.ShapeDtypeStruct((M, width), dtype))
        out_specs.append(pl.BlockSpec((tm, tn), lambda i, j, k, o=off // tn: (i, j + o)))
    for _ in range(n_racc):
        out_shape.append(jax.ShapeDtypeStruct((SUB, N), F32))
        out_specs.append(pl.BlockSpec((SUB, tn), lambda i, j, k: (0, 0)))
    n_alias = len(aliases)
    n_pairs = [len(g) for g in groups]
    use_scratch = nk > 1
    scratch = [pltpu.VMEM((tm, tn), F32) for _ in groups] if use_scratch else []
    nx = 0
    if xchg is not None:
        nx = xchg.n
        ins += xchg.arrs
        in_specs += xchg.specs
        out_shape += xchg.out_shape
        out_specs += xchg.specs
        scratch += xchg.scratch

    def body(*refs):
        it = iter(refs)
        pair_refs = [[(next(it), next(it)) for _ in range(n)] for n in n_pairs]
        tile_refs = [next(it) for _ in tiles]
        row_refs = [next(it) for _ in rows]
        for _ in range(n_alias):
            next(it)
        x_in = [next(it) for _ in range(nx)]
        out_refs = [next(it) for _ in outs]
        racc_refs = [next(it) for _ in range(n_racc)]
        x_out = [next(it) for _ in range(nx)]
        acc_refs = [next(it) for _ in groups] if use_scratch else []
        x_sems = list(it)
        i, j, k = pl.program_id(0), pl.program_id(1), pl.program_id(2)
        if nx:
            @pl.when((i == 0) & (j == 0) & (k == 0))
            def _():
                xchg.start(x_in, x_out, x_sems)

        def products():
            res = []
            for prs in pair_refs:
                s = None
                for a_ref, b_ref in prs:
                    d = dot(a_ref[...], b_ref[...])
                    s = d if s is None else s + d
                res.append(s)
            return res

        def finish(accs):
            out_v, racc_v = epi(accs, [t[...] for t in tile_refs], [r[...] for r in row_refs])
            for o_ref, v in zip(out_refs, out_v):
                o_ref[...] = v.astype(o_ref.dtype)
            for r_ref, v in zip(racc_refs, racc_v):
                part = _rowsum8(v)

                @pl.when(i == 0)
                def _():
                    r_ref[...] = part

                @pl.when(i > 0)
                def _():
                    r_ref[...] += part

        if not use_scratch and msplit > 1:
            ts = tm // msplit
            for s in range(msplit):
                rs = pl.ds(s * ts, ts)
                accs = []
                for prs in pair_refs:
                    acc = None
                    for a_ref, b_ref in prs:
                        dd = dot(a_ref[:, rs] if mode == "tn" else a_ref[rs, :], b_ref[...])
                        acc = dd if acc is None else acc + dd
                    accs.append(acc)
                out_v, _ = epi(accs, [t[rs, :] for t in tile_refs], [r[...] for r in row_refs])
                for o_ref, v in zip(out_refs, out_v):
                    o_ref[rs, :] = v.astype(o_ref.dtype)
        elif not use_scratch:
            finish(products())
        else:
            @pl.when(k == 0)
            def _():
                for acc in acc_refs:
                    acc[...] = jnp.zeros_like(acc)

            for acc, p in zip(acc_refs, products()):
                acc[...] += p

            @pl.when(k == nk - 1)
            def _():
                finish([acc[...] for acc in acc_refs])

        if nx:
            @pl.when((i == ni - 1) & (j == nj - 1) & (k == nk - 1))
            def _():
                xchg.wait(x_in, x_out, x_sems)

    res = pl.pallas_call(
        body,
        name=name,
        grid=(ni, nj, nk),
        in_specs=in_specs,
        out_specs=out_specs,
        out_shape=out_shape,
        scratch_shapes=scratch,
        input_output_aliases=aliases,
        compiler_params=_cparams(3, side_effects=nx > 0),
    )(*ins)
    return res


def _ln_stats(r):
    mu = jnp.mean(r, axis=-1, keepdims=True)
    xc = r - mu
    var = jnp.mean(xc * xc, axis=-1, keepdims=True)
    rstd = lax.rsqrt(var + LN_EPS)
    return xc * rstd, rstd


def _ln_bwd(dh, xhat, rstd, g):
    dxh = dh * g
    m1 = jnp.mean(dxh, axis=-1, keepdims=True)
    m2 = jnp.mean(dxh * xhat, axis=-1, keepdims=True)
    return rstd * (dxh - m1 - xhat * m2)


def _split3(x):
    hi = x.astype(CDT)
    r1 = x - hi.astype(F32)
    mid = r1.astype(CDT)
    lo = (r1 - mid.astype(F32)).astype(CDT)
    return hi, mid, lo


def _tri_matmul(tri, x):
    hi, mid, lo = _split3(x)
    return _nn(tri, hi) + _nn(tri, mid) + _nn(tri, lo)


def _dot3(dot, a, b):
    a_hi, b_hi = a.astype(CDT), b.astype(CDT)
    a_lo = (a - a_hi.astype(F32)).astype(CDT)
    b_lo = (b - b_hi.astype(F32)).astype(CDT)
    return dot(a_hi, b_hi) + dot(a_hi, b_lo) + dot(a_lo, b_hi)


def _gla_masks(reverse):
    C = HG_CHUNK
    r = lax.broadcasted_iota(jnp.int32, (C, C), 0)
    c = lax.broadcasted_iota(jnp.int32, (C, C), 1)
    keep = (c >= r) if reverse else (r >= c)
    return keep


def _m(fn, *lists):
    return [fn(*args) for args in zip(*lists)]


def _gla_chunk_fwd(qraw, fraw, lb, keep, reverse):
    C = HG_CHUNK
    end = 0 if reverse else C - 1
    tri = jnp.where(keep, 1.0, 0.0).astype(CDT)
    sq = _m(_sigmoid, qraw)
    q = _m(lambda x, s: x * s, qraw, sq)
    sg = _m(_sigmoid, fraw)
    f = _m(lambda l_, s: l_ + (1.0 - l_) * s, lb, sg)
    k = _m(lambda x: 1.0 - x, f)
    g = _m(jnp.log, f)
    b = _m(lambda x: _tri_matmul(tri, x), g)
    b_end = _m(lambda x: x[end:end + 1, :], b)
    b_mid = _m(lambda x: x[C // 2:C // 2 + 1, :], b)
    eq = _m(lambda x, m_: jnp.exp(x - m_), b, b_mid)
    ek = _m(lambda x, m_: jnp.exp(m_ - x), b, b_mid)
    eb = _m(jnp.exp, b)
    e2 = _m(lambda x, e_: jnp.exp(e_ - x), b, b_end)
    e_end = _m(jnp.exp, b_end)
    qt = _m(lambda x, e_: x * e_, q, eq)
    kt = _m(lambda x, e_: x * e_, k, ek)
    qs = _m(lambda x, e_: (x * e_).astype(CDT), q, eb)
    k2 = _m(lambda x, e_: (x * e_).astype(CDT), k, e2)
    a = _m(lambda x, y: jnp.where(keep, _dot3(_nt, x, y), 0.0).astype(CDT), qt, kt)
    return dict(sq=sq, q=q, sg=sg, f=f, k=k, eq=eq, ek=ek, eb=eb, e2=e2, e_end=e_end, qt=qt, kt=kt, qs=qs, k2=k2, a=a)


def _gla_fwd(proj, lbl4, f_off, reverse, name):
    S = proj.shape[0]
    C = HG_CHUNK
    R = _rtile(S, 512)
    cpb, nblk = R // C, S // R
    d = 1 if reverse else 0
    blk_map = (lambda b: nblk - 1 - b) if reverse else (lambda b: b)

    W = HG_HPS * HG_DK

    def body(q_ref, i_ref, f_ref, lb_ref, o_ref, st_ref, s_scr):
        @pl.when(pl.program_id(1) == 0)
        def _():
            s_scr[...] = jnp.zeros_like(s_scr)

        l = lb_ref[...]
        lbs = _sigmoid(l[2 * d:2 * d + 1, :] - l[2 * d + 1:2 * d + 2, :])
        keep = _gla_masks(reverse)
        heads = list(range(HG_HPS))
        css = [pl.ds(hh * HG_DK, HG_DK) for hh in heads]
        lb = [lbs[:, hh * HG_DK:(hh + 1) * HG_DK] for hh in heads]
        for cc in range(cpb):
            c = cpb - 1 - cc if reverse else cc
            sl = pl.ds(c * C, C)
            v = [i_ref[sl, cs] for cs in css]
            t = _gla_chunk_fwd([q_ref[sl, cs] for cs in css], [f_ref[sl, cs] for cs in css], lb, keep, reverse)
            st = [s_scr[hh] for hh in heads]
            o = _m(lambda qs, s_, a, v_: _nt(qs, s_) + _nn(a, v_), t["qs"], st, t["a"], v)
            new = _m(lambda e_, s_, v_, k2: e_ * s_ + _tn(v_, k2), t["e_end"], st, v, t["k2"])
            for hh in heads:
                st_ref[c, hh] = st[hh]
                o_ref[sl, css[hh]] = o[hh]
                s_scr[hh] = new[hh]

    col = lambda off: (lambda h, b: (blk_map(b), off // W + h))
    return pl.pallas_call(
        body,
        name=name,
        grid=(HG_HEADS // HG_HPS, nblk),
        in_specs=[
            pl.BlockSpec((R, W), col(OFF_Q)),
            pl.BlockSpec((R, W), col(OFF_I)),
            pl.BlockSpec((R, W), col(f_off)),
            pl.BlockSpec((4, W), lambda h, b: (0, h)),
        ],
        out_specs=[
            pl.BlockSpec((R, W), lambda h, b: (blk_map(b), h)),
            pl.BlockSpec((cpb, HG_HPS, HG_DK, HG_DK), lambda h, b: (blk_map(b), h, 0, 0)),
        ],
        out_shape=[
            jax.ShapeDtypeStruct((S, HG_W), F32),
            jax.ShapeDtypeStruct((S // C, HG_HEADS, HG_DK, HG_DK), F32),
        ],
        scratch_shapes=[pltpu.VMEM((HG_HPS, HG_DK, HG_DK), F32)],
        compiler_params=_cparams(2),
    )(proj, proj, proj, lbl4)


def _gla_bwd(proj, lbl4, f_off, reverse, do, states, dproj, prev, name):
    S = proj.shape[0]
    PW = proj.shape[1]
    C = HG_CHUNK
    R = _rtile(S, 512)
    cpb, nblk = R // C, S // R
    d = 1 if reverse else 0
    blk_map = (lambda b: b) if reverse else (lambda b: nblk - 1 - b)
    final = prev is not None

    def body(*refs):
        if final:
            q_ref, i_ref, f_ref, lb_ref, do_ref, st_ref, pq_ref, pi_ref, _dp, dq_ref, di_ref, df_ref, dl_ref, ds_scr = refs
        else:
            q_ref, i_ref, f_ref, lb_ref, do_ref, st_ref, dq_ref, di_ref, df_ref, dl_ref, ds_scr = refs
        blk = pl.program_id(1)

        @pl.when(blk == 0)
        def _():
            ds_scr[...] = jnp.zeros_like(ds_scr)
            dl_ref[...] = jnp.zeros_like(dl_ref)

        l = lb_ref[...]
        lbs = _sigmoid(l[2 * d:2 * d + 1, :] - l[2 * d + 1:2 * d + 2, :])
        keep = _gla_masks(reverse)
        keep_t = _gla_masks(not reverse)
        tri_t = jnp.where(keep_t, 1.0, 0.0).astype(CDT)
        end = 0 if reverse else C - 1
        is_end = lax.broadcasted_iota(jnp.int32, (C, HG_DK), 0) == end
        dl_all = [jnp.zeros((SUB, HG_DK), F32) for _ in range(HG_HPS)]
        for cc, heads in [(cc, [hh]) for cc in range(cpb) for hh in range(HG_HPS)]:
            css = [pl.ds(hh * HG_DK, HG_DK) for hh in heads]
            lb = [lbs[:, hh * HG_DK:(hh + 1) * HG_DK] for hh in heads]
            dl_acc = [dl_all[hh] for hh in heads]
            c = cc if reverse else cpb - 1 - cc
            sl = pl.ds(c * C, C)
            qraw = [q_ref[sl, cs] for cs in css]
            v = [i_ref[sl, cs] for cs in css]
            t = _gla_chunk_fwd(qraw, [f_ref[sl, cs] for cs in css], lb, keep, reverse)
            dob = [do_ref[sl, cs].astype(CDT) for cs in css]
            vb = _m(lambda x: x.astype(CDT), v)
            st = [st_ref[c, hh] for hh in heads]
            ds = [ds_scr[hh] for hh in heads]
            dsb = _m(lambda x: x.astype(CDT), ds)
            d_qs = _m(_nn, dob, st)
            d_a = _m(lambda x, y: jnp.where(keep, _nt(x, y), 0.0), dob, vb)
            d_qt = _m(lambda x, y: _dot3(_nn, x, y), d_a, t["kt"])
            d_kt = _m(lambda x, y: _dot3(_tn, x, y), d_a, t["qt"])
            d_v = _m(lambda a, x, k2, s_: _tn(a, x) + _nt(k2, s_), t["a"], dob, t["k2"], dsb)
            d_k2 = _m(_nn, vb, dsb)
            d_e = _m(lambda s_, x: jnp.sum(s_ * x, axis=0, keepdims=True), st, ds)
            new_ds = _m(lambda e_, x, y, qs: e_ * x + _tn(y, qs), t["e_end"], ds, dob, t["qs"])
            dq = _m(lambda a, ea, b_, eb_: a * ea + b_ * eb_, d_qt, t["eq"], d_qs, t["eb"])
            dk = _m(lambda a, ea, b_, eb_: a * ea + b_ * eb_, d_kt, t["ek"], d_k2, t["e2"])
            db_end = _m(lambda x, k_, e2, de, ee: jnp.sum(x * (k_ * e2), axis=0, keepdims=True) + de * ee,
                        d_k2, t["k"], t["e2"], d_e, t["e_end"])
            db = _m(lambda q_, dq_, k_, dk_, be: q_ * dq_ - k_ * dk_ + jnp.where(is_end, be, 0.0),
                    t["q"], dq, t["k"], dk, db_end)
            dg = _m(lambda x: _tri_matmul(tri_t, x), db)
            df = _m(lambda g_, f_, dk_: g_ / f_ - dk_, dg, t["f"], dk)
            dfraw = _m(lambda x, l_, s_: x * (1.0 - l_) * s_ * (1.0 - s_), df, lb, t["sg"])
            dl_acc = _m(lambda acc, x, s_: acc + _rowsum8(x * (1.0 - s_)), dl_acc, df, t["sg"])
            dqraw = _m(lambda x, s_, r: x * (s_ * (1.0 + r * (1.0 - s_))), dq, t["sq"], qraw)
            if final:
                dqraw = [x + pq_ref[sl, cs] for x, cs in zip(dqraw, css)]
                d_v = [x + pi_ref[sl, cs] for x, cs in zip(d_v, css)]
            for n, hh in enumerate(heads):
                dl_all[hh] = dl_acc[n]
                ds_scr[hh] = new_ds[n]
                dq_ref[sl, css[n]] = dqraw[n].astype(dq_ref.dtype)
                di_ref[sl, css[n]] = d_v[n].astype(di_ref.dtype)
                df_ref[sl, css[n]] = dfraw[n].astype(df_ref.dtype)
        dl_ref[...] += jnp.concatenate(dl_all, axis=1) * (lbs * (1.0 - lbs))

    W = HG_HPS * HG_DK
    col = lambda off: (lambda h, b: (blk_map(b), off // W + h))
    blk = lambda: pl.BlockSpec((R, W), lambda h, b: (blk_map(b), h))
    ins = [proj, proj, proj, lbl4, do, states]
    in_specs = [
        pl.BlockSpec((R, W), col(OFF_Q)),
        pl.BlockSpec((R, W), col(OFF_I)),
        pl.BlockSpec((R, W), col(f_off)),
        pl.BlockSpec((4, W), lambda h, b: (0, h)),
        blk(),
        pl.BlockSpec((cpb, HG_HPS, HG_DK, HG_DK), lambda h, b: (blk_map(b), h, 0, 0)),
    ]
    dl_shape = jax.ShapeDtypeStruct((SUB, HG_W), F32)
    dl_spec = pl.BlockSpec((SUB, W), lambda h, b: (0, h))
    dp_shape = jax.ShapeDtypeStruct((S, PW), CDT)
    if final:
        ins += [prev[0], prev[1], dproj]
        in_specs += [blk(), blk(), pl.BlockSpec(memory_space=pl.ANY)]
        out_shape = [jax.ShapeDtypeStruct((S, HG_W), CDT), jax.ShapeDtypeStruct((S, HG_W), CDT), dp_shape, dl_shape]
        out_specs = [blk(), blk(), pl.BlockSpec((R, W), col(f_off)), dl_spec]
        aliases = {8: 2}
    else:
        out_shape = [jax.ShapeDtypeStruct((S, HG_W), F32), jax.ShapeDtypeStruct((S, HG_W), F32), dp_shape, dl_shape]
        out_specs = [blk(), blk(), pl.BlockSpec((R, W), col(f_off)), dl_spec]
        aliases = {}
        if dproj is not None:
            ins += [dproj]
            in_specs += [pl.BlockSpec(memory_space=pl.ANY)]
            aliases = {6: 2}
    if (not final) and dproj is not None:
        def body_wrapped(*refs, _b=body):
            _b(*refs[:6], *refs[7:])
        kern = body_wrapped
    else:
        kern = body
    dq, di, dproj, dl = pl.pallas_call(
        kern,
        name=name,
        grid=(HG_HEADS // HG_HPS, nblk),
        in_specs=in_specs,
        out_specs=out_specs,
        out_shape=out_shape,
        scratch_shapes=[pltpu.VMEM((HG_HPS, HG_DK, HG_DK), F32)],
        input_output_aliases=aliases,
        compiler_params=_cparams(2),
    )(*ins)
    return dproj, dq, di, dl


def _hgrn_post_fwd(o_f, o_b, proj, norm_g):
    S = o_f.shape[0]

    def epi(accs, tiles, rows):
        of, ob, graw = tiles
        ng = rows[0][:, :HG_DK]
        o = of + ob
        ys = []
        for h in range(HG_HEADS):
            oh = o[:, h * HG_DK:(h + 1) * HG_DK]
            rs = lax.rsqrt(jnp.mean(oh * oh, axis=-1, keepdims=True) + RMS_EPS)
            ys.append(oh * rs * ng * _sigmoid(graw[:, h * HG_DK:(h + 1) * HG_DK]))
        return [jnp.concatenate(ys, axis=1)], []

    tm = _rtile(S, 512)
    (y,) = _fused_mm("hgrn_post_fwd", "nn", [], S, HG_W, 1, tm, HG_W, 1, [(HG_W, CDT, 0, None)], epi,
                     tiles=[(o_f, 0), (o_b, 0), (proj, OFF_G)], rows=[(jnp.tile(norm_g, (1, HG_HEADS)), 0)])
    return y


def _hgrn_post_bwd(dy, o_f, o_b, proj, norm_g, dproj):
    S = o_f.shape[0]

    def epi(accs, tiles, rows):
        dyv, of, ob, graw = tiles
        ng = rows[0][:, :HG_DK]
        o = of + ob
        dos, dgs, dns = [], [], []
        for h in range(HG_HEADS):
            sl = slice(h * HG_DK, (h + 1) * HG_DK)
            oh, gh, dyh = o[:, sl], graw[:, sl], dyv[:, sl].astype(F32)
            rs = lax.rsqrt(jnp.mean(oh * oh, axis=-1, keepdims=True) + RMS_EPS)
            xh = oh * rs
            sg = _sigmoid(gh)
            dn = dyh * sg
            dgs.append(dyh * (xh * ng) * sg * (1.0 - sg))
            dns.append(dn * xh)
            dxh = dn * ng
            dos.append(rs * (dxh - xh * jnp.mean(dxh * xh, axis=-1, keepdims=True)))
        return [jnp.concatenate(dos, axis=1), jnp.concatenate(dgs, axis=1)], [jnp.concatenate(dns, axis=1)]

    tm = _rtile(S, 512)
    do, dproj, dn = _fused_mm("hgrn_post_bwd", "nn", [], S, HG_W, 1, tm, HG_W, 1,
                              [(HG_W, F32, 0, None), (dproj.shape[1], CDT, OFF_G, dproj)], epi,
                              tiles=[(dy, 0), (o_f, 0), (o_b, 0), (proj, OFF_G)],
                              rows=[(jnp.tile(norm_g, (1, HG_HEADS)), 0)], n_racc=1)
    return do, dproj, dn


def _copy_into(name, src, dst, off):
    S, W = src.shape
    tm = _rtile(S, 512)
    (dst,) = _fused_mm(name, "nn", [], S, W, 1, tm, W, 1, [(dst.shape[1], dst.dtype, off, dst)],
                       lambda accs, tiles, rows: ([tiles[0]], []), tiles=[(src, 0)])
    return dst


def _rms_stats(x):
    rs = lax.rsqrt(jnp.mean(x * x, axis=-1, keepdims=True) + RMS_EPS)
    return x * rs, rs


def _mla_up(proj, cf, sf, g_cq, g_ckv, wuq_p, wukv):
    S = proj.shape[0]
    tm = _rtile(S, 512)
    H = MLA_HEADS

    def body(cq_ref, ckv_ref, kr_ref, krot_ref, cf_ref, sf_ref, gq_ref, gkv_ref, wq_ref, wkv_ref,
             q_ref, k_ref, v_ref, vt_ref, cqn_ref, ckvn_ref):
        cqn = (_rms_stats(cq_ref[...])[0] * gq_ref[...]).astype(CDT)
        ckvn = (_rms_stats(ckv_ref[...])[0] * gkv_ref[...]).astype(CDT)
        cqn_ref[...] = cqn
        ckvn_ref[...] = ckvn
        cfv, sfv = cf_ref[...], sf_ref[...]
        r = _nn(cqn, wq_ref[0]) * MLA_QSCALE
        q_ref[0, :, 0:LANE] = r[:, 0:LANE].astype(CDT)
        q_ref[0, :, LANE:2 * LANE] = (r[:, LANE:2 * LANE] * cfv + r[:, 2 * LANE:3 * LANE] * sfv).astype(CDT)
        kv = _nn(ckvn, wkv_ref[0])
        k_ref[0, :, 0:LANE] = kv[:, 0:LANE].astype(CDT)
        k_ref[0, :, LANE:2 * LANE] = (kr_ref[...] * cfv + krot_ref[...] * sfv).astype(CDT)
        vv = kv[:, LANE:2 * LANE]
        v_ref[0] = vv.astype(CDT)
        vt_ref[0, 0, 0:LANE, :] = vv.T.astype(CDT)
        vt_ref[0, 0, LANE:VT_ROWS, :] = jnp.ones((VT_ROWS - LANE, tm), CDT)

    PWb = proj.shape[1]
    kr_off = PWb - KR_PAD
    cspec = lambda off, w: pl.BlockSpec((tm, w), lambda i, h, o=off // w: (i, o))
    return pl.pallas_call(
        body,
        name="mla_up_fwd",
        grid=(S // tm, H),
        in_specs=[
            cspec(OFF_CQ, MLA_RANK), cspec(OFF_CKV, MLA_RANK), cspec(kr_off, LANE), cspec(kr_off + LANE, LANE),
            pl.BlockSpec((tm, LANE), lambda i, h: (i, 0)), pl.BlockSpec((tm, LANE), lambda i, h: (i, 0)),
            pl.BlockSpec((1, MLA_RANK), lambda i, h: (0, 0)), pl.BlockSpec((1, MLA_RANK), lambda i, h: (0, 0)),
            pl.BlockSpec((1, MLA_RANK, 3 * LANE), lambda i, h: (h, 0, 0)),
            pl.BlockSpec((1, MLA_RANK, 2 * LANE), lambda i, h: (h, 0, 0)),
        ],
        out_specs=[
            pl.BlockSpec((1, tm, 2 * LANE), lambda i, h: (h, i, 0)),
            pl.BlockSpec((1, tm, 2 * LANE), lambda i, h: (h, i, 0)),
            pl.BlockSpec((1, tm, LANE), lambda i, h: (h, i, 0)),
            pl.BlockSpec((1, 1, VT_ROWS, tm), lambda i, h: (h, i, 0, 0)),
            pl.BlockSpec((tm, MLA_RANK), lambda i, h: (i, 0)),
            pl.BlockSpec((tm, MLA_RANK), lambda i, h: (i, 0)),
        ],
        out_shape=[
            jax.ShapeDtypeStruct((H, S, 2 * LANE), CDT), jax.ShapeDtypeStruct((H, S, 2 * LANE), CDT),
            jax.ShapeDtypeStruct((H, S, LANE), CDT), jax.ShapeDtypeStruct((H, S // tm, VT_ROWS, tm), CDT),
            jax.ShapeDtypeStruct((S, MLA_RANK), CDT), jax.ShapeDtypeStruct((S, MLA_RANK), CDT),
        ],
        compiler_params=_cparams(2),
    )(proj, proj, proj, proj, cf, sf, g_cq, g_ckv, wuq_p, wukv)


def _mla_attn_fwd(q_cat, k_cat, vt, xchg=None):
    H, S, _ = q_cat.shape
    tq = _tile(S, MLA_TQ)
    _, nkb, _, tk = vt.shape
    nq = S // tq
    nx = xchg.n if xchg is not None else 0

    def body(*refs):
        q_ref, k_ref, vt_ref = refs[:3]
        x_in = refs[3:3 + nx]
        y_ref, ot_ref, lse_ref = refs[3 + nx:6 + nx]
        x_out = refs[6 + nx:6 + 2 * nx]
        m_scr, acc_scr = refs[6 + 2 * nx:8 + 2 * nx]
        x_sems = refs[8 + 2 * nx:]
        h, i = pl.program_id(0), pl.program_id(1)
        if nx:
            @pl.when((h == 0) & (i == 0))
            def _():
                xchg.start(x_in, x_out, x_sems)

        q = q_ref[0]
        m_scr[...] = jnp.full_like(m_scr, -jnp.inf)
        acc_scr[...] = jnp.zeros_like(acc_scr)

        def step(j, carry):
            kj = k_ref[0, pl.ds(pl.multiple_of(j * tk, tk), tk), :]
            st = _nt(kj, q)
            m_old = m_scr[...]
            m_new = jnp.maximum(m_old, jnp.max(st, axis=0, keepdims=True))
            pt = jnp.exp2(st - m_new)
            acc_scr[...] = jnp.exp2(m_old - m_new) * acc_scr[...] + _nn(vt_ref[0, j], pt)
            m_scr[...] = m_new
            return carry

        lax.fori_loop(0, nkb, step, 0, unroll=4 if nkb % 4 == 0 else 1)
        l = acc_scr[LANE:LANE + 1, :]
        ot = acc_scr[0:LANE, :] / l
        ot_ref[0] = ot
        y_ref[...] = ot.T.astype(CDT)
        lse_ref[0, 0] = m_scr[...] + jnp.log2(l)

        if nx:
            @pl.when((h == H - 1) & (i == nq - 1))
            def _():
                xchg.wait(x_in, x_out, x_sems)

    return pl.pallas_call(
        body,
        name="mla_attn_fwd",
        grid=(H, nq),
        in_specs=[
            pl.BlockSpec((1, tq, 2 * LANE), lambda h, i: (h, i, 0)),
            pl.BlockSpec((1, S, 2 * LANE), lambda h, i: (h, 0, 0)),
            pl.BlockSpec((1, nkb, VT_ROWS, tk), lambda h, i: (h, 0, 0, 0)),
        ] + (xchg.specs if nx else []),
        out_specs=[
            pl.BlockSpec((tq, LANE), lambda h, i: (i, h)),
            pl.BlockSpec((1, LANE, tq), lambda h, i: (h, 0, i)),
            pl.BlockSpec((1, 1, 1, tq), lambda h, i: (h, i, 0, 0)),
        ] + (xchg.specs if nx else []),
        out_shape=[
            jax.ShapeDtypeStruct((S, H * LANE), CDT),
            jax.ShapeDtypeStruct((H, LANE, S), F32),
            jax.ShapeDtypeStruct((H, nq, 1, tq), F32),
        ] + (xchg.out_shape if nx else []),
        scratch_shapes=[pltpu.VMEM((1, tq), F32), pltpu.VMEM((VT_ROWS, tq), F32)] + (xchg.scratch if nx else []),
        compiler_params=_cparams(2, side_effects=nx > 0),
    )(q_cat, k_cat, vt, *(xchg.arrs if nx else []))


def _mla_delta(dy, ot):
    H, _, S = ot.shape
    tq = _tile(S, MLA_TQ)
    nq = S // tq

    def body(dy_ref, ot_ref, d_ref):
        d_ref[0, 0] = jnp.sum(dy_ref[...].astype(F32).T * ot_ref[0], axis=0, keepdims=True)

    return pl.pallas_call(
        body,
        name="mla_delta",
        grid=(H, nq),
        in_specs=[pl.BlockSpec((tq, LANE), lambda h, i: (i, h)), pl.BlockSpec((1, LANE, tq), lambda h, i: (h, 0, i))],
        out_specs=pl.BlockSpec((1, 1, 1, tq), lambda h, i: (h, i, 0, 0)),
        out_shape=jax.ShapeDtypeStruct((H, nq, 1, tq), F32),
        compiler_params=_cparams(2),
    )(dy, ot)


def _mla_attn_bwd(q_cat, k_cat, v, dy, lse, delta):
    H, S, _ = q_cat.shape
    _, nq, _, tq = lse.shape
    tk = _tile(S, 512)
    nkb = S // tk

    def body(k_ref, v_ref, q_ref, do_ref, lse_ref, dl_ref, dk_ref, dv_ref, dq_ref, dk_scr, dv_scr):
        ki = pl.program_id(1)

        @pl.when(ki == 0)
        def _():
            dq_ref[...] = jnp.zeros_like(dq_ref)

        kb, vb = k_ref[0], v_ref[0]
        dk_scr[...] = jnp.zeros_like(dk_scr)
        dv_scr[...] = jnp.zeros_like(dv_scr)

        def step(i, carry):
            rows = pl.ds(pl.multiple_of(i * tq, tq), tq)
            qc = q_ref[0, rows, :]
            doc = do_ref[rows, :]
            pt = jnp.exp2(_nt(kb, qc) - lse_ref[0, i])
            dv_scr[...] += _nn(pt, doc)
            dst = (pt * (_nt(vb, doc) - dl_ref[0, i])).astype(CDT)
            dk_scr[...] += _nn(dst, qc)
            dq_ref[0, rows, :] += _tn(dst, kb)
            return carry

        lax.fori_loop(0, nq, step, 0, unroll=2 if nq % 2 == 0 else 1)
        dk_ref[0] = dk_scr[...] * (MLA_SCALE / MLA_QSCALE)
        dv_ref[0] = dv_scr[...]

    return pl.pallas_call(
        body,
        name="mla_attn_bwd",
        grid=(H, nkb),
        in_specs=[
            pl.BlockSpec((1, tk, 2 * LANE), lambda h, j: (h, j, 0)),
            pl.BlockSpec((1, tk, LANE), lambda h, j: (h, j, 0)),
            pl.BlockSpec((1, S, 2 * LANE), lambda h, j: (h, 0, 0)),
            pl.BlockSpec((S, LANE), lambda h, j: (0, h)),
            pl.BlockSpec((1, nq, 1, tq), lambda h, j: (h, 0, 0, 0)),
            pl.BlockSpec((1, nq, 1, tq), lambda h, j: (h, 0, 0, 0)),
        ],
        out_specs=[
            pl.BlockSpec((1, tk, 2 * LANE), lambda h, j: (h, j, 0)),
            pl.BlockSpec((1, tk, LANE), lambda h, j: (h, j, 0)),
            pl.BlockSpec((1, S, 2 * LANE), lambda h, j: (h, 0, 0)),
        ],
        out_shape=[
            jax.ShapeDtypeStruct((H, S, 2 * LANE), F32),
            jax.ShapeDtypeStruct((H, S, LANE), F32),
            jax.ShapeDtypeStruct((H, S, 2 * LANE), F32),
        ],
        scratch_shapes=[pltpu.VMEM((tk, 2 * LANE), F32), pltpu.VMEM((tk, LANE), F32)],
        compiler_params=_cparams(2),
    )(k_cat, v, q_cat, dy, lse, delta)


def _mla_up_bwd(dq_cat, dk_cat, dv, proj, cf, sf, g_cq, g_ckv, wuq_p, wukv, dproj):
    H, S, _ = dq_cat.shape
    tm = _rtile(S, 256)
    PW = proj.shape[1]
    kr_off = PW - KR_PAD

    def body(dq_ref, dk_ref, dv_ref, cq_ref, ckv_ref, cf_ref, sf_ref, gq_ref, gkv_ref, wq_ref, wkv_ref,
             dqp_ref, dkvp_ref, dcq_ref, dckv_ref, dkr_ref, dgq_ref, dgkv_ref, aq_scr, akv_scr, akr_scr):
        i, h = pl.program_id(0), pl.program_id(1)

        @pl.when(h == 0)
        def _():
            aq_scr[...] = jnp.zeros_like(aq_scr)
            akv_scr[...] = jnp.zeros_like(akv_scr)
            akr_scr[...] = jnp.zeros_like(akr_scr)

        cfv, sfv = cf_ref[...], sf_ref[...]
        dq = dq_ref[0] * MLA_SCALE
        dqr = dq[:, LANE:2 * LANE]
        dqp = jnp.concatenate([dq[:, 0:LANE], dqr * cfv, dqr * sfv], axis=1).astype(CDT)
        dqp_ref[0] = dqp
        aq_scr[...] += _nt(dqp, wq_ref[0])
        dk = dk_ref[0]
        dkvp = jnp.concatenate([dk[:, 0:LANE], dv_ref[0]], axis=1).astype(CDT)
        dkvp_ref[0] = dkvp
        akv_scr[...] += _nt(dkvp, wkv_ref[0])
        akr_scr[...] += dk[:, LANE:2 * LANE]

        @pl.when(h == H - 1)
        def _():
            def rms_bwd(c_ref, g_ref, acc_ref, d_ref, dg_ref):
                xh, rs = _rms_stats(c_ref[...])
                dn = acc_ref[...]
                dxh = dn * g_ref[...]
                d_ref[...] = (rs * (dxh - xh * jnp.mean(dxh * xh, axis=-1, keepdims=True))).astype(d_ref.dtype)
                part = _rowsum8(dn * xh)

                @pl.when(i == 0)
                def _():
                    dg_ref[...] = part

                @pl.when(i > 0)
                def _():
                    dg_ref[...] += part

            rms_bwd(cq_ref, gq_ref, aq_scr, dcq_ref, dgq_ref)
            rms_bwd(ckv_ref, gkv_ref, akv_scr, dckv_ref, dgkv_ref)
            dkr = akr_scr[...]
            dkr_ref[...] = jnp.concatenate([dkr * cfv, dkr * sfv, jnp.zeros((tm, KR_PAD - 2 * LANE), F32)], axis=1).astype(dkr_ref.dtype)

    cspec = lambda off, w: pl.BlockSpec((tm, w), lambda i, h, o=off // w: (i, o))
    hspec = lambda w: pl.BlockSpec((1, tm, w), lambda i, h: (h, i, 0))
    outs = pl.pallas_call(
        body,
        name="mla_up_bwd",
        grid=(S // tm, H),
        in_specs=[
            hspec(2 * LANE), hspec(2 * LANE), hspec(LANE),
            cspec(OFF_CQ, MLA_RANK), cspec(OFF_CKV, MLA_RANK),
            pl.BlockSpec((tm, LANE), lambda i, h: (i, 0)), pl.BlockSpec((tm, LANE), lambda i, h: (i, 0)),
            pl.BlockSpec((1, MLA_RANK), lambda i, h: (0, 0)), pl.BlockSpec((1, MLA_RANK), lambda i, h: (0, 0)),
            pl.BlockSpec((1, MLA_RANK, 3 * LANE), lambda i, h: (h, 0, 0)),
            pl.BlockSpec((1, MLA_RANK, 2 * LANE), lambda i, h: (h, 0, 0)),
        ],
        out_specs=[
            hspec(3 * LANE), hspec(2 * LANE),
            pl.BlockSpec((tm, MLA_RANK), lambda i, h: (i, 0)),
            pl.BlockSpec((tm, MLA_RANK), lambda i, h: (i, 0)),
            pl.BlockSpec((tm, KR_PAD), lambda i, h: (i, 0)),
            pl.BlockSpec((SUB, MLA_RANK), lambda i, h: (0, 0)),
            pl.BlockSpec((SUB, MLA_RANK), lambda i, h: (0, 0)),
        ],
        out_shape=[
            jax.ShapeDtypeStruct((H, S, 3 * LANE), CDT), jax.ShapeDtypeStruct((H, S, 2 * LANE), CDT),
            jax.ShapeDtypeStruct((S, MLA_RANK), CDT), jax.ShapeDtypeStruct((S, MLA_RANK), CDT),
            jax.ShapeDtypeStruct((S, KR_PAD), CDT),
            jax.ShapeDtypeStruct((SUB, MLA_RANK), F32), jax.ShapeDtypeStruct((SUB, MLA_RANK), F32),
        ],
        scratch_shapes=[pltpu.VMEM((tm, MLA_RANK), F32), pltpu.VMEM((tm, MLA_RANK), F32), pltpu.VMEM((tm, LANE), F32)],
        compiler_params=_cparams(2),
    )(dq_cat, dk_cat, dv, proj, proj, cf, sf, g_cq, g_ckv, wuq_p, wukv)
    dqp, dkvp, dcq, dckv, dkr, dgq, dgkv = outs
    dproj = _copy_into("dproj_cq", dcq, dproj, OFF_CQ)
    dproj = _copy_into("dproj_ckv", dckv, dproj, OFF_CKV)
    dproj = _copy_into("dproj_kr", dkr, dproj, kr_off)
    return dproj, dqp, dkvp, dgq, dgkv


def _heads_tn(name, a, b):
    S, Ka = a.shape
    H, _, W = b.shape
    tk = _rtile(S, 1024)
    nk = S // tk

    def body(a_ref, b_ref, o_ref, acc):
        k = pl.program_id(1)

        @pl.when(k == 0)
        def _():
            acc[...] = jnp.zeros_like(acc)

        acc[...] += _tn(a_ref[...], b_ref[0])

        @pl.when(k == nk - 1)
        def _():
            o_ref[0] = acc[...].astype(o_ref.dtype)

    return pl.pallas_call(
        body,
        name=name,
        grid=(H, nk),
        in_specs=[pl.BlockSpec((tk, Ka), lambda h, k: (k, 0)), pl.BlockSpec((1, tk, W), lambda h, k: (h, k, 0))],
        out_specs=pl.BlockSpec((1, Ka, W), lambda h, k: (h, 0, 0)),
        out_shape=jax.ShapeDtypeStruct((H, Ka, W), CDT),
        scratch_shapes=[pltpu.VMEM((Ka, W), F32)],
        compiler_params=_cparams(2),
    )(a, b)


def _mem_softmax(q, k):
    s = _nt(q, k) * (MEM_HD ** -0.5)
    p = jnp.exp(s - jnp.max(s, axis=1, keepdims=True))
    return p / jnp.sum(p, axis=1, keepdims=True)


def _mem_attn_fwd(proj, memkv):
    S = proj.shape[0]
    Mm = memkv.shape[0]
    tm = _rtile(S, 512)

    def body(q_ref, k_ref, v_ref, y_ref):
        pn = _mem_softmax(q_ref[...], k_ref[...])
        y_ref[...] = _nn(pn, v_ref[...]).astype(y_ref.dtype)

    return pl.pallas_call(
        body,
        name="mem_attn_fwd",
        grid=(S // tm, MEM_HEADS),
        in_specs=[
            pl.BlockSpec((tm, MEM_HD), lambda i, h: (i, OFF_QM // MEM_HD + h)),
            pl.BlockSpec((Mm, MEM_HD), lambda i, h: (0, h)),
            pl.BlockSpec((Mm, MEM_HD), lambda i, h: (0, MEM_HEADS + h)),
        ],
        out_specs=pl.BlockSpec((tm, MEM_HD), lambda i, h: (i, h)),
        out_shape=jax.ShapeDtypeStruct((S, MEM_W), CDT),
        compiler_params=_cparams(2),
    )(proj, memkv, memkv)


def _mem_attn_bwd(dy, proj, memkv, dproj):
    S = proj.shape[0]
    Mm = memkv.shape[0]
    tm = _rtile(S, 512)
    scale = MEM_HD ** -0.5

    def body(dy_ref, q_ref, k_ref, v_ref, _dp, dq_ref, dk_ref, dv_ref):
        i = pl.program_id(1)
        q, k, dyv = q_ref[...].astype(CDT), k_ref[...], dy_ref[...]
        pn = _mem_softmax(q, k)
        dvp = _tn(pn, dyv)
        dp = _nt(dyv, v_ref[...])
        ds = pn * (dp - jnp.sum(dp * pn, axis=1, keepdims=True)) * scale
        dq_ref[...] = _nn(ds, k).astype(dq_ref.dtype)
        dkp = _tn(ds, q)

        @pl.when(i == 0)
        def _():
            dk_ref[...] = dkp
            dv_ref[...] = dvp

        @pl.when(i > 0)
        def _():
            dk_ref[...] += dkp
            dv_ref[...] += dvp

    dproj, dk, dv = pl.pallas_call(
        body,
        name="mem_attn_bwd",
        grid=(MEM_HEADS, S // tm),
        in_specs=[
            pl.BlockSpec((tm, MEM_HD), lambda h, i: (i, h)),
            pl.BlockSpec((tm, MEM_HD), lambda h, i: (i, OFF_QM // MEM_HD + h)),
            pl.BlockSpec((Mm, MEM_HD), lambda h, i: (0, h)),
            pl.BlockSpec((Mm, MEM_HD), lambda h, i: (0, MEM_HEADS + h)),
            pl.BlockSpec(memory_space=pl.ANY),
        ],
        out_specs=[
            pl.BlockSpec((tm, MEM_HD), lambda h, i: (i, OFF_QM // MEM_HD + h)),
            pl.BlockSpec((Mm, MEM_HD), lambda h, i: (0, h)),
            pl.BlockSpec((Mm, MEM_HD), lambda h, i: (0, h)),
        ],
        out_shape=[
            jax.ShapeDtypeStruct(dproj.shape, dproj.dtype),
            jax.ShapeDtypeStruct((Mm, MEM_W), F32),
            jax.ShapeDtypeStruct((Mm, MEM_W), F32),
        ],
        input_output_aliases={4: 0},
        compiler_params=_cparams(2),
    )(dy, proj, memkv, memkv, dproj)
    return dproj, dk, dv


def _small_allreduce(vec):
    NS = vec.shape[1]

    def body(v_ref, o_ref, gbuf, send, recv):
        x, y, c, me = _my_place()
        gbuf[me] = v_ref[...]
        copies = []
        for kk in range(1, N_DEV):
            peer, _ = _peer(x, y, c, kk)
            cp = pltpu.make_async_remote_copy(src_ref=v_ref, dst_ref=gbuf.at[me], send_sem=send.at[kk - 1],
                                              recv_sem=recv.at[kk - 1], device_id=peer, device_id_type=MESH)
            cp.start()
            copies.append(cp)
        for cp in copies:
            cp.wait()
        tot = gbuf[0]
        for d in range(1, N_DEV):
            tot = tot + gbuf[d]
        o_ref[...] = jnp.sum(tot, axis=0, keepdims=True)

    return pl.pallas_call(
        body,
        name="small_allreduce",
        in_specs=[pl.BlockSpec(memory_space=pltpu.VMEM)],
        out_specs=pl.BlockSpec(memory_space=pltpu.VMEM),
        out_shape=jax.ShapeDtypeStruct((1, NS), F32),
        scratch_shapes=[pltpu.VMEM((N_DEV, SUB, NS), F32), pltpu.SemaphoreType.DMA((N_DEV - 1,)),
                        pltpu.SemaphoreType.DMA((N_DEV - 1,))],
        compiler_params=pltpu.CompilerParams(has_side_effects=True, vmem_limit_bytes=V7X_VMEM_LIMIT),
    )(vec)


def _adamw_math(g, w, m, v):
    nm = ADAM_B1 * m + (1.0 - ADAM_B1) * g
    nv = ADAM_B2 * v + (1.0 - ADAM_B2) * (g * g)
    mh = nm / (1.0 - ADAM_B1 ** ADAM_STEP)
    vh = nv / (1.0 - ADAM_B2 ** ADAM_STEP)
    delta = -ADAM_LR * (mh / (jnp.sqrt(vh) + ADAM_EPS) + ADAM_WD * w)
    return delta, nm, nv


def _adam_big(name, recv, w, m, v):
    R, C = w.shape
    tr = _rtile(R, max(SUB, (65536 // C) // SUB * SUB))

    def body(r_ref, w_ref, m_ref, v_ref, g_ref, d_ref, nm_ref, nv_ref):
        g = r_ref[0].astype(F32)
        for d in range(1, N_DEV):
            g = g + r_ref[d].astype(F32)
        delta, nm, nv = _adamw_math(g, w_ref[...], m_ref[...], v_ref[...])
        g_ref[...] = g
        d_ref[...] = delta
        nm_ref[...] = nm
        nv_ref[...] = nv

    blk = pl.BlockSpec((tr, C), lambda i: (i, 0))
    return pl.pallas_call(
        body,
        name=name,
        grid=(R // tr,),
        in_specs=[pl.BlockSpec((N_DEV, tr, C), lambda i: (0, i, 0)), blk, blk, blk],
        out_specs=[blk, blk, blk, blk],
        out_shape=[jax.ShapeDtypeStruct((R, C), F32)] * 4,
        compiler_params=_cparams(1),
    )(recv, w, m, v)


def _adam_small(g, w, m, v):
    def body(g_ref, w_ref, m_ref, v_ref, d_ref, nm_ref, nv_ref):
        delta, nm, nv = _adamw_math(g_ref[...], w_ref[...], m_ref[...], v_ref[...])
        d_ref[...] = delta
        nm_ref[...] = nm
        nv_ref[...] = nv

    return pl.pallas_call(body, name="adam_small", out_shape=[jax.ShapeDtypeStruct(g.shape, F32)] * 3)(g, w, m, v)


def _rot(w):
    h = w.shape[-1] // 2
    return jnp.concatenate([-w[..., h:], w[..., :h]], axis=-1)


def _unrot(dw):
    h = dw.shape[-1] // 2
    return jnp.concatenate([dw[..., h:], -dw[..., :h]], axis=-1)


def _pad_cols(w, width):
    return jnp.pad(w, [(0, 0)] * (w.ndim - 1) + [(0, width - w.shape[-1])])


def kernel(x, mem, positions, ln_emb_g, ln_emb_b, hgrn_lb_logits, w_in, hgrn_norm_g, mla_g_cq, mla_g_ckv, mla_w_uq, mla_w_ukv, mem_w_kv, w_branch, w_o, ln1_g, ln1_b, w_ffn_gate, w_ffn_up, w_ffn_down, ln2_g, ln2_b, loss_target, m_ln_emb_g, m_ln_emb_b, m_hgrn_lb_logits, m_w_in, m_hgrn_norm_g, m_mla_g_cq, m_mla_g_ckv, m_mla_w_uq, m_mla_w_ukv, m_mem_w_kv, m_w_branch, m_w_o, m_ln1_g, m_ln1_b, m_w_ffn_gate, m_w_ffn_up, m_w_ffn_down, m_ln2_g, m_ln2_b, v_ln_emb_g, v_ln_emb_b, v_hgrn_lb_logits, v_w_in, v_hgrn_norm_g, v_mla_g_cq, v_mla_g_ckv, v_mla_w_uq, v_mla_w_ukv, v_mem_w_kv, v_w_branch, v_w_o, v_ln1_g, v_ln1_b, v_w_ffn_gate, v_w_ffn_up, v_w_ffn_down, v_ln2_g, v_ln2_b):
    x2, tgt = x[0], loss_target[0]
    S, D = x2.shape
    Mm = mem.shape[1]
    F = w_ffn_gate.shape[2] * N_DEV
    GW = 3 * D
    PW = OFF_GATE + GW + KR_PAD
    KR = OFF_GATE + GW
    NIN = w_in.shape[2] * N_DEV
    assert NIN == OFF_GATE + MLA_ROPE + GW
    _, _, _, me = _my_place()
    row = lambda a: a.reshape(1, -1)

    big_w = [w_in[0], mla_w_uq[0], mla_w_ukv[0], mem_w_kv[0], w_branch[0].reshape(3 * BR_W, -1), w_o[0],
             w_ffn_gate[0], w_ffn_up[0], w_ffn_down[0]]
    big_m = [m_w_in[0], m_mla_w_uq[0], m_mla_w_ukv[0], m_mem_w_kv[0], m_w_branch[0].reshape(3 * BR_W, -1), m_w_o[0],
             m_w_ffn_gate[0], m_w_ffn_up[0], m_w_ffn_down[0]]
    big_v = [v_w_in[0], v_mla_w_uq[0], v_mla_w_ukv[0], v_mem_w_kv[0], v_w_branch[0].reshape(3 * BR_W, -1), v_w_o[0],
             v_w_ffn_gate[0], v_w_ffn_up[0], v_w_ffn_down[0]]
    big_wb = [w.astype(CDT) for w in big_w]
    g_in, g_lb = _all_gather_two_level("weights_all_gather", [big_wb[0], hgrn_lb_logits.reshape(4, -1)])
    cols = lambda g: jnp.transpose(g, (1, 0, 2)).reshape(g.shape[1], -1)
    win = cols(g_in)
    kr_w = win[:, OFF_QM:OFF_QM + MLA_ROPE]
    zeros64 = jnp.zeros_like(kr_w)
    win_p = jnp.concatenate([win[:, :OFF_QM], win[:, OFF_QM + MLA_ROPE:], kr_w, zeros64, _rot(kr_w), zeros64,
                             jnp.zeros((D, KR_PAD - 2 * LANE), CDT)], axis=1)
    lbl4 = jnp.transpose(g_lb, (1, 0, 2)).reshape(4, -1)

    half = MLA_ROPE // 2
    inv_freq = jnp.power(ROPE_THETA, -jnp.arange(half, dtype=F32) / half)
    ang = positions[0].astype(F32)[:, None] * inv_freq
    cf = _pad_cols(jnp.tile(jnp.cos(ang), (1, 2)), LANE)
    sf = _pad_cols(jnp.tile(jnp.sin(ang), (1, 2)), LANE)

    tm512 = _rtile(S, 512)
    ident = lambda accs, tiles, rows: ([accs[0]], [])

    def epi_ln0(accs, tiles, rows):
        h = _ln_stats(tiles[0])[0] * rows[0] + rows[1]
        return [h, h], []

    h0, h0b = _fused_mm("ln_emb_fwd", "nn", [], S, D, 1, tm512, D, 1, [(D, F32, 0, None), (D, CDT, 0, None)], epi_ln0,
                        tiles=[(x2, 0)], rows=[(row(ln_emb_g), 0), (row(ln_emb_b), 0)])
    proj, g_uq, g_ukv, g_mkv, g_wb, g_wo = _fused_mm(
        "proj", "nn", [[(h0b, 0, win_p, 0)]], S, PW, D, _rtile(S, 1024), _tile(PW, 512), D, [(PW, F32, 0, None)], ident,
        xchg=_Xchg(big_wb[1:6], False), msplit=2 if S % 2048 == 0 else 1)
    wuq_p =jnp.concatenate([g_uq[..., :MLA_NOPE], _pad_cols(g_uq[..., MLA_NOPE:], LANE),
                             _pad_cols(_rot(g_uq[..., MLA_NOPE:]), LANE)], axis=-1)
    wukv = g_ukv
    wmkv = g_mkv.reshape(-1, g_mkv.shape[-1])
    wb = jnp.transpose(g_wb.reshape(N_DEV, 3, BR_W, -1), (1, 2, 0, 3)).reshape(3, BR_W, D)
    wo = g_wo.reshape(-1, D)
    o_f, st_f = _gla_fwd(proj, lbl4, OFF_FF, False, "gla_fwd_f")
    o_b, st_b = _gla_fwd(proj, lbl4, OFF_FB, True, "gla_fwd_b")
    y_hg = _hgrn_post_fwd(o_f, o_b, proj, hgrn_norm_g)
    q_cat, k_cat, v_mla, vt_mla, cqn, ckvn = _mla_up(proj, cf, sf, mla_g_cq, mla_g_ckv, wuq_p, wukv)
    y_mla, ot, lse, g_wg, g_wu, g_wd = _mla_attn_fwd(q_cat, k_cat, vt_mla, _Xchg(big_wb[6:9], False))
    wg, wu = cols(g_wg), cols(g_wu)
    wd = g_wd.reshape(-1, D)
    memb = mem[0].astype(CDT)
    (memkv,) = _fused_mm("mem_kv", "nn", [[(memb, 0, wmkv, 0)]], Mm, 2 * MEM_W, D, Mm, _tile(2 * MEM_W, 512), D,
                         [(2 * MEM_W, CDT, 0, None)], ident)
    y_mem = _mem_attn_fwd(proj, memkv)
    ys = [y_hg, y_mla, y_mem]
    tnD = _tile(D, 512, OFF_GATE)

    def epi_branch(accs, tiles, rows):
        return [_sigmoid(tiles[0]) * accs[0] + _sigmoid(tiles[1]) * accs[1] + _sigmoid(tiles[2]) * accs[2]], []

    (merged,) = _fused_mm("branch_fwd", "nn", [[(ys[b], 0, wb[b], 0)] for b in range(3)], S, D, BR_W, tm512, tnD, BR_W,
                          [(D, CDT, 0, None)], epi_branch, tiles=[(proj, OFF_GATE + b * D) for b in range(3)])

    def epi_ln1(accs, tiles, rows):
        r1v = ALPHA * tiles[0] + accs[0]
        return [r1v, _ln_stats(r1v)[0] * rows[0] + rows[1]], []

    r1, h1b = _fused_mm("wo_ln1", "nn", [[(merged, 0, wo, 0)]], S, D, D, tm512, D, _tile(D, 512),
                        [(D, F32, 0, None), (D, CDT, 0, None)], epi_ln1, tiles=[(h0, 0)], rows=[(ln1_g, 0), (ln1_b, 0)])
    tnF = _tile(F, 512)

    def epi_up(accs, tiles, rows):
        gp, up = accs
        return [gp, up, gp * _sigmoid(gp) * up], []

    tm1k, ms1k = _rtile(S, 1024), (2 if S % 2048 == 0 else 1)
    gpb, upb, act = _fused_mm("ffn_up", "nn", [[(h1b, 0, wg, 0)], [(h1b, 0, wu, 0)]], S, F, D, tm1k, tnF, D,
                              [(F, CDT, 0, None)] * 3, epi_up, msplit=ms1k)

    def epi_down(accs, tiles, rows):
        g1, b1, g2, b2 = rows
        h1 = _ln_stats(tiles[0])[0] * g1 + b1
        xh2, rstd2 = _ln_stats(ALPHA * h1 + accs[0])
        diff = xh2 * g2 + b2 - tiles[1]
        dh2 = diff * (1.0 / D)
        dr2v = _ln_bwd(dh2, xh2, rstd2, g2)
        return [dr2v, dr2v], [dh2 * xh2, dh2, diff * diff * (0.5 / D)]

    dr2, dr2b, dg2, db2, lossp = _fused_mm(
        "ffn_down_loss", "nn", [[(act, 0, wd, 0)]], S, D, F, tm512, D, tnF, [(D, F32, 0, None), (D, CDT, 0, None)],
        epi_down, tiles=[(r1, 0), (tgt, 0)], rows=[(ln1_g, 0), (ln1_b, 0), (ln2_g, 0), (ln2_b, 0)], n_racc=3)

    def epi_dact(accs, tiles, rows):
        da, gp, up = accs[0], tiles[0].astype(F32), tiles[1].astype(F32)
        s = _sigmoid(gp)
        return [da * up * (s * (1.0 + gp * (1.0 - s))), da * (gp * s)], []

    dgp, dup = _fused_mm("ffn_dact", "nt", [[(dr2b, 0, wd, 0)]], S, F, D, tm1k, tnF, D, [(F, CDT, 0, None)] * 2,
                         epi_dact, tiles=[(gpb, 0), (upb, 0)], msplit=ms1k)
    tkS = _rtile(S, 1024)
    (d_wd,) = _fused_mm("dw_down", "tn", [[(act, 0, dr2b, 0)]], F, D, S, tnF, D, tkS, [(D, CDT, 0, None)], ident)
    d_wg, d_wu = _fused_mm("dw_gate_up", "tn", [[(h1b, 0, dgp, 0)], [(h1b, 0, dup, 0)]], D, F, S, _tile(D, 1024), tnF, tkS,
                           [(F, CDT, 0, None)] * 2, lambda accs, tiles, rows: (accs, []))

    def epi_dh1(accs, tiles, rows):
        dh1 = accs[0] + ALPHA * tiles[0]
        xh1, rstd1 = _ln_stats(tiles[1])
        dr1v = _ln_bwd(dh1, xh1, rstd1, rows[0])
        return [dr1v, dr1v], [dh1 * xh1, dh1]

    uncols = lambda dw: jnp.transpose(dw.reshape(dw.shape[0], N_DEV, -1), (1, 0, 2))
    dr1, dr1b, dg1, db1, r_wg, r_wu, r_wd = _fused_mm(
        "dh1_ln1", "nt", [[(dgp, 0, wg, 0), (dup, 0, wu, 0)]], S, D, F, tm512, D, tnF,
        [(D, F32, 0, None), (D, CDT, 0, None)], epi_dh1, tiles=[(dr2, 0), (r1, 0)], rows=[(ln1_g, 0)], n_racc=2,
        xchg=_Xchg([uncols(d_wg), uncols(d_wu), d_wd.reshape(N_DEV, -1, D)], True))
    (dmerged,) = _fused_mm("dmerged", "nt", [[(dr1b, 0, wo, 0)]], S, D, D, tm512, _tile(D, 512), D, [(D, CDT, 0, None)], ident)
    (d_wo,) = _fused_mm("dw_o", "tn", [[(merged, 0, dr1b, 0)]], D, D, S, _tile(D, 512), D, tkS, [(D, CDT, 0, None)], ident)

    def epi_dbranch(accs, tiles, rows):
        dm, s = tiles[0].astype(F32), _sigmoid(tiles[1])
        return [dm * s, dm * accs[0] * s * (1.0 - s)], []

    dproj = None
    d_wbs, dys = [], []
    for b in range(3):
        du, dproj = _fused_mm(f"branch_bwd{b}", "nn", [[(ys[b], 0, wb[b], 0)]], S, D, BR_W, tm512, tnD, BR_W,
                              [(D, CDT, 0, None), (PW, CDT, OFF_GATE + b * D, dproj)], epi_dbranch,
                              tiles=[(dmerged, 0), (proj, OFF_GATE + b * D)])
        (dwb,) = _fused_mm(f"dw_branch{b}", "tn", [[(ys[b], 0, du, 0)]], BR_W, D, S, _tile(BR_W, 512), D, tkS,
                           [(D, CDT, 0, None)], ident)
        (dyb,) = _fused_mm(f"dy_branch{b}", "nt", [[(du, 0, wb[b], 0)]], S, BR_W, D, tm512, _tile(BR_W, 512), D,
                           [(BR_W, F32 if b == 0 else CDT, 0, None)], ident)
        d_wbs.append(dwb)
        dys.append(dyb)
    dy_hg, dy_mla, dy_mem = dys

    dproj, dk_mem, dv_mem = _mem_attn_bwd(dy_mem, proj, memkv, dproj)
    dkv_mem = jnp.concatenate([dk_mem, dv_mem], axis=1).astype(CDT)
    (d_wmkv,) = _fused_mm("dw_memkv", "tn", [[(memb, 0, dkv_mem, 0)]], D, 2 * MEM_W, Mm, _tile(D, 512), 2 * MEM_W, Mm,
                          [(2 * MEM_W, CDT, 0, None)], ident)

    delta = _mla_delta(dy_mla, ot)
    dk_cat, dv_h, dq_cat = _mla_attn_bwd(q_cat, k_cat, v_mla, dy_mla, lse, delta)
    dproj, dqp, dkvp, dgq, dgkv = _mla_up_bwd(dq_cat, dk_cat, dv_h, proj, cf, sf, mla_g_cq, mla_g_ckv, wuq_p, wukv, dproj)
    d_wuq_p = _heads_tn("dw_uq", cqn, dqp).astype(F32)
    d_wukv = _heads_tn("dw_ukv", ckvn, dkvp)
    d_wuq = jnp.concatenate([d_wuq_p[..., :MLA_NOPE],
                             d_wuq_p[..., LANE:LANE + MLA_ROPE] + _unrot(d_wuq_p[..., 2 * LANE:2 * LANE + MLA_ROPE])],
                            axis=-1).astype(CDT)

    do_hg, dproj, dng = _hgrn_post_bwd(dy_hg, o_f, o_b, proj, hgrn_norm_g, dproj)
    dproj, dq1, di1, dl_f = _gla_bwd(proj, lbl4, OFF_FF, False, do_hg, st_f, dproj, None, "gla_bwd_f")
    dproj, dq2, di2, dl_b = _gla_bwd(proj, lbl4, OFF_FB, True, do_hg, st_b, dproj, (dq1, di1), "gla_bwd_b")
    dproj = _copy_into("dproj_q", dq2, dproj, OFF_Q)
    dproj = _copy_into("dproj_i", di2, dproj, OFF_I)

    def epi_dh0(accs, tiles, rows):
        dh0 = accs[0] + ALPHA * tiles[0]
        xh, rstd = _ln_stats(tiles[1])
        return [_ln_bwd(dh0, xh, rstd, rows[0])], [dh0 * xh, dh0]

    d_wb = jnp.transpose(jnp.stack(d_wbs).reshape(3, BR_W, N_DEV, -1), (2, 0, 1, 3)).reshape(N_DEV, 3 * BR_W, -1)
    d_win_p, r_uq, r_ukv, r_mkv, r_wb, r_wo = _fused_mm(
        "dw_in", "tn", [[(h0b, 0, dproj, 0)]], D, PW, S, _tile(D, 1024), _tile(PW, 1536), tkS, [(PW, CDT, 0, None)], ident,
        xchg=_Xchg([d_wuq, d_wukv, d_wmkv.reshape(N_DEV, -1, 2 * MEM_W), d_wb, d_wo.reshape(N_DEV, -1, D)], True))
    d_kr = (d_win_p[:, KR:KR + MLA_ROPE].astype(F32) + _unrot(d_win_p[:, KR + LANE:KR + LANE + MLA_ROPE].astype(F32))).astype(CDT)
    d_win = jnp.concatenate([d_win_p[:, :OFF_QM], d_kr, d_win_p[:, OFF_QM:KR]], axis=1)
    grad_x, dge, dbe, r_in = _fused_mm(
        "dh0_ln_emb", "nt", [[(dproj, 0, win_p, 0)]], S, D, PW, tm512, D, _tile(PW, 512), [(D, F32, 0, None)], epi_dh0,
        tiles=[(dr1, 0), (x2, 0)], rows=[(row(ln_emb_g), 0)], n_racc=2, xchg=_Xchg([uncols(d_win)], True))

    recv = [r_in, r_uq, r_ukv, r_mkv, r_wb, r_wo, r_wg, r_wu, r_wd]
    names = ["w_in", "w_uq", "w_ukv", "mem_w_kv", "w_branch", "w_o", "w_gate", "w_up", "w_down"]
    big_out = [_adam_big("adam_" + nme, r, w, m_, v_) for nme, r, w, m_, v_ in zip(names, recv, big_w, big_m, big_v)]

    parts = [dge, dbe, dng, dgq, dgkv, dg1, db1, dg2, db2, dl_f, dl_b, lossp]
    widths = [p.shape[1] for p in parts]
    red = _small_allreduce(jnp.concatenate(parts, axis=1))[0]
    offs = [sum(widths[:i]) for i in range(len(widths))]
    rs = [red[o:o + w_] for o, w_ in zip(offs, widths)]
    g_le_g, g_le_b, g_ng, g_gq, g_gkv, g_l1g, g_l1b, g_l2g, g_l2b, g_dlf, g_dlb, g_loss = rs
    loss = jnp.sum(g_loss)
    g_ng = g_ng.reshape(HG_HEADS, HG_DK).sum(axis=0)
    dl0 = jnp.stack([g_dlf, g_dlb])
    g_lb_full = jnp.stack([dl0, -dl0], axis=1)
    lbw = hgrn_lb_logits.shape[2]
    g_lb = lax.dynamic_slice_in_dim(g_lb_full, me * lbw, lbw, axis=2)

    small_g = [g_le_g, g_le_b, g_lb, g_ng.reshape(1, -1), g_gq.reshape(1, -1), g_gkv.reshape(1, -1), g_l1g.reshape(1, -1),
               g_l1b.reshape(1, -1), g_l2g.reshape(1, -1), g_l2b.reshape(1, -1)]
    small_w = [ln_emb_g, ln_emb_b, hgrn_lb_logits, hgrn_norm_g, mla_g_cq, mla_g_ckv, ln1_g, ln1_b, ln2_g, ln2_b]
    small_m = [m_ln_emb_g, m_ln_emb_b, m_hgrn_lb_logits, m_hgrn_norm_g, m_mla_g_cq, m_mla_g_ckv, m_ln1_g, m_ln1_b, m_ln2_g, m_ln2_b]
    small_v = [v_ln_emb_g, v_ln_emb_b, v_hgrn_lb_logits, v_hgrn_norm_g, v_mla_g_cq, v_mla_g_ckv, v_ln1_g, v_ln1_b, v_ln2_g, v_ln2_b]
    small_g = [g.reshape(w.shape) for g, w in zip(small_g, small_w)]
    pack = lambda lst: jnp.concatenate([a.reshape(-1) for a in lst]).reshape(1, -1)
    s_delta, s_nm, s_nv = _adam_small(pack(small_g), pack(small_w), pack(small_m), pack(small_v))
    sizes = [w.size for w in small_w]
    soffs = [sum(sizes[:i]) for i in range(len(sizes))]
    unpack = lambda p: [p[0, o:o + n].reshape(w.shape) for o, n, w in zip(soffs, sizes, small_w)]
    s_delta, s_nm, s_nv = unpack(s_delta), unpack(s_nm), unpack(s_nv)

    def ordered(small, big):
        sm = list(small)
        bg = [b.reshape(w.shape) for b, w in zip(big, [w_in, mla_w_uq, mla_w_ukv, mem_w_kv, w_branch, w_o, w_ffn_gate, w_ffn_up, w_ffn_down])]
        return [sm[0], sm[1], sm[2], bg[0], sm[3], sm[4], sm[5], bg[1], bg[2], bg[3], bg[4], bg[5], sm[6], sm[7], bg[6], bg[7], bg[8], sm[8], sm[9]]

    grads = ordered(small_g, [o[0] for o in big_out])
    deltas = ordered(s_delta, [o[1] for o in big_out])
    new_m = ordered(s_nm, [o[2] for o in big_out])
    new_v = ordered(s_nv, [o[3] for o in big_out])
    return (loss, grad_x[None], *grads, *deltas, *new_m, *new_v)
```

```python
import functools

import jax
import jax.numpy as jnp
from jax import lax
from jax.experimental import pallas as pl
from jax.experimental.pallas import tpu as pltpu

F32 = jnp.float32
CDT = jnp.bfloat16
MESH = pl.DeviceIdType.MESH
N_DEV = 8
V7X_VMEM_LIMIT = 60 * 1024 * 1024
LANE = 128
SUB = 8

HG_HEADS, HG_DK, HG_CHUNK = 8, 128, 64
HG_HPS = 8
HG_W = HG_HEADS * HG_DK
MLA_HEADS, MLA_RANK, MLA_NOPE, MLA_ROPE, MLA_V = 8, 512, 128, 64, 128
MLA_QK = MLA_NOPE + MLA_ROPE
MLA_SCALE = MLA_QK ** -0.5
MLA_QSCALE = MLA_SCALE * 1.4426950408889634
VT_ROWS = LANE + 16
MLA_TQ = 1024
MLA_W = MLA_HEADS * MLA_V
MEM_HEADS, MEM_HD = 4, 256
MEM_W = MEM_HEADS * MEM_HD
BR_W = 1024
ROPE_THETA = 10000.0
ALPHA = 2.0 ** 0.25
LN_EPS = 1e-5
RMS_EPS = 1e-6
ADAM_LR, ADAM_B1, ADAM_B2, ADAM_EPS, ADAM_WD, ADAM_STEP = 0.001, 0.9, 0.999, 1e-08, 0.01, 10
ADAM_BLOCK_ELEMS = 256 * 1024

OFF_Q, OFF_I, OFF_FB, OFF_FF, OFF_G = 0, 1024, 2048, 3072, 4096
OFF_CQ, OFF_CKV, OFF_QM, OFF_GATE = 5120, 5632, 6144, 7168
KR_PAD = 512


def _cparams(n_grid, side_effects=False):
    return pltpu.CompilerParams(dimension_semantics=("arbitrary",) * n_grid, vmem_limit_bytes=V7X_VMEM_LIMIT,
                                has_side_effects=side_effects)


def _tile(n, pref, *offsets):
    if n <= pref and all(o % n == 0 for o in offsets):
        return n
    t = (min(pref, n) // LANE) * LANE
    while t >= LANE:
        if n % t == 0 and all(o % t == 0 for o in offsets):
            return t
        t -= LANE
    raise ValueError(f"no tile for {n} {pref} {offsets}")


def _rtile(n, pref):
    if n <= pref:
        return n
    t = (pref // SUB) * SUB
    while t >= SUB:
        if n % t == 0:
            return t
        t -= SUB
    raise ValueError(f"no row tile for {n} {pref}")


def _dot(a, b, dims):
    return lax.dot_general(a.astype(CDT), b.astype(CDT), (dims, ((), ())), preferred_element_type=F32)


def _nn(a, b):
    return _dot(a, b, ((1,), (0,)))


def _nt(a, b):
    return _dot(a, b, ((1,), (1,)))


def _tn(a, b):
    return _dot(a, b, ((0,), (0,)))


_DOTS = {"nn": _nn, "nt": _nt, "tn": _tn}


def _sigmoid(x):
    return 1.0 / (1.0 + jnp.exp(-x))


def _rowsum8(v):
    r, w = v.shape
    return v.reshape(r // SUB, SUB, w).sum(axis=0)


def _my_place():
    x, y, c = lax.axis_index("x"), lax.axis_index("y"), lax.axis_index("c")
    return x, y, c, 4 * x + 2 * y + c


def _peer(x, y, c, kk):
    px = 1 - x if kk & 4 else x
    py = 1 - y if kk & 2 else y
    pc = 1 - c if kk & 1 else c
    return (px, py, pc), 4 * px + 2 * py + pc


class _Xchg:
    def __init__(self, arrs, scatter):
        self.arrs, self.scatter, self.n = list(arrs), scatter, len(arrs)
        hbm = pl.BlockSpec(memory_space=pl.ANY)
        self.specs = [hbm] * self.n
        self.out_shape = [jax.ShapeDtypeStruct(((N_DEV,) + a.shape[1:]) if scatter else ((N_DEV,) + a.shape), a.dtype)
                          for a in self.arrs]
        ncp = self.n * (N_DEV - 1)
        self.scratch = [pltpu.SemaphoreType.DMA((ncp,)), pltpu.SemaphoreType.DMA((ncp,)), pltpu.SemaphoreType.DMA((self.n,))]

    def _copies(self, ins, outs, send, recv, loc):
        x, y, c, me = _my_place()
        copies = []
        for w in range(self.n):
            copies.append(pltpu.make_async_copy(ins[w].at[me] if self.scatter else ins[w], outs[w].at[me], loc.at[w]))
            for kk in range(1, N_DEV):
                peer, pid = _peer(x, y, c, kk)
                s = w * (N_DEV - 1) + kk - 1
                copies.append(pltpu.make_async_remote_copy(
                    src_ref=ins[w].at[pid] if self.scatter else ins[w], dst_ref=outs[w].at[me],
                    send_sem=send.at[s], recv_sem=recv.at[s], device_id=peer, device_id_type=MESH))
        return copies

    def start(self, ins, outs, sems):
        for cp in self._copies(ins, outs, *sems):
            cp.start()

    def wait(self, ins, outs, sems):
        for cp in self._copies(ins, outs, *sems):
            cp.wait()


def _all_gather_two_level(name, arrs):
    n = len(arrs)
    NC = N_DEV - 1

    def body(*refs):
        ins, outs = refs[:n], refs[n:2 * n]
        send, recv, loc = refs[2 * n:]
        x, y, c, me = _my_place()
        sibling = (x, y, 1 - c)
        chips = [(1 - x, y), (x, 1 - y), (1 - x, 1 - y)]
        slot = lambda px, py, pc: 4 * px + 2 * py + pc

        def copy(w, k, block, to, src=None):
            dst = outs[w].at[slot(*block)]
            return pltpu.make_async_remote_copy(src_ref=dst if src is None else src, dst_ref=dst,
                                                send_sem=send.at[w * NC + k], recv_sem=recv.at[w * NC + k],
                                                device_id=to, device_id_type=MESH)

        mine = [pltpu.make_async_copy(ins[w], outs[w].at[me], loc.at[w]) for w in range(n)]
        for cp in mine:
            cp.start()
        first = []
        for w in range(n):
            first.append(copy(w, 0, (x, y, c), sibling, src=ins[w]))
            first += [copy(w, 1 + j, (x, y, c), (*chip, c), src=ins[w]) for j, chip in enumerate(chips)]
        for cp in first:
            cp.start()
        passed = []
        for j, chip in enumerate(chips):
            for w in range(n):
                copy(w, 1 + j, (*chip, c), (x, y, c)).wait_recv()
                fwd = copy(w, 4 + j, (*chip, c), sibling)
                fwd.start()
                passed.append(fwd)
        for w in range(n):
            copy(w, 0, sibling, (x, y, c)).wait_recv()
            for j, chip in enumerate(chips):
                copy(w, 4 + j, (*chip, 1 - c), (x, y, c)).wait_recv()
        for cp in first + passed:
            cp.wait_send()
        for cp in mine:
            cp.wait()

    hbm = pl.BlockSpec(memory_space=pl.ANY)
    return pl.pallas_call(
        body,
        name=name,
        in_specs=[hbm] * n,
        out_specs=[hbm] * n,
        out_shape=[jax.ShapeDtypeStruct((N_DEV,) + a.shape, a.dtype) for a in arrs],
        scratch_shapes=[pltpu.SemaphoreType.DMA((n * NC,)), pltpu.SemaphoreType.DMA((n * NC,)), pltpu.SemaphoreType.DMA((n,))],
        compiler_params=pltpu.CompilerParams(has_side_effects=True),
    )(*arrs)


def _fused_mm(name, mode, groups, M, N, K, tm, tn, tk, outs, epi, tiles=(), rows=(), n_racc=0, xchg=None, msplit=1):
    ni, nj, nk = M // tm, N // tn, K // tk
    assert M % tm == 0 and N % tn == 0 and K % tk == 0, (name, M, N, K, tm, tn, tk)
    assert n_racc == 0 or nj == 1
    assert msplit == 1 or (nk == 1 and n_racc == 0 and tm % (16 * msplit) == 0)
    dot = _DOTS[mode] if groups else None
    ins, in_specs = [], []
    for g in groups:
        for a, a_off, b, b_off in g:
            if mode == "tn":
                assert a_off % tm == 0
                in_specs.append(pl.BlockSpec((tk, tm), lambda i, j, k, o=a_off // tm: (k, i + o)))
            else:
                assert a_off % tk == 0
                in_specs.append(pl.BlockSpec((tm, tk), lambda i, j, k, o=a_off // tk: (i, k + o)))
            ins.append(a)
            if mode == "nt":
                assert b_off % tk == 0
                in_specs.append(pl.BlockSpec((tn, tk), lambda i, j, k, o=b_off // tk: (j, k + o)))
            else:
                assert b_off % tn == 0
                in_specs.append(pl.BlockSpec((tk, tn), lambda i, j, k, o=b_off // tn: (k, j + o)))
            ins.append(b)
    for arr, off in tiles:
        assert off % tn == 0
        ins.append(arr)
        in_specs.append(pl.BlockSpec((tm, tn), lambda i, j, k, o=off // tn: (i, j + o)))
    for arr, off in rows:
        assert off % tn == 0
        ins.append(arr)
        in_specs.append(pl.BlockSpec((1, tn), lambda i, j, k, o=off // tn: (0, j + o)))
    aliases = {}
    out_shape, out_specs = [], []
    for oi, (width, dtype, off, alias) in enumerate(outs):
        assert off % tn == 0
        if alias is not None:
            aliases[len(ins)] = oi
            ins.append(alias)
            in_specs.append(pl.BlockSpec(memory_space=pl.ANY))
        out_shape.append(jax.ShapeDtypeStruct((M, width), dtype))
        out_specs.append(pl.BlockSpec((tm, tn), lambda i, j, k, o=off // tn: (i, j + o)))
    for _ in range(n_racc):
        out_shape.append(jax.ShapeDtypeStruct((SUB, N), F32))
        out_specs.append(pl.BlockSpec((SUB, tn), lambda i, j, k: (0, 0)))
    n_alias = len(aliases)
    n_pairs = [len(g) for g in groups]
    use_scratch = nk > 1
    scratch = [pltpu.VMEM((tm, tn), F32) for _ in groups] if use_scratch else []
    nx = 0
    if xchg is not None:
        nx = xchg.n
        ins += xchg.arrs
        in_specs += xchg.specs
        out_shape += xchg.out_shape
        out_specs += xchg.specs
        scratch += xchg.scratch

    def body(*refs):
        it = iter(refs)
        pair_refs = [[(next(it), next(it)) for _ in range(n)] for n in n_pairs]
        tile_refs = [next(it) for _ in tiles]
        row_refs = [next(it) for _ in rows]
        for _ in range(n_alias):
            next(it)
        x_in = [next(it) for _ in range(nx)]
        out_refs = [next(it) for _ in outs]
        racc_refs = [next(it) for _ in range(n_racc)]
        x_out = [next(it) for _ in range(nx)]
        acc_refs = [next(it) for _ in groups] if use_scratch else []
        x_sems = list(it)
        i, j, k = pl.program_id(0), pl.program_id(1), pl.program_id(2)
        if nx:
            @pl.when((i == 0) & (j == 0) & (k == 0))
            def _():
                xchg.start(x_in, x_out, x_sems)

        def products():
            res = []
            for prs in pair_refs:
                s = None
                for a_ref, b_ref in prs:
                    d = dot(a_ref[...], b_ref[...])
                    s = d if s is None else s + d
                res.append(s)
            return res

        def finish(accs):
            out_v, racc_v = epi(accs, [t[...] for t in tile_refs], [r[...] for r in row_refs])
            for o_ref, v in zip(out_refs, out_v):
                o_ref[...] = v.astype(o_ref.dtype)
            for r_ref, v in zip(racc_refs, racc_v):
                part = _rowsum8(v)

                @pl.when(i == 0)
                def _():
                    r_ref[...] = part

                @pl.when(i > 0)
                def _():
                    r_ref[...] += part

        if not use_scratch and msplit > 1:
            ts = tm // msplit
            for s in range(msplit):
                rs = pl.ds(s * ts, ts)
                accs = []
                for prs in pair_refs:
                    acc = None
                    for a_ref, b_ref in prs:
                        dd = dot(a_ref[:, rs] if mode == "tn" else a_ref[rs, :], b_ref[...])
                        acc = dd if acc is None else acc + dd
                    accs.append(acc)
                out_v, _ = epi(accs, [t[rs, :] for t in tile_refs], [r[...] for r in row_refs])
                for o_ref, v in zip(out_refs, out_v):
                    o_ref[rs, :] = v.astype(o_ref.dtype)
        elif not use_scratch:
            finish(products())
        else:
            @pl.when(k == 0)
            def _():
                for acc in acc_refs:
                    acc[...] = jnp.zeros_like(acc)

            for acc, p in zip(acc_refs, products()):
                acc[...] += p

            @pl.when(k == nk - 1)
            def _():
                finish([acc[...] for acc in acc_refs])

        if nx:
            @pl.when((i == ni - 1) & (j == nj - 1) & (k == nk - 1))
            def _():
                xchg.wait(x_in, x_out, x_sems)

    res = pl.pallas_call(
        body,
        name=name,
        grid=(ni, nj, nk),
        in_specs=in_specs,
        out_specs=out_specs,
        out_shape=out_shape,
        scratch_shapes=scratch,
        input_output_aliases=aliases,
        compiler_params=_cparams(3, side_effects=nx > 0),
    )(*ins)
    return res


def _ln_stats(r):
    mu = jnp.mean(r, axis=-1, keepdims=True)
    xc = r - mu
    var = jnp.mean(xc * xc, axis=-1, keepdims=True)
    rstd = lax.rsqrt(var + LN_EPS)
    return xc * rstd, rstd


def _ln_bwd(dh, xhat, rstd, g):
    dxh = dh * g
    m1 = jnp.mean(dxh, axis=-1, keepdims=True)
    m2 = jnp.mean(dxh * xhat, axis=-1, keepdims=True)
    return rstd * (dxh - m1 - xhat * m2)


def _split3(x):
    hi = x.astype(CDT)
    r1 = x - hi.astype(F32)
    mid = r1.astype(CDT)
    lo = (r1 - mid.astype(F32)).astype(CDT)
    return hi, mid, lo


def _tri_matmul(tri, x):
    hi, mid, lo = _split3(x)
    return _nn(tri, hi) + _nn(tri, mid) + _nn(tri, lo)


def _dot3(dot, a, b):
    a_hi, b_hi = a.astype(CDT), b.astype(CDT)
    a_lo = (a - a_hi.astype(F32)).astype(CDT)
    b_lo = (b - b_hi.astype(F32)).astype(CDT)
    return dot(a_hi, b_hi) + dot(a_hi, b_lo) + dot(a_lo, b_hi)


def _gla_masks(reverse):
    C = HG_CHUNK
    r = lax.broadcasted_iota(jnp.int32, (C, C), 0)
    c = lax.broadcasted_iota(jnp.int32, (C, C), 1)
    keep = (c >= r) if reverse else (r >= c)
    return keep


def _m(fn, *lists):
    return [fn(*args) for args in zip(*lists)]


def _gla_chunk_fwd(qraw, fraw, lb, keep, reverse):
    C = HG_CHUNK
    end = 0 if reverse else C - 1
    tri = jnp.where(keep, 1.0, 0.0).astype(CDT)
    sq = _m(_sigmoid, qraw)
    q = _m(lambda x, s: x * s, qraw, sq)
    sg = _m(_sigmoid, fraw)
    f = _m(lambda l_, s: l_ + (1.0 - l_) * s, lb, sg)
    k = _m(lambda x: 1.0 - x, f)
    g = _m(jnp.log, f)
    b = _m(lambda x: _tri_matmul(tri, x), g)
    b_end = _m(lambda x: x[end:end + 1, :], b)
    b_mid = _m(lambda x: x[C // 2:C // 2 + 1, :], b)
    eq = _m(lambda x, m_: jnp.exp(x - m_), b, b_mid)
    ek = _m(lambda x, m_: jnp.exp(m_ - x), b, b_mid)
    eb = _m(jnp.exp, b)
    e2 = _m(lambda x, e_: jnp.exp(e_ - x), b, b_end)
    e_end = _m(jnp.exp, b_end)
    qt = _m(lambda x, e_: x * e_, q, eq)
    kt = _m(lambda x, e_: x * e_, k, ek)
    qs = _m(lambda x, e_: (x * e_).astype(CDT), q, eb)
    k2 = _m(lambda x, e_: (x * e_).astype(CDT), k, e2)
    a = _m(lambda x, y: jnp.where(keep, _dot3(_nt, x, y), 0.0).astype(CDT), qt, kt)
    return dict(sq=sq, q=q, sg=sg, f=f, k=k, eq=eq, ek=ek, eb=eb, e2=e2, e_end=e_end, qt=qt, kt=kt, qs=qs, k2=k2, a=a)


def _gla_fwd(proj, lbl4, f_off, reverse, name):
    S = proj.shape[0]
    C = HG_CHUNK
    R = _rtile(S, 512)
    cpb, nblk = R // C, S // R
    d = 1 if reverse else 0
    blk_map = (lambda b: nblk - 1 - b) if reverse else (lambda b: b)

    W = HG_HPS * HG_DK

    def body(q_ref, i_ref, f_ref, lb_ref, o_ref, st_ref, s_scr):
        @pl.when(pl.program_id(1) == 0)
        def _():
            s_scr[...] = jnp.zeros_like(s_scr)

        l = lb_ref[...]
        lbs = _sigmoid(l[2 * d:2 * d + 1, :] - l[2 * d + 1:2 * d + 2, :])
        keep = _gla_masks(reverse)
        heads = list(range(HG_HPS))
        css = [pl.ds(hh * HG_DK, HG_DK) for hh in heads]
        lb = [lbs[:, hh * HG_DK:(hh + 1) * HG_DK] for hh in heads]
        for cc in range(cpb):
            c = cpb - 1 - cc if reverse else cc
            sl = pl.ds(c * C, C)
            v = [i_ref[sl, cs] for cs in css]
            t = _gla_chunk_fwd([q_ref[sl, cs] for cs in css], [f_ref[sl, cs] for cs in css], lb, keep, reverse)
            st = [s_scr[hh] for hh in heads]
            o = _m(lambda qs, s_, a, v_: _nt(qs, s_) + _nn(a, v_), t["qs"], st, t["a"], v)
            new = _m(lambda e_, s_, v_, k2: e_ * s_ + _tn(v_, k2), t["e_end"], st, v, t["k2"])
            for hh in heads:
                st_ref[c, hh] = st[hh]
                o_ref[sl, css[hh]] = o[hh]
                s_scr[hh] = new[hh]

    col = lambda off: (lambda h, b: (blk_map(b), off // W + h))
    return pl.pallas_call(
        body,
        name=name,
        grid=(HG_HEADS // HG_HPS, nblk),
        in_specs=[
            pl.BlockSpec((R, W), col(OFF_Q)),
            pl.BlockSpec((R, W), col(OFF_I)),
            pl.BlockSpec((R, W), col(f_off)),
            pl.BlockSpec((4, W), lambda h, b: (0, h)),
        ],
        out_specs=[
            pl.BlockSpec((R, W), lambda h, b: (blk_map(b), h)),
            pl.BlockSpec((cpb, HG_HPS, HG_DK, HG_DK), lambda h, b: (blk_map(b), h, 0, 0)),
        ],
        out_shape=[
            jax.ShapeDtypeStruct((S, HG_W), F32),
            jax.ShapeDtypeStruct((S // C, HG_HEADS, HG_DK, HG_DK), F32),
        ],
        scratch_shapes=[pltpu.VMEM((HG_HPS, HG_DK, HG_DK), F32)],
        compiler_params=_cparams(2),
    )(proj, proj, proj, lbl4)


def _gla_bwd(proj, lbl4, f_off, reverse, do, states, dproj, prev, name):
    S = proj.shape[0]
    PW = proj.shape[1]
    C = HG_CHUNK
    R = _rtile(S, 512)
    cpb, nblk = R // C, S // R
    d = 1 if reverse else 0
    blk_map = (lambda b: b) if reverse else (lambda b: nblk - 1 - b)
    final = prev is not None

    if final:
        assert HG_HPS == HG_HEADS and (OFF_Q, OFF_I, f_off) == (0, HG_W, 2 * HG_W)

    def body(*refs):
        if final:
            q_ref, i_ref, f_ref, lb_ref, do_ref, st_ref, pq_ref, pi_ref, _dp, o3_ref, dl_ref, ds_scr = refs
            dq_ref = di_ref = df_ref = o3_ref
        else:
            q_ref, i_ref, f_ref, lb_ref, do_ref, st_ref, dq_ref, di_ref, df_ref, dl_ref, ds_scr = refs
        out_off = (OFF_Q, OFF_I, f_off) if final else (0, 0, 0)
        blk = pl.program_id(1)

        @pl.when(blk == 0)
        def _():
            ds_scr[...] = jnp.zeros_like(ds_scr)
            dl_ref[...] = jnp.zeros_like(dl_ref)

        l = lb_ref[...]
        lbs = _sigmoid(l[2 * d:2 * d + 1, :] - l[2 * d + 1:2 * d + 2, :])
        keep = _gla_masks(reverse)
        keep_t = _gla_masks(not reverse)
        tri_t = jnp.where(keep_t, 1.0, 0.0).astype(CDT)
        end = 0 if reverse else C - 1
        is_end = lax.broadcasted_iota(jnp.int32, (C, HG_DK), 0) == end
        dl_all = [jnp.zeros((SUB, HG_DK), F32) for _ in range(HG_HPS)]
        for cc, heads in [(cc, [hh]) for cc in range(cpb) for hh in range(HG_HPS)]:
            css = [pl.ds(hh * HG_DK, HG_DK) for hh in heads]
            lb = [lbs[:, hh * HG_DK:(hh + 1) * HG_DK] for hh in heads]
            dl_acc = [dl_all[hh] for hh in heads]
            c = cc if reverse else cpb - 1 - cc
            sl = pl.ds(c * C, C)
            qraw = [q_ref[sl, cs] for cs in css]
            v = [i_ref[sl, cs] for cs in css]
            t = _gla_chunk_fwd(qraw, [f_ref[sl, cs] for cs in css], lb, keep, reverse)
            dob = [do_ref[sl, cs].astype(CDT) for cs in css]
            vb = _m(lambda x: x.astype(CDT), v)
            st = [st_ref[c, hh] for hh in heads]
            ds = [ds_scr[hh] for hh in heads]
            dsb = _m(lambda x: x.astype(CDT), ds)
            d_qs = _m(_nn, dob, st)
            d_a = _m(lambda x, y: jnp.where(keep, _nt(x, y), 0.0), dob, vb)
            d_qt = _m(lambda x, y: _dot3(_nn, x, y), d_a, t["kt"])
            d_kt = _m(lambda x, y: _dot3(_tn, x, y), d_a, t["qt"])
            d_v = _m(lambda a, x, k2, s_: _tn(a, x) + _nt(k2, s_), t["a"], dob, t["k2"], dsb)
            d_k2 = _m(_nn, vb, dsb)
            d_e = _m(lambda s_, x: jnp.sum(s_ * x, axis=0, keepdims=True), st, ds)
            new_ds = _m(lambda e_, x, y, qs: e_ * x + _tn(y, qs), t["e_end"], ds, dob, t["qs"])
            dq = _m(lambda a, ea, b_, eb_: a * ea + b_ * eb_, d_qt, t["eq"], d_qs, t["eb"])
            dk = _m(lambda a, ea, b_, eb_: a * ea + b_ * eb_, d_kt, t["ek"], d_k2, t["e2"])
            db_end = _m(lambda x, k_, e2, de, ee: jnp.sum(x * (k_ * e2), axis=0, keepdims=True) + de * ee,
                        d_k2, t["k"], t["e2"], d_e, t["e_end"])
            db = _m(lambda q_, dq_, k_, dk_, be: q_ * dq_ - k_ * dk_ + jnp.where(is_end, be, 0.0),
                    t["q"], dq, t["k"], dk, db_end)
            dg = _m(lambda x: _tri_matmul(tri_t, x), db)
            df = _m(lambda g_, f_, dk_: g_ / f_ - dk_, dg, t["f"], dk)
            dfraw = _m(lambda x, l_, s_: x * (1.0 - l_) * s_ * (1.0 - s_), df, lb, t["sg"])
            dl_acc = _m(lambda acc, x, s_: acc + _rowsum8(x * (1.0 - s_)), dl_acc, df, t["sg"])
            dqraw = _m(lambda x, s_, r: x * (s_ * (1.0 + r * (1.0 - s_))), dq, t["sq"], qraw)
            if final:
                dqraw = [x + pq_ref[sl, cs] for x, cs in zip(dqraw, css)]
                d_v = [x + pi_ref[sl, cs] for x, cs in zip(d_v, css)]
            for n, hh in enumerate(heads):
                dl_all[hh] = dl_acc[n]
                ds_scr[hh] = new_ds[n]
                for ref, off, val in zip((dq_ref, di_ref, df_ref), out_off, (dqraw[n], d_v[n], dfraw[n])):
                    ref[sl, pl.ds(off + hh * HG_DK, HG_DK)] = val.astype(ref.dtype)
        dl_ref[...] += jnp.concatenate(dl_all, axis=1) * (lbs * (1.0 - lbs))

    W = HG_HPS * HG_DK
    col = lambda off: (lambda h, b: (blk_map(b), off // W + h))
    blk = lambda: pl.BlockSpec((R, W), lambda h, b: (blk_map(b), h))
    ins = [proj, proj, proj, lbl4, do, states]
    in_specs = [
        pl.BlockSpec((R, W), col(OFF_Q)),
        pl.BlockSpec((R, W), col(OFF_I)),
        pl.BlockSpec((R, W), col(f_off)),
        pl.BlockSpec((4, W), lambda h, b: (0, h)),
        blk(),
        pl.BlockSpec((cpb, HG_HPS, HG_DK, HG_DK), lambda h, b: (blk_map(b), h, 0, 0)),
    ]
    dl_shape = jax.ShapeDtypeStruct((SUB, HG_W), F32)
    dl_spec = pl.BlockSpec((SUB, W), lambda h, b: (0, h))
    dp_shape = jax.ShapeDtypeStruct((S, PW), CDT)
    if final:
        ins += [prev[0], prev[1], dproj]
        in_specs += [blk(), blk(), pl.BlockSpec(memory_space=pl.ANY)]
        out_shape = [dp_shape, dl_shape]
        out_specs = [pl.BlockSpec((R, 3 * HG_W), lambda h, b: (blk_map(b), 0)), dl_spec]
        aliases = {8: 0}
    else:
        out_shape = [jax.ShapeDtypeStruct((S, HG_W), F32), jax.ShapeDtypeStruct((S, HG_W), F32), dp_shape, dl_shape]
        out_specs = [blk(), blk(), pl.BlockSpec((R, W), col(f_off)), dl_spec]
        aliases = {}
        if dproj is not None:
            ins += [dproj]
            in_specs += [pl.BlockSpec(memory_space=pl.ANY)]
            aliases = {6: 2}
    if (not final) and dproj is not None:
        def body_wrapped(*refs, _b=body):
            _b(*refs[:6], *refs[7:])
        kern = body_wrapped
    else:
        kern = body
    res = pl.pallas_call(
        kern,
        name=name,
        grid=(HG_HEADS // HG_HPS, nblk),
        in_specs=in_specs,
        out_specs=out_specs,
        out_shape=out_shape,
        scratch_shapes=[pltpu.VMEM((HG_HPS, HG_DK, HG_DK), F32)],
        input_output_aliases=aliases,
        compiler_params=_cparams(2),
    )(*ins)
    if final:
        return res[0], None, None, res[1]
    dq, di, dproj, dl = res
    return dproj, dq, di, dl


def _hgrn_post_fwd(o_f, o_b, proj, norm_g):
    S = o_f.shape[0]

    def epi(accs, tiles, rows):
        of, ob, graw = tiles
        ng = rows[0][:, :HG_DK]
        o = of + ob
        ys = []
        for h in range(HG_HEADS):
            oh = o[:, h * HG_DK:(h + 1) * HG_DK]
            rs = lax.rsqrt(jnp.mean(oh * oh, axis=-1, keepdims=True) + RMS_EPS)
            ys.append(oh * rs * ng * _sigmoid(graw[:, h * HG_DK:(h + 1) * HG_DK]))
        return [jnp.concatenate(ys, axis=1)], []

    tm = _rtile(S, 512)
    (y,) = _fused_mm("hgrn_post_fwd", "nn", [], S, HG_W, 1, tm, HG_W, 1, [(HG_W, CDT, 0, None)], epi,
                     tiles=[(o_f, 0), (o_b, 0), (proj, OFF_G)], rows=[(jnp.tile(norm_g, (1, HG_HEADS)), 0)])
    return y


def _hgrn_post_bwd(dy, o_f, o_b, proj, norm_g, dproj):
    S = o_f.shape[0]

    def epi(accs, tiles, rows):
        dyv, of, ob, graw = tiles
        ng = rows[0][:, :HG_DK]
        o = of + ob
        dos, dgs, dns = [], [], []
        for h in range(HG_HEADS):
            sl = slice(h * HG_DK, (h + 1) * HG_DK)
            oh, gh, dyh = o[:, sl], graw[:, sl], dyv[:, sl].astype(F32)
            rs = lax.rsqrt(jnp.mean(oh * oh, axis=-1, keepdims=True) + RMS_EPS)
            xh = oh * rs
            sg = _sigmoid(gh)
            dn = dyh * sg
            dgs.append(dyh * (xh * ng) * sg * (1.0 - sg))
            dns.append(dn * xh)
            dxh = dn * ng
            dos.append(rs * (dxh - xh * jnp.mean(dxh * xh, axis=-1, keepdims=True)))
        return [jnp.concatenate(dos, axis=1), jnp.concatenate(dgs, axis=1)], [jnp.concatenate(dns, axis=1)]

    tm = _rtile(S, 512)
    do, dproj, dn = _fused_mm("hgrn_post_bwd", "nn", [], S, HG_W, 1, tm, HG_W, 1,
                              [(HG_W, F32, 0, None), (dproj.shape[1], CDT, OFF_G, dproj)], epi,
                              tiles=[(dy, 0), (o_f, 0), (o_b, 0), (proj, OFF_G)],
                              rows=[(jnp.tile(norm_g, (1, HG_HEADS)), 0)], n_racc=1)
    return do, dproj, dn


def _copy_into(name, src, dst, off):
    S, W = src.shape
    tm = _rtile(S, 512)
    (dst,) = _fused_mm(name, "nn", [], S, W, 1, tm, W, 1, [(dst.shape[1], dst.dtype, off, dst)],
                       lambda accs, tiles, rows: ([tiles[0]], []), tiles=[(src, 0)])
    return dst


def _rms_stats(x):
    rs = lax.rsqrt(jnp.mean(x * x, axis=-1, keepdims=True) + RMS_EPS)
    return x * rs, rs


def _mla_up(proj, cf, sf, g_cq, g_ckv, wuq_p, wukv):
    S = proj.shape[0]
    tm = _rtile(S, 512)
    H = MLA_HEADS

    def body(cq_ref, ckv_ref, kr_ref, krot_ref, cf_ref, sf_ref, gq_ref, gkv_ref, wq_ref, wkv_ref,
             q_ref, k_ref, v_ref, vt_ref, cqn_ref, ckvn_ref):
        cqn = (_rms_stats(cq_ref[...])[0] * gq_ref[...]).astype(CDT)
        ckvn = (_rms_stats(ckv_ref[...])[0] * gkv_ref[...]).astype(CDT)
        cqn_ref[...] = cqn
        ckvn_ref[...] = ckvn
        cfv, sfv = cf_ref[...], sf_ref[...]
        r = _nn(cqn, wq_ref[0]) * MLA_QSCALE
        q_ref[0, :, 0:LANE] = r[:, 0:LANE].astype(CDT)
        q_ref[0, :, LANE:2 * LANE] = (r[:, LANE:2 * LANE] * cfv + r[:, 2 * LANE:3 * LANE] * sfv).astype(CDT)
        kv = _nn(ckvn, wkv_ref[0])
        k_ref[0, :, 0:LANE] = kv[:, 0:LANE].astype(CDT)
        k_ref[0, :, LANE:2 * LANE] = (kr_ref[...] * cfv + krot_ref[...] * sfv).astype(CDT)
        vv = kv[:, LANE:2 * LANE]
        v_ref[0] = vv.astype(CDT)
        vt_ref[0, 0, 0:LANE, :] = vv.T.astype(CDT)
        vt_ref[0, 0, LANE:VT_ROWS, :] = jnp.ones((VT_ROWS - LANE, tm), CDT)

    PWb = proj.shape[1]
    kr_off = PWb - KR_PAD
    cspec = lambda off, w: pl.BlockSpec((tm, w), lambda i, h, o=off // w: (i, o))
    return pl.pallas_call(
        body,
        name="mla_up_fwd",
        grid=(S // tm, H),
        in_specs=[
            cspec(OFF_CQ, MLA_RANK), cspec(OFF_CKV, MLA_RANK), cspec(kr_off, LANE), cspec(kr_off + LANE, LANE),
            pl.BlockSpec((tm, LANE), lambda i, h: (i, 0)), pl.BlockSpec((tm, LANE), lambda i, h: (i, 0)),
            pl.BlockSpec((1, MLA_RANK), lambda i, h: (0, 0)), pl.BlockSpec((1, MLA_RANK), lambda i, h: (0, 0)),
            pl.BlockSpec((1, MLA_RANK, 3 * LANE), lambda i, h: (h, 0, 0)),
            pl.BlockSpec((1, MLA_RANK, 2 * LANE), lambda i, h: (h, 0, 0)),
        ],
        out_specs=[
            pl.BlockSpec((1, tm, 2 * LANE), lambda i, h: (h, i, 0)),
            pl.BlockSpec((1, tm, 2 * LANE), lambda i, h: (h, i, 0)),
            pl.BlockSpec((1, tm, LANE), lambda i, h: (h, i, 0)),
            pl.BlockSpec((1, 1, VT_ROWS, tm), lambda i, h: (h, i, 0, 0)),
            pl.BlockSpec((tm, MLA_RANK), lambda i, h: (i, 0)),
            pl.BlockSpec((tm, MLA_RANK), lambda i, h: (i, 0)),
        ],
        out_shape=[
            jax.ShapeDtypeStruct((H, S, 2 * LANE), CDT), jax.ShapeDtypeStruct((H, S, 2 * LANE), CDT),
            jax.ShapeDtypeStruct((H, S, LANE), CDT), jax.ShapeDtypeStruct((H, S // tm, VT_ROWS, tm), CDT),
            jax.ShapeDtypeStruct((S, MLA_RANK), CDT), jax.ShapeDtypeStruct((S, MLA_RANK), CDT),
        ],
        compiler_params=_cparams(2),
    )(proj, proj, proj, proj, cf, sf, g_cq, g_ckv, wuq_p, wukv)


def _mla_attn_fwd(q_cat, k_cat, vt, xchg=None):
    H, S, _ = q_cat.shape
    tq = _tile(S, MLA_TQ)
    _, nkb, _, tk = vt.shape
    nq = S // tq
    nx = xchg.n if xchg is not None else 0

    def body(*refs):
        q_ref, k_ref, vt_ref = refs[:3]
        x_in = refs[3:3 + nx]
        y_ref, ot_ref, lse_ref = refs[3 + nx:6 + nx]
        x_out = refs[6 + nx:6 + 2 * nx]
        m_scr, acc_scr = refs[6 + 2 * nx:8 + 2 * nx]
        x_sems = refs[8 + 2 * nx:]
        h, i = pl.program_id(0), pl.program_id(1)
        if nx:
            @pl.when((h == 0) & (i == 0))
            def _():
                xchg.start(x_in, x_out, x_sems)

        q = q_ref[0]
        m_scr[...] = jnp.full_like(m_scr, -jnp.inf)
        acc_scr[...] = jnp.zeros_like(acc_scr)

        def step(j, carry):
            kj = k_ref[0, pl.ds(pl.multiple_of(j * tk, tk), tk), :]
            st = _nt(kj, q)
            m_old = m_scr[...]
            m_new = jnp.maximum(m_old, jnp.max(st, axis=0, keepdims=True))
            pt = jnp.exp2(st - m_new)
            acc_scr[...] = jnp.exp2(m_old - m_new) * acc_scr[...] + _nn(vt_ref[0, j], pt)
            m_scr[...] = m_new
            return carry

        lax.fori_loop(0, nkb, step, 0, unroll=4 if nkb % 4 == 0 else 1)
        l = acc_scr[LANE:LANE + 1, :]
        ot = acc_scr[0:LANE, :] / l
        ot_ref[0] = ot
        y_ref[...] = ot.T.astype(CDT)
        lse_ref[0, 0] = m_scr[...] + jnp.log2(l)

        if nx:
            @pl.when((h == H - 1) & (i == nq - 1))
            def _():
                xchg.wait(x_in, x_out, x_sems)

    return pl.pallas_call(
        body,
        name="mla_attn_fwd",
        grid=(H, nq),
        in_specs=[
            pl.BlockSpec((1, tq, 2 * LANE), lambda h, i: (h, i, 0)),
            pl.BlockSpec((1, S, 2 * LANE), lambda h, i: (h, 0, 0)),
            pl.BlockSpec((1, nkb, VT_ROWS, tk), lambda h, i: (h, 0, 0, 0)),
        ] + (xchg.specs if nx else []),
        out_specs=[
            pl.BlockSpec((tq, LANE), lambda h, i: (i, h)),
            pl.BlockSpec((1, LANE, tq), lambda h, i: (h, 0, i)),
            pl.BlockSpec((1, 1, 1, tq), lambda h, i: (h, i, 0, 0)),
        ] + (xchg.specs if nx else []),
        out_shape=[
            jax.ShapeDtypeStruct((S, H * LANE), CDT),
            jax.ShapeDtypeStruct((H, LANE, S), F32),
            jax.ShapeDtypeStruct((H, nq, 1, tq), F32),
        ] + (xchg.out_shape if nx else []),
        scratch_shapes=[pltpu.VMEM((1, tq), F32), pltpu.VMEM((VT_ROWS, tq), F32)] + (xchg.scratch if nx else []),
        compiler_params=_cparams(2, side_effects=nx > 0),
    )(q_cat, k_cat, vt, *(xchg.arrs if nx else []))


def _mla_delta(dy, ot):
    H, _, S = ot.shape
    tq = _tile(S, MLA_TQ)
    nq = S // tq

    def body(dy_ref, ot_ref, d_ref):
        d_ref[0, 0] = jnp.sum(dy_ref[...].astype(F32).T * ot_ref[0], axis=0, keepdims=True)

    return pl.pallas_call(
        body,
        name="mla_delta",
        grid=(H, nq),
        in_specs=[pl.BlockSpec((tq, LANE), lambda h, i: (i, h)), pl.BlockSpec((1, LANE, tq), lambda h, i: (h, 0, i))],
        out_specs=pl.BlockSpec((1, 1, 1, tq), lambda h, i: (h, i, 0, 0)),
        out_shape=jax.ShapeDtypeStruct((H, nq, 1, tq), F32),
        compiler_params=_cparams(2),
    )(dy, ot)


def _mla_attn_bwd(q_cat, k_cat, v, dy, lse, delta):
    H, S, _ = q_cat.shape
    _, nq, _, tq = lse.shape
    tk = _tile(S, 512)
    nkb = S // tk

    def body(k_ref, v_ref, q_ref, do_ref, lse_ref, dl_ref, dk_ref, dv_ref, dq_ref, dk_scr, dv_scr):
        ki = pl.program_id(1)

        @pl.when(ki == 0)
        def _():
            dq_ref[...] = jnp.zeros_like(dq_ref)

        kb, vb = k_ref[0], v_ref[0]
        dk_scr[...] = jnp.zeros_like(dk_scr)
        dv_scr[...] = jnp.zeros_like(dv_scr)

        def step(i, carry):
            rows = pl.ds(pl.multiple_of(i * tq, tq), tq)
            qc = q_ref[0, rows, :]
            doc = do_ref[rows, :]
            pt = jnp.exp2(_nt(kb, qc) - lse_ref[0, i])
            dv_scr[...] += _nn(pt, doc)
            dst = (pt * (_nt(vb, doc) - dl_ref[0, i])).astype(CDT)
            dk_scr[...] += _nn(dst, qc)
            dq_ref[0, rows, :] += _tn(dst, kb)
            return carry

        lax.fori_loop(0, nq, step, 0, unroll=2 if nq % 2 == 0 else 1)
        dk_ref[0] = dk_scr[...] * (MLA_SCALE / MLA_QSCALE)
        dv_ref[0] = dv_scr[...]

    return pl.pallas_call(
        body,
        name="mla_attn_bwd",
        grid=(H, nkb),
        in_specs=[
            pl.BlockSpec((1, tk, 2 * LANE), lambda h, j: (h, j, 0)),
            pl.BlockSpec((1, tk, LANE), lambda h, j: (h, j, 0)),
            pl.BlockSpec((1, S, 2 * LANE), lambda h, j: (h, 0, 0)),
            pl.BlockSpec((S, LANE), lambda h, j: (0, h)),
            pl.BlockSpec((1, nq, 1, tq), lambda h, j: (h, 0, 0, 0)),
            pl.BlockSpec((1, nq, 1, tq), lambda h, j: (h, 0, 0, 0)),
        ],
        out_specs=[
            pl.BlockSpec((1, tk, 2 * LANE), lambda h, j: (h, j, 0)),
            pl.BlockSpec((1, tk, LANE), lambda h, j: (h, j, 0)),
            pl.BlockSpec((1, S, 2 * LANE), lambda h, j: (h, 0, 0)),
        ],
        out_shape=[
            jax.ShapeDtypeStruct((H, S, 2 * LANE), F32),
            jax.ShapeDtypeStruct((H, S, LANE), F32),
            jax.ShapeDtypeStruct((H, S, 2 * LANE), F32),
        ],
        scratch_shapes=[pltpu.VMEM((tk, 2 * LANE), F32), pltpu.VMEM((tk, LANE), F32)],
        compiler_params=_cparams(2),
    )(k_cat, v, q_cat, dy, lse, delta)


def _mla_up_bwd(dq_cat, dk_cat, dv, proj, cf, sf, g_cq, g_ckv, wuq_p, wukv, dproj):
    H, S, _ = dq_cat.shape
    tm = _rtile(S, 256)
    PW = proj.shape[1]
    kr_off = PW - KR_PAD

    assert OFF_CKV == OFF_CQ + MLA_RANK and OFF_CQ % (2 * MLA_RANK) == 0

    def body(dq_ref, dk_ref, dv_ref, cq_ref, ckv_ref, cf_ref, sf_ref, gq_ref, gkv_ref, wq_ref, wkv_ref, _dp,
             dqp_ref, dkvp_ref, dc_ref, dkr_ref, dgq_ref, dgkv_ref, aq_scr, akv_scr, akr_scr):
        i, h = pl.program_id(0), pl.program_id(1)
        dcq_ref, dckv_ref = dc_ref.at[:, 0:MLA_RANK], dc_ref.at[:, MLA_RANK:2 * MLA_RANK]

        @pl.when(h == 0)
        def _():
            aq_scr[...] = jnp.zeros_like(aq_scr)
            akv_scr[...] = jnp.zeros_like(akv_scr)
            akr_scr[...] = jnp.zeros_like(akr_scr)

        cfv, sfv = cf_ref[...], sf_ref[...]
        dq = dq_ref[0] * MLA_SCALE
        dqr = dq[:, LANE:2 * LANE]
        dqp = jnp.concatenate([dq[:, 0:LANE], dqr * cfv, dqr * sfv], axis=1).astype(CDT)
        dqp_ref[0] = dqp
        aq_scr[...] += _nt(dqp, wq_ref[0])
        dk = dk_ref[0]
        dkvp = jnp.concatenate([dk[:, 0:LANE], dv_ref[0]], axis=1).astype(CDT)
        dkvp_ref[0] = dkvp
        akv_scr[...] += _nt(dkvp, wkv_ref[0])
        akr_scr[...] += dk[:, LANE:2 * LANE]

        @pl.when(h == H - 1)
        def _():
            def rms_bwd(c_ref, g_ref, acc_ref, d_ref, dg_ref):
                xh, rs = _rms_stats(c_ref[...])
                dn = acc_ref[...]
                dxh = dn * g_ref[...]
                d_ref[...] = (rs * (dxh - xh * jnp.mean(dxh * xh, axis=-1, keepdims=True))).astype(d_ref.dtype)
                part = _rowsum8(dn * xh)

                @pl.when(i == 0)
                def _():
                    dg_ref[...] = part

                @pl.when(i > 0)
                def _():
                    dg_ref[...] += part

            rms_bwd(cq_ref, gq_ref, aq_scr, dcq_ref, dgq_ref)
            rms_bwd(ckv_ref, gkv_ref, akv_scr, dckv_ref, dgkv_ref)
            dkr = akr_scr[...]
            dkr_ref[...] = jnp.concatenate([dkr * cfv, dkr * sfv, jnp.zeros((tm, KR_PAD - 2 * LANE), F32)], axis=1).astype(dkr_ref.dtype)

    cspec = lambda off, w: pl.BlockSpec((tm, w), lambda i, h, o=off // w: (i, o))
    hspec = lambda w: pl.BlockSpec((1, tm, w), lambda i, h: (h, i, 0))
    outs = pl.pallas_call(
        body,
        name="mla_up_bwd",
        grid=(S // tm, H),
        in_specs=[
            hspec(2 * LANE), hspec(2 * LANE), hspec(LANE),
            cspec(OFF_CQ, MLA_RANK), cspec(OFF_CKV, MLA_RANK),
            pl.BlockSpec((tm, LANE), lambda i, h: (i, 0)), pl.BlockSpec((tm, LANE), lambda i, h: (i, 0)),
            pl.BlockSpec((1, MLA_RANK), lambda i, h: (0, 0)), pl.BlockSpec((1, MLA_RANK), lambda i, h: (0, 0)),
            pl.BlockSpec((1, MLA_RANK, 3 * LANE), lambda i, h: (h, 0, 0)),
            pl.BlockSpec((1, MLA_RANK, 2 * LANE), lambda i, h: (h, 0, 0)),
            pl.BlockSpec(memory_space=pl.ANY),
        ],
        out_specs=[
            hspec(3 * LANE), hspec(2 * LANE),
            pl.BlockSpec((tm, 2 * MLA_RANK), lambda i, h: (i, OFF_CQ // (2 * MLA_RANK))),
            pl.BlockSpec((tm, KR_PAD), lambda i, h: (i, 0)),
            pl.BlockSpec((SUB, MLA_RANK), lambda i, h: (0, 0)),
            pl.BlockSpec((SUB, MLA_RANK), lambda i, h: (0, 0)),
        ],
        out_shape=[
            jax.ShapeDtypeStruct((H, S, 3 * LANE), CDT), jax.ShapeDtypeStruct((H, S, 2 * LANE), CDT),
            jax.ShapeDtypeStruct(dproj.shape, dproj.dtype),
            jax.ShapeDtypeStruct((S, KR_PAD), CDT),
            jax.ShapeDtypeStruct((SUB, MLA_RANK), F32), jax.ShapeDtypeStruct((SUB, MLA_RANK), F32),
        ],
        scratch_shapes=[pltpu.VMEM((tm, MLA_RANK), F32), pltpu.VMEM((tm, MLA_RANK), F32), pltpu.VMEM((tm, LANE), F32)],
        input_output_aliases={11: 2},
        compiler_params=_cparams(2),
    )(dq_cat, dk_cat, dv, proj, proj, cf, sf, g_cq, g_ckv, wuq_p, wukv, dproj)
    dqp, dkvp, dproj, dkr, dgq, dgkv = outs
    dproj = _copy_into("dproj_kr", dkr, dproj, kr_off)
    return dproj, dqp, dkvp, dgq, dgkv


def _heads_tn(name, a, b):
    S, Ka = a.shape
    H, _, W = b.shape
    tk = _rtile(S, 1024)
    nk = S // tk

    def body(a_ref, b_ref, o_ref, acc):
        k = pl.program_id(1)

        @pl.when(k == 0)
        def _():
            acc[...] = jnp.zeros_like(acc)

        acc[...] += _tn(a_ref[...], b_ref[0])

        @pl.when(k == nk - 1)
        def _():
            o_ref[0] = acc[...].astype(o_ref.dtype)

    return pl.pallas_call(
        body,
        name=name,
        grid=(H, nk),
        in_specs=[pl.BlockSpec((tk, Ka), lambda h, k: (k, 0)), pl.BlockSpec((1, tk, W), lambda h, k: (h, k, 0))],
        out_specs=pl.BlockSpec((1, Ka, W), lambda h, k: (h, 0, 0)),
        out_shape=jax.ShapeDtypeStruct((H, Ka, W), CDT),
        scratch_shapes=[pltpu.VMEM((Ka, W), F32)],
        compiler_params=_cparams(2),
    )(a, b)


def _mem_softmax(q, k):
    s = _nt(q, k) * (MEM_HD ** -0.5)
    p = jnp.exp(s - jnp.max(s, axis=1, keepdims=True))
    return p / jnp.sum(p, axis=1, keepdims=True)


def _mem_attn_fwd(proj, memkv):
    S = proj.shape[0]
    Mm = memkv.shape[0]
    tm = _rtile(S, 512)

    def body(q_ref, k_ref, v_ref, y_ref):
        pn = _mem_softmax(q_ref[...], k_ref[...])
        y_ref[...] = _nn(pn, v_ref[...]).astype(y_ref.dtype)

    return pl.pallas_call(
        body,
        name="mem_attn_fwd",
        grid=(S // tm, MEM_HEADS),
        in_specs=[
            pl.BlockSpec((tm, MEM_HD), lambda i, h: (i, OFF_QM // MEM_HD + h)),
            pl.BlockSpec((Mm, MEM_HD), lambda i, h: (0, h)),
            pl.BlockSpec((Mm, MEM_HD), lambda i, h: (0, MEM_HEADS + h)),
        ],
        out_specs=pl.BlockSpec((tm, MEM_HD), lambda i, h: (i, h)),
        out_shape=jax.ShapeDtypeStruct((S, MEM_W), CDT),
        compiler_params=_cparams(2),
    )(proj, memkv, memkv)


def _mem_attn_bwd(dy, proj, memkv, dproj):
    S = proj.shape[0]
    Mm = memkv.shape[0]
    tm = _rtile(S, 512)
    scale = MEM_HD ** -0.5

    def body(dy_ref, q_ref, k_ref, v_ref, _dp, dq_ref, dk_ref, dv_ref):
        i = pl.program_id(1)
        q, k, dyv = q_ref[...].astype(CDT), k_ref[...], dy_ref[...]
        pn = _mem_softmax(q, k)
        dvp = _tn(pn, dyv)
        dp = _nt(dyv, v_ref[...])
        ds = pn * (dp - jnp.sum(dp * pn, axis=1, keepdims=True)) * scale
        dq_ref[...] = _nn(ds, k).astype(dq_ref.dtype)
        dkp = _tn(ds, q)

        @pl.when(i == 0)
        def _():
            dk_ref[...] = dkp
            dv_ref[...] = dvp

        @pl.when(i > 0)
        def _():
            dk_ref[...] += dkp
            dv_ref[...] += dvp

    dproj, dk, dv = pl.pallas_call(
        body,
        name="mem_attn_bwd",
        grid=(MEM_HEADS, S // tm),
        in_specs=[
            pl.BlockSpec((tm, MEM_HD), lambda h, i: (i, h)),
            pl.BlockSpec((tm, MEM_HD), lambda h, i: (i, OFF_QM // MEM_HD + h)),
            pl.BlockSpec((Mm, MEM_HD), lambda h, i: (0, h)),
            pl.BlockSpec((Mm, MEM_HD), lambda h, i: (0, MEM_HEADS + h)),
            pl.BlockSpec(memory_space=pl.ANY),
        ],
        out_specs=[
            pl.BlockSpec((tm, MEM_HD), lambda h, i: (i, OFF_QM // MEM_HD + h)),
            pl.BlockSpec((Mm, MEM_HD), lambda h, i: (0, h)),
            pl.BlockSpec((Mm, MEM_HD), lambda h, i: (0, h)),
        ],
        out_shape=[
            jax.ShapeDtypeStruct(dproj.shape, dproj.dtype),
            jax.ShapeDtypeStruct((Mm, MEM_W), F32),
            jax.ShapeDtypeStruct((Mm, MEM_W), F32),
        ],
        input_output_aliases={4: 0},
        compiler_params=_cparams(2),
    )(dy, proj, memkv, memkv, dproj)
    return dproj, dk, dv


def _small_allreduce(vec):
    NS = vec.shape[1]

    def body(v_ref, o_ref, gbuf, send, recv):
        x, y, c, me = _my_place()
        gbuf[me] = v_ref[...]
        copies = []
        for kk in range(1, N_DEV):
            peer, _ = _peer(x, y, c, kk)
            cp = pltpu.make_async_remote_copy(src_ref=v_ref, dst_ref=gbuf.at[me], send_sem=send.at[kk - 1],
                                              recv_sem=recv.at[kk - 1], device_id=peer, device_id_type=MESH)
            cp.start()
            copies.append(cp)
        for cp in copies:
            cp.wait()
        tot = gbuf[0]
        for d in range(1, N_DEV):
            tot = tot + gbuf[d]
        o_ref[...] = jnp.sum(tot, axis=0, keepdims=True)

    return pl.pallas_call(
        body,
        name="small_allreduce",
        in_specs=[pl.BlockSpec(memory_space=pltpu.VMEM)],
        out_specs=pl.BlockSpec(memory_space=pltpu.VMEM),
        out_shape=jax.ShapeDtypeStruct((1, NS), F32),
        scratch_shapes=[pltpu.VMEM((N_DEV, SUB, NS), F32), pltpu.SemaphoreType.DMA((N_DEV - 1,)),
                        pltpu.SemaphoreType.DMA((N_DEV - 1,))],
        compiler_params=pltpu.CompilerParams(has_side_effects=True, vmem_limit_bytes=V7X_VMEM_LIMIT),
    )(vec)


def _adamw_math(g, w, m, v):
    nm = ADAM_B1 * m + (1.0 - ADAM_B1) * g
    nv = ADAM_B2 * v + (1.0 - ADAM_B2) * (g * g)
    mh = nm / (1.0 - ADAM_B1 ** ADAM_STEP)
    vh = nv / (1.0 - ADAM_B2 ** ADAM_STEP)
    delta = -ADAM_LR * (mh / (jnp.sqrt(vh) + ADAM_EPS) + ADAM_WD * w)
    return delta, nm, nv


def _adam_big(name, recv, w, m, v):
    R, C = w.shape
    tr = _rtile(R, max(SUB, (ADAM_BLOCK_ELEMS // C) // SUB * SUB))

    def body(r_ref, w_ref, m_ref, v_ref, g_ref, d_ref, nm_ref, nv_ref):
        g = r_ref[0].astype(F32)
        for d in range(1, N_DEV):
            g = g + r_ref[d].astype(F32)
        delta, nm, nv = _adamw_math(g, w_ref[...], m_ref[...], v_ref[...])
        g_ref[...] = g
        d_ref[...] = delta
        nm_ref[...] = nm
        nv_ref[...] = nv

    blk = pl.BlockSpec((tr, C), lambda i: (i, 0))
    return pl.pallas_call(
        body,
        name=name,
        grid=(R // tr,),
        in_specs=[pl.BlockSpec((N_DEV, tr, C), lambda i: (0, i, 0)), blk, blk, blk],
        out_specs=[blk, blk, blk, blk],
        out_shape=[jax.ShapeDtypeStruct((R, C), F32)] * 4,
        compiler_params=_cparams(1),
    )(recv, w, m, v)


def _adam_small(g, w, m, v):
    def body(g_ref, w_ref, m_ref, v_ref, d_ref, nm_ref, nv_ref):
        delta, nm, nv = _adamw_math(g_ref[...], w_ref[...], m_ref[...], v_ref[...])
        d_ref[...] = delta
        nm_ref[...] = nm
        nv_ref[...] = nv

    return pl.pallas_call(body, name="adam_small", out_shape=[jax.ShapeDtypeStruct(g.shape, F32)] * 3)(g, w, m, v)


def _rot(w):
    h = w.shape[-1] // 2
    return jnp.concatenate([-w[..., h:], w[..., :h]], axis=-1)


def _unrot(dw):
    h = dw.shape[-1] // 2
    return jnp.concatenate([dw[..., h:], -dw[..., :h]], axis=-1)


def _pad_cols(w, width):
    return jnp.pad(w, [(0, 0)] * (w.ndim - 1) + [(0, width - w.shape[-1])])


def kernel(x, mem, positions, ln_emb_g, ln_emb_b, hgrn_lb_logits, w_in, hgrn_norm_g, mla_g_cq, mla_g_ckv, mla_w_uq, mla_w_ukv, mem_w_kv, w_branch, w_o, ln1_g, ln1_b, w_ffn_gate, w_ffn_up, w_ffn_down, ln2_g, ln2_b, loss_target, m_ln_emb_g, m_ln_emb_b, m_hgrn_lb_logits, m_w_in, m_hgrn_norm_g, m_mla_g_cq, m_mla_g_ckv, m_mla_w_uq, m_mla_w_ukv, m_mem_w_kv, m_w_branch, m_w_o, m_ln1_g, m_ln1_b, m_w_ffn_gate, m_w_ffn_up, m_w_ffn_down, m_ln2_g, m_ln2_b, v_ln_emb_g, v_ln_emb_b, v_hgrn_lb_logits, v_w_in, v_hgrn_norm_g, v_mla_g_cq, v_mla_g_ckv, v_mla_w_uq, v_mla_w_ukv, v_mem_w_kv, v_w_branch, v_w_o, v_ln1_g, v_ln1_b, v_w_ffn_gate, v_w_ffn_up, v_w_ffn_down, v_ln2_g, v_ln2_b):
    x2, tgt = x[0], loss_target[0]
    S, D = x2.shape
    Mm = mem.shape[1]
    F = w_ffn_gate.shape[2] * N_DEV
    GW = 3 * D
    PW = OFF_GATE + GW + KR_PAD
    KR = OFF_GATE + GW
    NIN = w_in.shape[2] * N_DEV
    assert NIN == OFF_GATE + MLA_ROPE + GW
    _, _, _, me = _my_place()
    row = lambda a: a.reshape(1, -1)

    big_w = [w_in[0], mla_w_uq[0], mla_w_ukv[0], mem_w_kv[0], w_branch[0].reshape(3 * BR_W, -1), w_o[0],
             w_ffn_gate[0], w_ffn_up[0], w_ffn_down[0]]
    big_m = [m_w_in[0], m_mla_w_uq[0], m_mla_w_ukv[0], m_mem_w_kv[0], m_w_branch[0].reshape(3 * BR_W, -1), m_w_o[0],
             m_w_ffn_gate[0], m_w_ffn_up[0], m_w_ffn_down[0]]
    big_v = [v_w_in[0], v_mla_w_uq[0], v_mla_w_ukv[0], v_mem_w_kv[0], v_w_branch[0].reshape(3 * BR_W, -1), v_w_o[0],
             v_w_ffn_gate[0], v_w_ffn_up[0], v_w_ffn_down[0]]
    big_wb = [w.astype(CDT) for w in big_w]
    g_in, g_lb = _all_gather_two_level("weights_all_gather", [big_wb[0], hgrn_lb_logits.reshape(4, -1)])
    cols = lambda g: jnp.transpose(g, (1, 0, 2)).reshape(g.shape[1], -1)
    win = cols(g_in)
    kr_w = win[:, OFF_QM:OFF_QM + MLA_ROPE]
    zeros64 = jnp.zeros_like(kr_w)
    win_p = jnp.concatenate([win[:, :OFF_FB], win[:, OFF_FF:OFF_G], win[:, OFF_FB:OFF_FF], win[:, OFF_G:OFF_QM],
                             win[:, OFF_QM + MLA_ROPE:], kr_w, zeros64, _rot(kr_w), zeros64,
                             jnp.zeros((D, KR_PAD - 2 * LANE), CDT)], axis=1)
    lbl4 = jnp.transpose(g_lb, (1, 0, 2)).reshape(4, -1)

    half = MLA_ROPE // 2
    inv_freq = jnp.power(ROPE_THETA, -jnp.arange(half, dtype=F32) / half)
    ang = positions[0].astype(F32)[:, None] * inv_freq
    cf = _pad_cols(jnp.tile(jnp.cos(ang), (1, 2)), LANE)
    sf = _pad_cols(jnp.tile(jnp.sin(ang), (1, 2)), LANE)

    tm512 = _rtile(S, 512)
    ident = lambda accs, tiles, rows: ([accs[0]], [])

    def epi_ln0(accs, tiles, rows):
        h = _ln_stats(tiles[0])[0] * rows[0] + rows[1]
        return [h, h], []

    h0, h0b = _fused_mm("ln_emb_fwd", "nn", [], S, D, 1, tm512, D, 1, [(D, F32, 0, None), (D, CDT, 0, None)], epi_ln0,
                        tiles=[(x2, 0)], rows=[(row(ln_emb_g), 0), (row(ln_emb_b), 0)])
    proj, g_uq, g_ukv, g_mkv, g_wb, g_wo = _fused_mm(
        "proj", "nn", [[(h0b, 0, win_p, 0)]], S, PW, D, _rtile(S, 1024), _tile(PW, 512), D, [(PW, F32, 0, None)], ident,
        xchg=_Xchg(big_wb[1:6], False), msplit=2 if S % 2048 == 0 else 1)
    wuq_p =jnp.concatenate([g_uq[..., :MLA_NOPE], _pad_cols(g_uq[..., MLA_NOPE:], LANE),
                             _pad_cols(_rot(g_uq[..., MLA_NOPE:]), LANE)], axis=-1)
    wukv = g_ukv
    wmkv = g_mkv.reshape(-1, g_mkv.shape[-1])
    wb = jnp.transpose(g_wb.reshape(N_DEV, 3, BR_W, -1), (1, 2, 0, 3)).reshape(3, BR_W, D)
    wo = g_wo.reshape(-1, D)
    o_f, st_f = _gla_fwd(proj, lbl4, OFF_FF, False, "gla_fwd_f")
    o_b, st_b = _gla_fwd(proj, lbl4, OFF_FB, True, "gla_fwd_b")
    y_hg = _hgrn_post_fwd(o_f, o_b, proj, hgrn_norm_g)
    q_cat, k_cat, v_mla, vt_mla, cqn, ckvn = _mla_up(proj, cf, sf, mla_g_cq, mla_g_ckv, wuq_p, wukv)
    y_mla, ot, lse, g_wg, g_wu, g_wd = _mla_attn_fwd(q_cat, k_cat, vt_mla, _Xchg(big_wb[6:9], False))
    wg, wu = cols(g_wg), cols(g_wu)
    wd = g_wd.reshape(-1, D)
    memb = mem[0].astype(CDT)
    (memkv,) = _fused_mm("mem_kv", "nn", [[(memb, 0, wmkv, 0)]], Mm, 2 * MEM_W, D, Mm, _tile(2 * MEM_W, 512), D,
                         [(2 * MEM_W, CDT, 0, None)], ident)
    y_mem = _mem_attn_fwd(proj, memkv)
    ys = [y_hg, y_mla, y_mem]
    tnD = _tile(D, 512, OFF_GATE)

    def epi_branch(accs, tiles, rows):
        return [_sigmoid(tiles[0]) * accs[0] + _sigmoid(tiles[1]) * accs[1] + _sigmoid(tiles[2]) * accs[2]], []

    (merged,) = _fused_mm("branch_fwd", "nn", [[(ys[b], 0, wb[b], 0)] for b in range(3)], S, D, BR_W, tm512, tnD, BR_W,
                          [(D, CDT, 0, None)], epi_branch, tiles=[(proj, OFF_GATE + b * D) for b in range(3)])

    def epi_ln1(accs, tiles, rows):
        r1v = ALPHA * tiles[0] + accs[0]
        return [r1v, _ln_stats(r1v)[0] * rows[0] + rows[1]], []

    r1, h1b = _fused_mm("wo_ln1", "nn", [[(merged, 0, wo, 0)]], S, D, D, tm512, D, D,
                        [(D, F32, 0, None), (D, CDT, 0, None)], epi_ln1, tiles=[(h0, 0)], rows=[(ln1_g, 0), (ln1_b, 0)])
    tnF = _tile(F, 512)

    def epi_up(accs, tiles, rows):
        gp, up = accs
        return [gp, up, gp * _sigmoid(gp) * up], []

    tm1k, ms1k = _rtile(S, 1024), (2 if S % 2048 == 0 else 1)
    gpb, upb, act = _fused_mm("ffn_up", "nn", [[(h1b, 0, wg, 0)], [(h1b, 0, wu, 0)]], S, F, D, tm1k, tnF, D,
                              [(F, CDT, 0, None)] * 3, epi_up, msplit=ms1k)

    def epi_down(accs, tiles, rows):
        g1, b1, g2, b2 = rows
        h1 = _ln_stats(tiles[0])[0] * g1 + b1
        xh2, rstd2 = _ln_stats(ALPHA * h1 + accs[0])
        diff = xh2 * g2 + b2 - tiles[1]
        dh2 = diff * (1.0 / D)
        dr2v = _ln_bwd(dh2, xh2, rstd2, g2)
        return [dr2v, dr2v], [dh2 * xh2, dh2, diff * diff * (0.5 / D)]

    dr2, dr2b, dg2, db2, lossp = _fused_mm(
        "ffn_down_loss", "nn", [[(act, 0, wd, 0)]], S, D, F, tm512, D, _tile(F, 704), [(D, F32, 0, None), (D, CDT, 0, None)],
        epi_down, tiles=[(r1, 0), (tgt, 0)], rows=[(ln1_g, 0), (ln1_b, 0), (ln2_g, 0), (ln2_b, 0)], n_racc=3)

    def epi_dact(accs, tiles, rows):
        da, gp, up = accs[0], tiles[0].astype(F32), tiles[1].astype(F32)
        s = _sigmoid(gp)
        return [da * up * (s * (1.0 + gp * (1.0 - s))), da * (gp * s)], []

    dgp, dup = _fused_mm("ffn_dact", "nt", [[(dr2b, 0, wd, 0)]], S, F, D, tm1k, tnF, D, [(F, CDT, 0, None)] * 2,
                         epi_dact, tiles=[(gpb, 0), (upb, 0)], msplit=ms1k)
    tkS = _rtile(S, 2048)
    (d_wd,) = _fused_mm("dw_down", "tn", [[(act, 0, dr2b, 0)]], F, D, S, tnF, D, tkS, [(D, CDT, 0, None)], ident)
    d_wg, d_wu = _fused_mm("dw_gate_up", "tn", [[(h1b, 0, dgp, 0)], [(h1b, 0, dup, 0)]], D, F, S, _tile(D, 1024), tnF, tkS,
                           [(F, CDT, 0, None)] * 2, lambda accs, tiles, rows: (accs, []))

    def epi_dh1(accs, tiles, rows):
        dh1 = accs[0] + ALPHA * tiles[0]
        xh1, rstd1 = _ln_stats(tiles[1])
        dr1v = _ln_bwd(dh1, xh1, rstd1, rows[0])
        return [dr1v, dr1v], [dh1 * xh1, dh1]

    uncols = lambda dw: jnp.transpose(dw.reshape(dw.shape[0], N_DEV, -1), (1, 0, 2))
    dr1, dr1b, dg1, db1, r_wg, r_wu, r_wd = _fused_mm(
        "dh1_ln1", "nt", [[(dgp, 0, wg, 0), (dup, 0, wu, 0)]], S, D, F, tm512, D, _tile(F, 704),
        [(D, F32, 0, None), (D, CDT, 0, None)], epi_dh1, tiles=[(dr2, 0), (r1, 0)], rows=[(ln1_g, 0)], n_racc=2,
        xchg=_Xchg([uncols(d_wg), uncols(d_wu), d_wd.reshape(N_DEV, -1, D)], True))
    (dmerged,) = _fused_mm("dmerged", "nt", [[(dr1b, 0, wo, 0)]], S, D, D, tm512, _tile(D, 512), D, [(D, CDT, 0, None)], ident)
    (d_wo,) = _fused_mm("dw_o", "tn", [[(merged, 0, dr1b, 0)]], D, D, S, _tile(D, 512), D, tkS, [(D, CDT, 0, None)], ident)

    def epi_dbranch(accs, tiles, rows):
        dm, s = tiles[0].astype(F32), _sigmoid(tiles[1])
        return [dm * s, dm * accs[0] * s * (1.0 - s)], []

    dproj = None
    d_wbs, dys = [], []
    for b in range(3):
        du, dproj = _fused_mm(f"branch_bwd{b}", "nn", [[(ys[b], 0, wb[b], 0)]], S, D, BR_W, tm512, tnD, BR_W,
                              [(D, CDT, 0, None), (PW, CDT, OFF_GATE + b * D, dproj)], epi_dbranch,
                              tiles=[(dmerged, 0), (proj, OFF_GATE + b * D)])
        (dwb,) = _fused_mm(f"dw_branch{b}", "tn", [[(ys[b], 0, du, 0)]], BR_W, D, S, _tile(BR_W, 512), D, tkS,
                           [(D, CDT, 0, None)], ident)
        (dyb,) = _fused_mm(f"dy_branch{b}", "nt", [[(du, 0, wb[b], 0)]], S, BR_W, D, tm512, _tile(BR_W, 512), D,
                           [(BR_W, F32 if b == 0 else CDT, 0, None)], ident)
        d_wbs.append(dwb)
        dys.append(dyb)
    dy_hg, dy_mla, dy_mem = dys

    dproj, dk_mem, dv_mem = _mem_attn_bwd(dy_mem, proj, memkv, dproj)
    dkv_mem = jnp.concatenate([dk_mem, dv_mem], axis=1).astype(CDT)
    (d_wmkv,) = _fused_mm("dw_memkv", "tn", [[(memb, 0, dkv_mem, 0)]], D, 2 * MEM_W, Mm, _tile(D, 512), 2 * MEM_W, Mm,
                          [(2 * MEM_W, CDT, 0, None)], ident)

    delta = _mla_delta(dy_mla, ot)
    dk_cat, dv_h, dq_cat = _mla_attn_bwd(q_cat, k_cat, v_mla, dy_mla, lse, delta)
    dproj, dqp, dkvp, dgq, dgkv = _mla_up_bwd(dq_cat, dk_cat, dv_h, proj, cf, sf, mla_g_cq, mla_g_ckv, wuq_p, wukv, dproj)
    d_wuq_p = _heads_tn("dw_uq", cqn, dqp).astype(F32)
    d_wukv = _heads_tn("dw_ukv", ckvn, dkvp)
    d_wuq = jnp.concatenate([d_wuq_p[..., :MLA_NOPE],
                             d_wuq_p[..., LANE:LANE + MLA_ROPE] + _unrot(d_wuq_p[..., 2 * LANE:2 * LANE + MLA_ROPE])],
                            axis=-1).astype(CDT)

    do_hg, dproj, dng = _hgrn_post_bwd(dy_hg, o_f, o_b, proj, hgrn_norm_g, dproj)
    dproj, dq1, di1, dl_f = _gla_bwd(proj, lbl4, OFF_FF, False, do_hg, st_f, dproj, None, "gla_bwd_f")
    dproj, _, _, dl_b = _gla_bwd(proj, lbl4, OFF_FB, True, do_hg, st_b, dproj, (dq1, di1), "gla_bwd_b")

    def epi_dh0(accs, tiles, rows):
        dh0 = accs[0] + ALPHA * tiles[0]
        xh, rstd = _ln_stats(tiles[1])
        return [_ln_bwd(dh0, xh, rstd, rows[0])], [dh0 * xh, dh0]

    d_wb = jnp.transpose(jnp.stack(d_wbs).reshape(3, BR_W, N_DEV, -1), (2, 0, 1, 3)).reshape(N_DEV, 3 * BR_W, -1)
    d_win_p, r_uq, r_ukv, r_mkv, r_wb, r_wo = _fused_mm(
        "dw_in", "tn", [[(h0b, 0, dproj, 0)]], D, PW, S, _tile(D, 1024), _tile(PW, 1536), tkS, [(PW, CDT, 0, None)], ident,
        xchg=_Xchg([d_wuq, d_wukv, d_wmkv.reshape(N_DEV, -1, 2 * MEM_W), d_wb, d_wo.reshape(N_DEV, -1, D)], True))
    d_kr = (d_win_p[:, KR:KR + MLA_ROPE].astype(F32) + _unrot(d_win_p[:, KR + LANE:KR + LANE + MLA_ROPE].astype(F32))).astype(CDT)
    d_win = jnp.concatenate([d_win_p[:, :OFF_FB], d_win_p[:, OFF_FF:OFF_G], d_win_p[:, OFF_FB:OFF_FF],
                             d_win_p[:, OFF_G:OFF_QM], d_kr, d_win_p[:, OFF_QM:KR]], axis=1)
    grad_x, dge, dbe, r_in = _fused_mm(
        "dh0_ln_emb", "nt", [[(dproj, 0, win_p, 0)]], S, D, PW, tm512, D, _tile(PW, 1536), [(D, F32, 0, None)], epi_dh0,
        tiles=[(dr1, 0), (x2, 0)], rows=[(row(ln_emb_g), 0)], n_racc=2, xchg=_Xchg([uncols(d_win)], True))

    recv = [r_in, r_uq, r_ukv, r_mkv, r_wb, r_wo, r_wg, r_wu, r_wd]
    names = ["w_in", "w_uq", "w_ukv", "mem_w_kv", "w_branch", "w_o", "w_gate", "w_up", "w_down"]
    big_out = [_adam_big("adam_" + nme, r, w, m_, v_) for nme, r, w, m_, v_ in zip(names, recv, big_w, big_m, big_v)]

    parts = [dge, dbe, dng, dgq, dgkv, dg1, db1, dg2, db2, dl_f, dl_b, lossp]
    widths = [p.shape[1] for p in parts]
    red = _small_allreduce(jnp.concatenate(parts, axis=1))[0]
    offs = [sum(widths[:i]) for i in range(len(widths))]
    rs = [red[o:o + w_] for o, w_ in zip(offs, widths)]
    g_le_g, g_le_b, g_ng, g_gq, g_gkv, g_l1g, g_l1b, g_l2g, g_l2b, g_dlf, g_dlb, g_loss = rs
    loss = jnp.sum(g_loss)
    g_ng = g_ng.reshape(HG_HEADS, HG_DK).sum(axis=0)
    dl0 = jnp.stack([g_dlf, g_dlb])
    g_lb_full = jnp.stack([dl0, -dl0], axis=1)
    lbw = hgrn_lb_logits.shape[2]
    g_lb = lax.dynamic_slice_in_dim(g_lb_full, me * lbw, lbw, axis=2)

    small_g = [g_le_g, g_le_b, g_lb, g_ng.reshape(1, -1), g_gq.reshape(1, -1), g_gkv.reshape(1, -1), g_l1g.reshape(1, -1),
               g_l1b.reshape(1, -1), g_l2g.reshape(1, -1), g_l2b.reshape(1, -1)]
    small_w = [ln_emb_g, ln_emb_b, hgrn_lb_logits, hgrn_norm_g, mla_g_cq, mla_g_ckv, ln1_g, ln1_b, ln2_g, ln2_b]
    small_m = [m_ln_emb_g, m_ln_emb_b, m_hgrn_lb_logits, m_hgrn_norm_g, m_mla_g_cq, m_mla_g_ckv, m_ln1_g, m_ln1_b, m_ln2_g, m_ln2_b]
    small_v = [v_ln_emb_g, v_ln_emb_b, v_hgrn_lb_logits, v_hgrn_norm_g, v_mla_g_cq, v_mla_g_ckv, v_ln1_g, v_ln1_b, v_ln2_g, v_ln2_b]
    small_g = [g.reshape(w.shape) for g, w in zip(small_g, small_w)]
    pack = lambda lst: jnp.concatenate([a.reshape(-1) for a in lst]).reshape(1, -1)
    s_delta, s_nm, s_nv = _adam_small(pack(small_g), pack(small_w), pack(small_m), pack(small_v))
    sizes = [w.size for w in small_w]
    soffs = [sum(sizes[:i]) for i in range(len(sizes))]
    unpack = lambda p: [p[0, o:o + n].reshape(w.shape) for o, n, w in zip(soffs, sizes, small_w)]
    s_delta, s_nm, s_nv = unpack(s_delta), unpack(s_nm), unpack(s_nv)

    def ordered(small, big):
        sm = list(small)
        bg = [b.reshape(w.shape) for b, w in zip(big, [w_in, mla_w_uq, mla_w_ukv, mem_w_kv, w_branch, w_o, w_ffn_gate, w_ffn_up, w_ffn_down])]
        return [sm[0], sm[1], sm[2], bg[0], sm[3], sm[4], sm[5], bg[1], bg[2], bg[3], bg[4], bg[5], sm[6], sm[7], bg[6], bg[7], bg[8], sm[8], sm[9]]

    grads = ordered(small_g, [o[0] for o in big_out])
    deltas = ordered(s_delta, [o[1] for o in big_out])
    new_m = ordered(s_nm, [o[2] for o in big_out])
    new_v = ordered(s_nv, [o[3] for o in big_out])
    return (loss, grad_x[None], *grads, *deltas, *new_m, *new_v)
```

```python
import functools

import jax
import jax.numpy as jnp
from jax import lax
from jax.experimental import pallas as pl
from jax.experimental.pallas import tpu as pltpu

F32 = jnp.float32
CDT = jnp.bfloat16
MESH = pl.DeviceIdType.MESH
N_DEV = 8
V7X_VMEM_LIMIT = 60 * 1024 * 1024
LANE = 128
SUB = 8

HG_HEADS, HG_DK, HG_CHUNK = 8, 128, 64
HG_HPS = 8
HG_W = HG_HEADS * HG_DK
MLA_HEADS, MLA_RANK, MLA_NOPE, MLA_ROPE, MLA_V = 8, 512, 128, 64, 128
MLA_QK = MLA_NOPE + MLA_ROPE
MLA_SCALE = MLA_QK ** -0.5
MLA_QSCALE = MLA_SCALE * 1.4426950408889634
VT_ROWS = LANE + 16
MLA_TQ = 1024
MLA_FWD_SLABS = 4
MLA_W = MLA_HEADS * MLA_V
MEM_HEADS, MEM_HD = 4, 256
MEM_W = MEM_HEADS * MEM_HD
BR_W = 1024
ROPE_THETA = 10000.0
ALPHA = 2.0 ** 0.25
LN_EPS = 1e-5
RMS_EPS = 1e-6
ADAM_LR, ADAM_B1, ADAM_B2, ADAM_EPS, ADAM_WD, ADAM_STEP = 0.001, 0.9, 0.999, 1e-08, 0.01, 10
ADAM_BLOCK_ELEMS = 256 * 1024

OFF_Q, OFF_I, OFF_FB, OFF_FF, OFF_G = 0, 1024, 2048, 3072, 4096
OFF_CQ, OFF_CKV, OFF_QM, OFF_GATE = 5120, 5632, 6144, 7168
KR_PAD = 512


def _cparams(n_grid, side_effects=False):
    return pltpu.CompilerParams(dimension_semantics=("arbitrary",) * n_grid, vmem_limit_bytes=V7X_VMEM_LIMIT,
                                has_side_effects=side_effects)


def _tile(n, pref, *offsets):
    if n <= pref and all(o % n == 0 for o in offsets):
        return n
    t = (min(pref, n) // LANE) * LANE
    while t >= LANE:
        if n % t == 0 and all(o % t == 0 for o in offsets):
            return t
        t -= LANE
    raise ValueError(f"no tile for {n} {pref} {offsets}")


def _rtile(n, pref):
    if n <= pref:
        return n
    t = (pref // SUB) * SUB
    while t >= SUB:
        if n % t == 0:
            return t
        t -= SUB
    raise ValueError(f"no row tile for {n} {pref}")


def _dot(a, b, dims):
    return lax.dot_general(a.astype(CDT), b.astype(CDT), (dims, ((), ())), preferred_element_type=F32)


def _nn(a, b):
    return _dot(a, b, ((1,), (0,)))


def _nt(a, b):
    return _dot(a, b, ((1,), (1,)))


def _tn(a, b):
    return _dot(a, b, ((0,), (0,)))


_DOTS = {"nn": _nn, "nt": _nt, "tn": _tn}


def _sigmoid(x):
    return 1.0 / (1.0 + jnp.exp(-x))


def _rowsum8(v):
    r, w = v.shape
    return v.reshape(r // SUB, SUB, w).sum(axis=0)


def _my_place():
    x, y, c = lax.axis_index("x"), lax.axis_index("y"), lax.axis_index("c")
    return x, y, c, 4 * x + 2 * y + c


def _peer(x, y, c, kk):
    px = 1 - x if kk & 4 else x
    py = 1 - y if kk & 2 else y
    pc = 1 - c if kk & 1 else c
    return (px, py, pc), 4 * px + 2 * py + pc


class _Xchg:
    def __init__(self, arrs, scatter):
        self.arrs, self.scatter, self.n = list(arrs), scatter, len(arrs)
        hbm = pl.BlockSpec(memory_space=pl.ANY)
        self.specs = [hbm] * self.n
        self.out_shape = [jax.ShapeDtypeStruct(((N_DEV,) + a.shape[1:]) if scatter else ((N_DEV,) + a.shape), a.dtype)
                          for a in self.arrs]
        ncp = self.n * (N_DEV - 1)
        self.scratch = [pltpu.SemaphoreType.DMA((ncp,)), pltpu.SemaphoreType.DMA((ncp,)), pltpu.SemaphoreType.DMA((self.n,))]

    def _copies(self, ins, outs, send, recv, loc):
        x, y, c, me = _my_place()
        copies = []
        for w in range(self.n):
            copies.append(pltpu.make_async_copy(ins[w].at[me] if self.scatter else ins[w], outs[w].at[me], loc.at[w]))
            for kk in range(1, N_DEV):
                peer, pid = _peer(x, y, c, kk)
                s = w * (N_DEV - 1) + kk - 1
                copies.append(pltpu.make_async_remote_copy(
                    src_ref=ins[w].at[pid] if self.scatter else ins[w], dst_ref=outs[w].at[me],
                    send_sem=send.at[s], recv_sem=recv.at[s], device_id=peer, device_id_type=MESH))
        return copies

    def start(self, ins, outs, sems):
        for cp in self._copies(ins, outs, *sems):
            cp.start()

    def wait(self, ins, outs, sems):
        for cp in self._copies(ins, outs, *sems):
            cp.wait()


def _all_gather_two_level(name, arrs):
    n = len(arrs)
    NC = N_DEV - 1

    def body(*refs):
        ins, outs = refs[:n], refs[n:2 * n]
        send, recv, loc = refs[2 * n:]
        x, y, c, me = _my_place()
        sibling = (x, y, 1 - c)
        chips = [(1 - x, y), (x, 1 - y), (1 - x, 1 - y)]
        slot = lambda px, py, pc: 4 * px + 2 * py + pc

        def copy(w, k, block, to, src=None):
            dst = outs[w].at[slot(*block)]
            return pltpu.make_async_remote_copy(src_ref=dst if src is None else src, dst_ref=dst,
                                                send_sem=send.at[w * NC + k], recv_sem=recv.at[w * NC + k],
                                                device_id=to, device_id_type=MESH)

        mine = [pltpu.make_async_copy(ins[w], outs[w].at[me], loc.at[w]) for w in range(n)]
        for cp in mine:
            cp.start()
        first = []
        for w in range(n):
            first.append(copy(w, 0, (x, y, c), sibling, src=ins[w]))
            first += [copy(w, 1 + j, (x, y, c), (*chip, c), src=ins[w]) for j, chip in enumerate(chips)]
        for cp in first:
            cp.start()
        passed = []
        for j, chip in enumerate(chips):
            for w in range(n):
                copy(w, 1 + j, (*chip, c), (x, y, c)).wait_recv()
                fwd = copy(w, 4 + j, (*chip, c), sibling)
                fwd.start()
                passed.append(fwd)
        for w in range(n):
            copy(w, 0, sibling, (x, y, c)).wait_recv()
            for j, chip in enumerate(chips):
                copy(w, 4 + j, (*chip, 1 - c), (x, y, c)).wait_recv()
        for cp in first + passed:
            cp.wait_send()
        for cp in mine:
            cp.wait()

    hbm = pl.BlockSpec(memory_space=pl.ANY)
    return pl.pallas_call(
        body,
        name=name,
        in_specs=[hbm] * n,
        out_specs=[hbm] * n,
        out_shape=[jax.ShapeDtypeStruct((N_DEV,) + a.shape, a.dtype) for a in arrs],
        scratch_shapes=[pltpu.SemaphoreType.DMA((n * NC,)), pltpu.SemaphoreType.DMA((n * NC,)), pltpu.SemaphoreType.DMA((n,))],
        compiler_params=pltpu.CompilerParams(has_side_effects=True),
    )(*arrs)


def _fused_mm(name, mode, groups, M, N, K, tm, tn, tk, outs, epi, tiles=(), rows=(), n_racc=0, xchg=None, msplit=1):
    ni, nj, nk = M // tm, N // tn, K // tk
    assert M % tm == 0 and N % tn == 0 and K % tk == 0, (name, M, N, K, tm, tn, tk)
    assert n_racc == 0 or nj == 1
    assert msplit == 1 or (nk == 1 and n_racc == 0 and tm % (16 * msplit) == 0)
    dot = _DOTS[mode] if groups else None
    ins, in_specs = [], []
    for g in groups:
        for a, a_off, b, b_off in g:
            if mode == "tn":
                assert a_off % tm == 0
                in_specs.append(pl.BlockSpec((tk, tm), lambda i, j, k, o=a_off // tm: (k, i + o)))
            else:
                assert a_off % tk == 0
                in_specs.append(pl.BlockSpec((tm, tk), lambda i, j, k, o=a_off // tk: (i, k + o)))
            ins.append(a)
            if mode == "nt":
                assert b_off % tk == 0
                in_specs.append(pl.BlockSpec((tn, tk), lambda i, j, k, o=b_off // tk: (j, k + o)))
            else:
                assert b_off % tn == 0
                in_specs.append(pl.BlockSpec((tk, tn), lambda i, j, k, o=b_off // tn: (k, j + o)))
            ins.append(b)
    for arr, off in tiles:
        assert off % tn == 0
        ins.append(arr)
        in_specs.append(pl.BlockSpec((tm, tn), lambda i, j, k, o=off // tn: (i, j + o)))
    for arr, off in rows:
        assert off % tn == 0
        ins.append(arr)
        in_specs.append(pl.BlockSpec((1, tn), lambda i, j, k, o=off // tn: (0, j + o)))
    aliases = {}
    out_shape, out_specs = [], []
    for oi, (width, dtype, off, alias) in enumerate(outs):
        assert off % tn == 0
        if alias is not None:
            aliases[len(ins)] = oi
            ins.append(alias)
            in_specs.append(pl.BlockSpec(memory_space=pl.ANY))
        out_shape.append(jax.ShapeDtypeStruct((M, width), dtype))
        out_specs.append(pl.BlockSpec((tm, tn), lambda i, j, k, o=off // tn: (i, j + o)))
    for _ in range(n_racc):
        out_shape.append(jax.ShapeDtypeStruct((SUB, N), F32))
        out_specs.append(pl.BlockSpec((SUB, tn), lambda i, j, k: (0, 0)))
    n_alias = len(aliases)
    n_pairs = [len(g) for g in groups]
    use_scratch = nk > 1
    scratch = [pltpu.VMEM((tm, tn), F32) for _ in groups] if use_scratch else []
    nx = 0
    if xchg is not None:
        nx = xchg.n
        ins += xchg.arrs
        in_specs += xchg.specs
        out_shape += xchg.out_shape
        out_specs += xchg.specs
        scratch += xchg.scratch

    def body(*refs):
        it = iter(refs)
        pair_refs = [[(next(it), next(it)) for _ in range(n)] for n in n_pairs]
        tile_refs = [next(it) for _ in tiles]
        row_refs = [next(it) for _ in rows]
        for _ in range(n_alias):
            next(it)
        x_in = [next(it) for _ in range(nx)]
        out_refs = [next(it) for _ in outs]
        racc_refs = [next(it) for _ in range(n_racc)]
        x_out = [next(it) for _ in range(nx)]
        acc_refs = [next(it) for _ in groups] if use_scratch else []
        x_sems = list(it)
        i, j, k = pl.program_id(0), pl.program_id(1), pl.program_id(2)
        if nx:
            @pl.when((i == 0) & (j == 0) & (k == 0))
            def _():
                xchg.start(x_in, x_out, x_sems)

        def products():
            res = []
            for prs in pair_refs:
                s = None
                for a_ref, b_ref in prs:
                    d = dot(a_ref[...], b_ref[...])
                    s = d if s is None else s + d
                res.append(s)
            return res

        def finish(accs):
            out_v, racc_v = epi(accs, [t[...] for t in tile_refs], [r[...] for r in row_refs])
            for o_ref, v in zip(out_refs, out_v):
                o_ref[...] = v.astype(o_ref.dtype)
            for r_ref, v in zip(racc_refs, racc_v):
                part = _rowsum8(v)

                @pl.when(i == 0)
                def _():
                    r_ref[...] = part

                @pl.when(i > 0)
                def _():
                    r_ref[...] += part

        if not use_scratch and msplit > 1:
            ts = tm // msplit
            for s in range(msplit):
                rs = pl.ds(s * ts, ts)
                accs = []
                for prs in pair_refs:
                    acc = None
                    for a_ref, b_ref in prs:
                        dd = dot(a_ref[:, rs] if mode == "tn" else a_ref[rs, :], b_ref[...])
                        acc = dd if acc is None else acc + dd
                    accs.append(acc)
                out_v, _ = epi(accs, [t[rs, :] for t in tile_refs], [r[...] for r in row_refs])
                for o_ref, v in zip(out_refs, out_v):
                    o_ref[rs, :] = v.astype(o_ref.dtype)
        elif not use_scratch:
            finish(products())
        else:
            @pl.when(k == 0)
            def _():
                for acc in acc_refs:
                    acc[...] = jnp.zeros_like(acc)

            for acc, p in zip(acc_refs, products()):
                acc[...] += p

            @pl.when(k == nk - 1)
            def _():
                finish([acc[...] for acc in acc_refs])

        if nx:
            @pl.when((i == ni - 1) & (j == nj - 1) & (k == nk - 1))
            def _():
                xchg.wait(x_in, x_out, x_sems)

    res = pl.pallas_call(
        body,
        name=name,
        grid=(ni, nj, nk),
        in_specs=in_specs,
        out_specs=out_specs,
        out_shape=out_shape,
        scratch_shapes=scratch,
        input_output_aliases=aliases,
        compiler_params=_cparams(3, side_effects=nx > 0),
    )(*ins)
    return res


def _ln_stats(r):
    mu = jnp.mean(r, axis=-1, keepdims=True)
    xc = r - mu
    var = jnp.mean(xc * xc, axis=-1, keepdims=True)
    rstd = lax.rsqrt(var + LN_EPS)
    return xc * rstd, rstd


def _ln_bwd(dh, xhat, rstd, g):
    dxh = dh * g
    m1 = jnp.mean(dxh, axis=-1, keepdims=True)
    m2 = jnp.mean(dxh * xhat, axis=-1, keepdims=True)
    return rstd * (dxh - m1 - xhat * m2)


def _split3(x):
    hi = x.astype(CDT)
    r1 = x - hi.astype(F32)
    mid = r1.astype(CDT)
    lo = (r1 - mid.astype(F32)).astype(CDT)
    return hi, mid, lo


def _tri_matmul(tri, x):
    hi, mid, lo = _split3(x)
    return _nn(tri, hi) + _nn(tri, mid) + _nn(tri, lo)


def _dot3(dot, a, b):
    a_hi, b_hi = a.astype(CDT), b.astype(CDT)
    a_lo = (a - a_hi.astype(F32)).astype(CDT)
    b_lo = (b - b_hi.astype(F32)).astype(CDT)
    return dot(a_hi, b_hi) + dot(a_hi, b_lo) + dot(a_lo, b_hi)


def _gla_masks(reverse):
    C = HG_CHUNK
    r = lax.broadcasted_iota(jnp.int32, (C, C), 0)
    c = lax.broadcasted_iota(jnp.int32, (C, C), 1)
    keep = (c >= r) if reverse else (r >= c)
    return keep


def _m(fn, *lists):
    return [fn(*args) for args in zip(*lists)]


def _gla_chunk_fwd(qraw, fraw, lb, keep, reverse):
    C = HG_CHUNK
    end = 0 if reverse else C - 1
    tri = jnp.where(keep, 1.0, 0.0).astype(CDT)
    sq = _m(_sigmoid, qraw)
    q = _m(lambda x, s: x * s, qraw, sq)
    sg = _m(_sigmoid, fraw)
    f = _m(lambda l_, s: l_ + (1.0 - l_) * s, lb, sg)
    k = _m(lambda x: 1.0 - x, f)
    g = _m(jnp.log, f)
    b = _m(lambda x: _tri_matmul(tri, x), g)
    b_end = _m(lambda x: x[end:end + 1, :], b)
    b_mid = _m(lambda x: x[C // 2:C // 2 + 1, :], b)
    eq = _m(lambda x, m_: jnp.exp(x - m_), b, b_mid)
    ek = _m(lambda x, m_: jnp.exp(m_ - x), b, b_mid)
    eb = _m(jnp.exp, b)
    e2 = _m(lambda x, e_: jnp.exp(e_ - x), b, b_end)
    e_end = _m(jnp.exp, b_end)
    qt = _m(lambda x, e_: x * e_, q, eq)
    kt = _m(lambda x, e_: x * e_, k, ek)
    qs = _m(lambda x, e_: (x * e_).astype(CDT), q, eb)
    k2 = _m(lambda x, e_: (x * e_).astype(CDT), k, e2)
    a = _m(lambda x, y: jnp.where(keep, _dot3(_nt, x, y), 0.0).astype(CDT), qt, kt)
    return dict(sq=sq, q=q, sg=sg, f=f, k=k, eq=eq, ek=ek, eb=eb, e2=e2, e_end=e_end, qt=qt, kt=kt, qs=qs, k2=k2, a=a)


def _gla_fwd(proj, lbl4, f_off, reverse, name):
    S = proj.shape[0]
    C = HG_CHUNK
    R = _rtile(S, 512)
    cpb, nblk = R // C, S // R
    d = 1 if reverse else 0
    blk_map = (lambda b: nblk - 1 - b) if reverse else (lambda b: b)

    W = HG_HPS * HG_DK

    def body(q_ref, i_ref, f_ref, lb_ref, o_ref, st_ref, s_scr):
        @pl.when(pl.program_id(1) == 0)
        def _():
            s_scr[...] = jnp.zeros_like(s_scr)

        l = lb_ref[...]
        lbs = _sigmoid(l[2 * d:2 * d + 1, :] - l[2 * d + 1:2 * d + 2, :])
        keep = _gla_masks(reverse)
        heads = list(range(HG_HPS))
        css = [pl.ds(hh * HG_DK, HG_DK) for hh in heads]
        lb = [lbs[:, hh * HG_DK:(hh + 1) * HG_DK] for hh in heads]
        for cc in range(cpb):
            c = cpb - 1 - cc if reverse else cc
            sl = pl.ds(c * C, C)
            v = [i_ref[sl, cs] for cs in css]
            t = _gla_chunk_fwd([q_ref[sl, cs] for cs in css], [f_ref[sl, cs] for cs in css], lb, keep, reverse)
            st = [s_scr[hh] for hh in heads]
            o = _m(lambda qs, s_, a, v_: _nt(qs, s_) + _nn(a, v_), t["qs"], st, t["a"], v)
            new = _m(lambda e_, s_, v_, k2: e_ * s_ + _tn(v_, k2), t["e_end"], st, v, t["k2"])
            for hh in heads:
                st_ref[c, hh] = st[hh]
                o_ref[sl, css[hh]] = o[hh]
                s_scr[hh] = new[hh]

    col = lambda off: (lambda h, b: (blk_map(b), off // W + h))
    return pl.pallas_call(
        body,
        name=name,
        grid=(HG_HEADS // HG_HPS, nblk),
        in_specs=[
            pl.BlockSpec((R, W), col(OFF_Q)),
            pl.BlockSpec((R, W), col(OFF_I)),
            pl.BlockSpec((R, W), col(f_off)),
            pl.BlockSpec((4, W), lambda h, b: (0, h)),
        ],
        out_specs=[
            pl.BlockSpec((R, W), lambda h, b: (blk_map(b), h)),
            pl.BlockSpec((cpb, HG_HPS, HG_DK, HG_DK), lambda h, b: (blk_map(b), h, 0, 0)),
        ],
        out_shape=[
            jax.ShapeDtypeStruct((S, HG_W), F32),
            jax.ShapeDtypeStruct((S // C, HG_HEADS, HG_DK, HG_DK), F32),
        ],
        scratch_shapes=[pltpu.VMEM((HG_HPS, HG_DK, HG_DK), F32)],
        compiler_params=_cparams(2),
    )(proj, proj, proj, lbl4)


def _gla_bwd(proj, lbl4, f_off, reverse, do, states, dproj, prev, name):
    S = proj.shape[0]
    PW = proj.shape[1]
    C = HG_CHUNK
    R = _rtile(S, 512)
    cpb, nblk = R // C, S // R
    d = 1 if reverse else 0
    blk_map = (lambda b: b) if reverse else (lambda b: nblk - 1 - b)
    final = prev is not None

    if final:
        assert HG_HPS == HG_HEADS and (OFF_Q, OFF_I, f_off) == (0, HG_W, 2 * HG_W)

    def body(*refs):
        if final:
            q_ref, i_ref, f_ref, lb_ref, do_ref, st_ref, pq_ref, pi_ref, _dp, o3_ref, dl_ref, ds_scr = refs
            dq_ref = di_ref = df_ref = o3_ref
        else:
            q_ref, i_ref, f_ref, lb_ref, do_ref, st_ref, dq_ref, di_ref, df_ref, dl_ref, ds_scr = refs
        out_off = (OFF_Q, OFF_I, f_off) if final else (0, 0, 0)
        blk = pl.program_id(1)

        @pl.when(blk == 0)
        def _():
            ds_scr[...] = jnp.zeros_like(ds_scr)
            dl_ref[...] = jnp.zeros_like(dl_ref)

        l = lb_ref[...]
        lbs = _sigmoid(l[2 * d:2 * d + 1, :] - l[2 * d + 1:2 * d + 2, :])
        keep = _gla_masks(reverse)
        keep_t = _gla_masks(not reverse)
        tri_t = jnp.where(keep_t, 1.0, 0.0).astype(CDT)
        end = 0 if reverse else C - 1
        is_end = lax.broadcasted_iota(jnp.int32, (C, HG_DK), 0) == end
        dl_all = [jnp.zeros((SUB, HG_DK), F32) for _ in range(HG_HPS)]
        for cc, heads in [(cc, [hh]) for cc in range(cpb) for hh in range(HG_HPS)]:
            css = [pl.ds(hh * HG_DK, HG_DK) for hh in heads]
            lb = [lbs[:, hh * HG_DK:(hh + 1) * HG_DK] for hh in heads]
            dl_acc = [dl_all[hh] for hh in heads]
            c = cc if reverse else cpb - 1 - cc
            sl = pl.ds(c * C, C)
            qraw = [q_ref[sl, cs] for cs in css]
            v = [i_ref[sl, cs] for cs in css]
            t = _gla_chunk_fwd(qraw, [f_ref[sl, cs] for cs in css], lb, keep, reverse)
            dob = [do_ref[sl, cs].astype(CDT) for cs in css]
            vb = _m(lambda x: x.astype(CDT), v)
            st = [st_ref[c, hh] for hh in heads]
            ds = [ds_scr[hh] for hh in heads]
            dsb = _m(lambda x: x.astype(CDT), ds)
            d_qs = _m(_nn, dob, st)
            d_a = _m(lambda x, y: jnp.where(keep, _nt(x, y), 0.0), dob, vb)
            d_qt = _m(lambda x, y: _dot3(_nn, x, y), d_a, t["kt"])
            d_kt = _m(lambda x, y: _dot3(_tn, x, y), d_a, t["qt"])
            d_v = _m(lambda a, x, k2, s_: _tn(a, x) + _nt(k2, s_), t["a"], dob, t["k2"], dsb)
            d_k2 = _m(_nn, vb, dsb)
            d_e = _m(lambda s_, x: jnp.sum(s_ * x, axis=0, keepdims=True), st, ds)
            new_ds = _m(lambda e_, x, y, qs: e_ * x + _tn(y, qs), t["e_end"], ds, dob, t["qs"])
            dq = _m(lambda a, ea, b_, eb_: a * ea + b_ * eb_, d_qt, t["eq"], d_qs, t["eb"])
            dk = _m(lambda a, ea, b_, eb_: a * ea + b_ * eb_, d_kt, t["ek"], d_k2, t["e2"])
            db_end = _m(lambda x, k_, e2, de, ee: jnp.sum(x * (k_ * e2), axis=0, keepdims=True) + de * ee,
                        d_k2, t["k"], t["e2"], d_e, t["e_end"])
            db = _m(lambda q_, dq_, k_, dk_, be: q_ * dq_ - k_ * dk_ + jnp.where(is_end, be, 0.0),
                    t["q"], dq, t["k"], dk, db_end)
            dg = _m(lambda x: _tri_matmul(tri_t, x), db)
            df = _m(lambda g_, f_, dk_: g_ / f_ - dk_, dg, t["f"], dk)
            dfraw = _m(lambda x, l_, s_: x * (1.0 - l_) * s_ * (1.0 - s_), df, lb, t["sg"])
            dl_acc = _m(lambda acc, x, s_: acc + _rowsum8(x * (1.0 - s_)), dl_acc, df, t["sg"])
            dqraw = _m(lambda x, s_, r: x * (s_ * (1.0 + r * (1.0 - s_))), dq, t["sq"], qraw)
            if final:
                dqraw = [x + pq_ref[sl, cs] for x, cs in zip(dqraw, css)]
                d_v = [x + pi_ref[sl, cs] for x, cs in zip(d_v, css)]
            for n, hh in enumerate(heads):
                dl_all[hh] = dl_acc[n]
                ds_scr[hh] = new_ds[n]
                for ref, off, val in zip((dq_ref, di_ref, df_ref), out_off, (dqraw[n], d_v[n], dfraw[n])):
                    ref[sl, pl.ds(off + hh * HG_DK, HG_DK)] = val.astype(ref.dtype)
        dl_ref[...] += jnp.concatenate(dl_all, axis=1) * (lbs * (1.0 - lbs))

    W = HG_HPS * HG_DK
    col = lambda off: (lambda h, b: (blk_map(b), off // W + h))
    blk = lambda: pl.BlockSpec((R, W), lambda h, b: (blk_map(b), h))
    ins = [proj, proj, proj, lbl4, do, states]
    in_specs = [
        pl.BlockSpec((R, W), col(OFF_Q)),
        pl.BlockSpec((R, W), col(OFF_I)),
        pl.BlockSpec((R, W), col(f_off)),
        pl.BlockSpec((4, W), lambda h, b: (0, h)),
        blk(),
        pl.BlockSpec((cpb, HG_HPS, HG_DK, HG_DK), lambda h, b: (blk_map(b), h, 0, 0)),
    ]
    dl_shape = jax.ShapeDtypeStruct((SUB, HG_W), F32)
    dl_spec = pl.BlockSpec((SUB, W), lambda h, b: (0, h))
    dp_shape = jax.ShapeDtypeStruct((S, PW), CDT)
    if final:
        ins += [prev[0], prev[1], dproj]
        in_specs += [blk(), blk(), pl.BlockSpec(memory_space=pl.ANY)]
        out_shape = [dp_shape, dl_shape]
        out_specs = [pl.BlockSpec((R, 3 * HG_W), lambda h, b: (blk_map(b), 0)), dl_spec]
        aliases = {8: 0}
    else:
        out_shape = [jax.ShapeDtypeStruct((S, HG_W), F32), jax.ShapeDtypeStruct((S, HG_W), F32), dp_shape, dl_shape]
        out_specs = [blk(), blk(), pl.BlockSpec((R, W), col(f_off)), dl_spec]
        aliases = {}
        if dproj is not None:
            ins += [dproj]
            in_specs += [pl.BlockSpec(memory_space=pl.ANY)]
            aliases = {6: 2}
    if (not final) and dproj is not None:
        def body_wrapped(*refs, _b=body):
            _b(*refs[:6], *refs[7:])
        kern = body_wrapped
    else:
        kern = body
    res = pl.pallas_call(
        kern,
        name=name,
        grid=(HG_HEADS // HG_HPS, nblk),
        in_specs=in_specs,
        out_specs=out_specs,
        out_shape=out_shape,
        scratch_shapes=[pltpu.VMEM((HG_HPS, HG_DK, HG_DK), F32)],
        input_output_aliases=aliases,
        compiler_params=_cparams(2),
    )(*ins)
    if final:
        return res[0], None, None, res[1]
    dq, di, dproj, dl = res
    return dproj, dq, di, dl


def _hgrn_post_fwd(o_f, o_b, proj, norm_g):
    S = o_f.shape[0]

    def epi(accs, tiles, rows):
        of, ob, graw = tiles
        ng = rows[0][:, :HG_DK]
        o = of + ob
        ys = []
        for h in range(HG_HEADS):
            oh = o[:, h * HG_DK:(h + 1) * HG_DK]
            rs = lax.rsqrt(jnp.mean(oh * oh, axis=-1, keepdims=True) + RMS_EPS)
            ys.append(oh * rs * ng * _sigmoid(graw[:, h * HG_DK:(h + 1) * HG_DK]))
        return [jnp.concatenate(ys, axis=1)], []

    tm = _rtile(S, 512)
    (y,) = _fused_mm("hgrn_post_fwd", "nn", [], S, HG_W, 1, tm, HG_W, 1, [(HG_W, CDT, 0, None)], epi,
                     tiles=[(o_f, 0), (o_b, 0), (proj, OFF_G)], rows=[(jnp.tile(norm_g, (1, HG_HEADS)), 0)])
    return y


def _hgrn_post_bwd(dy, o_f, o_b, proj, norm_g, dproj):
    S = o_f.shape[0]

    def epi(accs, tiles, rows):
        dyv, of, ob, graw = tiles
        ng = rows[0][:, :HG_DK]
        o = of + ob
        dos, dgs, dns = [], [], []
        for h in range(HG_HEADS):
            sl = slice(h * HG_DK, (h + 1) * HG_DK)
            oh, gh, dyh = o[:, sl], graw[:, sl], dyv[:, sl].astype(F32)
            rs = lax.rsqrt(jnp.mean(oh * oh, axis=-1, keepdims=True) + RMS_EPS)
            xh = oh * rs
            sg = _sigmoid(gh)
            dn = dyh * sg
            dgs.append(dyh * (xh * ng) * sg * (1.0 - sg))
            dns.append(dn * xh)
            dxh = dn * ng
            dos.append(rs * (dxh - xh * jnp.mean(dxh * xh, axis=-1, keepdims=True)))
        return [jnp.concatenate(dos, axis=1), jnp.concatenate(dgs, axis=1)], [jnp.concatenate(dns, axis=1)]

    tm = _rtile(S, 512)
    do, dproj, dn = _fused_mm("hgrn_post_bwd", "nn", [], S, HG_W, 1, tm, HG_W, 1,
                              [(HG_W, F32, 0, None), (dproj.shape[1], CDT, OFF_G, dproj)], epi,
                              tiles=[(dy, 0), (o_f, 0), (o_b, 0), (proj, OFF_G)],
                              rows=[(jnp.tile(norm_g, (1, HG_HEADS)), 0)], n_racc=1)
    return do, dproj, dn


def _copy_into(name, src, dst, off):
    S, W = src.shape
    tm = _rtile(S, 512)
    (dst,) = _fused_mm(name, "nn", [], S, W, 1, tm, W, 1, [(dst.shape[1], dst.dtype, off, dst)],
                       lambda accs, tiles, rows: ([tiles[0]], []), tiles=[(src, 0)])
    return dst


def _rms_stats(x):
    rs = lax.rsqrt(jnp.mean(x * x, axis=-1, keepdims=True) + RMS_EPS)
    return x * rs, rs


def _mla_up(proj, cf, sf, g_cq, g_ckv, wuq_p, wukv):
    S = proj.shape[0]
    tm = _rtile(S, 512)
    H = MLA_HEADS

    def body(cq_ref, ckv_ref, kr_ref, krot_ref, cf_ref, sf_ref, gq_ref, gkv_ref, wq_ref, wkv_ref,
             q_ref, k_ref, v_ref, vt_ref, cqn_ref, ckvn_ref):
        cqn = (_rms_stats(cq_ref[...])[0] * gq_ref[...]).astype(CDT)
        ckvn = (_rms_stats(ckv_ref[...])[0] * gkv_ref[...]).astype(CDT)
        cqn_ref[...] = cqn
        ckvn_ref[...] = ckvn
        cfv, sfv = cf_ref[...], sf_ref[...]
        r = _nn(cqn, wq_ref[0]) * MLA_QSCALE
        q_ref[0, :, 0:LANE] = r[:, 0:LANE].astype(CDT)
        q_ref[0, :, LANE:2 * LANE] = (r[:, LANE:2 * LANE] * cfv + r[:, 2 * LANE:3 * LANE] * sfv).astype(CDT)
        kv = _nn(ckvn, wkv_ref[0])
        k_ref[0, :, 0:LANE] = kv[:, 0:LANE].astype(CDT)
        k_ref[0, :, LANE:2 * LANE] = (kr_ref[...] * cfv + krot_ref[...] * sfv).astype(CDT)
        vv = kv[:, LANE:2 * LANE]
        v_ref[0] = vv.astype(CDT)
        vt_ref[0, 0, 0:LANE, :] = vv.T.astype(CDT)
        vt_ref[0, 0, LANE:VT_ROWS, :] = jnp.ones((VT_ROWS - LANE, tm), CDT)

    PWb = proj.shape[1]
    kr_off = PWb - KR_PAD
    cspec = lambda off, w: pl.BlockSpec((tm, w), lambda i, h, o=off // w: (i, o))
    return pl.pallas_call(
        body,
        name="mla_up_fwd",
        grid=(S // tm, H),
        in_specs=[
            cspec(OFF_CQ, MLA_RANK), cspec(OFF_CKV, MLA_RANK), cspec(kr_off, LANE), cspec(kr_off + LANE, LANE),
            pl.BlockSpec((tm, LANE), lambda i, h: (i, 0)), pl.BlockSpec((tm, LANE), lambda i, h: (i, 0)),
            pl.BlockSpec((1, MLA_RANK), lambda i, h: (0, 0)), pl.BlockSpec((1, MLA_RANK), lambda i, h: (0, 0)),
            pl.BlockSpec((1, MLA_RANK, 3 * LANE), lambda i, h: (h, 0, 0)),
            pl.BlockSpec((1, MLA_RANK, 2 * LANE), lambda i, h: (h, 0, 0)),
        ],
        out_specs=[
            pl.BlockSpec((1, tm, 2 * LANE), lambda i, h: (h, i, 0)),
            pl.BlockSpec((1, tm, 2 * LANE), lambda i, h: (h, i, 0)),
            pl.BlockSpec((1, tm, LANE), lambda i, h: (h, i, 0)),
            pl.BlockSpec((1, 1, VT_ROWS, tm), lambda i, h: (h, i, 0, 0)),
            pl.BlockSpec((tm, MLA_RANK), lambda i, h: (i, 0)),
            pl.BlockSpec((tm, MLA_RANK), lambda i, h: (i, 0)),
        ],
        out_shape=[
            jax.ShapeDtypeStruct((H, S, 2 * LANE), CDT), jax.ShapeDtypeStruct((H, S, 2 * LANE), CDT),
            jax.ShapeDtypeStruct((H, S, LANE), CDT), jax.ShapeDtypeStruct((H, S // tm, VT_ROWS, tm), CDT),
            jax.ShapeDtypeStruct((S, MLA_RANK), CDT), jax.ShapeDtypeStruct((S, MLA_RANK), CDT),
        ],
        compiler_params=_cparams(2),
    )(proj, proj, proj, proj, cf, sf, g_cq, g_ckv, wuq_p, wukv)


def _mla_attn_fwd(q_cat, k_cat, vt, xchg=None):
    H, S, _ = q_cat.shape
    tq = _tile(S, MLA_TQ)
    _, nkb, _, tk = vt.shape
    nq = S // tq
    nx = xchg.n if xchg is not None else 0

    def body(*refs):
        q_ref, k_ref, vt_ref = refs[:3]
        x_in = refs[3:3 + nx]
        y_ref, ot_ref, lse_ref = refs[3 + nx:6 + nx]
        x_out = refs[6 + nx:6 + 2 * nx]
        m_scr, acc_scr = refs[6 + 2 * nx:8 + 2 * nx]
        x_sems = refs[8 + 2 * nx:]
        h, i = pl.program_id(0), pl.program_id(1)
        if nx:
            @pl.when((h == 0) & (i == 0))
            def _():
                xchg.start(x_in, x_out, x_sems)

        nsub = MLA_FWD_SLABS if tq % (MLA_FWD_SLABS * LANE) == 0 else 1
        ws = tq // nsub
        subs = [pl.ds(s * ws, ws) for s in range(nsub)]
        qs = [q_ref[0, sb, :] for sb in subs]
        m_scr[...] = jnp.full_like(m_scr, -jnp.inf)
        acc_scr[...] = jnp.zeros_like(acc_scr)

        def step(j, carry):
            kj = k_ref[0, pl.ds(pl.multiple_of(j * tk, tk), tk), :]
            vtj = vt_ref[0, j]
            sts = [_nt(kj, qq) for qq in qs]
            m_old = [m_scr[:, sb] for sb in subs]
            m_new = _m(lambda mo, st: jnp.maximum(mo, jnp.max(st, axis=0, keepdims=True)), m_old, sts)
            pts = _m(lambda st, mn: jnp.exp2(st - mn), sts, m_new)
            pvs = _m(lambda pt: _nn(vtj, pt), pts)
            for sb, mo, mn, pv in zip(subs, m_old, m_new, pvs):
                acc_scr[:, sb] = jnp.exp2(mo - mn) * acc_scr[:, sb] + pv
                m_scr[:, sb] = mn
            return carry

        lax.fori_loop(0, nkb, step, 0, unroll=4 if nkb % 4 == 0 else 1)
        l = acc_scr[LANE:LANE + 1, :]
        ot = acc_scr[0:LANE, :] / l
        ot_ref[0] = ot
        y_ref[...] = ot.T.astype(CDT)
        lse_ref[0, 0] = m_scr[...] + jnp.log2(l)

        if nx:
            @pl.when((h == H - 1) & (i == nq - 1))
            def _():
                xchg.wait(x_in, x_out, x_sems)

    return pl.pallas_call(
        body,
        name="mla_attn_fwd",
        grid=(H, nq),
        in_specs=[
            pl.BlockSpec((1, tq, 2 * LANE), lambda h, i: (h, i, 0)),
            pl.BlockSpec((1, S, 2 * LANE), lambda h, i: (h, 0, 0)),
            pl.BlockSpec((1, nkb, VT_ROWS, tk), lambda h, i: (h, 0, 0, 0)),
        ] + (xchg.specs if nx else []),
        out_specs=[
            pl.BlockSpec((tq, LANE), lambda h, i: (i, h)),
            pl.BlockSpec((1, LANE, tq), lambda h, i: (h, 0, i)),
            pl.BlockSpec((1, 1, 1, tq), lambda h, i: (h, i, 0, 0)),
        ] + (xchg.specs if nx else []),
        out_shape=[
            jax.ShapeDtypeStruct((S, H * LANE), CDT),
            jax.ShapeDtypeStruct((H, LANE, S), F32),
            jax.ShapeDtypeStruct((H, nq, 1, tq), F32),
        ] + (xchg.out_shape if nx else []),
        scratch_shapes=[pltpu.VMEM((1, tq), F32), pltpu.VMEM((VT_ROWS, tq), F32)] + (xchg.scratch if nx else []),
        compiler_params=_cparams(2, side_effects=nx > 0),
    )(q_cat, k_cat, vt, *(xchg.arrs if nx else []))


def _mla_delta(dy, ot):
    H, _, S = ot.shape
    tq = _tile(S, MLA_TQ)
    nq = S // tq

    def body(dy_ref, ot_ref, d_ref):
        d_ref[0, 0] = jnp.sum(dy_ref[...].astype(F32).T * ot_ref[0], axis=0, keepdims=True)

    return pl.pallas_call(
        body,
        name="mla_delta",
        grid=(H, nq),
        in_specs=[pl.BlockSpec((tq, LANE), lambda h, i: (i, h)), pl.BlockSpec((1, LANE, tq), lambda h, i: (h, 0, i))],
        out_specs=pl.BlockSpec((1, 1, 1, tq), lambda h, i: (h, i, 0, 0)),
        out_shape=jax.ShapeDtypeStruct((H, nq, 1, tq), F32),
        compiler_params=_cparams(2),
    )(dy, ot)


def _mla_attn_bwd(q_cat, k_cat, v, dy, lse, delta):
    H, S, _ = q_cat.shape
    _, nq, _, tq = lse.shape
    tk = _tile(S, 512)
    nkb = S // tk

    def body(k_ref, v_ref, q_ref, do_ref, lse_ref, dl_ref, dk_ref, dv_ref, dq_ref, dk_scr, dv_scr):
        ki = pl.program_id(1)

        @pl.when(ki == 0)
        def _():
            dq_ref[...] = jnp.zeros_like(dq_ref)

        kb, vb = k_ref[0], v_ref[0]
        dk_scr[...] = jnp.zeros_like(dk_scr)
        dv_scr[...] = jnp.zeros_like(dv_scr)

        def step(i, carry):
            rows = pl.ds(pl.multiple_of(i * tq, tq), tq)
            qc = q_ref[0, rows, :]
            doc = do_ref[rows, :]
            pt = jnp.exp2(_nt(kb, qc) - lse_ref[0, i])
            dv_scr[...] += _nn(pt, doc)
            dst = (pt * (_nt(vb, doc) - dl_ref[0, i])).astype(CDT)
            dk_scr[...] += _nn(dst, qc)
            dq_ref[0, rows, :] += _tn(dst, kb)
            return carry

        lax.fori_loop(0, nq, step, 0, unroll=2 if nq % 2 == 0 else 1)
        dk_ref[0] = dk_scr[...] * (MLA_SCALE / MLA_QSCALE)
        dv_ref[0] = dv_scr[...]

    return pl.pallas_call(
        body,
        name="mla_attn_bwd",
        grid=(H, nkb),
        in_specs=[
            pl.BlockSpec((1, tk, 2 * LANE), lambda h, j: (h, j, 0)),
            pl.BlockSpec((1, tk, LANE), lambda h, j: (h, j, 0)),
            pl.BlockSpec((1, S, 2 * LANE), lambda h, j: (h, 0, 0)),
            pl.BlockSpec((S, LANE), lambda h, j: (0, h)),
            pl.BlockSpec((1, nq, 1, tq), lambda h, j: (h, 0, 0, 0)),
            pl.BlockSpec((1, nq, 1, tq), lambda h, j: (h, 0, 0, 0)),
        ],
        out_specs=[
            pl.BlockSpec((1, tk, 2 * LANE), lambda h, j: (h, j, 0)),
            pl.BlockSpec((1, tk, LANE), lambda h, j: (h, j, 0)),
            pl.BlockSpec((1, S, 2 * LANE), lambda h, j: (h, 0, 0)),
        ],
        out_shape=[
            jax.ShapeDtypeStruct((H, S, 2 * LANE), F32),
            jax.ShapeDtypeStruct((H, S, LANE), F32),
            jax.ShapeDtypeStruct((H, S, 2 * LANE), F32),
        ],
        scratch_shapes=[pltpu.VMEM((tk, 2 * LANE), F32), pltpu.VMEM((tk, LANE), F32)],
        compiler_params=_cparams(2),
    )(k_cat, v, q_cat, dy, lse, delta)


def _mla_up_bwd(dq_cat, dk_cat, dv, proj, cf, sf, g_cq, g_ckv, wuq_p, wukv, dproj):
    H, S, _ = dq_cat.shape
    tm = _rtile(S, 256)
    PW = proj.shape[1]
    kr_off = PW - KR_PAD

    assert OFF_CKV == OFF_CQ + MLA_RANK and OFF_CQ % (2 * MLA_RANK) == 0

    def body(dq_ref, dk_ref, dv_ref, cq_ref, ckv_ref, cf_ref, sf_ref, gq_ref, gkv_ref, wq_ref, wkv_ref, _dp,
             dqp_ref, dkvp_ref, dc_ref, dkr_ref, dgq_ref, dgkv_ref, aq_scr, akv_scr, akr_scr):
        i, h = pl.program_id(0), pl.program_id(1)
        dcq_ref, dckv_ref = dc_ref.at[:, 0:MLA_RANK], dc_ref.at[:, MLA_RANK:2 * MLA_RANK]

        @pl.when(h == 0)
        def _():
            aq_scr[...] = jnp.zeros_like(aq_scr)
            akv_scr[...] = jnp.zeros_like(akv_scr)
            akr_scr[...] = jnp.zeros_like(akr_scr)

        cfv, sfv = cf_ref[...], sf_ref[...]
        dq = dq_ref[0] * MLA_SCALE
        dqr = dq[:, LANE:2 * LANE]
        dqp = jnp.concatenate([dq[:, 0:LANE], dqr * cfv, dqr * sfv], axis=1).astype(CDT)
        dqp_ref[0] = dqp
        aq_scr[...] += _nt(dqp, wq_ref[0])
        dk = dk_ref[0]
        dkvp = jnp.concatenate([dk[:, 0:LANE], dv_ref[0]], axis=1).astype(CDT)
        dkvp_ref[0] = dkvp
        akv_scr[...] += _nt(dkvp, wkv_ref[0])
        akr_scr[...] += dk[:, LANE:2 * LANE]

        @pl.when(h == H - 1)
        def _():
            def rms_bwd(c_ref, g_ref, acc_ref, d_ref, dg_ref):
                xh, rs = _rms_stats(c_ref[...])
                dn = acc_ref[...]
                dxh = dn * g_ref[...]
                d_ref[...] = (rs * (dxh - xh * jnp.mean(dxh * xh, axis=-1, keepdims=True))).astype(d_ref.dtype)
                part = _rowsum8(dn * xh)

                @pl.when(i == 0)
                def _():
                    dg_ref[...] = part

                @pl.when(i > 0)
                def _():
                    dg_ref[...] += part

            rms_bwd(cq_ref, gq_ref, aq_scr, dcq_ref, dgq_ref)
            rms_bwd(ckv_ref, gkv_ref, akv_scr, dckv_ref, dgkv_ref)
            dkr = akr_scr[...]
            dkr_ref[...] = jnp.concatenate([dkr * cfv, dkr * sfv, jnp.zeros((tm, KR_PAD - 2 * LANE), F32)], axis=1).astype(dkr_ref.dtype)

    cspec = lambda off, w: pl.BlockSpec((tm, w), lambda i, h, o=off // w: (i, o))
    hspec = lambda w: pl.BlockSpec((1, tm, w), lambda i, h: (h, i, 0))
    outs = pl.pallas_call(
        body,
        name="mla_up_bwd",
        grid=(S // tm, H),
        in_specs=[
            hspec(2 * LANE), hspec(2 * LANE), hspec(LANE),
            cspec(OFF_CQ, MLA_RANK), cspec(OFF_CKV, MLA_RANK),
            pl.BlockSpec((tm, LANE), lambda i, h: (i, 0)), pl.BlockSpec((tm, LANE), lambda i, h: (i, 0)),
            pl.BlockSpec((1, MLA_RANK), lambda i, h: (0, 0)), pl.BlockSpec((1, MLA_RANK), lambda i, h: (0, 0)),
            pl.BlockSpec((1, MLA_RANK, 3 * LANE), lambda i, h: (h, 0, 0)),
            pl.BlockSpec((1, MLA_RANK, 2 * LANE), lambda i, h: (h, 0, 0)),
            pl.BlockSpec(memory_space=pl.ANY),
        ],
        out_specs=[
            hspec(3 * LANE), hspec(2 * LANE),
            pl.BlockSpec((tm, 2 * MLA_RANK), lambda i, h: (i, OFF_CQ // (2 * MLA_RANK))),
            pl.BlockSpec((tm, KR_PAD), lambda i, h: (i, 0)),
            pl.BlockSpec((SUB, MLA_RANK), lambda i, h: (0, 0)),
            pl.BlockSpec((SUB, MLA_RANK), lambda i, h: (0, 0)),
        ],
        out_shape=[
            jax.ShapeDtypeStruct((H, S, 3 * LANE), CDT), jax.ShapeDtypeStruct((H, S, 2 * LANE), CDT),
            jax.ShapeDtypeStruct(dproj.shape, dproj.dtype),
            jax.ShapeDtypeStruct((S, KR_PAD), CDT),
            jax.ShapeDtypeStruct((SUB, MLA_RANK), F32), jax.ShapeDtypeStruct((SUB, MLA_RANK), F32),
        ],
        scratch_shapes=[pltpu.VMEM((tm, MLA_RANK), F32), pltpu.VMEM((tm, MLA_RANK), F32), pltpu.VMEM((tm, LANE), F32)],
        input_output_aliases={11: 2},
        compiler_params=_cparams(2),
    )(dq_cat, dk_cat, dv, proj, proj, cf, sf, g_cq, g_ckv, wuq_p, wukv, dproj)
    dqp, dkvp, dproj, dkr, dgq, dgkv = outs
    dproj = _copy_into("dproj_kr", dkr, dproj, kr_off)
    return dproj, dqp, dkvp, dgq, dgkv


def _heads_tn(name, a, b):
    S, Ka = a.shape
    H, _, W = b.shape
    tk = _rtile(S, 1024)
    nk = S // tk

    def body(a_ref, b_ref, o_ref, acc):
        k = pl.program_id(1)

        @pl.when(k == 0)
        def _():
            acc[...] = jnp.zeros_like(acc)

        acc[...] += _tn(a_ref[...], b_ref[0])

        @pl.when(k == nk - 1)
        def _():
            o_ref[0] = acc[...].astype(o_ref.dtype)

    return pl.pallas_call(
        body,
        name=name,
        grid=(H, nk),
        in_specs=[pl.BlockSpec((tk, Ka), lambda h, k: (k, 0)), pl.BlockSpec((1, tk, W), lambda h, k: (h, k, 0))],
        out_specs=pl.BlockSpec((1, Ka, W), lambda h, k: (h, 0, 0)),
        out_shape=jax.ShapeDtypeStruct((H, Ka, W), CDT),
        scratch_shapes=[pltpu.VMEM((Ka, W), F32)],
        compiler_params=_cparams(2),
    )(a, b)


def _mem_softmax(q, k):
    s = _nt(q, k) * (MEM_HD ** -0.5)
    p = jnp.exp(s - jnp.max(s, axis=1, keepdims=True))
    return p / jnp.sum(p, axis=1, keepdims=True)


def _mem_attn_fwd(proj, memkv):
    S = proj.shape[0]
    Mm = memkv.shape[0]
    tm = _rtile(S, 512)

    def body(q_ref, k_ref, v_ref, y_ref):
        pn = _mem_softmax(q_ref[...], k_ref[...])
        y_ref[...] = _nn(pn, v_ref[...]).astype(y_ref.dtype)

    return pl.pallas_call(
        body,
        name="mem_attn_fwd",
        grid=(S // tm, MEM_HEADS),
        in_specs=[
            pl.BlockSpec((tm, MEM_HD), lambda i, h: (i, OFF_QM // MEM_HD + h)),
            pl.BlockSpec((Mm, MEM_HD), lambda i, h: (0, h)),
            pl.BlockSpec((Mm, MEM_HD), lambda i, h: (0, MEM_HEADS + h)),
        ],
        out_specs=pl.BlockSpec((tm, MEM_HD), lambda i, h: (i, h)),
        out_shape=jax.ShapeDtypeStruct((S, MEM_W), CDT),
        compiler_params=_cparams(2),
    )(proj, memkv, memkv)


def _mem_attn_bwd(dy, proj, memkv, dproj):
    S = proj.shape[0]
    Mm = memkv.shape[0]
    tm = _rtile(S, 512)
    scale = MEM_HD ** -0.5

    def body(dy_ref, q_ref, k_ref, v_ref, _dp, dq_ref, dk_ref, dv_ref):
        i = pl.program_id(1)
        q, k, dyv = q_ref[...].astype(CDT), k_ref[...], dy_ref[...]
        pn = _mem_softmax(q, k)
        dvp = _tn(pn, dyv)
        dp = _nt(dyv, v_ref[...])
        ds = pn * (dp - jnp.sum(dp * pn, axis=1, keepdims=True)) * scale
        dq_ref[...] = _nn(ds, k).astype(dq_ref.dtype)
        dkp = _tn(ds, q)

        @pl.when(i == 0)
        def _():
            dk_ref[...] = dkp
            dv_ref[...] = dvp

        @pl.when(i > 0)
        def _():
            dk_ref[...] += dkp
            dv_ref[...] += dvp

    dproj, dk, dv = pl.pallas_call(
        body,
        name="mem_attn_bwd",
        grid=(MEM_HEADS, S // tm),
        in_specs=[
            pl.BlockSpec((tm, MEM_HD), lambda h, i: (i, h)),
            pl.BlockSpec((tm, MEM_HD), lambda h, i: (i, OFF_QM // MEM_HD + h)),
            pl.BlockSpec((Mm, MEM_HD), lambda h, i: (0, h)),
            pl.BlockSpec((Mm, MEM_HD), lambda h, i: (0, MEM_HEADS + h)),
            pl.BlockSpec(memory_space=pl.ANY),
        ],
        out_specs=[
            pl.BlockSpec((tm, MEM_HD), lambda h, i: (i, OFF_QM // MEM_HD + h)),
            pl.BlockSpec((Mm, MEM_HD), lambda h, i: (0, h)),
            pl.BlockSpec((Mm, MEM_HD), lambda h, i: (0, h)),
        ],
        out_shape=[
            jax.ShapeDtypeStruct(dproj.shape, dproj.dtype),
            jax.ShapeDtypeStruct((Mm, MEM_W), F32),
            jax.ShapeDtypeStruct((Mm, MEM_W), F32),
        ],
        input_output_aliases={4: 0},
        compiler_params=_cparams(2),
    )(dy, proj, memkv, memkv, dproj)
    return dproj, dk, dv


def _small_allreduce(vec):
    NS = vec.shape[1]

    def body(v_ref, o_ref, gbuf, send, recv):
        x, y, c, me = _my_place()
        gbuf[me] = v_ref[...]
        copies = []
        for kk in range(1, N_DEV):
            peer, _ = _peer(x, y, c, kk)
            cp = pltpu.make_async_remote_copy(src_ref=v_ref, dst_ref=gbuf.at[me], send_sem=send.at[kk - 1],
                                              recv_sem=recv.at[kk - 1], device_id=peer, device_id_type=MESH)
            cp.start()
            copies.append(cp)
        for cp in copies:
            cp.wait()
        tot = gbuf[0]
        for d in range(1, N_DEV):
            tot = tot + gbuf[d]
        o_ref[...] = jnp.sum(tot, axis=0, keepdims=True)

    return pl.pallas_call(
        body,
        name="small_allreduce",
        in_specs=[pl.BlockSpec(memory_space=pltpu.VMEM)],
        out_specs=pl.BlockSpec(memory_space=pltpu.VMEM),
        out_shape=jax.ShapeDtypeStruct((1, NS), F32),
        scratch_shapes=[pltpu.VMEM((N_DEV, SUB, NS), F32), pltpu.SemaphoreType.DMA((N_DEV - 1,)),
                        pltpu.SemaphoreType.DMA((N_DEV - 1,))],
        compiler_params=pltpu.CompilerParams(has_side_effects=True, vmem_limit_bytes=V7X_VMEM_LIMIT),
    )(vec)


def _adamw_math(g, w, m, v):
    nm = ADAM_B1 * m + (1.0 - ADAM_B1) * g
    nv = ADAM_B2 * v + (1.0 - ADAM_B2) * (g * g)
    mh = nm / (1.0 - ADAM_B1 ** ADAM_STEP)
    vh = nv / (1.0 - ADAM_B2 ** ADAM_STEP)
    delta = -ADAM_LR * (mh / (jnp.sqrt(vh) + ADAM_EPS) + ADAM_WD * w)
    return delta, nm, nv


def _adam_big(name, recv, w, m, v):
    _, R, C = w.shape
    tr = _rtile(R, max(SUB, (ADAM_BLOCK_ELEMS // C) // SUB * SUB))

    def body(r_ref, w_ref, m_ref, v_ref, g_ref, d_ref, nm_ref, nv_ref):
        g = r_ref[0].astype(F32)
        for d in range(1, N_DEV):
            g = g + r_ref[d].astype(F32)
        delta, nm, nv = _adamw_math(g, w_ref[0], m_ref[0], v_ref[0])
        g_ref[0] = g
        d_ref[0] = delta
        nm_ref[0] = nm
        nv_ref[0] = nv

    blk = pl.BlockSpec((1, tr, C), lambda i: (0, i, 0))
    return pl.pallas_call(
        body,
        name=name,
        grid=(R // tr,),
        in_specs=[pl.BlockSpec((N_DEV, tr, C), lambda i: (0, i, 0)), blk, blk, blk],
        out_specs=[blk, blk, blk, blk],
        out_shape=[jax.ShapeDtypeStruct((1, R, C), F32)] * 4,
        compiler_params=_cparams(1),
    )(recv, w, m, v)


def _to_bf16(name, w):
    _, R, C = w.shape
    tr = _rtile(R, max(SUB, (ADAM_BLOCK_ELEMS // C) // SUB * SUB))

    def body(w_ref, o_ref):
        o_ref[...] = w_ref[0].astype(CDT)

    return pl.pallas_call(
        body,
        name=name,
        grid=(R // tr,),
        in_specs=[pl.BlockSpec((1, tr, C), lambda i: (0, i, 0))],
        out_specs=pl.BlockSpec((tr, C), lambda i: (i, 0)),
        out_shape=jax.ShapeDtypeStruct((R, C), CDT),
        compiler_params=_cparams(1),
    )(w)


def _adam_small(g, w, m, v):
    def body(g_ref, w_ref, m_ref, v_ref, d_ref, nm_ref, nv_ref):
        delta, nm, nv = _adamw_math(g_ref[...], w_ref[...], m_ref[...], v_ref[...])
        d_ref[...] = delta
        nm_ref[...] = nm
        nv_ref[...] = nv

    return pl.pallas_call(body, name="adam_small", out_shape=[jax.ShapeDtypeStruct(g.shape, F32)] * 3)(g, w, m, v)


def _rot(w, axis=-1):
    x1, x2 = jnp.split(w, 2, axis=axis)
    return jnp.concatenate([-x2, x1], axis=axis)


def _unrot(dw, axis=-1):
    d1, d2 = jnp.split(dw, 2, axis=axis)
    return jnp.concatenate([d2, -d1], axis=axis)


def _pad_cols(w, width):
    return jnp.pad(w, [(0, 0)] * (w.ndim - 1) + [(0, width - w.shape[-1])])


def kernel(x, mem, positions, ln_emb_g, ln_emb_b, hgrn_lb_logits, w_in, hgrn_norm_g, mla_g_cq, mla_g_ckv, mla_w_uq, mla_w_ukv, mem_w_kv, w_branch, w_o, ln1_g, ln1_b, w_ffn_gate, w_ffn_up, w_ffn_down, ln2_g, ln2_b, loss_target, m_ln_emb_g, m_ln_emb_b, m_hgrn_lb_logits, m_w_in, m_hgrn_norm_g, m_mla_g_cq, m_mla_g_ckv, m_mla_w_uq, m_mla_w_ukv, m_mem_w_kv, m_w_branch, m_w_o, m_ln1_g, m_ln1_b, m_w_ffn_gate, m_w_ffn_up, m_w_ffn_down, m_ln2_g, m_ln2_b, v_ln_emb_g, v_ln_emb_b, v_hgrn_lb_logits, v_w_in, v_hgrn_norm_g, v_mla_g_cq, v_mla_g_ckv, v_mla_w_uq, v_mla_w_ukv, v_mem_w_kv, v_w_branch, v_w_o, v_ln1_g, v_ln1_b, v_w_ffn_gate, v_w_ffn_up, v_w_ffn_down, v_ln2_g, v_ln2_b):
    x2, tgt = x[0], loss_target[0]
    S, D = x2.shape
    Mm = mem.shape[1]
    F = w_ffn_gate.shape[2] * N_DEV
    GW = 3 * D
    PW = OFF_GATE + GW + KR_PAD
    KR = OFF_GATE + GW
    NIN = w_in.shape[2] * N_DEV
    assert NIN == OFF_GATE + MLA_ROPE + GW
    _, _, _, me = _my_place()
    row = lambda a: a.reshape(1, -1)

    br3 = lambda a: a.reshape(1, 3 * BR_W, -1)
    tp = lambda a: jnp.swapaxes(a, 1, 2)
    big_w = [tp(w_in), mla_w_uq, mla_w_ukv, mem_w_kv, br3(w_branch), w_o, tp(w_ffn_gate), tp(w_ffn_up), w_ffn_down]
    big_m = [tp(m_w_in), m_mla_w_uq, m_mla_w_ukv, m_mem_w_kv, br3(m_w_branch), m_w_o, tp(m_w_ffn_gate), tp(m_w_ffn_up),
             m_w_ffn_down]
    big_v = [tp(v_w_in), v_mla_w_uq, v_mla_w_ukv, v_mem_w_kv, br3(v_w_branch), v_w_o, tp(v_w_ffn_gate), tp(v_w_ffn_up),
             v_w_ffn_down]
    transposed = (0, 6, 7)
    wnames = ["w_in", "w_uq", "w_ukv", "mem_w_kv", "w_branch", "w_o", "w_gate", "w_up", "w_down"]
    big_wb = [_to_bf16("bf16_" + nme, w) for nme, w in zip(wnames, big_w)]
    g_in, g_lb = _all_gather_two_level("weights_all_gather", [big_wb[0], hgrn_lb_logits.reshape(4, -1)])
    win_t = g_in.reshape(NIN, D)
    kr_w = win_t[OFF_QM:OFF_QM + MLA_ROPE]
    zeros64 = jnp.zeros_like(kr_w)
    win_pt = jnp.concatenate([win_t[:OFF_FB], win_t[OFF_FF:OFF_G], win_t[OFF_FB:OFF_FF], win_t[OFF_G:OFF_QM],
                              win_t[OFF_QM + MLA_ROPE:], kr_w, zeros64, _rot(kr_w, 0), zeros64,
                              jnp.zeros((KR_PAD - 2 * LANE, D), CDT)], axis=0)
    lbl4 = jnp.transpose(g_lb, (1, 0, 2)).reshape(4, -1)

    half = MLA_ROPE // 2
    inv_freq = jnp.power(ROPE_THETA, -jnp.arange(half, dtype=F32) / half)
    ang = positions[0].astype(F32)[:, None] * inv_freq
    cf = _pad_cols(jnp.tile(jnp.cos(ang), (1, 2)), LANE)
    sf = _pad_cols(jnp.tile(jnp.sin(ang), (1, 2)), LANE)

    tm512 = _rtile(S, 512)
    ident = lambda accs, tiles, rows: ([accs[0]], [])

    def epi_ln0(accs, tiles, rows):
        h = _ln_stats(tiles[0])[0] * rows[0] + rows[1]
        return [h, h], []

    h0, h0b = _fused_mm("ln_emb_fwd", "nn", [], S, D, 1, tm512, D, 1, [(D, F32, 0, None), (D, CDT, 0, None)], epi_ln0,
                        tiles=[(x2, 0)], rows=[(row(ln_emb_g), 0), (row(ln_emb_b), 0)])
    proj, g_uq, g_ukv, g_mkv, g_wb, g_wo = _fused_mm(
        "proj", "nt", [[(h0b, 0, win_pt, 0)]], S, PW, D, _rtile(S, 1024), _tile(PW, 512), D, [(PW, F32, 0, None)], ident,
        xchg=_Xchg(big_wb[1:6], False), msplit=2 if S % 2048 == 0 else 1)
    wuq_p =jnp.concatenate([g_uq[..., :MLA_NOPE], _pad_cols(g_uq[..., MLA_NOPE:], LANE),
                             _pad_cols(_rot(g_uq[..., MLA_NOPE:]), LANE)], axis=-1)
    wukv = g_ukv
    wmkv = g_mkv.reshape(-1, g_mkv.shape[-1])
    wb = jnp.transpose(g_wb.reshape(N_DEV, 3, BR_W, -1), (1, 2, 0, 3)).reshape(3, BR_W, D)
    wo = g_wo.reshape(-1, D)
    o_f, st_f = _gla_fwd(proj, lbl4, OFF_FF, False, "gla_fwd_f")
    o_b, st_b = _gla_fwd(proj, lbl4, OFF_FB, True, "gla_fwd_b")
    y_hg = _hgrn_post_fwd(o_f, o_b, proj, hgrn_norm_g)
    q_cat, k_cat, v_mla, vt_mla, cqn, ckvn = _mla_up(proj, cf, sf, mla_g_cq, mla_g_ckv, wuq_p, wukv)
    y_mla, ot, lse, g_wg, g_wu, g_wd = _mla_attn_fwd(q_cat, k_cat, vt_mla, _Xchg(big_wb[6:9], False))
    wg_t, wu_t = g_wg.reshape(F, D), g_wu.reshape(F, D)
    wd = g_wd.reshape(-1, D)
    memb = mem[0].astype(CDT)
    (memkv,) = _fused_mm("mem_kv", "nn", [[(memb, 0, wmkv, 0)]], Mm, 2 * MEM_W, D, Mm, _tile(2 * MEM_W, 512), D,
                         [(2 * MEM_W, CDT, 0, None)], ident)
    y_mem = _mem_attn_fwd(proj, memkv)
    ys = [y_hg, y_mla, y_mem]
    tnD = _tile(D, 512, OFF_GATE)

    def epi_branch(accs, tiles, rows):
        return [_sigmoid(tiles[0]) * accs[0] + _sigmoid(tiles[1]) * accs[1] + _sigmoid(tiles[2]) * accs[2]], []

    (merged,) = _fused_mm("branch_fwd", "nn", [[(ys[b], 0, wb[b], 0)] for b in range(3)], S, D, BR_W, tm512, tnD, BR_W,
                          [(D, CDT, 0, None)], epi_branch, tiles=[(proj, OFF_GATE + b * D) for b in range(3)])

    def epi_ln1(accs, tiles, rows):
        r1v = ALPHA * tiles[0] + accs[0]
        return [r1v, _ln_stats(r1v)[0] * rows[0] + rows[1]], []

    r1, h1b = _fused_mm("wo_ln1", "nn", [[(merged, 0, wo, 0)]], S, D, D, tm512, D, D,
                        [(D, F32, 0, None), (D, CDT, 0, None)], epi_ln1, tiles=[(h0, 0)], rows=[(ln1_g, 0), (ln1_b, 0)])
    tnF = _tile(F, 512)

    def epi_up(accs, tiles, rows):
        gp, up = accs
        return [gp, up, gp * _sigmoid(gp) * up], []

    tm1k, ms1k = _rtile(S, 1024), (2 if S % 2048 == 0 else 1)
    gpb, upb, act = _fused_mm("ffn_up", "nt", [[(h1b, 0, wg_t, 0)], [(h1b, 0, wu_t, 0)]], S, F, D, tm1k, tnF, D,
                              [(F, CDT, 0, None)] * 3, epi_up, msplit=ms1k)

    def epi_down(accs, tiles, rows):
        g1, b1, g2, b2 = rows
        h1 = _ln_stats(tiles[0])[0] * g1 + b1
        xh2, rstd2 = _ln_stats(ALPHA * h1 + accs[0])
        diff = xh2 * g2 + b2 - tiles[1]
        dh2 = diff * (1.0 / D)
        dr2v = _ln_bwd(dh2, xh2, rstd2, g2)
        return [dr2v, dr2v], [dh2 * xh2, dh2, diff * diff * (0.5 / D)]

    dr2, dr2b, dg2, db2, lossp = _fused_mm(
        "ffn_down_loss", "nn", [[(act, 0, wd, 0)]], S, D, F, tm512, D, _tile(F, 704), [(D, F32, 0, None), (D, CDT, 0, None)],
        epi_down, tiles=[(r1, 0), (tgt, 0)], rows=[(ln1_g, 0), (ln1_b, 0), (ln2_g, 0), (ln2_b, 0)], n_racc=3)

    def epi_dact(accs, tiles, rows):
        da, gp, up = accs[0], tiles[0].astype(F32), tiles[1].astype(F32)
        s = _sigmoid(gp)
        return [da * up * (s * (1.0 + gp * (1.0 - s))), da * (gp * s)], []

    dgp, dup = _fused_mm("ffn_dact", "nt", [[(dr2b, 0, wd, 0)]], S, F, D, tm1k, tnF, D, [(F, CDT, 0, None)] * 2,
                         epi_dact, tiles=[(gpb, 0), (upb, 0)], msplit=ms1k)
    tkS = _rtile(S, 2048)
    (d_wd,) = _fused_mm("dw_down", "tn", [[(act, 0, dr2b, 0)]], F, D, S, tnF, D, tkS, [(D, CDT, 0, None)], ident)
    d_wg_t, d_wu_t = _fused_mm("dw_gate_up", "tn", [[(dgp, 0, h1b, 0)], [(dup, 0, h1b, 0)]], F, D, S, tnF, _tile(D, 1024),
                               tkS, [(D, CDT, 0, None)] * 2, lambda accs, tiles, rows: (accs, []))

    def epi_dh1(accs, tiles, rows):
        dh1 = accs[0] + ALPHA * tiles[0]
        xh1, rstd1 = _ln_stats(tiles[1])
        dr1v = _ln_bwd(dh1, xh1, rstd1, rows[0])
        return [dr1v, dr1v], [dh1 * xh1, dh1]

    rows8 = lambda dw: dw.reshape(N_DEV, -1, dw.shape[-1])
    dr1, dr1b, dg1, db1, r_wg, r_wu, r_wd = _fused_mm(
        "dh1_ln1", "nn", [[(dgp, 0, wg_t, 0), (dup, 0, wu_t, 0)]], S, D, F, tm512, D, _tile(F, 704),
        [(D, F32, 0, None), (D, CDT, 0, None)], epi_dh1, tiles=[(dr2, 0), (r1, 0)], rows=[(ln1_g, 0)], n_racc=2,
        xchg=_Xchg([rows8(d_wg_t), rows8(d_wu_t), rows8(d_wd)], True))
    (dmerged,) = _fused_mm("dmerged", "nt", [[(dr1b, 0, wo, 0)]], S, D, D, tm512, _tile(D, 512), D, [(D, CDT, 0, None)], ident)
    (d_wo,) = _fused_mm("dw_o", "tn", [[(merged, 0, dr1b, 0)]], D, D, S, _tile(D, 512), D, tkS, [(D, CDT, 0, None)], ident)

    def epi_dbranch(accs, tiles, rows):
        dm, s = tiles[0].astype(F32), _sigmoid(tiles[1])
        return [dm * s, dm * accs[0] * s * (1.0 - s)], []

    dproj = None
    d_wbs, dys = [], []
    for b in range(3):
        du, dproj = _fused_mm(f"branch_bwd{b}", "nn", [[(ys[b], 0, wb[b], 0)]], S, D, BR_W, tm512, tnD, BR_W,
                              [(D, CDT, 0, None), (PW, CDT, OFF_GATE + b * D, dproj)], epi_dbranch,
                              tiles=[(dmerged, 0), (proj, OFF_GATE + b * D)])
        (dwb,) = _fused_mm(f"dw_branch{b}", "tn", [[(ys[b], 0, du, 0)]], BR_W, D, S, _tile(BR_W, 512), D, tkS,
                           [(D, CDT, 0, None)], ident)
        (dyb,) = _fused_mm(f"dy_branch{b}", "nt", [[(du, 0, wb[b], 0)]], S, BR_W, D, tm512, _tile(BR_W, 512), D,
                           [(BR_W, F32 if b == 0 else CDT, 0, None)], ident)
        d_wbs.append(dwb)
        dys.append(dyb)
    dy_hg, dy_mla, dy_mem = dys

    dproj, dk_mem, dv_mem = _mem_attn_bwd(dy_mem, proj, memkv, dproj)
    dkv_mem = jnp.concatenate([dk_mem, dv_mem], axis=1).astype(CDT)
    (d_wmkv,) = _fused_mm("dw_memkv", "tn", [[(memb, 0, dkv_mem, 0)]], D, 2 * MEM_W, Mm, _tile(D, 512), 2 * MEM_W, Mm,
                          [(2 * MEM_W, CDT, 0, None)], ident)

    delta = _mla_delta(dy_mla, ot)
    dk_cat, dv_h, dq_cat = _mla_attn_bwd(q_cat, k_cat, v_mla, dy_mla, lse, delta)
    dproj, dqp, dkvp, dgq, dgkv = _mla_up_bwd(dq_cat, dk_cat, dv_h, proj, cf, sf, mla_g_cq, mla_g_ckv, wuq_p, wukv, dproj)
    d_wuq_p = _heads_tn("dw_uq", cqn, dqp).astype(F32)
    d_wukv = _heads_tn("dw_ukv", ckvn, dkvp)
    d_wuq = jnp.concatenate([d_wuq_p[..., :MLA_NOPE],
                             d_wuq_p[..., LANE:LANE + MLA_ROPE] + _unrot(d_wuq_p[..., 2 * LANE:2 * LANE + MLA_ROPE])],
                            axis=-1).astype(CDT)

    do_hg, dproj, dng = _hgrn_post_bwd(dy_hg, o_f, o_b, proj, hgrn_norm_g, dproj)
    dproj, dq1, di1, dl_f = _gla_bwd(proj, lbl4, OFF_FF, False, do_hg, st_f, dproj, None, "gla_bwd_f")
    dproj, _, _, dl_b = _gla_bwd(proj, lbl4, OFF_FB, True, do_hg, st_b, dproj, (dq1, di1), "gla_bwd_b")

    def epi_dh0(accs, tiles, rows):
        dh0 = accs[0] + ALPHA * tiles[0]
        xh, rstd = _ln_stats(tiles[1])
        return [_ln_bwd(dh0, xh, rstd, rows[0])], [dh0 * xh, dh0]

    d_wb = jnp.transpose(jnp.stack(d_wbs).reshape(3, BR_W, N_DEV, -1), (2, 0, 1, 3)).reshape(N_DEV, 3 * BR_W, -1)
    d_win_pt, r_uq, r_ukv, r_mkv, r_wb, r_wo = _fused_mm(
        "dw_in", "tn", [[(dproj, 0, h0b, 0)]], PW, D, S, _tile(PW, 1536), _tile(D, 1024), tkS, [(D, CDT, 0, None)], ident,
        xchg=_Xchg([d_wuq, d_wukv, rows8(d_wmkv), d_wb, rows8(d_wo)], True))
    d_kr = (d_win_pt[KR:KR + MLA_ROPE].astype(F32) + _unrot(d_win_pt[KR + LANE:KR + LANE + MLA_ROPE].astype(F32), 0)).astype(CDT)
    d_win_t = jnp.concatenate([d_win_pt[:OFF_FB], d_win_pt[OFF_FF:OFF_G], d_win_pt[OFF_FB:OFF_FF],
                               d_win_pt[OFF_G:OFF_QM], d_kr, d_win_pt[OFF_QM:KR]], axis=0)
    grad_x, dge, dbe, r_in = _fused_mm(
        "dh0_ln_emb", "nn", [[(dproj, 0, win_pt, 0)]], S, D, PW, tm512, D, _tile(PW, 1536), [(D, F32, 0, None)], epi_dh0,
        tiles=[(dr1, 0), (x2, 0)], rows=[(row(ln_emb_g), 0)], n_racc=2, xchg=_Xchg([rows8(d_win_t)], True))

    recv = [r_in, r_uq, r_ukv, r_mkv, r_wb, r_wo, r_wg, r_wu, r_wd]
    names = ["w_in", "w_uq", "w_ukv", "mem_w_kv", "w_branch", "w_o", "w_gate", "w_up", "w_down"]
    big_out = [_adam_big("adam_" + nme, r, w, m_, v_) for nme, r, w, m_, v_ in zip(names, recv, big_w, big_m, big_v)]

    parts = [dge, dbe, dng, dgq, dgkv, dg1, db1, dg2, db2, dl_f, dl_b, lossp]
    widths = [p.shape[1] for p in parts]
    red = _small_allreduce(jnp.concatenate(parts, axis=1))[0]
    offs = [sum(widths[:i]) for i in range(len(widths))]
    rs = [red[o:o + w_] for o, w_ in zip(offs, widths)]
    g_le_g, g_le_b, g_ng, g_gq, g_gkv, g_l1g, g_l1b, g_l2g, g_l2b, g_dlf, g_dlb, g_loss = rs
    loss = jnp.sum(g_loss)
    g_ng = g_ng.reshape(HG_HEADS, HG_DK).sum(axis=0)
    dl0 = jnp.stack([g_dlf, g_dlb])
    g_lb_full = jnp.stack([dl0, -dl0], axis=1)
    lbw = hgrn_lb_logits.shape[2]
    g_lb = lax.dynamic_slice_in_dim(g_lb_full, me * lbw, lbw, axis=2)

    small_g = [g_le_g, g_le_b, g_lb, g_ng.reshape(1, -1), g_gq.reshape(1, -1), g_gkv.reshape(1, -1), g_l1g.reshape(1, -1),
               g_l1b.reshape(1, -1), g_l2g.reshape(1, -1), g_l2b.reshape(1, -1)]
    small_w = [ln_emb_g, ln_emb_b, hgrn_lb_logits, hgrn_norm_g, mla_g_cq, mla_g_ckv, ln1_g, ln1_b, ln2_g, ln2_b]
    small_m = [m_ln_emb_g, m_ln_emb_b, m_hgrn_lb_logits, m_hgrn_norm_g, m_mla_g_cq, m_mla_g_ckv, m_ln1_g, m_ln1_b, m_ln2_g, m_ln2_b]
    small_v = [v_ln_emb_g, v_ln_emb_b, v_hgrn_lb_logits, v_hgrn_norm_g, v_mla_g_cq, v_mla_g_ckv, v_ln1_g, v_ln1_b, v_ln2_g, v_ln2_b]
    small_g = [g.reshape(w.shape) for g, w in zip(small_g, small_w)]
    pack = lambda lst: jnp.concatenate([a.reshape(-1) for a in lst]).reshape(1, -1)
    s_delta, s_nm, s_nv = _adam_small(pack(small_g), pack(small_w), pack(small_m), pack(small_v))
    sizes = [w.size for w in small_w]
    soffs = [sum(sizes[:i]) for i in range(len(sizes))]
    unpack = lambda p: [p[0, o:o + n].reshape(w.shape) for o, n, w in zip(soffs, sizes, small_w)]
    s_delta, s_nm, s_nv = unpack(s_delta), unpack(s_nm), unpack(s_nv)

    def ordered(small, big):
        sm = list(small)
        big = [tp(b) if n in transposed else b for n, b in enumerate(big)]
        bg = [b.reshape(w.shape) for b, w in zip(big, [w_in, mla_w_uq, mla_w_ukv, mem_w_kv, w_branch, w_o, w_ffn_gate, w_ffn_up, w_ffn_down])]
        return [sm[0], sm[1], sm[2], bg[0], sm[3], sm[4], sm[5], bg[1], bg[2], bg[3], bg[4], bg[5], sm[6], sm[7], bg[6], bg[7], bg[8], sm[8], sm[9]]

    grads = ordered(small_g, [o[0] for o in big_out])
    deltas = ordered(s_delta, [o[1] for o in big_out])
    new_m = ordered(s_nm, [o[2] for o in big_out])
    new_v = ordered(s_nv, [o[3] for o in big_out])
    return (loss, grad_x[None], *grads, *deltas, *new_m, *new_v)
```

```python
import functools

import jax
import jax.numpy as jnp
from jax import lax
from jax.experimental import pallas as pl
from jax.experimental.pallas import tpu as pltpu

F32 = jnp.float32
CDT = jnp.bfloat16
MESH = pl.DeviceIdType.MESH
N_DEV = 8
V7X_VMEM_LIMIT = 60 * 1024 * 1024
LANE = 128
SUB = 8

HG_HEADS, HG_DK, HG_CHUNK = 8, 128, 64
HG_HPS = 8
HG_W = HG_HEADS * HG_DK
MLA_HEADS, MLA_RANK, MLA_NOPE, MLA_ROPE, MLA_V = 8, 512, 128, 64, 128
MLA_QK = MLA_NOPE + MLA_ROPE
MLA_SCALE = MLA_QK ** -0.5
MLA_QSCALE = MLA_SCALE * 1.4426950408889634
VT_ROWS = LANE + 16
MLA_TQ = 1024
MLA_FWD_SLABS = 4
MLA_W = MLA_HEADS * MLA_V
MEM_HEADS, MEM_HD = 4, 256
MEM_W = MEM_HEADS * MEM_HD
BR_W = 1024
ROPE_THETA = 10000.0
ALPHA = 2.0 ** 0.25
LN_EPS = 1e-5
RMS_EPS = 1e-6
ADAM_LR, ADAM_B1, ADAM_B2, ADAM_EPS, ADAM_WD, ADAM_STEP = 0.001, 0.9, 0.999, 1e-08, 0.01, 10
ADAM_BLOCK_ELEMS = 256 * 1024

OFF_Q, OFF_I, OFF_FB, OFF_FF, OFF_G = 0, 1024, 2048, 3072, 4096
OFF_CQ, OFF_CKV, OFF_QM, OFF_GATE = 5120, 5632, 6144, 7168
KR_PAD = 512


def _cparams(n_grid, side_effects=False):
    return pltpu.CompilerParams(dimension_semantics=("arbitrary",) * n_grid, vmem_limit_bytes=V7X_VMEM_LIMIT,
                                has_side_effects=side_effects)


def _tile(n, pref, *offsets):
    if n <= pref and all(o % n == 0 for o in offsets):
        return n
    t = (min(pref, n) // LANE) * LANE
    while t >= LANE:
        if n % t == 0 and all(o % t == 0 for o in offsets):
            return t
        t -= LANE
    raise ValueError(f"no tile for {n} {pref} {offsets}")


def _rtile(n, pref):
    if n <= pref:
        return n
    t = (pref // SUB) * SUB
    while t >= SUB:
        if n % t == 0:
            return t
        t -= SUB
    raise ValueError(f"no row tile for {n} {pref}")


def _dot(a, b, dims):
    return lax.dot_general(a.astype(CDT), b.astype(CDT), (dims, ((), ())), preferred_element_type=F32)


def _nn(a, b):
    return _dot(a, b, ((1,), (0,)))


def _nt(a, b):
    return _dot(a, b, ((1,), (1,)))


def _tn(a, b):
    return _dot(a, b, ((0,), (0,)))


_DOTS = {"nn": _nn, "nt": _nt, "tn": _tn}


def _sigmoid(x):
    return 1.0 / (1.0 + jnp.exp(-x))


def _rowsum8(v):
    r, w = v.shape
    return v.reshape(r // SUB, SUB, w).sum(axis=0)


def _my_place():
    x, y, c = lax.axis_index("x"), lax.axis_index("y"), lax.axis_index("c")
    return x, y, c, 4 * x + 2 * y + c


def _peer(x, y, c, kk):
    px = 1 - x if kk & 4 else x
    py = 1 - y if kk & 2 else y
    pc = 1 - c if kk & 1 else c
    return (px, py, pc), 4 * px + 2 * py + pc


class _Xchg:
    def __init__(self, arrs, scatter):
        self.arrs, self.scatter, self.n = list(arrs), scatter, len(arrs)
        hbm = pl.BlockSpec(memory_space=pl.ANY)
        self.specs = [hbm] * self.n
        self.out_shape = [jax.ShapeDtypeStruct(((N_DEV,) + a.shape[1:]) if scatter else ((N_DEV,) + a.shape), a.dtype)
                          for a in self.arrs]
        ncp = self.n * (N_DEV - 1)
        self.scratch = [pltpu.SemaphoreType.DMA((ncp,)), pltpu.SemaphoreType.DMA((ncp,)), pltpu.SemaphoreType.DMA((self.n,))]

    def _copies(self, ins, outs, send, recv, loc):
        x, y, c, me = _my_place()
        copies = []
        for w in range(self.n):
            copies.append(pltpu.make_async_copy(ins[w].at[me] if self.scatter else ins[w], outs[w].at[me], loc.at[w]))
            for kk in range(1, N_DEV):
                peer, pid = _peer(x, y, c, kk)
                s = w * (N_DEV - 1) + kk - 1
                copies.append(pltpu.make_async_remote_copy(
                    src_ref=ins[w].at[pid] if self.scatter else ins[w], dst_ref=outs[w].at[me],
                    send_sem=send.at[s], recv_sem=recv.at[s], device_id=peer, device_id_type=MESH))
        return copies

    def start(self, ins, outs, sems):
        for cp in self._copies(ins, outs, *sems):
            cp.start()

    def wait(self, ins, outs, sems):
        for cp in self._copies(ins, outs, *sems):
            cp.wait()


def _all_gather_two_level(name, arrs):
    n = len(arrs)
    NC = N_DEV - 1

    def body(*refs):
        ins, outs = refs[:n], refs[n:2 * n]
        send, recv, loc = refs[2 * n:]
        x, y, c, me = _my_place()
        sibling = (x, y, 1 - c)
        chips = [(1 - x, y), (x, 1 - y), (1 - x, 1 - y)]
        slot = lambda px, py, pc: 4 * px + 2 * py + pc

        def copy(w, k, block, to, src=None):
            dst = outs[w].at[slot(*block)]
            return pltpu.make_async_remote_copy(src_ref=dst if src is None else src, dst_ref=dst,
                                                send_sem=send.at[w * NC + k], recv_sem=recv.at[w * NC + k],
                                                device_id=to, device_id_type=MESH)

        mine = [pltpu.make_async_copy(ins[w], outs[w].at[me], loc.at[w]) for w in range(n)]
        for cp in mine:
            cp.start()
        first = []
        for w in range(n):
            first.append(copy(w, 0, (x, y, c), sibling, src=ins[w]))
            first += [copy(w, 1 + j, (x, y, c), (*chip, c), src=ins[w]) for j, chip in enumerate(chips)]
        for cp in first:
            cp.start()
        passed = []
        for j, chip in enumerate(chips):
            for w in range(n):
                copy(w, 1 + j, (*chip, c), (x, y, c)).wait_recv()
                fwd = copy(w, 4 + j, (*chip, c), sibling)
                fwd.start()
                passed.append(fwd)
        for w in range(n):
            copy(w, 0, sibling, (x, y, c)).wait_recv()
            for j, chip in enumerate(chips):
                copy(w, 4 + j, (*chip, 1 - c), (x, y, c)).wait_recv()
        for cp in first + passed:
            cp.wait_send()
        for cp in mine:
            cp.wait()

    hbm = pl.BlockSpec(memory_space=pl.ANY)
    return pl.pallas_call(
        body,
        name=name,
        in_specs=[hbm] * n,
        out_specs=[hbm] * n,
        out_shape=[jax.ShapeDtypeStruct((N_DEV,) + a.shape, a.dtype) for a in arrs],
        scratch_shapes=[pltpu.SemaphoreType.DMA((n * NC,)), pltpu.SemaphoreType.DMA((n * NC,)), pltpu.SemaphoreType.DMA((n,))],
        compiler_params=pltpu.CompilerParams(has_side_effects=True),
    )(*arrs)


def _fused_mm(name, mode, groups, M, N, K, tm, tn, tk, outs, epi, tiles=(), rows=(), n_racc=0, xchg=None, msplit=1):
    ni, nj, nk = M // tm, N // tn, K // tk
    assert M % tm == 0 and N % tn == 0 and K % tk == 0, (name, M, N, K, tm, tn, tk)
    assert n_racc == 0 or nj == 1
    assert msplit == 1 or (nk == 1 and n_racc == 0 and tm % (16 * msplit) == 0)
    dot = _DOTS[mode] if groups else None
    ins, in_specs = [], []
    for g in groups:
        for a, a_off, b, b_off in g:
            if mode == "tn":
                assert a_off % tm == 0
                in_specs.append(pl.BlockSpec((tk, tm), lambda i, j, k, o=a_off // tm: (k, i + o)))
            else:
                assert a_off % tk == 0
                in_specs.append(pl.BlockSpec((tm, tk), lambda i, j, k, o=a_off // tk: (i, k + o)))
            ins.append(a)
            if mode == "nt":
                assert b_off % tk == 0
                in_specs.append(pl.BlockSpec((tn, tk), lambda i, j, k, o=b_off // tk: (j, k + o)))
            else:
                assert b_off % tn == 0
                in_specs.append(pl.BlockSpec((tk, tn), lambda i, j, k, o=b_off // tn: (k, j + o)))
            ins.append(b)
    for arr, off in tiles:
        assert off % tn == 0
        ins.append(arr)
        in_specs.append(pl.BlockSpec((tm, tn), lambda i, j, k, o=off // tn: (i, j + o)))
    for arr, off in rows:
        assert off % tn == 0
        ins.append(arr)
        in_specs.append(pl.BlockSpec((1, tn), lambda i, j, k, o=off // tn: (0, j + o)))
    aliases = {}
    out_shape, out_specs = [], []
    for oi, (width, dtype, off, alias) in enumerate(outs):
        assert off % tn == 0
        if alias is not None:
            aliases[len(ins)] = oi
            ins.append(alias)
            in_specs.append(pl.BlockSpec(memory_space=pl.ANY))
        out_shape.append(jax.ShapeDtypeStruct((M, width), dtype))
        out_specs.append(pl.BlockSpec((tm, tn), lambda i, j, k, o=off // tn: (i, j + o)))
    for _ in range(n_racc):
        out_shape.append(jax.ShapeDtypeStruct((SUB, N), F32))
        out_specs.append(pl.BlockSpec((SUB, tn), lambda i, j, k: (0, 0)))
    n_alias = len(aliases)
    n_pairs = [len(g) for g in groups]
    use_scratch = nk > 1
    scratch = [pltpu.VMEM((tm, tn), F32) for _ in groups] if use_scratch else []
    nx = 0
    if xchg is not None:
        nx = xchg.n
        ins += xchg.arrs
        in_specs += xchg.specs
        out_shape += xchg.out_shape
        out_specs += xchg.specs
        scratch += xchg.scratch

    def body(*refs):
        it = iter(refs)
        pair_refs = [[(next(it), next(it)) for _ in range(n)] for n in n_pairs]
        tile_refs = [next(it) for _ in tiles]
        row_refs = [next(it) for _ in rows]
        for _ in range(n_alias):
            next(it)
        x_in = [next(it) for _ in range(nx)]
        out_refs = [next(it) for _ in outs]
        racc_refs = [next(it) for _ in range(n_racc)]
        x_out = [next(it) for _ in range(nx)]
        acc_refs = [next(it) for _ in groups] if use_scratch else []
        x_sems = list(it)
        i, j, k = pl.program_id(0), pl.program_id(1), pl.program_id(2)
        if nx:
            @pl.when((i == 0) & (j == 0) & (k == 0))
            def _():
                xchg.start(x_in, x_out, x_sems)

        def products():
            res = []
            for prs in pair_refs:
                s = None
                for a_ref, b_ref in prs:
                    d = dot(a_ref[...], b_ref[...])
                    s = d if s is None else s + d
                res.append(s)
            return res

        def finish(accs):
            out_v, racc_v = epi(accs, [t[...] for t in tile_refs], [r[...] for r in row_refs])
            for o_ref, v in zip(out_refs, out_v):
                o_ref[...] = v.astype(o_ref.dtype)
            for r_ref, v in zip(racc_refs, racc_v):
                part = _rowsum8(v)

                @pl.when(i == 0)
                def _():
                    r_ref[...] = part

                @pl.when(i > 0)
                def _():
                    r_ref[...] += part

        if not use_scratch and msplit > 1:
            ts = tm // msplit
            for s in range(msplit):
                rs = pl.ds(s * ts, ts)
                accs = []
                for prs in pair_refs:
                    acc = None
                    for a_ref, b_ref in prs:
                        dd = dot(a_ref[:, rs] if mode == "tn" else a_ref[rs, :], b_ref[...])
                        acc = dd if acc is None else acc + dd
                    accs.append(acc)
                out_v, _ = epi(accs, [t[rs, :] for t in tile_refs], [r[...] for r in row_refs])
                for o_ref, v in zip(out_refs, out_v):
                    o_ref[rs, :] = v.astype(o_ref.dtype)
        elif not use_scratch:
            finish(products())
        else:
            @pl.when(k == 0)
            def _():
                for acc in acc_refs:
                    acc[...] = jnp.zeros_like(acc)

            for acc, p in zip(acc_refs, products()):
                acc[...] += p

            @pl.when(k == nk - 1)
            def _():
                finish([acc[...] for acc in acc_refs])

        if nx:
            @pl.when((i == ni - 1) & (j == nj - 1) & (k == nk - 1))
            def _():
                xchg.wait(x_in, x_out, x_sems)

    res = pl.pallas_call(
        body,
        name=name,
        grid=(ni, nj, nk),
        in_specs=in_specs,
        out_specs=out_specs,
        out_shape=out_shape,
        scratch_shapes=scratch,
        input_output_aliases=aliases,
        compiler_params=_cparams(3, side_effects=nx > 0),
    )(*ins)
    return res


def _ln_stats(r):
    mu = jnp.mean(r, axis=-1, keepdims=True)
    xc = r - mu
    var = jnp.mean(xc * xc, axis=-1, keepdims=True)
    rstd = lax.rsqrt(var + LN_EPS)
    return xc * rstd, rstd


def _ln_bwd(dh, xhat, rstd, g):
    dxh = dh * g
    m1 = jnp.mean(dxh, axis=-1, keepdims=True)
    m2 = jnp.mean(dxh * xhat, axis=-1, keepdims=True)
    return rstd * (dxh - m1 - xhat * m2)


def _split3(x):
    hi = x.astype(CDT)
    r1 = x - hi.astype(F32)
    mid = r1.astype(CDT)
    lo = (r1 - mid.astype(F32)).astype(CDT)
    return hi, mid, lo


def _tri_matmul(tri, x):
    hi, mid, lo = _split3(x)
    return _nn(tri, hi) + _nn(tri, mid) + _nn(tri, lo)


def _dot3(dot, a, b):
    a_hi, b_hi = a.astype(CDT), b.astype(CDT)
    a_lo = (a - a_hi.astype(F32)).astype(CDT)
    b_lo = (b - b_hi.astype(F32)).astype(CDT)
    return dot(a_hi, b_hi) + dot(a_hi, b_lo) + dot(a_lo, b_hi)


def _gla_masks(reverse):
    C = HG_CHUNK
    r = lax.broadcasted_iota(jnp.int32, (C, C), 0)
    c = lax.broadcasted_iota(jnp.int32, (C, C), 1)
    keep = (c >= r) if reverse else (r >= c)
    return keep


def _m(fn, *lists):
    return [fn(*args) for args in zip(*lists)]


def _gla_chunk_fwd(qraw, fraw, lb, keep, reverse):
    C = HG_CHUNK
    end = 0 if reverse else C - 1
    tri = jnp.where(keep, 1.0, 0.0).astype(CDT)
    sq = _m(_sigmoid, qraw)
    q = _m(lambda x, s: x * s, qraw, sq)
    sg = _m(_sigmoid, fraw)
    f = _m(lambda l_, s: l_ + (1.0 - l_) * s, lb, sg)
    k = _m(lambda x: 1.0 - x, f)
    g = _m(jnp.log, f)
    b = _m(lambda x: _tri_matmul(tri, x), g)
    b_end = _m(lambda x: x[end:end + 1, :], b)
    b_mid = _m(lambda x: x[C // 2:C // 2 + 1, :], b)
    eq = _m(lambda x, m_: jnp.exp(x - m_), b, b_mid)
    ek = _m(lambda x, m_: jnp.exp(m_ - x), b, b_mid)
    eb = _m(jnp.exp, b)
    e2 = _m(lambda x, e_: jnp.exp(e_ - x), b, b_end)
    e_end = _m(jnp.exp, b_end)
    qt = _m(lambda x, e_: x * e_, q, eq)
    kt = _m(lambda x, e_: x * e_, k, ek)
    qs = _m(lambda x, e_: (x * e_).astype(CDT), q, eb)
    k2 = _m(lambda x, e_: (x * e_).astype(CDT), k, e2)
    a = _m(lambda x, y: jnp.where(keep, _dot3(_nt, x, y), 0.0).astype(CDT), qt, kt)
    return dict(sq=sq, q=q, sg=sg, f=f, k=k, eq=eq, ek=ek, eb=eb, e2=e2, e_end=e_end, qt=qt, kt=kt, qs=qs, k2=k2, a=a)


def _gla_fwd(proj, lbl4, f_off, reverse, name):
    S = proj.shape[0]
    C = HG_CHUNK
    R = _rtile(S, 512)
    cpb, nblk = R // C, S // R
    d = 1 if reverse else 0
    blk_map = (lambda b: nblk - 1 - b) if reverse else (lambda b: b)

    W = HG_HPS * HG_DK

    def body(q_ref, i_ref, f_ref, lb_ref, o_ref, st_ref, s_scr):
        @pl.when(pl.program_id(1) == 0)
        def _():
            s_scr[...] = jnp.zeros_like(s_scr)

        l = lb_ref[...]
        lbs = _sigmoid(l[2 * d:2 * d + 1, :] - l[2 * d + 1:2 * d + 2, :])
        keep = _gla_masks(reverse)
        heads = list(range(HG_HPS))
        css = [pl.ds(hh * HG_DK, HG_DK) for hh in heads]
        lb = [lbs[:, hh * HG_DK:(hh + 1) * HG_DK] for hh in heads]
        for cc in range(cpb):
            c = cpb - 1 - cc if reverse else cc
            sl = pl.ds(c * C, C)
            v = [i_ref[sl, cs] for cs in css]
            t = _gla_chunk_fwd([q_ref[sl, cs] for cs in css], [f_ref[sl, cs] for cs in css], lb, keep, reverse)
            st = [s_scr[hh] for hh in heads]
            o = _m(lambda qs, s_, a, v_: _nt(qs, s_) + _nn(a, v_), t["qs"], st, t["a"], v)
            new = _m(lambda e_, s_, v_, k2: e_ * s_ + _tn(v_, k2), t["e_end"], st, v, t["k2"])
            for hh in heads:
                st_ref[c, hh] = st[hh]
                o_ref[sl, css[hh]] = o[hh]
                s_scr[hh] = new[hh]

    col = lambda off: (lambda h, b: (blk_map(b), off // W + h))
    return pl.pallas_call(
        body,
        name=name,
        grid=(HG_HEADS // HG_HPS, nblk),
        in_specs=[
            pl.BlockSpec((R, W), col(OFF_Q)),
            pl.BlockSpec((R, W), col(OFF_I)),
            pl.BlockSpec((R, W), col(f_off)),
            pl.BlockSpec((4, W), lambda h, b: (0, h)),
        ],
        out_specs=[
            pl.BlockSpec((R, W), lambda h, b: (blk_map(b), h)),
            pl.BlockSpec((cpb, HG_HPS, HG_DK, HG_DK), lambda h, b: (blk_map(b), h, 0, 0)),
        ],
        out_shape=[
            jax.ShapeDtypeStruct((S, HG_W), F32),
            jax.ShapeDtypeStruct((S // C, HG_HEADS, HG_DK, HG_DK), F32),
        ],
        scratch_shapes=[pltpu.VMEM((HG_HPS, HG_DK, HG_DK), F32)],
        compiler_params=_cparams(2),
    )(proj, proj, proj, lbl4)


def _gla_bwd(proj, lbl4, f_off, reverse, do, states, dproj, prev, name):
    S = proj.shape[0]
    PW = proj.shape[1]
    C = HG_CHUNK
    R = _rtile(S, 512)
    cpb, nblk = R // C, S // R
    d = 1 if reverse else 0
    blk_map = (lambda b: b) if reverse else (lambda b: nblk - 1 - b)
    final = prev is not None

    if final:
        assert HG_HPS == HG_HEADS and (OFF_Q, OFF_I, f_off) == (0, HG_W, 2 * HG_W)

    def body(*refs):
        if final:
            q_ref, i_ref, f_ref, lb_ref, do_ref, st_ref, pq_ref, pi_ref, _dp, o3_ref, dl_ref, ds_scr = refs
            dq_ref = di_ref = df_ref = o3_ref
        else:
            q_ref, i_ref, f_ref, lb_ref, do_ref, st_ref, dq_ref, di_ref, df_ref, dl_ref, ds_scr = refs
        out_off = (OFF_Q, OFF_I, f_off) if final else (0, 0, 0)
        blk = pl.program_id(1)

        @pl.when(blk == 0)
        def _():
            ds_scr[...] = jnp.zeros_like(ds_scr)
            dl_ref[...] = jnp.zeros_like(dl_ref)

        l = lb_ref[...]
        lbs = _sigmoid(l[2 * d:2 * d + 1, :] - l[2 * d + 1:2 * d + 2, :])
        keep = _gla_masks(reverse)
        keep_t = _gla_masks(not reverse)
        tri_t = jnp.where(keep_t, 1.0, 0.0).astype(CDT)
        end = 0 if reverse else C - 1
        is_end = lax.broadcasted_iota(jnp.int32, (C, HG_DK), 0) == end
        dl_all = [jnp.zeros((SUB, HG_DK), F32) for _ in range(HG_HPS)]
        for cc, heads in [(cc, [hh]) for cc in range(cpb) for hh in range(HG_HPS)]:
            css = [pl.ds(hh * HG_DK, HG_DK) for hh in heads]
            lb = [lbs[:, hh * HG_DK:(hh + 1) * HG_DK] for hh in heads]
            dl_acc = [dl_all[hh] for hh in heads]
            c = cc if reverse else cpb - 1 - cc
            sl = pl.ds(c * C, C)
            qraw = [q_ref[sl, cs] for cs in css]
            v = [i_ref[sl, cs] for cs in css]
            t = _gla_chunk_fwd(qraw, [f_ref[sl, cs] for cs in css], lb, keep, reverse)
            dob = [do_ref[sl, cs].astype(CDT) for cs in css]
            vb = _m(lambda x: x.astype(CDT), v)
            st = [st_ref[c, hh] for hh in heads]
            ds = [ds_scr[hh] for hh in heads]
            dsb = _m(lambda x: x.astype(CDT), ds)
            d_qs = _m(_nn, dob, st)
            d_a = _m(lambda x, y: jnp.where(keep, _nt(x, y), 0.0), dob, vb)
            d_qt = _m(lambda x, y: _dot3(_nn, x, y), d_a, t["kt"])
            d_kt = _m(lambda x, y: _dot3(_tn, x, y), d_a, t["qt"])
            d_v = _m(lambda a, x, k2, s_: _tn(a, x) + _nt(k2, s_), t["a"], dob, t["k2"], dsb)
            d_k2 = _m(_nn, vb, dsb)
            d_e = _m(lambda s_, x: jnp.sum(s_ * x, axis=0, keepdims=True), st, ds)
            new_ds = _m(lambda e_, x, y, qs: e_ * x + _tn(y, qs), t["e_end"], ds, dob, t["qs"])
            dq = _m(lambda a, ea, b_, eb_: a * ea + b_ * eb_, d_qt, t["eq"], d_qs, t["eb"])
            dk = _m(lambda a, ea, b_, eb_: a * ea + b_ * eb_, d_kt, t["ek"], d_k2, t["e2"])
            db_end = _m(lambda x, k_, e2, de, ee: jnp.sum(x * (k_ * e2), axis=0, keepdims=True) + de * ee,
                        d_k2, t["k"], t["e2"], d_e, t["e_end"])
            db = _m(lambda q_, dq_, k_, dk_, be: q_ * dq_ - k_ * dk_ + jnp.where(is_end, be, 0.0),
                    t["q"], dq, t["k"], dk, db_end)
            dg = _m(lambda x: _tri_matmul(tri_t, x), db)
            df = _m(lambda g_, f_, dk_: g_ / f_ - dk_, dg, t["f"], dk)
            dfraw = _m(lambda x, l_, s_: x * (1.0 - l_) * s_ * (1.0 - s_), df, lb, t["sg"])
            dl_acc = _m(lambda acc, x, s_: acc + _rowsum8(x * (1.0 - s_)), dl_acc, df, t["sg"])
            dqraw = _m(lambda x, s_, r: x * (s_ * (1.0 + r * (1.0 - s_))), dq, t["sq"], qraw)
            if final:
                dqraw = [x + pq_ref[sl, cs] for x, cs in zip(dqraw, css)]
                d_v = [x + pi_ref[sl, cs] for x, cs in zip(d_v, css)]
            for n, hh in enumerate(heads):
                dl_all[hh] = dl_acc[n]
                ds_scr[hh] = new_ds[n]
                for ref, off, val in zip((dq_ref, di_ref, df_ref), out_off, (dqraw[n], d_v[n], dfraw[n])):
                    ref[sl, pl.ds(off + hh * HG_DK, HG_DK)] = val.astype(ref.dtype)
        dl_ref[...] += jnp.concatenate(dl_all, axis=1) * (lbs * (1.0 - lbs))

    W = HG_HPS * HG_DK
    col = lambda off: (lambda h, b: (blk_map(b), off // W + h))
    blk = lambda: pl.BlockSpec((R, W), lambda h, b: (blk_map(b), h))
    ins = [proj, proj, proj, lbl4, do, states]
    in_specs = [
        pl.BlockSpec((R, W), col(OFF_Q)),
        pl.BlockSpec((R, W), col(OFF_I)),
        pl.BlockSpec((R, W), col(f_off)),
        pl.BlockSpec((4, W), lambda h, b: (0, h)),
        blk(),
        pl.BlockSpec((cpb, HG_HPS, HG_DK, HG_DK), lambda h, b: (blk_map(b), h, 0, 0)),
    ]
    dl_shape = jax.ShapeDtypeStruct((SUB, HG_W), F32)
    dl_spec = pl.BlockSpec((SUB, W), lambda h, b: (0, h))
    dp_shape = jax.ShapeDtypeStruct((S, PW), CDT)
    if final:
        ins += [prev[0], prev[1], dproj]
        in_specs += [blk(), blk(), pl.BlockSpec(memory_space=pl.ANY)]
        out_shape = [dp_shape, dl_shape]
        out_specs = [pl.BlockSpec((R, 3 * HG_W), lambda h, b: (blk_map(b), 0)), dl_spec]
        aliases = {8: 0}
    else:
        out_shape = [jax.ShapeDtypeStruct((S, HG_W), F32), jax.ShapeDtypeStruct((S, HG_W), F32), dp_shape, dl_shape]
        out_specs = [blk(), blk(), pl.BlockSpec((R, W), col(f_off)), dl_spec]
        aliases = {}
        if dproj is not None:
            ins += [dproj]
            in_specs += [pl.BlockSpec(memory_space=pl.ANY)]
            aliases = {6: 2}
    if (not final) and dproj is not None:
        def body_wrapped(*refs, _b=body):
            _b(*refs[:6], *refs[7:])
        kern = body_wrapped
    else:
        kern = body
    res = pl.pallas_call(
        kern,
        name=name,
        grid=(HG_HEADS // HG_HPS, nblk),
        in_specs=in_specs,
        out_specs=out_specs,
        out_shape=out_shape,
        scratch_shapes=[pltpu.VMEM((HG_HPS, HG_DK, HG_DK), F32)],
        input_output_aliases=aliases,
        compiler_params=_cparams(2),
    )(*ins)
    if final:
        return res[0], None, None, res[1]
    dq, di, dproj, dl = res
    return dproj, dq, di, dl


def _hgrn_post_fwd(o_f, o_b, proj, norm_g):
    S = o_f.shape[0]

    def epi(accs, tiles, rows):
        of, ob, graw = tiles
        ng = rows[0][:, :HG_DK]
        o = of + ob
        ys = []
        for h in range(HG_HEADS):
            oh = o[:, h * HG_DK:(h + 1) * HG_DK]
            rs = lax.rsqrt(jnp.mean(oh * oh, axis=-1, keepdims=True) + RMS_EPS)
            ys.append(oh * rs * ng * _sigmoid(graw[:, h * HG_DK:(h + 1) * HG_DK]))
        return [jnp.concatenate(ys, axis=1)], []

    tm = _rtile(S, 512)
    (y,) = _fused_mm("hgrn_post_fwd", "nn", [], S, HG_W, 1, tm, HG_W, 1, [(HG_W, CDT, 0, None)], epi,
                     tiles=[(o_f, 0), (o_b, 0), (proj, OFF_G)], rows=[(jnp.tile(norm_g, (1, HG_HEADS)), 0)])
    return y


def _hgrn_post_bwd(dy, o_f, o_b, proj, norm_g, dproj):
    S = o_f.shape[0]

    def epi(accs, tiles, rows):
        dyv, of, ob, graw = tiles
        ng = rows[0][:, :HG_DK]
        o = of + ob
        dos, dgs, dns = [], [], []
        for h in range(HG_HEADS):
            sl = slice(h * HG_DK, (h + 1) * HG_DK)
            oh, gh, dyh = o[:, sl], graw[:, sl], dyv[:, sl].astype(F32)
            rs = lax.rsqrt(jnp.mean(oh * oh, axis=-1, keepdims=True) + RMS_EPS)
            xh = oh * rs
            sg = _sigmoid(gh)
            dn = dyh * sg
            dgs.append(dyh * (xh * ng) * sg * (1.0 - sg))
            dns.append(dn * xh)
            dxh = dn * ng
            dos.append(rs * (dxh - xh * jnp.mean(dxh * xh, axis=-1, keepdims=True)))
        return [jnp.concatenate(dos, axis=1), jnp.concatenate(dgs, axis=1)], [jnp.concatenate(dns, axis=1)]

    tm = _rtile(S, 512)
    do, dproj, dn = _fused_mm("hgrn_post_bwd", "nn", [], S, HG_W, 1, tm, HG_W, 1,
                              [(HG_W, F32, 0, None), (dproj.shape[1], CDT, OFF_G, dproj)], epi,
                              tiles=[(dy, 0), (o_f, 0), (o_b, 0), (proj, OFF_G)],
                              rows=[(jnp.tile(norm_g, (1, HG_HEADS)), 0)], n_racc=1)
    return do, dproj, dn


def _copy_into(name, src, dst, off):
    S, W = src.shape
    tm = _rtile(S, 512)
    (dst,) = _fused_mm(name, "nn", [], S, W, 1, tm, W, 1, [(dst.shape[1], dst.dtype, off, dst)],
                       lambda accs, tiles, rows: ([tiles[0]], []), tiles=[(src, 0)])
    return dst


def _rms_stats(x):
    rs = lax.rsqrt(jnp.mean(x * x, axis=-1, keepdims=True) + RMS_EPS)
    return x * rs, rs


def _mla_up(proj, cf, sf, g_cq, g_ckv, wuq_p, wukv):
    S = proj.shape[0]
    tm = _rtile(S, 512)
    H = MLA_HEADS

    def body(cq_ref, ckv_ref, kr_ref, krot_ref, cf_ref, sf_ref, gq_ref, gkv_ref, wq_ref, wkv_ref,
             q_ref, k_ref, v_ref, vt_ref, cqn_ref, ckvn_ref):
        cqn = (_rms_stats(cq_ref[...])[0] * gq_ref[...]).astype(CDT)
        ckvn = (_rms_stats(ckv_ref[...])[0] * gkv_ref[...]).astype(CDT)
        cqn_ref[...] = cqn
        ckvn_ref[...] = ckvn
        cfv, sfv = cf_ref[...], sf_ref[...]
        r = _nn(cqn, wq_ref[0]) * MLA_QSCALE
        q_ref[0, :, 0:LANE] = r[:, 0:LANE].astype(CDT)
        q_ref[0, :, LANE:2 * LANE] = (r[:, LANE:2 * LANE] * cfv + r[:, 2 * LANE:3 * LANE] * sfv).astype(CDT)
        kv = _nn(ckvn, wkv_ref[0])
        k_ref[0, :, 0:LANE] = kv[:, 0:LANE].astype(CDT)
        k_ref[0, :, LANE:2 * LANE] = (kr_ref[...] * cfv + krot_ref[...] * sfv).astype(CDT)
        vv = kv[:, LANE:2 * LANE]
        v_ref[0] = vv.astype(CDT)
        vt_ref[0, 0, 0:LANE, :] = vv.T.astype(CDT)
        vt_ref[0, 0, LANE:VT_ROWS, :] = jnp.ones((VT_ROWS - LANE, tm), CDT)

    PWb = proj.shape[1]
    kr_off = PWb - KR_PAD
    cspec = lambda off, w: pl.BlockSpec((tm, w), lambda i, h, o=off // w: (i, o))
    return pl.pallas_call(
        body,
        name="mla_up_fwd",
        grid=(S // tm, H),
        in_specs=[
            cspec(OFF_CQ, MLA_RANK), cspec(OFF_CKV, MLA_RANK), cspec(kr_off, LANE), cspec(kr_off + LANE, LANE),
            pl.BlockSpec((tm, LANE), lambda i, h: (i, 0)), pl.BlockSpec((tm, LANE), lambda i, h: (i, 0)),
            pl.BlockSpec((1, MLA_RANK), lambda i, h: (0, 0)), pl.BlockSpec((1, MLA_RANK), lambda i, h: (0, 0)),
            pl.BlockSpec((1, MLA_RANK, 3 * LANE), lambda i, h: (h, 0, 0)),
            pl.BlockSpec((1, MLA_RANK, 2 * LANE), lambda i, h: (h, 0, 0)),
        ],
        out_specs=[
            pl.BlockSpec((1, tm, 2 * LANE), lambda i, h: (h, i, 0)),
            pl.BlockSpec((1, tm, 2 * LANE), lambda i, h: (h, i, 0)),
            pl.BlockSpec((1, tm, LANE), lambda i, h: (h, i, 0)),
            pl.BlockSpec((1, 1, VT_ROWS, tm), lambda i, h: (h, i, 0, 0)),
            pl.BlockSpec((tm, MLA_RANK), lambda i, h: (i, 0)),
            pl.BlockSpec((tm, MLA_RANK), lambda i, h: (i, 0)),
        ],
        out_shape=[
            jax.ShapeDtypeStruct((H, S, 2 * LANE), CDT), jax.ShapeDtypeStruct((H, S, 2 * LANE), CDT),
            jax.ShapeDtypeStruct((H, S, LANE), CDT), jax.ShapeDtypeStruct((H, S // tm, VT_ROWS, tm), CDT),
            jax.ShapeDtypeStruct((S, MLA_RANK), CDT), jax.ShapeDtypeStruct((S, MLA_RANK), CDT),
        ],
        compiler_params=_cparams(2),
    )(proj, proj, proj, proj, cf, sf, g_cq, g_ckv, wuq_p, wukv)


def _mla_attn_fwd(q_cat, k_cat, vt, xchg=None):
    H, S, _ = q_cat.shape
    tq = _tile(S, MLA_TQ)
    _, nkb, _, tk = vt.shape
    nq = S // tq
    nx = xchg.n if xchg is not None else 0

    def body(*refs):
        q_ref, k_ref, vt_ref = refs[:3]
        x_in = refs[3:3 + nx]
        y_ref, ot_ref, lse_ref = refs[3 + nx:6 + nx]
        x_out = refs[6 + nx:6 + 2 * nx]
        m_scr, acc_scr = refs[6 + 2 * nx:8 + 2 * nx]
        x_sems = refs[8 + 2 * nx:]
        h, i = pl.program_id(0), pl.program_id(1)
        if nx:
            @pl.when((h == 0) & (i == 0))
            def _():
                xchg.start(x_in, x_out, x_sems)

        nsub = MLA_FWD_SLABS if tq % (MLA_FWD_SLABS * LANE) == 0 else 1
        ws = tq // nsub
        subs = [pl.ds(s * ws, ws) for s in range(nsub)]
        qs = [q_ref[0, sb, :] for sb in subs]
        m_scr[...] = jnp.full_like(m_scr, -jnp.inf)
        acc_scr[...] = jnp.zeros_like(acc_scr)

        def step(j, carry):
            kj = k_ref[0, pl.ds(pl.multiple_of(j * tk, tk), tk), :]
            vtj = vt_ref[0, j]
            sts = [_nt(kj, qq) for qq in qs]
            m_old = [m_scr[:, sb] for sb in subs]
            m_new = _m(lambda mo, st: jnp.maximum(mo, jnp.max(st, axis=0, keepdims=True)), m_old, sts)
            pts = _m(lambda st, mn: jnp.exp2(st - mn), sts, m_new)
            pvs = _m(lambda pt: _nn(vtj, pt), pts)
            for sb, mo, mn, pv in zip(subs, m_old, m_new, pvs):
                acc_scr[:, sb] = jnp.exp2(mo - mn) * acc_scr[:, sb] + pv
                m_scr[:, sb] = mn
            return carry

        lax.fori_loop(0, nkb, step, 0, unroll=4 if nkb % 4 == 0 else 1)
        l = acc_scr[LANE:LANE + 1, :]
        ot = acc_scr[0:LANE, :] / l
        ot_ref[0] = ot
        y_ref[...] = ot.T.astype(CDT)
        lse_ref[0, 0] = m_scr[...] + jnp.log2(l)

        if nx:
            @pl.when((h == H - 1) & (i == nq - 1))
            def _():
                xchg.wait(x_in, x_out, x_sems)

    return pl.pallas_call(
        body,
        name="mla_attn_fwd",
        grid=(H, nq),
        in_specs=[
            pl.BlockSpec((1, tq, 2 * LANE), lambda h, i: (h, i, 0)),
            pl.BlockSpec((1, S, 2 * LANE), lambda h, i: (h, 0, 0)),
            pl.BlockSpec((1, nkb, VT_ROWS, tk), lambda h, i: (h, 0, 0, 0)),
        ] + (xchg.specs if nx else []),
        out_specs=[
            pl.BlockSpec((tq, LANE), lambda h, i: (i, h)),
            pl.BlockSpec((1, LANE, tq), lambda h, i: (h, 0, i)),
            pl.BlockSpec((1, 1, 1, tq), lambda h, i: (h, i, 0, 0)),
        ] + (xchg.specs if nx else []),
        out_shape=[
            jax.ShapeDtypeStruct((S, H * LANE), CDT),
            jax.ShapeDtypeStruct((H, LANE, S), F32),
            jax.ShapeDtypeStruct((H, nq, 1, tq), F32),
        ] + (xchg.out_shape if nx else []),
        scratch_shapes=[pltpu.VMEM((1, tq), F32), pltpu.VMEM((VT_ROWS, tq), F32)] + (xchg.scratch if nx else []),
        compiler_params=_cparams(2, side_effects=nx > 0),
    )(q_cat, k_cat, vt, *(xchg.arrs if nx else []))


def _mla_delta(dy, ot):
    H, _, S = ot.shape
    tq = _tile(S, MLA_TQ)
    nq = S // tq

    def body(dy_ref, ot_ref, d_ref):
        d_ref[0, 0] = jnp.sum(dy_ref[...].astype(F32).T * ot_ref[0], axis=0, keepdims=True)

    return pl.pallas_call(
        body,
        name="mla_delta",
        grid=(H, nq),
        in_specs=[pl.BlockSpec((tq, LANE), lambda h, i: (i, h)), pl.BlockSpec((1, LANE, tq), lambda h, i: (h, 0, i))],
        out_specs=pl.BlockSpec((1, 1, 1, tq), lambda h, i: (h, i, 0, 0)),
        out_shape=jax.ShapeDtypeStruct((H, nq, 1, tq), F32),
        compiler_params=_cparams(2),
    )(dy, ot)


def _mla_attn_bwd(q_cat, k_cat, v, dy, lse, delta, xchg=None):
    H, S, _ = q_cat.shape
    _, nq, _, tq = lse.shape
    tk = _tile(S, 512)
    nkb = S // tk
    nx = xchg.n if xchg is not None else 0

    def body(*refs):
        k_ref, v_ref, q_ref, do_ref, lse_ref, dl_ref = refs[:6]
        x_in = refs[6:6 + nx]
        dk_ref, dv_ref, dq_ref = refs[6 + nx:9 + nx]
        x_out = refs[9 + nx:9 + 2 * nx]
        dk_scr, dv_scr = refs[9 + 2 * nx:11 + 2 * nx]
        x_sems = refs[11 + 2 * nx:]
        hd, ki = pl.program_id(0), pl.program_id(1)
        if nx:
            @pl.when((hd == 0) & (ki == 0))
            def _():
                xchg.start(x_in, x_out, x_sems)

        @pl.when(ki == 0)
        def _():
            dq_ref[...] = jnp.zeros_like(dq_ref)

        kb, vb = k_ref[0], v_ref[0]
        dk_scr[...] = jnp.zeros_like(dk_scr)
        dv_scr[...] = jnp.zeros_like(dv_scr)

        def step(i, carry):
            rows = pl.ds(pl.multiple_of(i * tq, tq), tq)
            qc = q_ref[0, rows, :]
            doc = do_ref[rows, :]
            pt = jnp.exp2(_nt(kb, qc) - lse_ref[0, i])
            dv_scr[...] += _nn(pt, doc)
            dst = (pt * (_nt(vb, doc) - dl_ref[0, i])).astype(CDT)
            dk_scr[...] += _nn(dst, qc)
            dq_ref[0, rows, :] += _tn(dst, kb)
            return carry

        lax.fori_loop(0, nq, step, 0, unroll=2 if nq % 2 == 0 else 1)
        dk_ref[0] = dk_scr[...] * (MLA_SCALE / MLA_QSCALE)
        dv_ref[0] = dv_scr[...]

        if nx:
            @pl.when((hd == H - 1) & (ki == nkb - 1))
            def _():
                xchg.wait(x_in, x_out, x_sems)

    return pl.pallas_call(
        body,
        name="mla_attn_bwd",
        grid=(H, nkb),
        in_specs=[
            pl.BlockSpec((1, tk, 2 * LANE), lambda h, j: (h, j, 0)),
            pl.BlockSpec((1, tk, LANE), lambda h, j: (h, j, 0)),
            pl.BlockSpec((1, S, 2 * LANE), lambda h, j: (h, 0, 0)),
            pl.BlockSpec((S, LANE), lambda h, j: (0, h)),
            pl.BlockSpec((1, nq, 1, tq), lambda h, j: (h, 0, 0, 0)),
            pl.BlockSpec((1, nq, 1, tq), lambda h, j: (h, 0, 0, 0)),
        ] + (xchg.specs if nx else []),
        out_specs=[
            pl.BlockSpec((1, tk, 2 * LANE), lambda h, j: (h, j, 0)),
            pl.BlockSpec((1, tk, LANE), lambda h, j: (h, j, 0)),
            pl.BlockSpec((1, S, 2 * LANE), lambda h, j: (h, 0, 0)),
        ] + (xchg.specs if nx else []),
        out_shape=[
            jax.ShapeDtypeStruct((H, S, 2 * LANE), F32),
            jax.ShapeDtypeStruct((H, S, LANE), F32),
            jax.ShapeDtypeStruct((H, S, 2 * LANE), F32),
        ] + (xchg.out_shape if nx else []),
        scratch_shapes=[pltpu.VMEM((tk, 2 * LANE), F32), pltpu.VMEM((tk, LANE), F32)] + (xchg.scratch if nx else []),
        compiler_params=_cparams(2, side_effects=nx > 0),
    )(k_cat, v, q_cat, dy, lse, delta, *(xchg.arrs if nx else []))


def _mla_up_bwd(dq_cat, dk_cat, dv, proj, cf, sf, g_cq, g_ckv, wuq_p, wukv, dproj):
    H, S, _ = dq_cat.shape
    tm = _rtile(S, 256)
    PW = proj.shape[1]
    kr_off = PW - KR_PAD

    assert OFF_CKV == OFF_CQ + MLA_RANK and OFF_CQ % (2 * MLA_RANK) == 0

    def body(dq_ref, dk_ref, dv_ref, cq_ref, ckv_ref, cf_ref, sf_ref, gq_ref, gkv_ref, wq_ref, wkv_ref, _dp,
             dqp_ref, dkvp_ref, dc_ref, dkr_ref, dgq_ref, dgkv_ref, aq_scr, akv_scr, akr_scr):
        i, h = pl.program_id(0), pl.program_id(1)
        dcq_ref, dckv_ref = dc_ref.at[:, 0:MLA_RANK], dc_ref.at[:, MLA_RANK:2 * MLA_RANK]

        @pl.when(h == 0)
        def _():
            aq_scr[...] = jnp.zeros_like(aq_scr)
            akv_scr[...] = jnp.zeros_like(akv_scr)
            akr_scr[...] = jnp.zeros_like(akr_scr)

        cfv, sfv = cf_ref[...], sf_ref[...]
        dq = dq_ref[0] * MLA_SCALE
        dqr = dq[:, LANE:2 * LANE]
        dqp = jnp.concatenate([dq[:, 0:LANE], dqr * cfv, dqr * sfv], axis=1).astype(CDT)
        dqp_ref[0] = dqp
        aq_scr[...] += _nt(dqp, wq_ref[0])
        dk = dk_ref[0]
        dkvp = jnp.concatenate([dk[:, 0:LANE], dv_ref[0]], axis=1).astype(CDT)
        dkvp_ref[0] = dkvp
        akv_scr[...] += _nt(dkvp, wkv_ref[0])
        akr_scr[...] += dk[:, LANE:2 * LANE]

        @pl.when(h == H - 1)
        def _():
            def rms_bwd(c_ref, g_ref, acc_ref, d_ref, dg_ref):
                xh, rs = _rms_stats(c_ref[...])
                dn = acc_ref[...]
                dxh = dn * g_ref[...]
                d_ref[...] = (rs * (dxh - xh * jnp.mean(dxh * xh, axis=-1, keepdims=True))).astype(d_ref.dtype)
                part = _rowsum8(dn * xh)

                @pl.when(i == 0)
                def _():
                    dg_ref[...] = part

                @pl.when(i > 0)
                def _():
                    dg_ref[...] += part

            rms_bwd(cq_ref, gq_ref, aq_scr, dcq_ref, dgq_ref)
            rms_bwd(ckv_ref, gkv_ref, akv_scr, dckv_ref, dgkv_ref)
            dkr = akr_scr[...]
            dkr_ref[...] = jnp.concatenate([dkr * cfv, dkr * sfv, jnp.zeros((tm, KR_PAD - 2 * LANE), F32)], axis=1).astype(dkr_ref.dtype)

    cspec = lambda off, w: pl.BlockSpec((tm, w), lambda i, h, o=off // w: (i, o))
    hspec = lambda w: pl.BlockSpec((1, tm, w), lambda i, h: (h, i, 0))
    outs = pl.pallas_call(
        body,
        name="mla_up_bwd",
        grid=(S // tm, H),
        in_specs=[
            hspec(2 * LANE), hspec(2 * LANE), hspec(LANE),
            cspec(OFF_CQ, MLA_RANK), cspec(OFF_CKV, MLA_RANK),
            pl.BlockSpec((tm, LANE), lambda i, h: (i, 0)), pl.BlockSpec((tm, LANE), lambda i, h: (i, 0)),
            pl.BlockSpec((1, MLA_RANK), lambda i, h: (0, 0)), pl.BlockSpec((1, MLA_RANK), lambda i, h: (0, 0)),
            pl.BlockSpec((1, MLA_RANK, 3 * LANE), lambda i, h: (h, 0, 0)),
            pl.BlockSpec((1, MLA_RANK, 2 * LANE), lambda i, h: (h, 0, 0)),
            pl.BlockSpec(memory_space=pl.ANY),
        ],
        out_specs=[
            hspec(3 * LANE), hspec(2 * LANE),
            pl.BlockSpec((tm, 2 * MLA_RANK), lambda i, h: (i, OFF_CQ // (2 * MLA_RANK))),
            pl.BlockSpec((tm, KR_PAD), lambda i, h: (i, 0)),
            pl.BlockSpec((SUB, MLA_RANK), lambda i, h: (0, 0)),
            pl.BlockSpec((SUB, MLA_RANK), lambda i, h: (0, 0)),
        ],
        out_shape=[
            jax.ShapeDtypeStruct((H, S, 3 * LANE), CDT), jax.ShapeDtypeStruct((H, S, 2 * LANE), CDT),
            jax.ShapeDtypeStruct(dproj.shape, dproj.dtype),
            jax.ShapeDtypeStruct((S, KR_PAD), CDT),
            jax.ShapeDtypeStruct((SUB, MLA_RANK), F32), jax.ShapeDtypeStruct((SUB, MLA_RANK), F32),
        ],
        scratch_shapes=[pltpu.VMEM((tm, MLA_RANK), F32), pltpu.VMEM((tm, MLA_RANK), F32), pltpu.VMEM((tm, LANE), F32)],
        input_output_aliases={11: 2},
        compiler_params=_cparams(2),
    )(dq_cat, dk_cat, dv, proj, proj, cf, sf, g_cq, g_ckv, wuq_p, wukv, dproj)
    dqp, dkvp, dproj, dkr, dgq, dgkv = outs
    dproj = _copy_into("dproj_kr", dkr, dproj, kr_off)
    return dproj, dqp, dkvp, dgq, dgkv


def _heads_tn(name, a, b):
    S, Ka = a.shape
    H, _, W = b.shape
    tk = _rtile(S, 1024)
    nk = S // tk

    def body(a_ref, b_ref, o_ref, acc):
        k = pl.program_id(1)

        @pl.when(k == 0)
        def _():
            acc[...] = jnp.zeros_like(acc)

        acc[...] += _tn(a_ref[...], b_ref[0])

        @pl.when(k == nk - 1)
        def _():
            o_ref[0] = acc[...].astype(o_ref.dtype)

    return pl.pallas_call(
        body,
        name=name,
        grid=(H, nk),
        in_specs=[pl.BlockSpec((tk, Ka), lambda h, k: (k, 0)), pl.BlockSpec((1, tk, W), lambda h, k: (h, k, 0))],
        out_specs=pl.BlockSpec((1, Ka, W), lambda h, k: (h, 0, 0)),
        out_shape=jax.ShapeDtypeStruct((H, Ka, W), CDT),
        scratch_shapes=[pltpu.VMEM((Ka, W), F32)],
        compiler_params=_cparams(2),
    )(a, b)


def _mem_softmax(q, k):
    s = _nt(q, k) * (MEM_HD ** -0.5)
    p = jnp.exp(s - jnp.max(s, axis=1, keepdims=True))
    return p / jnp.sum(p, axis=1, keepdims=True)


def _mem_attn_fwd(proj, memkv):
    S = proj.shape[0]
    Mm = memkv.shape[0]
    tm = _rtile(S, 512)

    def body(q_ref, k_ref, v_ref, y_ref):
        pn = _mem_softmax(q_ref[...], k_ref[...])
        y_ref[...] = _nn(pn, v_ref[...]).astype(y_ref.dtype)

    return pl.pallas_call(
        body,
        name="mem_attn_fwd",
        grid=(S // tm, MEM_HEADS),
        in_specs=[
            pl.BlockSpec((tm, MEM_HD), lambda i, h: (i, OFF_QM // MEM_HD + h)),
            pl.BlockSpec((Mm, MEM_HD), lambda i, h: (0, h)),
            pl.BlockSpec((Mm, MEM_HD), lambda i, h: (0, MEM_HEADS + h)),
        ],
        out_specs=pl.BlockSpec((tm, MEM_HD), lambda i, h: (i, h)),
        out_shape=jax.ShapeDtypeStruct((S, MEM_W), CDT),
        compiler_params=_cparams(2),
    )(proj, memkv, memkv)


def _mem_attn_bwd(dy, proj, memkv, dproj):
    S = proj.shape[0]
    Mm = memkv.shape[0]
    tm = _rtile(S, 512)
    scale = MEM_HD ** -0.5

    def body(dy_ref, q_ref, k_ref, v_ref, _dp, dq_ref, dk_ref, dv_ref):
        i = pl.program_id(1)
        q, k, dyv = q_ref[...].astype(CDT), k_ref[...], dy_ref[...]
        pn = _mem_softmax(q, k)
        dvp = _tn(pn, dyv)
        dp = _nt(dyv, v_ref[...])
        ds = pn * (dp - jnp.sum(dp * pn, axis=1, keepdims=True)) * scale
        dq_ref[...] = _nn(ds, k).astype(dq_ref.dtype)
        dkp = _tn(ds, q)

        @pl.when(i == 0)
        def _():
            dk_ref[...] = dkp
            dv_ref[...] = dvp

        @pl.when(i > 0)
        def _():
            dk_ref[...] += dkp
            dv_ref[...] += dvp

    dproj, dk, dv = pl.pallas_call(
        body,
        name="mem_attn_bwd",
        grid=(MEM_HEADS, S // tm),
        in_specs=[
            pl.BlockSpec((tm, MEM_HD), lambda h, i: (i, h)),
            pl.BlockSpec((tm, MEM_HD), lambda h, i: (i, OFF_QM // MEM_HD + h)),
            pl.BlockSpec((Mm, MEM_HD), lambda h, i: (0, h)),
            pl.BlockSpec((Mm, MEM_HD), lambda h, i: (0, MEM_HEADS + h)),
            pl.BlockSpec(memory_space=pl.ANY),
        ],
        out_specs=[
            pl.BlockSpec((tm, MEM_HD), lambda h, i: (i, OFF_QM // MEM_HD + h)),
            pl.BlockSpec((Mm, MEM_HD), lambda h, i: (0, h)),
            pl.BlockSpec((Mm, MEM_HD), lambda h, i: (0, h)),
        ],
        out_shape=[
            jax.ShapeDtypeStruct(dproj.shape, dproj.dtype),
            jax.ShapeDtypeStruct((Mm, MEM_W), F32),
            jax.ShapeDtypeStruct((Mm, MEM_W), F32),
        ],
        input_output_aliases={4: 0},
        compiler_params=_cparams(2),
    )(dy, proj, memkv, memkv, dproj)
    return dproj, dk, dv


def _small_allreduce(vec):
    NS = vec.shape[1]

    def body(v_ref, o_ref, gbuf, send, recv):
        x, y, c, me = _my_place()
        gbuf[me] = v_ref[...]
        copies = []
        for kk in range(1, N_DEV):
            peer, _ = _peer(x, y, c, kk)
            cp = pltpu.make_async_remote_copy(src_ref=v_ref, dst_ref=gbuf.at[me], send_sem=send.at[kk - 1],
                                              recv_sem=recv.at[kk - 1], device_id=peer, device_id_type=MESH)
            cp.start()
            copies.append(cp)
        for cp in copies:
            cp.wait()
        tot = gbuf[0]
        for d in range(1, N_DEV):
            tot = tot + gbuf[d]
        o_ref[...] = jnp.sum(tot, axis=0, keepdims=True)

    return pl.pallas_call(
        body,
        name="small_allreduce",
        in_specs=[pl.BlockSpec(memory_space=pltpu.VMEM)],
        out_specs=pl.BlockSpec(memory_space=pltpu.VMEM),
        out_shape=jax.ShapeDtypeStruct((1, NS), F32),
        scratch_shapes=[pltpu.VMEM((N_DEV, SUB, NS), F32), pltpu.SemaphoreType.DMA((N_DEV - 1,)),
                        pltpu.SemaphoreType.DMA((N_DEV - 1,))],
        compiler_params=pltpu.CompilerParams(has_side_effects=True, vmem_limit_bytes=V7X_VMEM_LIMIT),
    )(vec)


def _adamw_math(g, w, m, v):
    nm = ADAM_B1 * m + (1.0 - ADAM_B1) * g
    nv = ADAM_B2 * v + (1.0 - ADAM_B2) * (g * g)
    mh = nm / (1.0 - ADAM_B1 ** ADAM_STEP)
    vh = nv / (1.0 - ADAM_B2 ** ADAM_STEP)
    delta = -ADAM_LR * (mh / (jnp.sqrt(vh) + ADAM_EPS) + ADAM_WD * w)
    return delta, nm, nv


def _adam_big(name, recv, w, m, v):
    _, R, C = w.shape
    tr = _rtile(R, max(SUB, (ADAM_BLOCK_ELEMS // C) // SUB * SUB))

    def body(r_ref, w_ref, m_ref, v_ref, g_ref, d_ref, nm_ref, nv_ref):
        g = r_ref[0].astype(F32)
        for d in range(1, N_DEV):
            g = g + r_ref[d].astype(F32)
        delta, nm, nv = _adamw_math(g, w_ref[0], m_ref[0], v_ref[0])
        g_ref[0] = g
        d_ref[0] = delta
        nm_ref[0] = nm
        nv_ref[0] = nv

    blk = pl.BlockSpec((1, tr, C), lambda i: (0, i, 0))
    return pl.pallas_call(
        body,
        name=name,
        grid=(R // tr,),
        in_specs=[pl.BlockSpec((N_DEV, tr, C), lambda i: (0, i, 0)), blk, blk, blk],
        out_specs=[blk, blk, blk, blk],
        out_shape=[jax.ShapeDtypeStruct((1, R, C), F32)] * 4,
        compiler_params=_cparams(1),
    )(recv, w, m, v)


def _to_bf16(name, w):
    _, R, C = w.shape
    tr = _rtile(R, max(SUB, (ADAM_BLOCK_ELEMS // C) // SUB * SUB))

    def body(w_ref, o_ref):
        o_ref[...] = w_ref[0].astype(CDT)

    return pl.pallas_call(
        body,
        name=name,
        grid=(R // tr,),
        in_specs=[pl.BlockSpec((1, tr, C), lambda i: (0, i, 0))],
        out_specs=pl.BlockSpec((tr, C), lambda i: (i, 0)),
        out_shape=jax.ShapeDtypeStruct((R, C), CDT),
        compiler_params=_cparams(1),
    )(w)


def _adam_small(g, w, m, v):
    def body(g_ref, w_ref, m_ref, v_ref, d_ref, nm_ref, nv_ref):
        delta, nm, nv = _adamw_math(g_ref[...], w_ref[...], m_ref[...], v_ref[...])
        d_ref[...] = delta
        nm_ref[...] = nm
        nv_ref[...] = nv

    return pl.pallas_call(body, name="adam_small", out_shape=[jax.ShapeDtypeStruct(g.shape, F32)] * 3)(g, w, m, v)


def _rot(w, axis=-1):
    x1, x2 = jnp.split(w, 2, axis=axis)
    return jnp.concatenate([-x2, x1], axis=axis)


def _unrot(dw, axis=-1):
    d1, d2 = jnp.split(dw, 2, axis=axis)
    return jnp.concatenate([d2, -d1], axis=axis)


def _pad_cols(w, width):
    return jnp.pad(w, [(0, 0)] * (w.ndim - 1) + [(0, width - w.shape[-1])])


def kernel(x, mem, positions, ln_emb_g, ln_emb_b, hgrn_lb_logits, w_in, hgrn_norm_g, mla_g_cq, mla_g_ckv, mla_w_uq, mla_w_ukv, mem_w_kv, w_branch, w_o, ln1_g, ln1_b, w_ffn_gate, w_ffn_up, w_ffn_down, ln2_g, ln2_b, loss_target, m_ln_emb_g, m_ln_emb_b, m_hgrn_lb_logits, m_w_in, m_hgrn_norm_g, m_mla_g_cq, m_mla_g_ckv, m_mla_w_uq, m_mla_w_ukv, m_mem_w_kv, m_w_branch, m_w_o, m_ln1_g, m_ln1_b, m_w_ffn_gate, m_w_ffn_up, m_w_ffn_down, m_ln2_g, m_ln2_b, v_ln_emb_g, v_ln_emb_b, v_hgrn_lb_logits, v_w_in, v_hgrn_norm_g, v_mla_g_cq, v_mla_g_ckv, v_mla_w_uq, v_mla_w_ukv, v_mem_w_kv, v_w_branch, v_w_o, v_ln1_g, v_ln1_b, v_w_ffn_gate, v_w_ffn_up, v_w_ffn_down, v_ln2_g, v_ln2_b):
    x2, tgt = x[0], loss_target[0]
    S, D = x2.shape
    Mm = mem.shape[1]
    F = w_ffn_gate.shape[2] * N_DEV
    GW = 3 * D
    PW = OFF_GATE + GW + KR_PAD
    KR = OFF_GATE + GW
    NIN = w_in.shape[2] * N_DEV
    assert NIN == OFF_GATE + MLA_ROPE + GW
    _, _, _, me = _my_place()
    row = lambda a: a.reshape(1, -1)

    br3 = lambda a: a.reshape(1, 3 * BR_W, -1)
    tp = lambda a: jnp.swapaxes(a, 1, 2)
    big_w = [tp(w_in), mla_w_uq, mla_w_ukv, mem_w_kv, br3(w_branch), w_o, tp(w_ffn_gate), tp(w_ffn_up), w_ffn_down]
    big_m = [tp(m_w_in), m_mla_w_uq, m_mla_w_ukv, m_mem_w_kv, br3(m_w_branch), m_w_o, tp(m_w_ffn_gate), tp(m_w_ffn_up),
             m_w_ffn_down]
    big_v = [tp(v_w_in), v_mla_w_uq, v_mla_w_ukv, v_mem_w_kv, br3(v_w_branch), v_w_o, tp(v_w_ffn_gate), tp(v_w_ffn_up),
             v_w_ffn_down]
    transposed = (0, 6, 7)
    wnames = ["w_in", "w_uq", "w_ukv", "mem_w_kv", "w_branch", "w_o", "w_gate", "w_up", "w_down"]
    big_wb = [_to_bf16("bf16_" + nme, w) for nme, w in zip(wnames, big_w)]
    g_in, g_lb = _all_gather_two_level("weights_all_gather", [big_wb[0], hgrn_lb_logits.reshape(4, -1)])
    win_t = g_in.reshape(NIN, D)
    kr_w = win_t[OFF_QM:OFF_QM + MLA_ROPE]
    zeros64 = jnp.zeros_like(kr_w)
    win_pt = jnp.concatenate([win_t[:OFF_FB], win_t[OFF_FF:OFF_G], win_t[OFF_FB:OFF_FF], win_t[OFF_G:OFF_QM],
                              win_t[OFF_QM + MLA_ROPE:], kr_w, zeros64, _rot(kr_w, 0), zeros64,
                              jnp.zeros((KR_PAD - 2 * LANE, D), CDT)], axis=0)
    lbl4 = jnp.transpose(g_lb, (1, 0, 2)).reshape(4, -1)

    half = MLA_ROPE // 2
    inv_freq = jnp.power(ROPE_THETA, -jnp.arange(half, dtype=F32) / half)
    ang = positions[0].astype(F32)[:, None] * inv_freq
    cf = _pad_cols(jnp.tile(jnp.cos(ang), (1, 2)), LANE)
    sf = _pad_cols(jnp.tile(jnp.sin(ang), (1, 2)), LANE)

    tm512 = _rtile(S, 512)
    ident = lambda accs, tiles, rows: ([accs[0]], [])

    def epi_ln0(accs, tiles, rows):
        h = _ln_stats(tiles[0])[0] * rows[0] + rows[1]
        return [h, h], []

    h0, h0b = _fused_mm("ln_emb_fwd", "nn", [], S, D, 1, tm512, D, 1, [(D, F32, 0, None), (D, CDT, 0, None)], epi_ln0,
                        tiles=[(x2, 0)], rows=[(row(ln_emb_g), 0), (row(ln_emb_b), 0)])
    proj, g_uq, g_ukv, g_mkv, g_wb, g_wo = _fused_mm(
        "proj", "nt", [[(h0b, 0, win_pt, 0)]], S, PW, D, _rtile(S, 1024), _tile(PW, 512), D, [(PW, F32, 0, None)], ident,
        xchg=_Xchg(big_wb[1:6], False), msplit=2 if S % 2048 == 0 else 1)
    wuq_p =jnp.concatenate([g_uq[..., :MLA_NOPE], _pad_cols(g_uq[..., MLA_NOPE:], LANE),
                             _pad_cols(_rot(g_uq[..., MLA_NOPE:]), LANE)], axis=-1)
    wukv = g_ukv
    wmkv = g_mkv.reshape(-1, g_mkv.shape[-1])
    wb = jnp.transpose(g_wb.reshape(N_DEV, 3, BR_W, -1), (1, 2, 0, 3)).reshape(3, BR_W, D)
    wo = g_wo.reshape(-1, D)
    o_f, st_f = _gla_fwd(proj, lbl4, OFF_FF, False, "gla_fwd_f")
    o_b, st_b = _gla_fwd(proj, lbl4, OFF_FB, True, "gla_fwd_b")
    y_hg = _hgrn_post_fwd(o_f, o_b, proj, hgrn_norm_g)
    q_cat, k_cat, v_mla, vt_mla, cqn, ckvn = _mla_up(proj, cf, sf, mla_g_cq, mla_g_ckv, wuq_p, wukv)
    y_mla, ot, lse, g_wg, g_wu, g_wd = _mla_attn_fwd(q_cat, k_cat, vt_mla, _Xchg(big_wb[6:9], False))
    wg_t, wu_t = g_wg.reshape(F, D), g_wu.reshape(F, D)
    wd = g_wd.reshape(-1, D)
    memb = mem[0].astype(CDT)
    (memkv,) = _fused_mm("mem_kv", "nn", [[(memb, 0, wmkv, 0)]], Mm, 2 * MEM_W, D, Mm, _tile(2 * MEM_W, 512), D,
                         [(2 * MEM_W, CDT, 0, None)], ident)
    y_mem = _mem_attn_fwd(proj, memkv)
    ys = [y_hg, y_mla, y_mem]
    tnD = _tile(D, 512, OFF_GATE)

    def epi_branch(accs, tiles, rows):
        return [_sigmoid(tiles[0]) * accs[0] + _sigmoid(tiles[1]) * accs[1] + _sigmoid(tiles[2]) * accs[2]], []

    (merged,) = _fused_mm("branch_fwd", "nn", [[(ys[b], 0, wb[b], 0)] for b in range(3)], S, D, BR_W, tm512, tnD, BR_W,
                          [(D, CDT, 0, None)], epi_branch, tiles=[(proj, OFF_GATE + b * D) for b in range(3)])

    def epi_ln1(accs, tiles, rows):
        r1v = ALPHA * tiles[0] + accs[0]
        return [r1v, _ln_stats(r1v)[0] * rows[0] + rows[1]], []

    r1, h1b = _fused_mm("wo_ln1", "nn", [[(merged, 0, wo, 0)]], S, D, D, tm512, D, D,
                        [(D, F32, 0, None), (D, CDT, 0, None)], epi_ln1, tiles=[(h0, 0)], rows=[(ln1_g, 0), (ln1_b, 0)])
    tnF = _tile(F, 512)

    def epi_up(accs, tiles, rows):
        gp, up = accs
        return [gp, up, gp * _sigmoid(gp) * up], []

    tm1k, ms1k = _rtile(S, 1024), (2 if S % 2048 == 0 else 1)
    gpb, upb, act = _fused_mm("ffn_up", "nt", [[(h1b, 0, wg_t, 0)], [(h1b, 0, wu_t, 0)]], S, F, D, tm1k, tnF, D,
                              [(F, CDT, 0, None)] * 3, epi_up, msplit=ms1k)

    def epi_down(accs, tiles, rows):
        g1, b1, g2, b2 = rows
        h1 = _ln_stats(tiles[0])[0] * g1 + b1
        xh2, rstd2 = _ln_stats(ALPHA * h1 + accs[0])
        diff = xh2 * g2 + b2 - tiles[1]
        dh2 = diff * (1.0 / D)
        dr2v = _ln_bwd(dh2, xh2, rstd2, g2)
        return [dr2v, dr2v], [dh2 * xh2, dh2, diff * diff * (0.5 / D)]

    acc_first = lambda epi: (lambda accs, tiles, rows: epi([tiles[0]], tiles[1:], rows))
    tm256 = _rtile(S, 256)
    (ff,) = _fused_mm("ffn_down", "nn", [[(act, 0, wd, 0)]], S, D, F, tm1k, _tile(D, 512), F, [(D, F32, 0, None)], ident,
                      msplit=ms1k)
    dr2, dr2b, dg2, db2, lossp = _fused_mm(
        "ffn_ln2_loss", "nn", [], S, D, 1, tm256, D, 1, [(D, F32, 0, None), (D, CDT, 0, None)], acc_first(epi_down),
        tiles=[(ff, 0), (r1, 0), (tgt, 0)], rows=[(ln1_g, 0), (ln1_b, 0), (ln2_g, 0), (ln2_b, 0)], n_racc=3)

    def epi_dact(accs, tiles, rows):
        da, gp, up = accs[0], tiles[0].astype(F32), tiles[1].astype(F32)
        s = _sigmoid(gp)
        return [da * up * (s * (1.0 + gp * (1.0 - s))), da * (gp * s)], []

    dgp, dup = _fused_mm("ffn_dact", "nt", [[(dr2b, 0, wd, 0)]], S, F, D, tm1k, tnF, D, [(F, CDT, 0, None)] * 2,
                         epi_dact, tiles=[(gpb, 0), (upb, 0)], msplit=ms1k)
    tkS = _rtile(S, 2048)
    (d_wd,) = _fused_mm("dw_down", "tn", [[(act, 0, dr2b, 0)]], F, D, S, tnF, D, tkS, [(D, CDT, 0, None)], ident)
    d_wg_t, d_wu_t = _fused_mm("dw_gate_up", "tn", [[(dgp, 0, h1b, 0)], [(dup, 0, h1b, 0)]], F, D, S, tnF, _tile(D, 1024),
                               tkS, [(D, CDT, 0, None)] * 2, lambda accs, tiles, rows: (accs, []))

    def epi_dh1(accs, tiles, rows):
        dh1 = accs[0] + ALPHA * tiles[0]
        xh1, rstd1 = _ln_stats(tiles[1])
        dr1v = _ln_bwd(dh1, xh1, rstd1, rows[0])
        return [dr1v, dr1v], [dh1 * xh1, dh1]

    rows8 = lambda dw: dw.reshape(N_DEV, -1, dw.shape[-1])
    (dh1_acc,) = _fused_mm("dh1", "nn", [[(dgp, 0, wg_t, 0), (dup, 0, wu_t, 0)]], S, D, F, tm512, _tile(D, 512), F,
                           [(D, F32, 0, None)], ident)
    dr1, dr1b, dg1, db1 = _fused_mm(
        "dh1_ln1", "nn", [], S, D, 1, tm256, D, 1, [(D, F32, 0, None), (D, CDT, 0, None)], acc_first(epi_dh1),
        tiles=[(dh1_acc, 0), (dr2, 0), (r1, 0)], rows=[(ln1_g, 0)], n_racc=2)
    (dmerged,) = _fused_mm("dmerged", "nt", [[(dr1b, 0, wo, 0)]], S, D, D, tm512, _tile(D, 512), D, [(D, CDT, 0, None)], ident)
    (d_wo,) = _fused_mm("dw_o", "tn", [[(merged, 0, dr1b, 0)]], D, D, S, _tile(D, 512), D, tkS, [(D, CDT, 0, None)], ident)

    def epi_dbranch(accs, tiles, rows):
        dm, s = tiles[0].astype(F32), _sigmoid(tiles[1])
        return [dm * s, dm * accs[0] * s * (1.0 - s)], []

    dproj = None
    d_wbs, dys = [], []
    for b in range(3):
        du, dproj = _fused_mm(f"branch_bwd{b}", "nn", [[(ys[b], 0, wb[b], 0)]], S, D, BR_W, tm512, tnD, BR_W,
                              [(D, CDT, 0, None), (PW, CDT, OFF_GATE + b * D, dproj)], epi_dbranch,
                              tiles=[(dmerged, 0), (proj, OFF_GATE + b * D)])
        (dwb,) = _fused_mm(f"dw_branch{b}", "tn", [[(ys[b], 0, du, 0)]], BR_W, D, S, _tile(BR_W, 512), D, tkS,
                           [(D, CDT, 0, None)], ident)
        (dyb,) = _fused_mm(f"dy_branch{b}", "nt", [[(du, 0, wb[b], 0)]], S, BR_W, D, tm512, _tile(BR_W, 512), D,
                           [(BR_W, F32 if b == 0 else CDT, 0, None)], ident)
        d_wbs.append(dwb)
        dys.append(dyb)
    dy_hg, dy_mla, dy_mem = dys

    dproj, dk_mem, dv_mem = _mem_attn_bwd(dy_mem, proj, memkv, dproj)
    dkv_mem = jnp.concatenate([dk_mem, dv_mem], axis=1).astype(CDT)
    (d_wmkv,) = _fused_mm("dw_memkv", "tn", [[(memb, 0, dkv_mem, 0)]], D, 2 * MEM_W, Mm, _tile(D, 512), 2 * MEM_W, Mm,
                          [(2 * MEM_W, CDT, 0, None)], ident)

    delta = _mla_delta(dy_mla, ot)
    dk_cat, dv_h, dq_cat, r_wg, r_wu, r_wd = _mla_attn_bwd(
        q_cat, k_cat, v_mla, dy_mla, lse, delta, _Xchg([rows8(d_wg_t), rows8(d_wu_t), rows8(d_wd)], True))
    dproj, dqp, dkvp, dgq, dgkv = _mla_up_bwd(dq_cat, dk_cat, dv_h, proj, cf, sf, mla_g_cq, mla_g_ckv, wuq_p, wukv, dproj)
    d_wuq_p = _heads_tn("dw_uq", cqn, dqp).astype(F32)
    d_wukv = _heads_tn("dw_ukv", ckvn, dkvp)
    d_wuq = jnp.concatenate([d_wuq_p[..., :MLA_NOPE],
                             d_wuq_p[..., LANE:LANE + MLA_ROPE] + _unrot(d_wuq_p[..., 2 * LANE:2 * LANE + MLA_ROPE])],
                            axis=-1).astype(CDT)

    do_hg, dproj, dng = _hgrn_post_bwd(dy_hg, o_f, o_b, proj, hgrn_norm_g, dproj)
    dproj, dq1, di1, dl_f = _gla_bwd(proj, lbl4, OFF_FF, False, do_hg, st_f, dproj, None, "gla_bwd_f")
    dproj, _, _, dl_b = _gla_bwd(proj, lbl4, OFF_FB, True, do_hg, st_b, dproj, (dq1, di1), "gla_bwd_b")

    def epi_dh0(accs, tiles, rows):
        dh0 = accs[0] + ALPHA * tiles[0]
        xh, rstd = _ln_stats(tiles[1])
        return [_ln_bwd(dh0, xh, rstd, rows[0])], [dh0 * xh, dh0]

    d_wb = jnp.transpose(jnp.stack(d_wbs).reshape(3, BR_W, N_DEV, -1), (2, 0, 1, 3)).reshape(N_DEV, 3 * BR_W, -1)
    d_win_pt, r_uq, r_ukv, r_mkv, r_wb, r_wo = _fused_mm(
        "dw_in", "tn", [[(dproj, 0, h0b, 0)]], PW, D, S, _tile(PW, 1536), _tile(D, 1024), tkS, [(D, CDT, 0, None)], ident,
        xchg=_Xchg([d_wuq, d_wukv, rows8(d_wmkv), d_wb, rows8(d_wo)], True))
    d_kr = (d_win_pt[KR:KR + MLA_ROPE].astype(F32) + _unrot(d_win_pt[KR + LANE:KR + LANE + MLA_ROPE].astype(F32), 0)).astype(CDT)
    d_win_t = jnp.concatenate([d_win_pt[:OFF_FB], d_win_pt[OFF_FF:OFF_G], d_win_pt[OFF_FB:OFF_FF],
                               d_win_pt[OFF_G:OFF_QM], d_kr, d_win_pt[OFF_QM:KR]], axis=0)
    grad_x, dge, dbe, r_in = _fused_mm(
        "dh0_ln_emb", "nn", [[(dproj, 0, win_pt, 0)]], S, D, PW, tm512, D, _tile(PW, 1536), [(D, F32, 0, None)], epi_dh0,
        tiles=[(dr1, 0), (x2, 0)], rows=[(row(ln_emb_g), 0)], n_racc=2, xchg=_Xchg([rows8(d_win_t)], True))

    recv = [r_in, r_uq, r_ukv, r_mkv, r_wb, r_wo, r_wg, r_wu, r_wd]
    names = ["w_in", "w_uq", "w_ukv", "mem_w_kv", "w_branch", "w_o", "w_gate", "w_up", "w_down"]
    big_out = [_adam_big("adam_" + nme, r, w, m_, v_) for nme, r, w, m_, v_ in zip(names, recv, big_w, big_m, big_v)]

    parts = [dge, dbe, dng, dgq, dgkv, dg1, db1, dg2, db2, dl_f, dl_b, lossp]
    widths = [p.shape[1] for p in parts]
    red = _small_allreduce(jnp.concatenate(parts, axis=1))[0]
    offs = [sum(widths[:i]) for i in range(len(widths))]
    rs = [red[o:o + w_] for o, w_ in zip(offs, widths)]
    g_le_g, g_le_b, g_ng, g_gq, g_gkv, g_l1g, g_l1b, g_l2g, g_l2b, g_dlf, g_dlb, g_loss = rs
    loss = jnp.sum(g_loss)
    g_ng = g_ng.reshape(HG_HEADS, HG_DK).sum(axis=0)
    dl0 = jnp.stack([g_dlf, g_dlb])
    g_lb_full = jnp.stack([dl0, -dl0], axis=1)
    lbw = hgrn_lb_logits.shape[2]
    g_lb = lax.dynamic_slice_in_dim(g_lb_full, me * lbw, lbw, axis=2)

    small_g = [g_le_g, g_le_b, g_lb, g_ng.reshape(1, -1), g_gq.reshape(1, -1), g_gkv.reshape(1, -1), g_l1g.reshape(1, -1),
               g_l1b.reshape(1, -1), g_l2g.reshape(1, -1), g_l2b.reshape(1, -1)]
    small_w = [ln_emb_g, ln_emb_b, hgrn_lb_logits, hgrn_norm_g, mla_g_cq, mla_g_ckv, ln1_g, ln1_b, ln2_g, ln2_b]
    small_m = [m_ln_emb_g, m_ln_emb_b, m_hgrn_lb_logits, m_hgrn_norm_g, m_mla_g_cq, m_mla_g_ckv, m_ln1_g, m_ln1_b, m_ln2_g, m_ln2_b]
    small_v = [v_ln_emb_g, v_ln_emb_b, v_hgrn_lb_logits, v_hgrn_norm_g, v_mla_g_cq, v_mla_g_ckv, v_ln1_g, v_ln1_b, v_ln2_g, v_ln2_b]
    small_g = [g.reshape(w.shape) for g, w in zip(small_g, small_w)]
    pack = lambda lst: jnp.concatenate([a.reshape(-1) for a in lst]).reshape(1, -1)
    s_delta, s_nm, s_nv = _adam_small(pack(small_g), pack(small_w), pack(small_m), pack(small_v))
    sizes = [w.size for w in small_w]
    soffs = [sum(sizes[:i]) for i in range(len(sizes))]
    unpack = lambda p: [p[0, o:o + n].reshape(w.shape) for o, n, w in zip(soffs, sizes, small_w)]
    s_delta, s_nm, s_nv = unpack(s_delta), unpack(s_nm), unpack(s_nv)

    def ordered(small, big):
        sm = list(small)
        big = [tp(b) if n in transposed else b for n, b in enumerate(big)]
        bg = [b.reshape(w.shape) for b, w in zip(big, [w_in, mla_w_uq, mla_w_ukv, mem_w_kv, w_branch, w_o, w_ffn_gate, w_ffn_up, w_ffn_down])]
        return [sm[0], sm[1], sm[2], bg[0], sm[3], sm[4], sm[5], bg[1], bg[2], bg[3], bg[4], bg[5], sm[6], sm[7], bg[6], bg[7], bg[8], sm[8], sm[9]]

    grads = ordered(small_g, [o[0] for o in big_out])
    deltas = ordered(s_delta, [o[1] for o in big_out])
    new_m = ordered(s_nm, [o[2] for o in big_out])
    new_v = ordered(s_nv, [o[3] for o in big_out])
    return (loss, grad_x[None], *grads, *deltas, *new_m, *new_v)
```

```python
import functools

import jax
import jax.numpy as jnp
from jax import lax
from jax.experimental import pallas as pl
from jax.experimental.pallas import tpu as pltpu

F32 = jnp.float32
CDT = jnp.bfloat16
MESH = pl.DeviceIdType.MESH
N_DEV = 8
V7X_VMEM_LIMIT = 60 * 1024 * 1024
LANE = 128
SUB = 8

HG_HEADS, HG_DK, HG_CHUNK = 8, 128, 64
HG_HPS = 8
HG_W = HG_HEADS * HG_DK
MLA_HEADS, MLA_RANK, MLA_NOPE, MLA_ROPE, MLA_V = 8, 512, 128, 64, 128
MLA_QK = MLA_NOPE + MLA_ROPE
MLA_SCALE = MLA_QK ** -0.5
MLA_QSCALE = MLA_SCALE * 1.4426950408889634
VT_ROWS = LANE + 16
MLA_TQ = 1024
MLA_FWD_SLABS = 4
MLA_W = MLA_HEADS * MLA_V
MEM_HEADS, MEM_HD = 4, 256
MEM_W = MEM_HEADS * MEM_HD
BR_W = 1024
ROPE_THETA = 10000.0
ALPHA = 2.0 ** 0.25
LN_EPS = 1e-5
RMS_EPS = 1e-6
ADAM_LR, ADAM_B1, ADAM_B2, ADAM_EPS, ADAM_WD, ADAM_STEP = 0.001, 0.9, 0.999, 1e-08, 0.01, 10
ADAM_BLOCK_ELEMS = 256 * 1024

OFF_Q, OFF_I, OFF_FB, OFF_FF, OFF_G = 0, 1024, 2048, 3072, 4096
OFF_CQ, OFF_CKV, OFF_QM, OFF_GATE = 5120, 5632, 6144, 7168
KR_PAD = 512


def _cparams(n_grid, side_effects=False):
    return pltpu.CompilerParams(dimension_semantics=("arbitrary",) * n_grid, vmem_limit_bytes=V7X_VMEM_LIMIT,
                                has_side_effects=side_effects)


def _tile(n, pref, *offsets):
    if n <= pref and all(o % n == 0 for o in offsets):
        return n
    t = (min(pref, n) // LANE) * LANE
    while t >= LANE:
        if n % t == 0 and all(o % t == 0 for o in offsets):
            return t
        t -= LANE
    raise ValueError(f"no tile for {n} {pref} {offsets}")


def _rtile(n, pref):
    if n <= pref:
        return n
    t = (pref // SUB) * SUB
    while t >= SUB:
        if n % t == 0:
            return t
        t -= SUB
    raise ValueError(f"no row tile for {n} {pref}")


def _dot(a, b, dims):
    return lax.dot_general(a.astype(CDT), b.astype(CDT), (dims, ((), ())), preferred_element_type=F32)


def _nn(a, b):
    return _dot(a, b, ((1,), (0,)))


def _nt(a, b):
    return _dot(a, b, ((1,), (1,)))


def _tn(a, b):
    return _dot(a, b, ((0,), (0,)))


_DOTS = {"nn": _nn, "nt": _nt, "tn": _tn}


def _sigmoid(x):
    return 1.0 / (1.0 + jnp.exp(-x))


def _rowsum8(v):
    r, w = v.shape
    return v.reshape(r // SUB, SUB, w).sum(axis=0)


def _my_place():
    x, y, c = lax.axis_index("x"), lax.axis_index("y"), lax.axis_index("c")
    return x, y, c, 4 * x + 2 * y + c


def _peer(x, y, c, kk):
    px = 1 - x if kk & 4 else x
    py = 1 - y if kk & 2 else y
    pc = 1 - c if kk & 1 else c
    return (px, py, pc), 4 * px + 2 * py + pc


class _Xchg:
    def __init__(self, arrs, scatter):
        self.arrs, self.scatter, self.n = list(arrs), scatter, len(arrs)
        hbm = pl.BlockSpec(memory_space=pl.ANY)
        self.specs = [hbm] * self.n
        self.out_shape = [jax.ShapeDtypeStruct(((N_DEV,) + a.shape[1:]) if scatter else ((N_DEV,) + a.shape), a.dtype)
                          for a in self.arrs]
        ncp = self.n * (N_DEV - 1)
        self.scratch = [pltpu.SemaphoreType.DMA((ncp,)), pltpu.SemaphoreType.DMA((ncp,)), pltpu.SemaphoreType.DMA((self.n,))]

    def _copies(self, ins, outs, send, recv, loc):
        x, y, c, me = _my_place()
        copies = []
        for w in range(self.n):
            copies.append(pltpu.make_async_copy(ins[w].at[me] if self.scatter else ins[w], outs[w].at[me], loc.at[w]))
            for kk in range(1, N_DEV):
                peer, pid = _peer(x, y, c, kk)
                s = w * (N_DEV - 1) + kk - 1
                copies.append(pltpu.make_async_remote_copy(
                    src_ref=ins[w].at[pid] if self.scatter else ins[w], dst_ref=outs[w].at[me],
                    send_sem=send.at[s], recv_sem=recv.at[s], device_id=peer, device_id_type=MESH))
        return copies

    def start(self, ins, outs, sems):
        for cp in self._copies(ins, outs, *sems):
            cp.start()

    def wait(self, ins, outs, sems):
        for cp in self._copies(ins, outs, *sems):
            cp.wait()


def _all_gather_two_level(name, arrs):
    n = len(arrs)
    NC = N_DEV - 1

    def body(*refs):
        ins, outs = refs[:n], refs[n:2 * n]
        send, recv, loc = refs[2 * n:]
        x, y, c, me = _my_place()
        sibling = (x, y, 1 - c)
        chips = [(1 - x, y), (x, 1 - y), (1 - x, 1 - y)]
        slot = lambda px, py, pc: 4 * px + 2 * py + pc

        def copy(w, k, block, to, src=None):
            dst = outs[w].at[slot(*block)]
            return pltpu.make_async_remote_copy(src_ref=dst if src is None else src, dst_ref=dst,
                                                send_sem=send.at[w * NC + k], recv_sem=recv.at[w * NC + k],
                                                device_id=to, device_id_type=MESH)

        mine = [pltpu.make_async_copy(ins[w], outs[w].at[me], loc.at[w]) for w in range(n)]
        for cp in mine:
            cp.start()
        first = []
        for w in range(n):
            first.append(copy(w, 0, (x, y, c), sibling, src=ins[w]))
            first += [copy(w, 1 + j, (x, y, c), (*chip, c), src=ins[w]) for j, chip in enumerate(chips)]
        for cp in first:
            cp.start()
        passed = []
        for j, chip in enumerate(chips):
            for w in range(n):
                copy(w, 1 + j, (*chip, c), (x, y, c)).wait_recv()
                fwd = copy(w, 4 + j, (*chip, c), sibling)
                fwd.start()
                passed.append(fwd)
        for w in range(n):
            copy(w, 0, sibling, (x, y, c)).wait_recv()
            for j, chip in enumerate(chips):
                copy(w, 4 + j, (*chip, 1 - c), (x, y, c)).wait_recv()
        for cp in first + passed:
            cp.wait_send()
        for cp in mine:
            cp.wait()

    hbm = pl.BlockSpec(memory_space=pl.ANY)
    return pl.pallas_call(
        body,
        name=name,
        in_specs=[hbm] * n,
        out_specs=[hbm] * n,
        out_shape=[jax.ShapeDtypeStruct((N_DEV,) + a.shape, a.dtype) for a in arrs],
        scratch_shapes=[pltpu.SemaphoreType.DMA((n * NC,)), pltpu.SemaphoreType.DMA((n * NC,)), pltpu.SemaphoreType.DMA((n,))],
        compiler_params=pltpu.CompilerParams(has_side_effects=True),
    )(*arrs)


def _fused_mm(name, mode, groups, M, N, K, tm, tn, tk, outs, epi, tiles=(), rows=(), n_racc=0, xchg=None, msplit=1):
    ni, nj, nk = M // tm, N // tn, K // tk
    assert M % tm == 0 and N % tn == 0 and K % tk == 0, (name, M, N, K, tm, tn, tk)
    assert n_racc == 0 or nj == 1
    assert msplit == 1 or (nk == 1 and n_racc == 0 and tm % (16 * msplit) == 0)
    dot = _DOTS[mode] if groups else None
    ins, in_specs = [], []
    for g in groups:
        for a, a_off, b, b_off in g:
            if mode == "tn":
                assert a_off % tm == 0
                in_specs.append(pl.BlockSpec((tk, tm), lambda i, j, k, o=a_off // tm: (k, i + o)))
            else:
                assert a_off % tk == 0
                in_specs.append(pl.BlockSpec((tm, tk), lambda i, j, k, o=a_off // tk: (i, k + o)))
            ins.append(a)
            if mode == "nt":
                assert b_off % tk == 0
                in_specs.append(pl.BlockSpec((tn, tk), lambda i, j, k, o=b_off // tk: (j, k + o)))
            else:
                assert b_off % tn == 0
                in_specs.append(pl.BlockSpec((tk, tn), lambda i, j, k, o=b_off // tn: (k, j + o)))
            ins.append(b)
    for arr, off in tiles:
        assert off % tn == 0
        ins.append(arr)
        in_specs.append(pl.BlockSpec((tm, tn), lambda i, j, k, o=off // tn: (i, j + o)))
    for arr, off in rows:
        assert off % tn == 0
        ins.append(arr)
        in_specs.append(pl.BlockSpec((1, tn), lambda i, j, k, o=off // tn: (0, j + o)))
    aliases = {}
    out_shape, out_specs = [], []
    for oi, (width, dtype, off, alias) in enumerate(outs):
        assert off % tn == 0
        if alias is not None:
            aliases[len(ins)] = oi
            ins.append(alias)
            in_specs.append(pl.BlockSpec(memory_space=pl.ANY))
        out_shape.append(jax.ShapeDtypeStruct((M, width), dtype))
        out_specs.append(pl.BlockSpec((tm, tn), lambda i, j, k, o=off // tn: (i, j + o)))
    for _ in range(n_racc):
        out_shape.append(jax.ShapeDtypeStruct((SUB, N), F32))
        out_specs.append(pl.BlockSpec((SUB, tn), lambda i, j, k: (0, 0)))
    n_alias = len(aliases)
    n_pairs = [len(g) for g in groups]
    use_scratch = nk > 1
    scratch = [pltpu.VMEM((tm, tn), F32) for _ in groups] if use_scratch else []
    nx = 0
    if xchg is not None:
        nx = xchg.n
        ins += xchg.arrs
        in_specs += xchg.specs
        out_shape += xchg.out_shape
        out_specs += xchg.specs
        scratch += xchg.scratch

    def body(*refs):
        it = iter(refs)
        pair_refs = [[(next(it), next(it)) for _ in range(n)] for n in n_pairs]
        tile_refs = [next(it) for _ in tiles]
        row_refs = [next(it) for _ in rows]
        for _ in range(n_alias):
            next(it)
        x_in = [next(it) for _ in range(nx)]
        out_refs = [next(it) for _ in outs]
        racc_refs = [next(it) for _ in range(n_racc)]
        x_out = [next(it) for _ in range(nx)]
        acc_refs = [next(it) for _ in groups] if use_scratch else []
        x_sems = list(it)
        i, j, k = pl.program_id(0), pl.program_id(1), pl.program_id(2)
        if nx:
            @pl.when((i == 0) & (j == 0) & (k == 0))
            def _():
                xchg.start(x_in, x_out, x_sems)

        def products():
            res = []
            for prs in pair_refs:
                s = None
                for a_ref, b_ref in prs:
                    d = dot(a_ref[...], b_ref[...])
                    s = d if s is None else s + d
                res.append(s)
            return res

        def finish(accs):
            out_v, racc_v = epi(accs, [t[...] for t in tile_refs], [r[...] for r in row_refs])
            for o_ref, v in zip(out_refs, out_v):
                o_ref[...] = v.astype(o_ref.dtype)
            for r_ref, v in zip(racc_refs, racc_v):
                part = _rowsum8(v)

                @pl.when(i == 0)
                def _():
                    r_ref[...] = part

                @pl.when(i > 0)
                def _():
                    r_ref[...] += part

        if not use_scratch and msplit > 1:
            ts = tm // msplit
            for s in range(msplit):
                rs = pl.ds(s * ts, ts)
                accs = []
                for prs in pair_refs:
                    acc = None
                    for a_ref, b_ref in prs:
                        dd = dot(a_ref[:, rs] if mode == "tn" else a_ref[rs, :], b_ref[...])
                        acc = dd if acc is None else acc + dd
                    accs.append(acc)
                out_v, _ = epi(accs, [t[rs, :] for t in tile_refs], [r[...] for r in row_refs])
                for o_ref, v in zip(out_refs, out_v):
                    o_ref[rs, :] = v.astype(o_ref.dtype)
        elif not use_scratch:
            finish(products())
        else:
            @pl.when(k == 0)
            def _():
                for acc in acc_refs:
                    acc[...] = jnp.zeros_like(acc)

            for acc, p in zip(acc_refs, products()):
                acc[...] += p

            @pl.when(k == nk - 1)
            def _():
                finish([acc[...] for acc in acc_refs])

        if nx:
            @pl.when((i == ni - 1) & (j == nj - 1) & (k == nk - 1))
            def _():
                xchg.wait(x_in, x_out, x_sems)

    res = pl.pallas_call(
        body,
        name=name,
        grid=(ni, nj, nk),
        in_specs=in_specs,
        out_specs=out_specs,
        out_shape=out_shape,
        scratch_shapes=scratch,
        input_output_aliases=aliases,
        compiler_params=_cparams(3, side_effects=nx > 0),
    )(*ins)
    return res


def _ln_stats(r):
    mu = jnp.mean(r, axis=-1, keepdims=True)
    xc = r - mu
    var = jnp.mean(xc * xc, axis=-1, keepdims=True)
    rstd = lax.rsqrt(var + LN_EPS)
    return xc * rstd, rstd


def _ln_bwd(dh, xhat, rstd, g):
    dxh = dh * g
    m1 = jnp.mean(dxh, axis=-1, keepdims=True)
    m2 = jnp.mean(dxh * xhat, axis=-1, keepdims=True)
    return rstd * (dxh - m1 - xhat * m2)


def _split3(x):
    hi = x.astype(CDT)
    r1 = x - hi.astype(F32)
    mid = r1.astype(CDT)
    lo = (r1 - mid.astype(F32)).astype(CDT)
    return hi, mid, lo


def _tri_matmul(tri, x):
    hi, mid, lo = _split3(x)
    return _nn(tri, hi) + _nn(tri, mid) + _nn(tri, lo)


def _dot3(dot, a, b):
    a_hi, b_hi = a.astype(CDT), b.astype(CDT)
    a_lo = (a - a_hi.astype(F32)).astype(CDT)
    b_lo = (b - b_hi.astype(F32)).astype(CDT)
    return dot(a_hi, b_hi) + dot(a_hi, b_lo) + dot(a_lo, b_hi)


def _gla_masks(reverse):
    C = HG_CHUNK
    r = lax.broadcasted_iota(jnp.int32, (C, C), 0)
    c = lax.broadcasted_iota(jnp.int32, (C, C), 1)
    keep = (c >= r) if reverse else (r >= c)
    return keep


def _m(fn, *lists):
    return [fn(*args) for args in zip(*lists)]


def _gla_chunk_fwd(qraw, fraw, lb, keep, reverse):
    C = HG_CHUNK
    end = 0 if reverse else C - 1
    tri = jnp.where(keep, 1.0, 0.0).astype(CDT)
    sq = _m(_sigmoid, qraw)
    q = _m(lambda x, s: x * s, qraw, sq)
    sg = _m(_sigmoid, fraw)
    f = _m(lambda l_, s: l_ + (1.0 - l_) * s, lb, sg)
    k = _m(lambda x: 1.0 - x, f)
    g = _m(jnp.log, f)
    b = _m(lambda x: _tri_matmul(tri, x), g)
    b_end = _m(lambda x: x[end:end + 1, :], b)
    b_mid = _m(lambda x: x[C // 2:C // 2 + 1, :], b)
    eq = _m(lambda x, m_: jnp.exp(x - m_), b, b_mid)
    ek = _m(lambda x, m_: jnp.exp(m_ - x), b, b_mid)
    eb = _m(jnp.exp, b)
    e2 = _m(lambda x, e_: jnp.exp(e_ - x), b, b_end)
    e_end = _m(jnp.exp, b_end)
    qt = _m(lambda x, e_: x * e_, q, eq)
    kt = _m(lambda x, e_: x * e_, k, ek)
    qs = _m(lambda x, e_: (x * e_).astype(CDT), q, eb)
    k2 = _m(lambda x, e_: (x * e_).astype(CDT), k, e2)
    a = _m(lambda x, y: jnp.where(keep, _dot3(_nt, x, y), 0.0).astype(CDT), qt, kt)
    return dict(sq=sq, q=q, sg=sg, f=f, k=k, eq=eq, ek=ek, eb=eb, e2=e2, e_end=e_end, qt=qt, kt=kt, qs=qs, k2=k2, a=a)


def _gla_fwd(proj, lbl4, f_off, reverse, name):
    S = proj.shape[0]
    C = HG_CHUNK
    R = _rtile(S, 512)
    cpb, nblk = R // C, S // R
    d = 1 if reverse else 0
    blk_map = (lambda b: nblk - 1 - b) if reverse else (lambda b: b)

    W = HG_HPS * HG_DK

    def body(q_ref, i_ref, f_ref, lb_ref, o_ref, st_ref, s_scr):
        @pl.when(pl.program_id(1) == 0)
        def _():
            s_scr[...] = jnp.zeros_like(s_scr)

        l = lb_ref[...]
        lbs = _sigmoid(l[2 * d:2 * d + 1, :] - l[2 * d + 1:2 * d + 2, :])
        keep = _gla_masks(reverse)
        heads = list(range(HG_HPS))
        css = [pl.ds(hh * HG_DK, HG_DK) for hh in heads]
        lb = [lbs[:, hh * HG_DK:(hh + 1) * HG_DK] for hh in heads]
        for cc in range(cpb):
            c = cpb - 1 - cc if reverse else cc
            sl = pl.ds(c * C, C)
            v = [i_ref[sl, cs] for cs in css]
            t = _gla_chunk_fwd([q_ref[sl, cs] for cs in css], [f_ref[sl, cs] for cs in css], lb, keep, reverse)
            st = [s_scr[hh] for hh in heads]
            o = _m(lambda qs, s_, a, v_: _nt(qs, s_) + _nn(a, v_), t["qs"], st, t["a"], v)
            new = _m(lambda e_, s_, v_, k2: e_ * s_ + _tn(v_, k2), t["e_end"], st, v, t["k2"])
            for hh in heads:
                st_ref[c, hh] = st[hh]
                o_ref[sl, css[hh]] = o[hh]
                s_scr[hh] = new[hh]

    col = lambda off: (lambda h, b: (blk_map(b), off // W + h))
    return pl.pallas_call(
        body,
        name=name,
        grid=(HG_HEADS // HG_HPS, nblk),
        in_specs=[
            pl.BlockSpec((R, W), col(OFF_Q)),
            pl.BlockSpec((R, W), col(OFF_I)),
            pl.BlockSpec((R, W), col(f_off)),
            pl.BlockSpec((4, W), lambda h, b: (0, h)),
        ],
        out_specs=[
            pl.BlockSpec((R, W), lambda h, b: (blk_map(b), h)),
            pl.BlockSpec((cpb, HG_HPS, HG_DK, HG_DK), lambda h, b: (blk_map(b), h, 0, 0)),
        ],
        out_shape=[
            jax.ShapeDtypeStruct((S, HG_W), F32),
            jax.ShapeDtypeStruct((S // C, HG_HEADS, HG_DK, HG_DK), F32),
        ],
        scratch_shapes=[pltpu.VMEM((HG_HPS, HG_DK, HG_DK), F32)],
        compiler_params=_cparams(2),
    )(proj, proj, proj, lbl4)


def _gla_bwd(proj, lbl4, f_off, reverse, do, states, dproj, prev, name):
    S = proj.shape[0]
    PW = proj.shape[1]
    C = HG_CHUNK
    R = _rtile(S, 512)
    cpb, nblk = R // C, S // R
    d = 1 if reverse else 0
    blk_map = (lambda b: b) if reverse else (lambda b: nblk - 1 - b)
    final = prev is not None

    if final:
        assert HG_HPS == HG_HEADS and (OFF_Q, OFF_I, f_off) == (0, HG_W, 2 * HG_W)

    def body(*refs):
        if final:
            q_ref, i_ref, f_ref, lb_ref, do_ref, st_ref, pq_ref, pi_ref, _dp, o3_ref, dl_ref, ds_scr = refs
            dq_ref = di_ref = df_ref = o3_ref
        else:
            q_ref, i_ref, f_ref, lb_ref, do_ref, st_ref, dq_ref, di_ref, df_ref, dl_ref, ds_scr = refs
        out_off = (OFF_Q, OFF_I, f_off) if final else (0, 0, 0)
        blk = pl.program_id(1)

        @pl.when(blk == 0)
        def _():
            ds_scr[...] = jnp.zeros_like(ds_scr)
            dl_ref[...] = jnp.zeros_like(dl_ref)

        l = lb_ref[...]
        lbs = _sigmoid(l[2 * d:2 * d + 1, :] - l[2 * d + 1:2 * d + 2, :])
        keep = _gla_masks(reverse)
        keep_t = _gla_masks(not reverse)
        tri_t = jnp.where(keep_t, 1.0, 0.0).astype(CDT)
        end = 0 if reverse else C - 1
        is_end = lax.broadcasted_iota(jnp.int32, (C, HG_DK), 0) == end
        dl_all = [jnp.zeros((SUB, HG_DK), F32) for _ in range(HG_HPS)]
        for cc, heads in [(cc, [hh]) for cc in range(cpb) for hh in range(HG_HPS)]:
            css = [pl.ds(hh * HG_DK, HG_DK) for hh in heads]
            lb = [lbs[:, hh * HG_DK:(hh + 1) * HG_DK] for hh in heads]
            dl_acc = [dl_all[hh] for hh in heads]
            c = cc if reverse else cpb - 1 - cc
            sl = pl.ds(c * C, C)
            qraw = [q_ref[sl, cs] for cs in css]
            v = [i_ref[sl, cs] for cs in css]
            t = _gla_chunk_fwd(qraw, [f_ref[sl, cs] for cs in css], lb, keep, reverse)
            dob = [do_ref[sl, cs].astype(CDT) for cs in css]
            vb = _m(lambda x: x.astype(CDT), v)
            st = [st_ref[c, hh] for hh in heads]
            ds = [ds_scr[hh] for hh in heads]
            dsb = _m(lambda x: x.astype(CDT), ds)
            d_qs = _m(_nn, dob, st)
            d_a = _m(lambda x, y: jnp.where(keep, _nt(x, y), 0.0), dob, vb)
            d_qt = _m(lambda x, y: _dot3(_nn, x, y), d_a, t["kt"])
            d_kt = _m(lambda x, y: _dot3(_tn, x, y), d_a, t["qt"])
            d_v = _m(lambda a, x, k2, s_: _tn(a, x) + _nt(k2, s_), t["a"], dob, t["k2"], dsb)
            d_k2 = _m(_nn, vb, dsb)
            d_e = _m(lambda s_, x: jnp.sum(s_ * x, axis=0, keepdims=True), st, ds)
            new_ds = _m(lambda e_, x, y, qs: e_ * x + _tn(y, qs), t["e_end"], ds, dob, t["qs"])
            dq = _m(lambda a, ea, b_, eb_: a * ea + b_ * eb_, d_qt, t["eq"], d_qs, t["eb"])
            dk = _m(lambda a, ea, b_, eb_: a * ea + b_ * eb_, d_kt, t["ek"], d_k2, t["e2"])
            db_end = _m(lambda x, k_, e2, de, ee: jnp.sum(x * (k_ * e2), axis=0, keepdims=True) + de * ee,
                        d_k2, t["k"], t["e2"], d_e, t["e_end"])
            db = _m(lambda q_, dq_, k_, dk_, be: q_ * dq_ - k_ * dk_ + jnp.where(is_end, be, 0.0),
                    t["q"], dq, t["k"], dk, db_end)
            dg = _m(lambda x: _tri_matmul(tri_t, x), db)
            df = _m(lambda g_, f_, dk_: g_ / f_ - dk_, dg, t["f"], dk)
            dfraw = _m(lambda x, l_, s_: x * (1.0 - l_) * s_ * (1.0 - s_), df, lb, t["sg"])
            dl_acc = _m(lambda acc, x, s_: acc + _rowsum8(x * (1.0 - s_)), dl_acc, df, t["sg"])
            dqraw = _m(lambda x, s_, r: x * (s_ * (1.0 + r * (1.0 - s_))), dq, t["sq"], qraw)
            if final:
                dqraw = [x + pq_ref[sl, cs] for x, cs in zip(dqraw, css)]
                d_v = [x + pi_ref[sl, cs] for x, cs in zip(d_v, css)]
            for n, hh in enumerate(heads):
                dl_all[hh] = dl_acc[n]
                ds_scr[hh] = new_ds[n]
                for ref, off, val in zip((dq_ref, di_ref, df_ref), out_off, (dqraw[n], d_v[n], dfraw[n])):
                    ref[sl, pl.ds(off + hh * HG_DK, HG_DK)] = val.astype(ref.dtype)
        dl_ref[...] += jnp.concatenate(dl_all, axis=1) * (lbs * (1.0 - lbs))

    W = HG_HPS * HG_DK
    col = lambda off: (lambda h, b: (blk_map(b), off // W + h))
    blk = lambda: pl.BlockSpec((R, W), lambda h, b: (blk_map(b), h))
    ins = [proj, proj, proj, lbl4, do, states]
    in_specs = [
        pl.BlockSpec((R, W), col(OFF_Q)),
        pl.BlockSpec((R, W), col(OFF_I)),
        pl.BlockSpec((R, W), col(f_off)),
        pl.BlockSpec((4, W), lambda h, b: (0, h)),
        blk(),
        pl.BlockSpec((cpb, HG_HPS, HG_DK, HG_DK), lambda h, b: (blk_map(b), h, 0, 0)),
    ]
    dl_shape = jax.ShapeDtypeStruct((SUB, HG_W), F32)
    dl_spec = pl.BlockSpec((SUB, W), lambda h, b: (0, h))
    dp_shape = jax.ShapeDtypeStruct((S, PW), CDT)
    if final:
        ins += [prev[0], prev[1], dproj]
        in_specs += [blk(), blk(), pl.BlockSpec(memory_space=pl.ANY)]
        out_shape = [dp_shape, dl_shape]
        out_specs = [pl.BlockSpec((R, 3 * HG_W), lambda h, b: (blk_map(b), 0)), dl_spec]
        aliases = {8: 0}
    else:
        out_shape = [jax.ShapeDtypeStruct((S, HG_W), F32), jax.ShapeDtypeStruct((S, HG_W), F32), dp_shape, dl_shape]
        out_specs = [blk(), blk(), pl.BlockSpec((R, W), col(f_off)), dl_spec]
        aliases = {}
        if dproj is not None:
            ins += [dproj]
            in_specs += [pl.BlockSpec(memory_space=pl.ANY)]
            aliases = {6: 2}
    if (not final) and dproj is not None:
        def body_wrapped(*refs, _b=body):
            _b(*refs[:6], *refs[7:])
        kern = body_wrapped
    else:
        kern = body
    res = pl.pallas_call(
        kern,
        name=name,
        grid=(HG_HEADS // HG_HPS, nblk),
        in_specs=in_specs,
        out_specs=out_specs,
        out_shape=out_shape,
        scratch_shapes=[pltpu.VMEM((HG_HPS, HG_DK, HG_DK), F32)],
        input_output_aliases=aliases,
        compiler_params=_cparams(2),
    )(*ins)
    if final:
        return res[0], None, None, res[1]
    dq, di, dproj, dl = res
    return dproj, dq, di, dl


def _hgrn_post_fwd(o_f, o_b, proj, norm_g):
    S = o_f.shape[0]

    def epi(accs, tiles, rows):
        of, ob, graw = tiles
        ng = rows[0][:, :HG_DK]
        o = of + ob
        ys = []
        for h in range(HG_HEADS):
            oh = o[:, h * HG_DK:(h + 1) * HG_DK]
            rs = lax.rsqrt(jnp.mean(oh * oh, axis=-1, keepdims=True) + RMS_EPS)
            ys.append(oh * rs * ng * _sigmoid(graw[:, h * HG_DK:(h + 1) * HG_DK]))
        return [jnp.concatenate(ys, axis=1)], []

    tm = _rtile(S, 512)
    (y,) = _fused_mm("hgrn_post_fwd", "nn", [], S, HG_W, 1, tm, HG_W, 1, [(HG_W, CDT, 0, None)], epi,
                     tiles=[(o_f, 0), (o_b, 0), (proj, OFF_G)], rows=[(jnp.tile(norm_g, (1, HG_HEADS)), 0)])
    return y


def _hgrn_post_bwd(dy, o_f, o_b, proj, norm_g, dproj):
    S = o_f.shape[0]

    def epi(accs, tiles, rows):
        dyv, of, ob, graw = tiles
        ng = rows[0][:, :HG_DK]
        o = of + ob
        dos, dgs, dns = [], [], []
        for h in range(HG_HEADS):
            sl = slice(h * HG_DK, (h + 1) * HG_DK)
            oh, gh, dyh = o[:, sl], graw[:, sl], dyv[:, sl].astype(F32)
            rs = lax.rsqrt(jnp.mean(oh * oh, axis=-1, keepdims=True) + RMS_EPS)
            xh = oh * rs
            sg = _sigmoid(gh)
            dn = dyh * sg
            dgs.append(dyh * (xh * ng) * sg * (1.0 - sg))
            dns.append(dn * xh)
            dxh = dn * ng
            dos.append(rs * (dxh - xh * jnp.mean(dxh * xh, axis=-1, keepdims=True)))
        return [jnp.concatenate(dos, axis=1), jnp.concatenate(dgs, axis=1)], [jnp.concatenate(dns, axis=1)]

    tm = _rtile(S, 512)
    do, dproj, dn = _fused_mm("hgrn_post_bwd", "nn", [], S, HG_W, 1, tm, HG_W, 1,
                              [(HG_W, F32, 0, None), (dproj.shape[1], CDT, OFF_G, dproj)], epi,
                              tiles=[(dy, 0), (o_f, 0), (o_b, 0), (proj, OFF_G)],
                              rows=[(jnp.tile(norm_g, (1, HG_HEADS)), 0)], n_racc=1)
    return do, dproj, dn


def _copy_into(name, src, dst, off):
    S, W = src.shape
    tm = _rtile(S, 512)
    (dst,) = _fused_mm(name, "nn", [], S, W, 1, tm, W, 1, [(dst.shape[1], dst.dtype, off, dst)],
                       lambda accs, tiles, rows: ([tiles[0]], []), tiles=[(src, 0)])
    return dst


def _rms_stats(x):
    rs = lax.rsqrt(jnp.mean(x * x, axis=-1, keepdims=True) + RMS_EPS)
    return x * rs, rs


def _mla_up(proj, cf, sf, g_cq, g_ckv, wuq_p, wukv):
    S = proj.shape[0]
    tm = _rtile(S, 512)
    H = MLA_HEADS

    def body(cq_ref, ckv_ref, kr_ref, krot_ref, cf_ref, sf_ref, gq_ref, gkv_ref, wq_ref, wkv_ref,
             q_ref, k_ref, v_ref, vt_ref, cqn_ref, ckvn_ref):
        cqn = (_rms_stats(cq_ref[...])[0] * gq_ref[...]).astype(CDT)
        ckvn = (_rms_stats(ckv_ref[...])[0] * gkv_ref[...]).astype(CDT)
        cqn_ref[...] = cqn
        ckvn_ref[...] = ckvn
        cfv, sfv = cf_ref[...], sf_ref[...]
        k_roped = (kr_ref[...] * cfv + krot_ref[...] * sfv).astype(CDT)
        ones = jnp.ones((VT_ROWS - LANE, tm), CDT)
        for h in range(H):
            r = _nn(cqn, wq_ref[h]) * MLA_QSCALE
            q_ref[h, :, 0:LANE] = r[:, 0:LANE].astype(CDT)
            q_ref[h, :, LANE:2 * LANE] = (r[:, LANE:2 * LANE] * cfv + r[:, 2 * LANE:3 * LANE] * sfv).astype(CDT)
            kv = _nn(ckvn, wkv_ref[h])
            k_ref[h, :, 0:LANE] = kv[:, 0:LANE].astype(CDT)
            k_ref[h, :, LANE:2 * LANE] = k_roped
            vv = kv[:, LANE:2 * LANE]
            v_ref[h] = vv.astype(CDT)
            vt_ref[h, 0, 0:LANE, :] = vv.T.astype(CDT)
            vt_ref[h, 0, LANE:VT_ROWS, :] = ones

    PWb = proj.shape[1]
    kr_off = PWb - KR_PAD
    cspec = lambda off, w: pl.BlockSpec((tm, w), lambda i, o=off // w: (i, o))
    return pl.pallas_call(
        body,
        name="mla_up_fwd",
        grid=(S // tm,),
        in_specs=[
            cspec(OFF_CQ, MLA_RANK), cspec(OFF_CKV, MLA_RANK), cspec(kr_off, LANE), cspec(kr_off + LANE, LANE),
            pl.BlockSpec((tm, LANE), lambda i: (i, 0)), pl.BlockSpec((tm, LANE), lambda i: (i, 0)),
            pl.BlockSpec((1, MLA_RANK), lambda i: (0, 0)), pl.BlockSpec((1, MLA_RANK), lambda i: (0, 0)),
            pl.BlockSpec((H, MLA_RANK, 3 * LANE), lambda i: (0, 0, 0)),
            pl.BlockSpec((H, MLA_RANK, 2 * LANE), lambda i: (0, 0, 0)),
        ],
        out_specs=[
            pl.BlockSpec((H, tm, 2 * LANE), lambda i: (0, i, 0)),
            pl.BlockSpec((H, tm, 2 * LANE), lambda i: (0, i, 0)),
            pl.BlockSpec((H, tm, LANE), lambda i: (0, i, 0)),
            pl.BlockSpec((H, 1, VT_ROWS, tm), lambda i: (0, i, 0, 0)),
            pl.BlockSpec((tm, MLA_RANK), lambda i: (i, 0)),
            pl.BlockSpec((tm, MLA_RANK), lambda i: (i, 0)),
        ],
        out_shape=[
            jax.ShapeDtypeStruct((H, S, 2 * LANE), CDT), jax.ShapeDtypeStruct((H, S, 2 * LANE), CDT),
            jax.ShapeDtypeStruct((H, S, LANE), CDT), jax.ShapeDtypeStruct((H, S // tm, VT_ROWS, tm), CDT),
            jax.ShapeDtypeStruct((S, MLA_RANK), CDT), jax.ShapeDtypeStruct((S, MLA_RANK), CDT),
        ],
        compiler_params=_cparams(1),
    )(proj, proj, proj, proj, cf, sf, g_cq, g_ckv, wuq_p, wukv)


def _mla_attn_fwd(q_cat, k_cat, vt, xchg=None):
    H, S, _ = q_cat.shape
    tq = _tile(S, MLA_TQ)
    _, nkb, _, tk = vt.shape
    nq = S // tq
    nx = xchg.n if xchg is not None else 0

    def body(*refs):
        q_ref, k_ref, vt_ref = refs[:3]
        x_in = refs[3:3 + nx]
        y_ref, ot_ref, lse_ref = refs[3 + nx:6 + nx]
        x_out = refs[6 + nx:6 + 2 * nx]
        m_scr, acc_scr = refs[6 + 2 * nx:8 + 2 * nx]
        x_sems = refs[8 + 2 * nx:]
        h, i = pl.program_id(0), pl.program_id(1)
        if nx:
            @pl.when((h == 0) & (i == 0))
            def _():
                xchg.start(x_in, x_out, x_sems)

        nsub = MLA_FWD_SLABS if tq % (MLA_FWD_SLABS * LANE) == 0 else 1
        ws = tq // nsub
        subs = [pl.ds(s * ws, ws) for s in range(nsub)]
        qs = [q_ref[0, sb, :] for sb in subs]
        m_scr[...] = jnp.full_like(m_scr, -jnp.inf)
        acc_scr[...] = jnp.zeros_like(acc_scr)

        def step(j, carry):
            kj = k_ref[0, pl.ds(pl.multiple_of(j * tk, tk), tk), :]
            vtj = vt_ref[0, j]
            sts = [_nt(kj, qq) for qq in qs]
            m_old = [m_scr[:, sb] for sb in subs]
            m_new = _m(lambda mo, st: jnp.maximum(mo, jnp.max(st, axis=0, keepdims=True)), m_old, sts)
            pts = _m(lambda st, mn: jnp.exp2(st - mn), sts, m_new)
            pvs = _m(lambda pt: _nn(vtj, pt), pts)
            for sb, mo, mn, pv in zip(subs, m_old, m_new, pvs):
                acc_scr[:, sb] = jnp.exp2(mo - mn) * acc_scr[:, sb] + pv
                m_scr[:, sb] = mn
            return carry

        lax.fori_loop(0, nkb, step, 0, unroll=4 if nkb % 4 == 0 else 1)
        l = acc_scr[LANE:LANE + 1, :]
        ot = acc_scr[0:LANE, :] / l
        ot_ref[0] = ot
        y_ref[...] = ot.T.astype(CDT)
        lse_ref[0, 0] = m_scr[...] + jnp.log2(l)

        if nx:
            @pl.when((h == H - 1) & (i == nq - 1))
            def _():
                xchg.wait(x_in, x_out, x_sems)

    return pl.pallas_call(
        body,
        name="mla_attn_fwd",
        grid=(H, nq),
        in_specs=[
            pl.BlockSpec((1, tq, 2 * LANE), lambda h, i: (h, i, 0)),
            pl.BlockSpec((1, S, 2 * LANE), lambda h, i: (h, 0, 0)),
            pl.BlockSpec((1, nkb, VT_ROWS, tk), lambda h, i: (h, 0, 0, 0)),
        ] + (xchg.specs if nx else []),
        out_specs=[
            pl.BlockSpec((tq, LANE), lambda h, i: (i, h)),
            pl.BlockSpec((1, LANE, tq), lambda h, i: (h, 0, i)),
            pl.BlockSpec((1, 1, 1, tq), lambda h, i: (h, i, 0, 0)),
        ] + (xchg.specs if nx else []),
        out_shape=[
            jax.ShapeDtypeStruct((S, H * LANE), CDT),
            jax.ShapeDtypeStruct((H, LANE, S), F32),
            jax.ShapeDtypeStruct((H, nq, 1, tq), F32),
        ] + (xchg.out_shape if nx else []),
        scratch_shapes=[pltpu.VMEM((1, tq), F32), pltpu.VMEM((VT_ROWS, tq), F32)] + (xchg.scratch if nx else []),
        compiler_params=_cparams(2, side_effects=nx > 0),
    )(q_cat, k_cat, vt, *(xchg.arrs if nx else []))


def _mla_delta(dy, ot):
    H, _, S = ot.shape
    tq = _tile(S, MLA_TQ)
    nq = S // tq

    def body(dy_ref, ot_ref, d_ref):
        d_ref[0, 0] = jnp.sum(dy_ref[...].astype(F32).T * ot_ref[0], axis=0, keepdims=True)

    return pl.pallas_call(
        body,
        name="mla_delta",
        grid=(H, nq),
        in_specs=[pl.BlockSpec((tq, LANE), lambda h, i: (i, h)), pl.BlockSpec((1, LANE, tq), lambda h, i: (h, 0, i))],
        out_specs=pl.BlockSpec((1, 1, 1, tq), lambda h, i: (h, i, 0, 0)),
        out_shape=jax.ShapeDtypeStruct((H, nq, 1, tq), F32),
        compiler_params=_cparams(2),
    )(dy, ot)


def _mla_attn_bwd(q_cat, k_cat, v, dy, lse, delta, xchg=None):
    H, S, _ = q_cat.shape
    _, nq, _, tq = lse.shape
    tk = _tile(S, 512)
    nkb = S // tk
    nx = xchg.n if xchg is not None else 0

    def body(*refs):
        k_ref, v_ref, q_ref, do_ref, lse_ref, dl_ref = refs[:6]
        x_in = refs[6:6 + nx]
        dk_ref, dv_ref, dq_ref = refs[6 + nx:9 + nx]
        x_out = refs[9 + nx:9 + 2 * nx]
        dk_scr, dv_scr = refs[9 + 2 * nx:11 + 2 * nx]
        x_sems = refs[11 + 2 * nx:]
        hd, ki = pl.program_id(0), pl.program_id(1)
        if nx:
            @pl.when((hd == 0) & (ki == 0))
            def _():
                xchg.start(x_in, x_out, x_sems)

        @pl.when(ki == 0)
        def _():
            dq_ref[...] = jnp.zeros_like(dq_ref)

        kb, vb = k_ref[0], v_ref[0]
        dk_scr[...] = jnp.zeros_like(dk_scr)
        dv_scr[...] = jnp.zeros_like(dv_scr)

        def step(i, carry):
            rows = pl.ds(pl.multiple_of(i * tq, tq), tq)
            qc = q_ref[0, rows, :]
            doc = do_ref[rows, :]
            pt = jnp.exp2(_nt(kb, qc) - lse_ref[0, i])
            dv_scr[...] += _nn(pt, doc)
            dst = (pt * (_nt(vb, doc) - dl_ref[0, i])).astype(CDT)
            dk_scr[...] += _nn(dst, qc)
            dq_ref[0, rows, :] += _tn(dst, kb)
            return carry

        lax.fori_loop(0, nq, step, 0, unroll=2 if nq % 2 == 0 else 1)
        dk_ref[0] = dk_scr[...] * (MLA_SCALE / MLA_QSCALE)
        dv_ref[0] = dv_scr[...]

        if nx:
            @pl.when((hd == H - 1) & (ki == nkb - 1))
            def _():
                xchg.wait(x_in, x_out, x_sems)

    return pl.pallas_call(
        body,
        name="mla_attn_bwd",
        grid=(H, nkb),
        in_specs=[
            pl.BlockSpec((1, tk, 2 * LANE), lambda h, j: (h, j, 0)),
            pl.BlockSpec((1, tk, LANE), lambda h, j: (h, j, 0)),
            pl.BlockSpec((1, S, 2 * LANE), lambda h, j: (h, 0, 0)),
            pl.BlockSpec((S, LANE), lambda h, j: (0, h)),
            pl.BlockSpec((1, nq, 1, tq), lambda h, j: (h, 0, 0, 0)),
            pl.BlockSpec((1, nq, 1, tq), lambda h, j: (h, 0, 0, 0)),
        ] + (xchg.specs if nx else []),
        out_specs=[
            pl.BlockSpec((1, tk, 2 * LANE), lambda h, j: (h, j, 0)),
            pl.BlockSpec((1, tk, LANE), lambda h, j: (h, j, 0)),
            pl.BlockSpec((1, S, 2 * LANE), lambda h, j: (h, 0, 0)),
        ] + (xchg.specs if nx else []),
        out_shape=[
            jax.ShapeDtypeStruct((H, S, 2 * LANE), F32),
            jax.ShapeDtypeStruct((H, S, LANE), F32),
            jax.ShapeDtypeStruct((H, S, 2 * LANE), F32),
        ] + (xchg.out_shape if nx else []),
        scratch_shapes=[pltpu.VMEM((tk, 2 * LANE), F32), pltpu.VMEM((tk, LANE), F32)] + (xchg.scratch if nx else []),
        compiler_params=_cparams(2, side_effects=nx > 0),
    )(k_cat, v, q_cat, dy, lse, delta, *(xchg.arrs if nx else []))


def _mla_up_bwd(dq_cat, dk_cat, dv, proj, cf, sf, g_cq, g_ckv, wuq_p, wukv, dproj):
    H, S, _ = dq_cat.shape
    tm = _rtile(S, 256)
    PW = proj.shape[1]
    kr_off = PW - KR_PAD

    assert OFF_CKV == OFF_CQ + MLA_RANK and OFF_CQ % (2 * MLA_RANK) == 0

    def body(dq_ref, dk_ref, dv_ref, cq_ref, ckv_ref, cf_ref, sf_ref, gq_ref, gkv_ref, wq_ref, wkv_ref, _dp,
             dqp_ref, dkvp_ref, dc_ref, dkr_ref, dgq_ref, dgkv_ref):
        i = pl.program_id(0)
        dcq_ref, dckv_ref = dc_ref.at[:, 0:MLA_RANK], dc_ref.at[:, MLA_RANK:2 * MLA_RANK]
        cfv, sfv = cf_ref[...], sf_ref[...]
        aq = jnp.zeros((tm, MLA_RANK), F32)
        akv = jnp.zeros((tm, MLA_RANK), F32)
        akr = jnp.zeros((tm, LANE), F32)
        for h in range(H):
            dq = dq_ref[h] * MLA_SCALE
            dqr = dq[:, LANE:2 * LANE]
            dqp = jnp.concatenate([dq[:, 0:LANE], dqr * cfv, dqr * sfv], axis=1).astype(CDT)
            dqp_ref[h] = dqp
            aq = aq + _nt(dqp, wq_ref[h])
            dk = dk_ref[h]
            dkvp = jnp.concatenate([dk[:, 0:LANE], dv_ref[h]], axis=1).astype(CDT)
            dkvp_ref[h] = dkvp
            akv = akv + _nt(dkvp, wkv_ref[h])
            akr = akr + dk[:, LANE:2 * LANE]

        def rms_bwd(c_ref, g_ref, dn, d_ref, dg_ref):
            xh, rs = _rms_stats(c_ref[...])
            dxh = dn * g_ref[...]
            d_ref[...] = (rs * (dxh - xh * jnp.mean(dxh * xh, axis=-1, keepdims=True))).astype(d_ref.dtype)
            part = _rowsum8(dn * xh)

            @pl.when(i == 0)
            def _():
                dg_ref[...] = part

            @pl.when(i > 0)
            def _():
                dg_ref[...] += part

        rms_bwd(cq_ref, gq_ref, aq, dcq_ref, dgq_ref)
        rms_bwd(ckv_ref, gkv_ref, akv, dckv_ref, dgkv_ref)
        dkr_ref[...] = jnp.concatenate([akr * cfv, akr * sfv, jnp.zeros((tm, KR_PAD - 2 * LANE), F32)], axis=1).astype(dkr_ref.dtype)

    cspec = lambda off, w: pl.BlockSpec((tm, w), lambda i, o=off // w: (i, o))
    hspec = lambda w: pl.BlockSpec((H, tm, w), lambda i: (0, i, 0))
    outs = pl.pallas_call(
        body,
        name="mla_up_bwd",
        grid=(S // tm,),
        in_specs=[
            hspec(2 * LANE), hspec(2 * LANE), hspec(LANE),
            cspec(OFF_CQ, MLA_RANK), cspec(OFF_CKV, MLA_RANK),
            pl.BlockSpec((tm, LANE), lambda i: (i, 0)), pl.BlockSpec((tm, LANE), lambda i: (i, 0)),
            pl.BlockSpec((1, MLA_RANK), lambda i: (0, 0)), pl.BlockSpec((1, MLA_RANK), lambda i: (0, 0)),
            pl.BlockSpec((H, MLA_RANK, 3 * LANE), lambda i: (0, 0, 0)),
            pl.BlockSpec((H, MLA_RANK, 2 * LANE), lambda i: (0, 0, 0)),
            pl.BlockSpec(memory_space=pl.ANY),
        ],
        out_specs=[
            hspec(3 * LANE), hspec(2 * LANE),
            pl.BlockSpec((tm, 2 * MLA_RANK), lambda i: (i, OFF_CQ // (2 * MLA_RANK))),
            pl.BlockSpec((tm, KR_PAD), lambda i: (i, 0)),
            pl.BlockSpec((SUB, MLA_RANK), lambda i: (0, 0)),
            pl.BlockSpec((SUB, MLA_RANK), lambda i: (0, 0)),
        ],
        out_shape=[
            jax.ShapeDtypeStruct((H, S, 3 * LANE), CDT), jax.ShapeDtypeStruct((H, S, 2 * LANE), CDT),
            jax.ShapeDtypeStruct(dproj.shape, dproj.dtype),
            jax.ShapeDtypeStruct((S, KR_PAD), CDT),
            jax.ShapeDtypeStruct((SUB, MLA_RANK), F32), jax.ShapeDtypeStruct((SUB, MLA_RANK), F32),
        ],
        input_output_aliases={11: 2},
        compiler_params=_cparams(1),
    )(dq_cat, dk_cat, dv, proj, proj, cf, sf, g_cq, g_ckv, wuq_p, wukv, dproj)
    dqp, dkvp, dproj, dkr, dgq, dgkv = outs
    dproj = _copy_into("dproj_kr", dkr, dproj, kr_off)
    return dproj, dqp, dkvp, dgq, dgkv


def _heads_tn(name, a, b):
    S, Ka = a.shape
    H, _, W = b.shape
    tk = _rtile(S, 1024)
    nk = S // tk

    def body(a_ref, b_ref, o_ref, acc):
        k = pl.program_id(1)

        @pl.when(k == 0)
        def _():
            acc[...] = jnp.zeros_like(acc)

        acc[...] += _tn(a_ref[...], b_ref[0])

        @pl.when(k == nk - 1)
        def _():
            o_ref[0] = acc[...].astype(o_ref.dtype)

    return pl.pallas_call(
        body,
        name=name,
        grid=(H, nk),
        in_specs=[pl.BlockSpec((tk, Ka), lambda h, k: (k, 0)), pl.BlockSpec((1, tk, W), lambda h, k: (h, k, 0))],
        out_specs=pl.BlockSpec((1, Ka, W), lambda h, k: (h, 0, 0)),
        out_shape=jax.ShapeDtypeStruct((H, Ka, W), CDT),
        scratch_shapes=[pltpu.VMEM((Ka, W), F32)],
        compiler_params=_cparams(2),
    )(a, b)


def _mem_softmax(q, k):
    s = _nt(q, k) * (MEM_HD ** -0.5)
    p = jnp.exp(s - jnp.max(s, axis=1, keepdims=True))
    return p / jnp.sum(p, axis=1, keepdims=True)


def _mem_attn_fwd(proj, memkv):
    S = proj.shape[0]
    Mm = memkv.shape[0]
    tm = _rtile(S, 512)

    def body(q_ref, k_ref, v_ref, y_ref):
        pn = _mem_softmax(q_ref[...], k_ref[...])
        y_ref[...] = _nn(pn, v_ref[...]).astype(y_ref.dtype)

    return pl.pallas_call(
        body,
        name="mem_attn_fwd",
        grid=(S // tm, MEM_HEADS),
        in_specs=[
            pl.BlockSpec((tm, MEM_HD), lambda i, h: (i, OFF_QM // MEM_HD + h)),
            pl.BlockSpec((Mm, MEM_HD), lambda i, h: (0, h)),
            pl.BlockSpec((Mm, MEM_HD), lambda i, h: (0, MEM_HEADS + h)),
        ],
        out_specs=pl.BlockSpec((tm, MEM_HD), lambda i, h: (i, h)),
        out_shape=jax.ShapeDtypeStruct((S, MEM_W), CDT),
        compiler_params=_cparams(2),
    )(proj, memkv, memkv)


def _mem_attn_bwd(dy, proj, memkv, dproj):
    S = proj.shape[0]
    Mm = memkv.shape[0]
    tm = _rtile(S, 512)
    scale = MEM_HD ** -0.5

    def body(dy_ref, q_ref, k_ref, v_ref, _dp, dq_ref, dk_ref, dv_ref):
        i = pl.program_id(1)
        q, k, dyv = q_ref[...].astype(CDT), k_ref[...], dy_ref[...]
        pn = _mem_softmax(q, k)
        dvp = _tn(pn, dyv)
        dp = _nt(dyv, v_ref[...])
        ds = pn * (dp - jnp.sum(dp * pn, axis=1, keepdims=True)) * scale
        dq_ref[...] = _nn(ds, k).astype(dq_ref.dtype)
        dkp = _tn(ds, q)

        @pl.when(i == 0)
        def _():
            dk_ref[...] = dkp
            dv_ref[...] = dvp

        @pl.when(i > 0)
        def _():
            dk_ref[...] += dkp
            dv_ref[...] += dvp

    dproj, dk, dv = pl.pallas_call(
        body,
        name="mem_attn_bwd",
        grid=(MEM_HEADS, S // tm),
        in_specs=[
            pl.BlockSpec((tm, MEM_HD), lambda h, i: (i, h)),
            pl.BlockSpec((tm, MEM_HD), lambda h, i: (i, OFF_QM // MEM_HD + h)),
            pl.BlockSpec((Mm, MEM_HD), lambda h, i: (0, h)),
            pl.BlockSpec((Mm, MEM_HD), lambda h, i: (0, MEM_HEADS + h)),
            pl.BlockSpec(memory_space=pl.ANY),
        ],
        out_specs=[
            pl.BlockSpec((tm, MEM_HD), lambda h, i: (i, OFF_QM // MEM_HD + h)),
            pl.BlockSpec((Mm, MEM_HD), lambda h, i: (0, h)),
            pl.BlockSpec((Mm, MEM_HD), lambda h, i: (0, h)),
        ],
        out_shape=[
            jax.ShapeDtypeStruct(dproj.shape, dproj.dtype),
            jax.ShapeDtypeStruct((Mm, MEM_W), F32),
            jax.ShapeDtypeStruct((Mm, MEM_W), F32),
        ],
        input_output_aliases={4: 0},
        compiler_params=_cparams(2),
    )(dy, proj, memkv, memkv, dproj)
    return dproj, dk, dv


def _small_allreduce(vec):
    NS = vec.shape[1]

    def body(v_ref, o_ref, gbuf, send, recv):
        x, y, c, me = _my_place()
        gbuf[me] = v_ref[...]
        copies = []
        for kk in range(1, N_DEV):
            peer, _ = _peer(x, y, c, kk)
            cp = pltpu.make_async_remote_copy(src_ref=v_ref, dst_ref=gbuf.at[me], send_sem=send.at[kk - 1],
                                              recv_sem=recv.at[kk - 1], device_id=peer, device_id_type=MESH)
            cp.start()
            copies.append(cp)
        for cp in copies:
            cp.wait()
        tot = gbuf[0]
        for d in range(1, N_DEV):
            tot = tot + gbuf[d]
        o_ref[...] = jnp.sum(tot, axis=0, keepdims=True)

    return pl.pallas_call(
        body,
        name="small_allreduce",
        in_specs=[pl.BlockSpec(memory_space=pltpu.VMEM)],
        out_specs=pl.BlockSpec(memory_space=pltpu.VMEM),
        out_shape=jax.ShapeDtypeStruct((1, NS), F32),
        scratch_shapes=[pltpu.VMEM((N_DEV, SUB, NS), F32), pltpu.SemaphoreType.DMA((N_DEV - 1,)),
                        pltpu.SemaphoreType.DMA((N_DEV - 1,))],
        compiler_params=pltpu.CompilerParams(has_side_effects=True, vmem_limit_bytes=V7X_VMEM_LIMIT),
    )(vec)


def _adamw_math(g, w, m, v):
    nm = ADAM_B1 * m + (1.0 - ADAM_B1) * g
    nv = ADAM_B2 * v + (1.0 - ADAM_B2) * (g * g)
    mh = nm / (1.0 - ADAM_B1 ** ADAM_STEP)
    vh = nv / (1.0 - ADAM_B2 ** ADAM_STEP)
    delta = -ADAM_LR * (mh / (jnp.sqrt(vh) + ADAM_EPS) + ADAM_WD * w)
    return delta, nm, nv


def _adam_big(name, recv, w, m, v):
    _, R, C = w.shape
    tr = _rtile(R, max(SUB, (ADAM_BLOCK_ELEMS // C) // SUB * SUB))

    def body(r_ref, w_ref, m_ref, v_ref, g_ref, d_ref, nm_ref, nv_ref):
        g = r_ref[0].astype(F32)
        for d in range(1, N_DEV):
            g = g + r_ref[d].astype(F32)
        delta, nm, nv = _adamw_math(g, w_ref[0], m_ref[0], v_ref[0])
        g_ref[0] = g
        d_ref[0] = delta
        nm_ref[0] = nm
        nv_ref[0] = nv

    blk = pl.BlockSpec((1, tr, C), lambda i: (0, i, 0))
    return pl.pallas_call(
        body,
        name=name,
        grid=(R // tr,),
        in_specs=[pl.BlockSpec((N_DEV, tr, C), lambda i: (0, i, 0)), blk, blk, blk],
        out_specs=[blk, blk, blk, blk],
        out_shape=[jax.ShapeDtypeStruct((1, R, C), F32)] * 4,
        compiler_params=_cparams(1),
    )(recv, w, m, v)


def _to_bf16(name, w):
    _, R, C = w.shape
    tr = _rtile(R, max(SUB, (ADAM_BLOCK_ELEMS // C) // SUB * SUB))

    def body(w_ref, o_ref):
        o_ref[...] = w_ref[0].astype(CDT)

    return pl.pallas_call(
        body,
        name=name,
        grid=(R // tr,),
        in_specs=[pl.BlockSpec((1, tr, C), lambda i: (0, i, 0))],
        out_specs=pl.BlockSpec((tr, C), lambda i: (i, 0)),
        out_shape=jax.ShapeDtypeStruct((R, C), CDT),
        compiler_params=_cparams(1),
    )(w)


def _adam_small(g, w, m, v):
    def body(g_ref, w_ref, m_ref, v_ref, d_ref, nm_ref, nv_ref):
        delta, nm, nv = _adamw_math(g_ref[...], w_ref[...], m_ref[...], v_ref[...])
        d_ref[...] = delta
        nm_ref[...] = nm
        nv_ref[...] = nv

    return pl.pallas_call(body, name="adam_small", out_shape=[jax.ShapeDtypeStruct(g.shape, F32)] * 3)(g, w, m, v)


def _rot(w, axis=-1):
    x1, x2 = jnp.split(w, 2, axis=axis)
    return jnp.concatenate([-x2, x1], axis=axis)


def _unrot(dw, axis=-1):
    d1, d2 = jnp.split(dw, 2, axis=axis)
    return jnp.concatenate([d2, -d1], axis=axis)


def _pad_cols(w, width):
    return jnp.pad(w, [(0, 0)] * (w.ndim - 1) + [(0, width - w.shape[-1])])


def kernel(x, mem, positions, ln_emb_g, ln_emb_b, hgrn_lb_logits, w_in, hgrn_norm_g, mla_g_cq, mla_g_ckv, mla_w_uq, mla_w_ukv, mem_w_kv, w_branch, w_o, ln1_g, ln1_b, w_ffn_gate, w_ffn_up, w_ffn_down, ln2_g, ln2_b, loss_target, m_ln_emb_g, m_ln_emb_b, m_hgrn_lb_logits, m_w_in, m_hgrn_norm_g, m_mla_g_cq, m_mla_g_ckv, m_mla_w_uq, m_mla_w_ukv, m_mem_w_kv, m_w_branch, m_w_o, m_ln1_g, m_ln1_b, m_w_ffn_gate, m_w_ffn_up, m_w_ffn_down, m_ln2_g, m_ln2_b, v_ln_emb_g, v_ln_emb_b, v_hgrn_lb_logits, v_w_in, v_hgrn_norm_g, v_mla_g_cq, v_mla_g_ckv, v_mla_w_uq, v_mla_w_ukv, v_mem_w_kv, v_w_branch, v_w_o, v_ln1_g, v_ln1_b, v_w_ffn_gate, v_w_ffn_up, v_w_ffn_down, v_ln2_g, v_ln2_b):
    x2, tgt = x[0], loss_target[0]
    S, D = x2.shape
    Mm = mem.shape[1]
    F = w_ffn_gate.shape[2] * N_DEV
    GW = 3 * D
    PW = OFF_GATE + GW + KR_PAD
    KR = OFF_GATE + GW
    NIN = w_in.shape[2] * N_DEV
    assert NIN == OFF_GATE + MLA_ROPE + GW
    _, _, _, me = _my_place()
    row = lambda a: a.reshape(1, -1)

    br3 = lambda a: a.reshape(1, 3 * BR_W, -1)
    tp = lambda a: jnp.swapaxes(a, 1, 2)
    big_w = [tp(w_in), mla_w_uq, mla_w_ukv, mem_w_kv, br3(w_branch), w_o, tp(w_ffn_gate), tp(w_ffn_up), w_ffn_down]
    big_m = [tp(m_w_in), m_mla_w_uq, m_mla_w_ukv, m_mem_w_kv, br3(m_w_branch), m_w_o, tp(m_w_ffn_gate), tp(m_w_ffn_up),
             m_w_ffn_down]
    big_v = [tp(v_w_in), v_mla_w_uq, v_mla_w_ukv, v_mem_w_kv, br3(v_w_branch), v_w_o, tp(v_w_ffn_gate), tp(v_w_ffn_up),
             v_w_ffn_down]
    transposed = (0, 6, 7)
    wnames = ["w_in", "w_uq", "w_ukv", "mem_w_kv", "w_branch", "w_o", "w_gate", "w_up", "w_down"]
    big_wb = [_to_bf16("bf16_" + nme, w) for nme, w in zip(wnames, big_w)]
    g_in, g_lb = _all_gather_two_level("weights_all_gather", [big_wb[0], hgrn_lb_logits.reshape(4, -1)])
    win_t = g_in.reshape(NIN, D)
    kr_w = win_t[OFF_QM:OFF_QM + MLA_ROPE]
    zeros64 = jnp.zeros_like(kr_w)
    win_pt = jnp.concatenate([win_t[:OFF_FB], win_t[OFF_FF:OFF_G], win_t[OFF_FB:OFF_FF], win_t[OFF_G:OFF_QM],
                              win_t[OFF_QM + MLA_ROPE:], kr_w, zeros64, _rot(kr_w, 0), zeros64,
                              jnp.zeros((KR_PAD - 2 * LANE, D), CDT)], axis=0)
    lbl4 = jnp.transpose(g_lb, (1, 0, 2)).reshape(4, -1)

    half = MLA_ROPE // 2
    inv_freq = jnp.power(ROPE_THETA, -jnp.arange(half, dtype=F32) / half)
    ang = positions[0].astype(F32)[:, None] * inv_freq
    cf = _pad_cols(jnp.tile(jnp.cos(ang), (1, 2)), LANE)
    sf = _pad_cols(jnp.tile(jnp.sin(ang), (1, 2)), LANE)

    tm512 = _rtile(S, 512)
    ident = lambda accs, tiles, rows: ([accs[0]], [])

    def epi_ln0(accs, tiles, rows):
        h = _ln_stats(tiles[0])[0] * rows[0] + rows[1]
        return [h, h], []

    h0, h0b = _fused_mm("ln_emb_fwd", "nn", [], S, D, 1, tm512, D, 1, [(D, F32, 0, None), (D, CDT, 0, None)], epi_ln0,
                        tiles=[(x2, 0)], rows=[(row(ln_emb_g), 0), (row(ln_emb_b), 0)])
    proj, g_uq, g_ukv, g_mkv, g_wb, g_wo = _fused_mm(
        "proj", "nt", [[(h0b, 0, win_pt, 0)]], S, PW, D, _rtile(S, 1024), _tile(PW, 1536), D, [(PW, F32, 0, None)], ident,
        xchg=_Xchg(big_wb[1:6], False), msplit=2 if S % 2048 == 0 else 1)
    wuq_p =jnp.concatenate([g_uq[..., :MLA_NOPE], _pad_cols(g_uq[..., MLA_NOPE:], LANE),
                             _pad_cols(_rot(g_uq[..., MLA_NOPE:]), LANE)], axis=-1)
    wukv = g_ukv
    wmkv = g_mkv.reshape(-1, g_mkv.shape[-1])
    wb = jnp.transpose(g_wb.reshape(N_DEV, 3, BR_W, -1), (1, 2, 0, 3)).reshape(3, BR_W, D)
    wo = g_wo.reshape(-1, D)
    o_f, st_f = _gla_fwd(proj, lbl4, OFF_FF, False, "gla_fwd_f")
    o_b, st_b = _gla_fwd(proj, lbl4, OFF_FB, True, "gla_fwd_b")
    y_hg = _hgrn_post_fwd(o_f, o_b, proj, hgrn_norm_g)
    q_cat, k_cat, v_mla, vt_mla, cqn, ckvn = _mla_up(proj, cf, sf, mla_g_cq, mla_g_ckv, wuq_p, wukv)
    y_mla, ot, lse, g_wg, g_wu, g_wd = _mla_attn_fwd(q_cat, k_cat, vt_mla, _Xchg(big_wb[6:9], False))
    wg_t, wu_t = g_wg.reshape(F, D), g_wu.reshape(F, D)
    wd = g_wd.reshape(-1, D)
    memb = mem[0].astype(CDT)
    (memkv,) = _fused_mm("mem_kv", "nn", [[(memb, 0, wmkv, 0)]], Mm, 2 * MEM_W, D, Mm, _tile(2 * MEM_W, 512), D,
                         [(2 * MEM_W, CDT, 0, None)], ident)
    y_mem = _mem_attn_fwd(proj, memkv)
    ys = [y_hg, y_mla, y_mem]
    tnD = _tile(D, 1024, OFF_GATE)

    def epi_branch(accs, tiles, rows):
        return [_sigmoid(tiles[0]) * accs[0] + _sigmoid(tiles[1]) * accs[1] + _sigmoid(tiles[2]) * accs[2]], []

    (merged,) = _fused_mm("branch_fwd", "nn", [[(ys[b], 0, wb[b], 0)] for b in range(3)], S, D, BR_W, tm512, tnD, BR_W,
                          [(D, CDT, 0, None)], epi_branch, tiles=[(proj, OFF_GATE + b * D) for b in range(3)])

    def epi_ln1(accs, tiles, rows):
        r1v = ALPHA * tiles[0] + accs[0]
        return [r1v, _ln_stats(r1v)[0] * rows[0] + rows[1]], []

    r1, h1b = _fused_mm("wo_ln1", "nn", [[(merged, 0, wo, 0)]], S, D, D, tm512, D, D,
                        [(D, F32, 0, None), (D, CDT, 0, None)], epi_ln1, tiles=[(h0, 0)], rows=[(ln1_g, 0), (ln1_b, 0)])
    tnF = _tile(F, 512)

    def epi_up(accs, tiles, rows):
        gp, up = accs
        return [gp, up, gp * _sigmoid(gp) * up], []

    tm1k, ms1k = _rtile(S, 1024), (2 if S % 2048 == 0 else 1)
    gpb, upb, act = _fused_mm("ffn_up", "nt", [[(h1b, 0, wg_t, 0)], [(h1b, 0, wu_t, 0)]], S, F, D, tm1k, tnF, D,
                              [(F, CDT, 0, None)] * 3, epi_up, msplit=ms1k)

    def epi_down(accs, tiles, rows):
        g1, b1, g2, b2 = rows
        h1 = _ln_stats(tiles[0])[0] * g1 + b1
        xh2, rstd2 = _ln_stats(ALPHA * h1 + accs[0])
        diff = xh2 * g2 + b2 - tiles[1]
        dh2 = diff * (1.0 / D)
        dr2v = _ln_bwd(dh2, xh2, rstd2, g2)
        return [dr2v, dr2v], [dh2 * xh2, dh2, diff * diff * (0.5 / D)]

    acc_first = lambda epi: (lambda accs, tiles, rows: epi([tiles[0]], tiles[1:], rows))
    tm256 = _rtile(S, 256)
    (ff,) = _fused_mm("ffn_down", "nn", [[(act, 0, wd, 0)]], S, D, F, tm1k, _tile(D, 512), F, [(D, F32, 0, None)], ident,
                      msplit=ms1k)
    dr2, dr2b, dg2, db2, lossp = _fused_mm(
        "ffn_ln2_loss", "nn", [], S, D, 1, tm256, D, 1, [(D, F32, 0, None), (D, CDT, 0, None)], acc_first(epi_down),
        tiles=[(ff, 0), (r1, 0), (tgt, 0)], rows=[(ln1_g, 0), (ln1_b, 0), (ln2_g, 0), (ln2_b, 0)], n_racc=3)

    def epi_dact(accs, tiles, rows):
        da, gp, up = accs[0], tiles[0].astype(F32), tiles[1].astype(F32)
        s = _sigmoid(gp)
        return [da * up * (s * (1.0 + gp * (1.0 - s))), da * (gp * s)], []

    dgp, dup = _fused_mm("ffn_dact", "nt", [[(dr2b, 0, wd, 0)]], S, F, D, tm1k, tnF, D, [(F, CDT, 0, None)] * 2,
                         epi_dact, tiles=[(gpb, 0), (upb, 0)], msplit=ms1k)
    tkS = _rtile(S, 2048)
    (d_wd,) = _fused_mm("dw_down", "tn", [[(act, 0, dr2b, 0)]], F, D, S, tnF, D, tkS, [(D, CDT, 0, None)], ident)
    d_wg_t, d_wu_t = _fused_mm("dw_gate_up", "tn", [[(dgp, 0, h1b, 0)], [(dup, 0, h1b, 0)]], F, D, S, tnF, _tile(D, 1024),
                               tkS, [(D, CDT, 0, None)] * 2, lambda accs, tiles, rows: (accs, []))

    def epi_dh1(accs, tiles, rows):
        dh1 = accs[0] + ALPHA * tiles[0]
        xh1, rstd1 = _ln_stats(tiles[1])
        dr1v = _ln_bwd(dh1, xh1, rstd1, rows[0])
        return [dr1v, dr1v], [dh1 * xh1, dh1]

    rows8 = lambda dw: dw.reshape(N_DEV, -1, dw.shape[-1])
    (dh1_acc,) = _fused_mm("dh1", "nn", [[(dgp, 0, wg_t, 0), (dup, 0, wu_t, 0)]], S, D, F, tm512, _tile(D, 512), F,
                           [(D, F32, 0, None)], ident)
    dr1, dr1b, dg1, db1 = _fused_mm(
        "dh1_ln1", "nn", [], S, D, 1, tm256, D, 1, [(D, F32, 0, None), (D, CDT, 0, None)], acc_first(epi_dh1),
        tiles=[(dh1_acc, 0), (dr2, 0), (r1, 0)], rows=[(ln1_g, 0)], n_racc=2)
    (dmerged,) = _fused_mm("dmerged", "nt", [[(dr1b, 0, wo, 0)]], S, D, D, tm512, D, D, [(D, CDT, 0, None)], ident)
    (d_wo,) = _fused_mm("dw_o", "tn", [[(merged, 0, dr1b, 0)]], D, D, S, _tile(D, 512), D, tkS, [(D, CDT, 0, None)], ident)

    def epi_dbranch(accs, tiles, rows):
        dm, s = tiles[0].astype(F32), _sigmoid(tiles[1])
        return [dm * s, dm * accs[0] * s * (1.0 - s)], []

    dproj = None
    d_wbs, dys = [], []
    for b in range(3):
        du, dproj = _fused_mm(f"branch_bwd{b}", "nn", [[(ys[b], 0, wb[b], 0)]], S, D, BR_W, tm512, tnD, BR_W,
                              [(D, CDT, 0, None), (PW, CDT, OFF_GATE + b * D, dproj)], epi_dbranch,
                              tiles=[(dmerged, 0), (proj, OFF_GATE + b * D)])
        (dwb,) = _fused_mm(f"dw_branch{b}", "tn", [[(ys[b], 0, du, 0)]], BR_W, D, S, _tile(BR_W, 512), D, tkS,
                           [(D, CDT, 0, None)], ident)
        (dyb,) = _fused_mm(f"dy_branch{b}", "nt", [[(du, 0, wb[b], 0)]], S, BR_W, D, tm512, BR_W, D,
                           [(BR_W, F32 if b == 0 else CDT, 0, None)], ident)
        d_wbs.append(dwb)
        dys.append(dyb)
    dy_hg, dy_mla, dy_mem = dys

    dproj, dk_mem, dv_mem = _mem_attn_bwd(dy_mem, proj, memkv, dproj)
    dkv_mem = jnp.concatenate([dk_mem, dv_mem], axis=1).astype(CDT)
    (d_wmkv,) = _fused_mm("dw_memkv", "tn", [[(memb, 0, dkv_mem, 0)]], D, 2 * MEM_W, Mm, _tile(D, 512), 2 * MEM_W, Mm,
                          [(2 * MEM_W, CDT, 0, None)], ident)

    delta = _mla_delta(dy_mla, ot)
    dk_cat, dv_h, dq_cat, r_wg, r_wu, r_wd = _mla_attn_bwd(
        q_cat, k_cat, v_mla, dy_mla, lse, delta, _Xchg([rows8(d_wg_t), rows8(d_wu_t), rows8(d_wd)], True))
    dproj, dqp, dkvp, dgq, dgkv = _mla_up_bwd(dq_cat, dk_cat, dv_h, proj, cf, sf, mla_g_cq, mla_g_ckv, wuq_p, wukv, dproj)
    d_wuq_p = _heads_tn("dw_uq", cqn, dqp).astype(F32)
    d_wukv = _heads_tn("dw_ukv", ckvn, dkvp)
    d_wuq = jnp.concatenate([d_wuq_p[..., :MLA_NOPE],
                             d_wuq_p[..., LANE:LANE + MLA_ROPE] + _unrot(d_wuq_p[..., 2 * LANE:2 * LANE + MLA_ROPE])],
                            axis=-1).astype(CDT)

    do_hg, dproj, dng = _hgrn_post_bwd(dy_hg, o_f, o_b, proj, hgrn_norm_g, dproj)
    dproj, dq1, di1, dl_f = _gla_bwd(proj, lbl4, OFF_FF, False, do_hg, st_f, dproj, None, "gla_bwd_f")
    dproj, _, _, dl_b = _gla_bwd(proj, lbl4, OFF_FB, True, do_hg, st_b, dproj, (dq1, di1), "gla_bwd_b")

    def epi_dh0(accs, tiles, rows):
        dh0 = accs[0] + ALPHA * tiles[0]
        xh, rstd = _ln_stats(tiles[1])
        return [_ln_bwd(dh0, xh, rstd, rows[0])], [dh0 * xh, dh0]

    d_wb = jnp.transpose(jnp.stack(d_wbs).reshape(3, BR_W, N_DEV, -1), (2, 0, 1, 3)).reshape(N_DEV, 3 * BR_W, -1)
    d_win_pt, r_uq, r_ukv, r_mkv, r_wb, r_wo = _fused_mm(
        "dw_in", "tn", [[(dproj, 0, h0b, 0)]], PW, D, S, _tile(PW, 1536), _tile(D, 1024), tkS, [(D, CDT, 0, None)], ident,
        xchg=_Xchg([d_wuq, d_wukv, rows8(d_wmkv), d_wb, rows8(d_wo)], True))
    d_kr = (d_win_pt[KR:KR + MLA_ROPE].astype(F32) + _unrot(d_win_pt[KR + LANE:KR + LANE + MLA_ROPE].astype(F32), 0)).astype(CDT)
    d_win_t = jnp.concatenate([d_win_pt[:OFF_FB], d_win_pt[OFF_FF:OFF_G], d_win_pt[OFF_FB:OFF_FF],
                               d_win_pt[OFF_G:OFF_QM], d_kr, d_win_pt[OFF_QM:KR]], axis=0)
    grad_x, dge, dbe, r_in = _fused_mm(
        "dh0_ln_emb", "nn", [[(dproj, 0, win_pt, 0)]], S, D, PW, tm512, D, _tile(PW, 1536), [(D, F32, 0, None)], epi_dh0,
        tiles=[(dr1, 0), (x2, 0)], rows=[(row(ln_emb_g), 0)], n_racc=2, xchg=_Xchg([rows8(d_win_t)], True))

    recv = [r_in, r_uq, r_ukv, r_mkv, r_wb, r_wo, r_wg, r_wu, r_wd]
    names = ["w_in", "w_uq", "w_ukv", "mem_w_kv", "w_branch", "w_o", "w_gate", "w_up", "w_down"]
    big_out = [_adam_big("adam_" + nme, r, w, m_, v_) for nme, r, w, m_, v_ in zip(names, recv, big_w, big_m, big_v)]

    parts = [dge, dbe, dng, dgq, dgkv, dg1, db1, dg2, db2, dl_f, dl_b, lossp]
    widths = [p.shape[1] for p in parts]
    red = _small_allreduce(jnp.concatenate(parts, axis=1))[0]
    offs = [sum(widths[:i]) for i in range(len(widths))]
    rs = [red[o:o + w_] for o, w_ in zip(offs, widths)]
    g_le_g, g_le_b, g_ng, g_gq, g_gkv, g_l1g, g_l1b, g_l2g, g_l2b, g_dlf, g_dlb, g_loss = rs
    loss = jnp.sum(g_loss)
    g_ng = g_ng.reshape(HG_HEADS, HG_DK).sum(axis=0)
    dl0 = jnp.stack([g_dlf, g_dlb])
    g_lb_full = jnp.stack([dl0, -dl0], axis=1)
    lbw = hgrn_lb_logits.shape[2]
    g_lb = lax.dynamic_slice_in_dim(g_lb_full, me * lbw, lbw, axis=2)

    small_g = [g_le_g, g_le_b, g_lb, g_ng.reshape(1, -1), g_gq.reshape(1, -1), g_gkv.reshape(1, -1), g_l1g.reshape(1, -1),
               g_l1b.reshape(1, -1), g_l2g.reshape(1, -1), g_l2b.reshape(1, -1)]
    small_w = [ln_emb_g, ln_emb_b, hgrn_lb_logits, hgrn_norm_g, mla_g_cq, mla_g_ckv, ln1_g, ln1_b, ln2_g, ln2_b]
    small_m = [m_ln_emb_g, m_ln_emb_b, m_hgrn_lb_logits, m_hgrn_norm_g, m_mla_g_cq, m_mla_g_ckv, m_ln1_g, m_ln1_b, m_ln2_g, m_ln2_b]
    small_v = [v_ln_emb_g, v_ln_emb_b, v_hgrn_lb_logits, v_hgrn_norm_g, v_mla_g_cq, v_mla_g_ckv, v_ln1_g, v_ln1_b, v_ln2_g, v_ln2_b]
    small_g = [g.reshape(w.shape) for g, w in zip(small_g, small_w)]
    pack = lambda lst: jnp.concatenate([a.reshape(-1) for a in lst]).reshape(1, -1)
    s_delta, s_nm, s_nv = _adam_small(pack(small_g), pack(small_w), pack(small_m), pack(small_v))
    sizes = [w.size for w in small_w]
    soffs = [sum(sizes[:i]) for i in range(len(sizes))]
    unpack = lambda p: [p[0, o:o + n].reshape(w.shape) for o, n, w in zip(soffs, sizes, small_w)]
    s_delta, s_nm, s_nv = unpack(s_delta), unpack(s_nm), unpack(s_nv)

    def ordered(small, big):
        sm = list(small)
        big = [tp(b) if n in transposed else b for n, b in enumerate(big)]
        bg = [b.reshape(w.shape) for b, w in zip(big, [w_in, mla_w_uq, mla_w_ukv, mem_w_kv, w_branch, w_o, w_ffn_gate, w_ffn_up, w_ffn_down])]
        return [sm[0], sm[1], sm[2], bg[0], sm[3], sm[4], sm[5], bg[1], bg[2], bg[3], bg[4], bg[5], sm[6], sm[7], bg[6], bg[7], bg[8], sm[8], sm[9]]

    grads = ordered(small_g, [o[0] for o in big_out])
    deltas = ordered(s_delta, [o[1] for o in big_out])
    new_m = ordered(s_nm, [o[2] for o in big_out])
    new_v = ordered(s_nv, [o[3] for o in big_out])
    return (loss, grad_x[None], *grads, *deltas, *new_m, *new_v)
```

```python
import functools

import jax
import jax.numpy as jnp
from jax import lax
from jax.experimental import pallas as pl
from jax.experimental.pallas import tpu as pltpu

F32 = jnp.float32
CDT = jnp.bfloat16
MESH = pl.DeviceIdType.MESH
N_DEV = 8
V7X_VMEM_LIMIT = 60 * 1024 * 1024
LANE = 128
SUB = 8

HG_HEADS, HG_DK, HG_CHUNK = 8, 128, 64
HG_HPS = 8
HG_BWD_GROUP = 8
HG_W = HG_HEADS * HG_DK
MLA_HEADS, MLA_RANK, MLA_NOPE, MLA_ROPE, MLA_V = 8, 512, 128, 64, 128
MLA_QK = MLA_NOPE + MLA_ROPE
MLA_SCALE = MLA_QK ** -0.5
MLA_QSCALE = MLA_SCALE * 1.4426950408889634
VT_ROWS = LANE + 16
MLA_TQ = 1024
MLA_FWD_SLABS = 4
MLA_W = MLA_HEADS * MLA_V
MEM_HEADS, MEM_HD = 4, 256
MEM_W = MEM_HEADS * MEM_HD
BR_W = 1024
ROPE_THETA = 10000.0
ALPHA = 2.0 ** 0.25
LN_EPS = 1e-5
RMS_EPS = 1e-6
ADAM_LR, ADAM_B1, ADAM_B2, ADAM_EPS, ADAM_WD, ADAM_STEP = 0.001, 0.9, 0.999, 1e-08, 0.01, 10
ADAM_BLOCK_ELEMS = 256 * 1024

OFF_Q, OFF_I, OFF_FB, OFF_FF, OFF_G = 0, 1024, 2048, 3072, 4096
OFF_CQ, OFF_CKV, OFF_QM, OFF_GATE = 5120, 5632, 6144, 7168
KR_PAD = 512


def _cparams(n_grid, side_effects=False):
    return pltpu.CompilerParams(dimension_semantics=("arbitrary",) * n_grid, vmem_limit_bytes=V7X_VMEM_LIMIT,
                                has_side_effects=side_effects)


def _tile(n, pref, *offsets):
    if n <= pref and all(o % n == 0 for o in offsets):
        return n
    t = (min(pref, n) // LANE) * LANE
    while t >= LANE:
        if n % t == 0 and all(o % t == 0 for o in offsets):
            return t
        t -= LANE
    raise ValueError(f"no tile for {n} {pref} {offsets}")


def _rtile(n, pref):
    if n <= pref:
        return n
    t = (pref // SUB) * SUB
    while t >= SUB:
        if n % t == 0:
            return t
        t -= SUB
    raise ValueError(f"no row tile for {n} {pref}")


def _dot(a, b, dims):
    return lax.dot_general(a.astype(CDT), b.astype(CDT), (dims, ((), ())), preferred_element_type=F32)


def _nn(a, b):
    return _dot(a, b, ((1,), (0,)))


def _nt(a, b):
    return _dot(a, b, ((1,), (1,)))


def _tn(a, b):
    return _dot(a, b, ((0,), (0,)))


_DOTS = {"nn": _nn, "nt": _nt, "tn": _tn}


def _sigmoid(x):
    return 1.0 / (1.0 + jnp.exp(-x))


def _rowsum8(v):
    r, w = v.shape
    return v.reshape(r // SUB, SUB, w).sum(axis=0)


def _my_place():
    x, y, c = lax.axis_index("x"), lax.axis_index("y"), lax.axis_index("c")
    return x, y, c, 4 * x + 2 * y + c


def _peer(x, y, c, kk):
    px = 1 - x if kk & 4 else x
    py = 1 - y if kk & 2 else y
    pc = 1 - c if kk & 1 else c
    return (px, py, pc), 4 * px + 2 * py + pc


class _Xchg:
    def __init__(self, arrs, scatter):
        self.arrs, self.scatter, self.n = list(arrs), scatter, len(arrs)
        hbm = pl.BlockSpec(memory_space=pl.ANY)
        self.specs = [hbm] * self.n
        self.out_shape = [jax.ShapeDtypeStruct(((N_DEV,) + a.shape[1:]) if scatter else ((N_DEV,) + a.shape), a.dtype)
                          for a in self.arrs]
        ncp = self.n * (N_DEV - 1)
        self.scratch = [pltpu.SemaphoreType.DMA((ncp,)), pltpu.SemaphoreType.DMA((ncp,)), pltpu.SemaphoreType.DMA((self.n,))]

    def _copies(self, ins, outs, send, recv, loc):
        x, y, c, me = _my_place()
        copies = []
        for w in range(self.n):
            copies.append(pltpu.make_async_copy(ins[w].at[me] if self.scatter else ins[w], outs[w].at[me], loc.at[w]))
            for kk in range(1, N_DEV):
                peer, pid = _peer(x, y, c, kk)
                s = w * (N_DEV - 1) + kk - 1
                copies.append(pltpu.make_async_remote_copy(
                    src_ref=ins[w].at[pid] if self.scatter else ins[w], dst_ref=outs[w].at[me],
                    send_sem=send.at[s], recv_sem=recv.at[s], device_id=peer, device_id_type=MESH))
        return copies

    def start(self, ins, outs, sems):
        for cp in self._copies(ins, outs, *sems):
            cp.start()

    def wait(self, ins, outs, sems):
        for cp in self._copies(ins, outs, *sems):
            cp.wait()


def _all_gather_two_level(name, arrs):
    n = len(arrs)
    NC = N_DEV - 1

    def body(*refs):
        ins, outs = refs[:n], refs[n:2 * n]
        send, recv, loc = refs[2 * n:]
        x, y, c, me = _my_place()
        sibling = (x, y, 1 - c)
        chips = [(1 - x, y), (x, 1 - y), (1 - x, 1 - y)]
        slot = lambda px, py, pc: 4 * px + 2 * py + pc

        def copy(w, k, block, to, src=None):
            dst = outs[w].at[slot(*block)]
            return pltpu.make_async_remote_copy(src_ref=dst if src is None else src, dst_ref=dst,
                                                send_sem=send.at[w * NC + k], recv_sem=recv.at[w * NC + k],
                                                device_id=to, device_id_type=MESH)

        mine = [pltpu.make_async_copy(ins[w], outs[w].at[me], loc.at[w]) for w in range(n)]
        for cp in mine:
            cp.start()
        first = []
        for w in range(n):
            first.append(copy(w, 0, (x, y, c), sibling, src=ins[w]))
            first += [copy(w, 1 + j, (x, y, c), (*chip, c), src=ins[w]) for j, chip in enumerate(chips)]
        for cp in first:
            cp.start()
        passed = []
        for j, chip in enumerate(chips):
            for w in range(n):
                copy(w, 1 + j, (*chip, c), (x, y, c)).wait_recv()
                fwd = copy(w, 4 + j, (*chip, c), sibling)
                fwd.start()
                passed.append(fwd)
        for w in range(n):
            copy(w, 0, sibling, (x, y, c)).wait_recv()
            for j, chip in enumerate(chips):
                copy(w, 4 + j, (*chip, 1 - c), (x, y, c)).wait_recv()
        for cp in first + passed:
            cp.wait_send()
        for cp in mine:
            cp.wait()

    hbm = pl.BlockSpec(memory_space=pl.ANY)
    return pl.pallas_call(
        body,
        name=name,
        in_specs=[hbm] * n,
        out_specs=[hbm] * n,
        out_shape=[jax.ShapeDtypeStruct((N_DEV,) + a.shape, a.dtype) for a in arrs],
        scratch_shapes=[pltpu.SemaphoreType.DMA((n * NC,)), pltpu.SemaphoreType.DMA((n * NC,)), pltpu.SemaphoreType.DMA((n,))],
        compiler_params=pltpu.CompilerParams(has_side_effects=True),
    )(*arrs)


def _fused_mm(name, mode, groups, M, N, K, tm, tn, tk, outs, epi, tiles=(), rows=(), n_racc=0, xchg=None, msplit=1):
    ni, nj, nk = M // tm, N // tn, K // tk
    assert M % tm == 0 and N % tn == 0 and K % tk == 0, (name, M, N, K, tm, tn, tk)
    assert n_racc == 0 or nj == 1
    assert msplit == 1 or (nk == 1 and n_racc == 0 and tm % (16 * msplit) == 0)
    dot = _DOTS[mode] if groups else None
    ins, in_specs = [], []
    for g in groups:
        for a, a_off, b, b_off in g:
            if mode == "tn":
                assert a_off % tm == 0
                in_specs.append(pl.BlockSpec((tk, tm), lambda i, j, k, o=a_off // tm: (k, i + o)))
            else:
                assert a_off % tk == 0
                in_specs.append(pl.BlockSpec((tm, tk), lambda i, j, k, o=a_off // tk: (i, k + o)))
            ins.append(a)
            if mode == "nt":
                assert b_off % tk == 0
                in_specs.append(pl.BlockSpec((tn, tk), lambda i, j, k, o=b_off // tk: (j, k + o)))
            else:
                assert b_off % tn == 0
                in_specs.append(pl.BlockSpec((tk, tn), lambda i, j, k, o=b_off // tn: (k, j + o)))
            ins.append(b)
    for arr, off in tiles:
        assert off % tn == 0
        ins.append(arr)
        in_specs.append(pl.BlockSpec((tm, tn), lambda i, j, k, o=off // tn: (i, j + o)))
    for arr, off in rows:
        assert off % tn == 0
        ins.append(arr)
        in_specs.append(pl.BlockSpec((1, tn), lambda i, j, k, o=off // tn: (0, j + o)))
    aliases = {}
    out_shape, out_specs = [], []
    for oi, (width, dtype, off, alias) in enumerate(outs):
        assert off % tn == 0
        if alias is not None:
            aliases[len(ins)] = oi
            ins.append(alias)
            in_specs.append(pl.BlockSpec(memory_space=pl.ANY))
        out_shape.append(jax.ShapeDtypeStruct((M, width), dtype))
        out_specs.append(pl.BlockSpec((tm, tn), lambda i, j, k, o=off // tn: (i, j + o)))
    for _ in range(n_racc):
        out_shape.append(jax.ShapeDtypeStruct((SUB, N), F32))
        out_specs.append(pl.BlockSpec((SUB, tn), lambda i, j, k: (0, 0)))
    n_alias = len(aliases)
    n_pairs = [len(g) for g in groups]
    use_scratch = nk > 1
    scratch = [pltpu.VMEM((tm, tn), F32) for _ in groups] if use_scratch else []
    nx = 0
    if xchg is not None:
        nx = xchg.n
        ins += xchg.arrs
        in_specs += xchg.specs
        out_shape += xchg.out_shape
        out_specs += xchg.specs
        scratch += xchg.scratch

    def body(*refs):
        it = iter(refs)
        pair_refs = [[(next(it), next(it)) for _ in range(n)] for n in n_pairs]
        tile_refs = [next(it) for _ in tiles]
        row_refs = [next(it) for _ in rows]
        for _ in range(n_alias):
            next(it)
        x_in = [next(it) for _ in range(nx)]
        out_refs = [next(it) for _ in outs]
        racc_refs = [next(it) for _ in range(n_racc)]
        x_out = [next(it) for _ in range(nx)]
        acc_refs = [next(it) for _ in groups] if use_scratch else []
        x_sems = list(it)
        i, j, k = pl.program_id(0), pl.program_id(1), pl.program_id(2)
        if nx:
            @pl.when((i == 0) & (j == 0) & (k == 0))
            def _():
                xchg.start(x_in, x_out, x_sems)

        def products():
            res = []
            for prs in pair_refs:
                s = None
                for a_ref, b_ref in prs:
                    d = dot(a_ref[...], b_ref[...])
                    s = d if s is None else s + d
                res.append(s)
            return res

        def finish(accs):
            out_v, racc_v = epi(accs, [t[...] for t in tile_refs], [r[...] for r in row_refs])
            for o_ref, v in zip(out_refs, out_v):
                o_ref[...] = v.astype(o_ref.dtype)
            for r_ref, v in zip(racc_refs, racc_v):
                part = _rowsum8(v)

                @pl.when(i == 0)
                def _():
                    r_ref[...] = part

                @pl.when(i > 0)
                def _():
                    r_ref[...] += part

        if not use_scratch and msplit > 1:
            ts = tm // msplit
            for s in range(msplit):
                rs = pl.ds(s * ts, ts)
                accs = []
                for prs in pair_refs:
                    acc = None
                    for a_ref, b_ref in prs:
                        dd = dot(a_ref[:, rs] if mode == "tn" else a_ref[rs, :], b_ref[...])
                        acc = dd if acc is None else acc + dd
                    accs.append(acc)
                out_v, _ = epi(accs, [t[rs, :] for t in tile_refs], [r[...] for r in row_refs])
                for o_ref, v in zip(out_refs, out_v):
                    o_ref[rs, :] = v.astype(o_ref.dtype)
        elif not use_scratch:
            finish(products())
        else:
            @pl.when(k == 0)
            def _():
                for acc in acc_refs:
                    acc[...] = jnp.zeros_like(acc)

            for acc, p in zip(acc_refs, products()):
                acc[...] += p

            @pl.when(k == nk - 1)
            def _():
                finish([acc[...] for acc in acc_refs])

        if nx:
            @pl.when((i == ni - 1) & (j == nj - 1) & (k == nk - 1))
            def _():
                xchg.wait(x_in, x_out, x_sems)

    res = pl.pallas_call(
        body,
        name=name,
        grid=(ni, nj, nk),
        in_specs=in_specs,
        out_specs=out_specs,
        out_shape=out_shape,
        scratch_shapes=scratch,
        input_output_aliases=aliases,
        compiler_params=_cparams(3, side_effects=nx > 0),
    )(*ins)
    return res


def _ln_stats(r):
    mu = jnp.mean(r, axis=-1, keepdims=True)
    xc = r - mu
    var = jnp.mean(xc * xc, axis=-1, keepdims=True)
    rstd = lax.rsqrt(var + LN_EPS)
    return xc * rstd, rstd


def _ln_bwd(dh, xhat, rstd, g):
    dxh = dh * g
    m1 = jnp.mean(dxh, axis=-1, keepdims=True)
    m2 = jnp.mean(dxh * xhat, axis=-1, keepdims=True)
    return rstd * (dxh - m1 - xhat * m2)


def _split3(x):
    hi = x.astype(CDT)
    r1 = x - hi.astype(F32)
    mid = r1.astype(CDT)
    lo = (r1 - mid.astype(F32)).astype(CDT)
    return hi, mid, lo


def _tri_matmul(tri, x):
    hi, mid, lo = _split3(x)
    return _nn(tri, hi) + _nn(tri, mid) + _nn(tri, lo)


def _dot3(dot, a, b):
    a_hi, b_hi = a.astype(CDT), b.astype(CDT)
    a_lo = (a - a_hi.astype(F32)).astype(CDT)
    b_lo = (b - b_hi.astype(F32)).astype(CDT)
    return dot(a_hi, b_hi) + dot(a_hi, b_lo) + dot(a_lo, b_hi)


def _gla_masks(reverse):
    C = HG_CHUNK
    r = lax.broadcasted_iota(jnp.int32, (C, C), 0)
    c = lax.broadcasted_iota(jnp.int32, (C, C), 1)
    keep = (c >= r) if reverse else (r >= c)
    return keep


def _m(fn, *lists):
    return [fn(*args) for args in zip(*lists)]


def _gla_chunk_fwd(qraw, fraw, lb, keep, reverse):
    C = HG_CHUNK
    end = 0 if reverse else C - 1
    tri = jnp.where(keep, 1.0, 0.0).astype(CDT)
    sq = _m(_sigmoid, qraw)
    q = _m(lambda x, s: x * s, qraw, sq)
    sg = _m(_sigmoid, fraw)
    f = _m(lambda l_, s: l_ + (1.0 - l_) * s, lb, sg)
    k = _m(lambda x: 1.0 - x, f)
    g = _m(jnp.log, f)
    b = _m(lambda x: _tri_matmul(tri, x), g)
    b_end = _m(lambda x: x[end:end + 1, :], b)
    b_mid = _m(lambda x: x[C // 2:C // 2 + 1, :], b)
    eq = _m(lambda x, m_: jnp.exp(x - m_), b, b_mid)
    ek = _m(lambda x, m_: jnp.exp(m_ - x), b, b_mid)
    eb = _m(jnp.exp, b)
    e2 = _m(lambda x, e_: jnp.exp(e_ - x), b, b_end)
    e_end = _m(jnp.exp, b_end)
    qt = _m(lambda x, e_: x * e_, q, eq)
    kt = _m(lambda x, e_: x * e_, k, ek)
    qs = _m(lambda x, e_: (x * e_).astype(CDT), q, eb)
    k2 = _m(lambda x, e_: (x * e_).astype(CDT), k, e2)
    a = _m(lambda x, y: jnp.where(keep, _dot3(_nt, x, y), 0.0).astype(CDT), qt, kt)
    return dict(sq=sq, q=q, sg=sg, f=f, k=k, eq=eq, ek=ek, eb=eb, e2=e2, e_end=e_end, qt=qt, kt=kt, qs=qs, k2=k2, a=a)


def _gla_fwd(proj, lbl4, f_off, reverse, name):
    S = proj.shape[0]
    C = HG_CHUNK
    R = _rtile(S, 512)
    cpb, nblk = R // C, S // R
    d = 1 if reverse else 0
    blk_map = (lambda b: nblk - 1 - b) if reverse else (lambda b: b)

    W = HG_HPS * HG_DK

    def body(q_ref, i_ref, f_ref, lb_ref, o_ref, st_ref, s_scr):
        @pl.when(pl.program_id(1) == 0)
        def _():
            s_scr[...] = jnp.zeros_like(s_scr)

        l = lb_ref[...]
        lbs = _sigmoid(l[2 * d:2 * d + 1, :] - l[2 * d + 1:2 * d + 2, :])
        keep = _gla_masks(reverse)
        heads = list(range(HG_HPS))
        css = [pl.ds(hh * HG_DK, HG_DK) for hh in heads]
        lb = [lbs[:, hh * HG_DK:(hh + 1) * HG_DK] for hh in heads]
        for cc in range(cpb):
            c = cpb - 1 - cc if reverse else cc
            sl = pl.ds(c * C, C)
            v = [i_ref[sl, cs] for cs in css]
            t = _gla_chunk_fwd([q_ref[sl, cs] for cs in css], [f_ref[sl, cs] for cs in css], lb, keep, reverse)
            st = [s_scr[hh] for hh in heads]
            o = _m(lambda qs, s_, a, v_: _nt(qs, s_) + _nn(a, v_), t["qs"], st, t["a"], v)
            new = _m(lambda e_, s_, v_, k2: e_ * s_ + _tn(v_, k2), t["e_end"], st, v, t["k2"])
            for hh in heads:
                st_ref[c, hh] = st[hh]
                o_ref[sl, css[hh]] = o[hh]
                s_scr[hh] = new[hh]

    col = lambda off: (lambda h, b: (blk_map(b), off // W + h))
    return pl.pallas_call(
        body,
        name=name,
        grid=(HG_HEADS // HG_HPS, nblk),
        in_specs=[
            pl.BlockSpec((R, W), col(OFF_Q)),
            pl.BlockSpec((R, W), col(OFF_I)),
            pl.BlockSpec((R, W), col(f_off)),
            pl.BlockSpec((4, W), lambda h, b: (0, h)),
        ],
        out_specs=[
            pl.BlockSpec((R, W), lambda h, b: (blk_map(b), h)),
            pl.BlockSpec((cpb, HG_HPS, HG_DK, HG_DK), lambda h, b: (blk_map(b), h, 0, 0)),
        ],
        out_shape=[
            jax.ShapeDtypeStruct((S, HG_W), F32),
            jax.ShapeDtypeStruct((S // C, HG_HEADS, HG_DK, HG_DK), F32),
        ],
        scratch_shapes=[pltpu.VMEM((HG_HPS, HG_DK, HG_DK), F32)],
        compiler_params=_cparams(2),
    )(proj, proj, proj, lbl4)


def _gla_bwd(proj, lbl4, f_off, reverse, do, states, dproj, prev, name):
    S = proj.shape[0]
    PW = proj.shape[1]
    C = HG_CHUNK
    R = _rtile(S, 512)
    cpb, nblk = R // C, S // R
    d = 1 if reverse else 0
    blk_map = (lambda b: b) if reverse else (lambda b: nblk - 1 - b)
    final = prev is not None

    if final:
        assert HG_HPS == HG_HEADS and (OFF_Q, OFF_I, f_off) == (0, HG_W, 2 * HG_W)

    def body(*refs):
        if final:
            q_ref, i_ref, f_ref, lb_ref, do_ref, st_ref, pq_ref, pi_ref, _dp, o3_ref, dl_ref, ds_scr = refs
            dq_ref = di_ref = df_ref = o3_ref
        else:
            q_ref, i_ref, f_ref, lb_ref, do_ref, st_ref, dq_ref, di_ref, df_ref, dl_ref, ds_scr = refs
        out_off = (OFF_Q, OFF_I, f_off) if final else (0, 0, 0)
        blk = pl.program_id(1)

        @pl.when(blk == 0)
        def _():
            ds_scr[...] = jnp.zeros_like(ds_scr)
            dl_ref[...] = jnp.zeros_like(dl_ref)

        l = lb_ref[...]
        lbs = _sigmoid(l[2 * d:2 * d + 1, :] - l[2 * d + 1:2 * d + 2, :])
        keep = _gla_masks(reverse)
        keep_t = _gla_masks(not reverse)
        tri_t = jnp.where(keep_t, 1.0, 0.0).astype(CDT)
        end = 0 if reverse else C - 1
        is_end = lax.broadcasted_iota(jnp.int32, (C, HG_DK), 0) == end
        dl_all = [jnp.zeros((SUB, HG_DK), F32) for _ in range(HG_HPS)]
        gsz = HG_BWD_GROUP
        for cc, heads in [(cc, list(range(g0, g0 + gsz))) for cc in range(cpb) for g0 in range(0, HG_HPS, gsz)]:
            css = [pl.ds(hh * HG_DK, HG_DK) for hh in heads]
            lb = [lbs[:, hh * HG_DK:(hh + 1) * HG_DK] for hh in heads]
            dl_acc = [dl_all[hh] for hh in heads]
            c = cc if reverse else cpb - 1 - cc
            sl = pl.ds(c * C, C)
            qraw = [q_ref[sl, cs] for cs in css]
            v = [i_ref[sl, cs] for cs in css]
            t = _gla_chunk_fwd(qraw, [f_ref[sl, cs] for cs in css], lb, keep, reverse)
            dob = [do_ref[sl, cs].astype(CDT) for cs in css]
            vb = _m(lambda x: x.astype(CDT), v)
            st = [st_ref[c, hh] for hh in heads]
            ds = [ds_scr[hh] for hh in heads]
            dsb = _m(lambda x: x.astype(CDT), ds)
            d_qs = _m(_nn, dob, st)
            d_a = _m(lambda x, y: jnp.where(keep, _nt(x, y), 0.0), dob, vb)
            d_qt = _m(lambda x, y: _dot3(_nn, x, y), d_a, t["kt"])
            d_kt = _m(lambda x, y: _dot3(_tn, x, y), d_a, t["qt"])
            d_v = _m(lambda a, x, k2, s_: _tn(a, x) + _nt(k2, s_), t["a"], dob, t["k2"], dsb)
            d_k2 = _m(_nn, vb, dsb)
            d_e = _m(lambda s_, x: jnp.sum(s_ * x, axis=0, keepdims=True), st, ds)
            new_ds = _m(lambda e_, x, y, qs: e_ * x + _tn(y, qs), t["e_end"], ds, dob, t["qs"])
            dq = _m(lambda a, ea, b_, eb_: a * ea + b_ * eb_, d_qt, t["eq"], d_qs, t["eb"])
            dk = _m(lambda a, ea, b_, eb_: a * ea + b_ * eb_, d_kt, t["ek"], d_k2, t["e2"])
            db_end = _m(lambda x, k_, e2, de, ee: jnp.sum(x * (k_ * e2), axis=0, keepdims=True) + de * ee,
                        d_k2, t["k"], t["e2"], d_e, t["e_end"])
            db = _m(lambda q_, dq_, k_, dk_, be: q_ * dq_ - k_ * dk_ + jnp.where(is_end, be, 0.0),
                    t["q"], dq, t["k"], dk, db_end)
            dg = _m(lambda x: _tri_matmul(tri_t, x), db)
            df = _m(lambda g_, f_, dk_: g_ / f_ - dk_, dg, t["f"], dk)
            dfraw = _m(lambda x, l_, s_: x * (1.0 - l_) * s_ * (1.0 - s_), df, lb, t["sg"])
            dl_acc = _m(lambda acc, x, s_: acc + _rowsum8(x * (1.0 - s_)), dl_acc, df, t["sg"])
            dqraw = _m(lambda x, s_, r: x * (s_ * (1.0 + r * (1.0 - s_))), dq, t["sq"], qraw)
            if final:
                dqraw = [x + pq_ref[sl, cs] for x, cs in zip(dqraw, css)]
                d_v = [x + pi_ref[sl, cs] for x, cs in zip(d_v, css)]
            for n, hh in enumerate(heads):
                dl_all[hh] = dl_acc[n]
                ds_scr[hh] = new_ds[n]
                for ref, off, val in zip((dq_ref, di_ref, df_ref), out_off, (dqraw[n], d_v[n], dfraw[n])):
                    ref[sl, pl.ds(off + hh * HG_DK, HG_DK)] = val.astype(ref.dtype)
        dl_ref[...] += jnp.concatenate(dl_all, axis=1) * (lbs * (1.0 - lbs))

    W = HG_HPS * HG_DK
    col = lambda off: (lambda h, b: (blk_map(b), off // W + h))
    blk = lambda: pl.BlockSpec((R, W), lambda h, b: (blk_map(b), h))
    ins = [proj, proj, proj, lbl4, do, states]
    in_specs = [
        pl.BlockSpec((R, W), col(OFF_Q)),
        pl.BlockSpec((R, W), col(OFF_I)),
        pl.BlockSpec((R, W), col(f_off)),
        pl.BlockSpec((4, W), lambda h, b: (0, h)),
        blk(),
        pl.BlockSpec((cpb, HG_HPS, HG_DK, HG_DK), lambda h, b: (blk_map(b), h, 0, 0)),
    ]
    dl_shape = jax.ShapeDtypeStruct((SUB, HG_W), F32)
    dl_spec = pl.BlockSpec((SUB, W), lambda h, b: (0, h))
    dp_shape = jax.ShapeDtypeStruct((S, PW), CDT)
    if final:
        ins += [prev[0], prev[1], dproj]
        in_specs += [blk(), blk(), pl.BlockSpec(memory_space=pl.ANY)]
        out_shape = [dp_shape, dl_shape]
        out_specs = [pl.BlockSpec((R, 3 * HG_W), lambda h, b: (blk_map(b), 0)), dl_spec]
        aliases = {8: 0}
    else:
        out_shape = [jax.ShapeDtypeStruct((S, HG_W), F32), jax.ShapeDtypeStruct((S, HG_W), F32), dp_shape, dl_shape]
        out_specs = [blk(), blk(), pl.BlockSpec((R, W), col(f_off)), dl_spec]
        aliases = {}
        if dproj is not None:
            ins += [dproj]
            in_specs += [pl.BlockSpec(memory_space=pl.ANY)]
            aliases = {6: 2}
    if (not final) and dproj is not None:
        def body_wrapped(*refs, _b=body):
            _b(*refs[:6], *refs[7:])
        kern = body_wrapped
    else:
        kern = body
    res = pl.pallas_call(
        kern,
        name=name,
        grid=(HG_HEADS // HG_HPS, nblk),
        in_specs=in_specs,
        out_specs=out_specs,
        out_shape=out_shape,
        scratch_shapes=[pltpu.VMEM((HG_HPS, HG_DK, HG_DK), F32)],
        input_output_aliases=aliases,
        compiler_params=_cparams(2),
    )(*ins)
    if final:
        return res[0], None, None, res[1]
    dq, di, dproj, dl = res
    return dproj, dq, di, dl


def _hgrn_post_fwd(o_f, o_b, proj, norm_g):
    S = o_f.shape[0]

    def epi(accs, tiles, rows):
        of, ob, graw = tiles
        ng = rows[0][:, :HG_DK]
        o = of + ob
        ys = []
        for h in range(HG_HEADS):
            oh = o[:, h * HG_DK:(h + 1) * HG_DK]
            rs = lax.rsqrt(jnp.mean(oh * oh, axis=-1, keepdims=True) + RMS_EPS)
            ys.append(oh * rs * ng * _sigmoid(graw[:, h * HG_DK:(h + 1) * HG_DK]))
        return [jnp.concatenate(ys, axis=1)], []

    tm = _rtile(S, 512)
    (y,) = _fused_mm("hgrn_post_fwd", "nn", [], S, HG_W, 1, tm, HG_W, 1, [(HG_W, CDT, 0, None)], epi,
                     tiles=[(o_f, 0), (o_b, 0), (proj, OFF_G)], rows=[(jnp.tile(norm_g, (1, HG_HEADS)), 0)])
    return y


def _hgrn_post_bwd(dy, o_f, o_b, proj, norm_g, dproj):
    S = o_f.shape[0]

    def epi(accs, tiles, rows):
        dyv, of, ob, graw = tiles
        ng = rows[0][:, :HG_DK]
        o = of + ob
        dos, dgs, dns = [], [], []
        for h in range(HG_HEADS):
            sl = slice(h * HG_DK, (h + 1) * HG_DK)
            oh, gh, dyh = o[:, sl], graw[:, sl], dyv[:, sl].astype(F32)
            rs = lax.rsqrt(jnp.mean(oh * oh, axis=-1, keepdims=True) + RMS_EPS)
            xh = oh * rs
            sg = _sigmoid(gh)
            dn = dyh * sg
            dgs.append(dyh * (xh * ng) * sg * (1.0 - sg))
            dns.append(dn * xh)
            dxh = dn * ng
            dos.append(rs * (dxh - xh * jnp.mean(dxh * xh, axis=-1, keepdims=True)))
        return [jnp.concatenate(dos, axis=1), jnp.concatenate(dgs, axis=1)], [jnp.concatenate(dns, axis=1)]

    tm = _rtile(S, 512)
    do, dproj, dn = _fused_mm("hgrn_post_bwd", "nn", [], S, HG_W, 1, tm, HG_W, 1,
                              [(HG_W, F32, 0, None), (dproj.shape[1], CDT, OFF_G, dproj)], epi,
                              tiles=[(dy, 0), (o_f, 0), (o_b, 0), (proj, OFF_G)],
                              rows=[(jnp.tile(norm_g, (1, HG_HEADS)), 0)], n_racc=1)
    return do, dproj, dn


def _copy_into(name, src, dst, off):
    S, W = src.shape
    tm = _rtile(S, 512)
    (dst,) = _fused_mm(name, "nn", [], S, W, 1, tm, W, 1, [(dst.shape[1], dst.dtype, off, dst)],
                       lambda accs, tiles, rows: ([tiles[0]], []), tiles=[(src, 0)])
    return dst


def _rms_stats(x):
    rs = lax.rsqrt(jnp.mean(x * x, axis=-1, keepdims=True) + RMS_EPS)
    return x * rs, rs


def _mla_up(proj, cf, sf, g_cq, g_ckv, wuq_p, wukv):
    S = proj.shape[0]
    tm = _rtile(S, 512)
    H = MLA_HEADS

    def body(cq_ref, ckv_ref, kr_ref, krot_ref, cf_ref, sf_ref, gq_ref, gkv_ref, wq_ref, wkv_ref,
             q_ref, k_ref, v_ref, vt_ref, cqn_ref, ckvn_ref):
        cqn = (_rms_stats(cq_ref[...])[0] * gq_ref[...]).astype(CDT)
        ckvn = (_rms_stats(ckv_ref[...])[0] * gkv_ref[...]).astype(CDT)
        cqn_ref[...] = cqn
        ckvn_ref[...] = ckvn
        cfv, sfv = cf_ref[...], sf_ref[...]
        k_roped = (kr_ref[...] * cfv + krot_ref[...] * sfv).astype(CDT)
        ones = jnp.ones((VT_ROWS - LANE, tm), CDT)
        for h in range(H):
            r = _nn(cqn, wq_ref[h]) * MLA_QSCALE
            q_ref[h, :, 0:LANE] = r[:, 0:LANE].astype(CDT)
            q_ref[h, :, LANE:2 * LANE] = (r[:, LANE:2 * LANE] * cfv + r[:, 2 * LANE:3 * LANE] * sfv).astype(CDT)
            kv = _nn(ckvn, wkv_ref[h])
            k_ref[h, :, 0:LANE] = kv[:, 0:LANE].astype(CDT)
            k_ref[h, :, LANE:2 * LANE] = k_roped
            vv = kv[:, LANE:2 * LANE]
            v_ref[h] = vv.astype(CDT)
            vt_ref[h, 0, 0:LANE, :] = vv.T.astype(CDT)
            vt_ref[h, 0, LANE:VT_ROWS, :] = ones

    PWb = proj.shape[1]
    kr_off = PWb - KR_PAD
    cspec = lambda off, w: pl.BlockSpec((tm, w), lambda i, o=off // w: (i, o))
    return pl.pallas_call(
        body,
        name="mla_up_fwd",
        grid=(S // tm,),
        in_specs=[
            cspec(OFF_CQ, MLA_RANK), cspec(OFF_CKV, MLA_RANK), cspec(kr_off, LANE), cspec(kr_off + LANE, LANE),
            pl.BlockSpec((tm, LANE), lambda i: (i, 0)), pl.BlockSpec((tm, LANE), lambda i: (i, 0)),
            pl.BlockSpec((1, MLA_RANK), lambda i: (0, 0)), pl.BlockSpec((1, MLA_RANK), lambda i: (0, 0)),
            pl.BlockSpec((H, MLA_RANK, 3 * LANE), lambda i: (0, 0, 0)),
            pl.BlockSpec((H, MLA_RANK, 2 * LANE), lambda i: (0, 0, 0)),
        ],
        out_specs=[
            pl.BlockSpec((H, tm, 2 * LANE), lambda i: (0, i, 0)),
            pl.BlockSpec((H, tm, 2 * LANE), lambda i: (0, i, 0)),
            pl.BlockSpec((H, tm, LANE), lambda i: (0, i, 0)),
            pl.BlockSpec((H, 1, VT_ROWS, tm), lambda i: (0, i, 0, 0)),
            pl.BlockSpec((tm, MLA_RANK), lambda i: (i, 0)),
            pl.BlockSpec((tm, MLA_RANK), lambda i: (i, 0)),
        ],
        out_shape=[
            jax.ShapeDtypeStruct((H, S, 2 * LANE), CDT), jax.ShapeDtypeStruct((H, S, 2 * LANE), CDT),
            jax.ShapeDtypeStruct((H, S, LANE), CDT), jax.ShapeDtypeStruct((H, S // tm, VT_ROWS, tm), CDT),
            jax.ShapeDtypeStruct((S, MLA_RANK), CDT), jax.ShapeDtypeStruct((S, MLA_RANK), CDT),
        ],
        compiler_params=_cparams(1),
    )(proj, proj, proj, proj, cf, sf, g_cq, g_ckv, wuq_p, wukv)


def _mla_attn_fwd(q_cat, k_cat, vt, xchg=None):
    H, S, _ = q_cat.shape
    tq = _tile(S, MLA_TQ)
    _, nkb, _, tk = vt.shape
    nq = S // tq
    nx = xchg.n if xchg is not None else 0

    def body(*refs):
        q_ref, k_ref, vt_ref = refs[:3]
        x_in = refs[3:3 + nx]
        y_ref, ot_ref, lse_ref = refs[3 + nx:6 + nx]
        x_out = refs[6 + nx:6 + 2 * nx]
        m_scr, acc_scr = refs[6 + 2 * nx:8 + 2 * nx]
        x_sems = refs[8 + 2 * nx:]
        h, i = pl.program_id(0), pl.program_id(1)
        if nx:
            @pl.when((h == 0) & (i == 0))
            def _():
                xchg.start(x_in, x_out, x_sems)

        nsub = MLA_FWD_SLABS if tq % (MLA_FWD_SLABS * LANE) == 0 else 1
        ws = tq // nsub
        subs = [pl.ds(s * ws, ws) for s in range(nsub)]
        qs = [q_ref[0, sb, :] for sb in subs]
        m_scr[...] = jnp.full_like(m_scr, -jnp.inf)
        acc_scr[...] = jnp.zeros_like(acc_scr)

        def step(j, carry):
            kj = k_ref[0, pl.ds(pl.multiple_of(j * tk, tk), tk), :]
            vtj = vt_ref[0, j]
            sts = [_nt(kj, qq) for qq in qs]
            m_old = [m_scr[:, sb] for sb in subs]
            m_new = _m(lambda mo, st: jnp.maximum(mo, jnp.max(st, axis=0, keepdims=True)), m_old, sts)
            pts = _m(lambda st, mn: jnp.exp2(st - mn), sts, m_new)
            pvs = _m(lambda pt: _nn(vtj, pt), pts)
            for sb, mo, mn, pv in zip(subs, m_old, m_new, pvs):
                acc_scr[:, sb] = jnp.exp2(mo - mn) * acc_scr[:, sb] + pv
                m_scr[:, sb] = mn
            return carry

        lax.fori_loop(0, nkb, step, 0, unroll=4 if nkb % 4 == 0 else 1)
        l = acc_scr[LANE:LANE + 1, :]
        ot = acc_scr[0:LANE, :] / l
        ot_ref[0] = ot
        y_ref[...] = ot.T.astype(CDT)
        lse_ref[0, 0] = m_scr[...] + jnp.log2(l)

        if nx:
            @pl.when((h == H - 1) & (i == nq - 1))
            def _():
                xchg.wait(x_in, x_out, x_sems)

    return pl.pallas_call(
        body,
        name="mla_attn_fwd",
        grid=(H, nq),
        in_specs=[
            pl.BlockSpec((1, tq, 2 * LANE), lambda h, i: (h, i, 0)),
            pl.BlockSpec((1, S, 2 * LANE), lambda h, i: (h, 0, 0)),
            pl.BlockSpec((1, nkb, VT_ROWS, tk), lambda h, i: (h, 0, 0, 0)),
        ] + (xchg.specs if nx else []),
        out_specs=[
            pl.BlockSpec((tq, LANE), lambda h, i: (i, h)),
            pl.BlockSpec((1, LANE, tq), lambda h, i: (h, 0, i)),
            pl.BlockSpec((1, 1, 1, tq), lambda h, i: (h, i, 0, 0)),
        ] + (xchg.specs if nx else []),
        out_shape=[
            jax.ShapeDtypeStruct((S, H * LANE), CDT),
            jax.ShapeDtypeStruct((H, LANE, S), F32),
            jax.ShapeDtypeStruct((H, nq, 1, tq), F32),
        ] + (xchg.out_shape if nx else []),
        scratch_shapes=[pltpu.VMEM((1, tq), F32), pltpu.VMEM((VT_ROWS, tq), F32)] + (xchg.scratch if nx else []),
        compiler_params=_cparams(2, side_effects=nx > 0),
    )(q_cat, k_cat, vt, *(xchg.arrs if nx else []))


def _mla_delta(dy, ot):
    H, _, S = ot.shape
    tq = _tile(S, MLA_TQ)
    nq = S // tq

    def body(dy_ref, ot_ref, d_ref):
        d_ref[0, 0] = jnp.sum(dy_ref[...].astype(F32).T * ot_ref[0], axis=0, keepdims=True)

    return pl.pallas_call(
        body,
        name="mla_delta",
        grid=(H, nq),
        in_specs=[pl.BlockSpec((tq, LANE), lambda h, i: (i, h)), pl.BlockSpec((1, LANE, tq), lambda h, i: (h, 0, i))],
        out_specs=pl.BlockSpec((1, 1, 1, tq), lambda h, i: (h, i, 0, 0)),
        out_shape=jax.ShapeDtypeStruct((H, nq, 1, tq), F32),
        compiler_params=_cparams(2),
    )(dy, ot)


def _mla_attn_bwd(q_cat, k_cat, v, dy, lse, delta, xchg=None):
    H, S, _ = q_cat.shape
    _, nq, _, tq = lse.shape
    tk = _tile(S, 512)
    nkb = S // tk
    nx = xchg.n if xchg is not None else 0

    def body(*refs):
        k_ref, v_ref, q_ref, do_ref, lse_ref, dl_ref = refs[:6]
        x_in = refs[6:6 + nx]
        dk_ref, dv_ref, dq_ref = refs[6 + nx:9 + nx]
        x_out = refs[9 + nx:9 + 2 * nx]
        dk_scr, dv_scr = refs[9 + 2 * nx:11 + 2 * nx]
        x_sems = refs[11 + 2 * nx:]
        hd, ki = pl.program_id(0), pl.program_id(1)
        if nx:
            @pl.when((hd == 0) & (ki == 0))
            def _():
                xchg.start(x_in, x_out, x_sems)

        @pl.when(ki == 0)
        def _():
            dq_ref[...] = jnp.zeros_like(dq_ref)

        kb, vb = k_ref[0], v_ref[0]
        dk_scr[...] = jnp.zeros_like(dk_scr)
        dv_scr[...] = jnp.zeros_like(dv_scr)

        def step(i, carry):
            rows = pl.ds(pl.multiple_of(i * tq, tq), tq)
            qc = q_ref[0, rows, :]
            doc = do_ref[rows, :]
            pt = jnp.exp2(_nt(kb, qc) - lse_ref[0, i])
            dv_scr[...] += _nn(pt, doc)
            dst = (pt * (_nt(vb, doc) - dl_ref[0, i])).astype(CDT)
            dk_scr[...] += _nn(dst, qc)
            dq_ref[0, rows, :] += _tn(dst, kb)
            return carry

        lax.fori_loop(0, nq, step, 0, unroll=2 if nq % 2 == 0 else 1)
        dk_ref[0] = dk_scr[...] * (MLA_SCALE / MLA_QSCALE)
        dv_ref[0] = dv_scr[...]

        if nx:
            @pl.when((hd == H - 1) & (ki == nkb - 1))
            def _():
                xchg.wait(x_in, x_out, x_sems)

    return pl.pallas_call(
        body,
        name="mla_attn_bwd",
        grid=(H, nkb),
        in_specs=[
            pl.BlockSpec((1, tk, 2 * LANE), lambda h, j: (h, j, 0)),
            pl.BlockSpec((1, tk, LANE), lambda h, j: (h, j, 0)),
            pl.BlockSpec((1, S, 2 * LANE), lambda h, j: (h, 0, 0)),
            pl.BlockSpec((S, LANE), lambda h, j: (0, h)),
            pl.BlockSpec((1, nq, 1, tq), lambda h, j: (h, 0, 0, 0)),
            pl.BlockSpec((1, nq, 1, tq), lambda h, j: (h, 0, 0, 0)),
        ] + (xchg.specs if nx else []),
        out_specs=[
            pl.BlockSpec((1, tk, 2 * LANE), lambda h, j: (h, j, 0)),
            pl.BlockSpec((1, tk, LANE), lambda h, j: (h, j, 0)),
            pl.BlockSpec((1, S, 2 * LANE), lambda h, j: (h, 0, 0)),
        ] + (xchg.specs if nx else []),
        out_shape=[
            jax.ShapeDtypeStruct((H, S, 2 * LANE), F32),
            jax.ShapeDtypeStruct((H, S, LANE), F32),
            jax.ShapeDtypeStruct((H, S, 2 * LANE), F32),
        ] + (xchg.out_shape if nx else []),
        scratch_shapes=[pltpu.VMEM((tk, 2 * LANE), F32), pltpu.VMEM((tk, LANE), F32)] + (xchg.scratch if nx else []),
        compiler_params=_cparams(2, side_effects=nx > 0),
    )(k_cat, v, q_cat, dy, lse, delta, *(xchg.arrs if nx else []))


def _mla_up_bwd(dq_cat, dk_cat, dv, proj, cf, sf, g_cq, g_ckv, wuq_p, wukv, dproj):
    H, S, _ = dq_cat.shape
    tm = _rtile(S, 256)
    PW = proj.shape[1]
    kr_off = PW - KR_PAD

    assert OFF_CKV == OFF_CQ + MLA_RANK and OFF_CQ % (2 * MLA_RANK) == 0

    def body(dq_ref, dk_ref, dv_ref, cq_ref, ckv_ref, cf_ref, sf_ref, gq_ref, gkv_ref, wq_ref, wkv_ref, _dp,
             dqp_ref, dkvp_ref, dc_ref, dkr_ref, dgq_ref, dgkv_ref):
        i = pl.program_id(0)
        dcq_ref, dckv_ref = dc_ref.at[:, 0:MLA_RANK], dc_ref.at[:, MLA_RANK:2 * MLA_RANK]
        cfv, sfv = cf_ref[...], sf_ref[...]
        aq = jnp.zeros((tm, MLA_RANK), F32)
        akv = jnp.zeros((tm, MLA_RANK), F32)
        akr = jnp.zeros((tm, LANE), F32)
        for h in range(H):
            dq = dq_ref[h] * MLA_SCALE
            dqr = dq[:, LANE:2 * LANE]
            dqp = jnp.concatenate([dq[:, 0:LANE], dqr * cfv, dqr * sfv], axis=1).astype(CDT)
            dqp_ref[:, pl.ds(h * 3 * LANE, 3 * LANE)] = dqp
            aq = aq + _nt(dqp, wq_ref[h])
            dk = dk_ref[h]
            dkvp = jnp.concatenate([dk[:, 0:LANE], dv_ref[h]], axis=1).astype(CDT)
            dkvp_ref[:, pl.ds(h * 2 * LANE, 2 * LANE)] = dkvp
            akv = akv + _nt(dkvp, wkv_ref[h])
            akr = akr + dk[:, LANE:2 * LANE]

        def rms_bwd(c_ref, g_ref, dn, d_ref, dg_ref):
            xh, rs = _rms_stats(c_ref[...])
            dxh = dn * g_ref[...]
            d_ref[...] = (rs * (dxh - xh * jnp.mean(dxh * xh, axis=-1, keepdims=True))).astype(d_ref.dtype)
            part = _rowsum8(dn * xh)

            @pl.when(i == 0)
            def _():
                dg_ref[...] = part

            @pl.when(i > 0)
            def _():
                dg_ref[...] += part

        rms_bwd(cq_ref, gq_ref, aq, dcq_ref, dgq_ref)
        rms_bwd(ckv_ref, gkv_ref, akv, dckv_ref, dgkv_ref)
        dkr_ref[...] = jnp.concatenate([akr * cfv, akr * sfv, jnp.zeros((tm, KR_PAD - 2 * LANE), F32)], axis=1).astype(dkr_ref.dtype)

    cspec = lambda off, w: pl.BlockSpec((tm, w), lambda i, o=off // w: (i, o))
    hspec = lambda w: pl.BlockSpec((H, tm, w), lambda i: (0, i, 0))
    outs = pl.pallas_call(
        body,
        name="mla_up_bwd",
        grid=(S // tm,),
        in_specs=[
            hspec(2 * LANE), hspec(2 * LANE), hspec(LANE),
            cspec(OFF_CQ, MLA_RANK), cspec(OFF_CKV, MLA_RANK),
            pl.BlockSpec((tm, LANE), lambda i: (i, 0)), pl.BlockSpec((tm, LANE), lambda i: (i, 0)),
            pl.BlockSpec((1, MLA_RANK), lambda i: (0, 0)), pl.BlockSpec((1, MLA_RANK), lambda i: (0, 0)),
            pl.BlockSpec((H, MLA_RANK, 3 * LANE), lambda i: (0, 0, 0)),
            pl.BlockSpec((H, MLA_RANK, 2 * LANE), lambda i: (0, 0, 0)),
            pl.BlockSpec(memory_space=pl.ANY),
        ],
        out_specs=[
            pl.BlockSpec((tm, H * 3 * LANE), lambda i: (i, 0)), pl.BlockSpec((tm, H * 2 * LANE), lambda i: (i, 0)),
            pl.BlockSpec((tm, 2 * MLA_RANK), lambda i: (i, OFF_CQ // (2 * MLA_RANK))),
            pl.BlockSpec((tm, KR_PAD), lambda i: (i, 0)),
            pl.BlockSpec((SUB, MLA_RANK), lambda i: (0, 0)),
            pl.BlockSpec((SUB, MLA_RANK), lambda i: (0, 0)),
        ],
        out_shape=[
            jax.ShapeDtypeStruct((S, H * 3 * LANE), CDT), jax.ShapeDtypeStruct((S, H * 2 * LANE), CDT),
            jax.ShapeDtypeStruct(dproj.shape, dproj.dtype),
            jax.ShapeDtypeStruct((S, KR_PAD), CDT),
            jax.ShapeDtypeStruct((SUB, MLA_RANK), F32), jax.ShapeDtypeStruct((SUB, MLA_RANK), F32),
        ],
        input_output_aliases={11: 2},
        compiler_params=_cparams(1),
    )(dq_cat, dk_cat, dv, proj, proj, cf, sf, g_cq, g_ckv, wuq_p, wukv, dproj)
    dqp, dkvp, dproj, dkr, dgq, dgkv = outs
    dproj = _copy_into("dproj_kr", dkr, dproj, kr_off)
    return dproj, dqp, dkvp, dgq, dgkv


def _mem_softmax(q, k):
    s = _nt(q, k) * (MEM_HD ** -0.5)
    p = jnp.exp(s - jnp.max(s, axis=1, keepdims=True))
    return p / jnp.sum(p, axis=1, keepdims=True)


def _mem_attn_fwd(proj, memkv):
    S = proj.shape[0]
    Mm = memkv.shape[0]
    tm = _rtile(S, 1024)

    def body(q_ref, k_ref, v_ref, y_ref):
        pn = _mem_softmax(q_ref[...], k_ref[...])
        y_ref[...] = _nn(pn, v_ref[...]).astype(y_ref.dtype)

    return pl.pallas_call(
        body,
        name="mem_attn_fwd",
        grid=(S // tm, MEM_HEADS),
        in_specs=[
            pl.BlockSpec((tm, MEM_HD), lambda i, h: (i, OFF_QM // MEM_HD + h)),
            pl.BlockSpec((Mm, MEM_HD), lambda i, h: (0, h)),
            pl.BlockSpec((Mm, MEM_HD), lambda i, h: (0, MEM_HEADS + h)),
        ],
        out_specs=pl.BlockSpec((tm, MEM_HD), lambda i, h: (i, h)),
        out_shape=jax.ShapeDtypeStruct((S, MEM_W), CDT),
        compiler_params=_cparams(2),
    )(proj, memkv, memkv)


def _mem_attn_bwd(dy, proj, memkv, dproj):
    S = proj.shape[0]
    Mm = memkv.shape[0]
    tm = _rtile(S, 1024)
    scale = MEM_HD ** -0.5

    def body(dy_ref, q_ref, k_ref, v_ref, _dp, dq_ref, dk_ref, dv_ref):
        i = pl.program_id(1)
        q, k, dyv = q_ref[...].astype(CDT), k_ref[...], dy_ref[...]
        pn = _mem_softmax(q, k)
        dvp = _tn(pn, dyv)
        dp = _nt(dyv, v_ref[...])
        ds = pn * (dp - jnp.sum(dp * pn, axis=1, keepdims=True)) * scale
        dq_ref[...] = _nn(ds, k).astype(dq_ref.dtype)
        dkp = _tn(ds, q)

        @pl.when(i == 0)
        def _():
            dk_ref[...] = dkp
            dv_ref[...] = dvp

        @pl.when(i > 0)
        def _():
            dk_ref[...] += dkp
            dv_ref[...] += dvp

    dproj, dk, dv = pl.pallas_call(
        body,
        name="mem_attn_bwd",
        grid=(MEM_HEADS, S // tm),
        in_specs=[
            pl.BlockSpec((tm, MEM_HD), lambda h, i: (i, h)),
            pl.BlockSpec((tm, MEM_HD), lambda h, i: (i, OFF_QM // MEM_HD + h)),
            pl.BlockSpec((Mm, MEM_HD), lambda h, i: (0, h)),
            pl.BlockSpec((Mm, MEM_HD), lambda h, i: (0, MEM_HEADS + h)),
            pl.BlockSpec(memory_space=pl.ANY),
        ],
        out_specs=[
            pl.BlockSpec((tm, MEM_HD), lambda h, i: (i, OFF_QM // MEM_HD + h)),
            pl.BlockSpec((Mm, MEM_HD), lambda h, i: (0, h)),
            pl.BlockSpec((Mm, MEM_HD), lambda h, i: (0, h)),
        ],
        out_shape=[
            jax.ShapeDtypeStruct(dproj.shape, dproj.dtype),
            jax.ShapeDtypeStruct((Mm, MEM_W), F32),
            jax.ShapeDtypeStruct((Mm, MEM_W), F32),
        ],
        input_output_aliases={4: 0},
        compiler_params=_cparams(2),
    )(dy, proj, memkv, memkv, dproj)
    return dproj, dk, dv


def _small_allreduce(vec):
    NS = vec.shape[1]

    def body(v_ref, o_ref, gbuf, send, recv):
        x, y, c, me = _my_place()
        gbuf[me] = v_ref[...]
        copies = []
        for kk in range(1, N_DEV):
            peer, _ = _peer(x, y, c, kk)
            cp = pltpu.make_async_remote_copy(src_ref=v_ref, dst_ref=gbuf.at[me], send_sem=send.at[kk - 1],
                                              recv_sem=recv.at[kk - 1], device_id=peer, device_id_type=MESH)
            cp.start()
            copies.append(cp)
        for cp in copies:
            cp.wait()
        tot = gbuf[0]
        for d in range(1, N_DEV):
            tot = tot + gbuf[d]
        o_ref[...] = jnp.sum(tot, axis=0, keepdims=True)

    return pl.pallas_call(
        body,
        name="small_allreduce",
        in_specs=[pl.BlockSpec(memory_space=pltpu.VMEM)],
        out_specs=pl.BlockSpec(memory_space=pltpu.VMEM),
        out_shape=jax.ShapeDtypeStruct((1, NS), F32),
        scratch_shapes=[pltpu.VMEM((N_DEV, SUB, NS), F32), pltpu.SemaphoreType.DMA((N_DEV - 1,)),
                        pltpu.SemaphoreType.DMA((N_DEV - 1,))],
        compiler_params=pltpu.CompilerParams(has_side_effects=True, vmem_limit_bytes=V7X_VMEM_LIMIT),
    )(vec)


def _adamw_math(g, w, m, v):
    nm = ADAM_B1 * m + (1.0 - ADAM_B1) * g
    nv = ADAM_B2 * v + (1.0 - ADAM_B2) * (g * g)
    mh = nm / (1.0 - ADAM_B1 ** ADAM_STEP)
    vh = nv / (1.0 - ADAM_B2 ** ADAM_STEP)
    delta = -ADAM_LR * (mh / (jnp.sqrt(vh) + ADAM_EPS) + ADAM_WD * w)
    return delta, nm, nv


def _adam_big(name, recv, w, m, v):
    _, R, C = w.shape
    tr = _rtile(R, max(SUB, (ADAM_BLOCK_ELEMS // C) // SUB * SUB))

    def body(r_ref, w_ref, m_ref, v_ref, g_ref, d_ref, nm_ref, nv_ref):
        g = r_ref[0].astype(F32)
        for d in range(1, N_DEV):
            g = g + r_ref[d].astype(F32)
        delta, nm, nv = _adamw_math(g, w_ref[0], m_ref[0], v_ref[0])
        g_ref[0] = g
        d_ref[0] = delta
        nm_ref[0] = nm
        nv_ref[0] = nv

    blk = pl.BlockSpec((1, tr, C), lambda i: (0, i, 0))
    return pl.pallas_call(
        body,
        name=name,
        grid=(R // tr,),
        in_specs=[pl.BlockSpec((N_DEV, tr, C), lambda i: (0, i, 0)), blk, blk, blk],
        out_specs=[blk, blk, blk, blk],
        out_shape=[jax.ShapeDtypeStruct((1, R, C), F32)] * 4,
        compiler_params=_cparams(1),
    )(recv, w, m, v)


def _to_bf16(name, w):
    _, R, C = w.shape
    tr = _rtile(R, max(SUB, (ADAM_BLOCK_ELEMS // C) // SUB * SUB))

    def body(w_ref, o_ref):
        o_ref[...] = w_ref[0].astype(CDT)

    return pl.pallas_call(
        body,
        name=name,
        grid=(R // tr,),
        in_specs=[pl.BlockSpec((1, tr, C), lambda i: (0, i, 0))],
        out_specs=pl.BlockSpec((tr, C), lambda i: (i, 0)),
        out_shape=jax.ShapeDtypeStruct((R, C), CDT),
        compiler_params=_cparams(1),
    )(w)


def _adam_small(g, w, m, v):
    def body(g_ref, w_ref, m_ref, v_ref, d_ref, nm_ref, nv_ref):
        delta, nm, nv = _adamw_math(g_ref[...], w_ref[...], m_ref[...], v_ref[...])
        d_ref[...] = delta
        nm_ref[...] = nm
        nv_ref[...] = nv

    return pl.pallas_call(body, name="adam_small", out_shape=[jax.ShapeDtypeStruct(g.shape, F32)] * 3)(g, w, m, v)


def _rot(w, axis=-1):
    x1, x2 = jnp.split(w, 2, axis=axis)
    return jnp.concatenate([-x2, x1], axis=axis)


def _unrot(dw, axis=-1):
    d1, d2 = jnp.split(dw, 2, axis=axis)
    return jnp.concatenate([d2, -d1], axis=axis)


def _pad_cols(w, width):
    return jnp.pad(w, [(0, 0)] * (w.ndim - 1) + [(0, width - w.shape[-1])])


def kernel(x, mem, positions, ln_emb_g, ln_emb_b, hgrn_lb_logits, w_in, hgrn_norm_g, mla_g_cq, mla_g_ckv, mla_w_uq, mla_w_ukv, mem_w_kv, w_branch, w_o, ln1_g, ln1_b, w_ffn_gate, w_ffn_up, w_ffn_down, ln2_g, ln2_b, loss_target, m_ln_emb_g, m_ln_emb_b, m_hgrn_lb_logits, m_w_in, m_hgrn_norm_g, m_mla_g_cq, m_mla_g_ckv, m_mla_w_uq, m_mla_w_ukv, m_mem_w_kv, m_w_branch, m_w_o, m_ln1_g, m_ln1_b, m_w_ffn_gate, m_w_ffn_up, m_w_ffn_down, m_ln2_g, m_ln2_b, v_ln_emb_g, v_ln_emb_b, v_hgrn_lb_logits, v_w_in, v_hgrn_norm_g, v_mla_g_cq, v_mla_g_ckv, v_mla_w_uq, v_mla_w_ukv, v_mem_w_kv, v_w_branch, v_w_o, v_ln1_g, v_ln1_b, v_w_ffn_gate, v_w_ffn_up, v_w_ffn_down, v_ln2_g, v_ln2_b):
    x2, tgt = x[0], loss_target[0]
    S, D = x2.shape
    Mm = mem.shape[1]
    F = w_ffn_gate.shape[2] * N_DEV
    GW = 3 * D
    PW = OFF_GATE + GW + KR_PAD
    KR = OFF_GATE + GW
    NIN = w_in.shape[2] * N_DEV
    assert NIN == OFF_GATE + MLA_ROPE + GW
    _, _, _, me = _my_place()
    row = lambda a: a.reshape(1, -1)

    br3 = lambda a: a.reshape(1, 3 * BR_W, -1)
    tp = lambda a: jnp.swapaxes(a, 1, 2)
    big_w = [tp(w_in), mla_w_uq, mla_w_ukv, mem_w_kv, br3(w_branch), w_o, tp(w_ffn_gate), tp(w_ffn_up), w_ffn_down]
    big_m = [tp(m_w_in), m_mla_w_uq, m_mla_w_ukv, m_mem_w_kv, br3(m_w_branch), m_w_o, tp(m_w_ffn_gate), tp(m_w_ffn_up),
             m_w_ffn_down]
    big_v = [tp(v_w_in), v_mla_w_uq, v_mla_w_ukv, v_mem_w_kv, br3(v_w_branch), v_w_o, tp(v_w_ffn_gate), tp(v_w_ffn_up),
             v_w_ffn_down]
    transposed = (0, 6, 7)
    wnames = ["w_in", "w_uq", "w_ukv", "mem_w_kv", "w_branch", "w_o", "w_gate", "w_up", "w_down"]
    big_wb = [_to_bf16("bf16_" + nme, w) for nme, w in zip(wnames, big_w)]
    g_in, g_lb = _all_gather_two_level("weights_all_gather", [big_wb[0], hgrn_lb_logits.reshape(4, -1)])
    win_t = g_in.reshape(NIN, D)
    kr_w = win_t[OFF_QM:OFF_QM + MLA_ROPE]
    zeros64 = jnp.zeros_like(kr_w)
    win_pt = jnp.concatenate([win_t[:OFF_FB], win_t[OFF_FF:OFF_G], win_t[OFF_FB:OFF_FF], win_t[OFF_G:OFF_QM],
                              win_t[OFF_QM + MLA_ROPE:], kr_w, zeros64, _rot(kr_w, 0), zeros64,
                              jnp.zeros((KR_PAD - 2 * LANE, D), CDT)], axis=0)
    lbl4 = jnp.transpose(g_lb, (1, 0, 2)).reshape(4, -1)

    half = MLA_ROPE // 2
    inv_freq = jnp.power(ROPE_THETA, -jnp.arange(half, dtype=F32) / half)
    ang = positions[0].astype(F32)[:, None] * inv_freq
    cf = _pad_cols(jnp.tile(jnp.cos(ang), (1, 2)), LANE)
    sf = _pad_cols(jnp.tile(jnp.sin(ang), (1, 2)), LANE)

    tm512 = _rtile(S, 512)
    ident = lambda accs, tiles, rows: ([accs[0]], [])

    def epi_ln0(accs, tiles, rows):
        h = _ln_stats(tiles[0])[0] * rows[0] + rows[1]
        return [h, h], []

    h0, h0b = _fused_mm("ln_emb_fwd", "nn", [], S, D, 1, tm512, D, 1, [(D, F32, 0, None), (D, CDT, 0, None)], epi_ln0,
                        tiles=[(x2, 0)], rows=[(row(ln_emb_g), 0), (row(ln_emb_b), 0)])
    proj, g_uq, g_ukv, g_mkv, g_wb, g_wo = _fused_mm(
        "proj", "nt", [[(h0b, 0, win_pt, 0)]], S, PW, D, _rtile(S, 1024), _tile(PW, 1536), D, [(PW, F32, 0, None)], ident,
        xchg=_Xchg(big_wb[1:6], False), msplit=2 if S % 2048 == 0 else 1)
    wuq_p =jnp.concatenate([g_uq[..., :MLA_NOPE], _pad_cols(g_uq[..., MLA_NOPE:], LANE),
                             _pad_cols(_rot(g_uq[..., MLA_NOPE:]), LANE)], axis=-1)
    wukv = g_ukv
    wmkv = g_mkv.reshape(-1, g_mkv.shape[-1])
    wb = jnp.transpose(g_wb.reshape(N_DEV, 3, BR_W, -1), (1, 2, 0, 3)).reshape(3, BR_W, D)
    wo = g_wo.reshape(-1, D)
    o_f, st_f = _gla_fwd(proj, lbl4, OFF_FF, False, "gla_fwd_f")
    o_b, st_b = _gla_fwd(proj, lbl4, OFF_FB, True, "gla_fwd_b")
    y_hg = _hgrn_post_fwd(o_f, o_b, proj, hgrn_norm_g)
    q_cat, k_cat, v_mla, vt_mla, cqn, ckvn = _mla_up(proj, cf, sf, mla_g_cq, mla_g_ckv, wuq_p, wukv)
    y_mla, ot, lse, g_wg, g_wu, g_wd = _mla_attn_fwd(q_cat, k_cat, vt_mla, _Xchg(big_wb[6:9], False))
    wg_t, wu_t = g_wg.reshape(F, D), g_wu.reshape(F, D)
    wd = g_wd.reshape(-1, D)
    memb = mem[0].astype(CDT)
    (memkv,) = _fused_mm("mem_kv", "nn", [[(memb, 0, wmkv, 0)]], Mm, 2 * MEM_W, D, Mm, _tile(2 * MEM_W, 512), D,
                         [(2 * MEM_W, CDT, 0, None)], ident)
    y_mem = _mem_attn_fwd(proj, memkv)
    ys = [y_hg, y_mla, y_mem]
    tnD = _tile(D, 1024, OFF_GATE)

    def epi_branch(accs, tiles, rows):
        return [_sigmoid(tiles[0]) * accs[0] + _sigmoid(tiles[1]) * accs[1] + _sigmoid(tiles[2]) * accs[2]], []

    (merged,) = _fused_mm("branch_fwd", "nn", [[(ys[b], 0, wb[b], 0)] for b in range(3)], S, D, BR_W, tm512, tnD, BR_W,
                          [(D, CDT, 0, None)], epi_branch, tiles=[(proj, OFF_GATE + b * D) for b in range(3)])

    def epi_ln1(accs, tiles, rows):
        r1v = ALPHA * tiles[0] + accs[0]
        return [r1v, _ln_stats(r1v)[0] * rows[0] + rows[1]], []

    r1, h1b = _fused_mm("wo_ln1", "nn", [[(merged, 0, wo, 0)]], S, D, D, tm512, D, D,
                        [(D, F32, 0, None), (D, CDT, 0, None)], epi_ln1, tiles=[(h0, 0)], rows=[(ln1_g, 0), (ln1_b, 0)])
    tnF = _tile(F, 512)

    def epi_up(accs, tiles, rows):
        gp, up = accs
        return [gp, up, gp * _sigmoid(gp) * up], []

    tm1k, ms1k = _rtile(S, 1024), (2 if S % 2048 == 0 else 1)
    gpb, upb, act = _fused_mm("ffn_up", "nt", [[(h1b, 0, wg_t, 0)], [(h1b, 0, wu_t, 0)]], S, F, D, tm1k, tnF, D,
                              [(F, CDT, 0, None)] * 3, epi_up, msplit=ms1k)

    def epi_down(accs, tiles, rows):
        g1, b1, g2, b2 = rows
        h1 = _ln_stats(tiles[0])[0] * g1 + b1
        xh2, rstd2 = _ln_stats(ALPHA * h1 + accs[0])
        diff = xh2 * g2 + b2 - tiles[1]
        dh2 = diff * (1.0 / D)
        dr2v = _ln_bwd(dh2, xh2, rstd2, g2)
        return [dr2v, dr2v], [dh2 * xh2, dh2, diff * diff * (0.5 / D)]

    acc_first = lambda epi: (lambda accs, tiles, rows: epi([tiles[0]], tiles[1:], rows))
    tm256 = _rtile(S, 256)
    (ff,) = _fused_mm("ffn_down", "nn", [[(act, 0, wd, 0)]], S, D, F, tm1k, _tile(D, 512), F, [(D, F32, 0, None)], ident,
                      msplit=ms1k)
    dr2, dr2b, dg2, db2, lossp = _fused_mm(
        "ffn_ln2_loss", "nn", [], S, D, 1, tm256, D, 1, [(D, F32, 0, None), (D, CDT, 0, None)], acc_first(epi_down),
        tiles=[(ff, 0), (r1, 0), (tgt, 0)], rows=[(ln1_g, 0), (ln1_b, 0), (ln2_g, 0), (ln2_b, 0)], n_racc=3)

    def epi_dact(accs, tiles, rows):
        da, gp, up = accs[0], tiles[0].astype(F32), tiles[1].astype(F32)
        s = _sigmoid(gp)
        return [da * up * (s * (1.0 + gp * (1.0 - s))), da * (gp * s)], []

    dgp, dup = _fused_mm("ffn_dact", "nt", [[(dr2b, 0, wd, 0)]], S, F, D, tm1k, tnF, D, [(F, CDT, 0, None)] * 2,
                         epi_dact, tiles=[(gpb, 0), (upb, 0)], msplit=ms1k)
    tkS = _rtile(S, 2048)
    (d_wd,) = _fused_mm("dw_down", "tn", [[(act, 0, dr2b, 0)]], F, D, S, tnF, D, tkS, [(D, CDT, 0, None)], ident)
    d_wg_t, d_wu_t = _fused_mm("dw_gate_up", "tn", [[(dgp, 0, h1b, 0)], [(dup, 0, h1b, 0)]], F, D, S, tnF, _tile(D, 1024),
                               tkS, [(D, CDT, 0, None)] * 2, lambda accs, tiles, rows: (accs, []))

    def epi_dh1(accs, tiles, rows):
        dh1 = accs[0] + ALPHA * tiles[0]
        xh1, rstd1 = _ln_stats(tiles[1])
        dr1v = _ln_bwd(dh1, xh1, rstd1, rows[0])
        return [dr1v, dr1v], [dh1 * xh1, dh1]

    rows8 = lambda dw: dw.reshape(N_DEV, -1, dw.shape[-1])
    (dh1_acc,) = _fused_mm("dh1", "nn", [[(dgp, 0, wg_t, 0), (dup, 0, wu_t, 0)]], S, D, F, tm512, _tile(D, 512), F,
                           [(D, F32, 0, None)], ident)
    dr1, dr1b, dg1, db1 = _fused_mm(
        "dh1_ln1", "nn", [], S, D, 1, tm256, D, 1, [(D, F32, 0, None), (D, CDT, 0, None)], acc_first(epi_dh1),
        tiles=[(dh1_acc, 0), (dr2, 0), (r1, 0)], rows=[(ln1_g, 0)], n_racc=2)
    (dmerged,) = _fused_mm("dmerged", "nt", [[(dr1b, 0, wo, 0)]], S, D, D, tm512, D, D, [(D, CDT, 0, None)], ident)
    (d_wo,) = _fused_mm("dw_o", "tn", [[(merged, 0, dr1b, 0)]], D, D, S, _tile(D, 512), D, tkS, [(D, CDT, 0, None)], ident)

    def epi_dbranch(accs, tiles, rows):
        dm, s = tiles[0].astype(F32), _sigmoid(tiles[1])
        return [dm * s, dm * accs[0] * s * (1.0 - s)], []

    dproj = None
    d_wbs, dys = [], []
    for b in range(3):
        du, dproj = _fused_mm(f"branch_bwd{b}", "nn", [[(ys[b], 0, wb[b], 0)]], S, D, BR_W, tm512, tnD, BR_W,
                              [(D, CDT, 0, None), (PW, CDT, OFF_GATE + b * D, dproj)], epi_dbranch,
                              tiles=[(dmerged, 0), (proj, OFF_GATE + b * D)])
        (dwb,) = _fused_mm(f"dw_branch{b}", "tn", [[(ys[b], 0, du, 0)]], BR_W, D, S, _tile(BR_W, 512), D, tkS,
                           [(D, CDT, 0, None)], ident)
        (dyb,) = _fused_mm(f"dy_branch{b}", "nt", [[(du, 0, wb[b], 0)]], S, BR_W, D, tm512, BR_W, D,
                           [(BR_W, F32 if b == 0 else CDT, 0, None)], ident)
        d_wbs.append(dwb)
        dys.append(dyb)
    dy_hg, dy_mla, dy_mem = dys

    dproj, dk_mem, dv_mem = _mem_attn_bwd(dy_mem, proj, memkv, dproj)
    dkv_mem = jnp.concatenate([dk_mem, dv_mem], axis=1).astype(CDT)
    (d_wmkv,) = _fused_mm("dw_memkv", "tn", [[(memb, 0, dkv_mem, 0)]], D, 2 * MEM_W, Mm, _tile(D, 512), 2 * MEM_W, Mm,
                          [(2 * MEM_W, CDT, 0, None)], ident)

    delta = _mla_delta(dy_mla, ot)
    dk_cat, dv_h, dq_cat, r_wg, r_wu, r_wd = _mla_attn_bwd(
        q_cat, k_cat, v_mla, dy_mla, lse, delta, _Xchg([rows8(d_wg_t), rows8(d_wu_t), rows8(d_wd)], True))
    dproj, dqp, dkvp, dgq, dgkv = _mla_up_bwd(dq_cat, dk_cat, dv_h, proj, cf, sf, mla_g_cq, mla_g_ckv, wuq_p, wukv, dproj)
    heads_major = lambda dw: jnp.transpose(dw.reshape(MLA_RANK, MLA_HEADS, -1), (1, 0, 2))
    (d_wuq_all,) = _fused_mm("dw_uq", "tn", [[(cqn, 0, dqp, 0)]], MLA_RANK, dqp.shape[1], S, MLA_RANK,
                             _tile(dqp.shape[1], 1536), tkS, [(dqp.shape[1], F32, 0, None)], ident)
    (d_wukv_all,) = _fused_mm("dw_ukv", "tn", [[(ckvn, 0, dkvp, 0)]], MLA_RANK, dkvp.shape[1], S, MLA_RANK,
                              _tile(dkvp.shape[1], 1024), tkS, [(dkvp.shape[1], CDT, 0, None)], ident)
    d_wuq_p, d_wukv = heads_major(d_wuq_all), heads_major(d_wukv_all)
    d_wuq = jnp.concatenate([d_wuq_p[..., :MLA_NOPE],
                             d_wuq_p[..., LANE:LANE + MLA_ROPE] + _unrot(d_wuq_p[..., 2 * LANE:2 * LANE + MLA_ROPE])],
                            axis=-1).astype(CDT)

    do_hg, dproj, dng = _hgrn_post_bwd(dy_hg, o_f, o_b, proj, hgrn_norm_g, dproj)
    dproj, dq1, di1, dl_f = _gla_bwd(proj, lbl4, OFF_FF, False, do_hg, st_f, dproj, None, "gla_bwd_f")
    dproj, _, _, dl_b = _gla_bwd(proj, lbl4, OFF_FB, True, do_hg, st_b, dproj, (dq1, di1), "gla_bwd_b")

    def epi_dh0(accs, tiles, rows):
        dh0 = accs[0] + ALPHA * tiles[0]
        xh, rstd = _ln_stats(tiles[1])
        return [_ln_bwd(dh0, xh, rstd, rows[0])], [dh0 * xh, dh0]

    d_wb = jnp.transpose(jnp.stack(d_wbs).reshape(3, BR_W, N_DEV, -1), (2, 0, 1, 3)).reshape(N_DEV, 3 * BR_W, -1)
    d_win_pt, r_uq, r_ukv, r_mkv, r_wb, r_wo = _fused_mm(
        "dw_in", "tn", [[(dproj, 0, h0b, 0)]], PW, D, S, _tile(PW, 1536), _tile(D, 1024), tkS, [(D, CDT, 0, None)], ident,
        xchg=_Xchg([d_wuq, d_wukv, rows8(d_wmkv), d_wb, rows8(d_wo)], True))
    d_kr = (d_win_pt[KR:KR + MLA_ROPE].astype(F32) + _unrot(d_win_pt[KR + LANE:KR + LANE + MLA_ROPE].astype(F32), 0)).astype(CDT)
    d_win_t = jnp.concatenate([d_win_pt[:OFF_FB], d_win_pt[OFF_FF:OFF_G], d_win_pt[OFF_FB:OFF_FF],
                               d_win_pt[OFF_G:OFF_QM], d_kr, d_win_pt[OFF_QM:KR]], axis=0)
    grad_x, dge, dbe, r_in = _fused_mm(
        "dh0_ln_emb", "nn", [[(dproj, 0, win_pt, 0)]], S, D, PW, tm512, D, _tile(PW, 1536), [(D, F32, 0, None)], epi_dh0,
        tiles=[(dr1, 0), (x2, 0)], rows=[(row(ln_emb_g), 0)], n_racc=2, xchg=_Xchg([rows8(d_win_t)], True))

    recv = [r_in, r_uq, r_ukv, r_mkv, r_wb, r_wo, r_wg, r_wu, r_wd]
    names = ["w_in", "w_uq", "w_ukv", "mem_w_kv", "w_branch", "w_o", "w_gate", "w_up", "w_down"]
    big_out = [_adam_big("adam_" + nme, r, w, m_, v_) for nme, r, w, m_, v_ in zip(names, recv, big_w, big_m, big_v)]

    parts = [dge, dbe, dng, dgq, dgkv, dg1, db1, dg2, db2, dl_f, dl_b, lossp]
    widths = [p.shape[1] for p in parts]
    red = _small_allreduce(jnp.concatenate(parts, axis=1))[0]
    offs = [sum(widths[:i]) for i in range(len(widths))]
    rs = [red[o:o + w_] for o, w_ in zip(offs, widths)]
    g_le_g, g_le_b, g_ng, g_gq, g_gkv, g_l1g, g_l1b, g_l2g, g_l2b, g_dlf, g_dlb, g_loss = rs
    loss = jnp.sum(g_loss)
    g_ng = g_ng.reshape(HG_HEADS, HG_DK).sum(axis=0)
    dl0 = jnp.stack([g_dlf, g_dlb])
    g_lb_full = jnp.stack([dl0, -dl0], axis=1)
    lbw = hgrn_lb_logits.shape[2]
    g_lb = lax.dynamic_slice_in_dim(g_lb_full, me * lbw, lbw, axis=2)

    small_g = [g_le_g, g_le_b, g_lb, g_ng.reshape(1, -1), g_gq.reshape(1, -1), g_gkv.reshape(1, -1), g_l1g.reshape(1, -1),
               g_l1b.reshape(1, -1), g_l2g.reshape(1, -1), g_l2b.reshape(1, -1)]
    small_w = [ln_emb_g, ln_emb_b, hgrn_lb_logits, hgrn_norm_g, mla_g_cq, mla_g_ckv, ln1_g, ln1_b, ln2_g, ln2_b]
    small_m = [m_ln_emb_g, m_ln_emb_b, m_hgrn_lb_logits, m_hgrn_norm_g, m_mla_g_cq, m_mla_g_ckv, m_ln1_g, m_ln1_b, m_ln2_g, m_ln2_b]
    small_v = [v_ln_emb_g, v_ln_emb_b, v_hgrn_lb_logits, v_hgrn_norm_g, v_mla_g_cq, v_mla_g_ckv, v_ln1_g, v_ln1_b, v_ln2_g, v_ln2_b]
    small_g = [g.reshape(w.shape) for g, w in zip(small_g, small_w)]
    pack = lambda lst: jnp.concatenate([a.reshape(-1) for a in lst]).reshape(1, -1)
    s_delta, s_nm, s_nv = _adam_small(pack(small_g), pack(small_w), pack(small_m), pack(small_v))
    sizes = [w.size for w in small_w]
    soffs = [sum(sizes[:i]) for i in range(len(sizes))]
    unpack = lambda p: [p[0, o:o + n].reshape(w.shape) for o, n, w in zip(soffs, sizes, small_w)]
    s_delta, s_nm, s_nv = unpack(s_delta), unpack(s_nm), unpack(s_nv)

    def ordered(small, big):
        sm = list(small)
        big = [tp(b) if n in transposed else b for n, b in enumerate(big)]
        bg = [b.reshape(w.shape) for b, w in zip(big, [w_in, mla_w_uq, mla_w_ukv, mem_w_kv, w_branch, w_o, w_ffn_gate, w_ffn_up, w_ffn_down])]
        return [sm[0], sm[1], sm[2], bg[0], sm[3], sm[4], sm[5], bg[1], bg[2], bg[3], bg[4], bg[5], sm[6], sm[7], bg[6], bg[7], bg[8], sm[8], sm[9]]

    grads = ordered(small_g, [o[0] for o in big_out])
    deltas = ordered(s_delta, [o[1] for o in big_out])
    new_m = ordered(s_nm, [o[2] for o in big_out])
    new_v = ordered(s_nv, [o[3] for o in big_out])
    return (loss, grad_x[None], *grads, *deltas, *new_m, *new_v)
```

```python
import functools

import jax
import jax.numpy as jnp
from jax import lax
from jax.experimental import pallas as pl
from jax.experimental.pallas import tpu as pltpu

F32 = jnp.float32
CDT = jnp.bfloat16
MESH = pl.DeviceIdType.MESH
N_DEV = 8
V7X_VMEM_LIMIT = 60 * 1024 * 1024
LANE = 128
SUB = 8

HG_HEADS, HG_DK, HG_CHUNK = 8, 128, 64
HG_HPS = 8
HG_BWD_GROUP = 8
HG_W = HG_HEADS * HG_DK
MLA_HEADS, MLA_RANK, MLA_NOPE, MLA_ROPE, MLA_V = 8, 512, 128, 64, 128
MLA_QK = MLA_NOPE + MLA_ROPE
MLA_SCALE = MLA_QK ** -0.5
MLA_QSCALE = MLA_SCALE * 1.4426950408889634
VT_ROWS = LANE + 16
MLA_TQ = 1024
MLA_TQ_FWD = 2048
MLA_FWD_SLABS = 8
MLA_W = MLA_HEADS * MLA_V
MEM_HEADS, MEM_HD = 4, 256
MEM_W = MEM_HEADS * MEM_HD
BR_W = 1024
ROPE_THETA = 10000.0
ALPHA = 2.0 ** 0.25
LN_EPS = 1e-5
RMS_EPS = 1e-6
ADAM_LR, ADAM_B1, ADAM_B2, ADAM_EPS, ADAM_WD, ADAM_STEP = 0.001, 0.9, 0.999, 1e-08, 0.01, 10
ADAM_BLOCK_ELEMS = 256 * 1024

OFF_Q, OFF_I, OFF_FB, OFF_FF, OFF_G = 0, 1024, 2048, 3072, 4096
OFF_CQ, OFF_CKV, OFF_QM, OFF_GATE = 5120, 5632, 6144, 7168
KR_PAD = 512


def _cparams(n_grid, side_effects=False):
    return pltpu.CompilerParams(dimension_semantics=("arbitrary",) * n_grid, vmem_limit_bytes=V7X_VMEM_LIMIT,
                                has_side_effects=side_effects)


def _tile(n, pref, *offsets):
    if n <= pref and all(o % n == 0 for o in offsets):
        return n
    t = (min(pref, n) // LANE) * LANE
    while t >= LANE:
        if n % t == 0 and all(o % t == 0 for o in offsets):
            return t
        t -= LANE
    raise ValueError(f"no tile for {n} {pref} {offsets}")


def _rtile(n, pref):
    if n <= pref:
        return n
    t = (pref // SUB) * SUB
    while t >= SUB:
        if n % t == 0:
            return t
        t -= SUB
    raise ValueError(f"no row tile for {n} {pref}")


def _dot(a, b, dims):
    return lax.dot_general(a.astype(CDT), b.astype(CDT), (dims, ((), ())), preferred_element_type=F32)


def _nn(a, b):
    return _dot(a, b, ((1,), (0,)))


def _nt(a, b):
    return _dot(a, b, ((1,), (1,)))


def _tn(a, b):
    return _dot(a, b, ((0,), (0,)))


_DOTS = {"nn": _nn, "nt": _nt, "tn": _tn}


def _sigmoid(x):
    return 1.0 / (1.0 + jnp.exp(-x))


def _rowsum8(v):
    r, w = v.shape
    return v.reshape(r // SUB, SUB, w).sum(axis=0)


def _my_place():
    x, y, c = lax.axis_index("x"), lax.axis_index("y"), lax.axis_index("c")
    return x, y, c, 4 * x + 2 * y + c


def _peer(x, y, c, kk):
    px = 1 - x if kk & 4 else x
    py = 1 - y if kk & 2 else y
    pc = 1 - c if kk & 1 else c
    return (px, py, pc), 4 * px + 2 * py + pc


class _Xchg:
    def __init__(self, arrs, scatter):
        self.arrs, self.scatter, self.n = list(arrs), scatter, len(arrs)
        hbm = pl.BlockSpec(memory_space=pl.ANY)
        self.specs = [hbm] * self.n
        self.out_shape = [jax.ShapeDtypeStruct(((N_DEV,) + a.shape[1:]) if scatter else ((N_DEV,) + a.shape), a.dtype)
                          for a in self.arrs]
        ncp = self.n * (N_DEV - 1)
        self.scratch = [pltpu.SemaphoreType.DMA((ncp,)), pltpu.SemaphoreType.DMA((ncp,)), pltpu.SemaphoreType.DMA((self.n,))]

    def _copies(self, ins, outs, send, recv, loc):
        x, y, c, me = _my_place()
        copies = []
        for w in range(self.n):
            copies.append(pltpu.make_async_copy(ins[w].at[me] if self.scatter else ins[w], outs[w].at[me], loc.at[w]))
            for kk in range(1, N_DEV):
                peer, pid = _peer(x, y, c, kk)
                s = w * (N_DEV - 1) + kk - 1
                copies.append(pltpu.make_async_remote_copy(
                    src_ref=ins[w].at[pid] if self.scatter else ins[w], dst_ref=outs[w].at[me],
                    send_sem=send.at[s], recv_sem=recv.at[s], device_id=peer, device_id_type=MESH))
        return copies

    def start(self, ins, outs, sems):
        for cp in self._copies(ins, outs, *sems):
            cp.start()

    def wait(self, ins, outs, sems):
        for cp in self._copies(ins, outs, *sems):
            cp.wait()


def _all_gather_two_level(name, arrs):
    n = len(arrs)
    NC = N_DEV - 1

    def body(*refs):
        ins, outs = refs[:n], refs[n:2 * n]
        send, recv, loc = refs[2 * n:]
        x, y, c, me = _my_place()
        sibling = (x, y, 1 - c)
        chips = [(1 - x, y), (x, 1 - y), (1 - x, 1 - y)]
        slot = lambda px, py, pc: 4 * px + 2 * py + pc

        def copy(w, k, block, to, src=None):
            dst = outs[w].at[slot(*block)]
            return pltpu.make_async_remote_copy(src_ref=dst if src is None else src, dst_ref=dst,
                                                send_sem=send.at[w * NC + k], recv_sem=recv.at[w * NC + k],
                                                device_id=to, device_id_type=MESH)

        mine = [pltpu.make_async_copy(ins[w], outs[w].at[me], loc.at[w]) for w in range(n)]
        for cp in mine:
            cp.start()
        first = []
        for w in range(n):
            first.append(copy(w, 0, (x, y, c), sibling, src=ins[w]))
            first += [copy(w, 1 + j, (x, y, c), (*chip, c), src=ins[w]) for j, chip in enumerate(chips)]
        for cp in first:
            cp.start()
        passed = []
        for j, chip in enumerate(chips):
            for w in range(n):
                copy(w, 1 + j, (*chip, c), (x, y, c)).wait_recv()
                fwd = copy(w, 4 + j, (*chip, c), sibling)
                fwd.start()
                passed.append(fwd)
        for w in range(n):
            copy(w, 0, sibling, (x, y, c)).wait_recv()
            for j, chip in enumerate(chips):
                copy(w, 4 + j, (*chip, 1 - c), (x, y, c)).wait_recv()
        for cp in first + passed:
            cp.wait_send()
        for cp in mine:
            cp.wait()

    hbm = pl.BlockSpec(memory_space=pl.ANY)
    return pl.pallas_call(
        body,
        name=name,
        in_specs=[hbm] * n,
        out_specs=[hbm] * n,
        out_shape=[jax.ShapeDtypeStruct((N_DEV,) + a.shape, a.dtype) for a in arrs],
        scratch_shapes=[pltpu.SemaphoreType.DMA((n * NC,)), pltpu.SemaphoreType.DMA((n * NC,)), pltpu.SemaphoreType.DMA((n,))],
        compiler_params=pltpu.CompilerParams(has_side_effects=True),
    )(*arrs)


def _fused_mm(name, mode, groups, M, N, K, tm, tn, tk, outs, epi, tiles=(), rows=(), n_racc=0, xchg=None, msplit=1):
    ni, nj, nk = M // tm, N // tn, K // tk
    assert M % tm == 0 and N % tn == 0 and K % tk == 0, (name, M, N, K, tm, tn, tk)
    assert n_racc == 0 or nj == 1
    assert msplit == 1 or (nk == 1 and n_racc == 0 and tm % (16 * msplit) == 0)
    dot = _DOTS[mode] if groups else None
    ins, in_specs = [], []
    for g in groups:
        for a, a_off, b, b_off in g:
            if mode == "tn":
                assert a_off % tm == 0
                in_specs.append(pl.BlockSpec((tk, tm), lambda i, j, k, o=a_off // tm: (k, i + o)))
            else:
                assert a_off % tk == 0
                in_specs.append(pl.BlockSpec((tm, tk), lambda i, j, k, o=a_off // tk: (i, k + o)))
            ins.append(a)
            if mode == "nt":
                assert b_off % tk == 0
                in_specs.append(pl.BlockSpec((tn, tk), lambda i, j, k, o=b_off // tk: (j, k + o)))
            else:
                assert b_off % tn == 0
                in_specs.append(pl.BlockSpec((tk, tn), lambda i, j, k, o=b_off // tn: (k, j + o)))
            ins.append(b)
    for arr, off in tiles:
        assert off % tn == 0
        ins.append(arr)
        in_specs.append(pl.BlockSpec((tm, tn), lambda i, j, k, o=off // tn: (i, j + o)))
    for arr, off in rows:
        assert off % tn == 0
        ins.append(arr)
        in_specs.append(pl.BlockSpec((1, tn), lambda i, j, k, o=off // tn: (0, j + o)))
    aliases = {}
    out_shape, out_specs = [], []
    for oi, (width, dtype, off, alias) in enumerate(outs):
        assert off % tn == 0
        if alias is not None:
            aliases[len(ins)] = oi
            ins.append(alias)
            in_specs.append(pl.BlockSpec(memory_space=pl.ANY))
        out_shape.append(jax.ShapeDtypeStruct((M, width), dtype))
        out_specs.append(pl.BlockSpec((tm, tn), lambda i, j, k, o=off // tn: (i, j + o)))
    for _ in range(n_racc):
        out_shape.append(jax.ShapeDtypeStruct((SUB, N), F32))
        out_specs.append(pl.BlockSpec((SUB, tn), lambda i, j, k: (0, 0)))
    n_alias = len(aliases)
    n_pairs = [len(g) for g in groups]
    use_scratch = nk > 1
    scratch = [pltpu.VMEM((tm, tn), F32) for _ in groups] if use_scratch else []
    nx = 0
    if xchg is not None:
        nx = xchg.n
        ins += xchg.arrs
        in_specs += xchg.specs
        out_shape += xchg.out_shape
        out_specs += xchg.specs
        scratch += xchg.scratch

    def body(*refs):
        it = iter(refs)
        pair_refs = [[(next(it), next(it)) for _ in range(n)] for n in n_pairs]
        tile_refs = [next(it) for _ in tiles]
        row_refs = [next(it) for _ in rows]
        for _ in range(n_alias):
            next(it)
        x_in = [next(it) for _ in range(nx)]
        out_refs = [next(it) for _ in outs]
        racc_refs = [next(it) for _ in range(n_racc)]
        x_out = [next(it) for _ in range(nx)]
        acc_refs = [next(it) for _ in groups] if use_scratch else []
        x_sems = list(it)
        i, j, k = pl.program_id(0), pl.program_id(1), pl.program_id(2)
        if nx:
            @pl.when((i == 0) & (j == 0) & (k == 0))
            def _():
                xchg.start(x_in, x_out, x_sems)

        def products():
            res = []
            for prs in pair_refs:
                s = None
                for a_ref, b_ref in prs:
                    d = dot(a_ref[...], b_ref[...])
                    s = d if s is None else s + d
                res.append(s)
            return res

        def finish(accs):
            out_v, racc_v = epi(accs, [t[...] for t in tile_refs], [r[...] for r in row_refs])
            for o_ref, v in zip(out_refs, out_v):
                o_ref[...] = v.astype(o_ref.dtype)
            for r_ref, v in zip(racc_refs, racc_v):
                part = _rowsum8(v)

                @pl.when(i == 0)
                def _():
                    r_ref[...] = part

                @pl.when(i > 0)
                def _():
                    r_ref[...] += part

        if not use_scratch and msplit > 1:
            ts = tm // msplit
            for s in range(msplit):
                rs = pl.ds(s * ts, ts)
                accs = []
                for prs in pair_refs:
                    acc = None
                    for a_ref, b_ref in prs:
                        dd = dot(a_ref[:, rs] if mode == "tn" else a_ref[rs, :], b_ref[...])
                        acc = dd if acc is None else acc + dd
                    accs.append(acc)
                out_v, _ = epi(accs, [t[rs, :] for t in tile_refs], [r[...] for r in row_refs])
                for o_ref, v in zip(out_refs, out_v):
                    o_ref[rs, :] = v.astype(o_ref.dtype)
        elif not use_scratch:
            finish(products())
        else:
            @pl.when(k == 0)
            def _():
                for acc in acc_refs:
                    acc[...] = jnp.zeros_like(acc)

            for acc, p in zip(acc_refs, products()):
                acc[...] += p

            @pl.when(k == nk - 1)
            def _():
                finish([acc[...] for acc in acc_refs])

        if nx:
            @pl.when((i == ni - 1) & (j == nj - 1) & (k == nk - 1))
            def _():
                xchg.wait(x_in, x_out, x_sems)

    res = pl.pallas_call(
        body,
        name=name,
        grid=(ni, nj, nk),
        in_specs=in_specs,
        out_specs=out_specs,
        out_shape=out_shape,
        scratch_shapes=scratch,
        input_output_aliases=aliases,
        compiler_params=_cparams(3, side_effects=nx > 0),
    )(*ins)
    return res


def _ln_stats(r):
    mu = jnp.mean(r, axis=-1, keepdims=True)
    xc = r - mu
    var = jnp.mean(xc * xc, axis=-1, keepdims=True)
    rstd = lax.rsqrt(var + LN_EPS)
    return xc * rstd, rstd


def _ln_bwd(dh, xhat, rstd, g):
    dxh = dh * g
    m1 = jnp.mean(dxh, axis=-1, keepdims=True)
    m2 = jnp.mean(dxh * xhat, axis=-1, keepdims=True)
    return rstd * (dxh - m1 - xhat * m2)


def _split3(x):
    hi = x.astype(CDT)
    r1 = x - hi.astype(F32)
    mid = r1.astype(CDT)
    lo = (r1 - mid.astype(F32)).astype(CDT)
    return hi, mid, lo


def _tri_matmul(tri, x):
    hi, mid, lo = _split3(x)
    return _nn(tri, hi) + _nn(tri, mid) + _nn(tri, lo)


def _dot3(dot, a, b):
    a_hi, b_hi = a.astype(CDT), b.astype(CDT)
    a_lo = (a - a_hi.astype(F32)).astype(CDT)
    b_lo = (b - b_hi.astype(F32)).astype(CDT)
    return dot(a_hi, b_hi) + dot(a_hi, b_lo) + dot(a_lo, b_hi)


def _gla_masks(reverse):
    C = HG_CHUNK
    r = lax.broadcasted_iota(jnp.int32, (C, C), 0)
    c = lax.broadcasted_iota(jnp.int32, (C, C), 1)
    keep = (c >= r) if reverse else (r >= c)
    return keep


def _m(fn, *lists):
    return [fn(*args) for args in zip(*lists)]


def _gla_chunk_fwd(qraw, fraw, lb, keep, reverse):
    C = HG_CHUNK
    end = 0 if reverse else C - 1
    tri = jnp.where(keep, 1.0, 0.0).astype(CDT)
    sq = _m(_sigmoid, qraw)
    q = _m(lambda x, s: x * s, qraw, sq)
    sg = _m(_sigmoid, fraw)
    f = _m(lambda l_, s: l_ + (1.0 - l_) * s, lb, sg)
    k = _m(lambda x: 1.0 - x, f)
    g = _m(jnp.log, f)
    b = _m(lambda x: _tri_matmul(tri, x), g)
    b_end = _m(lambda x: x[end:end + 1, :], b)
    b_mid = _m(lambda x: x[C // 2:C // 2 + 1, :], b)
    eq = _m(lambda x, m_: jnp.exp(x - m_), b, b_mid)
    ek = _m(lambda x, m_: jnp.exp(m_ - x), b, b_mid)
    eb = _m(jnp.exp, b)
    e2 = _m(lambda x, e_: jnp.exp(e_ - x), b, b_end)
    e_end = _m(jnp.exp, b_end)
    qt = _m(lambda x, e_: x * e_, q, eq)
    kt = _m(lambda x, e_: x * e_, k, ek)
    qs = _m(lambda x, e_: (x * e_).astype(CDT), q, eb)
    k2 = _m(lambda x, e_: (x * e_).astype(CDT), k, e2)
    a = _m(lambda x, y: jnp.where(keep, _dot3(_nt, x, y), 0.0).astype(CDT), qt, kt)
    return dict(sq=sq, q=q, sg=sg, f=f, k=k, eq=eq, ek=ek, eb=eb, e2=e2, e_end=e_end, qt=qt, kt=kt, qs=qs, k2=k2, a=a)


def _gla_fwd(proj, lbl4, f_off, reverse, name):
    S = proj.shape[0]
    C = HG_CHUNK
    R = _rtile(S, 512)
    cpb, nblk = R // C, S // R
    d = 1 if reverse else 0
    blk_map = (lambda b: nblk - 1 - b) if reverse else (lambda b: b)

    W = HG_HPS * HG_DK

    def body(q_ref, i_ref, f_ref, lb_ref, o_ref, st_ref, s_scr):
        @pl.when(pl.program_id(1) == 0)
        def _():
            s_scr[...] = jnp.zeros_like(s_scr)

        l = lb_ref[...]
        lbs = _sigmoid(l[2 * d:2 * d + 1, :] - l[2 * d + 1:2 * d + 2, :])
        keep = _gla_masks(reverse)
        heads = list(range(HG_HPS))
        css = [pl.ds(hh * HG_DK, HG_DK) for hh in heads]
        lb = [lbs[:, hh * HG_DK:(hh + 1) * HG_DK] for hh in heads]
        for cc in range(cpb):
            c = cpb - 1 - cc if reverse else cc
            sl = pl.ds(c * C, C)
            v = [i_ref[sl, cs] for cs in css]
            t = _gla_chunk_fwd([q_ref[sl, cs] for cs in css], [f_ref[sl, cs] for cs in css], lb, keep, reverse)
            st = [s_scr[hh] for hh in heads]
            o = _m(lambda qs, s_, a, v_: _nt(qs, s_) + _nn(a, v_), t["qs"], st, t["a"], v)
            new = _m(lambda e_, s_, v_, k2: e_ * s_ + _tn(v_, k2), t["e_end"], st, v, t["k2"])
            for hh in heads:
                st_ref[c, hh] = st[hh]
                o_ref[sl, css[hh]] = o[hh]
                s_scr[hh] = new[hh]

    col = lambda off: (lambda h, b: (blk_map(b), off // W + h))
    return pl.pallas_call(
        body,
        name=name,
        grid=(HG_HEADS // HG_HPS, nblk),
        in_specs=[
            pl.BlockSpec((R, W), col(OFF_Q)),
            pl.BlockSpec((R, W), col(OFF_I)),
            pl.BlockSpec((R, W), col(f_off)),
            pl.BlockSpec((4, W), lambda h, b: (0, h)),
        ],
        out_specs=[
            pl.BlockSpec((R, W), lambda h, b: (blk_map(b), h)),
            pl.BlockSpec((cpb, HG_HPS, HG_DK, HG_DK), lambda h, b: (blk_map(b), h, 0, 0)),
        ],
        out_shape=[
            jax.ShapeDtypeStruct((S, HG_W), F32),
            jax.ShapeDtypeStruct((S // C, HG_HEADS, HG_DK, HG_DK), F32),
        ],
        scratch_shapes=[pltpu.VMEM((HG_HPS, HG_DK, HG_DK), F32)],
        compiler_params=_cparams(2),
    )(proj, proj, proj, lbl4)


def _gla_bwd(proj, lbl4, f_off, reverse, do, states, dproj, prev, name):
    S = proj.shape[0]
    PW = proj.shape[1]
    C = HG_CHUNK
    R = _rtile(S, 512)
    cpb, nblk = R // C, S // R
    d = 1 if reverse else 0
    blk_map = (lambda b: b) if reverse else (lambda b: nblk - 1 - b)
    final = prev is not None

    if final:
        assert HG_HPS == HG_HEADS and (OFF_Q, OFF_I, f_off) == (0, HG_W, 2 * HG_W)

    def body(*refs):
        if final:
            q_ref, i_ref, f_ref, lb_ref, do_ref, st_ref, pq_ref, pi_ref, _dp, o3_ref, dl_ref, ds_scr = refs
            dq_ref = di_ref = df_ref = o3_ref
        else:
            q_ref, i_ref, f_ref, lb_ref, do_ref, st_ref, dq_ref, di_ref, df_ref, dl_ref, ds_scr = refs
        out_off = (OFF_Q, OFF_I, f_off) if final else (0, 0, 0)
        blk = pl.program_id(1)

        @pl.when(blk == 0)
        def _():
            ds_scr[...] = jnp.zeros_like(ds_scr)
            dl_ref[...] = jnp.zeros_like(dl_ref)

        l = lb_ref[...]
        lbs = _sigmoid(l[2 * d:2 * d + 1, :] - l[2 * d + 1:2 * d + 2, :])
        keep = _gla_masks(reverse)
        keep_t = _gla_masks(not reverse)
        tri_t = jnp.where(keep_t, 1.0, 0.0).astype(CDT)
        end = 0 if reverse else C - 1
        is_end = lax.broadcasted_iota(jnp.int32, (C, HG_DK), 0) == end
        dl_all = [jnp.zeros((SUB, HG_DK), F32) for _ in range(HG_HPS)]
        gsz = HG_BWD_GROUP
        for cc, heads in [(cc, list(range(g0, g0 + gsz))) for cc in range(cpb) for g0 in range(0, HG_HPS, gsz)]:
            css = [pl.ds(hh * HG_DK, HG_DK) for hh in heads]
            lb = [lbs[:, hh * HG_DK:(hh + 1) * HG_DK] for hh in heads]
            dl_acc = [dl_all[hh] for hh in heads]
            c = cc if reverse else cpb - 1 - cc
            sl = pl.ds(c * C, C)
            qraw = [q_ref[sl, cs] for cs in css]
            v = [i_ref[sl, cs] for cs in css]
            t = _gla_chunk_fwd(qraw, [f_ref[sl, cs] for cs in css], lb, keep, reverse)
            dob = [do_ref[sl, cs].astype(CDT) for cs in css]
            vb = _m(lambda x: x.astype(CDT), v)
            st = [st_ref[c, hh] for hh in heads]
            ds = [ds_scr[hh] for hh in heads]
            dsb = _m(lambda x: x.astype(CDT), ds)
            d_qs = _m(_nn, dob, st)
            d_a = _m(lambda x, y: jnp.where(keep, _nt(x, y), 0.0), dob, vb)
            d_qt = _m(lambda x, y: _dot3(_nn, x, y), d_a, t["kt"])
            d_kt = _m(lambda x, y: _dot3(_tn, x, y), d_a, t["qt"])
            d_v = _m(lambda a, x, k2, s_: _tn(a, x) + _nt(k2, s_), t["a"], dob, t["k2"], dsb)
            d_k2 = _m(_nn, vb, dsb)
            d_e = _m(lambda s_, x: jnp.sum(s_ * x, axis=0, keepdims=True), st, ds)
            new_ds = _m(lambda e_, x, y, qs: e_ * x + _tn(y, qs), t["e_end"], ds, dob, t["qs"])
            dq = _m(lambda a, ea, b_, eb_: a * ea + b_ * eb_, d_qt, t["eq"], d_qs, t["eb"])
            dk = _m(lambda a, ea, b_, eb_: a * ea + b_ * eb_, d_kt, t["ek"], d_k2, t["e2"])
            db_end = _m(lambda x, k_, e2, de, ee: jnp.sum(x * (k_ * e2), axis=0, keepdims=True) + de * ee,
                        d_k2, t["k"], t["e2"], d_e, t["e_end"])
            db = _m(lambda q_, dq_, k_, dk_, be: q_ * dq_ - k_ * dk_ + jnp.where(is_end, be, 0.0),
                    t["q"], dq, t["k"], dk, db_end)
            dg = _m(lambda x: _tri_matmul(tri_t, x), db)
            df = _m(lambda g_, f_, dk_: g_ / f_ - dk_, dg, t["f"], dk)
            dfraw = _m(lambda x, l_, s_: x * (1.0 - l_) * s_ * (1.0 - s_), df, lb, t["sg"])
            dl_acc = _m(lambda acc, x, s_: acc + _rowsum8(x * (1.0 - s_)), dl_acc, df, t["sg"])
            dqraw = _m(lambda x, s_, r: x * (s_ * (1.0 + r * (1.0 - s_))), dq, t["sq"], qraw)
            if final:
                dqraw = [x + pq_ref[sl, cs] for x, cs in zip(dqraw, css)]
                d_v = [x + pi_ref[sl, cs] for x, cs in zip(d_v, css)]
            for n, hh in enumerate(heads):
                dl_all[hh] = dl_acc[n]
                ds_scr[hh] = new_ds[n]
                for ref, off, val in zip((dq_ref, di_ref, df_ref), out_off, (dqraw[n], d_v[n], dfraw[n])):
                    ref[sl, pl.ds(off + hh * HG_DK, HG_DK)] = val.astype(ref.dtype)
        dl_ref[...] += jnp.concatenate(dl_all, axis=1) * (lbs * (1.0 - lbs))

    W = HG_HPS * HG_DK
    col = lambda off: (lambda h, b: (blk_map(b), off // W + h))
    blk = lambda: pl.BlockSpec((R, W), lambda h, b: (blk_map(b), h))
    ins = [proj, proj, proj, lbl4, do, states]
    in_specs = [
        pl.BlockSpec((R, W), col(OFF_Q)),
        pl.BlockSpec((R, W), col(OFF_I)),
        pl.BlockSpec((R, W), col(f_off)),
        pl.BlockSpec((4, W), lambda h, b: (0, h)),
        blk(),
        pl.BlockSpec((cpb, HG_HPS, HG_DK, HG_DK), lambda h, b: (blk_map(b), h, 0, 0)),
    ]
    dl_shape = jax.ShapeDtypeStruct((SUB, HG_W), F32)
    dl_spec = pl.BlockSpec((SUB, W), lambda h, b: (0, h))
    dp_shape = jax.ShapeDtypeStruct((S, PW), CDT)
    if final:
        ins += [prev[0], prev[1], dproj]
        in_specs += [blk(), blk(), pl.BlockSpec(memory_space=pl.ANY)]
        out_shape = [dp_shape, dl_shape]
        out_specs = [pl.BlockSpec((R, 3 * HG_W), lambda h, b: (blk_map(b), 0)), dl_spec]
        aliases = {8: 0}
    else:
        out_shape = [jax.ShapeDtypeStruct((S, HG_W), F32), jax.ShapeDtypeStruct((S, HG_W), F32), dp_shape, dl_shape]
        out_specs = [blk(), blk(), pl.BlockSpec((R, W), col(f_off)), dl_spec]
        aliases = {}
        if dproj is not None:
            ins += [dproj]
            in_specs += [pl.BlockSpec(memory_space=pl.ANY)]
            aliases = {6: 2}
    if (not final) and dproj is not None:
        def body_wrapped(*refs, _b=body):
            _b(*refs[:6], *refs[7:])
        kern = body_wrapped
    else:
        kern = body
    res = pl.pallas_call(
        kern,
        name=name,
        grid=(HG_HEADS // HG_HPS, nblk),
        in_specs=in_specs,
        out_specs=out_specs,
        out_shape=out_shape,
        scratch_shapes=[pltpu.VMEM((HG_HPS, HG_DK, HG_DK), F32)],
        input_output_aliases=aliases,
        compiler_params=_cparams(2),
    )(*ins)
    if final:
        return res[0], None, None, res[1]
    dq, di, dproj, dl = res
    return dproj, dq, di, dl


def _hgrn_post_fwd(o_f, o_b, proj, norm_g):
    S = o_f.shape[0]

    def epi(accs, tiles, rows):
        of, ob, graw = tiles
        ng = rows[0][:, :HG_DK]
        o = of + ob
        ys = []
        for h in range(HG_HEADS):
            oh = o[:, h * HG_DK:(h + 1) * HG_DK]
            rs = lax.rsqrt(jnp.mean(oh * oh, axis=-1, keepdims=True) + RMS_EPS)
            ys.append(oh * rs * ng * _sigmoid(graw[:, h * HG_DK:(h + 1) * HG_DK]))
        return [jnp.concatenate(ys, axis=1)], []

    tm = _rtile(S, 512)
    (y,) = _fused_mm("hgrn_post_fwd", "nn", [], S, HG_W, 1, tm, HG_W, 1, [(HG_W, CDT, 0, None)], epi,
                     tiles=[(o_f, 0), (o_b, 0), (proj, OFF_G)], rows=[(jnp.tile(norm_g, (1, HG_HEADS)), 0)])
    return y


def _hgrn_post_bwd(dy, o_f, o_b, proj, norm_g, dproj):
    S = o_f.shape[0]

    def epi(accs, tiles, rows):
        dyv, of, ob, graw = tiles
        ng = rows[0][:, :HG_DK]
        o = of + ob
        dos, dgs, dns = [], [], []
        for h in range(HG_HEADS):
            sl = slice(h * HG_DK, (h + 1) * HG_DK)
            oh, gh, dyh = o[:, sl], graw[:, sl], dyv[:, sl].astype(F32)
            rs = lax.rsqrt(jnp.mean(oh * oh, axis=-1, keepdims=True) + RMS_EPS)
            xh = oh * rs
            sg = _sigmoid(gh)
            dn = dyh * sg
            dgs.append(dyh * (xh * ng) * sg * (1.0 - sg))
            dns.append(dn * xh)
            dxh = dn * ng
            dos.append(rs * (dxh - xh * jnp.mean(dxh * xh, axis=-1, keepdims=True)))
        return [jnp.concatenate(dos, axis=1), jnp.concatenate(dgs, axis=1)], [jnp.concatenate(dns, axis=1)]

    tm = _rtile(S, 512)
    do, dproj, dn = _fused_mm("hgrn_post_bwd", "nn", [], S, HG_W, 1, tm, HG_W, 1,
                              [(HG_W, F32, 0, None), (dproj.shape[1], CDT, OFF_G, dproj)], epi,
                              tiles=[(dy, 0), (o_f, 0), (o_b, 0), (proj, OFF_G)],
                              rows=[(jnp.tile(norm_g, (1, HG_HEADS)), 0)], n_racc=1)
    return do, dproj, dn


def _copy_into(name, src, dst, off):
    S, W = src.shape
    tm = _rtile(S, 512)
    (dst,) = _fused_mm(name, "nn", [], S, W, 1, tm, W, 1, [(dst.shape[1], dst.dtype, off, dst)],
                       lambda accs, tiles, rows: ([tiles[0]], []), tiles=[(src, 0)])
    return dst


def _rms_stats(x):
    rs = lax.rsqrt(jnp.mean(x * x, axis=-1, keepdims=True) + RMS_EPS)
    return x * rs, rs


def _mla_up(proj, cf, sf, g_cq, g_ckv, wuq_p, wukv):
    S = proj.shape[0]
    tm = _rtile(S, 512)
    H = MLA_HEADS

    def body(cq_ref, ckv_ref, kr_ref, krot_ref, cf_ref, sf_ref, gq_ref, gkv_ref, wq_ref, wkv_ref,
             q_ref, k_ref, v_ref, vt_ref, cqn_ref, ckvn_ref):
        cqn = (_rms_stats(cq_ref[...])[0] * gq_ref[...]).astype(CDT)
        ckvn = (_rms_stats(ckv_ref[...])[0] * gkv_ref[...]).astype(CDT)
        cqn_ref[...] = cqn
        ckvn_ref[...] = ckvn
        cfv, sfv = cf_ref[...], sf_ref[...]
        k_roped = (kr_ref[...] * cfv + krot_ref[...] * sfv).astype(CDT)
        ones = jnp.ones((VT_ROWS - LANE, tm), CDT)
        for h in range(H):
            r = _nn(cqn, wq_ref[h]) * MLA_QSCALE
            q_ref[h, :, 0:LANE] = r[:, 0:LANE].astype(CDT)
            q_ref[h, :, LANE:2 * LANE] = (r[:, LANE:2 * LANE] * cfv + r[:, 2 * LANE:3 * LANE] * sfv).astype(CDT)
            kv = _nn(ckvn, wkv_ref[h])
            k_ref[h, :, 0:LANE] = kv[:, 0:LANE].astype(CDT)
            k_ref[h, :, LANE:2 * LANE] = k_roped
            vv = kv[:, LANE:2 * LANE]
            v_ref[h] = vv.astype(CDT)
            vt_ref[h, 0, 0:LANE, :] = vv.T.astype(CDT)
            vt_ref[h, 0, LANE:VT_ROWS, :] = ones

    PWb = proj.shape[1]
    kr_off = PWb - KR_PAD
    cspec = lambda off, w: pl.BlockSpec((tm, w), lambda i, o=off // w: (i, o))
    return pl.pallas_call(
        body,
        name="mla_up_fwd",
        grid=(S // tm,),
        in_specs=[
            cspec(OFF_CQ, MLA_RANK), cspec(OFF_CKV, MLA_RANK), cspec(kr_off, LANE), cspec(kr_off + LANE, LANE),
            pl.BlockSpec((tm, LANE), lambda i: (i, 0)), pl.BlockSpec((tm, LANE), lambda i: (i, 0)),
            pl.BlockSpec((1, MLA_RANK), lambda i: (0, 0)), pl.BlockSpec((1, MLA_RANK), lambda i: (0, 0)),
            pl.BlockSpec((H, MLA_RANK, 3 * LANE), lambda i: (0, 0, 0)),
            pl.BlockSpec((H, MLA_RANK, 2 * LANE), lambda i: (0, 0, 0)),
        ],
        out_specs=[
            pl.BlockSpec((H, tm, 2 * LANE), lambda i: (0, i, 0)),
            pl.BlockSpec((H, tm, 2 * LANE), lambda i: (0, i, 0)),
            pl.BlockSpec((H, tm, LANE), lambda i: (0, i, 0)),
            pl.BlockSpec((H, 1, VT_ROWS, tm), lambda i: (0, i, 0, 0)),
            pl.BlockSpec((tm, MLA_RANK), lambda i: (i, 0)),
            pl.BlockSpec((tm, MLA_RANK), lambda i: (i, 0)),
        ],
        out_shape=[
            jax.ShapeDtypeStruct((H, S, 2 * LANE), CDT), jax.ShapeDtypeStruct((H, S, 2 * LANE), CDT),
            jax.ShapeDtypeStruct((H, S, LANE), CDT), jax.ShapeDtypeStruct((H, S // tm, VT_ROWS, tm), CDT),
            jax.ShapeDtypeStruct((S, MLA_RANK), CDT), jax.ShapeDtypeStruct((S, MLA_RANK), CDT),
        ],
        compiler_params=_cparams(1),
    )(proj, proj, proj, proj, cf, sf, g_cq, g_ckv, wuq_p, wukv)


def _mla_attn_fwd(q_cat, k_cat, vt, xchg=None):
    H, S, _ = q_cat.shape
    tq = _tile(S, MLA_TQ_FWD)
    _, nkb, _, tk = vt.shape
    nq = S // tq
    nx = xchg.n if xchg is not None else 0

    def body(*refs):
        q_ref, k_ref, vt_ref = refs[:3]
        x_in = refs[3:3 + nx]
        y_ref, ot_ref, lse_ref = refs[3 + nx:6 + nx]
        x_out = refs[6 + nx:6 + 2 * nx]
        m_scr, acc_scr = refs[6 + 2 * nx:8 + 2 * nx]
        x_sems = refs[8 + 2 * nx:]
        h, i = pl.program_id(0), pl.program_id(1)
        if nx:
            @pl.when((h == 0) & (i == 0))
            def _():
                xchg.start(x_in, x_out, x_sems)

        nsub = MLA_FWD_SLABS if tq % (MLA_FWD_SLABS * LANE) == 0 else 1
        ws = tq // nsub
        subs = [pl.ds(s * ws, ws) for s in range(nsub)]
        qs = [q_ref[0, sb, :] for sb in subs]
        m_scr[...] = jnp.full_like(m_scr, -jnp.inf)
        acc_scr[...] = jnp.zeros_like(acc_scr)

        def step(j, carry):
            kj = k_ref[0, pl.ds(pl.multiple_of(j * tk, tk), tk), :]
            vtj = vt_ref[0, j]
            sts = [_nt(kj, qq) for qq in qs]
            m_old = [m_scr[:, sb] for sb in subs]
            m_new = _m(lambda mo, st: jnp.maximum(mo, jnp.max(st, axis=0, keepdims=True)), m_old, sts)
            pts = _m(lambda st, mn: jnp.exp2(st - mn), sts, m_new)
            pvs = _m(lambda pt: _nn(vtj, pt), pts)
            for sb, mo, mn, pv in zip(subs, m_old, m_new, pvs):
                acc_scr[:, sb] = jnp.exp2(mo - mn) * acc_scr[:, sb] + pv
                m_scr[:, sb] = mn
            return carry

        lax.fori_loop(0, nkb, step, 0, unroll=4 if nkb % 4 == 0 else 1)
        l = acc_scr[LANE:LANE + 1, :]
        ot = acc_scr[0:LANE, :] / l
        ot_ref[0] = ot
        y_ref[...] = ot.T.astype(CDT)
        lse_ref[0, 0] = m_scr[...] + jnp.log2(l)

        if nx:
            @pl.when((h == H - 1) & (i == nq - 1))
            def _():
                xchg.wait(x_in, x_out, x_sems)

    return pl.pallas_call(
        body,
        name="mla_attn_fwd",
        grid=(H, nq),
        in_specs=[
            pl.BlockSpec((1, tq, 2 * LANE), lambda h, i: (h, i, 0)),
            pl.BlockSpec((1, S, 2 * LANE), lambda h, i: (h, 0, 0)),
            pl.BlockSpec((1, nkb, VT_ROWS, tk), lambda h, i: (h, 0, 0, 0)),
        ] + (xchg.specs if nx else []),
        out_specs=[
            pl.BlockSpec((tq, LANE), lambda h, i: (i, h)),
            pl.BlockSpec((1, LANE, tq), lambda h, i: (h, 0, i)),
            pl.BlockSpec((1, 1, 1, tq), lambda h, i: (h, i, 0, 0)),
        ] + (xchg.specs if nx else []),
        out_shape=[
            jax.ShapeDtypeStruct((S, H * LANE), CDT),
            jax.ShapeDtypeStruct((H, LANE, S), F32),
            jax.ShapeDtypeStruct((H, nq, 1, tq), F32),
        ] + (xchg.out_shape if nx else []),
        scratch_shapes=[pltpu.VMEM((1, tq), F32), pltpu.VMEM((VT_ROWS, tq), F32)] + (xchg.scratch if nx else []),
        compiler_params=_cparams(2, side_effects=nx > 0),
    )(q_cat, k_cat, vt, *(xchg.arrs if nx else []))


def _mla_delta(dy, ot):
    H, _, S = ot.shape
    tq = _tile(S, MLA_TQ)
    nq = S // tq

    def body(dy_ref, ot_ref, d_ref):
        d_ref[0, 0] = jnp.sum(dy_ref[...].astype(F32).T * ot_ref[0], axis=0, keepdims=True)

    return pl.pallas_call(
        body,
        name="mla_delta",
        grid=(H, nq),
        in_specs=[pl.BlockSpec((tq, LANE), lambda h, i: (i, h)), pl.BlockSpec((1, LANE, tq), lambda h, i: (h, 0, i))],
        out_specs=pl.BlockSpec((1, 1, 1, tq), lambda h, i: (h, i, 0, 0)),
        out_shape=jax.ShapeDtypeStruct((H, nq, 1, tq), F32),
        compiler_params=_cparams(2),
    )(dy, ot)


def _mla_attn_bwd(q_cat, k_cat, v, dy, lse, delta, xchg=None):
    H, S, _ = q_cat.shape
    _, nq, _, tq = lse.shape
    tk = _tile(S, 512)
    nkb = S // tk
    nx = xchg.n if xchg is not None else 0

    def body(*refs):
        k_ref, v_ref, q_ref, do_ref, lse_ref, dl_ref = refs[:6]
        x_in = refs[6:6 + nx]
        dk_ref, dv_ref, dq_ref = refs[6 + nx:9 + nx]
        x_out = refs[9 + nx:9 + 2 * nx]
        dk_scr, dv_scr = refs[9 + 2 * nx:11 + 2 * nx]
        x_sems = refs[11 + 2 * nx:]
        hd, ki = pl.program_id(0), pl.program_id(1)
        if nx:
            @pl.when((hd == 0) & (ki == 0))
            def _():
                xchg.start(x_in, x_out, x_sems)

        @pl.when(ki == 0)
        def _():
            dq_ref[...] = jnp.zeros_like(dq_ref)

        kb, vb = k_ref[0], v_ref[0]
        dk_scr[...] = jnp.zeros_like(dk_scr)
        dv_scr[...] = jnp.zeros_like(dv_scr)

        def step(i, carry):
            rows = pl.ds(pl.multiple_of(i * tq, tq), tq)
            qc = q_ref[0, rows, :]
            doc = do_ref[rows, :]
            pt = jnp.exp2(_nt(kb, qc) - lse_ref[0, i])
            dv_scr[...] += _nn(pt, doc)
            dst = (pt * (_nt(vb, doc) - dl_ref[0, i])).astype(CDT)
            dk_scr[...] += _nn(dst, qc)
            dq_ref[0, rows, :] += _tn(dst, kb)
            return carry

        lax.fori_loop(0, nq, step, 0, unroll=2 if nq % 2 == 0 else 1)
        dk_ref[0] = dk_scr[...] * (MLA_SCALE / MLA_QSCALE)
        dv_ref[0] = dv_scr[...]

        if nx:
            @pl.when((hd == H - 1) & (ki == nkb - 1))
            def _():
                xchg.wait(x_in, x_out, x_sems)

    return pl.pallas_call(
        body,
        name="mla_attn_bwd",
        grid=(H, nkb),
        in_specs=[
            pl.BlockSpec((1, tk, 2 * LANE), lambda h, j: (h, j, 0)),
            pl.BlockSpec((1, tk, LANE), lambda h, j: (h, j, 0)),
            pl.BlockSpec((1, S, 2 * LANE), lambda h, j: (h, 0, 0)),
            pl.BlockSpec((S, LANE), lambda h, j: (0, h)),
            pl.BlockSpec((1, nq, 1, tq), lambda h, j: (h, 0, 0, 0)),
            pl.BlockSpec((1, nq, 1, tq), lambda h, j: (h, 0, 0, 0)),
        ] + (xchg.specs if nx else []),
        out_specs=[
            pl.BlockSpec((1, tk, 2 * LANE), lambda h, j: (h, j, 0)),
            pl.BlockSpec((1, tk, LANE), lambda h, j: (h, j, 0)),
            pl.BlockSpec((1, S, 2 * LANE), lambda h, j: (h, 0, 0)),
        ] + (xchg.specs if nx else []),
        out_shape=[
            jax.ShapeDtypeStruct((H, S, 2 * LANE), F32),
            jax.ShapeDtypeStruct((H, S, LANE), F32),
            jax.ShapeDtypeStruct((H, S, 2 * LANE), F32),
        ] + (xchg.out_shape if nx else []),
        scratch_shapes=[pltpu.VMEM((tk, 2 * LANE), F32), pltpu.VMEM((tk, LANE), F32)] + (xchg.scratch if nx else []),
        compiler_params=_cparams(2, side_effects=nx > 0),
    )(k_cat, v, q_cat, dy, lse, delta, *(xchg.arrs if nx else []))


def _mla_up_bwd(dq_cat, dk_cat, dv, proj, cf, sf, g_cq, g_ckv, wuq_p, wukv, dproj):
    H, S, _ = dq_cat.shape
    tm = _rtile(S, 256)
    PW = proj.shape[1]
    kr_off = PW - KR_PAD

    assert OFF_CKV == OFF_CQ + MLA_RANK and OFF_CQ % (2 * MLA_RANK) == 0

    def body(dq_ref, dk_ref, dv_ref, cq_ref, ckv_ref, cf_ref, sf_ref, gq_ref, gkv_ref, wq_ref, wkv_ref, _dp,
             dqp_ref, dkvp_ref, dc_ref, dkr_ref, dgq_ref, dgkv_ref):
        i = pl.program_id(0)
        dcq_ref, dckv_ref = dc_ref.at[:, 0:MLA_RANK], dc_ref.at[:, MLA_RANK:2 * MLA_RANK]
        cfv, sfv = cf_ref[...], sf_ref[...]
        aq = jnp.zeros((tm, MLA_RANK), F32)
        akv = jnp.zeros((tm, MLA_RANK), F32)
        akr = jnp.zeros((tm, LANE), F32)
        for h in range(H):
            dq = dq_ref[h] * MLA_SCALE
            dqr = dq[:, LANE:2 * LANE]
            dqp = jnp.concatenate([dq[:, 0:LANE], dqr * cfv, dqr * sfv], axis=1).astype(CDT)
            dqp_ref[:, pl.ds(h * 3 * LANE, 3 * LANE)] = dqp
            aq = aq + _nt(dqp, wq_ref[h])
            dk = dk_ref[h]
            dkvp = jnp.concatenate([dk[:, 0:LANE], dv_ref[h]], axis=1).astype(CDT)
            dkvp_ref[:, pl.ds(h * 2 * LANE, 2 * LANE)] = dkvp
            akv = akv + _nt(dkvp, wkv_ref[h])
            akr = akr + dk[:, LANE:2 * LANE]

        def rms_bwd(c_ref, g_ref, dn, d_ref, dg_ref):
            xh, rs = _rms_stats(c_ref[...])
            dxh = dn * g_ref[...]
            d_ref[...] = (rs * (dxh - xh * jnp.mean(dxh * xh, axis=-1, keepdims=True))).astype(d_ref.dtype)
            part = _rowsum8(dn * xh)

            @pl.when(i == 0)
            def _():
                dg_ref[...] = part

            @pl.when(i > 0)
            def _():
                dg_ref[...] += part

        rms_bwd(cq_ref, gq_ref, aq, dcq_ref, dgq_ref)
        rms_bwd(ckv_ref, gkv_ref, akv, dckv_ref, dgkv_ref)
        dkr_ref[...] = jnp.concatenate([akr * cfv, akr * sfv, jnp.zeros((tm, KR_PAD - 2 * LANE), F32)], axis=1).astype(dkr_ref.dtype)

    cspec = lambda off, w: pl.BlockSpec((tm, w), lambda i, o=off // w: (i, o))
    hspec = lambda w: pl.BlockSpec((H, tm, w), lambda i: (0, i, 0))
    outs = pl.pallas_call(
        body,
        name="mla_up_bwd",
        grid=(S // tm,),
        in_specs=[
            hspec(2 * LANE), hspec(2 * LANE), hspec(LANE),
            cspec(OFF_CQ, MLA_RANK), cspec(OFF_CKV, MLA_RANK),
            pl.BlockSpec((tm, LANE), lambda i: (i, 0)), pl.BlockSpec((tm, LANE), lambda i: (i, 0)),
            pl.BlockSpec((1, MLA_RANK), lambda i: (0, 0)), pl.BlockSpec((1, MLA_RANK), lambda i: (0, 0)),
            pl.BlockSpec((H, MLA_RANK, 3 * LANE), lambda i: (0, 0, 0)),
            pl.BlockSpec((H, MLA_RANK, 2 * LANE), lambda i: (0, 0, 0)),
            pl.BlockSpec(memory_space=pl.ANY),
        ],
        out_specs=[
            pl.BlockSpec((tm, H * 3 * LANE), lambda i: (i, 0)), pl.BlockSpec((tm, H * 2 * LANE), lambda i: (i, 0)),
            pl.BlockSpec((tm, 2 * MLA_RANK), lambda i: (i, OFF_CQ // (2 * MLA_RANK))),
            pl.BlockSpec((tm, KR_PAD), lambda i: (i, 0)),
            pl.BlockSpec((SUB, MLA_RANK), lambda i: (0, 0)),
            pl.BlockSpec((SUB, MLA_RANK), lambda i: (0, 0)),
        ],
        out_shape=[
            jax.ShapeDtypeStruct((S, H * 3 * LANE), CDT), jax.ShapeDtypeStruct((S, H * 2 * LANE), CDT),
            jax.ShapeDtypeStruct(dproj.shape, dproj.dtype),
            jax.ShapeDtypeStruct((S, KR_PAD), CDT),
            jax.ShapeDtypeStruct((SUB, MLA_RANK), F32), jax.ShapeDtypeStruct((SUB, MLA_RANK), F32),
        ],
        input_output_aliases={11: 2},
        compiler_params=_cparams(1),
    )(dq_cat, dk_cat, dv, proj, proj, cf, sf, g_cq, g_ckv, wuq_p, wukv, dproj)
    dqp, dkvp, dproj, dkr, dgq, dgkv = outs
    dproj = _copy_into("dproj_kr", dkr, dproj, kr_off)
    return dproj, dqp, dkvp, dgq, dgkv


def _mem_softmax(q, k):
    s = _nt(q, k) * (MEM_HD ** -0.5)
    p = jnp.exp(s - jnp.max(s, axis=1, keepdims=True))
    return p / jnp.sum(p, axis=1, keepdims=True)


def _mem_attn_fwd(proj, memkv):
    S = proj.shape[0]
    Mm = memkv.shape[0]
    tm = _rtile(S, 1024)

    def body(q_ref, k_ref, v_ref, y_ref):
        pn = _mem_softmax(q_ref[...], k_ref[...])
        y_ref[...] = _nn(pn, v_ref[...]).astype(y_ref.dtype)

    return pl.pallas_call(
        body,
        name="mem_attn_fwd",
        grid=(S // tm, MEM_HEADS),
        in_specs=[
            pl.BlockSpec((tm, MEM_HD), lambda i, h: (i, OFF_QM // MEM_HD + h)),
            pl.BlockSpec((Mm, MEM_HD), lambda i, h: (0, h)),
            pl.BlockSpec((Mm, MEM_HD), lambda i, h: (0, MEM_HEADS + h)),
        ],
        out_specs=pl.BlockSpec((tm, MEM_HD), lambda i, h: (i, h)),
        out_shape=jax.ShapeDtypeStruct((S, MEM_W), CDT),
        compiler_params=_cparams(2),
    )(proj, memkv, memkv)


def _mem_attn_bwd(dy, proj, memkv, dproj):
    S = proj.shape[0]
    Mm = memkv.shape[0]
    tm = _rtile(S, 1024)
    scale = MEM_HD ** -0.5

    def body(dy_ref, q_ref, k_ref, v_ref, _dp, dq_ref, dk_ref, dv_ref):
        i = pl.program_id(1)
        q, k, dyv = q_ref[...].astype(CDT), k_ref[...], dy_ref[...]
        pn = _mem_softmax(q, k)
        dvp = _tn(pn, dyv)
        dp = _nt(dyv, v_ref[...])
        ds = pn * (dp - jnp.sum(dp * pn, axis=1, keepdims=True)) * scale
        dq_ref[...] = _nn(ds, k).astype(dq_ref.dtype)
        dkp = _tn(ds, q)

        @pl.when(i == 0)
        def _():
            dk_ref[...] = dkp
            dv_ref[...] = dvp

        @pl.when(i > 0)
        def _():
            dk_ref[...] += dkp
            dv_ref[...] += dvp

    dproj, dk, dv = pl.pallas_call(
        body,
        name="mem_attn_bwd",
        grid=(MEM_HEADS, S // tm),
        in_specs=[
            pl.BlockSpec((tm, MEM_HD), lambda h, i: (i, h)),
            pl.BlockSpec((tm, MEM_HD), lambda h, i: (i, OFF_QM // MEM_HD + h)),
            pl.BlockSpec((Mm, MEM_HD), lambda h, i: (0, h)),
            pl.BlockSpec((Mm, MEM_HD), lambda h, i: (0, MEM_HEADS + h)),
            pl.BlockSpec(memory_space=pl.ANY),
        ],
        out_specs=[
            pl.BlockSpec((tm, MEM_HD), lambda h, i: (i, OFF_QM // MEM_HD + h)),
            pl.BlockSpec((Mm, MEM_HD), lambda h, i: (0, h)),
            pl.BlockSpec((Mm, MEM_HD), lambda h, i: (0, h)),
        ],
        out_shape=[
            jax.ShapeDtypeStruct(dproj.shape, dproj.dtype),
            jax.ShapeDtypeStruct((Mm, MEM_W), F32),
            jax.ShapeDtypeStruct((Mm, MEM_W), F32),
        ],
        input_output_aliases={4: 0},
        compiler_params=_cparams(2),
    )(dy, proj, memkv, memkv, dproj)
    return dproj, dk, dv


def _small_allreduce(vec):
    NS = vec.shape[1]

    def body(v_ref, o_ref, gbuf, send, recv):
        x, y, c, me = _my_place()
        gbuf[me] = v_ref[...]
        copies = []
        for kk in range(1, N_DEV):
            peer, _ = _peer(x, y, c, kk)
            cp = pltpu.make_async_remote_copy(src_ref=v_ref, dst_ref=gbuf.at[me], send_sem=send.at[kk - 1],
                                              recv_sem=recv.at[kk - 1], device_id=peer, device_id_type=MESH)
            cp.start()
            copies.append(cp)
        for cp in copies:
            cp.wait()
        tot = gbuf[0]
        for d in range(1, N_DEV):
            tot = tot + gbuf[d]
        o_ref[...] = jnp.sum(tot, axis=0, keepdims=True)

    return pl.pallas_call(
        body,
        name="small_allreduce",
        in_specs=[pl.BlockSpec(memory_space=pltpu.VMEM)],
        out_specs=pl.BlockSpec(memory_space=pltpu.VMEM),
        out_shape=jax.ShapeDtypeStruct((1, NS), F32),
        scratch_shapes=[pltpu.VMEM((N_DEV, SUB, NS), F32), pltpu.SemaphoreType.DMA((N_DEV - 1,)),
                        pltpu.SemaphoreType.DMA((N_DEV - 1,))],
        compiler_params=pltpu.CompilerParams(has_side_effects=True, vmem_limit_bytes=V7X_VMEM_LIMIT),
    )(vec)


def _adamw_math(g, w, m, v):
    nm = ADAM_B1 * m + (1.0 - ADAM_B1) * g
    nv = ADAM_B2 * v + (1.0 - ADAM_B2) * (g * g)
    mh = nm / (1.0 - ADAM_B1 ** ADAM_STEP)
    vh = nv / (1.0 - ADAM_B2 ** ADAM_STEP)
    delta = -ADAM_LR * (mh / (jnp.sqrt(vh) + ADAM_EPS) + ADAM_WD * w)
    return delta, nm, nv


def _adam_big(name, recv, w, m, v):
    _, R, C = w.shape
    tr = _rtile(R, max(SUB, (ADAM_BLOCK_ELEMS // C) // SUB * SUB))

    def body(r_ref, w_ref, m_ref, v_ref, g_ref, d_ref, nm_ref, nv_ref):
        g = r_ref[0].astype(F32)
        for d in range(1, N_DEV):
            g = g + r_ref[d].astype(F32)
        delta, nm, nv = _adamw_math(g, w_ref[0], m_ref[0], v_ref[0])
        g_ref[0] = g
        d_ref[0] = delta
        nm_ref[0] = nm
        nv_ref[0] = nv

    blk = pl.BlockSpec((1, tr, C), lambda i: (0, i, 0))
    return pl.pallas_call(
        body,
        name=name,
        grid=(R // tr,),
        in_specs=[pl.BlockSpec((N_DEV, tr, C), lambda i: (0, i, 0)), blk, blk, blk],
        out_specs=[blk, blk, blk, blk],
        out_shape=[jax.ShapeDtypeStruct((1, R, C), F32)] * 4,
        compiler_params=_cparams(1),
    )(recv, w, m, v)


def _to_bf16(name, w):
    _, R, C = w.shape
    tr = _rtile(R, max(SUB, (ADAM_BLOCK_ELEMS // C) // SUB * SUB))

    def body(w_ref, o_ref):
        o_ref[...] = w_ref[0].astype(CDT)

    return pl.pallas_call(
        body,
        name=name,
        grid=(R // tr,),
        in_specs=[pl.BlockSpec((1, tr, C), lambda i: (0, i, 0))],
        out_specs=pl.BlockSpec((tr, C), lambda i: (i, 0)),
        out_shape=jax.ShapeDtypeStruct((R, C), CDT),
        compiler_params=_cparams(1),
    )(w)


def _adam_small(g, w, m, v):
    def body(g_ref, w_ref, m_ref, v_ref, d_ref, nm_ref, nv_ref):
        delta, nm, nv = _adamw_math(g_ref[...], w_ref[...], m_ref[...], v_ref[...])
        d_ref[...] = delta
        nm_ref[...] = nm
        nv_ref[...] = nv

    return pl.pallas_call(body, name="adam_small", out_shape=[jax.ShapeDtypeStruct(g.shape, F32)] * 3)(g, w, m, v)


def _rot(w, axis=-1):
    x1, x2 = jnp.split(w, 2, axis=axis)
    return jnp.concatenate([-x2, x1], axis=axis)


def _unrot(dw, axis=-1):
    d1, d2 = jnp.split(dw, 2, axis=axis)
    return jnp.concatenate([d2, -d1], axis=axis)


def _pad_cols(w, width):
    return jnp.pad(w, [(0, 0)] * (w.ndim - 1) + [(0, width - w.shape[-1])])


def kernel(x, mem, positions, ln_emb_g, ln_emb_b, hgrn_lb_logits, w_in, hgrn_norm_g, mla_g_cq, mla_g_ckv, mla_w_uq, mla_w_ukv, mem_w_kv, w_branch, w_o, ln1_g, ln1_b, w_ffn_gate, w_ffn_up, w_ffn_down, ln2_g, ln2_b, loss_target, m_ln_emb_g, m_ln_emb_b, m_hgrn_lb_logits, m_w_in, m_hgrn_norm_g, m_mla_g_cq, m_mla_g_ckv, m_mla_w_uq, m_mla_w_ukv, m_mem_w_kv, m_w_branch, m_w_o, m_ln1_g, m_ln1_b, m_w_ffn_gate, m_w_ffn_up, m_w_ffn_down, m_ln2_g, m_ln2_b, v_ln_emb_g, v_ln_emb_b, v_hgrn_lb_logits, v_w_in, v_hgrn_norm_g, v_mla_g_cq, v_mla_g_ckv, v_mla_w_uq, v_mla_w_ukv, v_mem_w_kv, v_w_branch, v_w_o, v_ln1_g, v_ln1_b, v_w_ffn_gate, v_w_ffn_up, v_w_ffn_down, v_ln2_g, v_ln2_b):
    x2, tgt = x[0], loss_target[0]
    S, D = x2.shape
    Mm = mem.shape[1]
    F = w_ffn_gate.shape[2] * N_DEV
    GW = 3 * D
    PW = OFF_GATE + GW + KR_PAD
    KR = OFF_GATE + GW
    NIN = w_in.shape[2] * N_DEV
    assert NIN == OFF_GATE + MLA_ROPE + GW
    _, _, _, me = _my_place()
    row = lambda a: a.reshape(1, -1)

    br3 = lambda a: a.reshape(1, 3 * BR_W, -1)
    tp = lambda a: jnp.swapaxes(a, 1, 2)
    big_w = [tp(w_in), mla_w_uq, mla_w_ukv, mem_w_kv, br3(w_branch), w_o, tp(w_ffn_gate), tp(w_ffn_up), w_ffn_down]
    big_m = [tp(m_w_in), m_mla_w_uq, m_mla_w_ukv, m_mem_w_kv, br3(m_w_branch), m_w_o, tp(m_w_ffn_gate), tp(m_w_ffn_up),
             m_w_ffn_down]
    big_v = [tp(v_w_in), v_mla_w_uq, v_mla_w_ukv, v_mem_w_kv, br3(v_w_branch), v_w_o, tp(v_w_ffn_gate), tp(v_w_ffn_up),
             v_w_ffn_down]
    transposed = (0, 6, 7)
    wnames = ["w_in", "w_uq", "w_ukv", "mem_w_kv", "w_branch", "w_o", "w_gate", "w_up", "w_down"]
    big_wb = [_to_bf16("bf16_" + nme, w) for nme, w in zip(wnames, big_w)]
    g_in, g_lb = _all_gather_two_level("weights_all_gather", [big_wb[0], hgrn_lb_logits.reshape(4, -1)])
    win_t = g_in.reshape(NIN, D)
    kr_w = win_t[OFF_QM:OFF_QM + MLA_ROPE]
    zeros64 = jnp.zeros_like(kr_w)
    win_pt = jnp.concatenate([win_t[:OFF_FB], win_t[OFF_FF:OFF_G], win_t[OFF_FB:OFF_FF], win_t[OFF_G:OFF_QM],
                              win_t[OFF_QM + MLA_ROPE:], kr_w, zeros64, _rot(kr_w, 0), zeros64,
                              jnp.zeros((KR_PAD - 2 * LANE, D), CDT)], axis=0)
    lbl4 = jnp.transpose(g_lb, (1, 0, 2)).reshape(4, -1)

    half = MLA_ROPE // 2
    inv_freq = jnp.power(ROPE_THETA, -jnp.arange(half, dtype=F32) / half)
    ang = positions[0].astype(F32)[:, None] * inv_freq
    cf = _pad_cols(jnp.tile(jnp.cos(ang), (1, 2)), LANE)
    sf = _pad_cols(jnp.tile(jnp.sin(ang), (1, 2)), LANE)

    tm512 = _rtile(S, 512)
    ident = lambda accs, tiles, rows: ([accs[0]], [])

    def epi_ln0(accs, tiles, rows):
        h = _ln_stats(tiles[0])[0] * rows[0] + rows[1]
        return [h, h], []

    h0, h0b = _fused_mm("ln_emb_fwd", "nn", [], S, D, 1, tm512, D, 1, [(D, F32, 0, None), (D, CDT, 0, None)], epi_ln0,
                        tiles=[(x2, 0)], rows=[(row(ln_emb_g), 0), (row(ln_emb_b), 0)])
    proj, g_uq, g_ukv, g_mkv, g_wb, g_wo = _fused_mm(
        "proj", "nt", [[(h0b, 0, win_pt, 0)]], S, PW, D, _rtile(S, 1024), _tile(PW, 1536), D, [(PW, F32, 0, None)], ident,
        xchg=_Xchg(big_wb[1:6], False), msplit=2 if S % 2048 == 0 else 1)
    wuq_p =jnp.concatenate([g_uq[..., :MLA_NOPE], _pad_cols(g_uq[..., MLA_NOPE:], LANE),
                             _pad_cols(_rot(g_uq[..., MLA_NOPE:]), LANE)], axis=-1)
    wukv = g_ukv
    wmkv = g_mkv.reshape(-1, g_mkv.shape[-1])
    wb = jnp.transpose(g_wb.reshape(N_DEV, 3, BR_W, -1), (1, 2, 0, 3)).reshape(3, BR_W, D)
    wo = g_wo.reshape(-1, D)
    o_f, st_f = _gla_fwd(proj, lbl4, OFF_FF, False, "gla_fwd_f")
    o_b, st_b = _gla_fwd(proj, lbl4, OFF_FB, True, "gla_fwd_b")
    y_hg = _hgrn_post_fwd(o_f, o_b, proj, hgrn_norm_g)
    q_cat, k_cat, v_mla, vt_mla, cqn, ckvn = _mla_up(proj, cf, sf, mla_g_cq, mla_g_ckv, wuq_p, wukv)
    y_mla, ot, lse, g_wg, g_wu, g_wd = _mla_attn_fwd(q_cat, k_cat, vt_mla, _Xchg(big_wb[6:9], False))
    wg_t, wu_t = g_wg.reshape(F, D), g_wu.reshape(F, D)
    wd = g_wd.reshape(-1, D)
    memb = mem[0].astype(CDT)
    (memkv,) = _fused_mm("mem_kv", "nn", [[(memb, 0, wmkv, 0)]], Mm, 2 * MEM_W, D, Mm, _tile(2 * MEM_W, 512), D,
                         [(2 * MEM_W, CDT, 0, None)], ident)
    y_mem = _mem_attn_fwd(proj, memkv)
    ys = [y_hg, y_mla, y_mem]
    tnD = _tile(D, 1024, OFF_GATE)

    def epi_branch(accs, tiles, rows):
        return [_sigmoid(tiles[0]) * accs[0] + _sigmoid(tiles[1]) * accs[1] + _sigmoid(tiles[2]) * accs[2]], []

    (merged,) = _fused_mm("branch_fwd", "nn", [[(ys[b], 0, wb[b], 0)] for b in range(3)], S, D, BR_W, tm512, tnD, BR_W,
                          [(D, CDT, 0, None)], epi_branch, tiles=[(proj, OFF_GATE + b * D) for b in range(3)])

    def epi_ln1(accs, tiles, rows):
        r1v = ALPHA * tiles[0] + accs[0]
        return [r1v, _ln_stats(r1v)[0] * rows[0] + rows[1]], []

    r1, h1b = _fused_mm("wo_ln1", "nn", [[(merged, 0, wo, 0)]], S, D, D, tm512, D, D,
                        [(D, F32, 0, None), (D, CDT, 0, None)], epi_ln1, tiles=[(h0, 0)], rows=[(ln1_g, 0), (ln1_b, 0)])
    tnF = _tile(F, 512)

    def epi_up(accs, tiles, rows):
        gp, up = accs
        return [gp, up, gp * _sigmoid(gp) * up], []

    tm1k, ms1k = _rtile(S, 1024), (2 if S % 2048 == 0 else 1)
    gpb, upb, act = _fused_mm("ffn_up", "nt", [[(h1b, 0, wg_t, 0)], [(h1b, 0, wu_t, 0)]], S, F, D, tm1k, tnF, D,
                              [(F, CDT, 0, None)] * 3, epi_up, msplit=ms1k)

    def epi_down(accs, tiles, rows):
        g1, b1, g2, b2 = rows
        h1 = _ln_stats(tiles[0])[0] * g1 + b1
        xh2, rstd2 = _ln_stats(ALPHA * h1 + accs[0])
        diff = xh2 * g2 + b2 - tiles[1]
        dh2 = diff * (1.0 / D)
        dr2v = _ln_bwd(dh2, xh2, rstd2, g2)
        return [dr2v, dr2v], [dh2 * xh2, dh2, diff * diff * (0.5 / D)]

    acc_first = lambda epi: (lambda accs, tiles, rows: epi([tiles[0]], tiles[1:], rows))
    tm256 = _rtile(S, 256)
    (ff,) = _fused_mm("ffn_down", "nn", [[(act, 0, wd, 0)]], S, D, F, tm1k, _tile(D, 512), F, [(D, F32, 0, None)], ident,
                      msplit=ms1k)
    dr2, dr2b, dg2, db2, lossp = _fused_mm(
        "ffn_ln2_loss", "nn", [], S, D, 1, tm256, D, 1, [(D, F32, 0, None), (D, CDT, 0, None)], acc_first(epi_down),
        tiles=[(ff, 0), (r1, 0), (tgt, 0)], rows=[(ln1_g, 0), (ln1_b, 0), (ln2_g, 0), (ln2_b, 0)], n_racc=3)

    def epi_dact(accs, tiles, rows):
        da, gp, up = accs[0], tiles[0].astype(F32), tiles[1].astype(F32)
        s = _sigmoid(gp)
        return [da * up * (s * (1.0 + gp * (1.0 - s))), da * (gp * s)], []

    dgp, dup = _fused_mm("ffn_dact", "nt", [[(dr2b, 0, wd, 0)]], S, F, D, tm1k, tnF, D, [(F, CDT, 0, None)] * 2,
                         epi_dact, tiles=[(gpb, 0), (upb, 0)], msplit=ms1k)
    tkS = _rtile(S, 2048)
    (d_wd,) = _fused_mm("dw_down", "tn", [[(act, 0, dr2b, 0)]], F, D, S, tnF, D, tkS, [(D, CDT, 0, None)], ident)
    d_wg_t, d_wu_t = _fused_mm("dw_gate_up", "tn", [[(dgp, 0, h1b, 0)], [(dup, 0, h1b, 0)]], F, D, S, tnF, _tile(D, 1024),
                               tkS, [(D, CDT, 0, None)] * 2, lambda accs, tiles, rows: (accs, []))

    def epi_dh1(accs, tiles, rows):
        dh1 = accs[0] + ALPHA * tiles[0]
        xh1, rstd1 = _ln_stats(tiles[1])
        dr1v = _ln_bwd(dh1, xh1, rstd1, rows[0])
        return [dr1v, dr1v], [dh1 * xh1, dh1]

    rows8 = lambda dw: dw.reshape(N_DEV, -1, dw.shape[-1])
    (dh1_acc,) = _fused_mm("dh1", "nn", [[(dgp, 0, wg_t, 0), (dup, 0, wu_t, 0)]], S, D, F, tm512, _tile(D, 512), F,
                           [(D, F32, 0, None)], ident)
    dr1, dr1b, dg1, db1 = _fused_mm(
        "dh1_ln1", "nn", [], S, D, 1, tm256, D, 1, [(D, F32, 0, None), (D, CDT, 0, None)], acc_first(epi_dh1),
        tiles=[(dh1_acc, 0), (dr2, 0), (r1, 0)], rows=[(ln1_g, 0)], n_racc=2)
    (dmerged,) = _fused_mm("dmerged", "nt", [[(dr1b, 0, wo, 0)]], S, D, D, tm512, D, D, [(D, CDT, 0, None)], ident)
    (d_wo,) = _fused_mm("dw_o", "tn", [[(merged, 0, dr1b, 0)]], D, D, S, _tile(D, 512), D, tkS, [(D, CDT, 0, None)], ident)

    def epi_dbranch(accs, tiles, rows):
        dm, s = tiles[0].astype(F32), _sigmoid(tiles[1])
        return [dm * s, dm * accs[0] * s * (1.0 - s)], []

    dproj = None
    d_wbs, dys = [], []
    for b in range(3):
        du, dproj = _fused_mm(f"branch_bwd{b}", "nn", [[(ys[b], 0, wb[b], 0)]], S, D, BR_W, tm512, tnD, BR_W,
                              [(D, CDT, 0, None), (PW, CDT, OFF_GATE + b * D, dproj)], epi_dbranch,
                              tiles=[(dmerged, 0), (proj, OFF_GATE + b * D)])
        (dwb,) = _fused_mm(f"dw_branch{b}", "tn", [[(ys[b], 0, du, 0)]], BR_W, D, S, _tile(BR_W, 512), D, tkS,
                           [(D, CDT, 0, None)], ident)
        (dyb,) = _fused_mm(f"dy_branch{b}", "nt", [[(du, 0, wb[b], 0)]], S, BR_W, D, tm512, BR_W, D,
                           [(BR_W, F32 if b == 0 else CDT, 0, None)], ident)
        d_wbs.append(dwb)
        dys.append(dyb)
    dy_hg, dy_mla, dy_mem = dys

    dproj, dk_mem, dv_mem = _mem_attn_bwd(dy_mem, proj, memkv, dproj)
    dkv_mem = jnp.concatenate([dk_mem, dv_mem], axis=1).astype(CDT)
    (d_wmkv,) = _fused_mm("dw_memkv", "tn", [[(memb, 0, dkv_mem, 0)]], D, 2 * MEM_W, Mm, _tile(D, 512), 2 * MEM_W, Mm,
                          [(2 * MEM_W, CDT, 0, None)], ident)

    delta = _mla_delta(dy_mla, ot)
    lse_b = lse.reshape(delta.shape)
    dk_cat, dv_h, dq_cat, r_wg, r_wu, r_wd = _mla_attn_bwd(
        q_cat, k_cat, v_mla, dy_mla, lse_b, delta, _Xchg([rows8(d_wg_t), rows8(d_wu_t), rows8(d_wd)], True))
    dproj, dqp, dkvp, dgq, dgkv = _mla_up_bwd(dq_cat, dk_cat, dv_h, proj, cf, sf, mla_g_cq, mla_g_ckv, wuq_p, wukv, dproj)
    heads_major = lambda dw: jnp.transpose(dw.reshape(MLA_RANK, MLA_HEADS, -1), (1, 0, 2))
    (d_wuq_all,) = _fused_mm("dw_uq", "tn", [[(cqn, 0, dqp, 0)]], MLA_RANK, dqp.shape[1], S, MLA_RANK,
                             _tile(dqp.shape[1], 1536), tkS, [(dqp.shape[1], F32, 0, None)], ident)
    (d_wukv_all,) = _fused_mm("dw_ukv", "tn", [[(ckvn, 0, dkvp, 0)]], MLA_RANK, dkvp.shape[1], S, MLA_RANK,
                              _tile(dkvp.shape[1], 1024), tkS, [(dkvp.shape[1], CDT, 0, None)], ident)
    d_wuq_p, d_wukv = heads_major(d_wuq_all), heads_major(d_wukv_all)
    d_wuq = jnp.concatenate([d_wuq_p[..., :MLA_NOPE],
                             d_wuq_p[..., LANE:LANE + MLA_ROPE] + _unrot(d_wuq_p[..., 2 * LANE:2 * LANE + MLA_ROPE])],
                            axis=-1).astype(CDT)

    do_hg, dproj, dng = _hgrn_post_bwd(dy_hg, o_f, o_b, proj, hgrn_norm_g, dproj)
    dproj, dq1, di1, dl_f = _gla_bwd(proj, lbl4, OFF_FF, False, do_hg, st_f, dproj, None, "gla_bwd_f")
    dproj, _, _, dl_b = _gla_bwd(proj, lbl4, OFF_FB, True, do_hg, st_b, dproj, (dq1, di1), "gla_bwd_b")

    def epi_dh0(accs, tiles, rows):
        dh0 = accs[0] + ALPHA * tiles[0]
        xh, rstd = _ln_stats(tiles[1])
        return [_ln_bwd(dh0, xh, rstd, rows[0])], [dh0 * xh, dh0]

    d_wb = jnp.transpose(jnp.stack(d_wbs).reshape(3, BR_W, N_DEV, -1), (2, 0, 1, 3)).reshape(N_DEV, 3 * BR_W, -1)
    d_win_pt, r_uq, r_ukv, r_mkv, r_wb, r_wo = _fused_mm(
        "dw_in", "tn", [[(dproj, 0, h0b, 0)]], PW, D, S, _tile(PW, 1536), _tile(D, 1024), tkS, [(D, CDT, 0, None)], ident,
        xchg=_Xchg([d_wuq, d_wukv, rows8(d_wmkv), d_wb, rows8(d_wo)], True))
    d_kr = (d_win_pt[KR:KR + MLA_ROPE].astype(F32) + _unrot(d_win_pt[KR + LANE:KR + LANE + MLA_ROPE].astype(F32), 0)).astype(CDT)
    d_win_t = jnp.concatenate([d_win_pt[:OFF_FB], d_win_pt[OFF_FF:OFF_G], d_win_pt[OFF_FB:OFF_FF],
                               d_win_pt[OFF_G:OFF_QM], d_kr, d_win_pt[OFF_QM:KR]], axis=0)
    grad_x, dge, dbe, r_in = _fused_mm(
        "dh0_ln_emb", "nn", [[(dproj, 0, win_pt, 0)]], S, D, PW, tm512, D, _tile(PW, 1536), [(D, F32, 0, None)], epi_dh0,
        tiles=[(dr1, 0), (x2, 0)], rows=[(row(ln_emb_g), 0)], n_racc=2, xchg=_Xchg([rows8(d_win_t)], True))

    recv = [r_in, r_uq, r_ukv, r_mkv, r_wb, r_wo, r_wg, r_wu, r_wd]
    names = ["w_in", "w_uq", "w_ukv", "mem_w_kv", "w_branch", "w_o", "w_gate", "w_up", "w_down"]
    big_out = [_adam_big("adam_" + nme, r, w, m_, v_) for nme, r, w, m_, v_ in zip(names, recv, big_w, big_m, big_v)]

    parts = [dge, dbe, dng, dgq, dgkv, dg1, db1, dg2, db2, dl_f, dl_b, lossp]
    widths = [p.shape[1] for p in parts]
    red = _small_allreduce(jnp.concatenate(parts, axis=1))[0]
    offs = [sum(widths[:i]) for i in range(len(widths))]
    rs = [red[o:o + w_] for o, w_ in zip(offs, widths)]
    g_le_g, g_le_b, g_ng, g_gq, g_gkv, g_l1g, g_l1b, g_l2g, g_l2b, g_dlf, g_dlb, g_loss = rs
    loss = jnp.sum(g_loss)
    g_ng = g_ng.reshape(HG_HEADS, HG_DK).sum(axis=0)
    dl0 = jnp.stack([g_dlf, g_dlb])
    g_lb_full = jnp.stack([dl0, -dl0], axis=1)
    lbw = hgrn_lb_logits.shape[2]
    g_lb = lax.dynamic_slice_in_dim(g_lb_full, me * lbw, lbw, axis=2)

    small_g = [g_le_g, g_le_b, g_lb, g_ng.reshape(1, -1), g_gq.reshape(1, -1), g_gkv.reshape(1, -1), g_l1g.reshape(1, -1),
               g_l1b.reshape(1, -1), g_l2g.reshape(1, -1), g_l2b.reshape(1, -1)]
    small_w = [ln_emb_g, ln_emb_b, hgrn_lb_logits, hgrn_norm_g, mla_g_cq, mla_g_ckv, ln1_g, ln1_b, ln2_g, ln2_b]
    small_m = [m_ln_emb_g, m_ln_emb_b, m_hgrn_lb_logits, m_hgrn_norm_g, m_mla_g_cq, m_mla_g_ckv, m_ln1_g, m_ln1_b, m_ln2_g, m_ln2_b]
    small_v = [v_ln_emb_g, v_ln_emb_b, v_hgrn_lb_logits, v_hgrn_norm_g, v_mla_g_cq, v_mla_g_ckv, v_ln1_g, v_ln1_b, v_ln2_g, v_ln2_b]
    small_g = [g.reshape(w.shape) for g, w in zip(small_g, small_w)]
    pack = lambda lst: jnp.concatenate([a.reshape(-1) for a in lst]).reshape(1, -1)
    s_delta, s_nm, s_nv = _adam_small(pack(small_g), pack(small_w), pack(small_m), pack(small_v))
    sizes = [w.size for w in small_w]
    soffs = [sum(sizes[:i]) for i in range(len(sizes))]
    unpack = lambda p: [p[0, o:o + n].reshape(w.shape) for o, n, w in zip(soffs, sizes, small_w)]
    s_delta, s_nm, s_nv = unpack(s_delta), unpack(s_nm), unpack(s_nv)

    def ordered(small, big):
        sm = list(small)
        big = [tp(b) if n in transposed else b for n, b in enumerate(big)]
        bg = [b.reshape(w.shape) for b, w in zip(big, [w_in, mla_w_uq, mla_w_ukv, mem_w_kv, w_branch, w_o, w_ffn_gate, w_ffn_up, w_ffn_down])]
        return [sm[0], sm[1], sm[2], bg[0], sm[3], sm[4], sm[5], bg[1], bg[2], bg[3], bg[4], bg[5], sm[6], sm[7], bg[6], bg[7], bg[8], sm[8], sm[9]]

    grads = ordered(small_g, [o[0] for o in big_out])
    deltas = ordered(s_delta, [o[1] for o in big_out])
    new_m = ordered(s_nm, [o[2] for o in big_out])
    new_v = ordered(s_nv, [o[3] for o in big_out])
    return (loss, grad_x[None], *grads, *deltas, *new_m, *new_v)
```

```python
import functools

import jax
import jax.numpy as jnp
from jax import lax
from jax.experimental import pallas as pl
from jax.experimental.pallas import tpu as pltpu

F32 = jnp.float32
CDT = jnp.bfloat16
MESH = pl.DeviceIdType.MESH
N_DEV = 8
V7X_VMEM_LIMIT = 60 * 1024 * 1024
LANE = 128
SUB = 8

HG_HEADS, HG_DK, HG_CHUNK = 8, 128, 64
HG_HPS = 8
HG_BWD_GROUP = 8
HG_W = HG_HEADS * HG_DK
MLA_HEADS, MLA_RANK, MLA_NOPE, MLA_ROPE, MLA_V = 8, 512, 128, 64, 128
MLA_QK = MLA_NOPE + MLA_ROPE
MLA_SCALE = MLA_QK ** -0.5
MLA_QSCALE = MLA_SCALE * 1.4426950408889634
VT_ROWS = LANE + 16
MLA_TQ = 1024
MLA_BWD_SLABS = 1
MLA_TQ_FWD = 2048
MLA_FWD_SLABS = 8
MLA_W = MLA_HEADS * MLA_V
MEM_HEADS, MEM_HD = 4, 256
MEM_W = MEM_HEADS * MEM_HD
BR_W = 1024
ROPE_THETA = 10000.0
ALPHA = 2.0 ** 0.25
LN_EPS = 1e-5
RMS_EPS = 1e-6
ADAM_LR, ADAM_B1, ADAM_B2, ADAM_EPS, ADAM_WD, ADAM_STEP = 0.001, 0.9, 0.999, 1e-08, 0.01, 10
ADAM_BLOCK_ELEMS = 256 * 1024

OFF_Q, OFF_I, OFF_FB, OFF_FF, OFF_G = 0, 1024, 2048, 3072, 4096
OFF_CQ, OFF_CKV, OFF_QM, OFF_GATE = 5120, 5632, 6144, 7168
KR_PAD = 512


def _cparams(n_grid, side_effects=False):
    return pltpu.CompilerParams(dimension_semantics=("arbitrary",) * n_grid, vmem_limit_bytes=V7X_VMEM_LIMIT,
                                has_side_effects=side_effects)


def _tile(n, pref, *offsets):
    if n <= pref and all(o % n == 0 for o in offsets):
        return n
    t = (min(pref, n) // LANE) * LANE
    while t >= LANE:
        if n % t == 0 and all(o % t == 0 for o in offsets):
            return t
        t -= LANE
    raise ValueError(f"no tile for {n} {pref} {offsets}")


def _rtile(n, pref):
    if n <= pref:
        return n
    t = (pref // SUB) * SUB
    while t >= SUB:
        if n % t == 0:
            return t
        t -= SUB
    raise ValueError(f"no row tile for {n} {pref}")


def _dot(a, b, dims):
    return lax.dot_general(a.astype(CDT), b.astype(CDT), (dims, ((), ())), preferred_element_type=F32)


def _nn(a, b):
    return _dot(a, b, ((1,), (0,)))


def _nt(a, b):
    return _dot(a, b, ((1,), (1,)))


def _tn(a, b):
    return _dot(a, b, ((0,), (0,)))


_DOTS = {"nn": _nn, "nt": _nt, "tn": _tn}


def _sigmoid(x):
    return 1.0 / (1.0 + jnp.exp(-x))


def _rowsum8(v):
    r, w = v.shape
    return v.reshape(r // SUB, SUB, w).sum(axis=0)


def _my_place():
    x, y, c = lax.axis_index("x"), lax.axis_index("y"), lax.axis_index("c")
    return x, y, c, 4 * x + 2 * y + c


def _peer(x, y, c, kk):
    px = 1 - x if kk & 4 else x
    py = 1 - y if kk & 2 else y
    pc = 1 - c if kk & 1 else c
    return (px, py, pc), 4 * px + 2 * py + pc


class _Xchg:
    def __init__(self, arrs, scatter):
        self.arrs, self.scatter, self.n = list(arrs), scatter, len(arrs)
        hbm = pl.BlockSpec(memory_space=pl.ANY)
        self.specs = [hbm] * self.n
        self.out_shape = [jax.ShapeDtypeStruct(((N_DEV,) + a.shape[1:]) if scatter else ((N_DEV,) + a.shape), a.dtype)
                          for a in self.arrs]
        ncp = self.n * (N_DEV - 1)
        self.scratch = [pltpu.SemaphoreType.DMA((ncp,)), pltpu.SemaphoreType.DMA((ncp,)), pltpu.SemaphoreType.DMA((self.n,))]

    def _copies(self, ins, outs, send, recv, loc):
        x, y, c, me = _my_place()
        copies = []
        for w in range(self.n):
            copies.append(pltpu.make_async_copy(ins[w].at[me] if self.scatter else ins[w], outs[w].at[me], loc.at[w]))
            for kk in range(1, N_DEV):
                peer, pid = _peer(x, y, c, kk)
                s = w * (N_DEV - 1) + kk - 1
                copies.append(pltpu.make_async_remote_copy(
                    src_ref=ins[w].at[pid] if self.scatter else ins[w], dst_ref=outs[w].at[me],
                    send_sem=send.at[s], recv_sem=recv.at[s], device_id=peer, device_id_type=MESH))
        return copies

    def start(self, ins, outs, sems):
        for cp in self._copies(ins, outs, *sems):
            cp.start()

    def wait(self, ins, outs, sems):
        for cp in self._copies(ins, outs, *sems):
            cp.wait()


def _all_gather_two_level(name, arrs):
    n = len(arrs)
    NC = N_DEV - 1

    def body(*refs):
        ins, outs = refs[:n], refs[n:2 * n]
        send, recv, loc = refs[2 * n:]
        x, y, c, me = _my_place()
        sibling = (x, y, 1 - c)
        chips = [(1 - x, y), (x, 1 - y), (1 - x, 1 - y)]
        slot = lambda px, py, pc: 4 * px + 2 * py + pc

        def copy(w, k, block, to, src=None):
            dst = outs[w].at[slot(*block)]
            return pltpu.make_async_remote_copy(src_ref=dst if src is None else src, dst_ref=dst,
                                                send_sem=send.at[w * NC + k], recv_sem=recv.at[w * NC + k],
                                                device_id=to, device_id_type=MESH)

        mine = [pltpu.make_async_copy(ins[w], outs[w].at[me], loc.at[w]) for w in range(n)]
        for cp in mine:
            cp.start()
        first = []
        for w in range(n):
            first.append(copy(w, 0, (x, y, c), sibling, src=ins[w]))
            first += [copy(w, 1 + j, (x, y, c), (*chip, c), src=ins[w]) for j, chip in enumerate(chips)]
        for cp in first:
            cp.start()
        passed = []
        for j, chip in enumerate(chips):
            for w in range(n):
                copy(w, 1 + j, (*chip, c), (x, y, c)).wait_recv()
                fwd = copy(w, 4 + j, (*chip, c), sibling)
                fwd.start()
                passed.append(fwd)
        for w in range(n):
            copy(w, 0, sibling, (x, y, c)).wait_recv()
            for j, chip in enumerate(chips):
                copy(w, 4 + j, (*chip, 1 - c), (x, y, c)).wait_recv()
        for cp in first + passed:
            cp.wait_send()
        for cp in mine:
            cp.wait()

    hbm = pl.BlockSpec(memory_space=pl.ANY)
    return pl.pallas_call(
        body,
        name=name,
        in_specs=[hbm] * n,
        out_specs=[hbm] * n,
        out_shape=[jax.ShapeDtypeStruct((N_DEV,) + a.shape, a.dtype) for a in arrs],
        scratch_shapes=[pltpu.SemaphoreType.DMA((n * NC,)), pltpu.SemaphoreType.DMA((n * NC,)), pltpu.SemaphoreType.DMA((n,))],
        compiler_params=pltpu.CompilerParams(has_side_effects=True),
    )(*arrs)


def _fused_mm(name, mode, groups, M, N, K, tm, tn, tk, outs, epi, tiles=(), rows=(), n_racc=0, xchg=None, msplit=1):
    ni, nj, nk = M // tm, N // tn, K // tk
    assert M % tm == 0 and N % tn == 0 and K % tk == 0, (name, M, N, K, tm, tn, tk)
    assert n_racc == 0 or nj == 1
    assert msplit == 1 or (nk == 1 and n_racc == 0 and tm % (16 * msplit) == 0)
    dot = _DOTS[mode] if groups else None
    ins, in_specs = [], []
    for g in groups:
        for a, a_off, b, b_off in g:
            if mode == "tn":
                assert a_off % tm == 0
                in_specs.append(pl.BlockSpec((tk, tm), lambda i, j, k, o=a_off // tm: (k, i + o)))
            else:
                assert a_off % tk == 0
                in_specs.append(pl.BlockSpec((tm, tk), lambda i, j, k, o=a_off // tk: (i, k + o)))
            ins.append(a)
            if mode == "nt":
                assert b_off % tk == 0
                in_specs.append(pl.BlockSpec((tn, tk), lambda i, j, k, o=b_off // tk: (j, k + o)))
            else:
                assert b_off % tn == 0
                in_specs.append(pl.BlockSpec((tk, tn), lambda i, j, k, o=b_off // tn: (k, j + o)))
            ins.append(b)
    for arr, off in tiles:
        assert off % tn == 0
        ins.append(arr)
        in_specs.append(pl.BlockSpec((tm, tn), lambda i, j, k, o=off // tn: (i, j + o)))
    for arr, off in rows:
        assert off % tn == 0
        ins.append(arr)
        in_specs.append(pl.BlockSpec((1, tn), lambda i, j, k, o=off // tn: (0, j + o)))
    aliases = {}
    out_shape, out_specs = [], []
    for oi, (width, dtype, off, alias) in enumerate(outs):
        assert off % tn == 0
        if alias is not None:
            aliases[len(ins)] = oi
            ins.append(alias)
            in_specs.append(pl.BlockSpec(memory_space=pl.ANY))
        out_shape.append(jax.ShapeDtypeStruct((M, width), dtype))
        out_specs.append(pl.BlockSpec((tm, tn), lambda i, j, k, o=off // tn: (i, j + o)))
    for _ in range(n_racc):
        out_shape.append(jax.ShapeDtypeStruct((SUB, N), F32))
        out_specs.append(pl.BlockSpec((SUB, tn), lambda i, j, k: (0, 0)))
    n_alias = len(aliases)
    n_pairs = [len(g) for g in groups]
    use_scratch = nk > 1
    scratch = [pltpu.VMEM((tm, tn), F32) for _ in groups] if use_scratch else []
    nx = 0
    if xchg is not None:
        nx = xchg.n
        ins += xchg.arrs
        in_specs += xchg.specs
        out_shape += xchg.out_shape
        out_specs += xchg.specs
        scratch += xchg.scratch

    def body(*refs):
        it = iter(refs)
        pair_refs = [[(next(it), next(it)) for _ in range(n)] for n in n_pairs]
        tile_refs = [next(it) for _ in tiles]
        row_refs = [next(it) for _ in rows]
        for _ in range(n_alias):
            next(it)
        x_in = [next(it) for _ in range(nx)]
        out_refs = [next(it) for _ in outs]
        racc_refs = [next(it) for _ in range(n_racc)]
        x_out = [next(it) for _ in range(nx)]
        acc_refs = [next(it) for _ in groups] if use_scratch else []
        x_sems = list(it)
        i, j, k = pl.program_id(0), pl.program_id(1), pl.program_id(2)
        if nx:
            @pl.when((i == 0) & (j == 0) & (k == 0))
            def _():
                xchg.start(x_in, x_out, x_sems)

        def products():
            res = []
            for prs in pair_refs:
                s = None
                for a_ref, b_ref in prs:
                    d = dot(a_ref[...], b_ref[...])
                    s = d if s is None else s + d
                res.append(s)
            return res

        def finish(accs):
            out_v, racc_v = epi(accs, [t[...] for t in tile_refs], [r[...] for r in row_refs])
            for o_ref, v in zip(out_refs, out_v):
                o_ref[...] = v.astype(o_ref.dtype)
            for r_ref, v in zip(racc_refs, racc_v):
                part = _rowsum8(v)

                @pl.when(i == 0)
                def _():
                    r_ref[...] = part

                @pl.when(i > 0)
                def _():
                    r_ref[...] += part

        if not use_scratch and msplit > 1:
            ts = tm // msplit
            for s in range(msplit):
                rs = pl.ds(s * ts, ts)
                accs = []
                for prs in pair_refs:
                    acc = None
                    for a_ref, b_ref in prs:
                        dd = dot(a_ref[:, rs] if mode == "tn" else a_ref[rs, :], b_ref[...])
                        acc = dd if acc is None else acc + dd
                    accs.append(acc)
                out_v, _ = epi(accs, [t[rs, :] for t in tile_refs], [r[...] for r in row_refs])
                for o_ref, v in zip(out_refs, out_v):
                    o_ref[rs, :] = v.astype(o_ref.dtype)
        elif not use_scratch:
            finish(products())
        else:
            @pl.when(k == 0)
            def _():
                for acc in acc_refs:
                    acc[...] = jnp.zeros_like(acc)

            for acc, p in zip(acc_refs, products()):
                acc[...] += p

            @pl.when(k == nk - 1)
            def _():
                finish([acc[...] for acc in acc_refs])

        if nx:
            @pl.when((i == ni - 1) & (j == nj - 1) & (k == nk - 1))
            def _():
                xchg.wait(x_in, x_out, x_sems)

    res = pl.pallas_call(
        body,
        name=name,
        grid=(ni, nj, nk),
        in_specs=in_specs,
        out_specs=out_specs,
        out_shape=out_shape,
        scratch_shapes=scratch,
        input_output_aliases=aliases,
        compiler_params=_cparams(3, side_effects=nx > 0),
    )(*ins)
    return res


def _ln_stats(r):
    mu = jnp.mean(r, axis=-1, keepdims=True)
    xc = r - mu
    var = jnp.mean(xc * xc, axis=-1, keepdims=True)
    rstd = lax.rsqrt(var + LN_EPS)
    return xc * rstd, rstd


def _ln_bwd(dh, xhat, rstd, g):
    dxh = dh * g
    m1 = jnp.mean(dxh, axis=-1, keepdims=True)
    m2 = jnp.mean(dxh * xhat, axis=-1, keepdims=True)
    return rstd * (dxh - m1 - xhat * m2)


def _split3(x):
    hi = x.astype(CDT)
    r1 = x - hi.astype(F32)
    mid = r1.astype(CDT)
    lo = (r1 - mid.astype(F32)).astype(CDT)
    return hi, mid, lo


def _tri_matmul(tri, x):
    hi, mid, lo = _split3(x)
    return _nn(tri, hi) + _nn(tri, mid) + _nn(tri, lo)


def _dot3(dot, a, b):
    a_hi, b_hi = a.astype(CDT), b.astype(CDT)
    a_lo = (a - a_hi.astype(F32)).astype(CDT)
    b_lo = (b - b_hi.astype(F32)).astype(CDT)
    return dot(a_hi, b_hi) + dot(a_hi, b_lo) + dot(a_lo, b_hi)


def _gla_masks(reverse):
    C = HG_CHUNK
    r = lax.broadcasted_iota(jnp.int32, (C, C), 0)
    c = lax.broadcasted_iota(jnp.int32, (C, C), 1)
    keep = (c >= r) if reverse else (r >= c)
    return keep


def _m(fn, *lists):
    return [fn(*args) for args in zip(*lists)]


def _gla_chunk_fwd(qraw, fraw, lb, keep, reverse):
    C = HG_CHUNK
    end = 0 if reverse else C - 1
    tri = jnp.where(keep, 1.0, 0.0).astype(CDT)
    sq = _m(_sigmoid, qraw)
    q = _m(lambda x, s: x * s, qraw, sq)
    sg = _m(_sigmoid, fraw)
    f = _m(lambda l_, s: l_ + (1.0 - l_) * s, lb, sg)
    k = _m(lambda x: 1.0 - x, f)
    g = _m(jnp.log, f)
    b = _m(lambda x: _tri_matmul(tri, x), g)
    b_end = _m(lambda x: x[end:end + 1, :], b)
    b_mid = _m(lambda x: x[C // 2:C // 2 + 1, :], b)
    eq = _m(lambda x, m_: jnp.exp(x - m_), b, b_mid)
    ek = _m(lambda x, m_: jnp.exp(m_ - x), b, b_mid)
    eb = _m(jnp.exp, b)
    e2 = _m(lambda x, e_: jnp.exp(e_ - x), b, b_end)
    e_end = _m(jnp.exp, b_end)
    qt = _m(lambda x, e_: x * e_, q, eq)
    kt = _m(lambda x, e_: x * e_, k, ek)
    qs = _m(lambda x, e_: (x * e_).astype(CDT), q, eb)
    k2 = _m(lambda x, e_: (x * e_).astype(CDT), k, e2)
    a = _m(lambda x, y: jnp.where(keep, _dot3(_nt, x, y), 0.0).astype(CDT), qt, kt)
    return dict(sq=sq, q=q, sg=sg, f=f, k=k, eq=eq, ek=ek, eb=eb, e2=e2, e_end=e_end, qt=qt, kt=kt, qs=qs, k2=k2, a=a)


def _gla_fwd(proj, lbl4, f_off, reverse, name):
    S = proj.shape[0]
    C = HG_CHUNK
    R = _rtile(S, 512)
    cpb, nblk = R // C, S // R
    d = 1 if reverse else 0
    blk_map = (lambda b: nblk - 1 - b) if reverse else (lambda b: b)

    W = HG_HPS * HG_DK

    def body(q_ref, i_ref, f_ref, lb_ref, o_ref, st_ref, s_scr):
        @pl.when(pl.program_id(1) == 0)
        def _():
            s_scr[...] = jnp.zeros_like(s_scr)

        l = lb_ref[...]
        lbs = _sigmoid(l[2 * d:2 * d + 1, :] - l[2 * d + 1:2 * d + 2, :])
        keep = _gla_masks(reverse)
        heads = list(range(HG_HPS))
        css = [pl.ds(hh * HG_DK, HG_DK) for hh in heads]
        lb = [lbs[:, hh * HG_DK:(hh + 1) * HG_DK] for hh in heads]
        for cc in range(cpb):
            c = cpb - 1 - cc if reverse else cc
            sl = pl.ds(c * C, C)
            v = [i_ref[sl, cs] for cs in css]
            t = _gla_chunk_fwd([q_ref[sl, cs] for cs in css], [f_ref[sl, cs] for cs in css], lb, keep, reverse)
            st = [s_scr[hh] for hh in heads]
            o = _m(lambda qs, s_, a, v_: _nt(qs, s_) + _nn(a, v_), t["qs"], st, t["a"], v)
            new = _m(lambda e_, s_, v_, k2: e_ * s_ + _tn(v_, k2), t["e_end"], st, v, t["k2"])
            for hh in heads:
                st_ref[c, hh] = st[hh]
                o_ref[sl, css[hh]] = o[hh]
                s_scr[hh] = new[hh]

    col = lambda off: (lambda h, b: (blk_map(b), off // W + h))
    return pl.pallas_call(
        body,
        name=name,
        grid=(HG_HEADS // HG_HPS, nblk),
        in_specs=[
            pl.BlockSpec((R, W), col(OFF_Q)),
            pl.BlockSpec((R, W), col(OFF_I)),
            pl.BlockSpec((R, W), col(f_off)),
            pl.BlockSpec((4, W), lambda h, b: (0, h)),
        ],
        out_specs=[
            pl.BlockSpec((R, W), lambda h, b: (blk_map(b), h)),
            pl.BlockSpec((cpb, HG_HPS, HG_DK, HG_DK), lambda h, b: (blk_map(b), h, 0, 0)),
        ],
        out_shape=[
            jax.ShapeDtypeStruct((S, HG_W), F32),
            jax.ShapeDtypeStruct((S // C, HG_HEADS, HG_DK, HG_DK), F32),
        ],
        scratch_shapes=[pltpu.VMEM((HG_HPS, HG_DK, HG_DK), F32)],
        compiler_params=_cparams(2),
    )(proj, proj, proj, lbl4)


def _gla_bwd(proj, lbl4, f_off, reverse, do, states, dproj, prev, name):
    S = proj.shape[0]
    PW = proj.shape[1]
    C = HG_CHUNK
    R = _rtile(S, 512)
    cpb, nblk = R // C, S // R
    d = 1 if reverse else 0
    blk_map = (lambda b: b) if reverse else (lambda b: nblk - 1 - b)
    final = prev is not None

    if final:
        assert HG_HPS == HG_HEADS and (OFF_Q, OFF_I, f_off) == (0, HG_W, 2 * HG_W)

    def body(*refs):
        if final:
            q_ref, i_ref, f_ref, lb_ref, do_ref, st_ref, pq_ref, pi_ref, _dp, o3_ref, dl_ref, ds_scr = refs
            dq_ref = di_ref = df_ref = o3_ref
        else:
            q_ref, i_ref, f_ref, lb_ref, do_ref, st_ref, dq_ref, di_ref, df_ref, dl_ref, ds_scr = refs
        out_off = (OFF_Q, OFF_I, f_off) if final else (0, 0, 0)
        blk = pl.program_id(1)

        @pl.when(blk == 0)
        def _():
            ds_scr[...] = jnp.zeros_like(ds_scr)
            dl_ref[...] = jnp.zeros_like(dl_ref)

        l = lb_ref[...]
        lbs = _sigmoid(l[2 * d:2 * d + 1, :] - l[2 * d + 1:2 * d + 2, :])
        keep = _gla_masks(reverse)
        keep_t = _gla_masks(not reverse)
        tri_t = jnp.where(keep_t, 1.0, 0.0).astype(CDT)
        end = 0 if reverse else C - 1
        is_end = lax.broadcasted_iota(jnp.int32, (C, HG_DK), 0) == end
        dl_all = [jnp.zeros((SUB, HG_DK), F32) for _ in range(HG_HPS)]
        gsz = HG_BWD_GROUP
        for cc, heads in [(cc, list(range(g0, g0 + gsz))) for cc in range(cpb) for g0 in range(0, HG_HPS, gsz)]:
            css = [pl.ds(hh * HG_DK, HG_DK) for hh in heads]
            lb = [lbs[:, hh * HG_DK:(hh + 1) * HG_DK] for hh in heads]
            dl_acc = [dl_all[hh] for hh in heads]
            c = cc if reverse else cpb - 1 - cc
            sl = pl.ds(c * C, C)
            qraw = [q_ref[sl, cs] for cs in css]
            v = [i_ref[sl, cs] for cs in css]
            t = _gla_chunk_fwd(qraw, [f_ref[sl, cs] for cs in css], lb, keep, reverse)
            dob = [do_ref[sl, cs].astype(CDT) for cs in css]
            vb = _m(lambda x: x.astype(CDT), v)
            st = [st_ref[c, hh] for hh in heads]
            ds = [ds_scr[hh] for hh in heads]
            dsb = _m(lambda x: x.astype(CDT), ds)
            d_qs = _m(_nn, dob, st)
            d_a = _m(lambda x, y: jnp.where(keep, _nt(x, y), 0.0), dob, vb)
            d_qt = _m(lambda x, y: _dot3(_nn, x, y), d_a, t["kt"])
            d_kt = _m(lambda x, y: _dot3(_tn, x, y), d_a, t["qt"])
            d_v = _m(lambda a, x, k2, s_: _tn(a, x) + _nt(k2, s_), t["a"], dob, t["k2"], dsb)
            d_k2 = _m(_nn, vb, dsb)
            d_e = _m(lambda s_, x: jnp.sum(s_ * x, axis=0, keepdims=True), st, ds)
            new_ds = _m(lambda e_, x, y, qs: e_ * x + _tn(y, qs), t["e_end"], ds, dob, t["qs"])
            dq = _m(lambda a, ea, b_, eb_: a * ea + b_ * eb_, d_qt, t["eq"], d_qs, t["eb"])
            dk = _m(lambda a, ea, b_, eb_: a * ea + b_ * eb_, d_kt, t["ek"], d_k2, t["e2"])
            db_end = _m(lambda x, k_, e2, de, ee: jnp.sum(x * (k_ * e2), axis=0, keepdims=True) + de * ee,
                        d_k2, t["k"], t["e2"], d_e, t["e_end"])
            db = _m(lambda q_, dq_, k_, dk_, be: q_ * dq_ - k_ * dk_ + jnp.where(is_end, be, 0.0),
                    t["q"], dq, t["k"], dk, db_end)
            dg = _m(lambda x: _tri_matmul(tri_t, x), db)
            df = _m(lambda g_, f_, dk_: g_ / f_ - dk_, dg, t["f"], dk)
            dfraw = _m(lambda x, l_, s_: x * (1.0 - l_) * s_ * (1.0 - s_), df, lb, t["sg"])
            dl_acc = _m(lambda acc, x, s_: acc + _rowsum8(x * (1.0 - s_)), dl_acc, df, t["sg"])
            dqraw = _m(lambda x, s_, r: x * (s_ * (1.0 + r * (1.0 - s_))), dq, t["sq"], qraw)
            if final:
                dqraw = [x + pq_ref[sl, cs] for x, cs in zip(dqraw, css)]
                d_v = [x + pi_ref[sl, cs] for x, cs in zip(d_v, css)]
            for n, hh in enumerate(heads):
                dl_all[hh] = dl_acc[n]
                ds_scr[hh] = new_ds[n]
                for ref, off, val in zip((dq_ref, di_ref, df_ref), out_off, (dqraw[n], d_v[n], dfraw[n])):
                    ref[sl, pl.ds(off + hh * HG_DK, HG_DK)] = val.astype(ref.dtype)
        dl_ref[...] += jnp.concatenate(dl_all, axis=1) * (lbs * (1.0 - lbs))

    W = HG_HPS * HG_DK
    col = lambda off: (lambda h, b: (blk_map(b), off // W + h))
    blk = lambda: pl.BlockSpec((R, W), lambda h, b: (blk_map(b), h))
    ins = [proj, proj, proj, lbl4, do, states]
    in_specs = [
        pl.BlockSpec((R, W), col(OFF_Q)),
        pl.BlockSpec((R, W), col(OFF_I)),
        pl.BlockSpec((R, W), col(f_off)),
        pl.BlockSpec((4, W), lambda h, b: (0, h)),
        blk(),
        pl.BlockSpec((cpb, HG_HPS, HG_DK, HG_DK), lambda h, b: (blk_map(b), h, 0, 0)),
    ]
    dl_shape = jax.ShapeDtypeStruct((SUB, HG_W), F32)
    dl_spec = pl.BlockSpec((SUB, W), lambda h, b: (0, h))
    dp_shape = jax.ShapeDtypeStruct((S, PW), CDT)
    if final:
        ins += [prev[0], prev[1], dproj]
        in_specs += [blk(), blk(), pl.BlockSpec(memory_space=pl.ANY)]
        out_shape = [dp_shape, dl_shape]
        out_specs = [pl.BlockSpec((R, 3 * HG_W), lambda h, b: (blk_map(b), 0)), dl_spec]
        aliases = {8: 0}
    else:
        out_shape = [jax.ShapeDtypeStruct((S, HG_W), F32), jax.ShapeDtypeStruct((S, HG_W), F32), dp_shape, dl_shape]
        out_specs = [blk(), blk(), pl.BlockSpec((R, W), col(f_off)), dl_spec]
        aliases = {}
        if dproj is not None:
            ins += [dproj]
            in_specs += [pl.BlockSpec(memory_space=pl.ANY)]
            aliases = {6: 2}
    if (not final) and dproj is not None:
        def body_wrapped(*refs, _b=body):
            _b(*refs[:6], *refs[7:])
        kern = body_wrapped
    else:
        kern = body
    res = pl.pallas_call(
        kern,
        name=name,
        grid=(HG_HEADS // HG_HPS, nblk),
        in_specs=in_specs,
        out_specs=out_specs,
        out_shape=out_shape,
        scratch_shapes=[pltpu.VMEM((HG_HPS, HG_DK, HG_DK), F32)],
        input_output_aliases=aliases,
        compiler_params=_cparams(2),
    )(*ins)
    if final:
        return res[0], None, None, res[1]
    dq, di, dproj, dl = res
    return dproj, dq, di, dl


def _hgrn_post_fwd(o_f, o_b, proj, norm_g):
    S = o_f.shape[0]

    def epi(accs, tiles, rows):
        of, ob, graw = tiles
        ng = rows[0][:, :HG_DK]
        o = of + ob
        ys = []
        for h in range(HG_HEADS):
            oh = o[:, h * HG_DK:(h + 1) * HG_DK]
            rs = lax.rsqrt(jnp.mean(oh * oh, axis=-1, keepdims=True) + RMS_EPS)
            ys.append(oh * rs * ng * _sigmoid(graw[:, h * HG_DK:(h + 1) * HG_DK]))
        return [jnp.concatenate(ys, axis=1)], []

    tm = _rtile(S, 512)
    (y,) = _fused_mm("hgrn_post_fwd", "nn", [], S, HG_W, 1, tm, HG_W, 1, [(HG_W, CDT, 0, None)], epi,
                     tiles=[(o_f, 0), (o_b, 0), (proj, OFF_G)], rows=[(jnp.tile(norm_g, (1, HG_HEADS)), 0)])
    return y


def _hgrn_post_bwd(dy, o_f, o_b, proj, norm_g, dproj):
    S = o_f.shape[0]

    def epi(accs, tiles, rows):
        dyv, of, ob, graw = tiles
        ng = rows[0][:, :HG_DK]
        o = of + ob
        dos, dgs, dns = [], [], []
        for h in range(HG_HEADS):
            sl = slice(h * HG_DK, (h + 1) * HG_DK)
            oh, gh, dyh = o[:, sl], graw[:, sl], dyv[:, sl].astype(F32)
            rs = lax.rsqrt(jnp.mean(oh * oh, axis=-1, keepdims=True) + RMS_EPS)
            xh = oh * rs
            sg = _sigmoid(gh)
            dn = dyh * sg
            dgs.append(dyh * (xh * ng) * sg * (1.0 - sg))
            dns.append(dn * xh)
            dxh = dn * ng
            dos.append(rs * (dxh - xh * jnp.mean(dxh * xh, axis=-1, keepdims=True)))
        return [jnp.concatenate(dos, axis=1), jnp.concatenate(dgs, axis=1)], [jnp.concatenate(dns, axis=1)]

    tm = _rtile(S, 512)
    do, dproj, dn = _fused_mm("hgrn_post_bwd", "nn", [], S, HG_W, 1, tm, HG_W, 1,
                              [(HG_W, F32, 0, None), (dproj.shape[1], CDT, OFF_G, dproj)], epi,
                              tiles=[(dy, 0), (o_f, 0), (o_b, 0), (proj, OFF_G)],
                              rows=[(jnp.tile(norm_g, (1, HG_HEADS)), 0)], n_racc=1)
    return do, dproj, dn


def _copy_into(name, src, dst, off):
    S, W = src.shape
    tm = _rtile(S, 512)
    (dst,) = _fused_mm(name, "nn", [], S, W, 1, tm, W, 1, [(dst.shape[1], dst.dtype, off, dst)],
                       lambda accs, tiles, rows: ([tiles[0]], []), tiles=[(src, 0)])
    return dst


def _rms_stats(x):
    rs = lax.rsqrt(jnp.mean(x * x, axis=-1, keepdims=True) + RMS_EPS)
    return x * rs, rs


def _mla_up(proj, cf, sf, g_cq, g_ckv, wuq_p, wukv):
    S = proj.shape[0]
    tm = _rtile(S, 512)
    H = MLA_HEADS

    def body(cq_ref, ckv_ref, kr_ref, krot_ref, cf_ref, sf_ref, gq_ref, gkv_ref, wq_ref, wkv_ref,
             q_ref, k_ref, v_ref, vt_ref, cqn_ref, ckvn_ref):
        cqn = (_rms_stats(cq_ref[...])[0] * gq_ref[...]).astype(CDT)
        ckvn = (_rms_stats(ckv_ref[...])[0] * gkv_ref[...]).astype(CDT)
        cqn_ref[...] = cqn
        ckvn_ref[...] = ckvn
        cfv, sfv = cf_ref[...], sf_ref[...]
        k_roped = (kr_ref[...] * cfv + krot_ref[...] * sfv).astype(CDT)
        ones = jnp.ones((VT_ROWS - LANE, tm), CDT)
        for h in range(H):
            r = _nn(cqn, wq_ref[h]) * MLA_QSCALE
            q_ref[h, :, 0:LANE] = r[:, 0:LANE].astype(CDT)
            q_ref[h, :, LANE:2 * LANE] = (r[:, LANE:2 * LANE] * cfv + r[:, 2 * LANE:3 * LANE] * sfv).astype(CDT)
            kv = _nn(ckvn, wkv_ref[h])
            k_ref[h, :, 0:LANE] = kv[:, 0:LANE].astype(CDT)
            k_ref[h, :, LANE:2 * LANE] = k_roped
            vv = kv[:, LANE:2 * LANE]
            v_ref[h] = vv.astype(CDT)
            vt_ref[h, 0, 0:LANE, :] = vv.T.astype(CDT)
            vt_ref[h, 0, LANE:VT_ROWS, :] = ones

    PWb = proj.shape[1]
    kr_off = PWb - KR_PAD
    cspec = lambda off, w: pl.BlockSpec((tm, w), lambda i, o=off // w: (i, o))
    return pl.pallas_call(
        body,
        name="mla_up_fwd",
        grid=(S // tm,),
        in_specs=[
            cspec(OFF_CQ, MLA_RANK), cspec(OFF_CKV, MLA_RANK), cspec(kr_off, LANE), cspec(kr_off + LANE, LANE),
            pl.BlockSpec((tm, LANE), lambda i: (i, 0)), pl.BlockSpec((tm, LANE), lambda i: (i, 0)),
            pl.BlockSpec((1, MLA_RANK), lambda i: (0, 0)), pl.BlockSpec((1, MLA_RANK), lambda i: (0, 0)),
            pl.BlockSpec((H, MLA_RANK, 3 * LANE), lambda i: (0, 0, 0)),
            pl.BlockSpec((H, MLA_RANK, 2 * LANE), lambda i: (0, 0, 0)),
        ],
        out_specs=[
            pl.BlockSpec((H, tm, 2 * LANE), lambda i: (0, i, 0)),
            pl.BlockSpec((H, tm, 2 * LANE), lambda i: (0, i, 0)),
            pl.BlockSpec((H, tm, LANE), lambda i: (0, i, 0)),
            pl.BlockSpec((H, 1, VT_ROWS, tm), lambda i: (0, i, 0, 0)),
            pl.BlockSpec((tm, MLA_RANK), lambda i: (i, 0)),
            pl.BlockSpec((tm, MLA_RANK), lambda i: (i, 0)),
        ],
        out_shape=[
            jax.ShapeDtypeStruct((H, S, 2 * LANE), CDT), jax.ShapeDtypeStruct((H, S, 2 * LANE), CDT),
            jax.ShapeDtypeStruct((H, S, LANE), CDT), jax.ShapeDtypeStruct((H, S // tm, VT_ROWS, tm), CDT),
            jax.ShapeDtypeStruct((S, MLA_RANK), CDT), jax.ShapeDtypeStruct((S, MLA_RANK), CDT),
        ],
        compiler_params=_cparams(1),
    )(proj, proj, proj, proj, cf, sf, g_cq, g_ckv, wuq_p, wukv)


def _mla_attn_fwd(q_cat, k_cat, vt, xchg=None):
    H, S, _ = q_cat.shape
    tq = _tile(S, MLA_TQ_FWD)
    _, nkb, _, tk = vt.shape
    nq = S // tq
    nx = xchg.n if xchg is not None else 0

    def body(*refs):
        q_ref, k_ref, vt_ref = refs[:3]
        x_in = refs[3:3 + nx]
        y_ref, ot_ref, lse_ref = refs[3 + nx:6 + nx]
        x_out = refs[6 + nx:6 + 2 * nx]
        m_scr, acc_scr = refs[6 + 2 * nx:8 + 2 * nx]
        x_sems = refs[8 + 2 * nx:]
        h, i = pl.program_id(0), pl.program_id(1)
        if nx:
            @pl.when((h == 0) & (i == 0))
            def _():
                xchg.start(x_in, x_out, x_sems)

        nsub = MLA_FWD_SLABS if tq % (MLA_FWD_SLABS * LANE) == 0 else 1
        ws = tq // nsub
        subs = [pl.ds(s * ws, ws) for s in range(nsub)]
        qs = [q_ref[0, sb, :] for sb in subs]
        m_scr[...] = jnp.full_like(m_scr, -jnp.inf)
        acc_scr[...] = jnp.zeros_like(acc_scr)

        def step(j, carry):
            kj = k_ref[0, pl.ds(pl.multiple_of(j * tk, tk), tk), :]
            vtj = vt_ref[0, j]
            sts = [_nt(kj, qq) for qq in qs]
            m_old = [m_scr[:, sb] for sb in subs]
            m_new = _m(lambda mo, st: jnp.maximum(mo, jnp.max(st, axis=0, keepdims=True)), m_old, sts)
            pts = _m(lambda st, mn: jnp.exp2(st - mn), sts, m_new)
            pvs = _m(lambda pt: _nn(vtj, pt), pts)
            for sb, mo, mn, pv in zip(subs, m_old, m_new, pvs):
                acc_scr[:, sb] = jnp.exp2(mo - mn) * acc_scr[:, sb] + pv
                m_scr[:, sb] = mn
            return carry

        lax.fori_loop(0, nkb, step, 0, unroll=4 if nkb % 4 == 0 else 1)
        l = acc_scr[LANE:LANE + 1, :]
        ot = acc_scr[0:LANE, :] / l
        ot_ref[0] = ot
        y_ref[...] = ot.T.astype(CDT)
        lse_ref[0, 0] = m_scr[...] + jnp.log2(l)

        if nx:
            @pl.when((h == H - 1) & (i == nq - 1))
            def _():
                xchg.wait(x_in, x_out, x_sems)

    return pl.pallas_call(
        body,
        name="mla_attn_fwd",
        grid=(H, nq),
        in_specs=[
            pl.BlockSpec((1, tq, 2 * LANE), lambda h, i: (h, i, 0)),
            pl.BlockSpec((1, S, 2 * LANE), lambda h, i: (h, 0, 0)),
            pl.BlockSpec((1, nkb, VT_ROWS, tk), lambda h, i: (h, 0, 0, 0)),
        ] + (xchg.specs if nx else []),
        out_specs=[
            pl.BlockSpec((tq, LANE), lambda h, i: (i, h)),
            pl.BlockSpec((1, LANE, tq), lambda h, i: (h, 0, i)),
            pl.BlockSpec((1, 1, 1, tq), lambda h, i: (h, i, 0, 0)),
        ] + (xchg.specs if nx else []),
        out_shape=[
            jax.ShapeDtypeStruct((S, H * LANE), CDT),
            jax.ShapeDtypeStruct((H, LANE, S), F32),
            jax.ShapeDtypeStruct((H, nq, 1, tq), F32),
        ] + (xchg.out_shape if nx else []),
        scratch_shapes=[pltpu.VMEM((1, tq), F32), pltpu.VMEM((VT_ROWS, tq), F32)] + (xchg.scratch if nx else []),
        compiler_params=_cparams(2, side_effects=nx > 0),
    )(q_cat, k_cat, vt, *(xchg.arrs if nx else []))


def _mla_delta(dy, ot):
    H, _, S = ot.shape
    tq = _tile(S, MLA_TQ)
    nq = S // tq

    def body(dy_ref, ot_ref, d_ref):
        d_ref[0, 0] = jnp.sum(dy_ref[...].astype(F32).T * ot_ref[0], axis=0, keepdims=True)

    return pl.pallas_call(
        body,
        name="mla_delta",
        grid=(H, nq),
        in_specs=[pl.BlockSpec((tq, LANE), lambda h, i: (i, h)), pl.BlockSpec((1, LANE, tq), lambda h, i: (h, 0, i))],
        out_specs=pl.BlockSpec((1, 1, 1, tq), lambda h, i: (h, i, 0, 0)),
        out_shape=jax.ShapeDtypeStruct((H, nq, 1, tq), F32),
        compiler_params=_cparams(2),
    )(dy, ot)


def _mla_attn_bwd(q_cat, k_cat, v, dy, lse, delta, xchg=None):
    H, S, _ = q_cat.shape
    _, nq, _, tq = lse.shape
    tk = _tile(S, 512)
    nkb = S // tk
    nx = xchg.n if xchg is not None else 0

    def body(*refs):
        k_ref, v_ref, q_ref, do_ref, lse_ref, dl_ref = refs[:6]
        x_in = refs[6:6 + nx]
        dk_ref, dv_ref, dq_ref = refs[6 + nx:9 + nx]
        x_out = refs[9 + nx:9 + 2 * nx]
        dk_scr, dv_scr = refs[9 + 2 * nx:11 + 2 * nx]
        x_sems = refs[11 + 2 * nx:]
        hd, ki = pl.program_id(0), pl.program_id(1)
        if nx:
            @pl.when((hd == 0) & (ki == 0))
            def _():
                xchg.start(x_in, x_out, x_sems)

        @pl.when(ki == 0)
        def _():
            dq_ref[...] = jnp.zeros_like(dq_ref)

        kb, vb = k_ref[0], v_ref[0]
        dk_scr[...] = jnp.zeros_like(dk_scr)
        dv_scr[...] = jnp.zeros_like(dv_scr)

        nsub = MLA_BWD_SLABS if tq % (MLA_BWD_SLABS * LANE) == 0 else 1
        ws = tq // nsub

        def step(i, carry):
            rows = [pl.ds(pl.multiple_of(i * tq + s * ws, ws), ws) for s in range(nsub)]
            lanes = [slice(s * ws, (s + 1) * ws) for s in range(nsub)]
            qc = [q_ref[0, r, :] for r in rows]
            doc = [do_ref[r, :] for r in rows]
            lse_i, dl_i = lse_ref[0, i], dl_ref[0, i]
            st = _m(lambda q_: _nt(kb, q_), qc)
            dp = _m(lambda d_: _nt(vb, d_), doc)
            pt = _m(lambda s_, ln: jnp.exp2(s_ - lse_i[:, ln]), st, lanes)
            dst = _m(lambda p_, d_, ln: (p_ * (d_ - dl_i[:, ln])).astype(CDT), pt, dp, lanes)
            dv_scr[...] += sum(_m(_nn, pt, doc))
            dk_scr[...] += sum(_m(_nn, dst, qc))
            dqs = _m(lambda d_: _tn(d_, kb), dst)
            for r, dq_ in zip(rows, dqs):
                dq_ref[0, r, :] += dq_
            return carry

        lax.fori_loop(0, nq, step, 0, unroll=2 if nq % 2 == 0 else 1)
        dk_ref[0] = dk_scr[...] * (MLA_SCALE / MLA_QSCALE)
        dv_ref[0] = dv_scr[...]

        if nx:
            @pl.when((hd == H - 1) & (ki == nkb - 1))
            def _():
                xchg.wait(x_in, x_out, x_sems)

    return pl.pallas_call(
        body,
        name="mla_attn_bwd",
        grid=(H, nkb),
        in_specs=[
            pl.BlockSpec((1, tk, 2 * LANE), lambda h, j: (h, j, 0)),
            pl.BlockSpec((1, tk, LANE), lambda h, j: (h, j, 0)),
            pl.BlockSpec((1, S, 2 * LANE), lambda h, j: (h, 0, 0)),
            pl.BlockSpec((S, LANE), lambda h, j: (0, h)),
            pl.BlockSpec((1, nq, 1, tq), lambda h, j: (h, 0, 0, 0)),
            pl.BlockSpec((1, nq, 1, tq), lambda h, j: (h, 0, 0, 0)),
        ] + (xchg.specs if nx else []),
        out_specs=[
            pl.BlockSpec((1, tk, 2 * LANE), lambda h, j: (h, j, 0)),
            pl.BlockSpec((1, tk, LANE), lambda h, j: (h, j, 0)),
            pl.BlockSpec((1, S, 2 * LANE), lambda h, j: (h, 0, 0)),
        ] + (xchg.specs if nx else []),
        out_shape=[
            jax.ShapeDtypeStruct((H, S, 2 * LANE), F32),
            jax.ShapeDtypeStruct((H, S, LANE), F32),
            jax.ShapeDtypeStruct((H, S, 2 * LANE), F32),
        ] + (xchg.out_shape if nx else []),
        scratch_shapes=[pltpu.VMEM((tk, 2 * LANE), F32), pltpu.VMEM((tk, LANE), F32)] + (xchg.scratch if nx else []),
        compiler_params=_cparams(2, side_effects=nx > 0),
    )(k_cat, v, q_cat, dy, lse, delta, *(xchg.arrs if nx else []))


def _mla_up_bwd(dq_cat, dk_cat, dv, proj, cf, sf, g_cq, g_ckv, wuq_p, wukv, dproj):
    H, S, _ = dq_cat.shape
    tm = _rtile(S, 256)
    PW = proj.shape[1]
    kr_off = PW - KR_PAD

    assert OFF_CKV == OFF_CQ + MLA_RANK and OFF_CQ % (2 * MLA_RANK) == 0

    def body(dq_ref, dk_ref, dv_ref, cq_ref, ckv_ref, cf_ref, sf_ref, gq_ref, gkv_ref, wq_ref, wkv_ref, _dp,
             dqp_ref, dkvp_ref, dc_ref, dkr_ref, dgq_ref, dgkv_ref):
        i = pl.program_id(0)
        dcq_ref, dckv_ref = dc_ref.at[:, 0:MLA_RANK], dc_ref.at[:, MLA_RANK:2 * MLA_RANK]
        cfv, sfv = cf_ref[...], sf_ref[...]
        aq = jnp.zeros((tm, MLA_RANK), F32)
        akv = jnp.zeros((tm, MLA_RANK), F32)
        akr = jnp.zeros((tm, LANE), F32)
        for h in range(H):
            dq = dq_ref[h] * MLA_SCALE
            dqr = dq[:, LANE:2 * LANE]
            dqp = jnp.concatenate([dq[:, 0:LANE], dqr * cfv, dqr * sfv], axis=1).astype(CDT)
            dqp_ref[:, pl.ds(h * 3 * LANE, 3 * LANE)] = dqp
            aq = aq + _nt(dqp, wq_ref[h])
            dk = dk_ref[h]
            dkvp = jnp.concatenate([dk[:, 0:LANE], dv_ref[h]], axis=1).astype(CDT)
            dkvp_ref[:, pl.ds(h * 2 * LANE, 2 * LANE)] = dkvp
            akv = akv + _nt(dkvp, wkv_ref[h])
            akr = akr + dk[:, LANE:2 * LANE]

        def rms_bwd(c_ref, g_ref, dn, d_ref, dg_ref):
            xh, rs = _rms_stats(c_ref[...])
            dxh = dn * g_ref[...]
            d_ref[...] = (rs * (dxh - xh * jnp.mean(dxh * xh, axis=-1, keepdims=True))).astype(d_ref.dtype)
            part = _rowsum8(dn * xh)

            @pl.when(i == 0)
            def _():
                dg_ref[...] = part

            @pl.when(i > 0)
            def _():
                dg_ref[...] += part

        rms_bwd(cq_ref, gq_ref, aq, dcq_ref, dgq_ref)
        rms_bwd(ckv_ref, gkv_ref, akv, dckv_ref, dgkv_ref)
        dkr_ref[...] = jnp.concatenate([akr * cfv, akr * sfv, jnp.zeros((tm, KR_PAD - 2 * LANE), F32)], axis=1).astype(dkr_ref.dtype)

    cspec = lambda off, w: pl.BlockSpec((tm, w), lambda i, o=off // w: (i, o))
    hspec = lambda w: pl.BlockSpec((H, tm, w), lambda i: (0, i, 0))
    outs = pl.pallas_call(
        body,
        name="mla_up_bwd",
        grid=(S // tm,),
        in_specs=[
            hspec(2 * LANE), hspec(2 * LANE), hspec(LANE),
            cspec(OFF_CQ, MLA_RANK), cspec(OFF_CKV, MLA_RANK),
            pl.BlockSpec((tm, LANE), lambda i: (i, 0)), pl.BlockSpec((tm, LANE), lambda i: (i, 0)),
            pl.BlockSpec((1, MLA_RANK), lambda i: (0, 0)), pl.BlockSpec((1, MLA_RANK), lambda i: (0, 0)),
            pl.BlockSpec((H, MLA_RANK, 3 * LANE), lambda i: (0, 0, 0)),
            pl.BlockSpec((H, MLA_RANK, 2 * LANE), lambda i: (0, 0, 0)),
            pl.BlockSpec(memory_space=pl.ANY),
        ],
        out_specs=[
            pl.BlockSpec((tm, H * 3 * LANE), lambda i: (i, 0)), pl.BlockSpec((tm, H * 2 * LANE), lambda i: (i, 0)),
            pl.BlockSpec((tm, 2 * MLA_RANK), lambda i: (i, OFF_CQ // (2 * MLA_RANK))),
            pl.BlockSpec((tm, KR_PAD), lambda i: (i, 0)),
            pl.BlockSpec((SUB, MLA_RANK), lambda i: (0, 0)),
            pl.BlockSpec((SUB, MLA_RANK), lambda i: (0, 0)),
        ],
        out_shape=[
            jax.ShapeDtypeStruct((S, H * 3 * LANE), CDT), jax.ShapeDtypeStruct((S, H * 2 * LANE), CDT),
            jax.ShapeDtypeStruct(dproj.shape, dproj.dtype),
            jax.ShapeDtypeStruct((S, KR_PAD), CDT),
            jax.ShapeDtypeStruct((SUB, MLA_RANK), F32), jax.ShapeDtypeStruct((SUB, MLA_RANK), F32),
        ],
        input_output_aliases={11: 2},
        compiler_params=_cparams(1),
    )(dq_cat, dk_cat, dv, proj, proj, cf, sf, g_cq, g_ckv, wuq_p, wukv, dproj)
    dqp, dkvp, dproj, dkr, dgq, dgkv = outs
    dproj = _copy_into("dproj_kr", dkr, dproj, kr_off)
    return dproj, dqp, dkvp, dgq, dgkv


def _mem_softmax(q, k):
    s = _nt(q, k) * (MEM_HD ** -0.5)
    p = jnp.exp(s - jnp.max(s, axis=1, keepdims=True))
    return p / jnp.sum(p, axis=1, keepdims=True)


def _mem_attn_fwd(proj, memkv):
    S = proj.shape[0]
    Mm = memkv.shape[0]
    tm = _rtile(S, 1024)

    def body(q_ref, k_ref, v_ref, y_ref):
        pn = _mem_softmax(q_ref[...], k_ref[...])
        y_ref[...] = _nn(pn, v_ref[...]).astype(y_ref.dtype)

    return pl.pallas_call(
        body,
        name="mem_attn_fwd",
        grid=(S // tm, MEM_HEADS),
        in_specs=[
            pl.BlockSpec((tm, MEM_HD), lambda i, h: (i, OFF_QM // MEM_HD + h)),
            pl.BlockSpec((Mm, MEM_HD), lambda i, h: (0, h)),
            pl.BlockSpec((Mm, MEM_HD), lambda i, h: (0, MEM_HEADS + h)),
        ],
        out_specs=pl.BlockSpec((tm, MEM_HD), lambda i, h: (i, h)),
        out_shape=jax.ShapeDtypeStruct((S, MEM_W), CDT),
        compiler_params=_cparams(2),
    )(proj, memkv, memkv)


def _mem_attn_bwd(dy, proj, memkv, dproj):
    S = proj.shape[0]
    Mm = memkv.shape[0]
    tm = _rtile(S, 1024)
    scale = MEM_HD ** -0.5

    def body(dy_ref, q_ref, k_ref, v_ref, _dp, dq_ref, dk_ref, dv_ref):
        i = pl.program_id(1)
        q, k, dyv = q_ref[...].astype(CDT), k_ref[...], dy_ref[...]
        pn = _mem_softmax(q, k)
        dvp = _tn(pn, dyv)
        dp = _nt(dyv, v_ref[...])
        ds = pn * (dp - jnp.sum(dp * pn, axis=1, keepdims=True)) * scale
        dq_ref[...] = _nn(ds, k).astype(dq_ref.dtype)
        dkp = _tn(ds, q)

        @pl.when(i == 0)
        def _():
            dk_ref[...] = dkp
            dv_ref[...] = dvp

        @pl.when(i > 0)
        def _():
            dk_ref[...] += dkp
            dv_ref[...] += dvp

    dproj, dk, dv = pl.pallas_call(
        body,
        name="mem_attn_bwd",
        grid=(MEM_HEADS, S // tm),
        in_specs=[
            pl.BlockSpec((tm, MEM_HD), lambda h, i: (i, h)),
            pl.BlockSpec((tm, MEM_HD), lambda h, i: (i, OFF_QM // MEM_HD + h)),
            pl.BlockSpec((Mm, MEM_HD), lambda h, i: (0, h)),
            pl.BlockSpec((Mm, MEM_HD), lambda h, i: (0, MEM_HEADS + h)),
            pl.BlockSpec(memory_space=pl.ANY),
        ],
        out_specs=[
            pl.BlockSpec((tm, MEM_HD), lambda h, i: (i, OFF_QM // MEM_HD + h)),
            pl.BlockSpec((Mm, MEM_HD), lambda h, i: (0, h)),
            pl.BlockSpec((Mm, MEM_HD), lambda h, i: (0, h)),
        ],
        out_shape=[
            jax.ShapeDtypeStruct(dproj.shape, dproj.dtype),
            jax.ShapeDtypeStruct((Mm, MEM_W), F32),
            jax.ShapeDtypeStruct((Mm, MEM_W), F32),
        ],
        input_output_aliases={4: 0},
        compiler_params=_cparams(2),
    )(dy, proj, memkv, memkv, dproj)
    return dproj, dk, dv


def _small_allreduce(vec):
    NS = vec.shape[1]

    def body(v_ref, o_ref, gbuf, mine, send, recv):
        x, y, c, me = _my_place()
        mine[...] = jnp.sum(v_ref[...], axis=0, keepdims=True)
        gbuf[me] = mine[...]
        copies = []
        for kk in range(1, N_DEV):
            peer, _ = _peer(x, y, c, kk)
            cp = pltpu.make_async_remote_copy(src_ref=mine, dst_ref=gbuf.at[me], send_sem=send.at[kk - 1],
                                              recv_sem=recv.at[kk - 1], device_id=peer, device_id_type=MESH)
            cp.start()
            copies.append(cp)
        for cp in copies:
            cp.wait()
        tot = gbuf[0]
        for d in range(1, N_DEV):
            tot = tot + gbuf[d]
        o_ref[...] = tot

    return pl.pallas_call(
        body,
        name="small_allreduce",
        in_specs=[pl.BlockSpec(memory_space=pltpu.VMEM)],
        out_specs=pl.BlockSpec(memory_space=pltpu.VMEM),
        out_shape=jax.ShapeDtypeStruct((1, NS), F32),
        scratch_shapes=[pltpu.VMEM((N_DEV, 1, NS), F32), pltpu.VMEM((1, NS), F32), pltpu.SemaphoreType.DMA((N_DEV - 1,)),
                        pltpu.SemaphoreType.DMA((N_DEV - 1,))],
        compiler_params=pltpu.CompilerParams(has_side_effects=True, vmem_limit_bytes=V7X_VMEM_LIMIT),
    )(vec)


def _adamw_math(g, w, m, v):
    nm = ADAM_B1 * m + (1.0 - ADAM_B1) * g
    nv = ADAM_B2 * v + (1.0 - ADAM_B2) * (g * g)
    mh = nm / (1.0 - ADAM_B1 ** ADAM_STEP)
    vh = nv / (1.0 - ADAM_B2 ** ADAM_STEP)
    delta = -ADAM_LR * (mh / (jnp.sqrt(vh) + ADAM_EPS) + ADAM_WD * w)
    return delta, nm, nv


def _adam_big(name, recv, w, m, v):
    _, R, C = w.shape
    tr = _rtile(R, max(SUB, (ADAM_BLOCK_ELEMS // C) // SUB * SUB))

    def body(r_ref, w_ref, m_ref, v_ref, g_ref, d_ref, nm_ref, nv_ref):
        g = r_ref[0].astype(F32)
        for d in range(1, N_DEV):
            g = g + r_ref[d].astype(F32)
        delta, nm, nv = _adamw_math(g, w_ref[0], m_ref[0], v_ref[0])
        g_ref[0] = g
        d_ref[0] = delta
        nm_ref[0] = nm
        nv_ref[0] = nv

    blk = pl.BlockSpec((1, tr, C), lambda i: (0, i, 0))
    return pl.pallas_call(
        body,
        name=name,
        grid=(R // tr,),
        in_specs=[pl.BlockSpec((N_DEV, tr, C), lambda i: (0, i, 0)), blk, blk, blk],
        out_specs=[blk, blk, blk, blk],
        out_shape=[jax.ShapeDtypeStruct((1, R, C), F32)] * 4,
        compiler_params=_cparams(1),
    )(recv, w, m, v)


def _to_bf16(name, w):
    _, R, C = w.shape
    tr = _rtile(R, max(SUB, (ADAM_BLOCK_ELEMS // C) // SUB * SUB))

    def body(w_ref, o_ref):
        o_ref[...] = w_ref[0].astype(CDT)

    return pl.pallas_call(
        body,
        name=name,
        grid=(R // tr,),
        in_specs=[pl.BlockSpec((1, tr, C), lambda i: (0, i, 0))],
        out_specs=pl.BlockSpec((tr, C), lambda i: (i, 0)),
        out_shape=jax.ShapeDtypeStruct((R, C), CDT),
        compiler_params=_cparams(1),
    )(w)


def _adam_small(g, w, m, v):
    def body(g_ref, w_ref, m_ref, v_ref, d_ref, nm_ref, nv_ref):
        delta, nm, nv = _adamw_math(g_ref[...], w_ref[...], m_ref[...], v_ref[...])
        d_ref[...] = delta
        nm_ref[...] = nm
        nv_ref[...] = nv

    return pl.pallas_call(body, name="adam_small", out_shape=[jax.ShapeDtypeStruct(g.shape, F32)] * 3)(g, w, m, v)


def _rot(w, axis=-1):
    x1, x2 = jnp.split(w, 2, axis=axis)
    return jnp.concatenate([-x2, x1], axis=axis)


def _unrot(dw, axis=-1):
    d1, d2 = jnp.split(dw, 2, axis=axis)
    return jnp.concatenate([d2, -d1], axis=axis)


def _pad_cols(w, width):
    return jnp.pad(w, [(0, 0)] * (w.ndim - 1) + [(0, width - w.shape[-1])])


def kernel(x, mem, positions, ln_emb_g, ln_emb_b, hgrn_lb_logits, w_in, hgrn_norm_g, mla_g_cq, mla_g_ckv, mla_w_uq, mla_w_ukv, mem_w_kv, w_branch, w_o, ln1_g, ln1_b, w_ffn_gate, w_ffn_up, w_ffn_down, ln2_g, ln2_b, loss_target, m_ln_emb_g, m_ln_emb_b, m_hgrn_lb_logits, m_w_in, m_hgrn_norm_g, m_mla_g_cq, m_mla_g_ckv, m_mla_w_uq, m_mla_w_ukv, m_mem_w_kv, m_w_branch, m_w_o, m_ln1_g, m_ln1_b, m_w_ffn_gate, m_w_ffn_up, m_w_ffn_down, m_ln2_g, m_ln2_b, v_ln_emb_g, v_ln_emb_b, v_hgrn_lb_logits, v_w_in, v_hgrn_norm_g, v_mla_g_cq, v_mla_g_ckv, v_mla_w_uq, v_mla_w_ukv, v_mem_w_kv, v_w_branch, v_w_o, v_ln1_g, v_ln1_b, v_w_ffn_gate, v_w_ffn_up, v_w_ffn_down, v_ln2_g, v_ln2_b):
    x2, tgt = x[0], loss_target[0]
    S, D = x2.shape
    Mm = mem.shape[1]
    F = w_ffn_gate.shape[2] * N_DEV
    GW = 3 * D
    PW = OFF_GATE + GW + KR_PAD
    KR = OFF_GATE + GW
    NIN = w_in.shape[2] * N_DEV
    assert NIN == OFF_GATE + MLA_ROPE + GW
    _, _, _, me = _my_place()
    row = lambda a: a.reshape(1, -1)

    br3 = lambda a: a.reshape(1, 3 * BR_W, -1)
    tp = lambda a: jnp.swapaxes(a, 1, 2)
    big_w = [tp(w_in), mla_w_uq, mla_w_ukv, mem_w_kv, br3(w_branch), w_o, tp(w_ffn_gate), tp(w_ffn_up), w_ffn_down]
    big_m = [tp(m_w_in), m_mla_w_uq, m_mla_w_ukv, m_mem_w_kv, br3(m_w_branch), m_w_o, tp(m_w_ffn_gate), tp(m_w_ffn_up),
             m_w_ffn_down]
    big_v = [tp(v_w_in), v_mla_w_uq, v_mla_w_ukv, v_mem_w_kv, br3(v_w_branch), v_w_o, tp(v_w_ffn_gate), tp(v_w_ffn_up),
             v_w_ffn_down]
    transposed = (0, 6, 7)
    wnames = ["w_in", "w_uq", "w_ukv", "mem_w_kv", "w_branch", "w_o", "w_gate", "w_up", "w_down"]
    big_wb = [_to_bf16("bf16_" + nme, w) for nme, w in zip(wnames, big_w)]
    g_in, g_lb = _all_gather_two_level("weights_all_gather", [big_wb[0], hgrn_lb_logits.reshape(4, -1)])
    win_t = g_in.reshape(NIN, D)
    kr_w = win_t[OFF_QM:OFF_QM + MLA_ROPE]
    zeros64 = jnp.zeros_like(kr_w)
    win_pt = jnp.concatenate([win_t[:OFF_FB], win_t[OFF_FF:OFF_G], win_t[OFF_FB:OFF_FF], win_t[OFF_G:OFF_QM],
                              win_t[OFF_QM + MLA_ROPE:], kr_w, zeros64, _rot(kr_w, 0), zeros64,
                              jnp.zeros((KR_PAD - 2 * LANE, D), CDT)], axis=0)
    lbl4 = jnp.transpose(g_lb, (1, 0, 2)).reshape(4, -1)

    half = MLA_ROPE // 2
    inv_freq = jnp.power(ROPE_THETA, -jnp.arange(half, dtype=F32) / half)
    ang = positions[0].astype(F32)[:, None] * inv_freq
    cf = _pad_cols(jnp.tile(jnp.cos(ang), (1, 2)), LANE)
    sf = _pad_cols(jnp.tile(jnp.sin(ang), (1, 2)), LANE)

    tm512 = _rtile(S, 512)
    ident = lambda accs, tiles, rows: ([accs[0]], [])

    def epi_ln0(accs, tiles, rows):
        h = _ln_stats(tiles[0])[0] * rows[0] + rows[1]
        return [h, h], []

    h0, h0b = _fused_mm("ln_emb_fwd", "nn", [], S, D, 1, tm512, D, 1, [(D, F32, 0, None), (D, CDT, 0, None)], epi_ln0,
                        tiles=[(x2, 0)], rows=[(row(ln_emb_g), 0), (row(ln_emb_b), 0)])
    proj, g_uq, g_ukv, g_mkv, g_wb, g_wo = _fused_mm(
        "proj", "nt", [[(h0b, 0, win_pt, 0)]], S, PW, D, _rtile(S, 1024), _tile(PW, 1536), D, [(PW, F32, 0, None)], ident,
        xchg=_Xchg(big_wb[1:6], False), msplit=2 if S % 2048 == 0 else 1)
    wuq_p =jnp.concatenate([g_uq[..., :MLA_NOPE], _pad_cols(g_uq[..., MLA_NOPE:], LANE),
                             _pad_cols(_rot(g_uq[..., MLA_NOPE:]), LANE)], axis=-1)
    wukv = g_ukv
    wmkv = g_mkv.reshape(-1, g_mkv.shape[-1])
    wb = jnp.transpose(g_wb.reshape(N_DEV, 3, BR_W, -1), (1, 2, 0, 3)).reshape(3, BR_W, D)
    wo = g_wo.reshape(-1, D)
    o_f, st_f = _gla_fwd(proj, lbl4, OFF_FF, False, "gla_fwd_f")
    o_b, st_b = _gla_fwd(proj, lbl4, OFF_FB, True, "gla_fwd_b")
    y_hg = _hgrn_post_fwd(o_f, o_b, proj, hgrn_norm_g)
    q_cat, k_cat, v_mla, vt_mla, cqn, ckvn = _mla_up(proj, cf, sf, mla_g_cq, mla_g_ckv, wuq_p, wukv)
    y_mla, ot, lse, g_wg, g_wu, g_wd = _mla_attn_fwd(q_cat, k_cat, vt_mla, _Xchg(big_wb[6:9], False))
    wg_t, wu_t = g_wg.reshape(F, D), g_wu.reshape(F, D)
    wd = g_wd.reshape(-1, D)
    memb = mem[0].astype(CDT)
    (memkv,) = _fused_mm("mem_kv", "nn", [[(memb, 0, wmkv, 0)]], Mm, 2 * MEM_W, D, Mm, _tile(2 * MEM_W, 512), D,
                         [(2 * MEM_W, CDT, 0, None)], ident)
    y_mem = _mem_attn_fwd(proj, memkv)
    ys = [y_hg, y_mla, y_mem]
    tnD = _tile(D, 1024, OFF_GATE)

    def epi_branch(accs, tiles, rows):
        return [_sigmoid(tiles[0]) * accs[0] + _sigmoid(tiles[1]) * accs[1] + _sigmoid(tiles[2]) * accs[2]], []

    (merged,) = _fused_mm("branch_fwd", "nn", [[(ys[b], 0, wb[b], 0)] for b in range(3)], S, D, BR_W, tm512, tnD, BR_W,
                          [(D, CDT, 0, None)], epi_branch, tiles=[(proj, OFF_GATE + b * D) for b in range(3)])

    def epi_ln1(accs, tiles, rows):
        r1v = ALPHA * tiles[0] + accs[0]
        return [r1v, _ln_stats(r1v)[0] * rows[0] + rows[1]], []

    r1, h1b = _fused_mm("wo_ln1", "nn", [[(merged, 0, wo, 0)]], S, D, D, tm512, D, D,
                        [(D, F32, 0, None), (D, CDT, 0, None)], epi_ln1, tiles=[(h0, 0)], rows=[(ln1_g, 0), (ln1_b, 0)])
    tnF = _tile(F, 512)

    def epi_up(accs, tiles, rows):
        gp, up = accs
        return [gp, up, gp * _sigmoid(gp) * up], []

    tm1k, ms1k = _rtile(S, 1024), (2 if S % 2048 == 0 else 1)
    gpb, upb, act = _fused_mm("ffn_up", "nt", [[(h1b, 0, wg_t, 0)], [(h1b, 0, wu_t, 0)]], S, F, D, tm1k, tnF, D,
                              [(F, CDT, 0, None)] * 3, epi_up, msplit=ms1k)

    def epi_down(accs, tiles, rows):
        g1, b1, g2, b2 = rows
        h1 = _ln_stats(tiles[0])[0] * g1 + b1
        xh2, rstd2 = _ln_stats(ALPHA * h1 + accs[0])
        diff = xh2 * g2 + b2 - tiles[1]
        dh2 = diff * (1.0 / D)
        dr2v = _ln_bwd(dh2, xh2, rstd2, g2)
        return [dr2v, dr2v], [dh2 * xh2, dh2, diff * diff * (0.5 / D)]

    acc_first = lambda epi: (lambda accs, tiles, rows: epi([tiles[0]], tiles[1:], rows))
    tm256 = _rtile(S, 256)
    (ff,) = _fused_mm("ffn_down", "nn", [[(act, 0, wd, 0)]], S, D, F, tm1k, _tile(D, 512), F, [(D, F32, 0, None)], ident,
                      msplit=ms1k)
    dr2, dr2b, dg2, db2, lossp = _fused_mm(
        "ffn_ln2_loss", "nn", [], S, D, 1, tm256, D, 1, [(D, F32, 0, None), (D, CDT, 0, None)], acc_first(epi_down),
        tiles=[(ff, 0), (r1, 0), (tgt, 0)], rows=[(ln1_g, 0), (ln1_b, 0), (ln2_g, 0), (ln2_b, 0)], n_racc=3)

    def epi_dact(accs, tiles, rows):
        da, gp, up = accs[0], tiles[0].astype(F32), tiles[1].astype(F32)
        s = _sigmoid(gp)
        return [da * up * (s * (1.0 + gp * (1.0 - s))), da * (gp * s)], []

    dgp, dup = _fused_mm("ffn_dact", "nt", [[(dr2b, 0, wd, 0)]], S, F, D, tm1k, tnF, D, [(F, CDT, 0, None)] * 2,
                         epi_dact, tiles=[(gpb, 0), (upb, 0)], msplit=ms1k)
    tkS = _rtile(S, 2048)
    (d_wd,) = _fused_mm("dw_down", "tn", [[(act, 0, dr2b, 0)]], F, D, S, tnF, D, tkS, [(D, CDT, 0, None)], ident)
    d_wg_t, d_wu_t = _fused_mm("dw_gate_up", "tn", [[(dgp, 0, h1b, 0)], [(dup, 0, h1b, 0)]], F, D, S, tnF, _tile(D, 1024),
                               tkS, [(D, CDT, 0, None)] * 2, lambda accs, tiles, rows: (accs, []))

    def epi_dh1(accs, tiles, rows):
        dh1 = accs[0] + ALPHA * tiles[0]
        xh1, rstd1 = _ln_stats(tiles[1])
        dr1v = _ln_bwd(dh1, xh1, rstd1, rows[0])
        return [dr1v, dr1v], [dh1 * xh1, dh1]

    rows8 = lambda dw: dw.reshape(N_DEV, -1, dw.shape[-1])
    (dh1_acc,) = _fused_mm("dh1", "nn", [[(dgp, 0, wg_t, 0), (dup, 0, wu_t, 0)]], S, D, F, tm512, _tile(D, 512), F,
                           [(D, F32, 0, None)], ident)
    dr1, dr1b, dg1, db1 = _fused_mm(
        "dh1_ln1", "nn", [], S, D, 1, tm256, D, 1, [(D, F32, 0, None), (D, CDT, 0, None)], acc_first(epi_dh1),
        tiles=[(dh1_acc, 0), (dr2, 0), (r1, 0)], rows=[(ln1_g, 0)], n_racc=2)
    (dmerged,) = _fused_mm("dmerged", "nt", [[(dr1b, 0, wo, 0)]], S, D, D, tm512, D, D, [(D, CDT, 0, None)], ident)
    (d_wo,) = _fused_mm("dw_o", "tn", [[(merged, 0, dr1b, 0)]], D, D, S, _tile(D, 512), D, tkS, [(D, CDT, 0, None)], ident)

    def epi_dbranch(accs, tiles, rows):
        dm, s = tiles[0].astype(F32), _sigmoid(tiles[1])
        return [dm * s, dm * accs[0] * s * (1.0 - s)], []

    dproj = None
    d_wbs, dys = [], []
    for b in range(3):
        du, dproj = _fused_mm(f"branch_bwd{b}", "nn", [[(ys[b], 0, wb[b], 0)]], S, D, BR_W, tm512, tnD, BR_W,
                              [(D, CDT, 0, None), (PW, CDT, OFF_GATE + b * D, dproj)], epi_dbranch,
                              tiles=[(dmerged, 0), (proj, OFF_GATE + b * D)])
        (dwb,) = _fused_mm(f"dw_branch{b}", "tn", [[(ys[b], 0, du, 0)]], BR_W, D, S, _tile(BR_W, 512), D, tkS,
                           [(D, CDT, 0, None)], ident)
        (dyb,) = _fused_mm(f"dy_branch{b}", "nt", [[(du, 0, wb[b], 0)]], S, BR_W, D, tm512, BR_W, D,
                           [(BR_W, F32 if b == 0 else CDT, 0, None)], ident)
        d_wbs.append(dwb)
        dys.append(dyb)
    dy_hg, dy_mla, dy_mem = dys

    dproj, dk_mem, dv_mem = _mem_attn_bwd(dy_mem, proj, memkv, dproj)
    dkv_mem = jnp.concatenate([dk_mem, dv_mem], axis=1).astype(CDT)
    (d_wmkv,) = _fused_mm("dw_memkv", "tn", [[(memb, 0, dkv_mem, 0)]], D, 2 * MEM_W, Mm, _tile(D, 512), 2 * MEM_W, Mm,
                          [(2 * MEM_W, CDT, 0, None)], ident)

    delta = _mla_delta(dy_mla, ot)
    lse_b = lse.reshape(delta.shape)
    dk_cat, dv_h, dq_cat, r_wg, r_wu, r_wd = _mla_attn_bwd(
        q_cat, k_cat, v_mla, dy_mla, lse_b, delta, _Xchg([rows8(d_wg_t), rows8(d_wu_t), rows8(d_wd)], True))
    dproj, dqp, dkvp, dgq, dgkv = _mla_up_bwd(dq_cat, dk_cat, dv_h, proj, cf, sf, mla_g_cq, mla_g_ckv, wuq_p, wukv, dproj)
    heads_major = lambda dw: jnp.transpose(dw.reshape(MLA_RANK, MLA_HEADS, -1), (1, 0, 2))
    (d_wuq_all,) = _fused_mm("dw_uq", "tn", [[(cqn, 0, dqp, 0)]], MLA_RANK, dqp.shape[1], S, MLA_RANK,
                             _tile(dqp.shape[1], 1536), tkS, [(dqp.shape[1], F32, 0, None)], ident)
    (d_wukv_all,) = _fused_mm("dw_ukv", "tn", [[(ckvn, 0, dkvp, 0)]], MLA_RANK, dkvp.shape[1], S, MLA_RANK,
                              _tile(dkvp.shape[1], 1024), tkS, [(dkvp.shape[1], CDT, 0, None)], ident)
    d_wuq_p, d_wukv = heads_major(d_wuq_all), heads_major(d_wukv_all)
    d_wuq = jnp.concatenate([d_wuq_p[..., :MLA_NOPE],
                             d_wuq_p[..., LANE:LANE + MLA_ROPE] + _unrot(d_wuq_p[..., 2 * LANE:2 * LANE + MLA_ROPE])],
                            axis=-1).astype(CDT)

    do_hg, dproj, dng = _hgrn_post_bwd(dy_hg, o_f, o_b, proj, hgrn_norm_g, dproj)
    dproj, dq1, di1, dl_f = _gla_bwd(proj, lbl4, OFF_FF, False, do_hg, st_f, dproj, None, "gla_bwd_f")
    dproj, _, _, dl_b = _gla_bwd(proj, lbl4, OFF_FB, True, do_hg, st_b, dproj, (dq1, di1), "gla_bwd_b")

    def epi_dh0(accs, tiles, rows):
        dh0 = accs[0] + ALPHA * tiles[0]
        xh, rstd = _ln_stats(tiles[1])
        return [_ln_bwd(dh0, xh, rstd, rows[0])], [dh0 * xh, dh0]

    d_wb = jnp.transpose(jnp.stack(d_wbs).reshape(3, BR_W, N_DEV, -1), (2, 0, 1, 3)).reshape(N_DEV, 3 * BR_W, -1)
    d_win_pt, r_uq, r_ukv, r_mkv, r_wb, r_wo = _fused_mm(
        "dw_in", "tn", [[(dproj, 0, h0b, 0)]], PW, D, S, _tile(PW, 1536), _tile(D, 1024), tkS, [(D, CDT, 0, None)], ident,
        xchg=_Xchg([d_wuq, d_wukv, rows8(d_wmkv), d_wb, rows8(d_wo)], True))
    d_kr = (d_win_pt[KR:KR + MLA_ROPE].astype(F32) + _unrot(d_win_pt[KR + LANE:KR + LANE + MLA_ROPE].astype(F32), 0)).astype(CDT)
    d_win_t = jnp.concatenate([d_win_pt[:OFF_FB], d_win_pt[OFF_FF:OFF_G], d_win_pt[OFF_FB:OFF_FF],
                               d_win_pt[OFF_G:OFF_QM], d_kr, d_win_pt[OFF_QM:KR]], axis=0)
    grad_x, dge, dbe, r_in = _fused_mm(
        "dh0_ln_emb", "nn", [[(dproj, 0, win_pt, 0)]], S, D, PW, tm512, D, _tile(PW, 1536), [(D, F32, 0, None)], epi_dh0,
        tiles=[(dr1, 0), (x2, 0)], rows=[(row(ln_emb_g), 0)], n_racc=2, xchg=_Xchg([rows8(d_win_t)], True))

    recv = [r_in, r_uq, r_ukv, r_mkv, r_wb, r_wo, r_wg, r_wu, r_wd]
    names = ["w_in", "w_uq", "w_ukv", "mem_w_kv", "w_branch", "w_o", "w_gate", "w_up", "w_down"]
    big_out = [_adam_big("adam_" + nme, r, w, m_, v_) for nme, r, w, m_, v_ in zip(names, recv, big_w, big_m, big_v)]

    parts = [dge, dbe, dng, dgq, dgkv, dg1, db1, dg2, db2, dl_f, dl_b, lossp]
    widths = [p.shape[1] for p in parts]
    red = _small_allreduce(jnp.concatenate(parts, axis=1))[0]
    offs = [sum(widths[:i]) for i in range(len(widths))]
    rs = [red[o:o + w_] for o, w_ in zip(offs, widths)]
    g_le_g, g_le_b, g_ng, g_gq, g_gkv, g_l1g, g_l1b, g_l2g, g_l2b, g_dlf, g_dlb, g_loss = rs
    loss = jnp.sum(g_loss)
    g_ng = g_ng.reshape(HG_HEADS, HG_DK).sum(axis=0)
    dl0 = jnp.stack([g_dlf, g_dlb])
    g_lb_full = jnp.stack([dl0, -dl0], axis=1)
    lbw = hgrn_lb_logits.shape[2]
    g_lb = lax.dynamic_slice_in_dim(g_lb_full, me * lbw, lbw, axis=2)

    small_g = [g_le_g, g_le_b, g_lb, g_ng.reshape(1, -1), g_gq.reshape(1, -1), g_gkv.reshape(1, -1), g_l1g.reshape(1, -1),
               g_l1b.reshape(1, -1), g_l2g.reshape(1, -1), g_l2b.reshape(1, -1)]
    small_w = [ln_emb_g, ln_emb_b, hgrn_lb_logits, hgrn_norm_g, mla_g_cq, mla_g_ckv, ln1_g, ln1_b, ln2_g, ln2_b]
    small_m = [m_ln_emb_g, m_ln_emb_b, m_hgrn_lb_logits, m_hgrn_norm_g, m_mla_g_cq, m_mla_g_ckv, m_ln1_g, m_ln1_b, m_ln2_g, m_ln2_b]
    small_v = [v_ln_emb_g, v_ln_emb_b, v_hgrn_lb_logits, v_hgrn_norm_g, v_mla_g_cq, v_mla_g_ckv, v_ln1_g, v_ln1_b, v_ln2_g, v_ln2_b]
    small_g = [g.reshape(w.shape) for g, w in zip(small_g, small_w)]
    pack = lambda lst: jnp.concatenate([a.reshape(-1) for a in lst]).reshape(1, -1)
    s_delta, s_nm, s_nv = _adam_small(pack(small_g), pack(small_w), pack(small_m), pack(small_v))
    sizes = [w.size for w in small_w]
    soffs = [sum(sizes[:i]) for i in range(len(sizes))]
    unpack = lambda p: [p[0, o:o + n].reshape(w.shape) for o, n, w in zip(soffs, sizes, small_w)]
    s_delta, s_nm, s_nv = unpack(s_delta), unpack(s_nm), unpack(s_nv)

    def ordered(small, big):
        sm = list(small)
        big = [tp(b) if n in transposed else b for n, b in enumerate(big)]
        bg = [b.reshape(w.shape) for b, w in zip(big, [w_in, mla_w_uq, mla_w_ukv, mem_w_kv, w_branch, w_o, w_ffn_gate, w_ffn_up, w_ffn_down])]
        return [sm[0], sm[1], sm[2], bg[0], sm[3], sm[4], sm[5], bg[1], bg[2], bg[3], bg[4], bg[5], sm[6], sm[7], bg[6], bg[7], bg[8], sm[8], sm[9]]

    grads = ordered(small_g, [o[0] for o in big_out])
    deltas = ordered(s_delta, [o[1] for o in big_out])
    new_m = ordered(s_nm, [o[2] for o in big_out])
    new_v = ordered(s_nv, [o[3] for o in big_out])
    return (loss, grad_x[None], *grads, *deltas, *new_m, *new_v)
```

```python
import functools

import jax
import jax.numpy as jnp
from jax import lax
from jax.experimental import pallas as pl
from jax.experimental.pallas import tpu as pltpu

F32 = jnp.float32
CDT = jnp.bfloat16
MESH = pl.DeviceIdType.MESH
N_DEV = 8
V7X_VMEM_LIMIT = 60 * 1024 * 1024
LANE = 128
SUB = 8

HG_HEADS, HG_DK, HG_CHUNK = 8, 128, 64
HG_HPS = 8
HG_BWD_GROUP = 8
HG_W = HG_HEADS * HG_DK
MLA_HEADS, MLA_RANK, MLA_NOPE, MLA_ROPE, MLA_V = 8, 512, 128, 64, 128
MLA_QK = MLA_NOPE + MLA_ROPE
MLA_SCALE = MLA_QK ** -0.5
MLA_QSCALE = MLA_SCALE * 1.4426950408889634
VT_ROWS = LANE + 16
MLA_TQ = 1024
MLA_BWD_SLABS = 1
MLA_TQ_FWD = 2048
MLA_FWD_SLABS = 8
MLA_W = MLA_HEADS * MLA_V
MEM_HEADS, MEM_HD = 4, 256
MEM_W = MEM_HEADS * MEM_HD
BR_W = 1024
ROPE_THETA = 10000.0
ALPHA = 2.0 ** 0.25
LN_EPS = 1e-5
RMS_EPS = 1e-6
ADAM_LR, ADAM_B1, ADAM_B2, ADAM_EPS, ADAM_WD, ADAM_STEP = 0.001, 0.9, 0.999, 1e-08, 0.01, 10
ADAM_BLOCK_ELEMS = 256 * 1024

OFF_Q, OFF_I, OFF_FB, OFF_FF, OFF_G = 0, 1024, 2048, 3072, 4096
OFF_CQ, OFF_CKV, OFF_QM, OFF_GATE = 5120, 5632, 6144, 7168
KR_PAD = 512


def _cparams(n_grid, side_effects=False):
    return pltpu.CompilerParams(dimension_semantics=("arbitrary",) * n_grid, vmem_limit_bytes=V7X_VMEM_LIMIT,
                                has_side_effects=side_effects)


def _tile(n, pref, *offsets):
    if n <= pref and all(o % n == 0 for o in offsets):
        return n
    t = (min(pref, n) // LANE) * LANE
    while t >= LANE:
        if n % t == 0 and all(o % t == 0 for o in offsets):
            return t
        t -= LANE
    raise ValueError(f"no tile for {n} {pref} {offsets}")


def _rtile(n, pref):
    if n <= pref:
        return n
    t = (pref // SUB) * SUB
    while t >= SUB:
        if n % t == 0:
            return t
        t -= SUB
    raise ValueError(f"no row tile for {n} {pref}")


def _dot(a, b, dims):
    return lax.dot_general(a.astype(CDT), b.astype(CDT), (dims, ((), ())), preferred_element_type=F32)


def _nn(a, b):
    return _dot(a, b, ((1,), (0,)))


def _nt(a, b):
    return _dot(a, b, ((1,), (1,)))


def _tn(a, b):
    return _dot(a, b, ((0,), (0,)))


_DOTS = {"nn": _nn, "nt": _nt, "tn": _tn}


def _sigmoid(x):
    return 1.0 / (1.0 + jnp.exp(-x))


def _rowsum8(v):
    r, w = v.shape
    return v.reshape(r // SUB, SUB, w).sum(axis=0)


def _my_place():
    x, y, c = lax.axis_index("x"), lax.axis_index("y"), lax.axis_index("c")
    return x, y, c, 4 * x + 2 * y + c


def _peer(x, y, c, kk):
    px = 1 - x if kk & 4 else x
    py = 1 - y if kk & 2 else y
    pc = 1 - c if kk & 1 else c
    return (px, py, pc), 4 * px + 2 * py + pc


class _Xchg:
    def __init__(self, arrs, scatter):
        self.arrs, self.scatter, self.n = list(arrs), scatter, len(arrs)
        hbm = pl.BlockSpec(memory_space=pl.ANY)
        self.specs = [hbm] * self.n
        self.out_shape = [jax.ShapeDtypeStruct(((N_DEV,) + a.shape[1:]) if scatter else ((N_DEV,) + a.shape), a.dtype)
                          for a in self.arrs]
        ncp = self.n * (N_DEV - 1)
        self.scratch = [pltpu.SemaphoreType.DMA((ncp,)), pltpu.SemaphoreType.DMA((ncp,)), pltpu.SemaphoreType.DMA((self.n,))]

    def _copies(self, ins, outs, send, recv, loc):
        x, y, c, me = _my_place()
        copies = []
        for w in range(self.n):
            copies.append(pltpu.make_async_copy(ins[w].at[me] if self.scatter else ins[w], outs[w].at[me], loc.at[w]))
            for kk in range(1, N_DEV):
                peer, pid = _peer(x, y, c, kk)
                s = w * (N_DEV - 1) + kk - 1
                copies.append(pltpu.make_async_remote_copy(
                    src_ref=ins[w].at[pid] if self.scatter else ins[w], dst_ref=outs[w].at[me],
                    send_sem=send.at[s], recv_sem=recv.at[s], device_id=peer, device_id_type=MESH))
        return copies

    def start(self, ins, outs, sems):
        for cp in self._copies(ins, outs, *sems):
            cp.start()

    def wait(self, ins, outs, sems):
        for cp in self._copies(ins, outs, *sems):
            cp.wait()


def _all_gather_two_level(name, arrs):
    n = len(arrs)
    NC = N_DEV - 1

    def body(*refs):
        ins, outs = refs[:n], refs[n:2 * n]
        send, recv, loc = refs[2 * n:]
        x, y, c, me = _my_place()
        sibling = (x, y, 1 - c)
        chips = [(1 - x, y), (x, 1 - y), (1 - x, 1 - y)]
        slot = lambda px, py, pc: 4 * px + 2 * py + pc

        def copy(w, k, block, to, src=None):
            dst = outs[w].at[slot(*block)]
            return pltpu.make_async_remote_copy(src_ref=dst if src is None else src, dst_ref=dst,
                                                send_sem=send.at[w * NC + k], recv_sem=recv.at[w * NC + k],
                                                device_id=to, device_id_type=MESH)

        mine = [pltpu.make_async_copy(ins[w], outs[w].at[me], loc.at[w]) for w in range(n)]
        for cp in mine:
            cp.start()
        first = []
        for w in range(n):
            first.append(copy(w, 0, (x, y, c), sibling, src=ins[w]))
            first += [copy(w, 1 + j, (x, y, c), (*chip, c), src=ins[w]) for j, chip in enumerate(chips)]
        for cp in first:
            cp.start()
        passed = []
        for j, chip in enumerate(chips):
            for w in range(n):
                copy(w, 1 + j, (*chip, c), (x, y, c)).wait_recv()
                fwd = copy(w, 4 + j, (*chip, c), sibling)
                fwd.start()
                passed.append(fwd)
        for w in range(n):
            copy(w, 0, sibling, (x, y, c)).wait_recv()
            for j, chip in enumerate(chips):
                copy(w, 4 + j, (*chip, 1 - c), (x, y, c)).wait_recv()
        for cp in first + passed:
            cp.wait_send()
        for cp in mine:
            cp.wait()

    hbm = pl.BlockSpec(memory_space=pl.ANY)
    return pl.pallas_call(
        body,
        name=name,
        in_specs=[hbm] * n,
        out_specs=[hbm] * n,
        out_shape=[jax.ShapeDtypeStruct((N_DEV,) + a.shape, a.dtype) for a in arrs],
        scratch_shapes=[pltpu.SemaphoreType.DMA((n * NC,)), pltpu.SemaphoreType.DMA((n * NC,)), pltpu.SemaphoreType.DMA((n,))],
        compiler_params=pltpu.CompilerParams(has_side_effects=True),
    )(*arrs)


def _fused_mm(name, mode, groups, M, N, K, tm, tn, tk, outs, epi, tiles=(), rows=(), n_racc=0, xchg=None, msplit=1):
    ni, nj, nk = M // tm, N // tn, K // tk
    assert M % tm == 0 and N % tn == 0 and K % tk == 0, (name, M, N, K, tm, tn, tk)
    assert n_racc == 0 or nj == 1
    assert msplit == 1 or (nk == 1 and n_racc == 0 and tm % (16 * msplit) == 0)
    dot = _DOTS[mode] if groups else None
    ins, in_specs = [], []
    for g in groups:
        for a, a_off, b, b_off in g:
            if mode == "tn":
                assert a_off % tm == 0
                in_specs.append(pl.BlockSpec((tk, tm), lambda i, j, k, o=a_off // tm: (k, i + o)))
            else:
                assert a_off % tk == 0
                in_specs.append(pl.BlockSpec((tm, tk), lambda i, j, k, o=a_off // tk: (i, k + o)))
            ins.append(a)
            if mode == "nt":
                assert b_off % tk == 0
                in_specs.append(pl.BlockSpec((tn, tk), lambda i, j, k, o=b_off // tk: (j, k + o)))
            else:
                assert b_off % tn == 0
                in_specs.append(pl.BlockSpec((tk, tn), lambda i, j, k, o=b_off // tn: (k, j + o)))
            ins.append(b)
    for arr, off in tiles:
        assert off % tn == 0
        ins.append(arr)
        in_specs.append(pl.BlockSpec((tm, tn), lambda i, j, k, o=off // tn: (i, j + o)))
    for arr, off in rows:
        assert off % tn == 0
        ins.append(arr)
        in_specs.append(pl.BlockSpec((1, tn), lambda i, j, k, o=off // tn: (0, j + o)))
    aliases = {}
    out_shape, out_specs = [], []
    for oi, (width, dtype, off, alias) in enumerate(outs):
        assert off % tn == 0
        if alias is not None:
            aliases[len(ins)] = oi
            ins.append(alias)
            in_specs.append(pl.BlockSpec(memory_space=pl.ANY))
        out_shape.append(jax.ShapeDtypeStruct((M, width), dtype))
        out_specs.append(pl.BlockSpec((tm, tn), lambda i, j, k, o=off // tn: (i, j + o)))
    for _ in range(n_racc):
        out_shape.append(jax.ShapeDtypeStruct((SUB, N), F32))
        out_specs.append(pl.BlockSpec((SUB, tn), lambda i, j, k: (0, 0)))
    n_alias = len(aliases)
    n_pairs = [len(g) for g in groups]
    use_scratch = nk > 1
    scratch = [pltpu.VMEM((tm, tn), F32) for _ in groups] if use_scratch else []
    nx = 0
    if xchg is not None:
        nx = xchg.n
        ins += xchg.arrs
        in_specs += xchg.specs
        out_shape += xchg.out_shape
        out_specs += xchg.specs
        scratch += xchg.scratch

    def body(*refs):
        it = iter(refs)
        pair_refs = [[(next(it), next(it)) for _ in range(n)] for n in n_pairs]
        tile_refs = [next(it) for _ in tiles]
        row_refs = [next(it) for _ in rows]
        for _ in range(n_alias):
            next(it)
        x_in = [next(it) for _ in range(nx)]
        out_refs = [next(it) for _ in outs]
        racc_refs = [next(it) for _ in range(n_racc)]
        x_out = [next(it) for _ in range(nx)]
        acc_refs = [next(it) for _ in groups] if use_scratch else []
        x_sems = list(it)
        i, j, k = pl.program_id(0), pl.program_id(1), pl.program_id(2)
        if nx:
            @pl.when((i == 0) & (j == 0) & (k == 0))
            def _():
                xchg.start(x_in, x_out, x_sems)

        def products():
            res = []
            for prs in pair_refs:
                s = None
                for a_ref, b_ref in prs:
                    d = dot(a_ref[...], b_ref[...])
                    s = d if s is None else s + d
                res.append(s)
            return res

        def finish(accs):
            out_v, racc_v = epi(accs, [t[...] for t in tile_refs], [r[...] for r in row_refs])
            for o_ref, v in zip(out_refs, out_v):
                o_ref[...] = v.astype(o_ref.dtype)
            for r_ref, v in zip(racc_refs, racc_v):
                part = _rowsum8(v)

                @pl.when(i == 0)
                def _():
                    r_ref[...] = part

                @pl.when(i > 0)
                def _():
                    r_ref[...] += part

        if not use_scratch and msplit > 1:
            ts = tm // msplit
            for s in range(msplit):
                rs = pl.ds(s * ts, ts)
                accs = []
                for prs in pair_refs:
                    acc = None
                    for a_ref, b_ref in prs:
                        dd = dot(a_ref[:, rs] if mode == "tn" else a_ref[rs, :], b_ref[...])
                        acc = dd if acc is None else acc + dd
                    accs.append(acc)
                out_v, _ = epi(accs, [t[rs, :] for t in tile_refs], [r[...] for r in row_refs])
                for o_ref, v in zip(out_refs, out_v):
                    o_ref[rs, :] = v.astype(o_ref.dtype)
        elif not use_scratch:
            finish(products())
        else:
            @pl.when(k == 0)
            def _():
                for acc in acc_refs:
                    acc[...] = jnp.zeros_like(acc)

            for acc, p in zip(acc_refs, products()):
                acc[...] += p

            @pl.when(k == nk - 1)
            def _():
                finish([acc[...] for acc in acc_refs])

        if nx:
            @pl.when((i == ni - 1) & (j == nj - 1) & (k == nk - 1))
            def _():
                xchg.wait(x_in, x_out, x_sems)

    res = pl.pallas_call(
        body,
        name=name,
        grid=(ni, nj, nk),
        in_specs=in_specs,
        out_specs=out_specs,
        out_shape=out_shape,
        scratch_shapes=scratch,
        input_output_aliases=aliases,
        compiler_params=_cparams(3, side_effects=nx > 0),
    )(*ins)
    return res


def _ln_stats(r):
    mu = jnp.mean(r, axis=-1, keepdims=True)
    xc = r - mu
    var = jnp.mean(xc * xc, axis=-1, keepdims=True)
    rstd = lax.rsqrt(var + LN_EPS)
    return xc * rstd, rstd


def _ln_bwd(dh, xhat, rstd, g):
    dxh = dh * g
    m1 = jnp.mean(dxh, axis=-1, keepdims=True)
    m2 = jnp.mean(dxh * xhat, axis=-1, keepdims=True)
    return rstd * (dxh - m1 - xhat * m2)


def _split3(x):
    hi = x.astype(CDT)
    r1 = x - hi.astype(F32)
    mid = r1.astype(CDT)
    lo = (r1 - mid.astype(F32)).astype(CDT)
    return hi, mid, lo


def _tri_matmul(tri, x):
    hi, mid, lo = _split3(x)
    return _nn(tri, hi) + _nn(tri, mid) + _nn(tri, lo)


def _dot3(dot, a, b):
    a_hi, b_hi = a.astype(CDT), b.astype(CDT)
    a_lo = (a - a_hi.astype(F32)).astype(CDT)
    b_lo = (b - b_hi.astype(F32)).astype(CDT)
    return dot(a_hi, b_hi) + dot(a_hi, b_lo) + dot(a_lo, b_hi)


def _gla_masks(reverse):
    C = HG_CHUNK
    r = lax.broadcasted_iota(jnp.int32, (C, C), 0)
    c = lax.broadcasted_iota(jnp.int32, (C, C), 1)
    keep = (c >= r) if reverse else (r >= c)
    return keep


def _m(fn, *lists):
    return [fn(*args) for args in zip(*lists)]


def _gla_chunk_fwd(qraw, fraw, lb, keep, reverse):
    C = HG_CHUNK
    end = 0 if reverse else C - 1
    tri = jnp.where(keep, 1.0, 0.0).astype(CDT)
    sq = _m(_sigmoid, qraw)
    q = _m(lambda x, s: x * s, qraw, sq)
    sg = _m(_sigmoid, fraw)
    f = _m(lambda l_, s: l_ + (1.0 - l_) * s, lb, sg)
    k = _m(lambda x: 1.0 - x, f)
    g = _m(jnp.log, f)
    b = _m(lambda x: _tri_matmul(tri, x), g)
    b_end = _m(lambda x: x[end:end + 1, :], b)
    b_mid = _m(lambda x: x[C // 2:C // 2 + 1, :], b)
    eq = _m(lambda x, m_: jnp.exp(x - m_), b, b_mid)
    ek = _m(lambda x, m_: jnp.exp(m_ - x), b, b_mid)
    eb = _m(jnp.exp, b)
    e2 = _m(lambda x, e_: jnp.exp(e_ - x), b, b_end)
    e_end = _m(jnp.exp, b_end)
    qt = _m(lambda x, e_: x * e_, q, eq)
    kt = _m(lambda x, e_: x * e_, k, ek)
    qs = _m(lambda x, e_: (x * e_).astype(CDT), q, eb)
    k2 = _m(lambda x, e_: (x * e_).astype(CDT), k, e2)
    a = _m(lambda x, y: jnp.where(keep, _dot3(_nt, x, y), 0.0).astype(CDT), qt, kt)
    return dict(sq=sq, q=q, sg=sg, f=f, k=k, eq=eq, ek=ek, eb=eb, e2=e2, e_end=e_end, qt=qt, kt=kt, qs=qs, k2=k2, a=a)


def _gla_fwd(proj, lbl4, f_off, reverse, name):
    S = proj.shape[0]
    C = HG_CHUNK
    R = _rtile(S, 512)
    cpb, nblk = R // C, S // R
    d = 1 if reverse else 0
    blk_map = (lambda b: nblk - 1 - b) if reverse else (lambda b: b)

    W = HG_HPS * HG_DK

    def body(q_ref, i_ref, f_ref, lb_ref, o_ref, st_ref, s_scr):
        @pl.when(pl.program_id(1) == 0)
        def _():
            s_scr[...] = jnp.zeros_like(s_scr)

        l = lb_ref[...]
        lbs = _sigmoid(l[2 * d:2 * d + 1, :] - l[2 * d + 1:2 * d + 2, :])
        keep = _gla_masks(reverse)
        heads = list(range(HG_HPS))
        css = [pl.ds(hh * HG_DK, HG_DK) for hh in heads]
        lb = [lbs[:, hh * HG_DK:(hh + 1) * HG_DK] for hh in heads]
        for cc in range(cpb):
            c = cpb - 1 - cc if reverse else cc
            sl = pl.ds(c * C, C)
            v = [i_ref[sl, cs] for cs in css]
            t = _gla_chunk_fwd([q_ref[sl, cs] for cs in css], [f_ref[sl, cs] for cs in css], lb, keep, reverse)
            st = [s_scr[hh] for hh in heads]
            o = _m(lambda qs, s_, a, v_: _nt(qs, s_) + _nn(a, v_), t["qs"], st, t["a"], v)
            new = _m(lambda e_, s_, v_, k2: e_ * s_ + _tn(v_, k2), t["e_end"], st, v, t["k2"])
            for hh in heads:
                st_ref[c, hh] = st[hh]
                o_ref[sl, css[hh]] = o[hh]
                s_scr[hh] = new[hh]

    col = lambda off: (lambda h, b: (blk_map(b), off // W + h))
    return pl.pallas_call(
        body,
        name=name,
        grid=(HG_HEADS // HG_HPS, nblk),
        in_specs=[
            pl.BlockSpec((R, W), col(OFF_Q)),
            pl.BlockSpec((R, W), col(OFF_I)),
            pl.BlockSpec((R, W), col(f_off)),
            pl.BlockSpec((4, W), lambda h, b: (0, h)),
        ],
        out_specs=[
            pl.BlockSpec((R, W), lambda h, b: (blk_map(b), h)),
            pl.BlockSpec((cpb, HG_HPS, HG_DK, HG_DK), lambda h, b: (blk_map(b), h, 0, 0)),
        ],
        out_shape=[
            jax.ShapeDtypeStruct((S, HG_W), F32),
            jax.ShapeDtypeStruct((S // C, HG_HEADS, HG_DK, HG_DK), F32),
        ],
        scratch_shapes=[pltpu.VMEM((HG_HPS, HG_DK, HG_DK), F32)],
        compiler_params=_cparams(2),
    )(proj, proj, proj, lbl4)


def _gla_bwd(proj, lbl4, f_off, reverse, do, states, dproj, prev, name):
    S = proj.shape[0]
    PW = proj.shape[1]
    C = HG_CHUNK
    R = _rtile(S, 512)
    cpb, nblk = R // C, S // R
    d = 1 if reverse else 0
    blk_map = (lambda b: b) if reverse else (lambda b: nblk - 1 - b)
    final = prev is not None

    if final:
        assert HG_HPS == HG_HEADS and (OFF_Q, OFF_I, f_off) == (0, HG_W, 2 * HG_W)

    def body(*refs):
        if final:
            q_ref, i_ref, f_ref, lb_ref, do_ref, st_ref, pq_ref, pi_ref, _dp, o3_ref, dl_ref, ds_scr = refs
            dq_ref = di_ref = df_ref = o3_ref
        else:
            q_ref, i_ref, f_ref, lb_ref, do_ref, st_ref, dq_ref, di_ref, df_ref, dl_ref, ds_scr = refs
        out_off = (OFF_Q, OFF_I, f_off) if final else (0, 0, 0)
        blk = pl.program_id(1)

        @pl.when(blk == 0)
        def _():
            ds_scr[...] = jnp.zeros_like(ds_scr)
            dl_ref[...] = jnp.zeros_like(dl_ref)

        l = lb_ref[...]
        lbs = _sigmoid(l[2 * d:2 * d + 1, :] - l[2 * d + 1:2 * d + 2, :])
        keep = _gla_masks(reverse)
        keep_t = _gla_masks(not reverse)
        tri_t = jnp.where(keep_t, 1.0, 0.0).astype(CDT)
        end = 0 if reverse else C - 1
        is_end = lax.broadcasted_iota(jnp.int32, (C, HG_DK), 0) == end
        dl_all = [jnp.zeros((SUB, HG_DK), F32) for _ in range(HG_HPS)]
        gsz = HG_BWD_GROUP
        for cc, heads in [(cc, list(range(g0, g0 + gsz))) for cc in range(cpb) for g0 in range(0, HG_HPS, gsz)]:
            css = [pl.ds(hh * HG_DK, HG_DK) for hh in heads]
            lb = [lbs[:, hh * HG_DK:(hh + 1) * HG_DK] for hh in heads]
            dl_acc = [dl_all[hh] for hh in heads]
            c = cc if reverse else cpb - 1 - cc
            sl = pl.ds(c * C, C)
            qraw = [q_ref[sl, cs] for cs in css]
            v = [i_ref[sl, cs] for cs in css]
            t = _gla_chunk_fwd(qraw, [f_ref[sl, cs] for cs in css], lb, keep, reverse)
            dob = [do_ref[sl, cs].astype(CDT) for cs in css]
            vb = _m(lambda x: x.astype(CDT), v)
            st = [st_ref[c, hh] for hh in heads]
            ds = [ds_scr[hh] for hh in heads]
            dsb = _m(lambda x: x.astype(CDT), ds)
            d_qs = _m(_nn, dob, st)
            d_a = _m(lambda x, y: jnp.where(keep, _nt(x, y), 0.0), dob, vb)
            d_qt = _m(lambda x, y: _dot3(_nn, x, y), d_a, t["kt"])
            d_kt = _m(lambda x, y: _dot3(_tn, x, y), d_a, t["qt"])
            d_v = _m(lambda a, x, k2, s_: _tn(a, x) + _nt(k2, s_), t["a"], dob, t["k2"], dsb)
            d_k2 = _m(_nn, vb, dsb)
            d_e = _m(lambda s_, x: jnp.sum(s_ * x, axis=0, keepdims=True), st, ds)
            new_ds = _m(lambda e_, x, y, qs: e_ * x + _tn(y, qs), t["e_end"], ds, dob, t["qs"])
            dq = _m(lambda a, ea, b_, eb_: a * ea + b_ * eb_, d_qt, t["eq"], d_qs, t["eb"])
            dk = _m(lambda a, ea, b_, eb_: a * ea + b_ * eb_, d_kt, t["ek"], d_k2, t["e2"])
            db_end = _m(lambda x, k_, e2, de, ee: jnp.sum(x * (k_ * e2), axis=0, keepdims=True) + de * ee,
                        d_k2, t["k"], t["e2"], d_e, t["e_end"])
            db = _m(lambda q_, dq_, k_, dk_, be: q_ * dq_ - k_ * dk_ + jnp.where(is_end, be, 0.0),
                    t["q"], dq, t["k"], dk, db_end)
            dg = _m(lambda x: _tri_matmul(tri_t, x), db)
            df = _m(lambda g_, f_, dk_: g_ / f_ - dk_, dg, t["f"], dk)
            dfraw = _m(lambda x, l_, s_: x * (1.0 - l_) * s_ * (1.0 - s_), df, lb, t["sg"])
            dl_acc = _m(lambda acc, x, s_: acc + _rowsum8(x * (1.0 - s_)), dl_acc, df, t["sg"])
            dqraw = _m(lambda x, s_, r: x * (s_ * (1.0 + r * (1.0 - s_))), dq, t["sq"], qraw)
            if final:
                dqraw = [x + pq_ref[sl, cs] for x, cs in zip(dqraw, css)]
                d_v = [x + pi_ref[sl, cs] for x, cs in zip(d_v, css)]
            for n, hh in enumerate(heads):
                dl_all[hh] = dl_acc[n]
                ds_scr[hh] = new_ds[n]
                for ref, off, val in zip((dq_ref, di_ref, df_ref), out_off, (dqraw[n], d_v[n], dfraw[n])):
                    ref[sl, pl.ds(off + hh * HG_DK, HG_DK)] = val.astype(ref.dtype)
        dl_ref[...] += jnp.concatenate(dl_all, axis=1) * (lbs * (1.0 - lbs))

    W = HG_HPS * HG_DK
    col = lambda off: (lambda h, b: (blk_map(b), off // W + h))
    blk = lambda: pl.BlockSpec((R, W), lambda h, b: (blk_map(b), h))
    ins = [proj, proj, proj, lbl4, do, states]
    in_specs = [
        pl.BlockSpec((R, W), col(OFF_Q)),
        pl.BlockSpec((R, W), col(OFF_I)),
        pl.BlockSpec((R, W), col(f_off)),
        pl.BlockSpec((4, W), lambda h, b: (0, h)),
        blk(),
        pl.BlockSpec((cpb, HG_HPS, HG_DK, HG_DK), lambda h, b: (blk_map(b), h, 0, 0)),
    ]
    dl_shape = jax.ShapeDtypeStruct((SUB, HG_W), F32)
    dl_spec = pl.BlockSpec((SUB, W), lambda h, b: (0, h))
    dp_shape = jax.ShapeDtypeStruct((S, PW), CDT)
    if final:
        ins += [prev[0], prev[1], dproj]
        in_specs += [blk(), blk(), pl.BlockSpec(memory_space=pl.ANY)]
        out_shape = [dp_shape, dl_shape]
        out_specs = [pl.BlockSpec((R, 3 * HG_W), lambda h, b: (blk_map(b), 0)), dl_spec]
        aliases = {8: 0}
    else:
        out_shape = [jax.ShapeDtypeStruct((S, HG_W), F32), jax.ShapeDtypeStruct((S, HG_W), F32), dp_shape, dl_shape]
        out_specs = [blk(), blk(), pl.BlockSpec((R, W), col(f_off)), dl_spec]
        aliases = {}
        if dproj is not None:
            ins += [dproj]
            in_specs += [pl.BlockSpec(memory_space=pl.ANY)]
            aliases = {6: 2}
    if (not final) and dproj is not None:
        def body_wrapped(*refs, _b=body):
            _b(*refs[:6], *refs[7:])
        kern = body_wrapped
    else:
        kern = body
    res = pl.pallas_call(
        kern,
        name=name,
        grid=(HG_HEADS // HG_HPS, nblk),
        in_specs=in_specs,
        out_specs=out_specs,
        out_shape=out_shape,
        scratch_shapes=[pltpu.VMEM((HG_HPS, HG_DK, HG_DK), F32)],
        input_output_aliases=aliases,
        compiler_params=_cparams(2),
    )(*ins)
    if final:
        return res[0], None, None, res[1]
    dq, di, dproj, dl = res
    return dproj, dq, di, dl


def _hgrn_post_fwd(o_f, o_b, proj, norm_g):
    S = o_f.shape[0]

    def epi(accs, tiles, rows):
        of, ob, graw = tiles
        ng = rows[0][:, :HG_DK]
        o = of + ob
        ys = []
        for h in range(HG_HEADS):
            oh = o[:, h * HG_DK:(h + 1) * HG_DK]
            rs = lax.rsqrt(jnp.mean(oh * oh, axis=-1, keepdims=True) + RMS_EPS)
            ys.append(oh * rs * ng * _sigmoid(graw[:, h * HG_DK:(h + 1) * HG_DK]))
        return [jnp.concatenate(ys, axis=1)], []

    tm = _rtile(S, 512)
    (y,) = _fused_mm("hgrn_post_fwd", "nn", [], S, HG_W, 1, tm, HG_W, 1, [(HG_W, CDT, 0, None)], epi,
                     tiles=[(o_f, 0), (o_b, 0), (proj, OFF_G)], rows=[(jnp.tile(norm_g, (1, HG_HEADS)), 0)])
    return y


def _hgrn_post_bwd(dy, o_f, o_b, proj, norm_g, dproj):
    S = o_f.shape[0]

    def epi(accs, tiles, rows):
        dyv, of, ob, graw = tiles
        ng = rows[0][:, :HG_DK]
        o = of + ob
        dos, dgs, dns = [], [], []
        for h in range(HG_HEADS):
            sl = slice(h * HG_DK, (h + 1) * HG_DK)
            oh, gh, dyh = o[:, sl], graw[:, sl], dyv[:, sl].astype(F32)
            rs = lax.rsqrt(jnp.mean(oh * oh, axis=-1, keepdims=True) + RMS_EPS)
            xh = oh * rs
            sg = _sigmoid(gh)
            dn = dyh * sg
            dgs.append(dyh * (xh * ng) * sg * (1.0 - sg))
            dns.append(dn * xh)
            dxh = dn * ng
            dos.append(rs * (dxh - xh * jnp.mean(dxh * xh, axis=-1, keepdims=True)))
        return [jnp.concatenate(dos, axis=1), jnp.concatenate(dgs, axis=1)], [jnp.concatenate(dns, axis=1)]

    tm = _rtile(S, 512)
    do, dproj, dn = _fused_mm("hgrn_post_bwd", "nn", [], S, HG_W, 1, tm, HG_W, 1,
                              [(HG_W, F32, 0, None), (dproj.shape[1], CDT, OFF_G, dproj)], epi,
                              tiles=[(dy, 0), (o_f, 0), (o_b, 0), (proj, OFF_G)],
                              rows=[(jnp.tile(norm_g, (1, HG_HEADS)), 0)], n_racc=1)
    return do, dproj, dn


def _copy_into(name, src, dst, off):
    S, W = src.shape
    tm = _rtile(S, 512)
    (dst,) = _fused_mm(name, "nn", [], S, W, 1, tm, W, 1, [(dst.shape[1], dst.dtype, off, dst)],
                       lambda accs, tiles, rows: ([tiles[0]], []), tiles=[(src, 0)])
    return dst


def _rms_stats(x):
    rs = lax.rsqrt(jnp.mean(x * x, axis=-1, keepdims=True) + RMS_EPS)
    return x * rs, rs


def _mla_up(proj, cf, sf, g_cq, g_ckv, wuq_p, wukv):
    S = proj.shape[0]
    tm = _rtile(S, 512)
    H = MLA_HEADS

    def body(cq_ref, ckv_ref, kr_ref, krot_ref, cf_ref, sf_ref, gq_ref, gkv_ref, wq_ref, wkv_ref,
             q_ref, k_ref, v_ref, vt_ref, cqn_ref, ckvn_ref):
        cqn = (_rms_stats(cq_ref[...])[0] * gq_ref[...]).astype(CDT)
        ckvn = (_rms_stats(ckv_ref[...])[0] * gkv_ref[...]).astype(CDT)
        cqn_ref[...] = cqn
        ckvn_ref[...] = ckvn
        cfv, sfv = cf_ref[...], sf_ref[...]
        k_roped = (kr_ref[...] * cfv + krot_ref[...] * sfv).astype(CDT)
        ones = jnp.ones((VT_ROWS - LANE, tm), CDT)
        for h in range(H):
            r = _nn(cqn, wq_ref[h]) * MLA_QSCALE
            q_ref[h, :, 0:LANE] = r[:, 0:LANE].astype(CDT)
            q_ref[h, :, LANE:2 * LANE] = (r[:, LANE:2 * LANE] * cfv + r[:, 2 * LANE:3 * LANE] * sfv).astype(CDT)
            kv = _nn(ckvn, wkv_ref[h])
            k_ref[h, :, 0:LANE] = kv[:, 0:LANE].astype(CDT)
            k_ref[h, :, LANE:2 * LANE] = k_roped
            vv = kv[:, LANE:2 * LANE]
            v_ref[h] = vv.astype(CDT)
            vt_ref[h, 0, 0:LANE, :] = vv.T.astype(CDT)
            vt_ref[h, 0, LANE:VT_ROWS, :] = ones

    PWb = proj.shape[1]
    kr_off = PWb - KR_PAD
    cspec = lambda off, w: pl.BlockSpec((tm, w), lambda i, o=off // w: (i, o))
    return pl.pallas_call(
        body,
        name="mla_up_fwd",
        grid=(S // tm,),
        in_specs=[
            cspec(OFF_CQ, MLA_RANK), cspec(OFF_CKV, MLA_RANK), cspec(kr_off, LANE), cspec(kr_off + LANE, LANE),
            pl.BlockSpec((tm, LANE), lambda i: (i, 0)), pl.BlockSpec((tm, LANE), lambda i: (i, 0)),
            pl.BlockSpec((1, MLA_RANK), lambda i: (0, 0)), pl.BlockSpec((1, MLA_RANK), lambda i: (0, 0)),
            pl.BlockSpec((H, MLA_RANK, 3 * LANE), lambda i: (0, 0, 0)),
            pl.BlockSpec((H, MLA_RANK, 2 * LANE), lambda i: (0, 0, 0)),
        ],
        out_specs=[
            pl.BlockSpec((H, tm, 2 * LANE), lambda i: (0, i, 0)),
            pl.BlockSpec((H, tm, 2 * LANE), lambda i: (0, i, 0)),
            pl.BlockSpec((H, tm, LANE), lambda i: (0, i, 0)),
            pl.BlockSpec((H, 1, VT_ROWS, tm), lambda i: (0, i, 0, 0)),
            pl.BlockSpec((tm, MLA_RANK), lambda i: (i, 0)),
            pl.BlockSpec((tm, MLA_RANK), lambda i: (i, 0)),
        ],
        out_shape=[
            jax.ShapeDtypeStruct((H, S, 2 * LANE), CDT), jax.ShapeDtypeStruct((H, S, 2 * LANE), CDT),
            jax.ShapeDtypeStruct((H, S, LANE), CDT), jax.ShapeDtypeStruct((H, S // tm, VT_ROWS, tm), CDT),
            jax.ShapeDtypeStruct((S, MLA_RANK), CDT), jax.ShapeDtypeStruct((S, MLA_RANK), CDT),
        ],
        compiler_params=_cparams(1),
    )(proj, proj, proj, proj, cf, sf, g_cq, g_ckv, wuq_p, wukv)


def _mla_attn_fwd(q_cat, k_cat, vt, xchg=None):
    H, S, _ = q_cat.shape
    tq = _tile(S, MLA_TQ_FWD)
    _, nkb, _, tk = vt.shape
    nq = S // tq
    nx = xchg.n if xchg is not None else 0

    def body(*refs):
        q_ref, k_ref, vt_ref = refs[:3]
        x_in = refs[3:3 + nx]
        y_ref, ot_ref, lse_ref = refs[3 + nx:6 + nx]
        x_out = refs[6 + nx:6 + 2 * nx]
        m_scr, acc_scr = refs[6 + 2 * nx:8 + 2 * nx]
        x_sems = refs[8 + 2 * nx:]
        h, i = pl.program_id(0), pl.program_id(1)
        if nx:
            @pl.when((h == 0) & (i == 0))
            def _():
                xchg.start(x_in, x_out, x_sems)

        nsub = MLA_FWD_SLABS if tq % (MLA_FWD_SLABS * LANE) == 0 else 1
        ws = tq // nsub
        subs = [pl.ds(s * ws, ws) for s in range(nsub)]
        qs = [q_ref[0, sb, :] for sb in subs]
        m_scr[...] = jnp.full_like(m_scr, -jnp.inf)
        acc_scr[...] = jnp.zeros_like(acc_scr)

        def step(j, carry):
            kj = k_ref[0, pl.ds(pl.multiple_of(j * tk, tk), tk), :]
            vtj = vt_ref[0, j]
            sts = [_nt(kj, qq) for qq in qs]
            m_old = [m_scr[:, sb] for sb in subs]
            m_new = _m(lambda mo, st: jnp.maximum(mo, jnp.max(st, axis=0, keepdims=True)), m_old, sts)
            pts = _m(lambda st, mn: jnp.exp2(st - mn), sts, m_new)
            pvs = _m(lambda pt: _nn(vtj, pt), pts)
            for sb, mo, mn, pv in zip(subs, m_old, m_new, pvs):
                acc_scr[:, sb] = jnp.exp2(mo - mn) * acc_scr[:, sb] + pv
                m_scr[:, sb] = mn
            return carry

        lax.fori_loop(0, nkb, step, 0, unroll=4 if nkb % 4 == 0 else 1)
        l = acc_scr[LANE:LANE + 1, :]
        ot = acc_scr[0:LANE, :] / l
        ot_ref[0] = ot
        y_ref[...] = ot.T.astype(CDT)
        lse_ref[0, 0] = m_scr[...] + jnp.log2(l)

        if nx:
            @pl.when((h == H - 1) & (i == nq - 1))
            def _():
                xchg.wait(x_in, x_out, x_sems)

    return pl.pallas_call(
        body,
        name="mla_attn_fwd",
        grid=(H, nq),
        in_specs=[
            pl.BlockSpec((1, tq, 2 * LANE), lambda h, i: (h, i, 0)),
            pl.BlockSpec((1, S, 2 * LANE), lambda h, i: (h, 0, 0)),
            pl.BlockSpec((1, nkb, VT_ROWS, tk), lambda h, i: (h, 0, 0, 0)),
        ] + (xchg.specs if nx else []),
        out_specs=[
            pl.BlockSpec((tq, LANE), lambda h, i: (i, h)),
            pl.BlockSpec((1, LANE, tq), lambda h, i: (h, 0, i)),
            pl.BlockSpec((1, 1, 1, tq), lambda h, i: (h, i, 0, 0)),
        ] + (xchg.specs if nx else []),
        out_shape=[
            jax.ShapeDtypeStruct((S, H * LANE), CDT),
            jax.ShapeDtypeStruct((H, LANE, S), F32),
            jax.ShapeDtypeStruct((H, nq, 1, tq), F32),
        ] + (xchg.out_shape if nx else []),
        scratch_shapes=[pltpu.VMEM((1, tq), F32), pltpu.VMEM((VT_ROWS, tq), F32)] + (xchg.scratch if nx else []),
        compiler_params=_cparams(2, side_effects=nx > 0),
    )(q_cat, k_cat, vt, *(xchg.arrs if nx else []))


def _mla_delta(dy, ot):
    H, _, S = ot.shape
    tq = _tile(S, MLA_TQ)
    nq = S // tq

    def body(dy_ref, ot_ref, d_ref):
        d_ref[0, 0] = jnp.sum(dy_ref[...].astype(F32).T * ot_ref[0], axis=0, keepdims=True)

    return pl.pallas_call(
        body,
        name="mla_delta",
        grid=(H, nq),
        in_specs=[pl.BlockSpec((tq, LANE), lambda h, i: (i, h)), pl.BlockSpec((1, LANE, tq), lambda h, i: (h, 0, i))],
        out_specs=pl.BlockSpec((1, 1, 1, tq), lambda h, i: (h, i, 0, 0)),
        out_shape=jax.ShapeDtypeStruct((H, nq, 1, tq), F32),
        compiler_params=_cparams(2),
    )(dy, ot)


def _mla_attn_bwd(q_cat, k_cat, v, dy, lse, delta, xchg=None):
    H, S, _ = q_cat.shape
    _, nq, _, tq = lse.shape
    tk = _tile(S, 1024)
    nkb = S // tk
    nx = xchg.n if xchg is not None else 0

    def body(*refs):
        k_ref, v_ref, q_ref, do_ref, lse_ref, dl_ref = refs[:6]
        x_in = refs[6:6 + nx]
        dk_ref, dv_ref, dq_ref = refs[6 + nx:9 + nx]
        x_out = refs[9 + nx:9 + 2 * nx]
        dk_scr, dv_scr = refs[9 + 2 * nx:11 + 2 * nx]
        x_sems = refs[11 + 2 * nx:]
        hd, ki = pl.program_id(0), pl.program_id(1)
        if nx:
            @pl.when((hd == 0) & (ki == 0))
            def _():
                xchg.start(x_in, x_out, x_sems)

        @pl.when(ki == 0)
        def _():
            dq_ref[...] = jnp.zeros_like(dq_ref)

        kb, vb = k_ref[0], v_ref[0]
        dk_scr[...] = jnp.zeros_like(dk_scr)
        dv_scr[...] = jnp.zeros_like(dv_scr)

        nsub = MLA_BWD_SLABS if tq % (MLA_BWD_SLABS * LANE) == 0 else 1
        ws = tq // nsub

        def step(i, carry):
            rows = [pl.ds(pl.multiple_of(i * tq + s * ws, ws), ws) for s in range(nsub)]
            lanes = [slice(s * ws, (s + 1) * ws) for s in range(nsub)]
            qc = [q_ref[0, r, :] for r in rows]
            doc = [do_ref[r, :] for r in rows]
            lse_i, dl_i = lse_ref[0, i], dl_ref[0, i]
            st = _m(lambda q_: _nt(kb, q_), qc)
            dp = _m(lambda d_: _nt(vb, d_), doc)
            pt = _m(lambda s_, ln: jnp.exp2(s_ - lse_i[:, ln]), st, lanes)
            dst = _m(lambda p_, d_, ln: (p_ * (d_ - dl_i[:, ln])).astype(CDT), pt, dp, lanes)
            dv_scr[...] += sum(_m(_nn, pt, doc))
            dk_scr[...] += sum(_m(_nn, dst, qc))
            dqs = _m(lambda d_: _tn(d_, kb), dst)
            for r, dq_ in zip(rows, dqs):
                dq_ref[0, r, :] += dq_
            return carry

        lax.fori_loop(0, nq, step, 0, unroll=2 if nq % 2 == 0 else 1)
        dk_ref[0] = dk_scr[...] * (MLA_SCALE / MLA_QSCALE)
        dv_ref[0] = dv_scr[...]

        if nx:
            @pl.when((hd == H - 1) & (ki == nkb - 1))
            def _():
                xchg.wait(x_in, x_out, x_sems)

    return pl.pallas_call(
        body,
        name="mla_attn_bwd",
        grid=(H, nkb),
        in_specs=[
            pl.BlockSpec((1, tk, 2 * LANE), lambda h, j: (h, j, 0)),
            pl.BlockSpec((1, tk, LANE), lambda h, j: (h, j, 0)),
            pl.BlockSpec((1, S, 2 * LANE), lambda h, j: (h, 0, 0)),
            pl.BlockSpec((S, LANE), lambda h, j: (0, h)),
            pl.BlockSpec((1, nq, 1, tq), lambda h, j: (h, 0, 0, 0)),
            pl.BlockSpec((1, nq, 1, tq), lambda h, j: (h, 0, 0, 0)),
        ] + (xchg.specs if nx else []),
        out_specs=[
            pl.BlockSpec((1, tk, 2 * LANE), lambda h, j: (h, j, 0)),
            pl.BlockSpec((1, tk, LANE), lambda h, j: (h, j, 0)),
            pl.BlockSpec((1, S, 2 * LANE), lambda h, j: (h, 0, 0)),
        ] + (xchg.specs if nx else []),
        out_shape=[
            jax.ShapeDtypeStruct((H, S, 2 * LANE), F32),
            jax.ShapeDtypeStruct((H, S, LANE), F32),
            jax.ShapeDtypeStruct((H, S, 2 * LANE), F32),
        ] + (xchg.out_shape if nx else []),
        scratch_shapes=[pltpu.VMEM((tk, 2 * LANE), F32), pltpu.VMEM((tk, LANE), F32)] + (xchg.scratch if nx else []),
        compiler_params=_cparams(2, side_effects=nx > 0),
    )(k_cat, v, q_cat, dy, lse, delta, *(xchg.arrs if nx else []))


def _mla_up_bwd(dq_cat, dk_cat, dv, proj, cf, sf, g_cq, g_ckv, wuq_p, wukv, dproj):
    H, S, _ = dq_cat.shape
    tm = _rtile(S, 256)
    PW = proj.shape[1]
    kr_off = PW - KR_PAD

    assert OFF_CKV == OFF_CQ + MLA_RANK and OFF_CQ % (2 * MLA_RANK) == 0

    def body(dq_ref, dk_ref, dv_ref, cq_ref, ckv_ref, cf_ref, sf_ref, gq_ref, gkv_ref, wq_ref, wkv_ref, _dp,
             dqp_ref, dkvp_ref, dc_ref, dkr_ref, dgq_ref, dgkv_ref):
        i = pl.program_id(0)
        dcq_ref, dckv_ref = dc_ref.at[:, 0:MLA_RANK], dc_ref.at[:, MLA_RANK:2 * MLA_RANK]
        cfv, sfv = cf_ref[...], sf_ref[...]
        aq = jnp.zeros((tm, MLA_RANK), F32)
        akv = jnp.zeros((tm, MLA_RANK), F32)
        akr = jnp.zeros((tm, LANE), F32)
        for h in range(H):
            dq = dq_ref[h] * MLA_SCALE
            dqr = dq[:, LANE:2 * LANE]
            dqp = jnp.concatenate([dq[:, 0:LANE], dqr * cfv, dqr * sfv], axis=1).astype(CDT)
            dqp_ref[:, pl.ds(h * 3 * LANE, 3 * LANE)] = dqp
            aq = aq + _nt(dqp, wq_ref[h])
            dk = dk_ref[h]
            dkvp = jnp.concatenate([dk[:, 0:LANE], dv_ref[h]], axis=1).astype(CDT)
            dkvp_ref[:, pl.ds(h * 2 * LANE, 2 * LANE)] = dkvp
            akv = akv + _nt(dkvp, wkv_ref[h])
            akr = akr + dk[:, LANE:2 * LANE]

        def rms_bwd(c_ref, g_ref, dn, d_ref, dg_ref):
            xh, rs = _rms_stats(c_ref[...])
            dxh = dn * g_ref[...]
            d_ref[...] = (rs * (dxh - xh * jnp.mean(dxh * xh, axis=-1, keepdims=True))).astype(d_ref.dtype)
            part = _rowsum8(dn * xh)

            @pl.when(i == 0)
            def _():
                dg_ref[...] = part

            @pl.when(i > 0)
            def _():
                dg_ref[...] += part

        rms_bwd(cq_ref, gq_ref, aq, dcq_ref, dgq_ref)
        rms_bwd(ckv_ref, gkv_ref, akv, dckv_ref, dgkv_ref)
        dkr_ref[...] = jnp.concatenate([akr * cfv, akr * sfv, jnp.zeros((tm, KR_PAD - 2 * LANE), F32)], axis=1).astype(dkr_ref.dtype)

    cspec = lambda off, w: pl.BlockSpec((tm, w), lambda i, o=off // w: (i, o))
    hspec = lambda w: pl.BlockSpec((H, tm, w), lambda i: (0, i, 0))
    outs = pl.pallas_call(
        body,
        name="mla_up_bwd",
        grid=(S // tm,),
        in_specs=[
            hspec(2 * LANE), hspec(2 * LANE), hspec(LANE),
            cspec(OFF_CQ, MLA_RANK), cspec(OFF_CKV, MLA_RANK),
            pl.BlockSpec((tm, LANE), lambda i: (i, 0)), pl.BlockSpec((tm, LANE), lambda i: (i, 0)),
            pl.BlockSpec((1, MLA_RANK), lambda i: (0, 0)), pl.BlockSpec((1, MLA_RANK), lambda i: (0, 0)),
            pl.BlockSpec((H, MLA_RANK, 3 * LANE), lambda i: (0, 0, 0)),
            pl.BlockSpec((H, MLA_RANK, 2 * LANE), lambda i: (0, 0, 0)),
            pl.BlockSpec(memory_space=pl.ANY),
        ],
        out_specs=[
            pl.BlockSpec((tm, H * 3 * LANE), lambda i: (i, 0)), pl.BlockSpec((tm, H * 2 * LANE), lambda i: (i, 0)),
            pl.BlockSpec((tm, 2 * MLA_RANK), lambda i: (i, OFF_CQ // (2 * MLA_RANK))),
            pl.BlockSpec((tm, KR_PAD), lambda i: (i, 0)),
            pl.BlockSpec((SUB, MLA_RANK), lambda i: (0, 0)),
            pl.BlockSpec((SUB, MLA_RANK), lambda i: (0, 0)),
        ],
        out_shape=[
            jax.ShapeDtypeStruct((S, H * 3 * LANE), CDT), jax.ShapeDtypeStruct((S, H * 2 * LANE), CDT),
            jax.ShapeDtypeStruct(dproj.shape, dproj.dtype),
            jax.ShapeDtypeStruct((S, KR_PAD), CDT),
            jax.ShapeDtypeStruct((SUB, MLA_RANK), F32), jax.ShapeDtypeStruct((SUB, MLA_RANK), F32),
        ],
        input_output_aliases={11: 2},
        compiler_params=_cparams(1),
    )(dq_cat, dk_cat, dv, proj, proj, cf, sf, g_cq, g_ckv, wuq_p, wukv, dproj)
    dqp, dkvp, dproj, dkr, dgq, dgkv = outs
    dproj = _copy_into("dproj_kr", dkr, dproj, kr_off)
    return dproj, dqp, dkvp, dgq, dgkv


def _mem_softmax(q, k):
    s = _nt(q, k) * (MEM_HD ** -0.5)
    p = jnp.exp(s - jnp.max(s, axis=1, keepdims=True))
    return p / jnp.sum(p, axis=1, keepdims=True)


def _mem_attn_fwd(proj, memkv):
    S = proj.shape[0]
    Mm = memkv.shape[0]
    tm = _rtile(S, 1024)

    def body(q_ref, k_ref, v_ref, y_ref):
        pn = _mem_softmax(q_ref[...], k_ref[...])
        y_ref[...] = _nn(pn, v_ref[...]).astype(y_ref.dtype)

    return pl.pallas_call(
        body,
        name="mem_attn_fwd",
        grid=(S // tm, MEM_HEADS),
        in_specs=[
            pl.BlockSpec((tm, MEM_HD), lambda i, h: (i, OFF_QM // MEM_HD + h)),
            pl.BlockSpec((Mm, MEM_HD), lambda i, h: (0, h)),
            pl.BlockSpec((Mm, MEM_HD), lambda i, h: (0, MEM_HEADS + h)),
        ],
        out_specs=pl.BlockSpec((tm, MEM_HD), lambda i, h: (i, h)),
        out_shape=jax.ShapeDtypeStruct((S, MEM_W), CDT),
        compiler_params=_cparams(2),
    )(proj, memkv, memkv)


def _mem_attn_bwd(dy, proj, memkv, dproj):
    S = proj.shape[0]
    Mm = memkv.shape[0]
    tm = _rtile(S, 1024)
    scale = MEM_HD ** -0.5

    def body(dy_ref, q_ref, k_ref, v_ref, _dp, dq_ref, dk_ref, dv_ref):
        i = pl.program_id(1)
        q, k, dyv = q_ref[...].astype(CDT), k_ref[...], dy_ref[...]
        pn = _mem_softmax(q, k)
        dvp = _tn(pn, dyv)
        dp = _nt(dyv, v_ref[...])
        ds = pn * (dp - jnp.sum(dp * pn, axis=1, keepdims=True)) * scale
        dq_ref[...] = _nn(ds, k).astype(dq_ref.dtype)
        dkp = _tn(ds, q)

        @pl.when(i == 0)
        def _():
            dk_ref[...] = dkp
            dv_ref[...] = dvp

        @pl.when(i > 0)
        def _():
            dk_ref[...] += dkp
            dv_ref[...] += dvp

    dproj, dk, dv = pl.pallas_call(
        body,
        name="mem_attn_bwd",
        grid=(MEM_HEADS, S // tm),
        in_specs=[
            pl.BlockSpec((tm, MEM_HD), lambda h, i: (i, h)),
            pl.BlockSpec((tm, MEM_HD), lambda h, i: (i, OFF_QM // MEM_HD + h)),
            pl.BlockSpec((Mm, MEM_HD), lambda h, i: (0, h)),
            pl.BlockSpec((Mm, MEM_HD), lambda h, i: (0, MEM_HEADS + h)),
            pl.BlockSpec(memory_space=pl.ANY),
        ],
        out_specs=[
            pl.BlockSpec((tm, MEM_HD), lambda h, i: (i, OFF_QM // MEM_HD + h)),
            pl.BlockSpec((Mm, MEM_HD), lambda h, i: (0, h)),
            pl.BlockSpec((Mm, MEM_HD), lambda h, i: (0, h)),
        ],
        out_shape=[
            jax.ShapeDtypeStruct(dproj.shape, dproj.dtype),
            jax.ShapeDtypeStruct((Mm, MEM_W), F32),
            jax.ShapeDtypeStruct((Mm, MEM_W), F32),
        ],
        input_output_aliases={4: 0},
        compiler_params=_cparams(2),
    )(dy, proj, memkv, memkv, dproj)
    return dproj, dk, dv


def _small_allreduce(vec):
    NS = vec.shape[1]

    def body(v_ref, o_ref, gbuf, mine, send, recv):
        x, y, c, me = _my_place()
        mine[...] = jnp.sum(v_ref[...], axis=0, keepdims=True)
        gbuf[me] = mine[...]
        copies = []
        for kk in range(1, N_DEV):
            peer, _ = _peer(x, y, c, kk)
            cp = pltpu.make_async_remote_copy(src_ref=mine, dst_ref=gbuf.at[me], send_sem=send.at[kk - 1],
                                              recv_sem=recv.at[kk - 1], device_id=peer, device_id_type=MESH)
            cp.start()
            copies.append(cp)
        for cp in copies:
            cp.wait()
        tot = gbuf[0]
        for d in range(1, N_DEV):
            tot = tot + gbuf[d]
        o_ref[...] = tot

    return pl.pallas_call(
        body,
        name="small_allreduce",
        in_specs=[pl.BlockSpec(memory_space=pltpu.VMEM)],
        out_specs=pl.BlockSpec(memory_space=pltpu.VMEM),
        out_shape=jax.ShapeDtypeStruct((1, NS), F32),
        scratch_shapes=[pltpu.VMEM((N_DEV, 1, NS), F32), pltpu.VMEM((1, NS), F32), pltpu.SemaphoreType.DMA((N_DEV - 1,)),
                        pltpu.SemaphoreType.DMA((N_DEV - 1,))],
        compiler_params=pltpu.CompilerParams(has_side_effects=True, vmem_limit_bytes=V7X_VMEM_LIMIT),
    )(vec)


def _adamw_math(g, w, m, v):
    nm = ADAM_B1 * m + (1.0 - ADAM_B1) * g
    nv = ADAM_B2 * v + (1.0 - ADAM_B2) * (g * g)
    mh = nm / (1.0 - ADAM_B1 ** ADAM_STEP)
    vh = nv / (1.0 - ADAM_B2 ** ADAM_STEP)
    delta = -ADAM_LR * (mh / (jnp.sqrt(vh) + ADAM_EPS) + ADAM_WD * w)
    return delta, nm, nv


def _adam_big(name, recv, w, m, v):
    _, R, C = w.shape
    tr = _rtile(R, max(SUB, (ADAM_BLOCK_ELEMS // C) // SUB * SUB))

    def body(r_ref, w_ref, m_ref, v_ref, g_ref, d_ref, nm_ref, nv_ref):
        g = r_ref[0].astype(F32)
        for d in range(1, N_DEV):
            g = g + r_ref[d].astype(F32)
        delta, nm, nv = _adamw_math(g, w_ref[0], m_ref[0], v_ref[0])
        g_ref[0] = g
        d_ref[0] = delta
        nm_ref[0] = nm
        nv_ref[0] = nv

    blk = pl.BlockSpec((1, tr, C), lambda i: (0, i, 0))
    return pl.pallas_call(
        body,
        name=name,
        grid=(R // tr,),
        in_specs=[pl.BlockSpec((N_DEV, tr, C), lambda i: (0, i, 0)), blk, blk, blk],
        out_specs=[blk, blk, blk, blk],
        out_shape=[jax.ShapeDtypeStruct((1, R, C), F32)] * 4,
        compiler_params=_cparams(1),
    )(recv, w, m, v)


def _to_bf16(name, w):
    _, R, C = w.shape
    tr = _rtile(R, max(SUB, (ADAM_BLOCK_ELEMS // C) // SUB * SUB))

    def body(w_ref, o_ref):
        o_ref[...] = w_ref[0].astype(CDT)

    return pl.pallas_call(
        body,
        name=name,
        grid=(R // tr,),
        in_specs=[pl.BlockSpec((1, tr, C), lambda i: (0, i, 0))],
        out_specs=pl.BlockSpec((tr, C), lambda i: (i, 0)),
        out_shape=jax.ShapeDtypeStruct((R, C), CDT),
        compiler_params=_cparams(1),
    )(w)


def _adam_small(g, w, m, v):
    def body(g_ref, w_ref, m_ref, v_ref, d_ref, nm_ref, nv_ref):
        delta, nm, nv = _adamw_math(g_ref[...], w_ref[...], m_ref[...], v_ref[...])
        d_ref[...] = delta
        nm_ref[...] = nm
        nv_ref[...] = nv

    return pl.pallas_call(body, name="adam_small", out_shape=[jax.ShapeDtypeStruct(g.shape, F32)] * 3)(g, w, m, v)


def _rot(w, axis=-1):
    x1, x2 = jnp.split(w, 2, axis=axis)
    return jnp.concatenate([-x2, x1], axis=axis)


def _unrot(dw, axis=-1):
    d1, d2 = jnp.split(dw, 2, axis=axis)
    return jnp.concatenate([d2, -d1], axis=axis)


def _pad_cols(w, width):
    return jnp.pad(w, [(0, 0)] * (w.ndim - 1) + [(0, width - w.shape[-1])])


def kernel(x, mem, positions, ln_emb_g, ln_emb_b, hgrn_lb_logits, w_in, hgrn_norm_g, mla_g_cq, mla_g_ckv, mla_w_uq, mla_w_ukv, mem_w_kv, w_branch, w_o, ln1_g, ln1_b, w_ffn_gate, w_ffn_up, w_ffn_down, ln2_g, ln2_b, loss_target, m_ln_emb_g, m_ln_emb_b, m_hgrn_lb_logits, m_w_in, m_hgrn_norm_g, m_mla_g_cq, m_mla_g_ckv, m_mla_w_uq, m_mla_w_ukv, m_mem_w_kv, m_w_branch, m_w_o, m_ln1_g, m_ln1_b, m_w_ffn_gate, m_w_ffn_up, m_w_ffn_down, m_ln2_g, m_ln2_b, v_ln_emb_g, v_ln_emb_b, v_hgrn_lb_logits, v_w_in, v_hgrn_norm_g, v_mla_g_cq, v_mla_g_ckv, v_mla_w_uq, v_mla_w_ukv, v_mem_w_kv, v_w_branch, v_w_o, v_ln1_g, v_ln1_b, v_w_ffn_gate, v_w_ffn_up, v_w_ffn_down, v_ln2_g, v_ln2_b):
    x2, tgt = x[0], loss_target[0]
    S, D = x2.shape
    Mm = mem.shape[1]
    F = w_ffn_gate.shape[2] * N_DEV
    GW = 3 * D
    PW = OFF_GATE + GW + KR_PAD
    KR = OFF_GATE + GW
    NIN = w_in.shape[2] * N_DEV
    assert NIN == OFF_GATE + MLA_ROPE + GW
    _, _, _, me = _my_place()
    row = lambda a: a.reshape(1, -1)

    br3 = lambda a: a.reshape(1, 3 * BR_W, -1)
    tp = lambda a: jnp.swapaxes(a, 1, 2)
    big_w = [tp(w_in), mla_w_uq, mla_w_ukv, mem_w_kv, br3(w_branch), w_o, tp(w_ffn_gate), tp(w_ffn_up), w_ffn_down]
    big_m = [tp(m_w_in), m_mla_w_uq, m_mla_w_ukv, m_mem_w_kv, br3(m_w_branch), m_w_o, tp(m_w_ffn_gate), tp(m_w_ffn_up),
             m_w_ffn_down]
    big_v = [tp(v_w_in), v_mla_w_uq, v_mla_w_ukv, v_mem_w_kv, br3(v_w_branch), v_w_o, tp(v_w_ffn_gate), tp(v_w_ffn_up),
             v_w_ffn_down]
    transposed = (0, 6, 7)
    wnames = ["w_in", "w_uq", "w_ukv", "mem_w_kv", "w_branch", "w_o", "w_gate", "w_up", "w_down"]
    big_wb = [_to_bf16("bf16_" + nme, w) for nme, w in zip(wnames, big_w)]
    g_in, g_lb = _all_gather_two_level("weights_all_gather", [big_wb[0], hgrn_lb_logits.reshape(4, -1)])
    win_t = g_in.reshape(NIN, D)
    kr_w = win_t[OFF_QM:OFF_QM + MLA_ROPE]
    zeros64 = jnp.zeros_like(kr_w)
    win_pt = jnp.concatenate([win_t[:OFF_FB], win_t[OFF_FF:OFF_G], win_t[OFF_FB:OFF_FF], win_t[OFF_G:OFF_QM],
                              win_t[OFF_QM + MLA_ROPE:], kr_w, zeros64, _rot(kr_w, 0), zeros64,
                              jnp.zeros((KR_PAD - 2 * LANE, D), CDT)], axis=0)
    lbl4 = jnp.transpose(g_lb, (1, 0, 2)).reshape(4, -1)

    half = MLA_ROPE // 2
    inv_freq = jnp.power(ROPE_THETA, -jnp.arange(half, dtype=F32) / half)
    ang = positions[0].astype(F32)[:, None] * inv_freq
    cf = _pad_cols(jnp.tile(jnp.cos(ang), (1, 2)), LANE)
    sf = _pad_cols(jnp.tile(jnp.sin(ang), (1, 2)), LANE)

    tm512 = _rtile(S, 512)
    ident = lambda accs, tiles, rows: ([accs[0]], [])

    def epi_ln0(accs, tiles, rows):
        h = _ln_stats(tiles[0])[0] * rows[0] + rows[1]
        return [h, h], []

    h0, h0b = _fused_mm("ln_emb_fwd", "nn", [], S, D, 1, tm512, D, 1, [(D, F32, 0, None), (D, CDT, 0, None)], epi_ln0,
                        tiles=[(x2, 0)], rows=[(row(ln_emb_g), 0), (row(ln_emb_b), 0)])
    proj, g_uq, g_ukv, g_mkv, g_wb, g_wo = _fused_mm(
        "proj", "nt", [[(h0b, 0, win_pt, 0)]], S, PW, D, _rtile(S, 1024), _tile(PW, 1536), D, [(PW, F32, 0, None)], ident,
        xchg=_Xchg(big_wb[1:6], False), msplit=2 if S % 2048 == 0 else 1)
    wuq_p =jnp.concatenate([g_uq[..., :MLA_NOPE], _pad_cols(g_uq[..., MLA_NOPE:], LANE),
                             _pad_cols(_rot(g_uq[..., MLA_NOPE:]), LANE)], axis=-1)
    wukv = g_ukv
    wmkv = g_mkv.reshape(-1, g_mkv.shape[-1])
    wb = jnp.transpose(g_wb.reshape(N_DEV, 3, BR_W, -1), (1, 2, 0, 3)).reshape(3, BR_W, D)
    wo = g_wo.reshape(-1, D)
    o_f, st_f = _gla_fwd(proj, lbl4, OFF_FF, False, "gla_fwd_f")
    o_b, st_b = _gla_fwd(proj, lbl4, OFF_FB, True, "gla_fwd_b")
    y_hg = _hgrn_post_fwd(o_f, o_b, proj, hgrn_norm_g)
    q_cat, k_cat, v_mla, vt_mla, cqn, ckvn = _mla_up(proj, cf, sf, mla_g_cq, mla_g_ckv, wuq_p, wukv)
    y_mla, ot, lse, g_wg, g_wu, g_wd = _mla_attn_fwd(q_cat, k_cat, vt_mla, _Xchg(big_wb[6:9], False))
    wg_t, wu_t = g_wg.reshape(F, D), g_wu.reshape(F, D)
    wd = g_wd.reshape(-1, D)
    memb = mem[0].astype(CDT)
    (memkv,) = _fused_mm("mem_kv", "nn", [[(memb, 0, wmkv, 0)]], Mm, 2 * MEM_W, D, Mm, _tile(2 * MEM_W, 512), D,
                         [(2 * MEM_W, CDT, 0, None)], ident)
    y_mem = _mem_attn_fwd(proj, memkv)
    ys = [y_hg, y_mla, y_mem]
    tnD = _tile(D, 1024, OFF_GATE)

    def epi_branch(accs, tiles, rows):
        return [_sigmoid(tiles[0]) * accs[0] + _sigmoid(tiles[1]) * accs[1] + _sigmoid(tiles[2]) * accs[2]], []

    (merged,) = _fused_mm("branch_fwd", "nn", [[(ys[b], 0, wb[b], 0)] for b in range(3)], S, D, BR_W, tm512, tnD, BR_W,
                          [(D, CDT, 0, None)], epi_branch, tiles=[(proj, OFF_GATE + b * D) for b in range(3)])

    def epi_ln1(accs, tiles, rows):
        r1v = ALPHA * tiles[0] + accs[0]
        return [r1v, _ln_stats(r1v)[0] * rows[0] + rows[1]], []

    r1, h1b = _fused_mm("wo_ln1", "nn", [[(merged, 0, wo, 0)]], S, D, D, tm512, D, D,
                        [(D, F32, 0, None), (D, CDT, 0, None)], epi_ln1, tiles=[(h0, 0)], rows=[(ln1_g, 0), (ln1_b, 0)])
    tnF = _tile(F, 512)

    def epi_up(accs, tiles, rows):
        gp, up = accs
        return [gp, up, gp * _sigmoid(gp) * up], []

    tm1k, ms1k = _rtile(S, 1024), (2 if S % 2048 == 0 else 1)
    gpb, upb, act = _fused_mm("ffn_up", "nt", [[(h1b, 0, wg_t, 0)], [(h1b, 0, wu_t, 0)]], S, F, D, tm1k, tnF, D,
                              [(F, CDT, 0, None)] * 3, epi_up, msplit=ms1k)

    def epi_down(accs, tiles, rows):
        g1, b1, g2, b2 = rows
        h1 = _ln_stats(tiles[0])[0] * g1 + b1
        xh2, rstd2 = _ln_stats(ALPHA * h1 + accs[0])
        diff = xh2 * g2 + b2 - tiles[1]
        dh2 = diff * (1.0 / D)
        dr2v = _ln_bwd(dh2, xh2, rstd2, g2)
        return [dr2v, dr2v], [dh2 * xh2, dh2, diff * diff * (0.5 / D)]

    acc_first = lambda epi: (lambda accs, tiles, rows: epi([tiles[0]], tiles[1:], rows))
    tm256 = _rtile(S, 256)
    (ff,) = _fused_mm("ffn_down", "nn", [[(act, 0, wd, 0)]], S, D, F, tm1k, _tile(D, 512), F, [(D, F32, 0, None)], ident,
                      msplit=ms1k)
    dr2, dr2b, dg2, db2, lossp = _fused_mm(
        "ffn_ln2_loss", "nn", [], S, D, 1, tm256, D, 1, [(D, F32, 0, None), (D, CDT, 0, None)], acc_first(epi_down),
        tiles=[(ff, 0), (r1, 0), (tgt, 0)], rows=[(ln1_g, 0), (ln1_b, 0), (ln2_g, 0), (ln2_b, 0)], n_racc=3)

    def epi_dact(accs, tiles, rows):
        da, gp, up = accs[0], tiles[0].astype(F32), tiles[1].astype(F32)
        s = _sigmoid(gp)
        return [da * up * (s * (1.0 + gp * (1.0 - s))), da * (gp * s)], []

    dgp, dup = _fused_mm("ffn_dact", "nt", [[(dr2b, 0, wd, 0)]], S, F, D, tm1k, tnF, D, [(F, CDT, 0, None)] * 2,
                         epi_dact, tiles=[(gpb, 0), (upb, 0)], msplit=2 * ms1k)
    tkS = _rtile(S, 2048)
    (d_wd,) = _fused_mm("dw_down", "tn", [[(act, 0, dr2b, 0)]], F, D, S, tnF, D, tkS, [(D, CDT, 0, None)], ident)
    d_wg_t, d_wu_t = _fused_mm("dw_gate_up", "tn", [[(dgp, 0, h1b, 0)], [(dup, 0, h1b, 0)]], F, D, S, tnF, _tile(D, 1024),
                               tkS, [(D, CDT, 0, None)] * 2, lambda accs, tiles, rows: (accs, []))

    def epi_dh1(accs, tiles, rows):
        dh1 = accs[0] + ALPHA * tiles[0]
        xh1, rstd1 = _ln_stats(tiles[1])
        dr1v = _ln_bwd(dh1, xh1, rstd1, rows[0])
        return [dr1v, dr1v], [dh1 * xh1, dh1]

    rows8 = lambda dw: dw.reshape(N_DEV, -1, dw.shape[-1])
    (dh1_acc,) = _fused_mm("dh1", "nn", [[(dgp, 0, wg_t, 0), (dup, 0, wu_t, 0)]], S, D, F, tm512, _tile(D, 512), F,
                           [(D, F32, 0, None)], ident)
    dr1, dr1b, dg1, db1 = _fused_mm(
        "dh1_ln1", "nn", [], S, D, 1, tm256, D, 1, [(D, F32, 0, None), (D, CDT, 0, None)], acc_first(epi_dh1),
        tiles=[(dh1_acc, 0), (dr2, 0), (r1, 0)], rows=[(ln1_g, 0)], n_racc=2)
    (dmerged,) = _fused_mm("dmerged", "nt", [[(dr1b, 0, wo, 0)]], S, D, D, tm512, D, D, [(D, CDT, 0, None)], ident)
    (d_wo,) = _fused_mm("dw_o", "tn", [[(merged, 0, dr1b, 0)]], D, D, S, _tile(D, 512), D, tkS, [(D, CDT, 0, None)], ident)

    def epi_dbranch(accs, tiles, rows):
        dm, s = tiles[0].astype(F32), _sigmoid(tiles[1])
        return [dm * s, dm * accs[0] * s * (1.0 - s)], []

    dproj = None
    d_wbs, dys = [], []
    for b in range(3):
        du, dproj = _fused_mm(f"branch_bwd{b}", "nn", [[(ys[b], 0, wb[b], 0)]], S, D, BR_W, tm512, tnD, BR_W,
                              [(D, CDT, 0, None), (PW, CDT, OFF_GATE + b * D, dproj)], epi_dbranch,
                              tiles=[(dmerged, 0), (proj, OFF_GATE + b * D)])
        (dwb,) = _fused_mm(f"dw_branch{b}", "tn", [[(ys[b], 0, du, 0)]], BR_W, D, S, _tile(BR_W, 512), D, tkS,
                           [(D, CDT, 0, None)], ident)
        (dyb,) = _fused_mm(f"dy_branch{b}", "nt", [[(du, 0, wb[b], 0)]], S, BR_W, D, tm512, BR_W, D,
                           [(BR_W, F32 if b == 0 else CDT, 0, None)], ident)
        d_wbs.append(dwb)
        dys.append(dyb)
    dy_hg, dy_mla, dy_mem = dys

    dproj, dk_mem, dv_mem = _mem_attn_bwd(dy_mem, proj, memkv, dproj)
    dkv_mem = jnp.concatenate([dk_mem, dv_mem], axis=1).astype(CDT)
    (d_wmkv,) = _fused_mm("dw_memkv", "tn", [[(memb, 0, dkv_mem, 0)]], D, 2 * MEM_W, Mm, _tile(D, 512), 2 * MEM_W, Mm,
                          [(2 * MEM_W, CDT, 0, None)], ident)

    delta = _mla_delta(dy_mla, ot)
    lse_b = lse.reshape(delta.shape)
    dk_cat, dv_h, dq_cat, r_wg, r_wu, r_wd = _mla_attn_bwd(
        q_cat, k_cat, v_mla, dy_mla, lse_b, delta, _Xchg([rows8(d_wg_t), rows8(d_wu_t), rows8(d_wd)], True))
    dproj, dqp, dkvp, dgq, dgkv = _mla_up_bwd(dq_cat, dk_cat, dv_h, proj, cf, sf, mla_g_cq, mla_g_ckv, wuq_p, wukv, dproj)
    heads_major = lambda dw: jnp.transpose(dw.reshape(MLA_RANK, MLA_HEADS, -1), (1, 0, 2))
    (d_wuq_all,) = _fused_mm("dw_uq", "tn", [[(cqn, 0, dqp, 0)]], MLA_RANK, dqp.shape[1], S, MLA_RANK,
                             _tile(dqp.shape[1], 1536), tkS, [(dqp.shape[1], F32, 0, None)], ident)
    (d_wukv_all,) = _fused_mm("dw_ukv", "tn", [[(ckvn, 0, dkvp, 0)]], MLA_RANK, dkvp.shape[1], S, MLA_RANK,
                              _tile(dkvp.shape[1], 1024), tkS, [(dkvp.shape[1], CDT, 0, None)], ident)
    d_wuq_p, d_wukv = heads_major(d_wuq_all), heads_major(d_wukv_all)
    d_wuq = jnp.concatenate([d_wuq_p[..., :MLA_NOPE],
                             d_wuq_p[..., LANE:LANE + MLA_ROPE] + _unrot(d_wuq_p[..., 2 * LANE:2 * LANE + MLA_ROPE])],
                            axis=-1).astype(CDT)

    do_hg, dproj, dng = _hgrn_post_bwd(dy_hg, o_f, o_b, proj, hgrn_norm_g, dproj)
    dproj, dq1, di1, dl_f = _gla_bwd(proj, lbl4, OFF_FF, False, do_hg, st_f, dproj, None, "gla_bwd_f")
    dproj, _, _, dl_b = _gla_bwd(proj, lbl4, OFF_FB, True, do_hg, st_b, dproj, (dq1, di1), "gla_bwd_b")

    def epi_dh0(accs, tiles, rows):
        dh0 = accs[0] + ALPHA * tiles[0]
        xh, rstd = _ln_stats(tiles[1])
        return [_ln_bwd(dh0, xh, rstd, rows[0])], [dh0 * xh, dh0]

    d_wb = jnp.transpose(jnp.stack(d_wbs).reshape(3, BR_W, N_DEV, -1), (2, 0, 1, 3)).reshape(N_DEV, 3 * BR_W, -1)
    d_win_pt, r_uq, r_ukv, r_mkv, r_wb, r_wo = _fused_mm(
        "dw_in", "tn", [[(dproj, 0, h0b, 0)]], PW, D, S, _tile(PW, 1536), _tile(D, 1024), tkS, [(D, CDT, 0, None)], ident,
        xchg=_Xchg([d_wuq, d_wukv, rows8(d_wmkv), d_wb, rows8(d_wo)], True))
    d_kr = (d_win_pt[KR:KR + MLA_ROPE].astype(F32) + _unrot(d_win_pt[KR + LANE:KR + LANE + MLA_ROPE].astype(F32), 0)).astype(CDT)
    d_win_t = jnp.concatenate([d_win_pt[:OFF_FB], d_win_pt[OFF_FF:OFF_G], d_win_pt[OFF_FB:OFF_FF],
                               d_win_pt[OFF_G:OFF_QM], d_kr, d_win_pt[OFF_QM:KR]], axis=0)
    grad_x, dge, dbe, r_in = _fused_mm(
        "dh0_ln_emb", "nn", [[(dproj, 0, win_pt, 0)]], S, D, PW, tm512, D, _tile(PW, 1536), [(D, F32, 0, None)], epi_dh0,
        tiles=[(dr1, 0), (x2, 0)], rows=[(row(ln_emb_g), 0)], n_racc=2, xchg=_Xchg([rows8(d_win_t)], True))

    recv = [r_in, r_uq, r_ukv, r_mkv, r_wb, r_wo, r_wg, r_wu, r_wd]
    names = ["w_in", "w_uq", "w_ukv", "mem_w_kv", "w_branch", "w_o", "w_gate", "w_up", "w_down"]
    big_out = [_adam_big("adam_" + nme, r, w, m_, v_) for nme, r, w, m_, v_ in zip(names, recv, big_w, big_m, big_v)]

    parts = [dge, dbe, dng, dgq, dgkv, dg1, db1, dg2, db2, dl_f, dl_b, lossp]
    widths = [p.shape[1] for p in parts]
    red = _small_allreduce(jnp.concatenate(parts, axis=1))[0]
    offs = [sum(widths[:i]) for i in range(len(widths))]
    rs = [red[o:o + w_] for o, w_ in zip(offs, widths)]
    g_le_g, g_le_b, g_ng, g_gq, g_gkv, g_l1g, g_l1b, g_l2g, g_l2b, g_dlf, g_dlb, g_loss = rs
    loss = jnp.sum(g_loss)
    g_ng = g_ng.reshape(HG_HEADS, HG_DK).sum(axis=0)
    dl0 = jnp.stack([g_dlf, g_dlb])
    g_lb_full = jnp.stack([dl0, -dl0], axis=1)
    lbw = hgrn_lb_logits.shape[2]
    g_lb = lax.dynamic_slice_in_dim(g_lb_full, me * lbw, lbw, axis=2)

    small_g = [g_le_g, g_le_b, g_lb, g_ng.reshape(1, -1), g_gq.reshape(1, -1), g_gkv.reshape(1, -1), g_l1g.reshape(1, -1),
               g_l1b.reshape(1, -1), g_l2g.reshape(1, -1), g_l2b.reshape(1, -1)]
    small_w = [ln_emb_g, ln_emb_b, hgrn_lb_logits, hgrn_norm_g, mla_g_cq, mla_g_ckv, ln1_g, ln1_b, ln2_g, ln2_b]
    small_m = [m_ln_emb_g, m_ln_emb_b, m_hgrn_lb_logits, m_hgrn_norm_g, m_mla_g_cq, m_mla_g_ckv, m_ln1_g, m_ln1_b, m_ln2_g, m_ln2_b]
    small_v = [v_ln_emb_g, v_ln_emb_b, v_hgrn_lb_logits, v_hgrn_norm_g, v_mla_g_cq, v_mla_g_ckv, v_ln1_g, v_ln1_b, v_ln2_g, v_ln2_b]
    small_g = [g.reshape(w.shape) for g, w in zip(small_g, small_w)]
    pack = lambda lst: jnp.concatenate([a.reshape(-1) for a in lst]).reshape(1, -1)
    s_delta, s_nm, s_nv = _adam_small(pack(small_g), pack(small_w), pack(small_m), pack(small_v))
    sizes = [w.size for w in small_w]
    soffs = [sum(sizes[:i]) for i in range(len(sizes))]
    unpack = lambda p: [p[0, o:o + n].reshape(w.shape) for o, n, w in zip(soffs, sizes, small_w)]
    s_delta, s_nm, s_nv = unpack(s_delta), unpack(s_nm), unpack(s_nv)

    def ordered(small, big):
        sm = list(small)
        big = [tp(b) if n in transposed else b for n, b in enumerate(big)]
        bg = [b.reshape(w.shape) for b, w in zip(big, [w_in, mla_w_uq, mla_w_ukv, mem_w_kv, w_branch, w_o, w_ffn_gate, w_ffn_up, w_ffn_down])]
        return [sm[0], sm[1], sm[2], bg[0], sm[3], sm[4], sm[5], bg[1], bg[2], bg[3], bg[4], bg[5], sm[6], sm[7], bg[6], bg[7], bg[8], sm[8], sm[9]]

    grads = ordered(small_g, [o[0] for o in big_out])
    deltas = ordered(s_delta, [o[1] for o in big_out])
    new_m = ordered(s_nm, [o[2] for o in big_out])
    new_v = ordered(s_nv, [o[3] for o in big_out])
    return (loss, grad_x[None], *grads, *deltas, *new_m, *new_v)
```

```python
import functools

import jax
import jax.numpy as jnp
from jax import lax
from jax.experimental import pallas as pl
from jax.experimental.pallas import tpu as pltpu

F32 = jnp.float32
CDT = jnp.bfloat16
MESH = pl.DeviceIdType.MESH
N_DEV = 8
V7X_VMEM_LIMIT = 60 * 1024 * 1024
LANE = 128
SUB = 8

HG_HEADS, HG_DK, HG_CHUNK = 8, 128, 64
HG_HPS = 8
HG_BWD_GROUP = 8
HG_W = HG_HEADS * HG_DK
MLA_HEADS, MLA_RANK, MLA_NOPE, MLA_ROPE, MLA_V = 8, 512, 128, 64, 128
MLA_QK = MLA_NOPE + MLA_ROPE
MLA_SCALE = MLA_QK ** -0.5
MLA_QSCALE = MLA_SCALE * 1.4426950408889634
VT_ROWS = LANE + 16
MLA_TQ = 1024
MLA_BWD_SLABS = 1
MLA_TQ_FWD = 2048
MLA_FWD_SLABS = 8
MLA_W = MLA_HEADS * MLA_V
MEM_HEADS, MEM_HD = 4, 256
MEM_W = MEM_HEADS * MEM_HD
BR_W = 1024
ROPE_THETA = 10000.0
ALPHA = 2.0 ** 0.25
LN_EPS = 1e-5
RMS_EPS = 1e-6
ADAM_LR, ADAM_B1, ADAM_B2, ADAM_EPS, ADAM_WD, ADAM_STEP = 0.001, 0.9, 0.999, 1e-08, 0.01, 10
ADAM_BLOCK_ELEMS = 256 * 1024

OFF_Q, OFF_I, OFF_FB, OFF_FF, OFF_G = 0, 1024, 2048, 3072, 4096
OFF_CQ, OFF_CKV, OFF_QM, OFF_GATE = 5120, 5632, 6144, 7168
KR_PAD = 512


def _cparams(n_grid, side_effects=False):
    return pltpu.CompilerParams(dimension_semantics=("arbitrary",) * n_grid, vmem_limit_bytes=V7X_VMEM_LIMIT,
                                has_side_effects=side_effects)


def _tile(n, pref, *offsets):
    if n <= pref and all(o % n == 0 for o in offsets):
        return n
    t = (min(pref, n) // LANE) * LANE
    while t >= LANE:
        if n % t == 0 and all(o % t == 0 for o in offsets):
            return t
        t -= LANE
    raise ValueError(f"no tile for {n} {pref} {offsets}")


def _rtile(n, pref):
    if n <= pref:
        return n
    t = (pref // SUB) * SUB
    while t >= SUB:
        if n % t == 0:
            return t
        t -= SUB
    raise ValueError(f"no row tile for {n} {pref}")


def _dot(a, b, dims):
    return lax.dot_general(a.astype(CDT), b.astype(CDT), (dims, ((), ())), preferred_element_type=F32)


def _nn(a, b):
    return _dot(a, b, ((1,), (0,)))


def _nt(a, b):
    return _dot(a, b, ((1,), (1,)))


def _tn(a, b):
    return _dot(a, b, ((0,), (0,)))


_DOTS = {"nn": _nn, "nt": _nt, "tn": _tn}


def _sigmoid(x):
    return 1.0 / (1.0 + jnp.exp(-x))


def _rowsum8(v):
    r, w = v.shape
    return v.reshape(r // SUB, SUB, w).sum(axis=0)


def _my_place():
    x, y, c = lax.axis_index("x"), lax.axis_index("y"), lax.axis_index("c")
    return x, y, c, 4 * x + 2 * y + c


def _peer(x, y, c, kk):
    px = 1 - x if kk & 4 else x
    py = 1 - y if kk & 2 else y
    pc = 1 - c if kk & 1 else c
    return (px, py, pc), 4 * px + 2 * py + pc


class _Xchg:
    def __init__(self, arrs, scatter):
        self.arrs, self.scatter, self.n = list(arrs), scatter, len(arrs)
        hbm = pl.BlockSpec(memory_space=pl.ANY)
        self.specs = [hbm] * self.n
        self.out_shape = [jax.ShapeDtypeStruct(((N_DEV,) + a.shape[1:]) if scatter else ((N_DEV,) + a.shape), a.dtype)
                          for a in self.arrs]
        ncp = self.n * (N_DEV - 1)
        self.scratch = [pltpu.SemaphoreType.DMA((ncp,)), pltpu.SemaphoreType.DMA((ncp,)), pltpu.SemaphoreType.DMA((self.n,))]

    def _copies(self, ins, outs, send, recv, loc):
        x, y, c, me = _my_place()
        copies = []
        for w in range(self.n):
            copies.append(pltpu.make_async_copy(ins[w].at[me] if self.scatter else ins[w], outs[w].at[me], loc.at[w]))
            for kk in range(1, N_DEV):
                peer, pid = _peer(x, y, c, kk)
                s = w * (N_DEV - 1) + kk - 1
                copies.append(pltpu.make_async_remote_copy(
                    src_ref=ins[w].at[pid] if self.scatter else ins[w], dst_ref=outs[w].at[me],
                    send_sem=send.at[s], recv_sem=recv.at[s], device_id=peer, device_id_type=MESH))
        return copies

    def start(self, ins, outs, sems):
        for cp in self._copies(ins, outs, *sems):
            cp.start()

    def wait(self, ins, outs, sems):
        for cp in self._copies(ins, outs, *sems):
            cp.wait()


def _all_gather_two_level(name, arrs):
    n = len(arrs)
    NC = N_DEV - 1

    def body(*refs):
        ins, outs = refs[:n], refs[n:2 * n]
        send, recv, loc = refs[2 * n:]
        x, y, c, me = _my_place()
        sibling = (x, y, 1 - c)
        chips = [(1 - x, y), (x, 1 - y), (1 - x, 1 - y)]
        slot = lambda px, py, pc: 4 * px + 2 * py + pc

        def copy(w, k, block, to, src=None):
            dst = outs[w].at[slot(*block)]
            return pltpu.make_async_remote_copy(src_ref=dst if src is None else src, dst_ref=dst,
                                                send_sem=send.at[w * NC + k], recv_sem=recv.at[w * NC + k],
                                                device_id=to, device_id_type=MESH)

        mine = [pltpu.make_async_copy(ins[w], outs[w].at[me], loc.at[w]) for w in range(n)]
        for cp in mine:
            cp.start()
        first = []
        for w in range(n):
            first.append(copy(w, 0, (x, y, c), sibling, src=ins[w]))
            first += [copy(w, 1 + j, (x, y, c), (*chip, c), src=ins[w]) for j, chip in enumerate(chips)]
        for cp in first:
            cp.start()
        passed = []
        for j, chip in enumerate(chips):
            for w in range(n):
                copy(w, 1 + j, (*chip, c), (x, y, c)).wait_recv()
                fwd = copy(w, 4 + j, (*chip, c), sibling)
                fwd.start()
                passed.append(fwd)
        for w in range(n):
            copy(w, 0, sibling, (x, y, c)).wait_recv()
            for j, chip in enumerate(chips):
                copy(w, 4 + j, (*chip, 1 - c), (x, y, c)).wait_recv()
        for cp in first + passed:
            cp.wait_send()
        for cp in mine:
            cp.wait()

    hbm = pl.BlockSpec(memory_space=pl.ANY)
    return pl.pallas_call(
        body,
        name=name,
        in_specs=[hbm] * n,
        out_specs=[hbm] * n,
        out_shape=[jax.ShapeDtypeStruct((N_DEV,) + a.shape, a.dtype) for a in arrs],
        scratch_shapes=[pltpu.SemaphoreType.DMA((n * NC,)), pltpu.SemaphoreType.DMA((n * NC,)), pltpu.SemaphoreType.DMA((n,))],
        compiler_params=pltpu.CompilerParams(has_side_effects=True),
    )(*arrs)


def _fused_mm(name, mode, groups, M, N, K, tm, tn, tk, outs, epi, tiles=(), rows=(), n_racc=0, xchg=None, msplit=1):
    ni, nj, nk = M // tm, N // tn, K // tk
    assert M % tm == 0 and N % tn == 0 and K % tk == 0, (name, M, N, K, tm, tn, tk)
    assert n_racc == 0 or nj == 1
    assert msplit == 1 or (nk == 1 and n_racc == 0 and tm % (16 * msplit) == 0)
    dot = _DOTS[mode] if groups else None
    ins, in_specs = [], []
    for g in groups:
        for a, a_off, b, b_off in g:
            if mode == "tn":
                assert a_off % tm == 0
                in_specs.append(pl.BlockSpec((tk, tm), lambda i, j, k, o=a_off // tm: (k, i + o)))
            else:
                assert a_off % tk == 0
                in_specs.append(pl.BlockSpec((tm, tk), lambda i, j, k, o=a_off // tk: (i, k + o)))
            ins.append(a)
            if mode == "nt":
                assert b_off % tk == 0
                in_specs.append(pl.BlockSpec((tn, tk), lambda i, j, k, o=b_off // tk: (j, k + o)))
            else:
                assert b_off % tn == 0
                in_specs.append(pl.BlockSpec((tk, tn), lambda i, j, k, o=b_off // tn: (k, j + o)))
            ins.append(b)
    for arr, off in tiles:
        assert off % tn == 0
        ins.append(arr)
        in_specs.append(pl.BlockSpec((tm, tn), lambda i, j, k, o=off // tn: (i, j + o)))
    for arr, off in rows:
        assert off % tn == 0
        ins.append(arr)
        in_specs.append(pl.BlockSpec((1, tn), lambda i, j, k, o=off // tn: (0, j + o)))
    aliases = {}
    out_shape, out_specs = [], []
    for oi, (width, dtype, off, alias) in enumerate(outs):
        assert off % tn == 0
        if alias is not None:
            aliases[len(ins)] = oi
            ins.append(alias)
            in_specs.append(pl.BlockSpec(memory_space=pl.ANY))
        out_shape.append(jax.ShapeDtypeStruct((M, width), dtype))
        out_specs.append(pl.BlockSpec((tm, tn), lambda i, j, k, o=off // tn: (i, j + o)))
    for _ in range(n_racc):
        out_shape.append(jax.ShapeDtypeStruct((SUB, N), F32))
        out_specs.append(pl.BlockSpec((SUB, tn), lambda i, j, k: (0, 0)))
    n_alias = len(aliases)
    n_pairs = [len(g) for g in groups]
    use_scratch = nk > 1
    scratch = [pltpu.VMEM((tm, tn), F32) for _ in groups] if use_scratch else []
    nx = 0
    if xchg is not None:
        nx = xchg.n
        ins += xchg.arrs
        in_specs += xchg.specs
        out_shape += xchg.out_shape
        out_specs += xchg.specs
        scratch += xchg.scratch

    def body(*refs):
        it = iter(refs)
        pair_refs = [[(next(it), next(it)) for _ in range(n)] for n in n_pairs]
        tile_refs = [next(it) for _ in tiles]
        row_refs = [next(it) for _ in rows]
        for _ in range(n_alias):
            next(it)
        x_in = [next(it) for _ in range(nx)]
        out_refs = [next(it) for _ in outs]
        racc_refs = [next(it) for _ in range(n_racc)]
        x_out = [next(it) for _ in range(nx)]
        acc_refs = [next(it) for _ in groups] if use_scratch else []
        x_sems = list(it)
        i, j, k = pl.program_id(0), pl.program_id(1), pl.program_id(2)
        if nx:
            @pl.when((i == 0) & (j == 0) & (k == 0))
            def _():
                xchg.start(x_in, x_out, x_sems)

        def products():
            res = []
            for prs in pair_refs:
                s = None
                for a_ref, b_ref in prs:
                    d = dot(a_ref[...], b_ref[...])
                    s = d if s is None else s + d
                res.append(s)
            return res

        def finish(accs):
            out_v, racc_v = epi(accs, [t[...] for t in tile_refs], [r[...] for r in row_refs])
            for o_ref, v in zip(out_refs, out_v):
                o_ref[...] = v.astype(o_ref.dtype)
            for r_ref, v in zip(racc_refs, racc_v):
                part = _rowsum8(v)

                @pl.when(i == 0)
                def _():
                    r_ref[...] = part

                @pl.when(i > 0)
                def _():
                    r_ref[...] += part

        if not use_scratch and msplit > 1:
            ts = tm // msplit
            for s in range(msplit):
                rs = pl.ds(s * ts, ts)
                accs = []
                for prs in pair_refs:
                    acc = None
                    for a_ref, b_ref in prs:
                        dd = dot(a_ref[:, rs] if mode == "tn" else a_ref[rs, :], b_ref[...])
                        acc = dd if acc is None else acc + dd
                    accs.append(acc)
                out_v, _ = epi(accs, [t[rs, :] for t in tile_refs], [r[...] for r in row_refs])
                for o_ref, v in zip(out_refs, out_v):
                    o_ref[rs, :] = v.astype(o_ref.dtype)
        elif not use_scratch:
            finish(products())
        else:
            @pl.when(k == 0)
            def _():
                for acc in acc_refs:
                    acc[...] = jnp.zeros_like(acc)

            for acc, p in zip(acc_refs, products()):
                acc[...] += p

            @pl.when(k == nk - 1)
            def _():
                finish([acc[...] for acc in acc_refs])

        if nx:
            @pl.when((i == ni - 1) & (j == nj - 1) & (k == nk - 1))
            def _():
                xchg.wait(x_in, x_out, x_sems)

    res = pl.pallas_call(
        body,
        name=name,
        grid=(ni, nj, nk),
        in_specs=in_specs,
        out_specs=out_specs,
        out_shape=out_shape,
        scratch_shapes=scratch,
        input_output_aliases=aliases,
        compiler_params=_cparams(3, side_effects=nx > 0),
    )(*ins)
    return res


def _ln_stats(r):
    mu = jnp.mean(r, axis=-1, keepdims=True)
    xc = r - mu
    var = jnp.mean(xc * xc, axis=-1, keepdims=True)
    rstd = lax.rsqrt(var + LN_EPS)
    return xc * rstd, rstd


def _ln_bwd(dh, xhat, rstd, g):
    dxh = dh * g
    m1 = jnp.mean(dxh, axis=-1, keepdims=True)
    m2 = jnp.mean(dxh * xhat, axis=-1, keepdims=True)
    return rstd * (dxh - m1 - xhat * m2)


def _split3(x):
    hi = x.astype(CDT)
    r1 = x - hi.astype(F32)
    mid = r1.astype(CDT)
    lo = (r1 - mid.astype(F32)).astype(CDT)
    return hi, mid, lo


def _tri_matmul(tri, x):
    hi, mid, lo = _split3(x)
    return _nn(tri, hi) + _nn(tri, mid) + _nn(tri, lo)


def _dot3(dot, a, b):
    a_hi, b_hi = a.astype(CDT), b.astype(CDT)
    a_lo = (a - a_hi.astype(F32)).astype(CDT)
    b_lo = (b - b_hi.astype(F32)).astype(CDT)
    return dot(a_hi, b_hi) + dot(a_hi, b_lo) + dot(a_lo, b_hi)


def _gla_masks(reverse):
    C = HG_CHUNK
    r = lax.broadcasted_iota(jnp.int32, (C, C), 0)
    c = lax.broadcasted_iota(jnp.int32, (C, C), 1)
    keep = (c >= r) if reverse else (r >= c)
    return keep


def _m(fn, *lists):
    return [fn(*args) for args in zip(*lists)]


def _gla_chunk_fwd(qraw, fraw, lb, keep, reverse):
    C = HG_CHUNK
    end = 0 if reverse else C - 1
    tri = jnp.where(keep, 1.0, 0.0).astype(CDT)
    sq = _m(_sigmoid, qraw)
    q = _m(lambda x, s: x * s, qraw, sq)
    sg = _m(_sigmoid, fraw)
    f = _m(lambda l_, s: l_ + (1.0 - l_) * s, lb, sg)
    k = _m(lambda x: 1.0 - x, f)
    g = _m(jnp.log, f)
    b = _m(lambda x: _tri_matmul(tri, x), g)
    b_end = _m(lambda x: x[end:end + 1, :], b)
    b_mid = _m(lambda x: x[C // 2:C // 2 + 1, :], b)
    eq = _m(lambda x, m_: jnp.exp(x - m_), b, b_mid)
    ek = _m(lambda x, m_: jnp.exp(m_ - x), b, b_mid)
    eb = _m(jnp.exp, b)
    e2 = _m(lambda x, e_: jnp.exp(e_ - x), b, b_end)
    e_end = _m(jnp.exp, b_end)
    qt = _m(lambda x, e_: x * e_, q, eq)
    kt = _m(lambda x, e_: x * e_, k, ek)
    qs = _m(lambda x, e_: (x * e_).astype(CDT), q, eb)
    k2 = _m(lambda x, e_: (x * e_).astype(CDT), k, e2)
    a = _m(lambda x, y: jnp.where(keep, _dot3(_nt, x, y), 0.0).astype(CDT), qt, kt)
    return dict(sq=sq, q=q, sg=sg, f=f, k=k, eq=eq, ek=ek, eb=eb, e2=e2, e_end=e_end, qt=qt, kt=kt, qs=qs, k2=k2, a=a)


def _gla_fwd(proj, lbl4, f_off, reverse, name):
    S = proj.shape[0]
    C = HG_CHUNK
    R = _rtile(S, 512)
    cpb, nblk = R // C, S // R
    d = 1 if reverse else 0
    blk_map = (lambda b: nblk - 1 - b) if reverse else (lambda b: b)

    W = HG_HPS * HG_DK

    def body(q_ref, i_ref, f_ref, lb_ref, o_ref, st_ref, s_scr):
        @pl.when(pl.program_id(1) == 0)
        def _():
            s_scr[...] = jnp.zeros_like(s_scr)

        l = lb_ref[...]
        lbs = _sigmoid(l[2 * d:2 * d + 1, :] - l[2 * d + 1:2 * d + 2, :])
        keep = _gla_masks(reverse)
        heads = list(range(HG_HPS))
        css = [pl.ds(hh * HG_DK, HG_DK) for hh in heads]
        lb = [lbs[:, hh * HG_DK:(hh + 1) * HG_DK] for hh in heads]
        for cc in range(cpb):
            c = cpb - 1 - cc if reverse else cc
            sl = pl.ds(c * C, C)
            v = [i_ref[sl, cs] for cs in css]
            t = _gla_chunk_fwd([q_ref[sl, cs] for cs in css], [f_ref[sl, cs] for cs in css], lb, keep, reverse)
            st = [s_scr[hh] for hh in heads]
            o = _m(lambda qs, s_, a, v_: _nt(qs, s_) + _nn(a, v_), t["qs"], st, t["a"], v)
            new = _m(lambda e_, s_, v_, k2: e_ * s_ + _tn(v_, k2), t["e_end"], st, v, t["k2"])
            for hh in heads:
                st_ref[c, hh] = st[hh]
                o_ref[sl, css[hh]] = o[hh]
                s_scr[hh] = new[hh]

    col = lambda off: (lambda h, b: (blk_map(b), off // W + h))
    return pl.pallas_call(
        body,
        name=name,
        grid=(HG_HEADS // HG_HPS, nblk),
        in_specs=[
            pl.BlockSpec((R, W), col(OFF_Q)),
            pl.BlockSpec((R, W), col(OFF_I)),
            pl.BlockSpec((R, W), col(f_off)),
            pl.BlockSpec((4, W), lambda h, b: (0, h)),
        ],
        out_specs=[
            pl.BlockSpec((R, W), lambda h, b: (blk_map(b), h)),
            pl.BlockSpec((cpb, HG_HPS, HG_DK, HG_DK), lambda h, b: (blk_map(b), h, 0, 0)),
        ],
        out_shape=[
            jax.ShapeDtypeStruct((S, HG_W), F32),
            jax.ShapeDtypeStruct((S // C, HG_HEADS, HG_DK, HG_DK), F32),
        ],
        scratch_shapes=[pltpu.VMEM((HG_HPS, HG_DK, HG_DK), F32)],
        compiler_params=_cparams(2),
    )(proj, proj, proj, lbl4)


def _gla_bwd(proj, lbl4, f_off, reverse, do, states, dproj, prev, name):
    S = proj.shape[0]
    PW = proj.shape[1]
    C = HG_CHUNK
    R = _rtile(S, 512)
    cpb, nblk = R // C, S // R
    d = 1 if reverse else 0
    blk_map = (lambda b: b) if reverse else (lambda b: nblk - 1 - b)
    final = prev is not None

    if final:
        assert HG_HPS == HG_HEADS and (OFF_Q, OFF_I, f_off) == (0, HG_W, 2 * HG_W)

    def body(*refs):
        if final:
            q_ref, i_ref, f_ref, lb_ref, do_ref, st_ref, pq_ref, pi_ref, _dp, o3_ref, dl_ref, ds_scr = refs
            dq_ref = di_ref = df_ref = o3_ref
        else:
            q_ref, i_ref, f_ref, lb_ref, do_ref, st_ref, dq_ref, di_ref, df_ref, dl_ref, ds_scr = refs
        out_off = (OFF_Q, OFF_I, f_off) if final else (0, 0, 0)
        blk = pl.program_id(1)

        @pl.when(blk == 0)
        def _():
            ds_scr[...] = jnp.zeros_like(ds_scr)
            dl_ref[...] = jnp.zeros_like(dl_ref)

        l = lb_ref[...]
        lbs = _sigmoid(l[2 * d:2 * d + 1, :] - l[2 * d + 1:2 * d + 2, :])
        keep = _gla_masks(reverse)
        keep_t = _gla_masks(not reverse)
        tri_t = jnp.where(keep_t, 1.0, 0.0).astype(CDT)
        end = 0 if reverse else C - 1
        is_end = lax.broadcasted_iota(jnp.int32, (C, HG_DK), 0) == end
        dl_all = [jnp.zeros((SUB, HG_DK), F32) for _ in range(HG_HPS)]
        gsz = HG_BWD_GROUP
        for cc, heads in [(cc, list(range(g0, g0 + gsz))) for cc in range(cpb) for g0 in range(0, HG_HPS, gsz)]:
            css = [pl.ds(hh * HG_DK, HG_DK) for hh in heads]
            lb = [lbs[:, hh * HG_DK:(hh + 1) * HG_DK] for hh in heads]
            dl_acc = [dl_all[hh] for hh in heads]
            c = cc if reverse else cpb - 1 - cc
            sl = pl.ds(c * C, C)
            qraw = [q_ref[sl, cs] for cs in css]
            v = [i_ref[sl, cs] for cs in css]
            t = _gla_chunk_fwd(qraw, [f_ref[sl, cs] for cs in css], lb, keep, reverse)
            dob = [do_ref[sl, cs].astype(CDT) for cs in css]
            vb = _m(lambda x: x.astype(CDT), v)
            st = [st_ref[c, hh] for hh in heads]
            ds = [ds_scr[hh] for hh in heads]
            dsb = _m(lambda x: x.astype(CDT), ds)
            d_qs = _m(_nn, dob, st)
            d_a = _m(lambda x, y: jnp.where(keep, _nt(x, y), 0.0), dob, vb)
            d_qt = _m(lambda x, y: _dot3(_nn, x, y), d_a, t["kt"])
            d_kt = _m(lambda x, y: _dot3(_tn, x, y), d_a, t["qt"])
            d_v = _m(lambda a, x, k2, s_: _tn(a, x) + _nt(k2, s_), t["a"], dob, t["k2"], dsb)
            d_k2 = _m(_nn, vb, dsb)
            d_e = _m(lambda s_, x: jnp.sum(s_ * x, axis=0, keepdims=True), st, ds)
            new_ds = _m(lambda e_, x, y, qs: e_ * x + _tn(y, qs), t["e_end"], ds, dob, t["qs"])
            dq = _m(lambda a, ea, b_, eb_: a * ea + b_ * eb_, d_qt, t["eq"], d_qs, t["eb"])
            dk = _m(lambda a, ea, b_, eb_: a * ea + b_ * eb_, d_kt, t["ek"], d_k2, t["e2"])
            db_end = _m(lambda x, k_, e2, de, ee: jnp.sum(x * (k_ * e2), axis=0, keepdims=True) + de * ee,
                        d_k2, t["k"], t["e2"], d_e, t["e_end"])
            db = _m(lambda q_, dq_, k_, dk_, be: q_ * dq_ - k_ * dk_ + jnp.where(is_end, be, 0.0),
                    t["q"], dq, t["k"], dk, db_end)
            dg = _m(lambda x: _tri_matmul(tri_t, x), db)
            df = _m(lambda g_, f_, dk_: g_ / f_ - dk_, dg, t["f"], dk)
            dfraw = _m(lambda x, l_, s_: x * (1.0 - l_) * s_ * (1.0 - s_), df, lb, t["sg"])
            dl_acc = _m(lambda acc, x, s_: acc + _rowsum8(x * (1.0 - s_)), dl_acc, df, t["sg"])
            dqraw = _m(lambda x, s_, r: x * (s_ * (1.0 + r * (1.0 - s_))), dq, t["sq"], qraw)
            if final:
                dqraw = [x + pq_ref[sl, cs] for x, cs in zip(dqraw, css)]
                d_v = [x + pi_ref[sl, cs] for x, cs in zip(d_v, css)]
            for n, hh in enumerate(heads):
                dl_all[hh] = dl_acc[n]
                ds_scr[hh] = new_ds[n]
                for ref, off, val in zip((dq_ref, di_ref, df_ref), out_off, (dqraw[n], d_v[n], dfraw[n])):
                    ref[sl, pl.ds(off + hh * HG_DK, HG_DK)] = val.astype(ref.dtype)
        dl_ref[...] += jnp.concatenate(dl_all, axis=1) * (lbs * (1.0 - lbs))

    W = HG_HPS * HG_DK
    col = lambda off: (lambda h, b: (blk_map(b), off // W + h))
    blk = lambda: pl.BlockSpec((R, W), lambda h, b: (blk_map(b), h))
    ins = [proj, proj, proj, lbl4, do, states]
    in_specs = [
        pl.BlockSpec((R, W), col(OFF_Q)),
        pl.BlockSpec((R, W), col(OFF_I)),
        pl.BlockSpec((R, W), col(f_off)),
        pl.BlockSpec((4, W), lambda h, b: (0, h)),
        blk(),
        pl.BlockSpec((cpb, HG_HPS, HG_DK, HG_DK), lambda h, b: (blk_map(b), h, 0, 0)),
    ]
    dl_shape = jax.ShapeDtypeStruct((SUB, HG_W), F32)
    dl_spec = pl.BlockSpec((SUB, W), lambda h, b: (0, h))
    dp_shape = jax.ShapeDtypeStruct((S, PW), CDT)
    if final:
        ins += [prev[0], prev[1], dproj]
        in_specs += [blk(), blk(), pl.BlockSpec(memory_space=pl.ANY)]
        out_shape = [dp_shape, dl_shape]
        out_specs = [pl.BlockSpec((R, 3 * HG_W), lambda h, b: (blk_map(b), 0)), dl_spec]
        aliases = {8: 0}
    else:
        out_shape = [jax.ShapeDtypeStruct((S, HG_W), F32), jax.ShapeDtypeStruct((S, HG_W), F32), dp_shape, dl_shape]
        out_specs = [blk(), blk(), pl.BlockSpec((R, W), col(f_off)), dl_spec]
        aliases = {}
        if dproj is not None:
            ins += [dproj]
            in_specs += [pl.BlockSpec(memory_space=pl.ANY)]
            aliases = {6: 2}
    if (not final) and dproj is not None:
        def body_wrapped(*refs, _b=body):
            _b(*refs[:6], *refs[7:])
        kern = body_wrapped
    else:
        kern = body
    res = pl.pallas_call(
        kern,
        name=name,
        grid=(HG_HEADS // HG_HPS, nblk),
        in_specs=in_specs,
        out_specs=out_specs,
        out_shape=out_shape,
        scratch_shapes=[pltpu.VMEM((HG_HPS, HG_DK, HG_DK), F32)],
        input_output_aliases=aliases,
        compiler_params=_cparams(2),
    )(*ins)
    if final:
        return res[0], None, None, res[1]
    dq, di, dproj, dl = res
    return dproj, dq, di, dl


def _hgrn_post_fwd(o_f, o_b, proj, norm_g):
    S = o_f.shape[0]

    def epi(accs, tiles, rows):
        of, ob, graw = tiles
        ng = rows[0][:, :HG_DK]
        o = of + ob
        ys = []
        for h in range(HG_HEADS):
            oh = o[:, h * HG_DK:(h + 1) * HG_DK]
            rs = lax.rsqrt(jnp.mean(oh * oh, axis=-1, keepdims=True) + RMS_EPS)
            ys.append(oh * rs * ng * _sigmoid(graw[:, h * HG_DK:(h + 1) * HG_DK]))
        return [jnp.concatenate(ys, axis=1)], []

    tm = _rtile(S, 512)
    (y,) = _fused_mm("hgrn_post_fwd", "nn", [], S, HG_W, 1, tm, HG_W, 1, [(HG_W, CDT, 0, None)], epi,
                     tiles=[(o_f, 0), (o_b, 0), (proj, OFF_G)], rows=[(jnp.tile(norm_g, (1, HG_HEADS)), 0)])
    return y


def _hgrn_post_bwd(dy, o_f, o_b, proj, norm_g, dproj):
    S = o_f.shape[0]

    def epi(accs, tiles, rows):
        dyv, of, ob, graw = tiles
        ng = rows[0][:, :HG_DK]
        o = of + ob
        dos, dgs, dns = [], [], []
        for h in range(HG_HEADS):
            sl = slice(h * HG_DK, (h + 1) * HG_DK)
            oh, gh, dyh = o[:, sl], graw[:, sl], dyv[:, sl].astype(F32)
            rs = lax.rsqrt(jnp.mean(oh * oh, axis=-1, keepdims=True) + RMS_EPS)
            xh = oh * rs
            sg = _sigmoid(gh)
            dn = dyh * sg
            dgs.append(dyh * (xh * ng) * sg * (1.0 - sg))
            dns.append(dn * xh)
            dxh = dn * ng
            dos.append(rs * (dxh - xh * jnp.mean(dxh * xh, axis=-1, keepdims=True)))
        return [jnp.concatenate(dos, axis=1), jnp.concatenate(dgs, axis=1)], [jnp.concatenate(dns, axis=1)]

    tm = _rtile(S, 512)
    do, dproj, dn = _fused_mm("hgrn_post_bwd", "nn", [], S, HG_W, 1, tm, HG_W, 1,
                              [(HG_W, F32, 0, None), (dproj.shape[1], CDT, OFF_G, dproj)], epi,
                              tiles=[(dy, 0), (o_f, 0), (o_b, 0), (proj, OFF_G)],
                              rows=[(jnp.tile(norm_g, (1, HG_HEADS)), 0)], n_racc=1)
    return do, dproj, dn


def _copy_into(name, src, dst, off):
    S, W = src.shape
    tm = _rtile(S, 512)
    (dst,) = _fused_mm(name, "nn", [], S, W, 1, tm, W, 1, [(dst.shape[1], dst.dtype, off, dst)],
                       lambda accs, tiles, rows: ([tiles[0]], []), tiles=[(src, 0)])
    return dst


def _rms_stats(x):
    rs = lax.rsqrt(jnp.mean(x * x, axis=-1, keepdims=True) + RMS_EPS)
    return x * rs, rs


def _mla_up(proj, cf, sf, g_cq, g_ckv, wuq_p, wukv):
    S = proj.shape[0]
    tm = _rtile(S, 512)
    H = MLA_HEADS

    def body(cq_ref, ckv_ref, kr_ref, krot_ref, cf_ref, sf_ref, gq_ref, gkv_ref, wq_ref, wkv_ref,
             q_ref, k_ref, v_ref, vt_ref, cqn_ref, ckvn_ref):
        cqn = (_rms_stats(cq_ref[...])[0] * gq_ref[...]).astype(CDT)
        ckvn = (_rms_stats(ckv_ref[...])[0] * gkv_ref[...]).astype(CDT)
        cqn_ref[...] = cqn
        ckvn_ref[...] = ckvn
        cfv, sfv = cf_ref[...], sf_ref[...]
        k_roped = (kr_ref[...] * cfv + krot_ref[...] * sfv).astype(CDT)
        ones = jnp.ones((VT_ROWS - LANE, tm), CDT)
        for h in range(H):
            r = _nn(cqn, wq_ref[h]) * MLA_QSCALE
            q_ref[h, :, 0:LANE] = r[:, 0:LANE].astype(CDT)
            q_ref[h, :, LANE:2 * LANE] = (r[:, LANE:2 * LANE] * cfv + r[:, 2 * LANE:3 * LANE] * sfv).astype(CDT)
            kv = _nn(ckvn, wkv_ref[h])
            k_ref[h, :, 0:LANE] = kv[:, 0:LANE].astype(CDT)
            k_ref[h, :, LANE:2 * LANE] = k_roped
            vv = kv[:, LANE:2 * LANE]
            v_ref[h] = vv.astype(CDT)
            vt_ref[h, 0, 0:LANE, :] = vv.T.astype(CDT)
            vt_ref[h, 0, LANE:VT_ROWS, :] = ones

    PWb = proj.shape[1]
    kr_off = PWb - KR_PAD
    cspec = lambda off, w: pl.BlockSpec((tm, w), lambda i, o=off // w: (i, o))
    return pl.pallas_call(
        body,
        name="mla_up_fwd",
        grid=(S // tm,),
        in_specs=[
            cspec(OFF_CQ, MLA_RANK), cspec(OFF_CKV, MLA_RANK), cspec(kr_off, LANE), cspec(kr_off + LANE, LANE),
            pl.BlockSpec((tm, LANE), lambda i: (i, 0)), pl.BlockSpec((tm, LANE), lambda i: (i, 0)),
            pl.BlockSpec((1, MLA_RANK), lambda i: (0, 0)), pl.BlockSpec((1, MLA_RANK), lambda i: (0, 0)),
            pl.BlockSpec((H, MLA_RANK, 3 * LANE), lambda i: (0, 0, 0)),
            pl.BlockSpec((H, MLA_RANK, 2 * LANE), lambda i: (0, 0, 0)),
        ],
        out_specs=[
            pl.BlockSpec((H, tm, 2 * LANE), lambda i: (0, i, 0)),
            pl.BlockSpec((H, tm, 2 * LANE), lambda i: (0, i, 0)),
            pl.BlockSpec((H, tm, LANE), lambda i: (0, i, 0)),
            pl.BlockSpec((H, 1, VT_ROWS, tm), lambda i: (0, i, 0, 0)),
            pl.BlockSpec((tm, MLA_RANK), lambda i: (i, 0)),
            pl.BlockSpec((tm, MLA_RANK), lambda i: (i, 0)),
        ],
        out_shape=[
            jax.ShapeDtypeStruct((H, S, 2 * LANE), CDT), jax.ShapeDtypeStruct((H, S, 2 * LANE), CDT),
            jax.ShapeDtypeStruct((H, S, LANE), CDT), jax.ShapeDtypeStruct((H, S // tm, VT_ROWS, tm), CDT),
            jax.ShapeDtypeStruct((S, MLA_RANK), CDT), jax.ShapeDtypeStruct((S, MLA_RANK), CDT),
        ],
        compiler_params=_cparams(1),
    )(proj, proj, proj, proj, cf, sf, g_cq, g_ckv, wuq_p, wukv)


def _mla_attn_fwd(q_cat, k_cat, vt, xchg=None):
    H, S, _ = q_cat.shape
    tq = _tile(S, MLA_TQ_FWD)
    _, nkb, _, tk = vt.shape
    nq = S // tq
    nx = xchg.n if xchg is not None else 0

    def body(*refs):
        q_ref, k_ref, vt_ref = refs[:3]
        x_in = refs[3:3 + nx]
        y_ref, ot_ref, lse_ref = refs[3 + nx:6 + nx]
        x_out = refs[6 + nx:6 + 2 * nx]
        m_scr, acc_scr = refs[6 + 2 * nx:8 + 2 * nx]
        x_sems = refs[8 + 2 * nx:]
        h, i = pl.program_id(0), pl.program_id(1)
        if nx:
            @pl.when((h == 0) & (i == 0))
            def _():
                xchg.start(x_in, x_out, x_sems)

        nsub = MLA_FWD_SLABS if tq % (MLA_FWD_SLABS * LANE) == 0 else 1
        ws = tq // nsub
        subs = [pl.ds(s * ws, ws) for s in range(nsub)]
        qs = [q_ref[0, sb, :] for sb in subs]
        m_scr[...] = jnp.full_like(m_scr, -jnp.inf)
        acc_scr[...] = jnp.zeros_like(acc_scr)

        def step(j, carry):
            kj = k_ref[0, pl.ds(pl.multiple_of(j * tk, tk), tk), :]
            vtj = vt_ref[0, j]
            sts = [_nt(kj, qq) for qq in qs]
            m_old = [m_scr[:, sb] for sb in subs]
            m_new = _m(lambda mo, st: jnp.maximum(mo, jnp.max(st, axis=0, keepdims=True)), m_old, sts)
            pts = _m(lambda st, mn: jnp.exp2(st - mn), sts, m_new)
            pvs = _m(lambda pt: _nn(vtj, pt), pts)
            for sb, mo, mn, pv in zip(subs, m_old, m_new, pvs):
                acc_scr[:, sb] = jnp.exp2(mo - mn) * acc_scr[:, sb] + pv
                m_scr[:, sb] = mn
            return carry

        lax.fori_loop(0, nkb, step, 0, unroll=4 if nkb % 4 == 0 else 1)
        l = acc_scr[LANE:LANE + 1, :]
        ot = acc_scr[0:LANE, :] / l
        ot_ref[0] = ot
        y_ref[...] = ot.T.astype(CDT)
        lse_ref[0, 0] = m_scr[...] + jnp.log2(l)

        if nx:
            @pl.when((h == H - 1) & (i == nq - 1))
            def _():
                xchg.wait(x_in, x_out, x_sems)

    return pl.pallas_call(
        body,
        name="mla_attn_fwd",
        grid=(H, nq),
        in_specs=[
            pl.BlockSpec((1, tq, 2 * LANE), lambda h, i: (h, i, 0)),
            pl.BlockSpec((1, S, 2 * LANE), lambda h, i: (h, 0, 0)),
            pl.BlockSpec((1, nkb, VT_ROWS, tk), lambda h, i: (h, 0, 0, 0)),
        ] + (xchg.specs if nx else []),
        out_specs=[
            pl.BlockSpec((tq, LANE), lambda h, i: (i, h)),
            pl.BlockSpec((1, LANE, tq), lambda h, i: (h, 0, i)),
            pl.BlockSpec((1, 1, 1, tq), lambda h, i: (h, i, 0, 0)),
        ] + (xchg.specs if nx else []),
        out_shape=[
            jax.ShapeDtypeStruct((S, H * LANE), CDT),
            jax.ShapeDtypeStruct((H, LANE, S), F32),
            jax.ShapeDtypeStruct((H, nq, 1, tq), F32),
        ] + (xchg.out_shape if nx else []),
        scratch_shapes=[pltpu.VMEM((1, tq), F32), pltpu.VMEM((VT_ROWS, tq), F32)] + (xchg.scratch if nx else []),
        compiler_params=_cparams(2, side_effects=nx > 0),
    )(q_cat, k_cat, vt, *(xchg.arrs if nx else []))


def _mla_delta(dy, ot):
    H, _, S = ot.shape
    tq = _tile(S, MLA_TQ)
    nq = S // tq

    def body(dy_ref, ot_ref, d_ref):
        d_ref[0, 0] = jnp.sum(dy_ref[...].astype(F32).T * ot_ref[0], axis=0, keepdims=True)

    return pl.pallas_call(
        body,
        name="mla_delta",
        grid=(H, nq),
        in_specs=[pl.BlockSpec((tq, LANE), lambda h, i: (i, h)), pl.BlockSpec((1, LANE, tq), lambda h, i: (h, 0, i))],
        out_specs=pl.BlockSpec((1, 1, 1, tq), lambda h, i: (h, i, 0, 0)),
        out_shape=jax.ShapeDtypeStruct((H, nq, 1, tq), F32),
        compiler_params=_cparams(2),
    )(dy, ot)


def _mla_attn_bwd(q_cat, k_cat, v, dy, lse, delta, xchg=None):
    H, S, _ = q_cat.shape
    _, nq, _, tq = lse.shape
    tk = _tile(S, 1024)
    nkb = S // tk
    nx = xchg.n if xchg is not None else 0

    def body(*refs):
        k_ref, v_ref, q_ref, do_ref, lse_ref, dl_ref = refs[:6]
        x_in = refs[6:6 + nx]
        dk_ref, dv_ref, dq_ref = refs[6 + nx:9 + nx]
        x_out = refs[9 + nx:9 + 2 * nx]
        dk_scr, dv_scr = refs[9 + 2 * nx:11 + 2 * nx]
        x_sems = refs[11 + 2 * nx:]
        hd, ki = pl.program_id(0), pl.program_id(1)
        if nx:
            @pl.when((hd == 0) & (ki == 0))
            def _():
                xchg.start(x_in, x_out, x_sems)

        @pl.when(ki == 0)
        def _():
            dq_ref[...] = jnp.zeros_like(dq_ref)

        kb, vb = k_ref[0], v_ref[0]
        dk_scr[...] = jnp.zeros_like(dk_scr)
        dv_scr[...] = jnp.zeros_like(dv_scr)

        nsub = MLA_BWD_SLABS if tq % (MLA_BWD_SLABS * LANE) == 0 else 1
        ws = tq // nsub

        def step(i, carry):
            rows = [pl.ds(pl.multiple_of(i * tq + s * ws, ws), ws) for s in range(nsub)]
            lanes = [slice(s * ws, (s + 1) * ws) for s in range(nsub)]
            qc = [q_ref[0, r, :] for r in rows]
            doc = [do_ref[r, :] for r in rows]
            lse_i, dl_i = lse_ref[0, i], dl_ref[0, i]
            st = _m(lambda q_: _nt(kb, q_), qc)
            dp = _m(lambda d_: _nt(vb, d_), doc)
            pt = _m(lambda s_, ln: jnp.exp2(s_ - lse_i[:, ln]), st, lanes)
            dst = _m(lambda p_, d_, ln: (p_ * (d_ - dl_i[:, ln])).astype(CDT), pt, dp, lanes)
            dv_scr[...] += sum(_m(_nn, pt, doc))
            dk_scr[...] += sum(_m(_nn, dst, qc))
            dqs = _m(lambda d_: _tn(d_, kb), dst)
            for r, dq_ in zip(rows, dqs):
                dq_ref[0, r, :] += dq_
            return carry

        lax.fori_loop(0, nq, step, 0, unroll=2 if nq % 2 == 0 else 1)
        dk_ref[0] = dk_scr[...] * (MLA_SCALE / MLA_QSCALE)
        dv_ref[0] = dv_scr[...]

        if nx:
            @pl.when((hd == H - 1) & (ki == nkb - 1))
            def _():
                xchg.wait(x_in, x_out, x_sems)

    return pl.pallas_call(
        body,
        name="mla_attn_bwd",
        grid=(H, nkb),
        in_specs=[
            pl.BlockSpec((1, tk, 2 * LANE), lambda h, j: (h, j, 0)),
            pl.BlockSpec((1, tk, LANE), lambda h, j: (h, j, 0)),
            pl.BlockSpec((1, S, 2 * LANE), lambda h, j: (h, 0, 0)),
            pl.BlockSpec((S, LANE), lambda h, j: (0, h)),
            pl.BlockSpec((1, nq, 1, tq), lambda h, j: (h, 0, 0, 0)),
            pl.BlockSpec((1, nq, 1, tq), lambda h, j: (h, 0, 0, 0)),
        ] + (xchg.specs if nx else []),
        out_specs=[
            pl.BlockSpec((1, tk, 2 * LANE), lambda h, j: (h, j, 0)),
            pl.BlockSpec((1, tk, LANE), lambda h, j: (h, j, 0)),
            pl.BlockSpec((1, S, 2 * LANE), lambda h, j: (h, 0, 0)),
        ] + (xchg.specs if nx else []),
        out_shape=[
            jax.ShapeDtypeStruct((H, S, 2 * LANE), F32),
            jax.ShapeDtypeStruct((H, S, LANE), F32),
            jax.ShapeDtypeStruct((H, S, 2 * LANE), F32),
        ] + (xchg.out_shape if nx else []),
        scratch_shapes=[pltpu.VMEM((tk, 2 * LANE), F32), pltpu.VMEM((tk, LANE), F32)] + (xchg.scratch if nx else []),
        compiler_params=_cparams(2, side_effects=nx > 0),
    )(k_cat, v, q_cat, dy, lse, delta, *(xchg.arrs if nx else []))


def _mla_up_bwd(dq_cat, dk_cat, dv, proj, cf, sf, g_cq, g_ckv, wuq_p, wukv, dproj):
    H, S, _ = dq_cat.shape
    tm = _rtile(S, 256)
    PW = proj.shape[1]
    kr_off = PW - KR_PAD

    assert OFF_CKV == OFF_CQ + MLA_RANK and OFF_CQ % (2 * MLA_RANK) == 0

    def body(dq_ref, dk_ref, dv_ref, cq_ref, ckv_ref, cf_ref, sf_ref, gq_ref, gkv_ref, wq_ref, wkv_ref, _dp,
             dqp_ref, dkvp_ref, dc_ref, dkr_ref, dgq_ref, dgkv_ref):
        i = pl.program_id(0)
        dcq_ref, dckv_ref = dc_ref.at[:, 0:MLA_RANK], dc_ref.at[:, MLA_RANK:2 * MLA_RANK]
        cfv, sfv = cf_ref[...], sf_ref[...]
        aq = jnp.zeros((tm, MLA_RANK), F32)
        akv = jnp.zeros((tm, MLA_RANK), F32)
        akr = jnp.zeros((tm, LANE), F32)
        for h in range(H):
            dq = dq_ref[h] * MLA_SCALE
            dqr = dq[:, LANE:2 * LANE]
            dqp = jnp.concatenate([dq[:, 0:LANE], dqr * cfv, dqr * sfv], axis=1).astype(CDT)
            dqp_ref[:, pl.ds(h * 3 * LANE, 3 * LANE)] = dqp
            aq = aq + _nt(dqp, wq_ref[h])
            dk = dk_ref[h]
            dkvp = jnp.concatenate([dk[:, 0:LANE], dv_ref[h]], axis=1).astype(CDT)
            dkvp_ref[:, pl.ds(h * 2 * LANE, 2 * LANE)] = dkvp
            akv = akv + _nt(dkvp, wkv_ref[h])
            akr = akr + dk[:, LANE:2 * LANE]

        def rms_bwd(c_ref, g_ref, dn, d_ref, dg_ref):
            xh, rs = _rms_stats(c_ref[...])
            dxh = dn * g_ref[...]
            d_ref[...] = (rs * (dxh - xh * jnp.mean(dxh * xh, axis=-1, keepdims=True))).astype(d_ref.dtype)
            part = _rowsum8(dn * xh)

            @pl.when(i == 0)
            def _():
                dg_ref[...] = part

            @pl.when(i > 0)
            def _():
                dg_ref[...] += part

        rms_bwd(cq_ref, gq_ref, aq, dcq_ref, dgq_ref)
        rms_bwd(ckv_ref, gkv_ref, akv, dckv_ref, dgkv_ref)
        dkr_ref[...] = jnp.concatenate([akr * cfv, akr * sfv, jnp.zeros((tm, KR_PAD - 2 * LANE), F32)], axis=1).astype(dkr_ref.dtype)

    cspec = lambda off, w: pl.BlockSpec((tm, w), lambda i, o=off // w: (i, o))
    hspec = lambda w: pl.BlockSpec((H, tm, w), lambda i: (0, i, 0))
    outs = pl.pallas_call(
        body,
        name="mla_up_bwd",
        grid=(S // tm,),
        in_specs=[
            hspec(2 * LANE), hspec(2 * LANE), hspec(LANE),
            cspec(OFF_CQ, MLA_RANK), cspec(OFF_CKV, MLA_RANK),
            pl.BlockSpec((tm, LANE), lambda i: (i, 0)), pl.BlockSpec((tm, LANE), lambda i: (i, 0)),
            pl.BlockSpec((1, MLA_RANK), lambda i: (0, 0)), pl.BlockSpec((1, MLA_RANK), lambda i: (0, 0)),
            pl.BlockSpec((H, MLA_RANK, 3 * LANE), lambda i: (0, 0, 0)),
            pl.BlockSpec((H, MLA_RANK, 2 * LANE), lambda i: (0, 0, 0)),
            pl.BlockSpec(memory_space=pl.ANY),
        ],
        out_specs=[
            pl.BlockSpec((tm, H * 3 * LANE), lambda i: (i, 0)), pl.BlockSpec((tm, H * 2 * LANE), lambda i: (i, 0)),
            pl.BlockSpec((tm, 2 * MLA_RANK), lambda i: (i, OFF_CQ // (2 * MLA_RANK))),
            pl.BlockSpec((tm, KR_PAD), lambda i: (i, 0)),
            pl.BlockSpec((SUB, MLA_RANK), lambda i: (0, 0)),
            pl.BlockSpec((SUB, MLA_RANK), lambda i: (0, 0)),
        ],
        out_shape=[
            jax.ShapeDtypeStruct((S, H * 3 * LANE), CDT), jax.ShapeDtypeStruct((S, H * 2 * LANE), CDT),
            jax.ShapeDtypeStruct(dproj.shape, dproj.dtype),
            jax.ShapeDtypeStruct((S, KR_PAD), CDT),
            jax.ShapeDtypeStruct((SUB, MLA_RANK), F32), jax.ShapeDtypeStruct((SUB, MLA_RANK), F32),
        ],
        input_output_aliases={11: 2},
        compiler_params=_cparams(1),
    )(dq_cat, dk_cat, dv, proj, proj, cf, sf, g_cq, g_ckv, wuq_p, wukv, dproj)
    dqp, dkvp, dproj, dkr, dgq, dgkv = outs
    dproj = _copy_into("dproj_kr", dkr, dproj, kr_off)
    return dproj, dqp, dkvp, dgq, dgkv


def _mem_softmax(q, k):
    s = _nt(q, k) * (MEM_HD ** -0.5)
    p = jnp.exp(s - jnp.max(s, axis=1, keepdims=True))
    return p / jnp.sum(p, axis=1, keepdims=True)


def _mem_attn_fwd(proj, memkv):
    S = proj.shape[0]
    Mm = memkv.shape[0]
    tm = _rtile(S, 1024)

    def body(q_ref, k_ref, v_ref, y_ref):
        pn = _mem_softmax(q_ref[...], k_ref[...])
        y_ref[...] = _nn(pn, v_ref[...]).astype(y_ref.dtype)

    return pl.pallas_call(
        body,
        name="mem_attn_fwd",
        grid=(S // tm, MEM_HEADS),
        in_specs=[
            pl.BlockSpec((tm, MEM_HD), lambda i, h: (i, OFF_QM // MEM_HD + h)),
            pl.BlockSpec((Mm, MEM_HD), lambda i, h: (0, h)),
            pl.BlockSpec((Mm, MEM_HD), lambda i, h: (0, MEM_HEADS + h)),
        ],
        out_specs=pl.BlockSpec((tm, MEM_HD), lambda i, h: (i, h)),
        out_shape=jax.ShapeDtypeStruct((S, MEM_W), CDT),
        compiler_params=_cparams(2),
    )(proj, memkv, memkv)


def _mem_attn_bwd(dy, proj, memkv, dproj):
    S = proj.shape[0]
    Mm = memkv.shape[0]
    tm = _rtile(S, 1024)
    scale = MEM_HD ** -0.5

    def body(dy_ref, q_ref, k_ref, v_ref, _dp, dq_ref, dk_ref, dv_ref):
        i = pl.program_id(1)
        q, k, dyv = q_ref[...].astype(CDT), k_ref[...], dy_ref[...]
        pn = _mem_softmax(q, k)
        dvp = _tn(pn, dyv)
        dp = _nt(dyv, v_ref[...])
        ds = pn * (dp - jnp.sum(dp * pn, axis=1, keepdims=True)) * scale
        dq_ref[...] = _nn(ds, k).astype(dq_ref.dtype)
        dkp = _tn(ds, q)

        @pl.when(i == 0)
        def _():
            dk_ref[...] = dkp
            dv_ref[...] = dvp

        @pl.when(i > 0)
        def _():
            dk_ref[...] += dkp
            dv_ref[...] += dvp

    dproj, dk, dv = pl.pallas_call(
        body,
        name="mem_attn_bwd",
        grid=(MEM_HEADS, S // tm),
        in_specs=[
            pl.BlockSpec((tm, MEM_HD), lambda h, i: (i, h)),
            pl.BlockSpec((tm, MEM_HD), lambda h, i: (i, OFF_QM // MEM_HD + h)),
            pl.BlockSpec((Mm, MEM_HD), lambda h, i: (0, h)),
            pl.BlockSpec((Mm, MEM_HD), lambda h, i: (0, MEM_HEADS + h)),
            pl.BlockSpec(memory_space=pl.ANY),
        ],
        out_specs=[
            pl.BlockSpec((tm, MEM_HD), lambda h, i: (i, OFF_QM // MEM_HD + h)),
            pl.BlockSpec((Mm, MEM_HD), lambda h, i: (0, h)),
            pl.BlockSpec((Mm, MEM_HD), lambda h, i: (0, h)),
        ],
        out_shape=[
            jax.ShapeDtypeStruct(dproj.shape, dproj.dtype),
            jax.ShapeDtypeStruct((Mm, MEM_W), F32),
            jax.ShapeDtypeStruct((Mm, MEM_W), F32),
        ],
        input_output_aliases={4: 0},
        compiler_params=_cparams(2),
    )(dy, proj, memkv, memkv, dproj)
    return dproj, dk, dv


def _small_allreduce(vec):
    NS = vec.shape[1]

    def body(v_ref, o_ref, gbuf, mine, send, recv):
        x, y, c, me = _my_place()
        mine[...] = jnp.sum(v_ref[...], axis=0, keepdims=True)
        gbuf[me] = mine[...]
        copies = []
        for kk in range(1, N_DEV):
            peer, _ = _peer(x, y, c, kk)
            cp = pltpu.make_async_remote_copy(src_ref=mine, dst_ref=gbuf.at[me], send_sem=send.at[kk - 1],
                                              recv_sem=recv.at[kk - 1], device_id=peer, device_id_type=MESH)
            cp.start()
            copies.append(cp)
        for cp in copies:
            cp.wait()
        tot = gbuf[0]
        for d in range(1, N_DEV):
            tot = tot + gbuf[d]
        o_ref[...] = tot

    return pl.pallas_call(
        body,
        name="small_allreduce",
        in_specs=[pl.BlockSpec(memory_space=pltpu.VMEM)],
        out_specs=pl.BlockSpec(memory_space=pltpu.VMEM),
        out_shape=jax.ShapeDtypeStruct((1, NS), F32),
        scratch_shapes=[pltpu.VMEM((N_DEV, 1, NS), F32), pltpu.VMEM((1, NS), F32), pltpu.SemaphoreType.DMA((N_DEV - 1,)),
                        pltpu.SemaphoreType.DMA((N_DEV - 1,))],
        compiler_params=pltpu.CompilerParams(has_side_effects=True, vmem_limit_bytes=V7X_VMEM_LIMIT),
    )(vec)


def _adamw_math(g, w, m, v):
    nm = ADAM_B1 * m + (1.0 - ADAM_B1) * g
    nv = ADAM_B2 * v + (1.0 - ADAM_B2) * (g * g)
    mh = nm / (1.0 - ADAM_B1 ** ADAM_STEP)
    vh = nv / (1.0 - ADAM_B2 ** ADAM_STEP)
    delta = -ADAM_LR * (mh / (jnp.sqrt(vh) + ADAM_EPS) + ADAM_WD * w)
    return delta, nm, nv


def _adam_big(name, recv, w, m, v):
    _, R, C = w.shape
    tr = _rtile(R, max(SUB, (ADAM_BLOCK_ELEMS // C) // SUB * SUB))

    def body(r_ref, w_ref, m_ref, v_ref, g_ref, d_ref, nm_ref, nv_ref):
        g = r_ref[0].astype(F32)
        for d in range(1, N_DEV):
            g = g + r_ref[d].astype(F32)
        delta, nm, nv = _adamw_math(g, w_ref[0], m_ref[0], v_ref[0])
        g_ref[0] = g
        d_ref[0] = delta
        nm_ref[0] = nm
        nv_ref[0] = nv

    blk = pl.BlockSpec((1, tr, C), lambda i: (0, i, 0))
    return pl.pallas_call(
        body,
        name=name,
        grid=(R // tr,),
        in_specs=[pl.BlockSpec((N_DEV, tr, C), lambda i: (0, i, 0)), blk, blk, blk],
        out_specs=[blk, blk, blk, blk],
        out_shape=[jax.ShapeDtypeStruct((1, R, C), F32)] * 4,
        compiler_params=_cparams(1),
    )(recv, w, m, v)


def _to_bf16(name, w):
    _, R, C = w.shape
    tr = _rtile(R, max(SUB, (ADAM_BLOCK_ELEMS // C) // SUB * SUB))

    def body(w_ref, o_ref):
        o_ref[...] = w_ref[0].astype(CDT)

    return pl.pallas_call(
        body,
        name=name,
        grid=(R // tr,),
        in_specs=[pl.BlockSpec((1, tr, C), lambda i: (0, i, 0))],
        out_specs=pl.BlockSpec((tr, C), lambda i: (i, 0)),
        out_shape=jax.ShapeDtypeStruct((R, C), CDT),
        compiler_params=_cparams(1),
    )(w)


def _adam_small(g, w, m, v):
    def body(g_ref, w_ref, m_ref, v_ref, d_ref, nm_ref, nv_ref):
        delta, nm, nv = _adamw_math(g_ref[...], w_ref[...], m_ref[...], v_ref[...])
        d_ref[...] = delta
        nm_ref[...] = nm
        nv_ref[...] = nv

    return pl.pallas_call(body, name="adam_small", out_shape=[jax.ShapeDtypeStruct(g.shape, F32)] * 3)(g, w, m, v)


def _rot(w, axis=-1):
    x1, x2 = jnp.split(w, 2, axis=axis)
    return jnp.concatenate([-x2, x1], axis=axis)


def _unrot(dw, axis=-1):
    d1, d2 = jnp.split(dw, 2, axis=axis)
    return jnp.concatenate([d2, -d1], axis=axis)


def _pad_cols(w, width):
    return jnp.pad(w, [(0, 0)] * (w.ndim - 1) + [(0, width - w.shape[-1])])


def kernel(x, mem, positions, ln_emb_g, ln_emb_b, hgrn_lb_logits, w_in, hgrn_norm_g, mla_g_cq, mla_g_ckv, mla_w_uq, mla_w_ukv, mem_w_kv, w_branch, w_o, ln1_g, ln1_b, w_ffn_gate, w_ffn_up, w_ffn_down, ln2_g, ln2_b, loss_target, m_ln_emb_g, m_ln_emb_b, m_hgrn_lb_logits, m_w_in, m_hgrn_norm_g, m_mla_g_cq, m_mla_g_ckv, m_mla_w_uq, m_mla_w_ukv, m_mem_w_kv, m_w_branch, m_w_o, m_ln1_g, m_ln1_b, m_w_ffn_gate, m_w_ffn_up, m_w_ffn_down, m_ln2_g, m_ln2_b, v_ln_emb_g, v_ln_emb_b, v_hgrn_lb_logits, v_w_in, v_hgrn_norm_g, v_mla_g_cq, v_mla_g_ckv, v_mla_w_uq, v_mla_w_ukv, v_mem_w_kv, v_w_branch, v_w_o, v_ln1_g, v_ln1_b, v_w_ffn_gate, v_w_ffn_up, v_w_ffn_down, v_ln2_g, v_ln2_b):
    x2, tgt = x[0], loss_target[0]
    S, D = x2.shape
    Mm = mem.shape[1]
    F = w_ffn_gate.shape[2] * N_DEV
    GW = 3 * D
    PW = OFF_GATE + GW + KR_PAD
    KR = OFF_GATE + GW
    NIN = w_in.shape[2] * N_DEV
    assert NIN == OFF_GATE + MLA_ROPE + GW
    _, _, _, me = _my_place()
    row = lambda a: a.reshape(1, -1)

    br3 = lambda a: a.reshape(1, 3 * BR_W, -1)
    tp = lambda a: jnp.swapaxes(a, 1, 2)
    big_w = [tp(w_in), mla_w_uq, mla_w_ukv, mem_w_kv, br3(w_branch), w_o, tp(w_ffn_gate), tp(w_ffn_up), w_ffn_down]
    big_m = [tp(m_w_in), m_mla_w_uq, m_mla_w_ukv, m_mem_w_kv, br3(m_w_branch), m_w_o, tp(m_w_ffn_gate), tp(m_w_ffn_up),
             m_w_ffn_down]
    big_v = [tp(v_w_in), v_mla_w_uq, v_mla_w_ukv, v_mem_w_kv, br3(v_w_branch), v_w_o, tp(v_w_ffn_gate), tp(v_w_ffn_up),
             v_w_ffn_down]
    transposed = (0, 6, 7)
    wnames = ["w_in", "w_uq", "w_ukv", "mem_w_kv", "w_branch", "w_o", "w_gate", "w_up", "w_down"]
    big_wb = [_to_bf16("bf16_" + nme, w) for nme, w in zip(wnames, big_w)]
    g_in, g_lb = _all_gather_two_level("weights_all_gather", [big_wb[0], hgrn_lb_logits.reshape(4, -1)])
    win_t = g_in.reshape(NIN, D)
    kr_w = win_t[OFF_QM:OFF_QM + MLA_ROPE]
    zeros64 = jnp.zeros_like(kr_w)
    win_pt = jnp.concatenate([win_t[:OFF_FB], win_t[OFF_FF:OFF_G], win_t[OFF_FB:OFF_FF], win_t[OFF_G:OFF_QM],
                              win_t[OFF_QM + MLA_ROPE:], kr_w, zeros64, _rot(kr_w, 0), zeros64,
                              jnp.zeros((KR_PAD - 2 * LANE, D), CDT)], axis=0)
    lbl4 = jnp.transpose(g_lb, (1, 0, 2)).reshape(4, -1)

    half = MLA_ROPE // 2
    inv_freq = jnp.power(ROPE_THETA, -jnp.arange(half, dtype=F32) / half)
    ang = positions[0].astype(F32)[:, None] * inv_freq
    cf = _pad_cols(jnp.tile(jnp.cos(ang), (1, 2)), LANE)
    sf = _pad_cols(jnp.tile(jnp.sin(ang), (1, 2)), LANE)

    tm512 = _rtile(S, 512)
    ident = lambda accs, tiles, rows: ([accs[0]], [])

    def epi_ln0(accs, tiles, rows):
        h = _ln_stats(tiles[0])[0] * rows[0] + rows[1]
        return [h, h], []

    h0, h0b = _fused_mm("ln_emb_fwd", "nn", [], S, D, 1, tm512, D, 1, [(D, F32, 0, None), (D, CDT, 0, None)], epi_ln0,
                        tiles=[(x2, 0)], rows=[(row(ln_emb_g), 0), (row(ln_emb_b), 0)])
    proj, g_uq, g_ukv, g_mkv, g_wb, g_wo = _fused_mm(
        "proj", "nt", [[(h0b, 0, win_pt, 0)]], S, PW, D, _rtile(S, 1024), _tile(PW, 1536), D, [(PW, F32, 0, None)], ident,
        xchg=_Xchg(big_wb[1:6], False), msplit=2 if S % 2048 == 0 else 1)
    wuq_p =jnp.concatenate([g_uq[..., :MLA_NOPE], _pad_cols(g_uq[..., MLA_NOPE:], LANE),
                             _pad_cols(_rot(g_uq[..., MLA_NOPE:]), LANE)], axis=-1)
    wukv = g_ukv
    wmkv = g_mkv.reshape(-1, g_mkv.shape[-1])
    wb = jnp.transpose(g_wb.reshape(N_DEV, 3, BR_W, -1), (1, 2, 0, 3)).reshape(3, BR_W, D)
    wo = g_wo.reshape(-1, D)
    o_f, st_f = _gla_fwd(proj, lbl4, OFF_FF, False, "gla_fwd_f")
    o_b, st_b = _gla_fwd(proj, lbl4, OFF_FB, True, "gla_fwd_b")
    y_hg = _hgrn_post_fwd(o_f, o_b, proj, hgrn_norm_g)
    q_cat, k_cat, v_mla, vt_mla, cqn, ckvn = _mla_up(proj, cf, sf, mla_g_cq, mla_g_ckv, wuq_p, wukv)
    y_mla, ot, lse, g_wg, g_wu = _mla_attn_fwd(q_cat, k_cat, vt_mla, _Xchg(big_wb[6:8], False))
    wg_t, wu_t = g_wg.reshape(F, D), g_wu.reshape(F, D)
    memb = mem[0].astype(CDT)
    (memkv,) = _fused_mm("mem_kv", "nn", [[(memb, 0, wmkv, 0)]], Mm, 2 * MEM_W, D, Mm, _tile(2 * MEM_W, 512), D,
                         [(2 * MEM_W, CDT, 0, None)], ident)
    y_mem = _mem_attn_fwd(proj, memkv)
    ys = [y_hg, y_mla, y_mem]
    tnD = _tile(D, 1024, OFF_GATE)

    def epi_branch(accs, tiles, rows):
        return [_sigmoid(tiles[0]) * accs[0] + _sigmoid(tiles[1]) * accs[1] + _sigmoid(tiles[2]) * accs[2]], []

    (merged,) = _fused_mm("branch_fwd", "nn", [[(ys[b], 0, wb[b], 0)] for b in range(3)], S, D, BR_W, tm512, tnD, BR_W,
                          [(D, CDT, 0, None)], epi_branch, tiles=[(proj, OFF_GATE + b * D) for b in range(3)])

    def epi_ln1(accs, tiles, rows):
        r1v = ALPHA * tiles[0] + accs[0]
        return [r1v, _ln_stats(r1v)[0] * rows[0] + rows[1]], []

    r1, h1b = _fused_mm("wo_ln1", "nn", [[(merged, 0, wo, 0)]], S, D, D, tm512, D, D,
                        [(D, F32, 0, None), (D, CDT, 0, None)], epi_ln1, tiles=[(h0, 0)], rows=[(ln1_g, 0), (ln1_b, 0)])
    tnF = _tile(F, 512)

    def epi_up(accs, tiles, rows):
        gp, up = accs
        return [gp, up, gp * _sigmoid(gp) * up], []

    tm1k, ms1k = _rtile(S, 1024), (2 if S % 2048 == 0 else 1)
    gpb, upb, act, g_wd = _fused_mm("ffn_up", "nt", [[(h1b, 0, wg_t, 0)], [(h1b, 0, wu_t, 0)]], S, F, D, tm1k, tnF, D,
                                    [(F, CDT, 0, None)] * 3, epi_up, msplit=ms1k, xchg=_Xchg(big_wb[8:9], False))
    wd = g_wd.reshape(-1, D)

    def epi_down(accs, tiles, rows):
        g1, b1, g2, b2 = rows
        h1 = _ln_stats(tiles[0])[0] * g1 + b1
        xh2, rstd2 = _ln_stats(ALPHA * h1 + accs[0])
        diff = xh2 * g2 + b2 - tiles[1]
        dh2 = diff * (1.0 / D)
        dr2v = _ln_bwd(dh2, xh2, rstd2, g2)
        return [dr2v, dr2v], [dh2 * xh2, dh2, diff * diff * (0.5 / D)]

    acc_first = lambda epi: (lambda accs, tiles, rows: epi([tiles[0]], tiles[1:], rows))
    tm256 = _rtile(S, 256)
    (ff,) = _fused_mm("ffn_down", "nn", [[(act, 0, wd, 0)]], S, D, F, tm1k, _tile(D, 512), F, [(D, F32, 0, None)], ident,
                      msplit=ms1k)
    dr2, dr2b, dg2, db2, lossp = _fused_mm(
        "ffn_ln2_loss", "nn", [], S, D, 1, tm256, D, 1, [(D, F32, 0, None), (D, CDT, 0, None)], acc_first(epi_down),
        tiles=[(ff, 0), (r1, 0), (tgt, 0)], rows=[(ln1_g, 0), (ln1_b, 0), (ln2_g, 0), (ln2_b, 0)], n_racc=3)

    def epi_dact(accs, tiles, rows):
        da, gp, up = accs[0], tiles[0].astype(F32), tiles[1].astype(F32)
        s = _sigmoid(gp)
        return [da * up * (s * (1.0 + gp * (1.0 - s))), da * (gp * s)], []

    dgp, dup = _fused_mm("ffn_dact", "nt", [[(dr2b, 0, wd, 0)]], S, F, D, tm1k, tnF, D, [(F, CDT, 0, None)] * 2,
                         epi_dact, tiles=[(gpb, 0), (upb, 0)], msplit=2 * ms1k)
    tkS = _rtile(S, 2048)
    (d_wd,) = _fused_mm("dw_down", "tn", [[(act, 0, dr2b, 0)]], F, D, S, tnF, D, tkS, [(D, CDT, 0, None)], ident)
    d_wg_t, d_wu_t = _fused_mm("dw_gate_up", "tn", [[(dgp, 0, h1b, 0)], [(dup, 0, h1b, 0)]], F, D, S, tnF, _tile(D, 1024),
                               tkS, [(D, CDT, 0, None)] * 2, lambda accs, tiles, rows: (accs, []))

    def epi_dh1(accs, tiles, rows):
        dh1 = accs[0] + ALPHA * tiles[0]
        xh1, rstd1 = _ln_stats(tiles[1])
        dr1v = _ln_bwd(dh1, xh1, rstd1, rows[0])
        return [dr1v, dr1v], [dh1 * xh1, dh1]

    rows8 = lambda dw: dw.reshape(N_DEV, -1, dw.shape[-1])
    (dh1_acc,) = _fused_mm("dh1", "nn", [[(dgp, 0, wg_t, 0), (dup, 0, wu_t, 0)]], S, D, F, tm512, _tile(D, 512), F,
                           [(D, F32, 0, None)], ident)
    dr1, dr1b, dg1, db1 = _fused_mm(
        "dh1_ln1", "nn", [], S, D, 1, tm256, D, 1, [(D, F32, 0, None), (D, CDT, 0, None)], acc_first(epi_dh1),
        tiles=[(dh1_acc, 0), (dr2, 0), (r1, 0)], rows=[(ln1_g, 0)], n_racc=2)
    (dmerged,) = _fused_mm("dmerged", "nt", [[(dr1b, 0, wo, 0)]], S, D, D, tm512, D, D, [(D, CDT, 0, None)], ident)
    (d_wo,) = _fused_mm("dw_o", "tn", [[(merged, 0, dr1b, 0)]], D, D, S, _tile(D, 512), D, tkS, [(D, CDT, 0, None)], ident)

    def epi_dbranch(accs, tiles, rows):
        dm, s = tiles[0].astype(F32), _sigmoid(tiles[1])
        return [dm * s, dm * accs[0] * s * (1.0 - s)], []

    dproj = None
    d_wbs, dys = [], []
    for b in range(3):
        du, dproj = _fused_mm(f"branch_bwd{b}", "nn", [[(ys[b], 0, wb[b], 0)]], S, D, BR_W, tm512, tnD, BR_W,
                              [(D, CDT, 0, None), (PW, CDT, OFF_GATE + b * D, dproj)], epi_dbranch,
                              tiles=[(dmerged, 0), (proj, OFF_GATE + b * D)])
        (dwb,) = _fused_mm(f"dw_branch{b}", "tn", [[(ys[b], 0, du, 0)]], BR_W, D, S, _tile(BR_W, 512), D, tkS,
                           [(D, CDT, 0, None)], ident)
        (dyb,) = _fused_mm(f"dy_branch{b}", "nt", [[(du, 0, wb[b], 0)]], S, BR_W, D, tm512, BR_W, D,
                           [(BR_W, F32 if b == 0 else CDT, 0, None)], ident)
        d_wbs.append(dwb)
        dys.append(dyb)
    dy_hg, dy_mla, dy_mem = dys

    dproj, dk_mem, dv_mem = _mem_attn_bwd(dy_mem, proj, memkv, dproj)
    dkv_mem = jnp.concatenate([dk_mem, dv_mem], axis=1).astype(CDT)
    (d_wmkv,) = _fused_mm("dw_memkv", "tn", [[(memb, 0, dkv_mem, 0)]], D, 2 * MEM_W, Mm, _tile(D, 512), 2 * MEM_W, Mm,
                          [(2 * MEM_W, CDT, 0, None)], ident)

    delta = _mla_delta(dy_mla, ot)
    lse_b = lse.reshape(delta.shape)
    dk_cat, dv_h, dq_cat, r_wg, r_wu, r_wd = _mla_attn_bwd(
        q_cat, k_cat, v_mla, dy_mla, lse_b, delta, _Xchg([rows8(d_wg_t), rows8(d_wu_t), rows8(d_wd)], True))
    dproj, dqp, dkvp, dgq, dgkv = _mla_up_bwd(dq_cat, dk_cat, dv_h, proj, cf, sf, mla_g_cq, mla_g_ckv, wuq_p, wukv, dproj)
    heads_major = lambda dw: jnp.transpose(dw.reshape(MLA_RANK, MLA_HEADS, -1), (1, 0, 2))
    (d_wuq_all,) = _fused_mm("dw_uq", "tn", [[(cqn, 0, dqp, 0)]], MLA_RANK, dqp.shape[1], S, MLA_RANK,
                             _tile(dqp.shape[1], 1536), tkS, [(dqp.shape[1], F32, 0, None)], ident)
    (d_wukv_all,) = _fused_mm("dw_ukv", "tn", [[(ckvn, 0, dkvp, 0)]], MLA_RANK, dkvp.shape[1], S, MLA_RANK,
                              _tile(dkvp.shape[1], 1024), tkS, [(dkvp.shape[1], CDT, 0, None)], ident)
    d_wuq_p, d_wukv = heads_major(d_wuq_all), heads_major(d_wukv_all)
    d_wuq = jnp.concatenate([d_wuq_p[..., :MLA_NOPE],
                             d_wuq_p[..., LANE:LANE + MLA_ROPE] + _unrot(d_wuq_p[..., 2 * LANE:2 * LANE + MLA_ROPE])],
                            axis=-1).astype(CDT)

    do_hg, dproj, dng = _hgrn_post_bwd(dy_hg, o_f, o_b, proj, hgrn_norm_g, dproj)
    dproj, dq1, di1, dl_f = _gla_bwd(proj, lbl4, OFF_FF, False, do_hg, st_f, dproj, None, "gla_bwd_f")
    dproj, _, _, dl_b = _gla_bwd(proj, lbl4, OFF_FB, True, do_hg, st_b, dproj, (dq1, di1), "gla_bwd_b")

    def epi_dh0(accs, tiles, rows):
        dh0 = accs[0] + ALPHA * tiles[0]
        xh, rstd = _ln_stats(tiles[1])
        return [_ln_bwd(dh0, xh, rstd, rows[0])], [dh0 * xh, dh0]

    d_wb = jnp.transpose(jnp.stack(d_wbs).reshape(3, BR_W, N_DEV, -1), (2, 0, 1, 3)).reshape(N_DEV, 3 * BR_W, -1)
    d_win_pt, r_uq, r_ukv, r_mkv, r_wb, r_wo = _fused_mm(
        "dw_in", "tn", [[(dproj, 0, h0b, 0)]], PW, D, S, _tile(PW, 1536), _tile(D, 1024), tkS, [(D, CDT, 0, None)], ident,
        xchg=_Xchg([d_wuq, d_wukv, rows8(d_wmkv), d_wb, rows8(d_wo)], True))
    d_kr = (d_win_pt[KR:KR + MLA_ROPE].astype(F32) + _unrot(d_win_pt[KR + LANE:KR + LANE + MLA_ROPE].astype(F32), 0)).astype(CDT)
    d_win_t = jnp.concatenate([d_win_pt[:OFF_FB], d_win_pt[OFF_FF:OFF_G], d_win_pt[OFF_FB:OFF_FF],
                               d_win_pt[OFF_G:OFF_QM], d_kr, d_win_pt[OFF_QM:KR]], axis=0)
    grad_x, dge, dbe, r_in = _fused_mm(
        "dh0_ln_emb", "nn", [[(dproj, 0, win_pt, 0)]], S, D, PW, tm512, D, _tile(PW, 1536), [(D, F32, 0, None)], epi_dh0,
        tiles=[(dr1, 0), (x2, 0)], rows=[(row(ln_emb_g), 0)], n_racc=2, xchg=_Xchg([rows8(d_win_t)], True))

    recv = [r_in, r_uq, r_ukv, r_mkv, r_wb, r_wo, r_wg, r_wu, r_wd]
    names = ["w_in", "w_uq", "w_ukv", "mem_w_kv", "w_branch", "w_o", "w_gate", "w_up", "w_down"]
    big_out = [_adam_big("adam_" + nme, r, w, m_, v_) for nme, r, w, m_, v_ in zip(names, recv, big_w, big_m, big_v)]

    parts = [dge, dbe, dng, dgq, dgkv, dg1, db1, dg2, db2, dl_f, dl_b, lossp]
    widths = [p.shape[1] for p in parts]
    red = _small_allreduce(jnp.concatenate(parts, axis=1))[0]
    offs = [sum(widths[:i]) for i in range(len(widths))]
    rs = [red[o:o + w_] for o, w_ in zip(offs, widths)]
    g_le_g, g_le_b, g_ng, g_gq, g_gkv, g_l1g, g_l1b, g_l2g, g_l2b, g_dlf, g_dlb, g_loss = rs
    loss = jnp.sum(g_loss)
    g_ng = g_ng.reshape(HG_HEADS, HG_DK).sum(axis=0)
    dl0 = jnp.stack([g_dlf, g_dlb])
    g_lb_full = jnp.stack([dl0, -dl0], axis=1)
    lbw = hgrn_lb_logits.shape[2]
    g_lb = lax.dynamic_slice_in_dim(g_lb_full, me * lbw, lbw, axis=2)

    small_g = [g_le_g, g_le_b, g_lb, g_ng.reshape(1, -1), g_gq.reshape(1, -1), g_gkv.reshape(1, -1), g_l1g.reshape(1, -1),
               g_l1b.reshape(1, -1), g_l2g.reshape(1, -1), g_l2b.reshape(1, -1)]
    small_w = [ln_emb_g, ln_emb_b, hgrn_lb_logits, hgrn_norm_g, mla_g_cq, mla_g_ckv, ln1_g, ln1_b, ln2_g, ln2_b]
    small_m = [m_ln_emb_g, m_ln_emb_b, m_hgrn_lb_logits, m_hgrn_norm_g, m_mla_g_cq, m_mla_g_ckv, m_ln1_g, m_ln1_b, m_ln2_g, m_ln2_b]
    small_v = [v_ln_emb_g, v_ln_emb_b, v_hgrn_lb_logits, v_hgrn_norm_g, v_mla_g_cq, v_mla_g_ckv, v_ln1_g, v_ln1_b, v_ln2_g, v_ln2_b]
    small_g = [g.reshape(w.shape) for g, w in zip(small_g, small_w)]
    pack = lambda lst: jnp.concatenate([a.reshape(-1) for a in lst]).reshape(1, -1)
    s_delta, s_nm, s_nv = _adam_small(pack(small_g), pack(small_w), pack(small_m), pack(small_v))
    sizes = [w.size for w in small_w]
    soffs = [sum(sizes[:i]) for i in range(len(sizes))]
    unpack = lambda p: [p[0, o:o + n].reshape(w.shape) for o, n, w in zip(soffs, sizes, small_w)]
    s_delta, s_nm, s_nv = unpack(s_delta), unpack(s_nm), unpack(s_nv)

    def ordered(small, big):
        sm = list(small)
        big = [tp(b) if n in transposed else b for n, b in enumerate(big)]
        bg = [b.reshape(w.shape) for b, w in zip(big, [w_in, mla_w_uq, mla_w_ukv, mem_w_kv, w_branch, w_o, w_ffn_gate, w_ffn_up, w_ffn_down])]
        return [sm[0], sm[1], sm[2], bg[0], sm[3], sm[4], sm[5], bg[1], bg[2], bg[3], bg[4], bg[5], sm[6], sm[7], bg[6], bg[7], bg[8], sm[8], sm[9]]

    grads = ordered(small_g, [o[0] for o in big_out])
    deltas = ordered(s_delta, [o[1] for o in big_out])
    new_m = ordered(s_nm, [o[2] for o in big_out])
    new_v = ordered(s_nv, [o[3] for o in big_out])
    return (loss, grad_x[None], *grads, *deltas, *new_m, *new_v)
```

```python
import functools

import jax
import jax.numpy as jnp
from jax import lax
from jax.experimental import pallas as pl
from jax.experimental.pallas import tpu as pltpu

F32 = jnp.float32
CDT = jnp.bfloat16
MESH = pl.DeviceIdType.MESH
N_DEV = 8
V7X_VMEM_LIMIT = 60 * 1024 * 1024
LANE = 128
SUB = 8

HG_HEADS, HG_DK, HG_CHUNK = 8, 128, 64
HG_HPS = 8
HG_BWD_GROUP = 8
HG_W = HG_HEADS * HG_DK
MLA_HEADS, MLA_RANK, MLA_NOPE, MLA_ROPE, MLA_V = 8, 512, 128, 64, 128
MLA_QK = MLA_NOPE + MLA_ROPE
MLA_SCALE = MLA_QK ** -0.5
MLA_QSCALE = MLA_SCALE * 1.4426950408889634
VT_ROWS = LANE + 16
MLA_TQ = 1024
MLA_BWD_SLABS = 1
MLA_TQ_FWD = 2048
MLA_FWD_SLABS = 8
MLA_W = MLA_HEADS * MLA_V
MEM_HEADS, MEM_HD = 4, 256
MEM_W = MEM_HEADS * MEM_HD
BR_W = 1024
ROPE_THETA = 10000.0
ALPHA = 2.0 ** 0.25
LN_EPS = 1e-5
RMS_EPS = 1e-6
ADAM_LR, ADAM_B1, ADAM_B2, ADAM_EPS, ADAM_WD, ADAM_STEP = 0.001, 0.9, 0.999, 1e-08, 0.01, 10
ADAM_BLOCK_ELEMS = 256 * 1024

OFF_Q, OFF_I, OFF_FB, OFF_FF, OFF_G = 0, 1024, 2048, 3072, 4096
OFF_CQ, OFF_CKV, OFF_QM, OFF_GATE = 5120, 5632, 6144, 7168
KR_PAD = 512


def _cparams(n_grid, side_effects=False):
    return pltpu.CompilerParams(dimension_semantics=("arbitrary",) * n_grid, vmem_limit_bytes=V7X_VMEM_LIMIT,
                                has_side_effects=side_effects)


def _tile(n, pref, *offsets):
    if n <= pref and all(o % n == 0 for o in offsets):
        return n
    t = (min(pref, n) // LANE) * LANE
    while t >= LANE:
        if n % t == 0 and all(o % t == 0 for o in offsets):
            return t
        t -= LANE
    raise ValueError(f"no tile for {n} {pref} {offsets}")


def _rtile(n, pref):
    if n <= pref:
        return n
    t = (pref // SUB) * SUB
    while t >= SUB:
        if n % t == 0:
            return t
        t -= SUB
    raise ValueError(f"no row tile for {n} {pref}")


def _dot(a, b, dims):
    return lax.dot_general(a.astype(CDT), b.astype(CDT), (dims, ((), ())), preferred_element_type=F32)


def _nn(a, b):
    return _dot(a, b, ((1,), (0,)))


def _nt(a, b):
    return _dot(a, b, ((1,), (1,)))


def _tn(a, b):
    return _dot(a, b, ((0,), (0,)))


_DOTS = {"nn": _nn, "nt": _nt, "tn": _tn}


def _sigmoid(x):
    return 1.0 / (1.0 + jnp.exp(-x))


def _rowsum8(v):
    r, w = v.shape
    return v.reshape(r // SUB, SUB, w).sum(axis=0)


def _my_place():
    x, y, c = lax.axis_index("x"), lax.axis_index("y"), lax.axis_index("c")
    return x, y, c, 4 * x + 2 * y + c


def _peer(x, y, c, kk):
    px = 1 - x if kk & 4 else x
    py = 1 - y if kk & 2 else y
    pc = 1 - c if kk & 1 else c
    return (px, py, pc), 4 * px + 2 * py + pc


class _Xchg:
    def __init__(self, arrs, scatter):
        self.arrs, self.scatter, self.n = list(arrs), scatter, len(arrs)
        hbm = pl.BlockSpec(memory_space=pl.ANY)
        self.specs = [hbm] * self.n
        self.out_shape = [jax.ShapeDtypeStruct(((N_DEV,) + a.shape[1:]) if scatter else ((N_DEV,) + a.shape), a.dtype)
                          for a in self.arrs]
        ncp = self.n * (N_DEV - 1)
        self.scratch = [pltpu.SemaphoreType.DMA((ncp,)), pltpu.SemaphoreType.DMA((ncp,)), pltpu.SemaphoreType.DMA((self.n,))]

    def _copies(self, ins, outs, send, recv, loc):
        x, y, c, me = _my_place()
        copies = []
        for w in range(self.n):
            copies.append(pltpu.make_async_copy(ins[w].at[me] if self.scatter else ins[w], outs[w].at[me], loc.at[w]))
            for kk in range(1, N_DEV):
                peer, pid = _peer(x, y, c, kk)
                s = w * (N_DEV - 1) + kk - 1
                copies.append(pltpu.make_async_remote_copy(
                    src_ref=ins[w].at[pid] if self.scatter else ins[w], dst_ref=outs[w].at[me],
                    send_sem=send.at[s], recv_sem=recv.at[s], device_id=peer, device_id_type=MESH))
        return copies

    def start(self, ins, outs, sems):
        for cp in self._copies(ins, outs, *sems):
            cp.start()

    def wait(self, ins, outs, sems):
        for cp in self._copies(ins, outs, *sems):
            cp.wait()


def _all_gather_two_level(name, arrs):
    n = len(arrs)
    NC = N_DEV - 1

    def body(*refs):
        ins, outs = refs[:n], refs[n:2 * n]
        send, recv, loc = refs[2 * n:]
        x, y, c, me = _my_place()
        sibling = (x, y, 1 - c)
        chips = [(1 - x, y), (x, 1 - y), (1 - x, 1 - y)]
        slot = lambda px, py, pc: 4 * px + 2 * py + pc

        def copy(w, k, block, to, src=None):
            dst = outs[w].at[slot(*block)]
            return pltpu.make_async_remote_copy(src_ref=dst if src is None else src, dst_ref=dst,
                                                send_sem=send.at[w * NC + k], recv_sem=recv.at[w * NC + k],
                                                device_id=to, device_id_type=MESH)

        mine = [pltpu.make_async_copy(ins[w], outs[w].at[me], loc.at[w]) for w in range(n)]
        for cp in mine:
            cp.start()
        first = []
        for w in range(n):
            first.append(copy(w, 0, (x, y, c), sibling, src=ins[w]))
            first += [copy(w, 1 + j, (x, y, c), (*chip, c), src=ins[w]) for j, chip in enumerate(chips)]
        for cp in first:
            cp.start()
        passed = []
        for j, chip in enumerate(chips):
            for w in range(n):
                copy(w, 1 + j, (*chip, c), (x, y, c)).wait_recv()
                fwd = copy(w, 4 + j, (*chip, c), sibling)
                fwd.start()
                passed.append(fwd)
        for w in range(n):
            copy(w, 0, sibling, (x, y, c)).wait_recv()
            for j, chip in enumerate(chips):
                copy(w, 4 + j, (*chip, 1 - c), (x, y, c)).wait_recv()
        for cp in first + passed:
            cp.wait_send()
        for cp in mine:
            cp.wait()

    hbm = pl.BlockSpec(memory_space=pl.ANY)
    return pl.pallas_call(
        body,
        name=name,
        in_specs=[hbm] * n,
        out_specs=[hbm] * n,
        out_shape=[jax.ShapeDtypeStruct((N_DEV,) + a.shape, a.dtype) for a in arrs],
        scratch_shapes=[pltpu.SemaphoreType.DMA((n * NC,)), pltpu.SemaphoreType.DMA((n * NC,)), pltpu.SemaphoreType.DMA((n,))],
        compiler_params=pltpu.CompilerParams(has_side_effects=True),
    )(*arrs)


def _fused_mm(name, mode, groups, M, N, K, tm, tn, tk, outs, epi, tiles=(), rows=(), n_racc=0, xchg=None, msplit=1):
    ni, nj, nk = M // tm, N // tn, K // tk
    assert M % tm == 0 and N % tn == 0 and K % tk == 0, (name, M, N, K, tm, tn, tk)
    assert n_racc == 0 or nj == 1
    assert msplit == 1 or (nk == 1 and n_racc == 0 and tm % (16 * msplit) == 0)
    dot = _DOTS[mode] if groups else None
    ins, in_specs = [], []
    for g in groups:
        for a, a_off, b, b_off in g:
            if mode == "tn":
                assert a_off % tm == 0
                in_specs.append(pl.BlockSpec((tk, tm), lambda i, j, k, o=a_off // tm: (k, i + o)))
            else:
                assert a_off % tk == 0
                in_specs.append(pl.BlockSpec((tm, tk), lambda i, j, k, o=a_off // tk: (i, k + o)))
            ins.append(a)
            if mode == "nt":
                assert b_off % tk == 0
                in_specs.append(pl.BlockSpec((tn, tk), lambda i, j, k, o=b_off // tk: (j, k + o)))
            else:
                assert b_off % tn == 0
                in_specs.append(pl.BlockSpec((tk, tn), lambda i, j, k, o=b_off // tn: (k, j + o)))
            ins.append(b)
    for arr, off in tiles:
        assert off % tn == 0
        ins.append(arr)
        in_specs.append(pl.BlockSpec((tm, tn), lambda i, j, k, o=off // tn: (i, j + o)))
    for arr, off in rows:
        assert off % tn == 0
        ins.append(arr)
        in_specs.append(pl.BlockSpec((1, tn), lambda i, j, k, o=off // tn: (0, j + o)))
    aliases = {}
    out_shape, out_specs = [], []
    for oi, (width, dtype, off, alias) in enumerate(outs):
        assert off % tn == 0
        if alias is not None:
            aliases[len(ins)] = oi
            ins.append(alias)
            in_specs.append(pl.BlockSpec(memory_space=pl.ANY))
        out_shape.append(jax.ShapeDtypeStruct((M, width), dtype))
        out_specs.append(pl.BlockSpec((tm, tn), lambda i, j, k, o=off // tn: (i, j + o)))
    for _ in range(n_racc):
        out_shape.append(jax.ShapeDtypeStruct((SUB, N), F32))
        out_specs.append(pl.BlockSpec((SUB, tn), lambda i, j, k: (0, 0)))
    n_alias = len(aliases)
    n_pairs = [len(g) for g in groups]
    use_scratch = nk > 1
    scratch = [pltpu.VMEM((tm, tn), F32) for _ in groups] if use_scratch else []
    nx = 0
    if xchg is not None:
        nx = xchg.n
        ins += xchg.arrs
        in_specs += xchg.specs
        out_shape += xchg.out_shape
        out_specs += xchg.specs
        scratch += xchg.scratch

    def body(*refs):
        it = iter(refs)
        pair_refs = [[(next(it), next(it)) for _ in range(n)] for n in n_pairs]
        tile_refs = [next(it) for _ in tiles]
        row_refs = [next(it) for _ in rows]
        for _ in range(n_alias):
            next(it)
        x_in = [next(it) for _ in range(nx)]
        out_refs = [next(it) for _ in outs]
        racc_refs = [next(it) for _ in range(n_racc)]
        x_out = [next(it) for _ in range(nx)]
        acc_refs = [next(it) for _ in groups] if use_scratch else []
        x_sems = list(it)
        i, j, k = pl.program_id(0), pl.program_id(1), pl.program_id(2)
        if nx:
            @pl.when((i == 0) & (j == 0) & (k == 0))
            def _():
                xchg.start(x_in, x_out, x_sems)

        def products():
            res = []
            for prs in pair_refs:
                s = None
                for a_ref, b_ref in prs:
                    d = dot(a_ref[...], b_ref[...])
                    s = d if s is None else s + d
                res.append(s)
            return res

        def finish(accs):
            out_v, racc_v = epi(accs, [t[...] for t in tile_refs], [r[...] for r in row_refs])
            for o_ref, v in zip(out_refs, out_v):
                o_ref[...] = v.astype(o_ref.dtype)
            for r_ref, v in zip(racc_refs, racc_v):
                part = _rowsum8(v)

                @pl.when(i == 0)
                def _():
                    r_ref[...] = part

                @pl.when(i > 0)
                def _():
                    r_ref[...] += part

        if not use_scratch and msplit > 1:
            ts = tm // msplit
            for s in range(msplit):
                rs = pl.ds(s * ts, ts)
                accs = []
                for prs in pair_refs:
                    acc = None
                    for a_ref, b_ref in prs:
                        dd = dot(a_ref[:, rs] if mode == "tn" else a_ref[rs, :], b_ref[...])
                        acc = dd if acc is None else acc + dd
                    accs.append(acc)
                out_v, _ = epi(accs, [t[rs, :] for t in tile_refs], [r[...] for r in row_refs])
                for o_ref, v in zip(out_refs, out_v):
                    o_ref[rs, :] = v.astype(o_ref.dtype)
        elif not use_scratch:
            finish(products())
        else:
            @pl.when(k == 0)
            def _():
                for acc in acc_refs:
                    acc[...] = jnp.zeros_like(acc)

            for acc, p in zip(acc_refs, products()):
                acc[...] += p

            @pl.when(k == nk - 1)
            def _():
                finish([acc[...] for acc in acc_refs])

        if nx:
            @pl.when((i == ni - 1) & (j == nj - 1) & (k == nk - 1))
            def _():
                xchg.wait(x_in, x_out, x_sems)

    res = pl.pallas_call(
        body,
        name=name,
        grid=(ni, nj, nk),
        in_specs=in_specs,
        out_specs=out_specs,
        out_shape=out_shape,
        scratch_shapes=scratch,
        input_output_aliases=aliases,
        compiler_params=_cparams(3, side_effects=nx > 0),
    )(*ins)
    return res


def _ln_stats(r):
    mu = jnp.mean(r, axis=-1, keepdims=True)
    xc = r - mu
    var = jnp.mean(xc * xc, axis=-1, keepdims=True)
    rstd = lax.rsqrt(var + LN_EPS)
    return xc * rstd, rstd


def _ln_bwd(dh, xhat, rstd, g):
    dxh = dh * g
    m1 = jnp.mean(dxh, axis=-1, keepdims=True)
    m2 = jnp.mean(dxh * xhat, axis=-1, keepdims=True)
    return rstd * (dxh - m1 - xhat * m2)


def _split3(x):
    hi = x.astype(CDT)
    r1 = x - hi.astype(F32)
    mid = r1.astype(CDT)
    lo = (r1 - mid.astype(F32)).astype(CDT)
    return hi, mid, lo


def _tri_matmul(tri, x):
    hi, mid, lo = _split3(x)
    return _nn(tri, hi) + _nn(tri, mid) + _nn(tri, lo)


def _dot3(dot, a, b):
    a_hi, b_hi = a.astype(CDT), b.astype(CDT)
    a_lo = (a - a_hi.astype(F32)).astype(CDT)
    b_lo = (b - b_hi.astype(F32)).astype(CDT)
    return dot(a_hi, b_hi) + dot(a_hi, b_lo) + dot(a_lo, b_hi)


def _gla_masks(reverse):
    C = HG_CHUNK
    r = lax.broadcasted_iota(jnp.int32, (C, C), 0)
    c = lax.broadcasted_iota(jnp.int32, (C, C), 1)
    keep = (c >= r) if reverse else (r >= c)
    return keep


def _m(fn, *lists):
    return [fn(*args) for args in zip(*lists)]


def _gla_chunk_fwd(qraw, fraw, lb, keep, reverse):
    C = HG_CHUNK
    end = 0 if reverse else C - 1
    tri = jnp.where(keep, 1.0, 0.0).astype(CDT)
    sq = _m(_sigmoid, qraw)
    q = _m(lambda x, s: x * s, qraw, sq)
    sg = _m(_sigmoid, fraw)
    f = _m(lambda l_, s: l_ + (1.0 - l_) * s, lb, sg)
    k = _m(lambda x: 1.0 - x, f)
    g = _m(jnp.log, f)
    b = _m(lambda x: _tri_matmul(tri, x), g)
    b_end = _m(lambda x: x[end:end + 1, :], b)
    b_mid = _m(lambda x: x[C // 2:C // 2 + 1, :], b)
    eq = _m(lambda x, m_: jnp.exp(x - m_), b, b_mid)
    ek = _m(lambda x, m_: jnp.exp(m_ - x), b, b_mid)
    eb = _m(jnp.exp, b)
    e2 = _m(lambda x, e_: jnp.exp(e_ - x), b, b_end)
    e_end = _m(jnp.exp, b_end)
    qt = _m(lambda x, e_: x * e_, q, eq)
    kt = _m(lambda x, e_: x * e_, k, ek)
    qs = _m(lambda x, e_: (x * e_).astype(CDT), q, eb)
    k2 = _m(lambda x, e_: (x * e_).astype(CDT), k, e2)
    a = _m(lambda x, y: jnp.where(keep, _dot3(_nt, x, y), 0.0).astype(CDT), qt, kt)
    return dict(sq=sq, q=q, sg=sg, f=f, k=k, eq=eq, ek=ek, eb=eb, e2=e2, e_end=e_end, qt=qt, kt=kt, qs=qs, k2=k2, a=a)


def _gla_fwd(proj, lbl4, f_off, reverse, name):
    S = proj.shape[0]
    C = HG_CHUNK
    R = _rtile(S, 512)
    cpb, nblk = R // C, S // R
    d = 1 if reverse else 0
    blk_map = (lambda b: nblk - 1 - b) if reverse else (lambda b: b)

    W = HG_HPS * HG_DK

    def body(q_ref, i_ref, f_ref, lb_ref, o_ref, st_ref, s_scr):
        @pl.when(pl.program_id(1) == 0)
        def _():
            s_scr[...] = jnp.zeros_like(s_scr)

        l = lb_ref[...]
        lbs = _sigmoid(l[2 * d:2 * d + 1, :] - l[2 * d + 1:2 * d + 2, :])
        keep = _gla_masks(reverse)
        heads = list(range(HG_HPS))
        css = [pl.ds(hh * HG_DK, HG_DK) for hh in heads]
        lb = [lbs[:, hh * HG_DK:(hh + 1) * HG_DK] for hh in heads]
        for cc in range(cpb):
            c = cpb - 1 - cc if reverse else cc
            sl = pl.ds(c * C, C)
            v = [i_ref[sl, cs] for cs in css]
            t = _gla_chunk_fwd([q_ref[sl, cs] for cs in css], [f_ref[sl, cs] for cs in css], lb, keep, reverse)
            st = [s_scr[hh] for hh in heads]
            o = _m(lambda qs, s_, a, v_: _nt(qs, s_) + _nn(a, v_), t["qs"], st, t["a"], v)
            new = _m(lambda e_, s_, v_, k2: e_ * s_ + _tn(v_, k2), t["e_end"], st, v, t["k2"])
            for hh in heads:
                st_ref[c, hh] = st[hh]
                o_ref[sl, css[hh]] = o[hh]
                s_scr[hh] = new[hh]

    col = lambda off: (lambda h, b: (blk_map(b), off // W + h))
    return pl.pallas_call(
        body,
        name=name,
        grid=(HG_HEADS // HG_HPS, nblk),
        in_specs=[
            pl.BlockSpec((R, W), col(OFF_Q)),
            pl.BlockSpec((R, W), col(OFF_I)),
            pl.BlockSpec((R, W), col(f_off)),
            pl.BlockSpec((4, W), lambda h, b: (0, h)),
        ],
        out_specs=[
            pl.BlockSpec((R, W), lambda h, b: (blk_map(b), h)),
            pl.BlockSpec((cpb, HG_HPS, HG_DK, HG_DK), lambda h, b: (blk_map(b), h, 0, 0)),
        ],
        out_shape=[
            jax.ShapeDtypeStruct((S, HG_W), F32),
            jax.ShapeDtypeStruct((S // C, HG_HEADS, HG_DK, HG_DK), F32),
        ],
        scratch_shapes=[pltpu.VMEM((HG_HPS, HG_DK, HG_DK), F32)],
        compiler_params=_cparams(2),
    )(proj, proj, proj, lbl4)


def _gla_bwd(proj, lbl4, f_off, reverse, do, states, dproj, prev, name):
    S = proj.shape[0]
    PW = proj.shape[1]
    C = HG_CHUNK
    R = _rtile(S, 512)
    cpb, nblk = R // C, S // R
    d = 1 if reverse else 0
    blk_map = (lambda b: b) if reverse else (lambda b: nblk - 1 - b)
    final = prev is not None

    if final:
        assert HG_HPS == HG_HEADS and (OFF_Q, OFF_I, f_off) == (0, HG_W, 2 * HG_W)

    def body(*refs):
        if final:
            q_ref, i_ref, f_ref, lb_ref, do_ref, st_ref, pq_ref, pi_ref, _dp, o3_ref, dl_ref, ds_scr = refs
            dq_ref = di_ref = df_ref = o3_ref
        else:
            q_ref, i_ref, f_ref, lb_ref, do_ref, st_ref, dq_ref, di_ref, df_ref, dl_ref, ds_scr = refs
        out_off = (OFF_Q, OFF_I, f_off) if final else (0, 0, 0)
        blk = pl.program_id(1)

        @pl.when(blk == 0)
        def _():
            ds_scr[...] = jnp.zeros_like(ds_scr)
            dl_ref[...] = jnp.zeros_like(dl_ref)

        l = lb_ref[...]
        lbs = _sigmoid(l[2 * d:2 * d + 1, :] - l[2 * d + 1:2 * d + 2, :])
        keep = _gla_masks(reverse)
        keep_t = _gla_masks(not reverse)
        tri_t = jnp.where(keep_t, 1.0, 0.0).astype(CDT)
        end = 0 if reverse else C - 1
        is_end = lax.broadcasted_iota(jnp.int32, (C, HG_DK), 0) == end
        dl_all = [jnp.zeros((SUB, HG_DK), F32) for _ in range(HG_HPS)]
        gsz = HG_BWD_GROUP
        for cc, heads in [(cc, list(range(g0, g0 + gsz))) for cc in range(cpb) for g0 in range(0, HG_HPS, gsz)]:
            css = [pl.ds(hh * HG_DK, HG_DK) for hh in heads]
            lb = [lbs[:, hh * HG_DK:(hh + 1) * HG_DK] for hh in heads]
            dl_acc = [dl_all[hh] for hh in heads]
            c = cc if reverse else cpb - 1 - cc
            sl = pl.ds(c * C, C)
            qraw = [q_ref[sl, cs] for cs in css]
            v = [i_ref[sl, cs] for cs in css]
            t = _gla_chunk_fwd(qraw, [f_ref[sl, cs] for cs in css], lb, keep, reverse)
            dob = [do_ref[sl, cs].astype(CDT) for cs in css]
            vb = _m(lambda x: x.astype(CDT), v)
            st = [st_ref[c, hh] for hh in heads]
            ds = [ds_scr[hh] for hh in heads]
            dsb = _m(lambda x: x.astype(CDT), ds)
            d_qs = _m(_nn, dob, st)
            d_a = _m(lambda x, y: jnp.where(keep, _nt(x, y), 0.0), dob, vb)
            d_qt = _m(lambda x, y: _dot3(_nn, x, y), d_a, t["kt"])
            d_kt = _m(lambda x, y: _dot3(_tn, x, y), d_a, t["qt"])
            d_v = _m(lambda a, x, k2, s_: _tn(a, x) + _nt(k2, s_), t["a"], dob, t["k2"], dsb)
            d_k2 = _m(_nn, vb, dsb)
            d_e = _m(lambda s_, x: jnp.sum(s_ * x, axis=0, keepdims=True), st, ds)
            new_ds = _m(lambda e_, x, y, qs: e_ * x + _tn(y, qs), t["e_end"], ds, dob, t["qs"])
            dq = _m(lambda a, ea, b_, eb_: a * ea + b_ * eb_, d_qt, t["eq"], d_qs, t["eb"])
            dk = _m(lambda a, ea, b_, eb_: a * ea + b_ * eb_, d_kt, t["ek"], d_k2, t["e2"])
            db_end = _m(lambda x, k_, e2, de, ee: jnp.sum(x * (k_ * e2), axis=0, keepdims=True) + de * ee,
                        d_k2, t["k"], t["e2"], d_e, t["e_end"])
            db = _m(lambda q_, dq_, k_, dk_, be: q_ * dq_ - k_ * dk_ + jnp.where(is_end, be, 0.0),
                    t["q"], dq, t["k"], dk, db_end)
            dg = _m(lambda x: _tri_matmul(tri_t, x), db)
            df = _m(lambda g_, f_, dk_: g_ / f_ - dk_, dg, t["f"], dk)
            dfraw = _m(lambda x, l_, s_: x * (1.0 - l_) * s_ * (1.0 - s_), df, lb, t["sg"])
            dl_acc = _m(lambda acc, x, s_: acc + _rowsum8(x * (1.0 - s_)), dl_acc, df, t["sg"])
            dqraw = _m(lambda x, s_, r: x * (s_ * (1.0 + r * (1.0 - s_))), dq, t["sq"], qraw)
            if final:
                dqraw = [x + pq_ref[sl, cs] for x, cs in zip(dqraw, css)]
                d_v = [x + pi_ref[sl, cs] for x, cs in zip(d_v, css)]
            for n, hh in enumerate(heads):
                dl_all[hh] = dl_acc[n]
                ds_scr[hh] = new_ds[n]
                for ref, off, val in zip((dq_ref, di_ref, df_ref), out_off, (dqraw[n], d_v[n], dfraw[n])):
                    ref[sl, pl.ds(off + hh * HG_DK, HG_DK)] = val.astype(ref.dtype)
        dl_ref[...] += jnp.concatenate(dl_all, axis=1) * (lbs * (1.0 - lbs))

    W = HG_HPS * HG_DK
    col = lambda off: (lambda h, b: (blk_map(b), off // W + h))
    blk = lambda: pl.BlockSpec((R, W), lambda h, b: (blk_map(b), h))
    ins = [proj, proj, proj, lbl4, do, states]
    in_specs = [
        pl.BlockSpec((R, W), col(OFF_Q)),
        pl.BlockSpec((R, W), col(OFF_I)),
        pl.BlockSpec((R, W), col(f_off)),
        pl.BlockSpec((4, W), lambda h, b: (0, h)),
        blk(),
        pl.BlockSpec((cpb, HG_HPS, HG_DK, HG_DK), lambda h, b: (blk_map(b), h, 0, 0)),
    ]
    dl_shape = jax.ShapeDtypeStruct((SUB, HG_W), F32)
    dl_spec = pl.BlockSpec((SUB, W), lambda h, b: (0, h))
    dp_shape = jax.ShapeDtypeStruct((S, PW), CDT)
    if final:
        ins += [prev[0], prev[1], dproj]
        in_specs += [blk(), blk(), pl.BlockSpec(memory_space=pl.ANY)]
        out_shape = [dp_shape, dl_shape]
        out_specs = [pl.BlockSpec((R, 3 * HG_W), lambda h, b: (blk_map(b), 0)), dl_spec]
        aliases = {8: 0}
    else:
        out_shape = [jax.ShapeDtypeStruct((S, HG_W), F32), jax.ShapeDtypeStruct((S, HG_W), F32), dp_shape, dl_shape]
        out_specs = [blk(), blk(), pl.BlockSpec((R, W), col(f_off)), dl_spec]
        aliases = {}
        if dproj is not None:
            ins += [dproj]
            in_specs += [pl.BlockSpec(memory_space=pl.ANY)]
            aliases = {6: 2}
    if (not final) and dproj is not None:
        def body_wrapped(*refs, _b=body):
            _b(*refs[:6], *refs[7:])
        kern = body_wrapped
    else:
        kern = body
    res = pl.pallas_call(
        kern,
        name=name,
        grid=(HG_HEADS // HG_HPS, nblk),
        in_specs=in_specs,
        out_specs=out_specs,
        out_shape=out_shape,
        scratch_shapes=[pltpu.VMEM((HG_HPS, HG_DK, HG_DK), F32)],
        input_output_aliases=aliases,
        compiler_params=_cparams(2),
    )(*ins)
    if final:
        return res[0], None, None, res[1]
    dq, di, dproj, dl = res
    return dproj, dq, di, dl


def _hgrn_post_fwd(o_f, o_b, proj, norm_g):
    S = o_f.shape[0]

    def epi(accs, tiles, rows):
        of, ob, graw = tiles
        ng = rows[0][:, :HG_DK]
        o = of + ob
        ys = []
        for h in range(HG_HEADS):
            oh = o[:, h * HG_DK:(h + 1) * HG_DK]
            rs = lax.rsqrt(jnp.mean(oh * oh, axis=-1, keepdims=True) + RMS_EPS)
            ys.append(oh * rs * ng * _sigmoid(graw[:, h * HG_DK:(h + 1) * HG_DK]))
        return [jnp.concatenate(ys, axis=1)], []

    tm = _rtile(S, 512)
    (y,) = _fused_mm("hgrn_post_fwd", "nn", [], S, HG_W, 1, tm, HG_W, 1, [(HG_W, CDT, 0, None)], epi,
                     tiles=[(o_f, 0), (o_b, 0), (proj, OFF_G)], rows=[(jnp.tile(norm_g, (1, HG_HEADS)), 0)])
    return y


def _hgrn_post_bwd(dy, o_f, o_b, proj, norm_g, dproj):
    S = o_f.shape[0]

    def epi(accs, tiles, rows):
        dyv, of, ob, graw = tiles
        ng = rows[0][:, :HG_DK]
        o = of + ob
        dos, dgs, dns = [], [], []
        for h in range(HG_HEADS):
            sl = slice(h * HG_DK, (h + 1) * HG_DK)
            oh, gh, dyh = o[:, sl], graw[:, sl], dyv[:, sl].astype(F32)
            rs = lax.rsqrt(jnp.mean(oh * oh, axis=-1, keepdims=True) + RMS_EPS)
            xh = oh * rs
            sg = _sigmoid(gh)
            dn = dyh * sg
            dgs.append(dyh * (xh * ng) * sg * (1.0 - sg))
            dns.append(dn * xh)
            dxh = dn * ng
            dos.append(rs * (dxh - xh * jnp.mean(dxh * xh, axis=-1, keepdims=True)))
        return [jnp.concatenate(dos, axis=1), jnp.concatenate(dgs, axis=1)], [jnp.concatenate(dns, axis=1)]

    tm = _rtile(S, 512)
    do, dproj, dn = _fused_mm("hgrn_post_bwd", "nn", [], S, HG_W, 1, tm, HG_W, 1,
                              [(HG_W, F32, 0, None), (dproj.shape[1], CDT, OFF_G, dproj)], epi,
                              tiles=[(dy, 0), (o_f, 0), (o_b, 0), (proj, OFF_G)],
                              rows=[(jnp.tile(norm_g, (1, HG_HEADS)), 0)], n_racc=1)
    return do, dproj, dn


def _copy_into(name, src, dst, off):
    S, W = src.shape
    tm = _rtile(S, 512)
    (dst,) = _fused_mm(name, "nn", [], S, W, 1, tm, W, 1, [(dst.shape[1], dst.dtype, off, dst)],
                       lambda accs, tiles, rows: ([tiles[0]], []), tiles=[(src, 0)])
    return dst


def _rms_stats(x):
    rs = lax.rsqrt(jnp.mean(x * x, axis=-1, keepdims=True) + RMS_EPS)
    return x * rs, rs


def _mla_up(proj, cf, sf, g_cq, g_ckv, wuq_p, wukv):
    S = proj.shape[0]
    tm = _rtile(S, 512)
    H = MLA_HEADS

    def body(cq_ref, ckv_ref, kr_ref, krot_ref, cf_ref, sf_ref, gq_ref, gkv_ref, wq_ref, wkv_ref,
             q_ref, k_ref, v_ref, vt_ref, cqn_ref, ckvn_ref):
        cqn = (_rms_stats(cq_ref[...])[0] * gq_ref[...]).astype(CDT)
        ckvn = (_rms_stats(ckv_ref[...])[0] * gkv_ref[...]).astype(CDT)
        cqn_ref[...] = cqn
        ckvn_ref[...] = ckvn
        cfv, sfv = cf_ref[...], sf_ref[...]
        k_roped = (kr_ref[...] * cfv + krot_ref[...] * sfv).astype(CDT)
        ones = jnp.ones((VT_ROWS - LANE, tm), CDT)
        for h in range(H):
            r = _nn(cqn, wq_ref[h]) * MLA_QSCALE
            q_ref[h, :, 0:LANE] = r[:, 0:LANE].astype(CDT)
            q_ref[h, :, LANE:2 * LANE] = (r[:, LANE:2 * LANE] * cfv + r[:, 2 * LANE:3 * LANE] * sfv).astype(CDT)
            kv = _nn(ckvn, wkv_ref[h])
            k_ref[h, :, 0:LANE] = kv[:, 0:LANE].astype(CDT)
            k_ref[h, :, LANE:2 * LANE] = k_roped
            vv = kv[:, LANE:2 * LANE]
            v_ref[h] = vv.astype(CDT)
            vt_ref[h, 0, 0:LANE, :] = vv.T.astype(CDT)
            vt_ref[h, 0, LANE:VT_ROWS, :] = ones

    PWb = proj.shape[1]
    kr_off = PWb - KR_PAD
    cspec = lambda off, w: pl.BlockSpec((tm, w), lambda i, o=off // w: (i, o))
    return pl.pallas_call(
        body,
        name="mla_up_fwd",
        grid=(S // tm,),
        in_specs=[
            cspec(OFF_CQ, MLA_RANK), cspec(OFF_CKV, MLA_RANK), cspec(kr_off, LANE), cspec(kr_off + LANE, LANE),
            pl.BlockSpec((tm, LANE), lambda i: (i, 0)), pl.BlockSpec((tm, LANE), lambda i: (i, 0)),
            pl.BlockSpec((1, MLA_RANK), lambda i: (0, 0)), pl.BlockSpec((1, MLA_RANK), lambda i: (0, 0)),
            pl.BlockSpec((H, MLA_RANK, 3 * LANE), lambda i: (0, 0, 0)),
            pl.BlockSpec((H, MLA_RANK, 2 * LANE), lambda i: (0, 0, 0)),
        ],
        out_specs=[
            pl.BlockSpec((H, tm, 2 * LANE), lambda i: (0, i, 0)),
            pl.BlockSpec((H, tm, 2 * LANE), lambda i: (0, i, 0)),
            pl.BlockSpec((H, tm, LANE), lambda i: (0, i, 0)),
            pl.BlockSpec((H, 1, VT_ROWS, tm), lambda i: (0, i, 0, 0)),
            pl.BlockSpec((tm, MLA_RANK), lambda i: (i, 0)),
            pl.BlockSpec((tm, MLA_RANK), lambda i: (i, 0)),
        ],
        out_shape=[
            jax.ShapeDtypeStruct((H, S, 2 * LANE), CDT), jax.ShapeDtypeStruct((H, S, 2 * LANE), CDT),
            jax.ShapeDtypeStruct((H, S, LANE), CDT), jax.ShapeDtypeStruct((H, S // tm, VT_ROWS, tm), CDT),
            jax.ShapeDtypeStruct((S, MLA_RANK), CDT), jax.ShapeDtypeStruct((S, MLA_RANK), CDT),
        ],
        compiler_params=_cparams(1),
    )(proj, proj, proj, proj, cf, sf, g_cq, g_ckv, wuq_p, wukv)


def _mla_attn_fwd(q_cat, k_cat, vt, xchg=None):
    H, S, _ = q_cat.shape
    tq = _tile(S, MLA_TQ_FWD)
    _, nkb, _, tk = vt.shape
    nq = S // tq
    nx = xchg.n if xchg is not None else 0

    def body(*refs):
        q_ref, k_ref, vt_ref = refs[:3]
        x_in = refs[3:3 + nx]
        y_ref, ot_ref, lse_ref = refs[3 + nx:6 + nx]
        x_out = refs[6 + nx:6 + 2 * nx]
        m_scr, acc_scr = refs[6 + 2 * nx:8 + 2 * nx]
        x_sems = refs[8 + 2 * nx:]
        h, i = pl.program_id(0), pl.program_id(1)
        if nx:
            @pl.when((h == 0) & (i == 0))
            def _():
                xchg.start(x_in, x_out, x_sems)

        nsub = MLA_FWD_SLABS if tq % (MLA_FWD_SLABS * LANE) == 0 else 1
        ws = tq // nsub
        subs = [pl.ds(s * ws, ws) for s in range(nsub)]
        qs = [q_ref[0, sb, :] for sb in subs]
        m_scr[...] = jnp.full_like(m_scr, -jnp.inf)
        acc_scr[...] = jnp.zeros_like(acc_scr)

        def step(j, carry):
            kj = k_ref[0, pl.ds(pl.multiple_of(j * tk, tk), tk), :]
            vtj = vt_ref[0, j]
            sts = [_nt(kj, qq) for qq in qs]
            m_old = [m_scr[:, sb] for sb in subs]
            m_new = _m(lambda mo, st: jnp.maximum(mo, jnp.max(st, axis=0, keepdims=True)), m_old, sts)
            pts = _m(lambda st, mn: jnp.exp2(st - mn), sts, m_new)
            pvs = _m(lambda pt: _nn(vtj, pt), pts)
            for sb, mo, mn, pv in zip(subs, m_old, m_new, pvs):
                acc_scr[:, sb] = jnp.exp2(mo - mn) * acc_scr[:, sb] + pv
                m_scr[:, sb] = mn
            return carry

        lax.fori_loop(0, nkb, step, 0, unroll=4 if nkb % 4 == 0 else 1)
        l = acc_scr[LANE:LANE + 1, :]
        ot = acc_scr[0:LANE, :] / l
        ot_ref[0] = ot
        y_ref[...] = ot.T.astype(CDT)
        lse_ref[0, 0] = m_scr[...] + jnp.log2(l)

        if nx:
            @pl.when((h == H - 1) & (i == nq - 1))
            def _():
                xchg.wait(x_in, x_out, x_sems)

    return pl.pallas_call(
        body,
        name="mla_attn_fwd",
        grid=(H, nq),
        in_specs=[
            pl.BlockSpec((1, tq, 2 * LANE), lambda h, i: (h, i, 0)),
            pl.BlockSpec((1, S, 2 * LANE), lambda h, i: (h, 0, 0)),
            pl.BlockSpec((1, nkb, VT_ROWS, tk), lambda h, i: (h, 0, 0, 0)),
        ] + (xchg.specs if nx else []),
        out_specs=[
            pl.BlockSpec((tq, LANE), lambda h, i: (i, h)),
            pl.BlockSpec((1, LANE, tq), lambda h, i: (h, 0, i)),
            pl.BlockSpec((1, 1, 1, tq), lambda h, i: (h, i, 0, 0)),
        ] + (xchg.specs if nx else []),
        out_shape=[
            jax.ShapeDtypeStruct((S, H * LANE), CDT),
            jax.ShapeDtypeStruct((H, LANE, S), F32),
            jax.ShapeDtypeStruct((H, nq, 1, tq), F32),
        ] + (xchg.out_shape if nx else []),
        scratch_shapes=[pltpu.VMEM((1, tq), F32), pltpu.VMEM((VT_ROWS, tq), F32)] + (xchg.scratch if nx else []),
        compiler_params=_cparams(2, side_effects=nx > 0),
    )(q_cat, k_cat, vt, *(xchg.arrs if nx else []))


def _mla_delta(dy, ot):
    H, _, S = ot.shape
    tq = _tile(S, MLA_TQ)
    nq = S // tq

    def body(dy_ref, ot_ref, d_ref):
        d_ref[0, 0] = jnp.sum(dy_ref[...].astype(F32).T * ot_ref[0], axis=0, keepdims=True)

    return pl.pallas_call(
        body,
        name="mla_delta",
        grid=(H, nq),
        in_specs=[pl.BlockSpec((tq, LANE), lambda h, i: (i, h)), pl.BlockSpec((1, LANE, tq), lambda h, i: (h, 0, i))],
        out_specs=pl.BlockSpec((1, 1, 1, tq), lambda h, i: (h, i, 0, 0)),
        out_shape=jax.ShapeDtypeStruct((H, nq, 1, tq), F32),
        compiler_params=_cparams(2),
    )(dy, ot)


def _mla_attn_bwd(q_cat, k_cat, v, dy, lse, delta, xchg=None):
    H, S, _ = q_cat.shape
    _, nq, _, tq = lse.shape
    tk = _tile(S, 1024)
    nkb = S // tk
    nx = xchg.n if xchg is not None else 0

    def body(*refs):
        k_ref, v_ref, q_ref, do_ref, lse_ref, dl_ref = refs[:6]
        x_in = refs[6:6 + nx]
        dk_ref, dv_ref, dq_ref = refs[6 + nx:9 + nx]
        x_out = refs[9 + nx:9 + 2 * nx]
        dk_scr, dv_scr = refs[9 + 2 * nx:11 + 2 * nx]
        x_sems = refs[11 + 2 * nx:]
        hd, ki = pl.program_id(0), pl.program_id(1)
        if nx:
            @pl.when((hd == 0) & (ki == 0))
            def _():
                xchg.start(x_in, x_out, x_sems)

        @pl.when(ki == 0)
        def _():
            dq_ref[...] = jnp.zeros_like(dq_ref)

        kb, vb = k_ref[0], v_ref[0]
        dk_scr[...] = jnp.zeros_like(dk_scr)
        dv_scr[...] = jnp.zeros_like(dv_scr)

        nsub = MLA_BWD_SLABS if tq % (MLA_BWD_SLABS * LANE) == 0 else 1
        ws = tq // nsub

        def step(i, carry):
            rows = [pl.ds(pl.multiple_of(i * tq + s * ws, ws), ws) for s in range(nsub)]
            lanes = [slice(s * ws, (s + 1) * ws) for s in range(nsub)]
            qc = [q_ref[0, r, :] for r in rows]
            doc = [do_ref[r, :] for r in rows]
            lse_i, dl_i = lse_ref[0, i], dl_ref[0, i]
            st = _m(lambda q_: _nt(kb, q_), qc)
            dp = _m(lambda d_: _nt(vb, d_), doc)
            pt = _m(lambda s_, ln: jnp.exp2(s_ - lse_i[:, ln]), st, lanes)
            dst = _m(lambda p_, d_, ln: (p_ * (d_ - dl_i[:, ln])).astype(CDT), pt, dp, lanes)
            dv_scr[...] += sum(_m(_nn, pt, doc))
            dk_scr[...] += sum(_m(_nn, dst, qc))
            dqs = _m(lambda d_: _tn(d_, kb), dst)
            for r, dq_ in zip(rows, dqs):
                dq_ref[0, r, :] += dq_
            return carry

        lax.fori_loop(0, nq, step, 0, unroll=2 if nq % 2 == 0 else 1)
        dk_ref[0] = dk_scr[...] * (MLA_SCALE / MLA_QSCALE)
        dv_ref[0] = dv_scr[...]

        if nx:
            @pl.when((hd == H - 1) & (ki == nkb - 1))
            def _():
                xchg.wait(x_in, x_out, x_sems)

    return pl.pallas_call(
        body,
        name="mla_attn_bwd",
        grid=(H, nkb),
        in_specs=[
            pl.BlockSpec((1, tk, 2 * LANE), lambda h, j: (h, j, 0)),
            pl.BlockSpec((1, tk, LANE), lambda h, j: (h, j, 0)),
            pl.BlockSpec((1, S, 2 * LANE), lambda h, j: (h, 0, 0)),
            pl.BlockSpec((S, LANE), lambda h, j: (0, h)),
            pl.BlockSpec((1, nq, 1, tq), lambda h, j: (h, 0, 0, 0)),
            pl.BlockSpec((1, nq, 1, tq), lambda h, j: (h, 0, 0, 0)),
        ] + (xchg.specs if nx else []),
        out_specs=[
            pl.BlockSpec((1, tk, 2 * LANE), lambda h, j: (h, j, 0)),
            pl.BlockSpec((1, tk, LANE), lambda h, j: (h, j, 0)),
            pl.BlockSpec((1, S, 2 * LANE), lambda h, j: (h, 0, 0)),
        ] + (xchg.specs if nx else []),
        out_shape=[
            jax.ShapeDtypeStruct((H, S, 2 * LANE), F32),
            jax.ShapeDtypeStruct((H, S, LANE), F32),
            jax.ShapeDtypeStruct((H, S, 2 * LANE), F32),
        ] + (xchg.out_shape if nx else []),
        scratch_shapes=[pltpu.VMEM((tk, 2 * LANE), F32), pltpu.VMEM((tk, LANE), F32)] + (xchg.scratch if nx else []),
        compiler_params=_cparams(2, side_effects=nx > 0),
    )(k_cat, v, q_cat, dy, lse, delta, *(xchg.arrs if nx else []))


def _mla_up_bwd(dq_cat, dk_cat, dv, proj, cf, sf, g_cq, g_ckv, wuq_p, wukv, dproj):
    H, S, _ = dq_cat.shape
    tm = _rtile(S, 256)
    PW = proj.shape[1]
    kr_off = PW - KR_PAD

    assert OFF_CKV == OFF_CQ + MLA_RANK and OFF_CQ % (2 * MLA_RANK) == 0

    def body(dq_ref, dk_ref, dv_ref, cq_ref, ckv_ref, cf_ref, sf_ref, gq_ref, gkv_ref, wq_ref, wkv_ref, _dp,
             dqp_ref, dkvp_ref, dc_ref, dkr_ref, dgq_ref, dgkv_ref):
        i = pl.program_id(0)
        dcq_ref, dckv_ref = dc_ref.at[:, 0:MLA_RANK], dc_ref.at[:, MLA_RANK:2 * MLA_RANK]
        cfv, sfv = cf_ref[...], sf_ref[...]
        aq = jnp.zeros((tm, MLA_RANK), F32)
        akv = jnp.zeros((tm, MLA_RANK), F32)
        akr = jnp.zeros((tm, LANE), F32)
        for h in range(H):
            dq = dq_ref[h] * MLA_SCALE
            dqr = dq[:, LANE:2 * LANE]
            dqp = jnp.concatenate([dq[:, 0:LANE], dqr * cfv, dqr * sfv], axis=1).astype(CDT)
            dqp_ref[:, pl.ds(h * 3 * LANE, 3 * LANE)] = dqp
            aq = aq + _nt(dqp, wq_ref[h])
            dk = dk_ref[h]
            dkvp = jnp.concatenate([dk[:, 0:LANE], dv_ref[h]], axis=1).astype(CDT)
            dkvp_ref[:, pl.ds(h * 2 * LANE, 2 * LANE)] = dkvp
            akv = akv + _nt(dkvp, wkv_ref[h])
            akr = akr + dk[:, LANE:2 * LANE]

        def rms_bwd(c_ref, g_ref, dn, d_ref, dg_ref):
            xh, rs = _rms_stats(c_ref[...])
            dxh = dn * g_ref[...]
            d_ref[...] = (rs * (dxh - xh * jnp.mean(dxh * xh, axis=-1, keepdims=True))).astype(d_ref.dtype)
            part = _rowsum8(dn * xh)

            @pl.when(i == 0)
            def _():
                dg_ref[...] = part

            @pl.when(i > 0)
            def _():
                dg_ref[...] += part

        rms_bwd(cq_ref, gq_ref, aq, dcq_ref, dgq_ref)
        rms_bwd(ckv_ref, gkv_ref, akv, dckv_ref, dgkv_ref)
        dkr_ref[...] = jnp.concatenate([akr * cfv, akr * sfv, jnp.zeros((tm, KR_PAD - 2 * LANE), F32)], axis=1).astype(dkr_ref.dtype)

    cspec = lambda off, w: pl.BlockSpec((tm, w), lambda i, o=off // w: (i, o))
    hspec = lambda w: pl.BlockSpec((H, tm, w), lambda i: (0, i, 0))
    outs = pl.pallas_call(
        body,
        name="mla_up_bwd",
        grid=(S // tm,),
        in_specs=[
            hspec(2 * LANE), hspec(2 * LANE), hspec(LANE),
            cspec(OFF_CQ, MLA_RANK), cspec(OFF_CKV, MLA_RANK),
            pl.BlockSpec((tm, LANE), lambda i: (i, 0)), pl.BlockSpec((tm, LANE), lambda i: (i, 0)),
            pl.BlockSpec((1, MLA_RANK), lambda i: (0, 0)), pl.BlockSpec((1, MLA_RANK), lambda i: (0, 0)),
            pl.BlockSpec((H, MLA_RANK, 3 * LANE), lambda i: (0, 0, 0)),
            pl.BlockSpec((H, MLA_RANK, 2 * LANE), lambda i: (0, 0, 0)),
            pl.BlockSpec(memory_space=pl.ANY),
        ],
        out_specs=[
            pl.BlockSpec((tm, H * 3 * LANE), lambda i: (i, 0)), pl.BlockSpec((tm, H * 2 * LANE), lambda i: (i, 0)),
            pl.BlockSpec((tm, 2 * MLA_RANK), lambda i: (i, OFF_CQ // (2 * MLA_RANK))),
            pl.BlockSpec((tm, KR_PAD), lambda i: (i, 0)),
            pl.BlockSpec((SUB, MLA_RANK), lambda i: (0, 0)),
            pl.BlockSpec((SUB, MLA_RANK), lambda i: (0, 0)),
        ],
        out_shape=[
            jax.ShapeDtypeStruct((S, H * 3 * LANE), CDT), jax.ShapeDtypeStruct((S, H * 2 * LANE), CDT),
            jax.ShapeDtypeStruct(dproj.shape, dproj.dtype),
            jax.ShapeDtypeStruct((S, KR_PAD), CDT),
            jax.ShapeDtypeStruct((SUB, MLA_RANK), F32), jax.ShapeDtypeStruct((SUB, MLA_RANK), F32),
        ],
        input_output_aliases={11: 2},
        compiler_params=_cparams(1),
    )(dq_cat, dk_cat, dv, proj, proj, cf, sf, g_cq, g_ckv, wuq_p, wukv, dproj)
    dqp, dkvp, dproj, dkr, dgq, dgkv = outs
    dproj = _copy_into("dproj_kr", dkr, dproj, kr_off)
    return dproj, dqp, dkvp, dgq, dgkv


def _mem_softmax(q, k):
    s = _nt(q, k) * (MEM_HD ** -0.5)
    p = jnp.exp(s - jnp.max(s, axis=1, keepdims=True))
    return p / jnp.sum(p, axis=1, keepdims=True)


def _mem_attn_fwd(proj, memkv):
    S = proj.shape[0]
    Mm = memkv.shape[0]
    tm = _rtile(S, 1024)

    def body(q_ref, k_ref, v_ref, y_ref):
        pn = _mem_softmax(q_ref[...], k_ref[...])
        y_ref[...] = _nn(pn, v_ref[...]).astype(y_ref.dtype)

    return pl.pallas_call(
        body,
        name="mem_attn_fwd",
        grid=(S // tm, MEM_HEADS),
        in_specs=[
            pl.BlockSpec((tm, MEM_HD), lambda i, h: (i, OFF_QM // MEM_HD + h)),
            pl.BlockSpec((Mm, MEM_HD), lambda i, h: (0, h)),
            pl.BlockSpec((Mm, MEM_HD), lambda i, h: (0, MEM_HEADS + h)),
        ],
        out_specs=pl.BlockSpec((tm, MEM_HD), lambda i, h: (i, h)),
        out_shape=jax.ShapeDtypeStruct((S, MEM_W), CDT),
        compiler_params=_cparams(2),
    )(proj, memkv, memkv)


def _mem_attn_bwd(dy, proj, memkv, dproj):
    S = proj.shape[0]
    Mm = memkv.shape[0]
    tm = _rtile(S, 1024)
    scale = MEM_HD ** -0.5

    def body(dy_ref, q_ref, k_ref, v_ref, _dp, dq_ref, dk_ref, dv_ref):
        i = pl.program_id(1)
        q, k, dyv = q_ref[...].astype(CDT), k_ref[...], dy_ref[...]
        pn = _mem_softmax(q, k)
        dvp = _tn(pn, dyv)
        dp = _nt(dyv, v_ref[...])
        ds = pn * (dp - jnp.sum(dp * pn, axis=1, keepdims=True)) * scale
        dq_ref[...] = _nn(ds, k).astype(dq_ref.dtype)
        dkp = _tn(ds, q)

        @pl.when(i == 0)
        def _():
            dk_ref[...] = dkp
            dv_ref[...] = dvp

        @pl.when(i > 0)
        def _():
            dk_ref[...] += dkp
            dv_ref[...] += dvp

    dproj, dk, dv = pl.pallas_call(
        body,
        name="mem_attn_bwd",
        grid=(MEM_HEADS, S // tm),
        in_specs=[
            pl.BlockSpec((tm, MEM_HD), lambda h, i: (i, h)),
            pl.BlockSpec((tm, MEM_HD), lambda h, i: (i, OFF_QM // MEM_HD + h)),
            pl.BlockSpec((Mm, MEM_HD), lambda h, i: (0, h)),
            pl.BlockSpec((Mm, MEM_HD), lambda h, i: (0, MEM_HEADS + h)),
            pl.BlockSpec(memory_space=pl.ANY),
        ],
        out_specs=[
            pl.BlockSpec((tm, MEM_HD), lambda h, i: (i, OFF_QM // MEM_HD + h)),
            pl.BlockSpec((Mm, MEM_HD), lambda h, i: (0, h)),
            pl.BlockSpec((Mm, MEM_HD), lambda h, i: (0, h)),
        ],
        out_shape=[
            jax.ShapeDtypeStruct(dproj.shape, dproj.dtype),
            jax.ShapeDtypeStruct((Mm, MEM_W), F32),
            jax.ShapeDtypeStruct((Mm, MEM_W), F32),
        ],
        input_output_aliases={4: 0},
        compiler_params=_cparams(2),
    )(dy, proj, memkv, memkv, dproj)
    return dproj, dk, dv


def _small_allreduce(vec):
    NS = vec.shape[1]

    def body(v_ref, o_ref, gbuf, mine, send, recv):
        x, y, c, me = _my_place()
        mine[...] = jnp.sum(v_ref[...], axis=0, keepdims=True)
        gbuf[me] = mine[...]
        copies = []
        for kk in range(1, N_DEV):
            peer, _ = _peer(x, y, c, kk)
            cp = pltpu.make_async_remote_copy(src_ref=mine, dst_ref=gbuf.at[me], send_sem=send.at[kk - 1],
                                              recv_sem=recv.at[kk - 1], device_id=peer, device_id_type=MESH)
            cp.start()
            copies.append(cp)
        for cp in copies:
            cp.wait()
        tot = gbuf[0]
        for d in range(1, N_DEV):
            tot = tot + gbuf[d]
        o_ref[...] = tot

    return pl.pallas_call(
        body,
        name="small_allreduce",
        in_specs=[pl.BlockSpec(memory_space=pltpu.VMEM)],
        out_specs=pl.BlockSpec(memory_space=pltpu.VMEM),
        out_shape=jax.ShapeDtypeStruct((1, NS), F32),
        scratch_shapes=[pltpu.VMEM((N_DEV, 1, NS), F32), pltpu.VMEM((1, NS), F32), pltpu.SemaphoreType.DMA((N_DEV - 1,)),
                        pltpu.SemaphoreType.DMA((N_DEV - 1,))],
        compiler_params=pltpu.CompilerParams(has_side_effects=True, vmem_limit_bytes=V7X_VMEM_LIMIT),
    )(vec)


def _adamw_math(g, w, m, v):
    nm = ADAM_B1 * m + (1.0 - ADAM_B1) * g
    nv = ADAM_B2 * v + (1.0 - ADAM_B2) * (g * g)
    mh = nm / (1.0 - ADAM_B1 ** ADAM_STEP)
    vh = nv / (1.0 - ADAM_B2 ** ADAM_STEP)
    delta = -ADAM_LR * (mh / (jnp.sqrt(vh) + ADAM_EPS) + ADAM_WD * w)
    return delta, nm, nv


def _adam_big(name, recv, w, m, v):
    _, R, C = w.shape
    tr = _rtile(R, max(SUB, (ADAM_BLOCK_ELEMS // C) // SUB * SUB))

    def body(r_ref, w_ref, m_ref, v_ref, g_ref, d_ref, nm_ref, nv_ref):
        g = r_ref[0].astype(F32)
        for d in range(1, N_DEV):
            g = g + r_ref[d].astype(F32)
        delta, nm, nv = _adamw_math(g, w_ref[0], m_ref[0], v_ref[0])
        g_ref[0] = g
        d_ref[0] = delta
        nm_ref[0] = nm
        nv_ref[0] = nv

    blk = pl.BlockSpec((1, tr, C), lambda i: (0, i, 0))
    return pl.pallas_call(
        body,
        name=name,
        grid=(R // tr,),
        in_specs=[pl.BlockSpec((N_DEV, tr, C), lambda i: (0, i, 0)), blk, blk, blk],
        out_specs=[blk, blk, blk, blk],
        out_shape=[jax.ShapeDtypeStruct((1, R, C), F32)] * 4,
        compiler_params=_cparams(1),
    )(recv, w, m, v)


def _to_bf16(name, w):
    _, R, C = w.shape
    tr = _rtile(R, max(SUB, (ADAM_BLOCK_ELEMS // C) // SUB * SUB))

    def body(w_ref, o_ref):
        o_ref[...] = w_ref[0].astype(CDT)

    return pl.pallas_call(
        body,
        name=name,
        grid=(R // tr,),
        in_specs=[pl.BlockSpec((1, tr, C), lambda i: (0, i, 0))],
        out_specs=pl.BlockSpec((tr, C), lambda i: (i, 0)),
        out_shape=jax.ShapeDtypeStruct((R, C), CDT),
        compiler_params=_cparams(1),
    )(w)


def _adam_small(g, w, m, v):
    def body(g_ref, w_ref, m_ref, v_ref, d_ref, nm_ref, nv_ref):
        delta, nm, nv = _adamw_math(g_ref[...], w_ref[...], m_ref[...], v_ref[...])
        d_ref[...] = delta
        nm_ref[...] = nm
        nv_ref[...] = nv

    return pl.pallas_call(body, name="adam_small", out_shape=[jax.ShapeDtypeStruct(g.shape, F32)] * 3)(g, w, m, v)


def _rot(w, axis=-1):
    x1, x2 = jnp.split(w, 2, axis=axis)
    return jnp.concatenate([-x2, x1], axis=axis)


def _unrot(dw, axis=-1):
    d1, d2 = jnp.split(dw, 2, axis=axis)
    return jnp.concatenate([d2, -d1], axis=axis)


def _pad_cols(w, width):
    return jnp.pad(w, [(0, 0)] * (w.ndim - 1) + [(0, width - w.shape[-1])])


def kernel(x, mem, positions, ln_emb_g, ln_emb_b, hgrn_lb_logits, w_in, hgrn_norm_g, mla_g_cq, mla_g_ckv, mla_w_uq, mla_w_ukv, mem_w_kv, w_branch, w_o, ln1_g, ln1_b, w_ffn_gate, w_ffn_up, w_ffn_down, ln2_g, ln2_b, loss_target, m_ln_emb_g, m_ln_emb_b, m_hgrn_lb_logits, m_w_in, m_hgrn_norm_g, m_mla_g_cq, m_mla_g_ckv, m_mla_w_uq, m_mla_w_ukv, m_mem_w_kv, m_w_branch, m_w_o, m_ln1_g, m_ln1_b, m_w_ffn_gate, m_w_ffn_up, m_w_ffn_down, m_ln2_g, m_ln2_b, v_ln_emb_g, v_ln_emb_b, v_hgrn_lb_logits, v_w_in, v_hgrn_norm_g, v_mla_g_cq, v_mla_g_ckv, v_mla_w_uq, v_mla_w_ukv, v_mem_w_kv, v_w_branch, v_w_o, v_ln1_g, v_ln1_b, v_w_ffn_gate, v_w_ffn_up, v_w_ffn_down, v_ln2_g, v_ln2_b):
    x2, tgt = x[0], loss_target[0]
    S, D = x2.shape
    Mm = mem.shape[1]
    F = w_ffn_gate.shape[2] * N_DEV
    GW = 3 * D
    PW = OFF_GATE + GW + KR_PAD
    KR = OFF_GATE + GW
    NIN = w_in.shape[2] * N_DEV
    assert NIN == OFF_GATE + MLA_ROPE + GW
    _, _, _, me = _my_place()
    row = lambda a: a.reshape(1, -1)

    br3 = lambda a: a.reshape(1, 3 * BR_W, -1)
    tp = lambda a: jnp.swapaxes(a, 1, 2)
    big_w = [tp(w_in), mla_w_uq, mla_w_ukv, mem_w_kv, br3(w_branch), w_o, tp(w_ffn_gate), tp(w_ffn_up), w_ffn_down]
    big_m = [tp(m_w_in), m_mla_w_uq, m_mla_w_ukv, m_mem_w_kv, br3(m_w_branch), m_w_o, tp(m_w_ffn_gate), tp(m_w_ffn_up),
             m_w_ffn_down]
    big_v = [tp(v_w_in), v_mla_w_uq, v_mla_w_ukv, v_mem_w_kv, br3(v_w_branch), v_w_o, tp(v_w_ffn_gate), tp(v_w_ffn_up),
             v_w_ffn_down]
    transposed = (0, 6, 7)
    wnames = ["w_in", "w_uq", "w_ukv", "mem_w_kv", "w_branch", "w_o", "w_gate", "w_up", "w_down"]
    big_wb = [_to_bf16("bf16_" + nme, w) for nme, w in zip(wnames, big_w)]
    g_in, g_lb = _all_gather_two_level("weights_all_gather", [big_wb[0], hgrn_lb_logits.reshape(4, -1)])
    win_t = g_in.reshape(NIN, D)
    kr_w = win_t[OFF_QM:OFF_QM + MLA_ROPE]
    zeros64 = jnp.zeros_like(kr_w)
    win_pt = jnp.concatenate([win_t[:OFF_FB], win_t[OFF_FF:OFF_G], win_t[OFF_FB:OFF_FF], win_t[OFF_G:OFF_QM],
                              win_t[OFF_QM + MLA_ROPE:], kr_w, zeros64, _rot(kr_w, 0), zeros64,
                              jnp.zeros((KR_PAD - 2 * LANE, D), CDT)], axis=0)
    lbl4 = jnp.transpose(g_lb, (1, 0, 2)).reshape(4, -1)

    half = MLA_ROPE // 2
    inv_freq = jnp.power(ROPE_THETA, -jnp.arange(half, dtype=F32) / half)
    ang = positions[0].astype(F32)[:, None] * inv_freq
    cf = _pad_cols(jnp.tile(jnp.cos(ang), (1, 2)), LANE)
    sf = _pad_cols(jnp.tile(jnp.sin(ang), (1, 2)), LANE)

    tm512 = _rtile(S, 512)
    ident = lambda accs, tiles, rows: ([accs[0]], [])

    def epi_ln0(accs, tiles, rows):
        h = _ln_stats(tiles[0])[0] * rows[0] + rows[1]
        return [h, h], []

    h0, h0b = _fused_mm("ln_emb_fwd", "nn", [], S, D, 1, tm512, D, 1, [(D, F32, 0, None), (D, CDT, 0, None)], epi_ln0,
                        tiles=[(x2, 0)], rows=[(row(ln_emb_g), 0), (row(ln_emb_b), 0)])
    proj, g_uq, g_ukv, g_mkv, g_wb, g_wo = _fused_mm(
        "proj", "nt", [[(h0b, 0, win_pt, 0)]], S, PW, D, _rtile(S, 1024), _tile(PW, 1536), D, [(PW, F32, 0, None)], ident,
        xchg=_Xchg(big_wb[1:6], False), msplit=2 if S % 2048 == 0 else 1)
    wuq_p =jnp.concatenate([g_uq[..., :MLA_NOPE], _pad_cols(g_uq[..., MLA_NOPE:], LANE),
                             _pad_cols(_rot(g_uq[..., MLA_NOPE:]), LANE)], axis=-1)
    wukv = g_ukv
    wmkv = g_mkv.reshape(-1, g_mkv.shape[-1])
    wb = jnp.transpose(g_wb.reshape(N_DEV, 3, BR_W, -1), (1, 2, 0, 3)).reshape(3, BR_W, D)
    wo = g_wo.reshape(-1, D)
    o_f, st_f = _gla_fwd(proj, lbl4, OFF_FF, False, "gla_fwd_f")
    o_b, st_b = _gla_fwd(proj, lbl4, OFF_FB, True, "gla_fwd_b")
    y_hg = _hgrn_post_fwd(o_f, o_b, proj, hgrn_norm_g)
    q_cat, k_cat, v_mla, vt_mla, cqn, ckvn = _mla_up(proj, cf, sf, mla_g_cq, mla_g_ckv, wuq_p, wukv)
    y_mla, ot, lse, g_wg, g_wu = _mla_attn_fwd(q_cat, k_cat, vt_mla, _Xchg(big_wb[6:8], False))
    wg_t, wu_t = g_wg.reshape(F, D), g_wu.reshape(F, D)
    memb = mem[0].astype(CDT)
    (memkv,) = _fused_mm("mem_kv", "nn", [[(memb, 0, wmkv, 0)]], Mm, 2 * MEM_W, D, Mm, _tile(2 * MEM_W, 512), D,
                         [(2 * MEM_W, CDT, 0, None)], ident)
    y_mem = _mem_attn_fwd(proj, memkv)
    ys = [y_hg, y_mla, y_mem]
    tnD = _tile(D, 1024, OFF_GATE)

    def epi_branch(accs, tiles, rows):
        return [_sigmoid(tiles[0]) * accs[0] + _sigmoid(tiles[1]) * accs[1] + _sigmoid(tiles[2]) * accs[2]], []

    (merged,) = _fused_mm("branch_fwd", "nn", [[(ys[b], 0, wb[b], 0)] for b in range(3)], S, D, BR_W, tm512, tnD, BR_W,
                          [(D, CDT, 0, None)], epi_branch, tiles=[(proj, OFF_GATE + b * D) for b in range(3)])

    def epi_ln1(accs, tiles, rows):
        r1v = ALPHA * tiles[0] + accs[0]
        return [r1v, _ln_stats(r1v)[0] * rows[0] + rows[1]], []

    r1, h1b = _fused_mm("wo_ln1", "nn", [[(merged, 0, wo, 0)]], S, D, D, tm512, D, D,
                        [(D, F32, 0, None), (D, CDT, 0, None)], epi_ln1, tiles=[(h0, 0)], rows=[(ln1_g, 0), (ln1_b, 0)])
    tnF = _tile(F, 512)

    def epi_up(accs, tiles, rows):
        gp, up = accs
        return [gp, up, gp * _sigmoid(gp) * up], []

    tm1k, ms1k = _rtile(S, 1024), (2 if S % 2048 == 0 else 1)
    gpb, upb, act, g_wd = _fused_mm("ffn_up", "nt", [[(h1b, 0, wg_t, 0)], [(h1b, 0, wu_t, 0)]], S, F, D, tm1k, tnF, D,
                                    [(F, CDT, 0, None)] * 3, epi_up, msplit=ms1k, xchg=_Xchg(big_wb[8:9], False))
    wd = g_wd.reshape(-1, D)

    def epi_down(accs, tiles, rows):
        g1, b1, g2, b2 = rows
        h1 = _ln_stats(tiles[0])[0] * g1 + b1
        xh2, rstd2 = _ln_stats(ALPHA * h1 + accs[0])
        diff = xh2 * g2 + b2 - tiles[1]
        dh2 = diff * (1.0 / D)
        dr2v = _ln_bwd(dh2, xh2, rstd2, g2)
        return [dr2v, dr2v], [dh2 * xh2, dh2, diff * diff * (0.5 / D)]

    acc_first = lambda epi: (lambda accs, tiles, rows: epi([tiles[0]], tiles[1:], rows))
    tm256 = _rtile(S, 256)
    (ff,) = _fused_mm("ffn_down", "nn", [[(act, 0, wd, 0)]], S, D, F, tm1k, _tile(D, 512), F, [(D, F32, 0, None)], ident,
                      msplit=ms1k)
    dr2, dr2b, dg2, db2, lossp = _fused_mm(
        "ffn_ln2_loss", "nn", [], S, D, 1, tm256, D, 1, [(D, F32, 0, None), (D, CDT, 0, None)], acc_first(epi_down),
        tiles=[(ff, 0), (r1, 0), (tgt, 0)], rows=[(ln1_g, 0), (ln1_b, 0), (ln2_g, 0), (ln2_b, 0)], n_racc=3)

    def epi_dact(accs, tiles, rows):
        da, gp, up = accs[0], tiles[0].astype(F32), tiles[1].astype(F32)
        s = _sigmoid(gp)
        return [da * up * (s * (1.0 + gp * (1.0 - s))), da * (gp * s)], []

    dgp, dup = _fused_mm("ffn_dact", "nt", [[(dr2b, 0, wd, 0)]], S, F, D, tm1k, tnF, D, [(F, CDT, 0, None)] * 2,
                         epi_dact, tiles=[(gpb, 0), (upb, 0)], msplit=ms1k)
    tkS = _rtile(S, 2048)
    (d_wd,) = _fused_mm("dw_down", "tn", [[(act, 0, dr2b, 0)]], F, D, S, tnF, D, tkS, [(D, CDT, 0, None)], ident)
    d_wg_t, d_wu_t = _fused_mm("dw_gate_up", "tn", [[(dgp, 0, h1b, 0)], [(dup, 0, h1b, 0)]], F, D, S, tnF, _tile(D, 1024),
                               tkS, [(D, CDT, 0, None)] * 2, lambda accs, tiles, rows: (accs, []))

    def epi_dh1(accs, tiles, rows):
        dh1 = accs[0] + ALPHA * tiles[0]
        xh1, rstd1 = _ln_stats(tiles[1])
        dr1v = _ln_bwd(dh1, xh1, rstd1, rows[0])
        return [dr1v, dr1v], [dh1 * xh1, dh1]

    rows8 = lambda dw: dw.reshape(N_DEV, -1, dw.shape[-1])
    (dh1_acc,) = _fused_mm("dh1", "nn", [[(dgp, 0, wg_t, 0), (dup, 0, wu_t, 0)]], S, D, F, tm512, _tile(D, 512), F,
                           [(D, F32, 0, None)], ident)
    dr1, dr1b, dg1, db1 = _fused_mm(
        "dh1_ln1", "nn", [], S, D, 1, tm256, D, 1, [(D, F32, 0, None), (D, CDT, 0, None)], acc_first(epi_dh1),
        tiles=[(dh1_acc, 0), (dr2, 0), (r1, 0)], rows=[(ln1_g, 0)], n_racc=2)
    (dmerged,) = _fused_mm("dmerged", "nt", [[(dr1b, 0, wo, 0)]], S, D, D, tm512, D, D, [(D, CDT, 0, None)], ident)
    (d_wo,) = _fused_mm("dw_o", "tn", [[(merged, 0, dr1b, 0)]], D, D, S, _tile(D, 512), D, tkS, [(D, CDT, 0, None)], ident)

    def epi_dbranch(accs, tiles, rows):
        dm, s = tiles[0].astype(F32), _sigmoid(tiles[1])
        return [dm * s, dm * accs[0] * s * (1.0 - s)], []

    dproj = None
    d_wbs, dys = [], []
    for b in range(3):
        du, dproj = _fused_mm(f"branch_bwd{b}", "nn", [[(ys[b], 0, wb[b], 0)]], S, D, BR_W, tm1k, tnD, BR_W,
                              [(D, CDT, 0, None), (PW, CDT, OFF_GATE + b * D, dproj)], epi_dbranch,
                              tiles=[(dmerged, 0), (proj, OFF_GATE + b * D)], msplit=ms1k)
        (dwb,) = _fused_mm(f"dw_branch{b}", "tn", [[(ys[b], 0, du, 0)]], BR_W, D, S, _tile(BR_W, 512), D, tkS,
                           [(D, CDT, 0, None)], ident)
        (dyb,) = _fused_mm(f"dy_branch{b}", "nt", [[(du, 0, wb[b], 0)]], S, BR_W, D, tm512, BR_W, D,
                           [(BR_W, F32 if b == 0 else CDT, 0, None)], ident)
        d_wbs.append(dwb)
        dys.append(dyb)
    dy_hg, dy_mla, dy_mem = dys

    dproj, dk_mem, dv_mem = _mem_attn_bwd(dy_mem, proj, memkv, dproj)
    dkv_mem = jnp.concatenate([dk_mem, dv_mem], axis=1).astype(CDT)
    (d_wmkv,) = _fused_mm("dw_memkv", "tn", [[(memb, 0, dkv_mem, 0)]], D, 2 * MEM_W, Mm, _tile(D, 512), 2 * MEM_W, Mm,
                          [(2 * MEM_W, CDT, 0, None)], ident)

    delta = _mla_delta(dy_mla, ot)
    lse_b = lse.reshape(delta.shape)
    dk_cat, dv_h, dq_cat, r_wg, r_wu, r_wd = _mla_attn_bwd(
        q_cat, k_cat, v_mla, dy_mla, lse_b, delta, _Xchg([rows8(d_wg_t), rows8(d_wu_t), rows8(d_wd)], True))
    dproj, dqp, dkvp, dgq, dgkv = _mla_up_bwd(dq_cat, dk_cat, dv_h, proj, cf, sf, mla_g_cq, mla_g_ckv, wuq_p, wukv, dproj)
    heads_major = lambda dw: jnp.transpose(dw.reshape(MLA_RANK, MLA_HEADS, -1), (1, 0, 2))
    (d_wuq_all,) = _fused_mm("dw_uq", "tn", [[(cqn, 0, dqp, 0)]], MLA_RANK, dqp.shape[1], S, MLA_RANK,
                             _tile(dqp.shape[1], 1536), tkS, [(dqp.shape[1], F32, 0, None)], ident)
    (d_wukv_all,) = _fused_mm("dw_ukv", "tn", [[(ckvn, 0, dkvp, 0)]], MLA_RANK, dkvp.shape[1], S, MLA_RANK,
                              _tile(dkvp.shape[1], 1024), tkS, [(dkvp.shape[1], CDT, 0, None)], ident)
    d_wuq_p, d_wukv = heads_major(d_wuq_all), heads_major(d_wukv_all)
    d_wuq = jnp.concatenate([d_wuq_p[..., :MLA_NOPE],
                             d_wuq_p[..., LANE:LANE + MLA_ROPE] + _unrot(d_wuq_p[..., 2 * LANE:2 * LANE + MLA_ROPE])],
                            axis=-1).astype(CDT)

    do_hg, dproj, dng = _hgrn_post_bwd(dy_hg, o_f, o_b, proj, hgrn_norm_g, dproj)
    dproj, dq1, di1, dl_f = _gla_bwd(proj, lbl4, OFF_FF, False, do_hg, st_f, dproj, None, "gla_bwd_f")
    dproj, _, _, dl_b = _gla_bwd(proj, lbl4, OFF_FB, True, do_hg, st_b, dproj, (dq1, di1), "gla_bwd_b")

    def epi_dh0(accs, tiles, rows):
        dh0 = accs[0] + ALPHA * tiles[0]
        xh, rstd = _ln_stats(tiles[1])
        return [_ln_bwd(dh0, xh, rstd, rows[0])], [dh0 * xh, dh0]

    d_wb = jnp.transpose(jnp.stack(d_wbs).reshape(3, BR_W, N_DEV, -1), (2, 0, 1, 3)).reshape(N_DEV, 3 * BR_W, -1)
    d_win_pt, r_uq, r_ukv, r_mkv, r_wb, r_wo = _fused_mm(
        "dw_in", "tn", [[(dproj, 0, h0b, 0)]], PW, D, S, _tile(PW, 1536), _tile(D, 1024), tkS, [(D, CDT, 0, None)], ident,
        xchg=_Xchg([d_wuq, d_wukv, rows8(d_wmkv), d_wb, rows8(d_wo)], True))
    d_kr = (d_win_pt[KR:KR + MLA_ROPE].astype(F32) + _unrot(d_win_pt[KR + LANE:KR + LANE + MLA_ROPE].astype(F32), 0)).astype(CDT)
    d_win_t = jnp.concatenate([d_win_pt[:OFF_FB], d_win_pt[OFF_FF:OFF_G], d_win_pt[OFF_FB:OFF_FF],
                               d_win_pt[OFF_G:OFF_QM], d_kr, d_win_pt[OFF_QM:KR]], axis=0)
    grad_x, dge, dbe, r_in = _fused_mm(
        "dh0_ln_emb", "nn", [[(dproj, 0, win_pt, 0)]], S, D, PW, tm512, D, _tile(PW, 1536), [(D, F32, 0, None)], epi_dh0,
        tiles=[(dr1, 0), (x2, 0)], rows=[(row(ln_emb_g), 0)], n_racc=2, xchg=_Xchg([rows8(d_win_t)], True))

    recv = [r_in, r_uq, r_ukv, r_mkv, r_wb, r_wo, r_wg, r_wu, r_wd]
    names = ["w_in", "w_uq", "w_ukv", "mem_w_kv", "w_branch", "w_o", "w_gate", "w_up", "w_down"]
    big_out = [_adam_big("adam_" + nme, r, w, m_, v_) for nme, r, w, m_, v_ in zip(names, recv, big_w, big_m, big_v)]

    parts = [dge, dbe, dng, dgq, dgkv, dg1, db1, dg2, db2, dl_f, dl_b, lossp]
    widths = [p.shape[1] for p in parts]
    red = _small_allreduce(jnp.concatenate(parts, axis=1))[0]
    offs = [sum(widths[:i]) for i in range(len(widths))]
    rs = [red[o:o + w_] for o, w_ in zip(offs, widths)]
    g_le_g, g_le_b, g_ng, g_gq, g_gkv, g_l1g, g_l1b, g_l2g, g_l2b, g_dlf, g_dlb, g_loss = rs
    loss = jnp.sum(g_loss)
    g_ng = g_ng.reshape(HG_HEADS, HG_DK).sum(axis=0)
    dl0 = jnp.stack([g_dlf, g_dlb])
    g_lb_full = jnp.stack([dl0, -dl0], axis=1)
    lbw = hgrn_lb_logits.shape[2]
    g_lb = lax.dynamic_slice_in_dim(g_lb_full, me * lbw, lbw, axis=2)

    small_g = [g_le_g, g_le_b, g_lb, g_ng.reshape(1, -1), g_gq.reshape(1, -1), g_gkv.reshape(1, -1), g_l1g.reshape(1, -1),
               g_l1b.reshape(1, -1), g_l2g.reshape(1, -1), g_l2b.reshape(1, -1)]
    small_w = [ln_emb_g, ln_emb_b, hgrn_lb_logits, hgrn_norm_g, mla_g_cq, mla_g_ckv, ln1_g, ln1_b, ln2_g, ln2_b]
    small_m = [m_ln_emb_g, m_ln_emb_b, m_hgrn_lb_logits, m_hgrn_norm_g, m_mla_g_cq, m_mla_g_ckv, m_ln1_g, m_ln1_b, m_ln2_g, m_ln2_b]
    small_v = [v_ln_emb_g, v_ln_emb_b, v_hgrn_lb_logits, v_hgrn_norm_g, v_mla_g_cq, v_mla_g_ckv, v_ln1_g, v_ln1_b, v_ln2_g, v_ln2_b]
    small_g = [g.reshape(w.shape) for g, w in zip(small_g, small_w)]
    pack = lambda lst: jnp.concatenate([a.reshape(-1) for a in lst]).reshape(1, -1)
    s_delta, s_nm, s_nv = _adam_small(pack(small_g), pack(small_w), pack(small_m), pack(small_v))
    sizes = [w.size for w in small_w]
    soffs = [sum(sizes[:i]) for i in range(len(sizes))]
    unpack = lambda p: [p[0, o:o + n].reshape(w.shape) for o, n, w in zip(soffs, sizes, small_w)]
    s_delta, s_nm, s_nv = unpack(s_delta), unpack(s_nm), unpack(s_nv)

    def ordered(small, big):
        sm = list(small)
        big = [tp(b) if n in transposed else b for n, b in enumerate(big)]
        bg = [b.reshape(w.shape) for b, w in zip(big, [w_in, mla_w_uq, mla_w_ukv, mem_w_kv, w_branch, w_o, w_ffn_gate, w_ffn_up, w_ffn_down])]
        return [sm[0], sm[1], sm[2], bg[0], sm[3], sm[4], sm[5], bg[1], bg[2], bg[3], bg[4], bg[5], sm[6], sm[7], bg[6], bg[7], bg[8], sm[8], sm[9]]

    grads = ordered(small_g, [o[0] for o in big_out])
    deltas = ordered(s_delta, [o[1] for o in big_out])
    new_m = ordered(s_nm, [o[2] for o in big_out])
    new_v = ordered(s_nv, [o[3] for o in big_out])
    return (loss, grad_x[None], *grads, *deltas, *new_m, *new_v)
```
